```python
import math, functools
import jax, jax.numpy as jnp
from jax import lax
import numpy as np


D_MODEL = 1024
BATCH = 16
SEQ = 256
DEPTH = 4
DEC_BATCH = 2
DEC_SEQ = 1024
PAST_LEN = 512

GRID_W = 64
N_EVEN = (DEPTH + 1) // 2
N_ODD = DEPTH // 2
MIX_W = D_MODEL
HALF_W = MIX_W // 2
H_A = 4
DV_A = HALF_W // H_A
DK_A = DV_A // 2
GATE_RANK = 16
GLA_TAU = 16.0
GLA_CHUNK = 64
HD_B = 128
H_B = HALF_W // HD_B
KV_B = H_B // 2
HY_W = HALF_W
FILT_EMB = 33
FILT_BANDS = (FILT_EMB - 1) // 2
FILT_HID = 64
HY_MIN_DECAY = math.log(1e-2) / 1.5
HY_MAX_DECAY = math.log(1e-2) / 0.3
H_D = 4
V_D = HALF_W // H_D
NOPE_D = V_D
ROPE_D = NOPE_D // 2
Q_RANK = D_MODEL // 4
KV_RANK = D_MODEL // 8
FFN_H = -(-8 * D_MODEL // (3 * 256)) * 256
ROPE_THETA = 10000.0
Q_BLOCK = 128
EPS = 1e-6
EVEN_SIZES = [H_A * DK_A, H_A * DK_A, H_A * DV_A, H_A * DV_A, GATE_RANK, GATE_RANK, H_B * HD_B, KV_B * HD_B, KV_B * HD_B]
EVEN_SPLITS = [int(s) for s in np.cumsum(EVEN_SIZES)[:-1]]
IN_EVEN = sum(EVEN_SIZES)
ODD_SIZES = [3 * HY_W, Q_RANK, KV_RANK, ROPE_D]
ODD_SPLITS = [int(s) for s in np.cumsum(ODD_SIZES)[:-1]]
IN_ODD = sum(ODD_SIZES)

kernel_name = 'hybrid_diffusion_prefix_trunk_step'


def rms_norm(x, eps=EPS):
    xf = x.astype(jnp.float32)
    return (xf * lax.rsqrt(jnp.mean(xf * xf, axis=-1, keepdims=True) + eps)).astype(x.dtype)


def modulate(x, shift, scale):
    return rms_norm(x) * (1.0 + scale) + shift


def adaln(cvec, w_mod_l, b_mod_l):
    m = jax.nn.silu(cvec) @ w_mod_l + b_mod_l
    return jnp.split(m[:, None, :], 6, axis=-1)


def axial_rope_tables(n_tokens, dim):
    rows = n_tokens // GRID_W
    row = jnp.broadcast_to(jnp.arange(rows, dtype=jnp.float32)[:, None], (rows, GRID_W)).reshape(n_tokens)
    col = jnp.broadcast_to(jnp.arange(GRID_W, dtype=jnp.float32)[None, :], (rows, GRID_W)).reshape(n_tokens)
    n_freq = dim // 4
    inv = ROPE_THETA ** (-jnp.arange(n_freq, dtype=jnp.float32) / n_freq)
    ang = jnp.concatenate([row[:, None] * inv, col[:, None] * inv], axis=-1)
    return jnp.cos(ang), jnp.sin(ang)


def apply_rope(x, cos, sin):
    x1, x2 = jnp.split(x, 2, axis=-1)
    c = cos[None, :, None, :]
    s = sin[None, :, None, :]
    return jnp.concatenate([x1 * c - x2 * s, x1 * s + x2 * c], axis=-1).astype(x.dtype)


def block_attention(q, k, v, scale):
    B, Lq, H, dq = q.shape
    G = k.shape[2]
    rep = H // G
    dv = v.shape[-1]
    nb = Lq // Q_BLOCK
    qb = q.reshape(B, nb, Q_BLOCK, G, rep, dq).transpose(1, 0, 2, 3, 4, 5)

    def attend(qi):
        s = jnp.einsum('bqgrd,bkgd->bgrqk', qi, k, preferred_element_type=jnp.float32) * scale
        p = jax.nn.softmax(s, axis=-1).astype(v.dtype)
        return jnp.einsum('bgrqk,bkge->bqgre', p, v)

    o = lax.map(attend, qb)
    return o.transpose(1, 0, 2, 3, 4, 5).reshape(B, Lq, H, dv)


def gla_chunked(q, k, v, log_a, s0):
    B, L, H, dk = q.shape
    dv = v.shape[-1]
    n_chunks = L // GLA_CHUNK

    def chunks(t):
        return t.astype(jnp.float32).reshape(B, n_chunks, GLA_CHUNK, H, t.shape[-1]).transpose(1, 0, 3, 2, 4)

    xs = (chunks(q * dk ** -0.5), chunks(k), chunks(v), chunks(log_a))
    causal = jnp.tril(jnp.ones((GLA_CHUNK, GLA_CHUNK), dtype=bool))[:, :, None]

    def step(state, inp):
        qc, kc, vc, ac = inp
        b = jnp.cumsum(ac, axis=2)
        diff = b[:, :, :, None, :] - b[:, :, None, :, :]
        decay = jnp.exp(jnp.where(causal, diff, -jnp.inf))
        scores = jnp.einsum('bhid,bhjd,bhijd->bhij', qc, kc, decay)
        o = jnp.einsum('bhij,bhje->bhie', scores, vc) + jnp.einsum('bhid,bhde->bhie', qc * jnp.exp(b), state)
        b_last = b[:, :, -1:, :]
        new_state = state * jnp.exp(b_last[:, :, 0, :])[..., None] + jnp.einsum('bhjd,bhje->bhde', kc * jnp.exp(b_last - b), vc)
        return new_state, o

    s_fin, o = lax.scan(step, s0.astype(jnp.float32), xs)
    return o.transpose(1, 0, 3, 2, 4).reshape(B, L, H, dv), s_fin


def short_conv3(x, w, b):
    xp = jnp.pad(x, ((0, 0), (1, 1), (0, 0)))
    return xp[:, :-2] * w[0] + xp[:, 1:-1] * w[1] + xp[:, 2:] * w[2] + b


def implicit_filters(n_tokens, wf1, bf1, freq, wf2, bf2, wf3):
    t = jnp.linspace(0.0, 1.0, n_tokens, dtype=jnp.float32)[:, None]
    w = 2.0 * math.pi * jnp.arange(n_tokens, dtype=jnp.float32)[:, None] / n_tokens
    f = jnp.linspace(1e-4, FILT_BANDS - 1, FILT_BANDS, dtype=jnp.float32)[None, :]
    z = jnp.concatenate([t, jnp.cos(f * w), -jnp.sin(f * w)], axis=-1).astype(wf1.dtype)
    hid = jnp.sin(freq * (z @ wf1 + bf1))
    hid = jnp.sin(freq * (hid @ wf2 + bf2))
    filt = (hid @ wf3).astype(jnp.float32)
    deltas = jnp.abs(jnp.linspace(HY_MIN_DECAY, HY_MAX_DECAY, HY_W, dtype=jnp.float32))
    decay = jnp.exp(-t * deltas)
    return filt * jnp.concatenate([decay, decay], axis=-1)


def long_conv_bidir(u, filt):
    L = u.shape[1]
    n = 2 * L
    h_f, h_b = jnp.split(filt, 2, axis=-1)
    g = jnp.concatenate([h_f, jnp.zeros((1, HY_W), jnp.float32), jnp.flip(h_b[1:], axis=0)], axis=0)
    G = jnp.fft.rfft(g, axis=0)
    U = jnp.fft.rfft(u.astype(jnp.float32), n=n, axis=1)
    y = jnp.fft.irfft(U * G[None], n=n, axis=1)[:, :L]
    return y.astype(u.dtype)


def swiglu(h, w_in, w_out):
    g, u = jnp.split(h @ w_in, 2, axis=-1)
    return (jax.nn.silu(g) * u) @ w_out


def even_mixer(h, w_in, w_gf, b_gf, w_gb, b_gb, g_norm, g_q, g_k, w_out, ctx=None, rope=None):
    B, L, _ = h.shape
    q_a, k_a, v_a, r_a, z_f, z_b, q_g, k_g, v_g = jnp.split(h @ w_in, EVEN_SPLITS, axis=-1)
    q_a = q_a.reshape(B, L, H_A, DK_A)
    k_a = k_a.reshape(B, L, H_A, DK_A)
    v_a = v_a.reshape(B, L, H_A, DV_A)
    log_f = (jax.nn.log_sigmoid((z_f @ w_gf + b_gf).astype(jnp.float32)) / GLA_TAU).reshape(B, L, H_A, DK_A)
    log_b = (jax.nn.log_sigmoid((z_b @ w_gb + b_gb).astype(jnp.float32)) / GLA_TAU).reshape(B, L, H_A, DK_A)
    if ctx is None:
        s_f0 = jnp.zeros((B, H_A, DK_A, DV_A), jnp.float32)
        s_b0 = jnp.zeros((B, H_A, DK_A, DV_A), jnp.float32)
    else:
        s_f0, s_b0 = ctx[0], ctx[1]
    o_fw, s_f = gla_chunked(q_a, k_a, v_a, log_f, s_f0)
    o_bw, s_b = gla_chunked(jnp.flip(q_a, 1), jnp.flip(k_a, 1), jnp.flip(v_a, 1), jnp.flip(log_b, 1), s_b0)
    o_bw = jnp.flip(o_bw, 1)
    o_a = (rms_norm(o_fw + o_bw) * g_norm).astype(h.dtype) * jax.nn.silu(r_a).reshape(B, L, H_A, DV_A)
    o_a = o_a.reshape(B, L, HALF_W)
    q_g = rms_norm(q_g.reshape(B, L, H_B, HD_B)) * g_q
    k_g = rms_norm(k_g.reshape(B, L, KV_B, HD_B)) * g_k
    v_g = v_g.reshape(B, L, KV_B, HD_B)
    if ctx is None:
        keys, vals = k_g, v_g
        new_state = (s_f, s_b, k_g, v_g)
    else:
        cos, sin = rope
        q_g = apply_rope(q_g, cos, sin)
        keys = jnp.concatenate([ctx[2].astype(h.dtype), apply_rope(k_g, cos, sin)], axis=1)
        vals = jnp.concatenate([ctx[3].astype(h.dtype), v_g], axis=1)
        new_state = None
    o_g = block_attention(q_g, keys, vals, HD_B ** -0.5).reshape(B, L, HALF_W)
    return jnp.concatenate([o_a, o_g], axis=-1) @ w_out, new_state


def odd_mixer(h, w_in, w_conv, b_conv, skip, wf1, bf1, freq, wf2, bf2, wf3, g_q, w_qb, g_kv, w_kvb, w_out, ctx=None, rope=None):
    B, L, _ = h.shape
    u_hy, cq, ckv, kpe = jnp.split(h @ w_in, ODD_SPLITS, axis=-1)
    x0, x1, v = jnp.split(short_conv3(u_hy, w_conv, b_conv), 3, axis=-1)
    filt = implicit_filters(L, wf1, bf1, freq, wf2, bf2, wf3)
    gv = x1 * v
    y_hy = x0 * (long_conv_bidir(gv, filt) + gv * skip)
    q = ((rms_norm(cq) * g_q) @ w_qb).reshape(B, L, H_D, NOPE_D + ROPE_D)
    q_nope, q_pe = jnp.split(q, [NOPE_D], axis=-1)
    ckv = rms_norm(ckv) * g_kv
    if ctx is None:
        ckv_all, kpe_all = ckv, kpe
        new_state = (ckv, kpe)
    else:
        cos, sin = rope
        q_pe = apply_rope(q_pe, cos, sin)
        kpe_lat = apply_rope(kpe[:, :, None, :], cos, sin)[:, :, 0, :]
        ckv_all = jnp.concatenate([ctx[0].astype(h.dtype), ckv], axis=1)
        kpe_all = jnp.concatenate([ctx[1].astype(h.dtype), kpe_lat], axis=1)
        new_state = None
    Lk = ckv_all.shape[1]
    kv = (ckv_all @ w_kvb).reshape(B, Lk, H_D, NOPE_D + V_D)
    k_nope, vals = jnp.split(kv, [NOPE_D], axis=-1)
    keys = jnp.concatenate([k_nope, jnp.broadcast_to(kpe_all[:, :, None, :], (B, Lk, H_D, ROPE_D))], axis=-1)
    qf = jnp.concatenate([q_nope, q_pe], axis=-1)
    o_d = block_attention(qf, keys, vals, (NOPE_D + ROPE_D) ** -0.5).reshape(B, L, HALF_W)
    return jnp.concatenate([y_hy, o_d], axis=-1) @ w_out, new_state


def setup_inputs(seed: int = 0) -> dict:
    key = jax.random.key(seed)
    ks = iter(jax.random.split(key, 64))

    def nrm(shape, scale):
        return jax.random.normal(next(ks), shape, jnp.float32) * scale

    def gain(shape):
        return 1.0 + nrm(shape, 0.05)

    d = D_MODEL
    return {
        'x_prompt': nrm((BATCH, SEQ, d), 1.0),
        'x_sample': nrm((DEC_BATCH, DEC_SEQ, d), 1.0),
        'state_gla_fwd': nrm((DEC_BATCH, N_EVEN, H_A, DK_A, DV_A), 0.5),
        'state_gla_bwd': nrm((DEC_BATCH, N_EVEN, H_A, DK_A, DV_A), 0.5),
        'cache_gqa_k': nrm((DEC_BATCH, N_EVEN, PAST_LEN, KV_B, HD_B), 1.0),
        'cache_gqa_v': nrm((DEC_BATCH, N_EVEN, PAST_LEN, KV_B, HD_B), 1.0),
        'cache_mla_ckv': nrm((DEC_BATCH, N_ODD, PAST_LEN, KV_RANK), 1.0),
        'cache_mla_kpe': nrm((DEC_BATCH, N_ODD, PAST_LEN, ROPE_D), 1.0),
        'c': nrm((DEC_BATCH, d), 1.0),
        'c_ctx': nrm((d,), 1.0),
        'w_mod': nrm((DEPTH, d, 6 * d), 0.5 * d ** -0.5),
        'b_mod': nrm((DEPTH, 6 * d), 0.02),
        'w_in_even': nrm((N_EVEN, d, IN_EVEN), d ** -0.5),
        'w_gla_gate_f': nrm((N_EVEN, GATE_RANK, H_A * DK_A), GATE_RANK ** -0.5),
        'b_gla_gate_f': nrm((N_EVEN, H_A * DK_A), 0.1),
        'w_gla_gate_b': nrm((N_EVEN, GATE_RANK, H_A * DK_A), GATE_RANK ** -0.5),
        'b_gla_gate_b': nrm((N_EVEN, H_A * DK_A), 0.1),
        'g_gla_norm': gain((N_EVEN, DV_A)),
        'g_gqa_q': gain((N_EVEN, HD_B)),
        'g_gqa_k': gain((N_EVEN, HD_B)),
        'w_out_even': nrm((N_EVEN, MIX_W, d), MIX_W ** -0.5),
        'w_in_odd': nrm((N_ODD, d, IN_ODD), d ** -0.5),
        'w_hy_conv': nrm((N_ODD, 3, 3 * HY_W), 3 ** -0.5),
        'b_hy_conv': nrm((N_ODD, 3 * HY_W), 0.02),
        'hy_skip': nrm((N_ODD, HY_W), 0.5),
        'w_filt1': nrm((N_ODD, FILT_EMB, FILT_HID), FILT_EMB ** -0.5),
        'b_filt1': nrm((N_ODD, FILT_HID), 0.02),
        'filt_freq': gain((N_ODD, FILT_HID)),
        'w_filt2': nrm((N_ODD, FILT_HID, FILT_HID), FILT_HID ** -0.5),
        'b_filt2': nrm((N_ODD, FILT_HID), 0.02),
        'w_filt3': nrm((N_ODD, FILT_HID, 2 * HY_W), 0.1 * FILT_HID ** -0.5),
        'g_mla_q': gain((N_ODD, Q_RANK)),
        'w_mla_qb': nrm((N_ODD, Q_RANK, H_D * (NOPE_D + ROPE_D)), Q_RANK ** -0.5),
        'g_mla_kv': gain((N_ODD, KV_RANK)),
        'w_mla_kvb': nrm((N_ODD, KV_RANK, H_D * (NOPE_D + V_D)), KV_RANK ** -0.5),
        'w_out_odd': nrm((N_ODD, MIX_W, d), MIX_W ** -0.5),
        'w_ffn_in': nrm((DEPTH, d, 2 * FFN_H), d ** -0.5),
        'w_ffn_out': nrm((DEPTH, FFN_H, d), FFN_H ** -0.5),
        'g_final': gain((d,)),
    }


def reference(x_prompt, x_sample, state_gla_fwd, state_gla_bwd, cache_gqa_k, cache_gqa_v, cache_mla_ckv, cache_mla_kpe,
              c, c_ctx, w_mod, b_mod, w_in_even, w_gla_gate_f, b_gla_gate_f, w_gla_gate_b, b_gla_gate_b, g_gla_norm,
              g_gqa_q, g_gqa_k, w_out_even, w_in_odd, w_hy_conv, b_hy_conv, hy_skip, w_filt1, b_filt1, filt_freq,
              w_filt2, b_filt2, w_filt3, g_mla_q, w_mla_qb, g_mla_kv, w_mla_kvb, w_out_odd, w_ffn_in, w_ffn_out, g_final):
    L_s = x_sample.shape[1]
    rope_b = axial_rope_tables(L_s, HD_B)
    rope_d = axial_rope_tables(L_s, ROPE_D)
    c_pre = c_ctx[None, :]
    xc, xs = x_prompt, x_sample
    st_gf, st_gb, st_k, st_v, st_ckv, st_kpe = [], [], [], [], [], []
    for i in range(DEPTH):
        mc = adaln(c_pre, w_mod[i], b_mod[i])
        ms = adaln(c, w_mod[i], b_mod[i])
        hc = modulate(xc, mc[0], mc[1])
        hs = modulate(xs, ms[0], ms[1])
        if i % 2 == 0:
            j = i // 2
            mixer = functools.partial(even_mixer, w_in=w_in_even[j], w_gf=w_gla_gate_f[j], b_gf=b_gla_gate_f[j],
                                      w_gb=w_gla_gate_b[j], b_gb=b_gla_gate_b[j], g_norm=g_gla_norm[j],
                                      g_q=g_gqa_q[j], g_k=g_gqa_k[j], w_out=w_out_even[j])
            oc, st = mixer(hc)
            os_, _ = mixer(hs, ctx=(state_gla_fwd[:, j], state_gla_bwd[:, j], cache_gqa_k[:, j], cache_gqa_v[:, j]), rope=rope_b)
            st_gf.append(st[0])
            st_gb.append(st[1])
            st_k.append(st[2])
            st_v.append(st[3])
        else:
            j = i // 2
            mixer = functools.partial(odd_mixer, w_in=w_in_odd[j], w_conv=w_hy_conv[j], b_conv=b_hy_conv[j],
                                      skip=hy_skip[j], wf1=w_filt1[j], bf1=b_filt1[j], freq=filt_freq[j],
                                      wf2=w_filt2[j], bf2=b_filt2[j], wf3=w_filt3[j], g_q=g_mla_q[j],
                                      w_qb=w_mla_qb[j], g_kv=g_mla_kv[j], w_kvb=w_mla_kvb[j], w_out=w_out_odd[j])
            oc, st = mixer(hc)
            os_, _ = mixer(hs, ctx=(cache_mla_ckv[:, j], cache_mla_kpe[:, j]), rope=rope_d)
            st_ckv.append(st[0])
            st_kpe.append(st[1])
        xc = xc + mc[2] * oc
        xs = xs + ms[2] * os_
        xc = xc + mc[5] * swiglu(modulate(xc, mc[3], mc[4]), w_ffn_in[i], w_ffn_out[i])
        xs = xs + ms[5] * swiglu(modulate(xs, ms[3], ms[4]), w_ffn_in[i], w_ffn_out[i])
    y_prompt = rms_norm(xc) * g_final
    y_sample = rms_norm(xs) * g_final
    new_state_gla_fwd = jnp.stack(st_gf, axis=1)
    new_state_gla_bwd = jnp.stack(st_gb, axis=1)
    new_cache_gqa_k = jnp.stack(st_k, axis=1)
    new_cache_gqa_v = jnp.stack(st_v, axis=1)
    new_cache_mla_ckv = jnp.stack(st_ckv, axis=1)
    new_cache_mla_kpe = jnp.stack(st_kpe, axis=1)
    return (y_prompt, y_sample, new_state_gla_fwd, new_state_gla_bwd, new_cache_gqa_k, new_cache_gqa_v, new_cache_mla_ckv, new_cache_mla_kpe)
```

```python
import functools
import math

import numpy as np
import jax
import jax.numpy as jnp
from jax import lax
from jax.experimental import pallas as pl
from jax.experimental.pallas import tpu as pltpu

F32 = jnp.float32
BF16 = jnp.bfloat16

D_MODEL = 1024
BATCH, SEQ = 16, 256
DEC_BATCH, DEC_SEQ = 2, 1024
DEPTH = 4
PAST_LEN = 512
GRID_W = 64
HALF_W = D_MODEL // 2
H_A, DV_A, DK_A = 4, 128, 64
GATE_RANK = 16
GLA_TAU = 16.0
GLA_CHUNK = 64
HD_B, H_B, KV_B = 128, 4, 2
HY_W = HALF_W
FILT_EMB, FILT_HID = 33, 64
FILT_BANDS = (FILT_EMB - 1) // 2
HY_MIN_DECAY = math.log(1e-2) / 1.5
HY_MAX_DECAY = math.log(1e-2) / 0.3
H_D, V_D, NOPE_D, ROPE_D = 4, 128, 128, 64
Q_RANK, KV_RANK = 256, 128
FFN_H = 2816
ROPE_THETA = 10000.0
EPS = 1e-6

LANES = 128
VMEM_LIMIT = 56 * 1024 * 1024

TM = 1024
EVEN_W = 2688
EVEN_TN = 896
ODD_W = 2304
ODD_TN = 768
FFN_TN = 256
OUT_TN = 512
QB = 256


def _params(n_grid):
    return pltpu.CompilerParams(dimension_semantics=("arbitrary",) * n_grid, vmem_limit_bytes=VMEM_LIMIT)


def _nt(a, b):
    return lax.dot_general(a, b, (((1,), (1,)), ((), ())), preferred_element_type=F32)


def _mm(a, b):
    return jnp.dot(a, b, preferred_element_type=F32)


def _rms(x):
    return x * lax.rsqrt(jnp.mean(x * x, axis=-1, keepdims=True) + EPS)


def _mod_kernel(c_ref, w_ref, b_ref, o_ref):
    cv = c_ref[...]
    s = cv * jax.nn.sigmoid(cv)
    o_ref[...] = _mm(s.astype(BF16), w_ref[...].astype(BF16)) + b_ref[...]


def _modulation(cvec, w_mod, b_mod):
    return pl.pallas_call(
        _mod_kernel,
        grid=(DEPTH, 6),
        in_specs=[
            pl.BlockSpec((8, D_MODEL), lambda l, n: (0, 0)),
            pl.BlockSpec((None, D_MODEL, D_MODEL), lambda l, n: (l, 0, n)),
            pl.BlockSpec((None, 1, D_MODEL), lambda l, n: (l, 0, n)),
        ],
        out_specs=pl.BlockSpec((None, None, 8, D_MODEL), lambda l, n: (l, n, 0, 0)),
        out_shape=jax.ShapeDtypeStruct((DEPTH, 6, 8, D_MODEL), F32),
        compiler_params=_params(2),
        name="adaln_mod",
    )(cvec, w_mod, b_mod.reshape(DEPTH, 1, 6 * D_MODEL))


def _modulated(x_ref, sh_ref, sc_ref, row0, rstep):
    g = row0 + rstep * pl.program_id(0)
    x = x_ref[...]
    sh = sh_ref[pl.ds(g, 1), :]
    sc = sc_ref[pl.ds(g, 1), :]
    return (_rms(x) * (1.0 + sc) + sh).astype(BF16)


def _nmm_kernel(x_ref, sh_ref, sc_ref, w_ref, o_ref, h_ref, *, row0, rstep):
    @pl.when(pl.program_id(1) == 0)
    def _():
        h_ref[...] = _modulated(x_ref, sh_ref, sc_ref, row0, rstep)

    o_ref[...] = _mm(h_ref[...], w_ref[...].astype(BF16)).astype(o_ref.dtype)


def _nmm_swiglu_kernel(x_ref, sh_ref, sc_ref, wg_ref, wu_ref, o_ref, h_ref, *, row0, rstep):
    @pl.when(pl.program_id(1) == 0)
    def _():
        h_ref[...] = _modulated(x_ref, sh_ref, sc_ref, row0, rstep)

    h = h_ref[...]
    g = _mm(h, wg_ref[...].astype(BF16))
    u = _mm(h, wu_ref[...].astype(BF16))
    o_ref[...] = (g * jax.nn.sigmoid(g) * u).astype(o_ref.dtype)


def _mod_specs(layer, k_shift, k_scale):
    return [
        pl.BlockSpec((None, None, 8, D_MODEL), lambda i, j: (layer, k_shift, 0, 0)),
        pl.BlockSpec((None, None, 8, D_MODEL), lambda i, j: (layer, k_scale, 0, 0)),
    ]


def _in_proj(x, mod, layer, w, tn, row0, rstep):
    m, n = x.shape[0], w.shape[1]
    return pl.pallas_call(
        functools.partial(_nmm_kernel, row0=row0, rstep=rstep),
        grid=(m // TM, n // tn),
        in_specs=[pl.BlockSpec((TM, D_MODEL), lambda i, j: (i, 0))] + _mod_specs(layer, 0, 1)
        + [pl.BlockSpec((D_MODEL, tn), lambda i, j: (0, j))],
        out_specs=pl.BlockSpec((TM, tn), lambda i, j: (i, j)),
        out_shape=jax.ShapeDtypeStruct((m, n), F32),
        scratch_shapes=[pltpu.VMEM((TM, D_MODEL), BF16)],
        compiler_params=_params(2),
        name="norm_mod_proj",
    )(x, mod, mod, w)


def _ffn_in(x, mod, layer, w, row0, rstep):
    m = x.shape[0]
    nj = FFN_H // FFN_TN
    return pl.pallas_call(
        functools.partial(_nmm_swiglu_kernel, row0=row0, rstep=rstep),
        grid=(m // TM, nj),
        in_specs=[pl.BlockSpec((TM, D_MODEL), lambda i, j: (i, 0))] + _mod_specs(layer, 3, 4)
        + [pl.BlockSpec((D_MODEL, FFN_TN), lambda i, j: (0, j)),
           pl.BlockSpec((D_MODEL, FFN_TN), lambda i, j: (0, j + nj))],
        out_specs=pl.BlockSpec((TM, FFN_TN), lambda i, j: (i, j)),
        out_shape=jax.ShapeDtypeStruct((m, FFN_H), BF16),
        scratch_shapes=[pltpu.VMEM((TM, D_MODEL), BF16)],
        compiler_params=_params(2),
        name="norm_mod_ffn_in",
    )(x, mod, mod, w, w)


def _proj_res_kernel(*refs, n_act, row0, rstep):
    acts, ws = refs[:n_act], refs[n_act:2 * n_act]
    x_ref, gate_ref, o_ref = refs[2 * n_act:]
    g = row0 + rstep * pl.program_id(0)
    acc = _mm(acts[0][...], ws[0][...].astype(BF16))
    for a_ref, w_ref in zip(acts[1:], ws[1:]):
        acc = acc + _mm(a_ref[...], w_ref[...].astype(BF16))
    o_ref[...] = x_ref[...] + gate_ref[pl.ds(g, 1), :] * acc


def _out_proj(acts, w, x, mod, layer, k_gate, row0, rstep):
    m = x.shape[0]
    n_act = len(acts)
    kw = acts[0].shape[1]
    act_specs = [pl.BlockSpec((TM, kw), lambda i, j: (i, 0)) for _ in acts]
    w_specs = [pl.BlockSpec((kw, OUT_TN), functools.partial(lambda i, j, p: (p, j), p=p)) for p in range(n_act)]
    return pl.pallas_call(
        functools.partial(_proj_res_kernel, n_act=n_act, row0=row0, rstep=rstep),
        grid=(m // TM, D_MODEL // OUT_TN),
        in_specs=act_specs + w_specs + [
            pl.BlockSpec((TM, OUT_TN), lambda i, j: (i, j)),
            pl.BlockSpec((None, None, 8, OUT_TN), lambda i, j: (layer, k_gate, 0, j)),
        ],
        out_specs=pl.BlockSpec((TM, OUT_TN), lambda i, j: (i, j)),
        out_shape=jax.ShapeDtypeStruct((m, D_MODEL), F32),
        compiler_params=_params(2),
        name="out_proj_residual",
    )(*acts, *([w] * n_act), x, mod)


def _final_kernel(x_ref, g_ref, o_ref):
    o_ref[...] = _rms(x_ref[...]) * g_ref[...]


def _final_norm(x, g):
    m = x.shape[0]
    return pl.pallas_call(
        _final_kernel,
        grid=(m // TM,),
        in_specs=[pl.BlockSpec((TM, D_MODEL), lambda i: (i, 0)), pl.BlockSpec((1, D_MODEL), lambda i: (0, 0))],
        out_specs=pl.BlockSpec((TM, D_MODEL), lambda i: (i, 0)),
        out_shape=jax.ShapeDtypeStruct((m, D_MODEL), F32),
        compiler_params=_params(1),
        name="final_norm",
    )(x, g.reshape(1, D_MODEL))


def _softmax_pv(s, v):
    m = jnp.max(s, axis=-1, keepdims=True)
    e = jnp.exp(s - m)
    l = jnp.sum(e, axis=-1, keepdims=True)
    return _mm(e.astype(BF16), v) / l


def _gqa_kernel(*refs, sample):
    if sample:
        q_ref, k_ref, v_ref, gq_ref, gk_ref, ck_ref, cv_ref, cos_ref, sin_ref, o_ref, kb_ref, vb_ref = refs
    else:
        q_ref, k_ref, v_ref, gq_ref, gk_ref, o_ref, kn_ref, kb_ref, vb_ref = refs
    qi = pl.program_id(1)
    n_new = k_ref.shape[0]
    past = PAST_LEN if sample else 0

    @pl.when(qi == 0)
    def _():
        for g in range(KV_B):
            sl = slice(HD_B * g, HD_B * (g + 1))
            kn = _rms(k_ref[:, sl]) * gk_ref[...]
            if sample:
                kb_ref[0:past, sl] = ck_ref[:, sl].astype(BF16)
                vb_ref[0:past, sl] = cv_ref[:, sl].astype(BF16)
                kn = kn * cos_ref[...] + pltpu.roll(kn, HD_B // 2, 1) * sin_ref[...]
            else:
                kn_ref[:, sl] = kn
            kb_ref[past:past + n_new, sl] = kn.astype(BF16)
            vb_ref[past:past + n_new, sl] = v_ref[:, sl].astype(BF16)

    r0 = pl.multiple_of(qi * QB, QB)
    for h in range(H_B):
        g = h // (H_B // KV_B)
        gs = slice(HD_B * g, HD_B * (g + 1))
        hs = slice(HD_B * h, HD_B * (h + 1))
        qn = _rms(q_ref[:, hs]) * gq_ref[...]
        if sample:
            qn = qn * cos_ref[pl.ds(r0, QB), :] + pltpu.roll(qn, HD_B // 2, 1) * sin_ref[pl.ds(r0, QB), :]
        s = _nt(qn.astype(BF16), kb_ref[:, gs]) * (HD_B ** -0.5)
        o_ref[:, hs] = _softmax_pv(s, vb_ref[:, gs]).astype(o_ref.dtype)


def _gqa(proj, g_q, g_k, n_batch, seq, ctx=None, rope=None):
    sample = ctx is not None
    m = n_batch * seq
    nq = seq // QB
    in_specs = [
        pl.BlockSpec((QB, 512), lambda b, i: (b * nq + i, 3)),
        pl.BlockSpec((seq, 256), lambda b, i: (b, 8)),
        pl.BlockSpec((seq, 256), lambda b, i: (b, 9)),
        pl.BlockSpec((1, HD_B), lambda b, i: (0, 0)),
        pl.BlockSpec((1, HD_B), lambda b, i: (0, 0)),
    ]
    args = [proj, proj, proj, g_q.reshape(1, HD_B), g_k.reshape(1, HD_B)]
    o_spec = pl.BlockSpec((QB, 512), lambda b, i: (b * nq + i, 0))
    o_shape = jax.ShapeDtypeStruct((m, 512), BF16)
    if sample:
        in_specs += [
            pl.BlockSpec((None, PAST_LEN, 256), lambda b, i: (b, 0, 0)),
            pl.BlockSpec((None, PAST_LEN, 256), lambda b, i: (b, 0, 0)),
            pl.BlockSpec((seq, HD_B), lambda b, i: (0, 0)),
            pl.BlockSpec((seq, HD_B), lambda b, i: (0, 0)),
        ]
        args += [ctx[0], ctx[1], rope[0], rope[1]]
        out_specs, out_shape = o_spec, o_shape
    else:
        out_specs = [o_spec, pl.BlockSpec((seq, 256), lambda b, i: (b, 0))]
        out_shape = [o_shape, jax.ShapeDtypeStruct((m, 256), F32)]
    n_keys = seq + (PAST_LEN if sample else 0)
    return pl.pallas_call(
        functools.partial(_gqa_kernel, sample=sample),
        grid=(n_batch, nq),
        in_specs=in_specs,
        out_specs=out_specs,
        out_shape=out_shape,
        scratch_shapes=[pltpu.VMEM((n_keys, 256), BF16), pltpu.VMEM((n_keys, 256), BF16)],
        compiler_params=_params(2),
        name="gqa_sample" if sample else "gqa_prompt",
    )(*args)


def _mla_kernel(*refs, sample):
    if sample:
        (cq_ref, ckv_ref, kx_ref, gq_ref, wqb_ref, gkv_ref, wkvb_ref, cckv_ref, ckpe_ref, ex_ref, c4_ref, s4_ref,
         o_ref, kv_s, kx_s) = refs
    else:
        cq_ref, ckv_ref, kx_ref, gq_ref, wqb_ref, gkv_ref, wkvb_ref, o_ref, ckvn_ref, kv_s, kx_s = refs
    qi = pl.program_id(1)
    n_new = ckv_ref.shape[0]
    past = PAST_LEN if sample else 0

    def rope(x, c, s):
        x1, x2 = x[:, :LANES], x[:, LANES:]
        return jnp.concatenate([x1 * c - x2 * s, x1 * s + x2 * c], axis=1)

    @pl.when(qi == 0)
    def _():
        wkvb = wkvb_ref[...].astype(BF16)
        ckvn = _rms(ckv_ref[...]) * gkv_ref[...]
        if not sample:
            ckvn_ref[...] = ckvn
        kv_s[past:past + n_new, :] = _mm(ckvn.astype(BF16), wkvb).astype(BF16)
        kx = kx_ref[...]
        if sample:
            kv_s[0:past, :] = _mm(cckv_ref[...].astype(BF16), wkvb).astype(BF16)
            kx_s[0:past, :] = _mm(ckpe_ref[...].astype(BF16), ex_ref[...]).astype(BF16)
            kx = rope(kx, c4_ref[...], s4_ref[...])
        kx_s[past:past + n_new, :] = kx.astype(BF16)

    q = _mm((_rms(cq_ref[...]) * gq_ref[...]).astype(BF16), wqb_ref[...].astype(BF16))
    qpe = q[:, 4 * NOPE_D:]
    if sample:
        r0 = pl.multiple_of(qi * QB, QB)
        qpe = rope(qpe, c4_ref[pl.ds(r0, QB), :], s4_ref[pl.ds(r0, QB), :])
    lane_head = (lax.broadcasted_iota(jnp.int32, (1, 2 * LANES), 1) % LANES) // (ROPE_D // 2)
    scale = (NOPE_D + ROPE_D) ** -0.5
    for h in range(H_D):
        qm = jnp.where(lane_head == h, qpe, 0.0)
        s = _nt(q[:, NOPE_D * h:NOPE_D * (h + 1)].astype(BF16), kv_s[:, 256 * h:256 * h + NOPE_D])
        s = (s + _nt(qm.astype(BF16), kx_s[...])) * scale
        o_ref[:, V_D * h:V_D * (h + 1)] = _softmax_pv(s, kv_s[:, 256 * h + NOPE_D:256 * (h + 1)]).astype(o_ref.dtype)


def _mla(proj, g_q, w_qb, g_kv, w_kvb, n_batch, seq, ctx=None, rope=None, expand=None):
    sample = ctx is not None
    m = n_batch * seq
    nq = seq // QB
    in_specs = [
        pl.BlockSpec((QB, Q_RANK), lambda b, i: (b * nq + i, 6)),
        pl.BlockSpec((seq, KV_RANK), lambda b, i: (b, 14)),
        pl.BlockSpec((seq, 256), lambda b, i: (b, 8)),
        pl.BlockSpec((1, Q_RANK), lambda b, i: (0, 0)),
        pl.BlockSpec((Q_RANK, 768), lambda b, i: (0, 0)),
        pl.BlockSpec((1, KV_RANK), lambda b, i: (0, 0)),
        pl.BlockSpec((KV_RANK, 1024), lambda b, i: (0, 0)),
    ]
    args = [proj, proj, proj, g_q.reshape(1, Q_RANK), w_qb, g_kv.reshape(1, KV_RANK), w_kvb]
    o_spec = pl.BlockSpec((QB, 512), lambda b, i: (b * nq + i, 0))
    o_shape = jax.ShapeDtypeStruct((m, 512), BF16)
    if sample:
        in_specs += [
            pl.BlockSpec((None, PAST_LEN, KV_RANK), lambda b, i: (b, 0, 0)),
            pl.BlockSpec((None, PAST_LEN, ROPE_D), lambda b, i: (b, 0, 0)),
            pl.BlockSpec((ROPE_D, 256), lambda b, i: (0, 0)),
            pl.BlockSpec((seq, LANES), lambda b, i: (0, 0)),
            pl.BlockSpec((seq, LANES), lambda b, i: (0, 0)),
        ]
        args += [ctx[0], ctx[1], expand, rope[0], rope[1]]
        out_specs, out_shape = o_spec, o_shape
    else:
        out_specs = [o_spec, pl.BlockSpec((seq, KV_RANK), lambda b, i: (b, 0))]
        out_shape = [o_shape, jax.ShapeDtypeStruct((m, KV_RANK), F32)]
    n_keys = seq + (PAST_LEN if sample else 0)
    return pl.pallas_call(
        functools.partial(_mla_kernel, sample=sample),
        grid=(n_batch, nq),
        in_specs=in_specs,
        out_specs=out_specs,
        out_shape=out_shape,
        scratch_shapes=[pltpu.VMEM((n_keys, 1024), BF16), pltpu.VMEM((n_keys, 256), BF16)],
        compiler_params=_params(2),
        name="mla_sample" if sample else "mla_prompt",
    )(*args)


def _split_bf16(x):
    hi = x.astype(BF16)
    return hi, (x - hi.astype(F32)).astype(BF16)


def _dft(t_hi, t_lo, x):
    x_hi, x_lo = _split_bf16(x)
    return _mm(t_hi, x_hi) + _mm(t_hi, x_lo) + _mm(t_lo, x_hi)


def _filter_kernel(z_ref, wf1_ref, bf1_ref, fr_ref, wf2_ref, bf2_ref, wf3_ref, t_ref, dl_ref,
                   ch_ref, cl_ref, sh_ref, sl_ref, gre_ref, gim_ref):
    n_tok = z_ref.shape[0]
    fr = fr_ref[...]
    hid = jnp.sin(fr * (_mm(z_ref[...].astype(BF16), wf1_ref[...].astype(BF16)) + bf1_ref[...]))
    hid = jnp.sin(fr * (_mm(hid.astype(BF16), wf2_ref[...].astype(BF16)) + bf2_ref[...]))
    filt = _mm(hid.astype(BF16), wf3_ref[...].astype(BF16))
    decay = jnp.exp(-t_ref[...] * dl_ref[...])
    row = lax.broadcasted_iota(jnp.int32, (n_tok, 1), 0)
    h_f = filt[:, :HY_W] * decay
    h_b = jnp.where(row == 0, 0.0, filt[:, HY_W:] * decay)
    p, m = h_f + h_b, h_f - h_b
    g_re = _dft(ch_ref[...], cl_ref[...], p)
    g_im = _dft(sh_ref[...], sl_ref[...], m)
    sign = jnp.where(row % 2 == 0, 1.0, -1.0)
    nyquist = jnp.sum(p * sign, axis=0, keepdims=True)
    g_im = jnp.where(row == 0, nyquist, g_im)
    wk = jnp.where(row == 0, 0.5 / n_tok, 1.0 / n_tok)
    gre_ref[...] = g_re * wk
    gim_ref[...] = g_im * wk


def _filter_spectrum(z, wf1, bf1, freq, wf2, bf2, wf3, t_col, deltas, tabs):
    n_tok = z.shape[0]
    out = jax.ShapeDtypeStruct((n_tok, HY_W), F32)
    return pl.pallas_call(
        _filter_kernel,
        out_shape=[out, out],
        compiler_params=pltpu.CompilerParams(vmem_limit_bytes=VMEM_LIMIT),
        name="hyena_filter",
    )(z, wf1, bf1.reshape(1, FILT_HID), freq.reshape(1, FILT_HID), wf2, bf2.reshape(1, FILT_HID), wf3,
      t_col, deltas, tabs[0], tabs[1], tabs[2], tabs[3])


HY_CT = 256


def _hyena_kernel(u0_ref, u1_ref, u2_ref, w0_ref, w1_ref, w2_ref, b0_ref, b1_ref, b2_ref, skip_ref,
                  gre_ref, gim_ref, ch_ref, cl_ref, sh_ref, sl_ref, th_ref, tl_ref, o_ref):
    n_tok = u0_ref.shape[0]
    row = lax.broadcasted_iota(jnp.int32, (n_tok, 1), 0)

    def short_conv(u_ref, w_ref, b_ref):
        x, w = u_ref[...], w_ref[...]
        prev = jnp.where(row == 0, 0.0, pltpu.roll(x, 1, 0))
        nxt = jnp.where(row == n_tok - 1, 0.0, pltpu.roll(x, n_tok - 1, 0))
        return prev * w[0:1] + x * w[1:2] + nxt * w[2:3] + b_ref[...]

    x0 = short_conv(u0_ref, w0_ref, b0_ref)
    gv = short_conv(u1_ref, w1_ref, b1_ref) * short_conv(u2_ref, w2_ref, b2_ref)
    u_re = _dft(ch_ref[...], cl_ref[...], gv)
    u_im = _dft(sh_ref[...], sl_ref[...], gv)
    g_re, g_im = gre_ref[...], gim_ref[...]
    p_im = u_im * g_im
    y_re = u_re * g_re - jnp.where(row == 0, 0.0, p_im)
    y_im = jnp.where(row == 0, p_im, u_re * g_im + u_im * g_re)
    y = _dft(ch_ref[...], cl_ref[...], y_re) + _dft(th_ref[...], tl_ref[...], y_im)
    o_ref[...] = (x0 * (y + gv * skip_ref[...])).astype(o_ref.dtype)


def _hyena(proj, w_conv, b_conv, skip, g_re, g_im, tabs, n_batch, seq):
    nct = HY_W // HY_CT
    u_specs = [pl.BlockSpec((seq, HY_CT), functools.partial(lambda b, c, g: (b, g * nct + c), g=g)) for g in range(3)]
    w_specs = [pl.BlockSpec((3, HY_CT), functools.partial(lambda b, c, g: (0, g * nct + c), g=g)) for g in range(3)]
    b_specs = [pl.BlockSpec((1, HY_CT), functools.partial(lambda b, c, g: (0, g * nct + c), g=g)) for g in range(3)]
    tab_spec = pl.BlockSpec((seq, seq), lambda b, c: (0, 0))
    return pl.pallas_call(
        _hyena_kernel,
        grid=(n_batch, nct),
        in_specs=u_specs + w_specs + b_specs + [
            pl.BlockSpec((1, HY_CT), lambda b, c: (0, c)),
            pl.BlockSpec((seq, HY_CT), lambda b, c: (0, c)),
            pl.BlockSpec((seq, HY_CT), lambda b, c: (0, c)),
        ] + [tab_spec] * 6,
        out_specs=pl.BlockSpec((seq, HY_CT), lambda b, c: (b, c)),
        out_shape=jax.ShapeDtypeStruct((n_batch * seq, HY_W), BF16),
        compiler_params=_params(2),
        name="hyena_conv",
    )(proj, proj, proj, w_conv, w_conv, w_conv, b_conv, b_conv, b_conv, skip.reshape(1, HY_W), g_re, g_im, *tabs)


def _dft_tables(n_tok):
    k = jnp.arange(n_tok, dtype=jnp.int32)[:, None]
    s = jnp.arange(n_tok, dtype=jnp.int32)[None, :]
    ang = ((k * s) % (2 * n_tok)).astype(F32) * (math.pi / n_tok)
    cos_t = jnp.cos(ang)
    sin_f = jnp.where(k == 0, jnp.where(s % 2 == 0, 1.0, -1.0), -jnp.sin(ang))
    out = []
    for t in (cos_t, sin_f, sin_f.T):
        hi = t.astype(BF16)
        out += [hi, (t - hi.astype(F32)).astype(BF16)]
    return out


GLA_LEVELS = (32, 16, 8, 4, 2, 1)
_ROW_B = 2 * GLA_CHUNK * len(GLA_LEVELS)
_ROW_R = _ROW_B + GLA_CHUNK
_ROW_LAST = _ROW_R + GLA_CHUNK


def _gla_constants():
    c = GLA_CHUNK
    idx = np.arange(c)
    i, t = idx[:, None], idx[None, :]
    blocks, masks = [], []
    for s in GLA_LEVELS:
        mid = (idx // (2 * s)) * (2 * s) + s
        upper = (idx % (2 * s)) >= s
        blocks.append(upper[:, None] & (t >= mid[:, None]) & (t <= i))
        blocks.append((~upper)[:, None] & (t > i) & (t <= mid[:, None] - 1))
        masks.append(((i // (2 * s)) == (t // (2 * s))) & upper[:, None] & (~upper)[None, :])
    masks.append(i == t)
    blocks += [t <= i, t > i]
    ones = np.ones((2 * c, c), bool)
    fwd_t = np.concatenate(blocks + [ones], 0).astype(np.float32)
    bwd_t = np.concatenate([b[::-1, ::-1] for b in blocks] + [ones], 0).astype(np.float32)
    fwd_m = np.stack([np.tile(m, (H_A, 1)) for m in masks]).astype(np.float32)
    bwd_m = np.stack([np.tile(m[::-1, ::-1], (H_A, 1)) for m in masks]).astype(np.float32)
    head_of_row = np.repeat(np.arange(H_A), c)[:, None]
    head_of_lane = np.repeat(np.arange(H_A), DK_A)[None, :]
    head_mask = (head_of_row == head_of_lane).astype(np.float32)
    return fwd_t, bwd_t, fwd_m, bwd_m, head_mask


def _gla_chunk(q, k, v, la, t_ref, m_ref, hm, s_ref):
    c = GLA_CHUNK
    l1 = la.astype(BF16)
    r1 = la - l1.astype(F32)
    l2 = r1.astype(BF16)
    l3 = (r1 - l2.astype(F32)).astype(BF16)
    tmat = t_ref[...]
    x = jnp.exp(_mm(tmat, l1) + _mm(tmat, l2) + _mm(tmat, l3))

    def stack_heads(a):
        return (jnp.concatenate([a] * H_A, axis=0) * hm).astype(BF16)

    scores = _nt(stack_heads(q), k.astype(BF16)) * m_ref[len(GLA_LEVELS)]
    for lvl in range(len(GLA_LEVELS)):
        qt = q * x[2 * c * lvl:2 * c * lvl + c]
        kt = k * x[2 * c * lvl + c:2 * c * (lvl + 1)]
        scores = scores + _nt(stack_heads(qt), kt.astype(BF16)) * m_ref[lvl]
    scores = scores.astype(BF16)
    state = s_ref[...]
    inter = _mm(stack_heads(q * x[_ROW_B:_ROW_B + c]), state.astype(BF16))
    k_rest = (k * x[_ROW_R:_ROW_R + c]).T
    carry = x[_ROW_LAST:_ROW_LAST + 2 * c].T
    outs = []
    for h in range(H_A):
        rows = slice(c * h, c * (h + 1))
        v_h = v[:, DV_A * h:DV_A * (h + 1)].astype(BF16)
        outs.append(_mm(scores[rows], v_h) + inter[rows])
        s_ref[rows, :] = state[rows] * carry[rows] + _mm(k_rest[rows].astype(BF16), v_h)
    return jnp.concatenate(outs, axis=1)


def _gla_kernel(*refs, sample):
    if sample:
        (x_ref, z_ref, wf_ref, bf_ref, wb_ref, bb_ref, tf_ref, tb_ref, mf_ref, mb_ref, hm_ref, gn_ref, sf0_ref, sb0_ref,
         o_ref, la_f, la_b, o_f, o_b, s_f, s_b) = refs
    else:
        (x_ref, z_ref, wf_ref, bf_ref, wb_ref, bb_ref, tf_ref, tb_ref, mf_ref, mb_ref, hm_ref, gn_ref,
         o_ref, sf_out, sb_out, la_f, la_b, o_f, o_b, s_f, s_b) = refs
    n_tok = x_ref.shape[0]
    n_chunks = n_tok // GLA_CHUNK
    hk, hv = H_A * DK_A, H_A * DV_A
    zb = z_ref[...].astype(BF16)
    la_f[...] = jax.nn.log_sigmoid(_mm(zb, wf_ref[...].astype(BF16)) + bf_ref[...]) / GLA_TAU
    la_b[...] = jax.nn.log_sigmoid(_mm(zb, wb_ref[...].astype(BF16)) + bb_ref[...]) / GLA_TAU
    if sample:
        s_f[...] = sf0_ref[...]
        s_b[...] = sb0_ref[...]
    else:
        s_f[...] = jnp.zeros_like(s_f)
        s_b[...] = jnp.zeros_like(s_b)
    hm = hm_ref[...]

    def step(ci, carry):
        for la_ref, t_ref, m_ref, s_ref, out_ref, cidx in ((la_f, tf_ref, mf_ref, s_f, o_f, ci),
                                                           (la_b, tb_ref, mb_ref, s_b, o_b, n_chunks - 1 - ci)):
            rows = pl.ds(pl.multiple_of(cidx * GLA_CHUNK, GLA_CHUNK), GLA_CHUNK)
            q = x_ref[rows, 0:hk] * (DK_A ** -0.5)
            k = x_ref[rows, hk:2 * hk]
            v = x_ref[rows, 2 * hk:2 * hk + hv]
            out_ref[rows, :] = _gla_chunk(q, k, v, la_ref[rows, :], t_ref, m_ref, hm, s_ref)
        return carry

    lax.fori_loop(0, n_chunks, step, 0)
    if not sample:
        sf_out[...] = s_f[...]
        sb_out[...] = s_b[...]
    gain = gn_ref[...]
    for h in range(H_A):
        cols = slice(DV_A * h, DV_A * (h + 1))
        r = x_ref[:, 2 * hk + hv + DV_A * h:2 * hk + hv + DV_A * (h + 1)]
        o_ref[:, cols] = (_rms(o_f[:, cols] + o_b[:, cols]) * gain * (r * jax.nn.sigmoid(r))).astype(o_ref.dtype)


def _gla(proj, w_gf, b_gf, w_gb, b_gb, g_norm, consts, n_batch, seq, ctx=None):
    sample = ctx is not None
    hk, hv = H_A * DK_A, H_A * DV_A
    full = lambda shape: pl.BlockSpec(shape, lambda b: (0,) * len(shape))
    in_specs = [
        pl.BlockSpec((seq, 2 * hk + 2 * hv), lambda b: (b, 0)),
        pl.BlockSpec((seq, LANES), lambda b: (b, EVEN_W // LANES - 1)),
        full((LANES, hk)), full((1, hk)), full((LANES, hk)), full((1, hk)),
        full(consts[0].shape), full(consts[1].shape), full(consts[2].shape), full(consts[3].shape), full(consts[4].shape),
        full((1, DV_A)),
    ]
    args = [proj, proj, w_gf, b_gf.reshape(1, hk), w_gb, b_gb.reshape(1, hk), *consts, g_norm.reshape(1, DV_A)]
    o_spec = pl.BlockSpec((seq, hv), lambda b: (b, 0))
    o_shape = jax.ShapeDtypeStruct((n_batch * seq, hv), BF16)
    st_spec = pl.BlockSpec((None, hk, DV_A), lambda b: (b, 0, 0))
    if sample:
        in_specs += [st_spec, st_spec]
        args += [ctx[0], ctx[1]]
        out_specs, out_shape = o_spec, o_shape
    else:
        st_shape = jax.ShapeDtypeStruct((n_batch, hk, DV_A), F32)
        out_specs, out_shape = [o_spec, st_spec, st_spec], [o_shape, st_shape, st_shape]
    return pl.pallas_call(
        functools.partial(_gla_kernel, sample=sample),
        grid=(n_batch,),
        in_specs=in_specs,
        out_specs=out_specs,
        out_shape=out_shape,
        scratch_shapes=[pltpu.VMEM((seq, hk), F32), pltpu.VMEM((seq, hk), F32),
                        pltpu.VMEM((seq, hv), F32), pltpu.VMEM((seq, hv), F32),
                        pltpu.VMEM((hk, DV_A), F32), pltpu.VMEM((hk, DV_A), F32)],
        compiler_params=_params(1),
        name="gla_sample" if sample else "gla_prompt",
    )(*args)


def _axial_rope(n_tokens, dim):
    rows = n_tokens // GRID_W
    row = jnp.broadcast_to(jnp.arange(rows, dtype=F32)[:, None], (rows, GRID_W)).reshape(n_tokens)
    col = jnp.broadcast_to(jnp.arange(GRID_W, dtype=F32)[None, :], (rows, GRID_W)).reshape(n_tokens)
    n_freq = dim // 4
    inv = ROPE_THETA ** (-jnp.arange(n_freq, dtype=F32) / n_freq)
    ang = jnp.concatenate([row[:, None] * inv, col[:, None] * inv], axis=-1)
    return jnp.cos(ang), jnp.sin(ang)


def _filter_features(n_tokens):
    t = jnp.linspace(0.0, 1.0, n_tokens, dtype=F32)[:, None]
    w = 2.0 * math.pi * jnp.arange(n_tokens, dtype=F32)[:, None] / n_tokens
    f = jnp.linspace(1e-4, FILT_BANDS - 1, FILT_BANDS, dtype=F32)[None, :]
    z = jnp.concatenate([t, jnp.cos(f * w), -jnp.sin(f * w)], axis=-1)
    return jnp.pad(z, ((0, 0), (0, LANES - FILT_EMB))), t


_KPE_EXPAND = np.array([(p // LANES) * (ROPE_D // 2) + p % (ROPE_D // 2) for p in range(2 * LANES)])
_QB_PERM = np.array(
    [192 * (p // NOPE_D) + p % NOPE_D for p in range(H_D * NOPE_D)]
    + [192 * (p // 32) + NOPE_D + p % 32 for p in range(H_D * 32)]
    + [192 * (p // 32) + NOPE_D + 32 + p % 32 for p in range(H_D * 32)])


def _even_in_weight(w):
    return jnp.concatenate([w[:, :1536], w[:, 1568:2592], w[:, 1536:1568], jnp.zeros((D_MODEL, 96), F32)], axis=1)


def _odd_in_weight(w):
    return jnp.concatenate([w[:, :1984], jnp.zeros((D_MODEL, 64), F32), w[:, 1920 + _KPE_EXPAND]], axis=1)


def kernel(x_prompt, x_sample, state_gla_fwd, state_gla_bwd, cache_gqa_k, cache_gqa_v, cache_mla_ckv, cache_mla_kpe, c, c_ctx, w_mod, b_mod, w_in_even, w_gla_gate_f, b_gla_gate_f, w_gla_gate_b, b_gla_gate_b, g_gla_norm, g_gqa_q, g_gqa_k, w_out_even, w_in_odd, w_hy_conv, b_hy_conv, hy_skip, w_filt1, b_filt1, filt_freq, w_filt2, b_filt2, w_filt3, g_mla_q, w_mla_qb, g_mla_kv, w_mla_kvb, w_out_odd, w_ffn_in, w_ffn_out, g_final):
    n_c, n_s = BATCH * SEQ, DEC_BATCH * DEC_SEQ
    cvec = jnp.concatenate([c_ctx[None, :], c, jnp.zeros((8 - 1 - DEC_BATCH, D_MODEL), F32)], axis=0)
    mod = _modulation(cvec, w_mod, b_mod)
    xc = x_prompt.reshape(n_c, D_MODEL)
    xs = x_sample.reshape(n_s, D_MODEL)
    rows_c, rows_s = (0, 0), (1, DEC_SEQ // TM)

    gla_np = _gla_constants()
    gla_consts = [jnp.asarray(gla_np[0], BF16), jnp.asarray(gla_np[1], BF16),
                  jnp.asarray(gla_np[2]), jnp.asarray(gla_np[3]), jnp.asarray(gla_np[4])]
    cos_b, sin_b = _axial_rope(DEC_SEQ, HD_B)
    rope_b = (jnp.concatenate([cos_b, cos_b], axis=1), jnp.concatenate([-sin_b, sin_b], axis=1))
    cos_d, sin_d = _axial_rope(DEC_SEQ, ROPE_D)
    rope_d = (jnp.tile(cos_d, (1, H_D)), jnp.tile(sin_d, (1, H_D)))
    kpe_expand = jnp.asarray(np.arange(ROPE_D)[:, None] == _KPE_EXPAND[None, :], BF16)
    tabs_c, tabs_s = _dft_tables(SEQ), _dft_tables(DEC_SEQ)
    z_c, t_c = _filter_features(SEQ)
    z_s, t_s = _filter_features(DEC_SEQ)
    deltas = jnp.abs(jnp.linspace(HY_MIN_DECAY, HY_MAX_DECAY, HY_W, dtype=F32))[None, :]

    st_gf, st_gb, st_k, st_v, st_ckv, st_kpe = [], [], [], [], [], []
    for i in range(DEPTH):
        j = i // 2
        if i % 2 == 0:
            w_in = _even_in_weight(w_in_even[j])
            pad_f = jnp.zeros((LANES, H_A * DK_A), F32).at[0:GATE_RANK].set(w_gla_gate_f[j])
            pad_b = jnp.zeros((LANES, H_A * DK_A), F32).at[GATE_RANK:2 * GATE_RANK].set(w_gla_gate_b[j])
            pc = _in_proj(xc, mod, i, w_in, EVEN_TN, *rows_c)
            ps = _in_proj(xs, mod, i, w_in, EVEN_TN, *rows_s)
            gate_args = (pad_f, b_gla_gate_f[j], pad_b, b_gla_gate_b[j], g_gla_norm[j], gla_consts)
            a_c, s_f, s_b = _gla(pc, *gate_args, BATCH, SEQ)
            ctx_a = (state_gla_fwd[:, j].reshape(DEC_BATCH, H_A * DK_A, DV_A),
                     state_gla_bwd[:, j].reshape(DEC_BATCH, H_A * DK_A, DV_A))
            a_s = _gla(ps, *gate_args, DEC_BATCH, DEC_SEQ, ctx=ctx_a)
            b_c, k_norm = _gqa(pc, g_gqa_q[j], g_gqa_k[j], BATCH, SEQ)
            ctx_b = (cache_gqa_k[:, j].reshape(DEC_BATCH, PAST_LEN, KV_B * HD_B),
                     cache_gqa_v[:, j].reshape(DEC_BATCH, PAST_LEN, KV_B * HD_B))
            b_s = _gqa(ps, g_gqa_q[j], g_gqa_k[j], DEC_BATCH, DEC_SEQ, ctx=ctx_b, rope=rope_b)
            w_out = w_out_even[j]
            st_gf.append(s_f.reshape(BATCH, H_A, DK_A, DV_A))
            st_gb.append(s_b.reshape(BATCH, H_A, DK_A, DV_A))
            st_k.append(k_norm.reshape(BATCH, SEQ, KV_B, HD_B))
            st_v.append(pc[:, 2304:2560].reshape(BATCH, SEQ, KV_B, HD_B))
        else:
            w_in = _odd_in_weight(w_in_odd[j])
            pc = _in_proj(xc, mod, i, w_in, ODD_TN, *rows_c)
            ps = _in_proj(xs, mod, i, w_in, ODD_TN, *rows_s)
            wf1 = jnp.pad(w_filt1[j], ((0, LANES - FILT_EMB), (0, 0)))
            filt_args = (wf1, b_filt1[j], filt_freq[j], w_filt2[j], b_filt2[j], w_filt3[j])
            g_c = _filter_spectrum(z_c, *filt_args, t_c, deltas, tabs_c)
            g_s = _filter_spectrum(z_s, *filt_args, t_s, deltas, tabs_s)
            b_conv = b_hy_conv[j].reshape(1, 3 * HY_W)
            a_c = _hyena(pc, w_hy_conv[j], b_conv, hy_skip[j], g_c[0], g_c[1], tabs_c, BATCH, SEQ)
            a_s = _hyena(ps, w_hy_conv[j], b_conv, hy_skip[j], g_s[0], g_s[1], tabs_s, DEC_BATCH, DEC_SEQ)
            w_qb = w_mla_qb[j][:, _QB_PERM]
            b_c, ckv_norm = _mla(pc, g_mla_q[j], w_qb, g_mla_kv[j], w_mla_kvb[j], BATCH, SEQ)
            b_s = _mla(ps, g_mla_q[j], w_qb, g_mla_kv[j], w_mla_kvb[j], DEC_BATCH, DEC_SEQ,
                       ctx=(cache_mla_ckv[:, j], cache_mla_kpe[:, j]), rope=rope_d, expand=kpe_expand)
            w_out = w_out_odd[j]
            st_ckv.append(ckv_norm.reshape(BATCH, SEQ, KV_RANK))
            st_kpe.append(pc[:, 1920:1984].reshape(BATCH, SEQ, ROPE_D))
        xc = _out_proj([a_c, b_c], w_out, xc, mod, i, 2, *rows_c)
        xs = _out_proj([a_s, b_s], w_out, xs, mod, i, 2, *rows_s)
        f_c = _ffn_in(xc, mod, i, w_ffn_in[i], *rows_c)
        f_s = _ffn_in(xs, mod, i, w_ffn_in[i], *rows_s)
        xc = _out_proj([f_c], w_ffn_out[i], xc, mod, i, 5, *rows_c)
        xs = _out_proj([f_s], w_ffn_out[i], xs, mod, i, 5, *rows_s)
    y_prompt = _final_norm(xc, g_final).reshape(BATCH, SEQ, D_MODEL)
    y_sample = _final_norm(xs, g_final).reshape(DEC_BATCH, DEC_SEQ, D_MODEL)
    return (y_prompt, y_sample, jnp.stack(st_gf, axis=1), jnp.stack(st_gb, axis=1), jnp.stack(st_k, axis=1),
            jnp.stack(st_v, axis=1), jnp.stack(st_ckv, axis=1), jnp.stack(st_kpe, axis=1))
```

```python
import functools
import math

import numpy as np
import jax
import jax.numpy as jnp
from jax import lax
from jax.experimental import pallas as pl
from jax.experimental.pallas import tpu as pltpu

F32 = jnp.float32
BF16 = jnp.bfloat16

D_MODEL = 1024
BATCH, SEQ = 16, 256
DEC_BATCH, DEC_SEQ = 2, 1024
DEPTH = 4
PAST_LEN = 512
GRID_W = 64
HALF_W = D_MODEL // 2
H_A, DV_A, DK_A = 4, 128, 64
GATE_RANK = 16
GLA_TAU = 16.0
GLA_CHUNK = 64
HD_B, H_B, KV_B = 128, 4, 2
HY_W = HALF_W
FILT_EMB, FILT_HID = 33, 64
FILT_BANDS = (FILT_EMB - 1) // 2
HY_MIN_DECAY = math.log(1e-2) / 1.5
HY_MAX_DECAY = math.log(1e-2) / 0.3
H_D, V_D, NOPE_D, ROPE_D = 4, 128, 128, 64
Q_RANK, KV_RANK = 256, 128
FFN_H = 2816
ROPE_THETA = 10000.0
EPS = 1e-6

LANES = 128
VMEM_LIMIT = 56 * 1024 * 1024

MOD_ROWS = 1024
TM = 1024
TM_FFN = 2048
EVEN_W = 2688
EVEN_TN = EVEN_W // 3
ODD_W = 2304
ODD_TN = ODD_W // 2
FFN_TN = 256
QB = 256


def _params(n_grid):
    return pltpu.CompilerParams(dimension_semantics=("arbitrary",) * n_grid, vmem_limit_bytes=VMEM_LIMIT)


def _nt(a, b):
    return lax.dot_general(a, b, (((1,), (1,)), ((), ())), preferred_element_type=F32)


def _mm(a, b):
    return jnp.dot(a, b, preferred_element_type=F32)


def _rms(x):
    return x * lax.rsqrt(jnp.mean(x * x, axis=-1, keepdims=True) + EPS)


def _mod_kernel(c_ref, w_ref, b_ref, o_ref):
    cv = c_ref[...]
    s = cv * jax.nn.sigmoid(cv)
    o_ref[...] = _mm(s.astype(BF16), w_ref[...].astype(BF16)) + b_ref[...]


def _modulation(cvec, w_mod, b_mod):
    return pl.pallas_call(
        _mod_kernel,
        grid=(DEPTH, 6),
        in_specs=[
            pl.BlockSpec((8, D_MODEL), lambda l, n: (0, 0)),
            pl.BlockSpec((None, D_MODEL, D_MODEL), lambda l, n: (l, 0, n)),
            pl.BlockSpec((None, 1, D_MODEL), lambda l, n: (l, 0, n)),
        ],
        out_specs=pl.BlockSpec((None, None, 8, D_MODEL), lambda l, n: (l, n, 0, 0)),
        out_shape=jax.ShapeDtypeStruct((DEPTH, 6, 8, D_MODEL), F32),
        compiler_params=_params(2),
        name="adaln_mod",
    )(cvec, w_mod, b_mod.reshape(DEPTH, 1, 6 * D_MODEL))


def _modulated(x_ref, sh_ref, sc_ref, h_ref, row0, rstep):
    n_sub = x_ref.shape[0] // MOD_ROWS
    for s in range(n_sub):
        g = row0 + rstep * (pl.program_id(0) * n_sub + s)
        rows = slice(s * MOD_ROWS, (s + 1) * MOD_ROWS)
        sh = sh_ref[pl.ds(g, 1), :]
        sc = sc_ref[pl.ds(g, 1), :]
        h_ref[rows, :] = (_rms(x_ref[rows, :]) * (1.0 + sc) + sh).astype(BF16)


def _nmm_kernel(x_ref, sh_ref, sc_ref, w_ref, o_ref, h_ref, *, row0, rstep):
    @pl.when(pl.program_id(1) == 0)
    def _():
        _modulated(x_ref, sh_ref, sc_ref, h_ref, row0, rstep)

    o_ref[...] = _mm(h_ref[...], w_ref[...].astype(BF16)).astype(o_ref.dtype)


def _nmm_swiglu_kernel(x_ref, sh_ref, sc_ref, wg_ref, wu_ref, o_ref, h_ref, *, row0, rstep):
    @pl.when(pl.program_id(1) == 0)
    def _():
        _modulated(x_ref, sh_ref, sc_ref, h_ref, row0, rstep)

    h = h_ref[...]
    g = _mm(h, wg_ref[...].astype(BF16))
    u = _mm(h, wu_ref[...].astype(BF16))
    o_ref[...] = (g * jax.nn.sigmoid(g) * u).astype(o_ref.dtype)


def _mod_specs(layer, k_shift, k_scale):
    return [
        pl.BlockSpec((None, None, 8, D_MODEL), lambda i, j: (layer, k_shift, 0, 0)),
        pl.BlockSpec((None, None, 8, D_MODEL), lambda i, j: (layer, k_scale, 0, 0)),
    ]


def _in_proj(x, mod, layer, w, w_layer, tn, row0, rstep):
    m, n = x.shape[0], w.shape[2]
    return pl.pallas_call(
        functools.partial(_nmm_kernel, row0=row0, rstep=rstep),
        grid=(m // TM, n // tn),
        in_specs=[pl.BlockSpec((TM, D_MODEL), lambda i, j: (i, 0))] + _mod_specs(layer, 0, 1)
        + [pl.BlockSpec((None, D_MODEL, tn), lambda i, j: (w_layer, 0, j))],
        out_specs=pl.BlockSpec((TM, tn), lambda i, j: (i, j)),
        out_shape=jax.ShapeDtypeStruct((m, n), F32),
        scratch_shapes=[pltpu.VMEM((TM, D_MODEL), BF16)],
        compiler_params=_params(2),
        name="norm_mod_proj",
    )(x, mod, mod, w)


def _ffn_in(x, mod, layer, w, row0, rstep):
    m = x.shape[0]
    nj = FFN_H // FFN_TN
    return pl.pallas_call(
        functools.partial(_nmm_swiglu_kernel, row0=row0, rstep=rstep),
        grid=(m // TM_FFN, nj),
        in_specs=[pl.BlockSpec((TM_FFN, D_MODEL), lambda i, j: (i, 0))] + _mod_specs(layer, 3, 4)
        + [pl.BlockSpec((None, D_MODEL, FFN_TN), lambda i, j: (layer, 0, j)),
           pl.BlockSpec((None, D_MODEL, FFN_TN), lambda i, j: (layer, 0, j + nj))],
        out_specs=pl.BlockSpec((TM_FFN, FFN_TN), lambda i, j: (i, j)),
        out_shape=jax.ShapeDtypeStruct((m, FFN_H), BF16),
        scratch_shapes=[pltpu.VMEM((TM_FFN, D_MODEL), BF16)],
        compiler_params=_params(2),
        name="norm_mod_ffn_in",
    )(x, mod, mod, w, w)


def _proj_res_kernel(*refs, n_act, row0, rstep):
    acts, ws = refs[:n_act], refs[n_act:2 * n_act]
    x_ref, gate_ref, o_ref = refs[2 * n_act:]
    g = row0 + rstep * pl.program_id(0)
    acc = _mm(acts[0][...], ws[0][...].astype(BF16))
    for a_ref, w_ref in zip(acts[1:], ws[1:]):
        acc = acc + _mm(a_ref[...], w_ref[...].astype(BF16))
    o_ref[...] = x_ref[...] + gate_ref[pl.ds(g, 1), :] * acc


def _out_proj(acts, w, w_layer, x, mod, layer, k_gate, row0, rstep):
    m = x.shape[0]
    n_act = len(acts)
    kw = acts[0].shape[1]
    act_specs = [pl.BlockSpec((TM, kw), lambda i: (i, 0)) for _ in acts]
    w_specs = [pl.BlockSpec((None, kw, D_MODEL), functools.partial(lambda i, p: (w_layer, p, 0), p=p),
                            pipeline_mode=pl.Buffered(1)) for p in range(n_act)]
    return pl.pallas_call(
        functools.partial(_proj_res_kernel, n_act=n_act, row0=row0, rstep=rstep),
        grid=(m // TM,),
        in_specs=act_specs + w_specs + [
            pl.BlockSpec((TM, D_MODEL), lambda i: (i, 0)),
            pl.BlockSpec((None, None, 8, D_MODEL), lambda i: (layer, k_gate, 0, 0)),
        ],
        out_specs=pl.BlockSpec((TM, D_MODEL), lambda i: (i, 0)),
        out_shape=jax.ShapeDtypeStruct((m, D_MODEL), F32),
        compiler_params=_params(1),
        name="out_proj_residual",
    )(*acts, *([w] * n_act), x, mod)


def _final_kernel(x_ref, g_ref, o_ref):
    o_ref[...] = _rms(x_ref[...]) * g_ref[...]


def _final_norm(x, g):
    m = x.shape[0]
    return pl.pallas_call(
        _final_kernel,
        grid=(m // TM,),
        in_specs=[pl.BlockSpec((TM, D_MODEL), lambda i: (i, 0)), pl.BlockSpec((1, D_MODEL), lambda i: (0, 0))],
        out_specs=pl.BlockSpec((TM, D_MODEL), lambda i: (i, 0)),
        out_shape=jax.ShapeDtypeStruct((m, D_MODEL), F32),
        compiler_params=_params(1),
        name="final_norm",
    )(x, g.reshape(1, D_MODEL))


def _softmax_pv(s, v):
    m = jnp.max(s, axis=-1, keepdims=True)
    e = jnp.exp(s - m)
    l = jnp.sum(e, axis=-1, keepdims=True)
    return _mm(e.astype(BF16), v) / l


def _gqa_kernel(*refs, sample):
    if sample:
        q_ref, k_ref, v_ref, gq_ref, gk_ref, ck_ref, cv_ref, cos_ref, sin_ref, o_ref, kb_ref, vb_ref = refs
    else:
        q_ref, k_ref, v_ref, gq_ref, gk_ref, o_ref, kn_ref, kb_ref, vb_ref = refs
    qi = pl.program_id(1)
    n_new = k_ref.shape[0]
    past = PAST_LEN if sample else 0

    @pl.when(qi == 0)
    def _():
        for g in range(KV_B):
            sl = slice(HD_B * g, HD_B * (g + 1))
            kn = _rms(k_ref[:, sl]) * gk_ref[...]
            if sample:
                kb_ref[0:past, sl] = ck_ref[:, sl].astype(BF16)
                vb_ref[0:past, sl] = cv_ref[:, sl].astype(BF16)
                kn = kn * cos_ref[...] + pltpu.roll(kn, HD_B // 2, 1) * sin_ref[...]
            else:
                kn_ref[:, sl] = kn
            kb_ref[past:past + n_new, sl] = kn.astype(BF16)
            vb_ref[past:past + n_new, sl] = v_ref[:, sl].astype(BF16)

    r0 = pl.multiple_of(qi * QB, QB)
    for h in range(H_B):
        g = h // (H_B // KV_B)
        gs = slice(HD_B * g, HD_B * (g + 1))
        hs = slice(HD_B * h, HD_B * (h + 1))
        qn = _rms(q_ref[:, hs]) * gq_ref[...]
        if sample:
            qn = qn * cos_ref[pl.ds(r0, QB), :] + pltpu.roll(qn, HD_B // 2, 1) * sin_ref[pl.ds(r0, QB), :]
        s = _nt(qn.astype(BF16), kb_ref[:, gs]) * (HD_B ** -0.5)
        o_ref[:, hs] = _softmax_pv(s, vb_ref[:, gs]).astype(o_ref.dtype)


def _gqa(proj, g_q, g_k, n_batch, seq, ctx=None, rope=None):
    sample = ctx is not None
    m = n_batch * seq
    nq = seq // QB
    in_specs = [
        pl.BlockSpec((QB, 512), lambda b, i: (b * nq + i, 3)),
        pl.BlockSpec((seq, 256), lambda b, i: (b, 8)),
        pl.BlockSpec((seq, 256), lambda b, i: (b, 9)),
        pl.BlockSpec((1, HD_B), lambda b, i: (0, 0)),
        pl.BlockSpec((1, HD_B), lambda b, i: (0, 0)),
    ]
    args = [proj, proj, proj, g_q.reshape(1, HD_B), g_k.reshape(1, HD_B)]
    o_spec = pl.BlockSpec((QB, 512), lambda b, i: (b * nq + i, 0))
    o_shape = jax.ShapeDtypeStruct((m, 512), BF16)
    if sample:
        in_specs += [
            pl.BlockSpec((None, PAST_LEN, 256), lambda b, i: (b, 0, 0)),
            pl.BlockSpec((None, PAST_LEN, 256), lambda b, i: (b, 0, 0)),
            pl.BlockSpec((seq, HD_B), lambda b, i: (0, 0)),
            pl.BlockSpec((seq, HD_B), lambda b, i: (0, 0)),
        ]
        args += [ctx[0], ctx[1], rope[0], rope[1]]
        out_specs, out_shape = o_spec, o_shape
    else:
        out_specs = [o_spec, pl.BlockSpec((seq, 256), lambda b, i: (b, 0))]
        out_shape = [o_shape, jax.ShapeDtypeStruct((m, 256), F32)]
    n_keys = seq + (PAST_LEN if sample else 0)
    return pl.pallas_call(
        functools.partial(_gqa_kernel, sample=sample),
        grid=(n_batch, nq),
        in_specs=in_specs,
        out_specs=out_specs,
        out_shape=out_shape,
        scratch_shapes=[pltpu.VMEM((n_keys, 256), BF16), pltpu.VMEM((n_keys, 256), BF16)],
        compiler_params=_params(2),
        name="gqa_sample" if sample else "gqa_prompt",
    )(*args)


def _mla_kernel(*refs, sample):
    if sample:
        (cq_ref, ckv_ref, kx_ref, gq_ref, wqb_ref, gkv_ref, wkvb_ref, cckv_ref, ckpe_ref, ex_ref, c4_ref, s4_ref,
         o_ref, kv_s, kx_s) = refs
    else:
        cq_ref, ckv_ref, kx_ref, gq_ref, wqb_ref, gkv_ref, wkvb_ref, o_ref, ckvn_ref, kv_s, kx_s = refs
    qi = pl.program_id(1)
    n_new = ckv_ref.shape[0]
    past = PAST_LEN if sample else 0

    def rope(x, c, s):
        x1, x2 = x[:, :LANES], x[:, LANES:]
        return jnp.concatenate([x1 * c - x2 * s, x1 * s + x2 * c], axis=1)

    @pl.when(qi == 0)
    def _():
        wkvb = wkvb_ref[...].astype(BF16)
        ckvn = _rms(ckv_ref[...]) * gkv_ref[...]
        if not sample:
            ckvn_ref[...] = ckvn
        kv_s[past:past + n_new, :] = _mm(ckvn.astype(BF16), wkvb).astype(BF16)
        kx = kx_ref[...]
        if sample:
            kv_s[0:past, :] = _mm(cckv_ref[...].astype(BF16), wkvb).astype(BF16)
            kx_s[0:past, :] = _mm(ckpe_ref[...].astype(BF16), ex_ref[...]).astype(BF16)
            kx = rope(kx, c4_ref[...], s4_ref[...])
        kx_s[past:past + n_new, :] = kx.astype(BF16)

    q = _mm((_rms(cq_ref[...]) * gq_ref[...]).astype(BF16), wqb_ref[...].astype(BF16))
    qpe = q[:, 4 * NOPE_D:]
    if sample:
        r0 = pl.multiple_of(qi * QB, QB)
        qpe = rope(qpe, c4_ref[pl.ds(r0, QB), :], s4_ref[pl.ds(r0, QB), :])
    lane_head = (lax.broadcasted_iota(jnp.int32, (1, 2 * LANES), 1) % LANES) // (ROPE_D // 2)
    scale = (NOPE_D + ROPE_D) ** -0.5
    for h in range(H_D):
        qm = jnp.where(lane_head == h, qpe, 0.0)
        s = _nt(q[:, NOPE_D * h:NOPE_D * (h + 1)].astype(BF16), kv_s[:, 256 * h:256 * h + NOPE_D])
        s = (s + _nt(qm.astype(BF16), kx_s[...])) * scale
        o_ref[:, V_D * h:V_D * (h + 1)] = _softmax_pv(s, kv_s[:, 256 * h + NOPE_D:256 * (h + 1)]).astype(o_ref.dtype)


def _mla(proj, g_q, w_qb, g_kv, w_kvb, n_batch, seq, ctx=None, rope=None, expand=None):
    sample = ctx is not None
    m = n_batch * seq
    nq = seq // QB
    in_specs = [
        pl.BlockSpec((QB, Q_RANK), lambda b, i: (b * nq + i, 6)),
        pl.BlockSpec((seq, KV_RANK), lambda b, i: (b, 14)),
        pl.BlockSpec((seq, 256), lambda b, i: (b, 8)),
        pl.BlockSpec((1, Q_RANK), lambda b, i: (0, 0)),
        pl.BlockSpec((Q_RANK, 768), lambda b, i: (0, 0)),
        pl.BlockSpec((1, KV_RANK), lambda b, i: (0, 0)),
        pl.BlockSpec((KV_RANK, 1024), lambda b, i: (0, 0)),
    ]
    args = [proj, proj, proj, g_q.reshape(1, Q_RANK), w_qb, g_kv.reshape(1, KV_RANK), w_kvb]
    o_spec = pl.BlockSpec((QB, 512), lambda b, i: (b * nq + i, 0))
    o_shape = jax.ShapeDtypeStruct((m, 512), BF16)
    if sample:
        in_specs += [
            pl.BlockSpec((None, PAST_LEN, KV_RANK), lambda b, i: (b, 0, 0)),
            pl.BlockSpec((None, PAST_LEN, ROPE_D), lambda b, i: (b, 0, 0)),
            pl.BlockSpec((ROPE_D, 256), lambda b, i: (0, 0)),
            pl.BlockSpec((seq, LANES), lambda b, i: (0, 0)),
            pl.BlockSpec((seq, LANES), lambda b, i: (0, 0)),
        ]
        args += [ctx[0], ctx[1], expand, rope[0], rope[1]]
        out_specs, out_shape = o_spec, o_shape
    else:
        out_specs = [o_spec, pl.BlockSpec((seq, KV_RANK), lambda b, i: (b, 0))]
        out_shape = [o_shape, jax.ShapeDtypeStruct((m, KV_RANK), F32)]
    n_keys = seq + (PAST_LEN if sample else 0)
    return pl.pallas_call(
        functools.partial(_mla_kernel, sample=sample),
        grid=(n_batch, nq),
        in_specs=in_specs,
        out_specs=out_specs,
        out_shape=out_shape,
        scratch_shapes=[pltpu.VMEM((n_keys, 1024), BF16), pltpu.VMEM((n_keys, 256), BF16)],
        compiler_params=_params(2),
        name="mla_sample" if sample else "mla_prompt",
    )(*args)


def _split_bf16(x):
    hi = x.astype(BF16)
    return hi, (x - hi.astype(F32)).astype(BF16)


def _dft(t_hi, t_lo, x):
    x_hi, x_lo = _split_bf16(x)
    return _mm(t_hi, x_hi) + _mm(t_hi, x_lo) + _mm(t_lo, x_hi)


def _filter_kernel(z_ref, wf1_ref, bf1_ref, fr_ref, wf2_ref, bf2_ref, wf3_ref, t_ref, dl_ref,
                   ch_ref, cl_ref, sh_ref, sl_ref, gre_ref, gim_ref):
    n_tok = z_ref.shape[0]
    fr = fr_ref[...]
    hid = jnp.sin(fr * (_mm(z_ref[...].astype(BF16), wf1_ref[...].astype(BF16)) + bf1_ref[...]))
    hid = jnp.sin(fr * (_mm(hid.astype(BF16), wf2_ref[...].astype(BF16)) + bf2_ref[...]))
    filt = _mm(hid.astype(BF16), wf3_ref[...].astype(BF16))
    decay = jnp.exp(-t_ref[...] * dl_ref[...])
    row = lax.broadcasted_iota(jnp.int32, (n_tok, 1), 0)
    h_f = filt[:, :HY_W] * decay
    h_b = jnp.where(row == 0, 0.0, filt[:, HY_W:] * decay)
    p, m = h_f + h_b, h_f - h_b
    g_re = _dft(ch_ref[...], cl_ref[...], p)
    g_im = _dft(sh_ref[...], sl_ref[...], m)
    sign = jnp.where(row % 2 == 0, 1.0, -1.0)
    nyquist = jnp.sum(p * sign, axis=0, keepdims=True)
    g_im = jnp.where(row == 0, nyquist, g_im)
    wk = jnp.where(row == 0, 0.5 / n_tok, 1.0 / n_tok)
    gre_ref[...] = g_re * wk
    gim_ref[...] = g_im * wk


def _filter_spectrum(z, wf1, bf1, freq, wf2, bf2, wf3, t_col, deltas, tabs):
    n_tok = z.shape[0]
    out = jax.ShapeDtypeStruct((n_tok, HY_W), F32)
    return pl.pallas_call(
        _filter_kernel,
        out_shape=[out, out],
        compiler_params=pltpu.CompilerParams(vmem_limit_bytes=VMEM_LIMIT),
        name="hyena_filter",
    )(z, wf1, bf1.reshape(1, FILT_HID), freq.reshape(1, FILT_HID), wf2, bf2.reshape(1, FILT_HID), wf3,
      t_col, deltas, tabs[0], tabs[1], tabs[2], tabs[3])


HY_CT = 256


def _hyena_kernel(u0_ref, u1_ref, u2_ref, w0_ref, w1_ref, w2_ref, b0_ref, b1_ref, b2_ref, skip_ref,
                  gre_ref, gim_ref, ch_ref, cl_ref, sh_ref, sl_ref, th_ref, tl_ref, o_ref):
    n_tok = u0_ref.shape[0]
    row = lax.broadcasted_iota(jnp.int32, (n_tok, 1), 0)

    def short_conv(u_ref, w_ref, b_ref):
        x, w = u_ref[...], w_ref[...]
        prev = jnp.where(row == 0, 0.0, pltpu.roll(x, 1, 0))
        nxt = jnp.where(row == n_tok - 1, 0.0, pltpu.roll(x, n_tok - 1, 0))
        return prev * w[0:1] + x * w[1:2] + nxt * w[2:3] + b_ref[...]

    x0 = short_conv(u0_ref, w0_ref, b0_ref)
    gv = short_conv(u1_ref, w1_ref, b1_ref) * short_conv(u2_ref, w2_ref, b2_ref)
    u_re = _dft(ch_ref[...], cl_ref[...], gv)
    u_im = _dft(sh_ref[...], sl_ref[...], gv)
    g_re, g_im = gre_ref[...], gim_ref[...]
    p_im = u_im * g_im
    y_re = u_re * g_re - jnp.where(row == 0, 0.0, p_im)
    y_im = jnp.where(row == 0, p_im, u_re * g_im + u_im * g_re)
    y = _dft(ch_ref[...], cl_ref[...], y_re) + _dft(th_ref[...], tl_ref[...], y_im)
    o_ref[...] = (x0 * (y + gv * skip_ref[...])).astype(o_ref.dtype)


def _hyena(proj, w_conv, b_conv, skip, g_re, g_im, tabs, n_batch, seq):
    nct = HY_W // HY_CT
    u_specs = [pl.BlockSpec((seq, HY_CT), functools.partial(lambda b, c, g: (b, g * nct + c), g=g)) for g in range(3)]
    w_specs = [pl.BlockSpec((3, HY_CT), functools.partial(lambda b, c, g: (0, g * nct + c), g=g)) for g in range(3)]
    b_specs = [pl.BlockSpec((1, HY_CT), functools.partial(lambda b, c, g: (0, g * nct + c), g=g)) for g in range(3)]
    tab_spec = pl.BlockSpec((seq, seq), lambda b, c: (0, 0))
    return pl.pallas_call(
        _hyena_kernel,
        grid=(n_batch, nct),
        in_specs=u_specs + w_specs + b_specs + [
            pl.BlockSpec((1, HY_CT), lambda b, c: (0, c)),
            pl.BlockSpec((seq, HY_CT), lambda b, c: (0, c)),
            pl.BlockSpec((seq, HY_CT), lambda b, c: (0, c)),
        ] + [tab_spec] * 6,
        out_specs=pl.BlockSpec((seq, HY_CT), lambda b, c: (b, c)),
        out_shape=jax.ShapeDtypeStruct((n_batch * seq, HY_W), BF16),
        compiler_params=_params(2),
        name="hyena_conv",
    )(proj, proj, proj, w_conv, w_conv, w_conv, b_conv, b_conv, b_conv, skip.reshape(1, HY_W), g_re, g_im, *tabs)


def _dft_tables(n_tok):
    k = np.arange(n_tok)[:, None]
    s = np.arange(n_tok)[None, :]
    ang = ((k * s) % (2 * n_tok)) * (np.pi / n_tok)
    cos_t = np.cos(ang)
    sin_f = np.where(k == 0, np.where(s % 2 == 0, 1.0, -1.0), -np.sin(ang))
    out = []
    for t in (cos_t, sin_f, sin_f.T):
        hi = t.astype(BF16)
        out += [jnp.asarray(hi), jnp.asarray((t - hi.astype(np.float64)).astype(BF16))]
    return out


GLA_LEVELS = (32, 16, 8, 4, 2, 1)


def _gla_constants():
    c = GLA_CHUNK
    idx = np.arange(c)
    i, t = idx[:, None], idx[None, :]
    masks = []
    for s in GLA_LEVELS:
        upper = (idx % (2 * s)) >= s
        masks.append(((i // (2 * s)) == (t // (2 * s))) & upper[:, None] & (~upper)[None, :])
    masks.append(i == t)
    tri = t <= i
    fwd_m = np.stack([np.tile(m, (H_A, 1)) for m in masks]).astype(np.float32)
    bwd_m = np.stack([np.tile(m[::-1, ::-1], (H_A, 1)) for m in masks]).astype(np.float32)
    head_of_row = np.repeat(np.arange(H_A), c)[:, None]
    head_of_lane = np.repeat(np.arange(H_A), DK_A)[None, :]
    head_mask = head_of_row == head_of_lane
    return (jnp.asarray(tri, BF16), jnp.asarray(tri[::-1, ::-1], BF16), jnp.asarray(fwd_m), jnp.asarray(bwd_m),
            jnp.asarray(head_mask, BF16))


def _pair_reference(b, s, backward, row):
    c = GLA_CHUNK
    ref = s if backward else s - 1
    if 2 * s >= 8:
        pieces = [jnp.broadcast_to(b[p * 2 * s + ref:p * 2 * s + ref + 1, :], (2 * s, b.shape[1]))
                  for p in range(c // (2 * s))]
        return pieces[0] if len(pieces) == 1 else jnp.concatenate(pieces, axis=0)
    pos = row % (2 * s)
    out = None
    for o in range(2 * s):
        d = ref - o
        shifted = b if d == 0 else pltpu.roll(b, (-d) % c, 0)
        out = shifted if out is None else jnp.where(pos == o, shifted, out)
    return out


def _gla_chunk(q, k, v, la, t_ref, m_ref, hm, s_ref, backward):
    c = GLA_CHUNK
    l1 = la.astype(BF16)
    r1 = la - l1.astype(F32)
    l2 = r1.astype(BF16)
    l3 = (r1 - l2.astype(F32)).astype(BF16)
    tmat = t_ref[...]
    b = _mm(tmat, l1) + _mm(tmat, l2) + _mm(tmat, l3)
    row = lax.broadcasted_iota(jnp.int32, (c, 1), 0)
    last = 0 if backward else c - 1
    b_last = b[last:last + 1, :]

    def stack_heads(a):
        ab = a.astype(BF16)
        return jnp.concatenate([ab] * H_A, axis=0) * hm

    scores = _nt(stack_heads(q), k.astype(BF16)) * m_ref[len(GLA_LEVELS)]
    for lvl, s in enumerate(GLA_LEVELS):
        is_query = (row % (2 * s) < s) if backward else (row % (2 * s) >= s)
        delta = b - _pair_reference(b, s, backward, row)
        x = jnp.exp(jnp.where(is_query, delta, -delta))
        scores = scores + _nt(stack_heads(q * x), (k * x).astype(BF16)) * m_ref[lvl]
    scores = scores.astype(BF16)
    state = s_ref[...]
    inter = _mm(stack_heads(q * jnp.exp(b)), state.astype(BF16))
    k_rest = (k * jnp.exp(b_last - b)).T
    carry = jnp.broadcast_to(jnp.exp(b_last), (2 * c, b.shape[1])).T
    outs = []
    for h in range(H_A):
        rows = slice(c * h, c * (h + 1))
        v_h = v[:, DV_A * h:DV_A * (h + 1)].astype(BF16)
        outs.append(_mm(scores[rows], v_h) + inter[rows])
        s_ref[rows, :] = state[rows] * carry[rows] + _mm(k_rest[rows].astype(BF16), v_h)
    return jnp.concatenate(outs, axis=1)


def _gla_kernel(*refs, sample):
    if sample:
        (x_ref, z_ref, wf_ref, bf_ref, wb_ref, bb_ref, tf_ref, tb_ref, mf_ref, mb_ref, hm_ref, gn_ref, sf0_ref, sb0_ref,
         o_ref, la_f, la_b, o_f, o_b, s_f, s_b) = refs
    else:
        (x_ref, z_ref, wf_ref, bf_ref, wb_ref, bb_ref, tf_ref, tb_ref, mf_ref, mb_ref, hm_ref, gn_ref,
         o_ref, sf_out, sb_out, la_f, la_b, o_f, o_b, s_f, s_b) = refs
    n_tok = x_ref.shape[0]
    n_chunks = n_tok // GLA_CHUNK
    hk, hv = H_A * DK_A, H_A * DV_A
    zb = z_ref[...].astype(BF16)
    la_f[...] = jax.nn.log_sigmoid(_mm(zb, wf_ref[...].astype(BF16)) + bf_ref[...]) / GLA_TAU
    la_b[...] = jax.nn.log_sigmoid(_mm(zb, wb_ref[...].astype(BF16)) + bb_ref[...]) / GLA_TAU
    if sample:
        s_f[...] = sf0_ref[...]
        s_b[...] = sb0_ref[...]
    else:
        s_f[...] = jnp.zeros_like(s_f)
        s_b[...] = jnp.zeros_like(s_b)
    hm = hm_ref[...]

    def step(ci, carry):
        for la_ref, t_ref, m_ref, s_ref, out_ref, cidx, backward in (
                (la_f, tf_ref, mf_ref, s_f, o_f, ci, False), (la_b, tb_ref, mb_ref, s_b, o_b, n_chunks - 1 - ci, True)):
            rows = pl.ds(pl.multiple_of(cidx * GLA_CHUNK, GLA_CHUNK), GLA_CHUNK)
            q = x_ref[rows, 0:hk] * (DK_A ** -0.5)
            k = x_ref[rows, hk:2 * hk]
            v = x_ref[rows, 2 * hk:2 * hk + hv]
            out_ref[rows, :] = _gla_chunk(q, k, v, la_ref[rows, :], t_ref, m_ref, hm, s_ref, backward)
        return carry

    lax.fori_loop(0, n_chunks, step, 0)
    if not sample:
        sf_out[...] = s_f[...]
        sb_out[...] = s_b[...]
    gain = gn_ref[...]
    for h in range(H_A):
        cols = slice(DV_A * h, DV_A * (h + 1))
        r = x_ref[:, 2 * hk + hv + DV_A * h:2 * hk + hv + DV_A * (h + 1)]
        o_ref[:, cols] = (_rms(o_f[:, cols] + o_b[:, cols]) * gain * (r * jax.nn.sigmoid(r))).astype(o_ref.dtype)


def _gla(proj, w_gf, b_gf, w_gb, b_gb, g_norm, consts, n_batch, seq, ctx=None):
    sample = ctx is not None
    hk, hv = H_A * DK_A, H_A * DV_A
    full = lambda shape: pl.BlockSpec(shape, lambda b: (0,) * len(shape))
    in_specs = [
        pl.BlockSpec((seq, 2 * hk + 2 * hv), lambda b: (b, 0)),
        pl.BlockSpec((seq, LANES), lambda b: (b, EVEN_W // LANES - 1)),
        full((LANES, hk)), full((1, hk)), full((LANES, hk)), full((1, hk)),
        full(consts[0].shape), full(consts[1].shape), full(consts[2].shape), full(consts[3].shape), full(consts[4].shape),
        full((1, DV_A)),
    ]
    args = [proj, proj, w_gf, b_gf.reshape(1, hk), w_gb, b_gb.reshape(1, hk), *consts, g_norm.reshape(1, DV_A)]
    o_spec = pl.BlockSpec((seq, hv), lambda b: (b, 0))
    o_shape = jax.ShapeDtypeStruct((n_batch * seq, hv), BF16)
    st_spec = pl.BlockSpec((None, hk, DV_A), lambda b: (b, 0, 0))
    if sample:
        in_specs += [st_spec, st_spec]
        args += [ctx[0], ctx[1]]
        out_specs, out_shape = o_spec, o_shape
    else:
        st_shape = jax.ShapeDtypeStruct((n_batch, hk, DV_A), F32)
        out_specs, out_shape = [o_spec, st_spec, st_spec], [o_shape, st_shape, st_shape]
    return pl.pallas_call(
        functools.partial(_gla_kernel, sample=sample),
        grid=(n_batch,),
        in_specs=in_specs,
        out_specs=out_specs,
        out_shape=out_shape,
        scratch_shapes=[pltpu.VMEM((seq, hk), F32), pltpu.VMEM((seq, hk), F32),
                        pltpu.VMEM((seq, hv), F32), pltpu.VMEM((seq, hv), F32),
                        pltpu.VMEM((hk, DV_A), F32), pltpu.VMEM((hk, DV_A), F32)],
        compiler_params=_params(1),
        name="gla_sample" if sample else "gla_prompt",
    )(*args)


def _axial_rope(n_tokens, dim):
    rows = n_tokens // GRID_W
    row = np.repeat(np.arange(rows), GRID_W).astype(np.float64)
    col = np.tile(np.arange(GRID_W), rows).astype(np.float64)
    n_freq = dim // 4
    inv = ROPE_THETA ** (-np.arange(n_freq) / n_freq)
    ang = np.concatenate([row[:, None] * inv, col[:, None] * inv], axis=-1)
    return np.cos(ang).astype(np.float32), np.sin(ang).astype(np.float32)


def _filter_features(n_tokens):
    t = np.linspace(0.0, 1.0, n_tokens)[:, None]
    w = 2.0 * np.pi * np.arange(n_tokens)[:, None] / n_tokens
    f = np.linspace(1e-4, FILT_BANDS - 1, FILT_BANDS)[None, :]
    z = np.concatenate([t, np.cos(f * w), -np.sin(f * w)], axis=-1)
    z = np.pad(z, ((0, 0), (0, LANES - FILT_EMB)))
    return jnp.asarray(z, F32), jnp.asarray(t, F32)


_KPE_EXPAND = np.array([(p // LANES) * (ROPE_D // 2) + p % (ROPE_D // 2) for p in range(2 * LANES)])
_QB_PERM = np.array(
    [192 * (p // NOPE_D) + p % NOPE_D for p in range(H_D * NOPE_D)]
    + [192 * (p // 32) + NOPE_D + p % 32 for p in range(H_D * 32)]
    + [192 * (p // 32) + NOPE_D + 32 + p % 32 for p in range(H_D * 32)])


def _even_in_weight(w):
    pad = jnp.zeros(w.shape[:2] + (96,), F32)
    return jnp.concatenate([w[:, :, :1536], w[:, :, 1568:2592], w[:, :, 1536:1568], pad], axis=2)


def _odd_in_weight(w):
    pad = jnp.zeros(w.shape[:2] + (64,), F32)
    return jnp.concatenate([w[:, :, :1984], pad, w[:, :, 1920 + _KPE_EXPAND]], axis=2)


def kernel(x_prompt, x_sample, state_gla_fwd, state_gla_bwd, cache_gqa_k, cache_gqa_v, cache_mla_ckv, cache_mla_kpe, c, c_ctx, w_mod, b_mod, w_in_even, w_gla_gate_f, b_gla_gate_f, w_gla_gate_b, b_gla_gate_b, g_gla_norm, g_gqa_q, g_gqa_k, w_out_even, w_in_odd, w_hy_conv, b_hy_conv, hy_skip, w_filt1, b_filt1, filt_freq, w_filt2, b_filt2, w_filt3, g_mla_q, w_mla_qb, g_mla_kv, w_mla_kvb, w_out_odd, w_ffn_in, w_ffn_out, g_final):
    n_c, n_s = BATCH * SEQ, DEC_BATCH * DEC_SEQ
    cvec = jnp.concatenate([c_ctx[None, :], c, jnp.zeros((8 - 1 - DEC_BATCH, D_MODEL), F32)], axis=0)
    mod = _modulation(cvec, w_mod, b_mod)
    xc = x_prompt.reshape(n_c, D_MODEL)
    xs = x_sample.reshape(n_s, D_MODEL)
    rows_c, rows_s = (0, 0), (1, DEC_SEQ // MOD_ROWS)

    gla_consts = _gla_constants()
    cos_b, sin_b = _axial_rope(DEC_SEQ, HD_B)
    rope_b = (jnp.asarray(np.concatenate([cos_b, cos_b], axis=1)), jnp.asarray(np.concatenate([-sin_b, sin_b], axis=1)))
    cos_d, sin_d = _axial_rope(DEC_SEQ, ROPE_D)
    rope_d = (jnp.asarray(np.tile(cos_d, (1, H_D))), jnp.asarray(np.tile(sin_d, (1, H_D))))
    kpe_expand = jnp.asarray(np.arange(ROPE_D)[:, None] == _KPE_EXPAND[None, :], BF16)
    tabs_c, tabs_s = _dft_tables(SEQ), _dft_tables(DEC_SEQ)
    z_c, t_c = _filter_features(SEQ)
    z_s, t_s = _filter_features(DEC_SEQ)
    deltas = jnp.asarray(np.abs(np.linspace(HY_MIN_DECAY, HY_MAX_DECAY, HY_W))[None, :], F32)
    w_even_all = _even_in_weight(w_in_even)
    w_odd_all = _odd_in_weight(w_in_odd)

    st_gf, st_gb, st_k, st_v, st_ckv, st_kpe = [], [], [], [], [], []
    for i in range(DEPTH):
        j = i // 2
        if i % 2 == 0:
            pad_f = jnp.zeros((LANES, H_A * DK_A), F32).at[0:GATE_RANK].set(w_gla_gate_f[j])
            pad_b = jnp.zeros((LANES, H_A * DK_A), F32).at[GATE_RANK:2 * GATE_RANK].set(w_gla_gate_b[j])
            pc = _in_proj(xc, mod, i, w_even_all, j, EVEN_TN, *rows_c)
            ps = _in_proj(xs, mod, i, w_even_all, j, EVEN_TN, *rows_s)
            gate_args = (pad_f, b_gla_gate_f[j], pad_b, b_gla_gate_b[j], g_gla_norm[j], gla_consts)
            a_c, s_f, s_b = _gla(pc, *gate_args, BATCH, SEQ)
            ctx_a = (state_gla_fwd[:, j].reshape(DEC_BATCH, H_A * DK_A, DV_A),
                     state_gla_bwd[:, j].reshape(DEC_BATCH, H_A * DK_A, DV_A))
            a_s = _gla(ps, *gate_args, DEC_BATCH, DEC_SEQ, ctx=ctx_a)
            b_c, k_norm = _gqa(pc, g_gqa_q[j], g_gqa_k[j], BATCH, SEQ)
            ctx_b = (cache_gqa_k[:, j].reshape(DEC_BATCH, PAST_LEN, KV_B * HD_B),
                     cache_gqa_v[:, j].reshape(DEC_BATCH, PAST_LEN, KV_B * HD_B))
            b_s = _gqa(ps, g_gqa_q[j], g_gqa_k[j], DEC_BATCH, DEC_SEQ, ctx=ctx_b, rope=rope_b)
            w_out = w_out_even
            st_gf.append(s_f.reshape(BATCH, H_A, DK_A, DV_A))
            st_gb.append(s_b.reshape(BATCH, H_A, DK_A, DV_A))
            st_k.append(k_norm.reshape(BATCH, SEQ, KV_B, HD_B))
            st_v.append(pc[:, 2304:2560].reshape(BATCH, SEQ, KV_B, HD_B))
        else:
            pc = _in_proj(xc, mod, i, w_odd_all, j, ODD_TN, *rows_c)
            ps = _in_proj(xs, mod, i, w_odd_all, j, ODD_TN, *rows_s)
            wf1 = jnp.pad(w_filt1[j], ((0, LANES - FILT_EMB), (0, 0)))
            filt_args = (wf1, b_filt1[j], filt_freq[j], w_filt2[j], b_filt2[j], w_filt3[j])
            g_c = _filter_spectrum(z_c, *filt_args, t_c, deltas, tabs_c)
            g_s = _filter_spectrum(z_s, *filt_args, t_s, deltas, tabs_s)
            b_conv = b_hy_conv[j].reshape(1, 3 * HY_W)
            a_c = _hyena(pc, w_hy_conv[j], b_conv, hy_skip[j], g_c[0], g_c[1], tabs_c, BATCH, SEQ)
            a_s = _hyena(ps, w_hy_conv[j], b_conv, hy_skip[j], g_s[0], g_s[1], tabs_s, DEC_BATCH, DEC_SEQ)
            w_qb = w_mla_qb[j][:, _QB_PERM]
            b_c, ckv_norm = _mla(pc, g_mla_q[j], w_qb, g_mla_kv[j], w_mla_kvb[j], BATCH, SEQ)
            b_s = _mla(ps, g_mla_q[j], w_qb, g_mla_kv[j], w_mla_kvb[j], DEC_BATCH, DEC_SEQ,
                       ctx=(cache_mla_ckv[:, j], cache_mla_kpe[:, j]), rope=rope_d, expand=kpe_expand)
            w_out = w_out_odd
            st_ckv.append(ckv_norm.reshape(BATCH, SEQ, KV_RANK))
            st_kpe.append(pc[:, 1920:1984].reshape(BATCH, SEQ, ROPE_D))
        xc = _out_proj([a_c, b_c], w_out, j, xc, mod, i, 2, *rows_c)
        xs = _out_proj([a_s, b_s], w_out, j, xs, mod, i, 2, *rows_s)
        f_c = _ffn_in(xc, mod, i, w_ffn_in, *rows_c)
        f_s = _ffn_in(xs, mod, i, w_ffn_in, *rows_s)
        xc = _out_proj([f_c], w_ffn_out, i, xc, mod, i, 5, *rows_c)
        xs = _out_proj([f_s], w_ffn_out, i, xs, mod, i, 5, *rows_s)
    y_prompt = _final_norm(xc, g_final).reshape(BATCH, SEQ, D_MODEL)
    y_sample = _final_norm(xs, g_final).reshape(DEC_BATCH, DEC_SEQ, D_MODEL)
    return (y_prompt, y_sample, jnp.stack(st_gf, axis=1), jnp.stack(st_gb, axis=1), jnp.stack(st_k, axis=1),
            jnp.stack(st_v, axis=1), jnp.stack(st_ckv, axis=1), jnp.stack(st_kpe, axis=1))
```

```python
import functools
import math

import numpy as np
import jax
import jax.numpy as jnp
from jax import lax
from jax.experimental import pallas as pl
from jax.experimental.pallas import tpu as pltpu

F32 = jnp.float32
BF16 = jnp.bfloat16

D_MODEL = 1024
BATCH, SEQ = 16, 256
DEC_BATCH, DEC_SEQ = 2, 1024
DEPTH = 4
PAST_LEN = 512
GRID_W = 64
HALF_W = D_MODEL // 2
H_A, DV_A, DK_A = 4, 128, 64
GATE_RANK = 16
GLA_TAU = 16.0
GLA_CHUNK = 64
HD_B, H_B, KV_B = 128, 4, 2
HY_W = HALF_W
FILT_EMB, FILT_HID = 33, 64
FILT_BANDS = (FILT_EMB - 1) // 2
HY_MIN_DECAY = math.log(1e-2) / 1.5
HY_MAX_DECAY = math.log(1e-2) / 0.3
H_D, V_D, NOPE_D, ROPE_D = 4, 128, 128, 64
Q_RANK, KV_RANK = 256, 128
FFN_H = 2816
ROPE_THETA = 10000.0
EPS = 1e-6

LANES = 128
VMEM_LIMIT = 56 * 1024 * 1024

MOD_ROWS = 1024
TM = 1024
TM_IN = 512
TM_FFN = 2048
EVEN_W = 2688
ODD_W = 2048
FFN_TN = 256
QB = 256


def _params(n_grid):
    return pltpu.CompilerParams(dimension_semantics=("arbitrary",) * n_grid, vmem_limit_bytes=VMEM_LIMIT)


def _nt(a, b):
    return lax.dot_general(a, b, (((1,), (1,)), ((), ())), preferred_element_type=F32)


def _mm(a, b):
    return jnp.dot(a, b, preferred_element_type=F32)


def _rms(x):
    return x * lax.rsqrt(jnp.mean(x * x, axis=-1, keepdims=True) + EPS)


def _mod_kernel(c_ref, w_ref, b_ref, o_ref):
    cv = c_ref[...]
    s = cv * jax.nn.sigmoid(cv)
    o_ref[...] = _mm(s.astype(BF16), w_ref[...].astype(BF16)) + b_ref[...]


def _modulation(cvec, w_mod, b_mod):
    return pl.pallas_call(
        _mod_kernel,
        grid=(DEPTH, 6),
        in_specs=[
            pl.BlockSpec((8, D_MODEL), lambda l, n: (0, 0)),
            pl.BlockSpec((None, D_MODEL, D_MODEL), lambda l, n: (l, 0, n)),
            pl.BlockSpec((None, 1, D_MODEL), lambda l, n: (l, 0, n)),
        ],
        out_specs=pl.BlockSpec((None, None, 8, D_MODEL), lambda l, n: (l, n, 0, 0)),
        out_shape=jax.ShapeDtypeStruct((DEPTH, 6, 8, D_MODEL), F32),
        compiler_params=_params(2),
        name="adaln_mod",
    )(cvec, w_mod, b_mod.reshape(DEPTH, 1, 6 * D_MODEL))


def _mod_row(row0, rstep, tile_rows, sub):
    if tile_rows >= MOD_ROWS:
        return row0 + rstep * (pl.program_id(0) * (tile_rows // MOD_ROWS) + sub)
    return row0 + rstep * (pl.program_id(0) // (MOD_ROWS // tile_rows))


def _in_proj_kernel(x_ref, sh_ref, sc_ref, *refs, n_w, rotate, row0, rstep):
    w_refs, o_ref, wb_ref = refs[:n_w], refs[n_w], refs[n_w + 1]
    widths = [w.shape[1] for w in w_refs]
    starts = [sum(widths[:p]) for p in range(n_w)]

    @pl.when(pl.program_id(0) == 0)
    def _():
        for w_ref, c0, wd in zip(w_refs, starts, widths):
            wb_ref[:, c0:c0 + wd] = w_ref[...].astype(BF16)

    g = _mod_row(row0, rstep, x_ref.shape[0], 0)
    h = (_rms(x_ref[...]) * (1.0 + sc_ref[pl.ds(g, 1), :]) + sh_ref[pl.ds(g, 1), :]).astype(BF16)
    split = starts[1] if rotate else sum(widths)
    o_ref[:, 0:split] = _mm(h, wb_ref[:, 0:split])
    if rotate:
        tail = _mm(h, wb_ref[:, split:])
        o_ref[:, split:] = pltpu.roll(tail, tail.shape[1] - rotate, 1)


def _in_proj(x, mod, layer, w, w_layer, col_blocks, rotate, row0, rstep):
    m = x.shape[0]
    n = sum(wd for wd, _ in col_blocks)
    w_specs = [pl.BlockSpec((None, D_MODEL, wd), functools.partial(lambda i, b: (w_layer, 0, b), b=b),
                            pipeline_mode=pl.Buffered(1)) for wd, b in col_blocks]
    return pl.pallas_call(
        functools.partial(_in_proj_kernel, n_w=len(col_blocks), rotate=rotate, row0=row0, rstep=rstep),
        grid=(m // TM_IN,),
        in_specs=[pl.BlockSpec((TM_IN, D_MODEL), lambda i: (i, 0)),
                  pl.BlockSpec((None, None, 8, D_MODEL), lambda i: (layer, 0, 0, 0)),
                  pl.BlockSpec((None, None, 8, D_MODEL), lambda i: (layer, 1, 0, 0))] + w_specs,
        out_specs=pl.BlockSpec((TM_IN, n), lambda i: (i, 0)),
        out_shape=jax.ShapeDtypeStruct((m, n), F32),
        scratch_shapes=[pltpu.VMEM((D_MODEL, n), BF16)],
        compiler_params=_params(1),
        name="norm_mod_proj",
    )(x, mod, mod, *([w] * len(col_blocks)))


def _ffn_kernel(x_ref, sh_ref, sc_ref, gate_ref, wg_ref, wu_ref, wd_ref, o_ref, h_ref, *, row0, rstep):
    n_sub = x_ref.shape[0] // MOD_ROWS
    subs = [(slice(s * MOD_ROWS, (s + 1) * MOD_ROWS), _mod_row(row0, rstep, x_ref.shape[0], s)) for s in range(n_sub)]

    @pl.when(pl.program_id(1) == 0)
    def _():
        for rows, g in subs:
            x = x_ref[rows, :]
            o_ref[rows, :] = x
            h_ref[rows, :] = (_rms(x) * (1.0 + sc_ref[pl.ds(g, 1), :]) + sh_ref[pl.ds(g, 1), :]).astype(BF16)

    wg = wg_ref[...].astype(BF16)
    wu = wu_ref[...].astype(BF16)
    wd = wd_ref[...].astype(BF16)
    for rows, g in subs:
        h = h_ref[rows, :]
        a = _mm(h, wg)
        act = (a * jax.nn.sigmoid(a) * _mm(h, wu)).astype(BF16)
        o_ref[rows, :] += gate_ref[pl.ds(g, 1), :] * _mm(act, wd)


def _ffn(x, mod, layer, w_in, w_out, row0, rstep):
    m = x.shape[0]
    nj = FFN_H // FFN_TN
    mod_spec = lambda k: pl.BlockSpec((None, None, 8, D_MODEL), lambda i, j: (layer, k, 0, 0))
    return pl.pallas_call(
        functools.partial(_ffn_kernel, row0=row0, rstep=rstep),
        grid=(m // TM_FFN, nj),
        in_specs=[pl.BlockSpec((TM_FFN, D_MODEL), lambda i, j: (i, 0)), mod_spec(3), mod_spec(4), mod_spec(5),
                  pl.BlockSpec((None, D_MODEL, FFN_TN), lambda i, j: (layer, 0, j)),
                  pl.BlockSpec((None, D_MODEL, FFN_TN), lambda i, j: (layer, 0, j + nj)),
                  pl.BlockSpec((None, FFN_TN, D_MODEL), lambda i, j: (layer, j, 0))],
        out_specs=pl.BlockSpec((TM_FFN, D_MODEL), lambda i, j: (i, 0)),
        out_shape=jax.ShapeDtypeStruct((m, D_MODEL), F32),
        scratch_shapes=[pltpu.VMEM((TM_FFN, D_MODEL), BF16)],
        compiler_params=_params(2),
        name="ffn_residual",
    )(x, mod, mod, mod, w_in, w_in, w_out)


def _proj_res_kernel(*refs, n_act, row0, rstep):
    acts, ws = refs[:n_act], refs[n_act:2 * n_act]
    x_ref, gate_ref, o_ref = refs[2 * n_act:]
    g = row0 + rstep * pl.program_id(0)
    acc = _mm(acts[0][...], ws[0][...].astype(BF16))
    for a_ref, w_ref in zip(acts[1:], ws[1:]):
        acc = acc + _mm(a_ref[...], w_ref[...].astype(BF16))
    o_ref[...] = x_ref[...] + gate_ref[pl.ds(g, 1), :] * acc


def _out_proj(acts, w, w_layer, x, mod, layer, k_gate, row0, rstep):
    m = x.shape[0]
    n_act = len(acts)
    kw = acts[0].shape[1]
    act_specs = [pl.BlockSpec((TM, kw), lambda i: (i, 0)) for _ in acts]
    w_specs = [pl.BlockSpec((None, kw, D_MODEL), functools.partial(lambda i, p: (w_layer, p, 0), p=p),
                            pipeline_mode=pl.Buffered(1)) for p in range(n_act)]
    return pl.pallas_call(
        functools.partial(_proj_res_kernel, n_act=n_act, row0=row0, rstep=rstep),
        grid=(m // TM,),
        in_specs=act_specs + w_specs + [
            pl.BlockSpec((TM, D_MODEL), lambda i: (i, 0)),
            pl.BlockSpec((None, None, 8, D_MODEL), lambda i: (layer, k_gate, 0, 0)),
        ],
        out_specs=pl.BlockSpec((TM, D_MODEL), lambda i: (i, 0)),
        out_shape=jax.ShapeDtypeStruct((m, D_MODEL), F32),
        compiler_params=_params(1),
        name="out_proj_residual",
    )(*acts, *([w] * n_act), x, mod)


def _final_kernel(x_ref, g_ref, o_ref):
    o_ref[...] = _rms(x_ref[...]) * g_ref[...]


def _final_norm(x, g):
    m = x.shape[0]
    return pl.pallas_call(
        _final_kernel,
        grid=(m // TM,),
        in_specs=[pl.BlockSpec((TM, D_MODEL), lambda i: (i, 0)), pl.BlockSpec((1, D_MODEL), lambda i: (0, 0))],
        out_specs=pl.BlockSpec((TM, D_MODEL), lambda i: (i, 0)),
        out_shape=jax.ShapeDtypeStruct((m, D_MODEL), F32),
        compiler_params=_params(1),
        name="final_norm",
    )(x, g.reshape(1, D_MODEL))


def _softmax_pv(s, v):
    m = jnp.max(s, axis=-1, keepdims=True)
    e = jnp.exp(s - m)
    l = jnp.sum(e, axis=-1, keepdims=True)
    return _mm(e.astype(BF16), v) / l


def _gqa_kernel(*refs, sample):
    if sample:
        q_ref, k_ref, v_ref, gq_ref, gk_ref, ck_ref, cv_ref, cos_ref, sin_ref, o_ref, kb_ref, vb_ref = refs
    else:
        q_ref, k_ref, v_ref, gq_ref, gk_ref, o_ref, kn_ref, kb_ref, vb_ref = refs
    qi = pl.program_id(1)
    n_new = k_ref.shape[0]
    past = PAST_LEN if sample else 0

    @pl.when(qi == 0)
    def _():
        for g in range(KV_B):
            sl = slice(HD_B * g, HD_B * (g + 1))
            kn = _rms(k_ref[:, sl]) * gk_ref[...]
            if sample:
                kb_ref[0:past, sl] = ck_ref[:, sl].astype(BF16)
                vb_ref[0:past, sl] = cv_ref[:, sl].astype(BF16)
                kn = kn * cos_ref[...] + pltpu.roll(kn, HD_B // 2, 1) * sin_ref[...]
            else:
                kn_ref[:, sl] = kn
            kb_ref[past:past + n_new, sl] = kn.astype(BF16)
            vb_ref[past:past + n_new, sl] = v_ref[:, sl].astype(BF16)

    r0 = pl.multiple_of(qi * QB, QB)
    for h in range(H_B):
        g = h // (H_B // KV_B)
        gs = slice(HD_B * g, HD_B * (g + 1))
        hs = slice(HD_B * h, HD_B * (h + 1))
        qn = _rms(q_ref[:, hs]) * gq_ref[...]
        if sample:
            qn = qn * cos_ref[pl.ds(r0, QB), :] + pltpu.roll(qn, HD_B // 2, 1) * sin_ref[pl.ds(r0, QB), :]
        s = _nt(qn.astype(BF16), kb_ref[:, gs]) * (HD_B ** -0.5)
        o_ref[:, hs] = _softmax_pv(s, vb_ref[:, gs]).astype(o_ref.dtype)


def _gqa(proj, g_q, g_k, n_batch, seq, ctx=None, rope=None):
    sample = ctx is not None
    m = n_batch * seq
    nq = seq // QB
    in_specs = [
        pl.BlockSpec((QB, 512), lambda b, i: (b * nq + i, 3)),
        pl.BlockSpec((seq, 256), lambda b, i: (b, 8)),
        pl.BlockSpec((seq, 256), lambda b, i: (b, 9)),
        pl.BlockSpec((1, HD_B), lambda b, i: (0, 0)),
        pl.BlockSpec((1, HD_B), lambda b, i: (0, 0)),
    ]
    args = [proj, proj, proj, g_q.reshape(1, HD_B), g_k.reshape(1, HD_B)]
    o_spec = pl.BlockSpec((QB, 512), lambda b, i: (b * nq + i, 0))
    o_shape = jax.ShapeDtypeStruct((m, 512), BF16)
    if sample:
        in_specs += [
            pl.BlockSpec((None, PAST_LEN, 256), lambda b, i: (b, 0, 0)),
            pl.BlockSpec((None, PAST_LEN, 256), lambda b, i: (b, 0, 0)),
            pl.BlockSpec((seq, HD_B), lambda b, i: (0, 0)),
            pl.BlockSpec((seq, HD_B), lambda b, i: (0, 0)),
        ]
        args += [ctx[0], ctx[1], rope[0], rope[1]]
        out_specs, out_shape = o_spec, o_shape
    else:
        out_specs = [o_spec, pl.BlockSpec((seq, 256), lambda b, i: (b, 0))]
        out_shape = [o_shape, jax.ShapeDtypeStruct((m, 256), F32)]
    n_keys = seq + (PAST_LEN if sample else 0)
    return pl.pallas_call(
        functools.partial(_gqa_kernel, sample=sample),
        grid=(n_batch, nq),
        in_specs=in_specs,
        out_specs=out_specs,
        out_shape=out_shape,
        scratch_shapes=[pltpu.VMEM((n_keys, 256), BF16), pltpu.VMEM((n_keys, 256), BF16)],
        compiler_params=_params(2),
        name="gqa_sample" if sample else "gqa_prompt",
    )(*args)


def _mla_kernel(*refs, sample):
    if sample:
        (cq_ref, ckv_ref, kpe_ref, gq_ref, wqb_ref, gkv_ref, wkvb_ref, ex_ref, cckv_ref, ckpe_ref, c4_ref, s4_ref,
         o_ref, kv_s, kx_s) = refs
    else:
        cq_ref, ckv_ref, kpe_ref, gq_ref, wqb_ref, gkv_ref, wkvb_ref, ex_ref, o_ref, ckvn_ref, kv_s, kx_s = refs
    qi = pl.program_id(1)
    n_new = ckv_ref.shape[0]
    past = PAST_LEN if sample else 0

    def rope(x, c, s):
        x1, x2 = x[:, :LANES], x[:, LANES:]
        return jnp.concatenate([x1 * c - x2 * s, x1 * s + x2 * c], axis=1)

    @pl.when(qi == 0)
    def _():
        wkvb = wkvb_ref[...].astype(BF16)
        ckvn = _rms(ckv_ref[...]) * gkv_ref[...]
        if not sample:
            ckvn_ref[...] = ckvn
        kv_s[past:past + n_new, :] = _mm(ckvn.astype(BF16), wkvb).astype(BF16)
        kpe = kpe_ref[:, 0:ROPE_D]
        if sample:
            kv_s[0:past, :] = _mm(cckv_ref[...].astype(BF16), wkvb).astype(BF16)
            kx_s[0:past, :] = _mm(ckpe_ref[...].astype(BF16), ex_ref[...]).astype(BF16)
            k1 = kpe.astype(BF16)
            r1 = kpe - k1.astype(F32)
            k2 = r1.astype(BF16)
            k3 = (r1 - k2.astype(F32)).astype(BF16)
            kx = _mm(k1, ex_ref[...]) + _mm(k2, ex_ref[...]) + _mm(k3, ex_ref[...])
            kx_s[past:past + n_new, :] = rope(kx, c4_ref[...], s4_ref[...]).astype(BF16)
        else:
            kx_s[...] = _mm(kpe.astype(BF16), ex_ref[...]).astype(BF16)

    q = _mm((_rms(cq_ref[...]) * gq_ref[...]).astype(BF16), wqb_ref[...].astype(BF16))
    qpe = q[:, 4 * NOPE_D:]
    if sample:
        r0 = pl.multiple_of(qi * QB, QB)
        qpe = rope(qpe, c4_ref[pl.ds(r0, QB), :], s4_ref[pl.ds(r0, QB), :])
    lane_head = (lax.broadcasted_iota(jnp.int32, (1, 2 * LANES), 1) % LANES) // (ROPE_D // 2)
    scale = (NOPE_D + ROPE_D) ** -0.5
    for h in range(H_D):
        qm = jnp.where(lane_head == h, qpe, 0.0)
        s = _nt(q[:, NOPE_D * h:NOPE_D * (h + 1)].astype(BF16), kv_s[:, 256 * h:256 * h + NOPE_D])
        s = (s + _nt(qm.astype(BF16), kx_s[...])) * scale
        o_ref[:, V_D * h:V_D * (h + 1)] = _softmax_pv(s, kv_s[:, 256 * h + NOPE_D:256 * (h + 1)]).astype(o_ref.dtype)


def _mla(proj, g_q, w_qb, g_kv, w_kvb, n_batch, seq, ctx=None, rope=None, expand=None):
    sample = ctx is not None
    m = n_batch * seq
    nq = seq // QB
    in_specs = [
        pl.BlockSpec((QB, Q_RANK), lambda b, i: (b * nq + i, 6)),
        pl.BlockSpec((seq, KV_RANK), lambda b, i: (b, 14)),
        pl.BlockSpec((seq, LANES), lambda b, i: (b, 15)),
        pl.BlockSpec((1, Q_RANK), lambda b, i: (0, 0)),
        pl.BlockSpec((Q_RANK, 768), lambda b, i: (0, 0)),
        pl.BlockSpec((1, KV_RANK), lambda b, i: (0, 0)),
        pl.BlockSpec((KV_RANK, 1024), lambda b, i: (0, 0)),
        pl.BlockSpec((ROPE_D, 256), lambda b, i: (0, 0)),
    ]
    args = [proj, proj, proj, g_q.reshape(1, Q_RANK), w_qb, g_kv.reshape(1, KV_RANK), w_kvb, expand]
    o_spec = pl.BlockSpec((QB, 512), lambda b, i: (b * nq + i, 0))
    o_shape = jax.ShapeDtypeStruct((m, 512), BF16)
    if sample:
        in_specs += [
            pl.BlockSpec((None, PAST_LEN, KV_RANK), lambda b, i: (b, 0, 0)),
            pl.BlockSpec((None, PAST_LEN, ROPE_D), lambda b, i: (b, 0, 0)),
            pl.BlockSpec((seq, LANES), lambda b, i: (0, 0)),
            pl.BlockSpec((seq, LANES), lambda b, i: (0, 0)),
        ]
        args += [ctx[0], ctx[1], rope[0], rope[1]]
        out_specs, out_shape = o_spec, o_shape
    else:
        out_specs = [o_spec, pl.BlockSpec((seq, KV_RANK), lambda b, i: (b, 0))]
        out_shape = [o_shape, jax.ShapeDtypeStruct((m, KV_RANK), F32)]
    n_keys = seq + (PAST_LEN if sample else 0)
    return pl.pallas_call(
        functools.partial(_mla_kernel, sample=sample),
        grid=(n_batch, nq),
        in_specs=in_specs,
        out_specs=out_specs,
        out_shape=out_shape,
        scratch_shapes=[pltpu.VMEM((n_keys, 1024), BF16), pltpu.VMEM((n_keys, 256), BF16)],
        compiler_params=_params(2),
        name="mla_sample" if sample else "mla_prompt",
    )(*args)


def _split_bf16(x):
    hi = x.astype(BF16)
    return hi, (x - hi.astype(F32)).astype(BF16)


def _dft(t_hi, t_lo, x):
    x_hi, x_lo = _split_bf16(x)
    return _mm(t_hi, x_hi) + _mm(t_hi, x_lo) + _mm(t_lo, x_hi)


def _filter_kernel(z_ref, wf1_ref, bf1_ref, fr_ref, wf2_ref, bf2_ref, wf3_ref, t_ref, dl_ref,
                   ch_ref, cl_ref, sh_ref, sl_ref, gre_ref, gim_ref):
    n_tok = z_ref.shape[0]
    fr = fr_ref[...]
    hid = jnp.sin(fr * (_mm(z_ref[...].astype(BF16), wf1_ref[...].astype(BF16)) + bf1_ref[...]))
    hid = jnp.sin(fr * (_mm(hid.astype(BF16), wf2_ref[...].astype(BF16)) + bf2_ref[...]))
    filt = _mm(hid.astype(BF16), wf3_ref[...].astype(BF16))
    decay = jnp.exp(-t_ref[...] * dl_ref[...])
    row = lax.broadcasted_iota(jnp.int32, (n_tok, 1), 0)
    h_f = filt[:, :HY_W] * decay
    h_b = jnp.where(row == 0, 0.0, filt[:, HY_W:] * decay)
    p, m = h_f + h_b, h_f - h_b
    g_re = _dft(ch_ref[...], cl_ref[...], p)
    g_im = _dft(sh_ref[...], sl_ref[...], m)
    sign = jnp.where(row % 2 == 0, 1.0, -1.0)
    nyquist = jnp.sum(p * sign, axis=0, keepdims=True)
    g_im = jnp.where(row == 0, nyquist, g_im)
    wk = jnp.where(row == 0, 0.5 / n_tok, 1.0 / n_tok)
    gre_ref[...] = g_re * wk
    gim_ref[...] = g_im * wk


def _filter_spectrum(z, wf1, bf1, freq, wf2, bf2, wf3, t_col, deltas, tabs):
    n_tok = z.shape[0]
    out = jax.ShapeDtypeStruct((n_tok, HY_W), F32)
    return pl.pallas_call(
        _filter_kernel,
        out_shape=[out, out],
        compiler_params=pltpu.CompilerParams(vmem_limit_bytes=VMEM_LIMIT),
        name="hyena_filter",
    )(z, wf1, bf1.reshape(1, FILT_HID), freq.reshape(1, FILT_HID), wf2, bf2.reshape(1, FILT_HID), wf3,
      t_col, deltas, tabs[0], tabs[1], tabs[2], tabs[3])


HY_CT = 256


def _hyena_kernel(u0_ref, u1_ref, u2_ref, w0_ref, w1_ref, w2_ref, b0_ref, b1_ref, b2_ref, skip_ref,
                  gre_ref, gim_ref, ch_ref, cl_ref, sh_ref, sl_ref, th_ref, tl_ref, o_ref):
    n_tok = u0_ref.shape[0]
    row = lax.broadcasted_iota(jnp.int32, (n_tok, 1), 0)

    def short_conv(u_ref, w_ref, b_ref):
        x, w = u_ref[...], w_ref[...]
        prev = jnp.where(row == 0, 0.0, pltpu.roll(x, 1, 0))
        nxt = jnp.where(row == n_tok - 1, 0.0, pltpu.roll(x, n_tok - 1, 0))
        return prev * w[0:1] + x * w[1:2] + nxt * w[2:3] + b_ref[...]

    x0 = short_conv(u0_ref, w0_ref, b0_ref)
    gv = short_conv(u1_ref, w1_ref, b1_ref) * short_conv(u2_ref, w2_ref, b2_ref)
    u_re = _dft(ch_ref[...], cl_ref[...], gv)
    u_im = _dft(sh_ref[...], sl_ref[...], gv)
    g_re, g_im = gre_ref[...], gim_ref[...]
    p_im = u_im * g_im
    y_re = u_re * g_re - jnp.where(row == 0, 0.0, p_im)
    y_im = jnp.where(row == 0, p_im, u_re * g_im + u_im * g_re)
    y = _dft(ch_ref[...], cl_ref[...], y_re) + _dft(th_ref[...], tl_ref[...], y_im)
    o_ref[...] = (x0 * (y + gv * skip_ref[...])).astype(o_ref.dtype)


def _hyena(proj, w_conv, b_conv, skip, g_re, g_im, tabs, n_batch, seq):
    nct = HY_W // HY_CT
    u_specs = [pl.BlockSpec((seq, HY_CT), functools.partial(lambda b, c, g: (b, g * nct + c), g=g)) for g in range(3)]
    w_specs = [pl.BlockSpec((3, HY_CT), functools.partial(lambda b, c, g: (0, g * nct + c), g=g)) for g in range(3)]
    b_specs = [pl.BlockSpec((1, HY_CT), functools.partial(lambda b, c, g: (0, g * nct + c), g=g)) for g in range(3)]
    tab_spec = pl.BlockSpec((seq, seq), lambda b, c: (0, 0))
    return pl.pallas_call(
        _hyena_kernel,
        grid=(n_batch, nct),
        in_specs=u_specs + w_specs + b_specs + [
            pl.BlockSpec((1, HY_CT), lambda b, c: (0, c)),
            pl.BlockSpec((seq, HY_CT), lambda b, c: (0, c)),
            pl.BlockSpec((seq, HY_CT), lambda b, c: (0, c)),
        ] + [tab_spec] * 6,
        out_specs=pl.BlockSpec((seq, HY_CT), lambda b, c: (b, c)),
        out_shape=jax.ShapeDtypeStruct((n_batch * seq, HY_W), BF16),
        compiler_params=_params(2),
        name="hyena_conv",
    )(proj, proj, proj, w_conv, w_conv, w_conv, b_conv, b_conv, b_conv, skip.reshape(1, HY_W), g_re, g_im, *tabs)


def _dft_tables(n_tok):
    k = np.arange(n_tok)[:, None]
    s = np.arange(n_tok)[None, :]
    ang = ((k * s) % (2 * n_tok)) * (np.pi / n_tok)
    cos_t = np.cos(ang)
    sin_f = np.where(k == 0, np.where(s % 2 == 0, 1.0, -1.0), -np.sin(ang))
    out = []
    for t in (cos_t, sin_f, sin_f.T):
        hi = t.astype(BF16)
        out += [jnp.asarray(hi), jnp.asarray((t - hi.astype(np.float64)).astype(BF16))]
    return out


GLA_LEVELS = (32, 16, 8, 4, 2, 1)


def _gla_constants():
    c = GLA_CHUNK
    idx = np.arange(c)
    i, t = idx[:, None], idx[None, :]
    masks = []
    for s in GLA_LEVELS:
        upper = (idx % (2 * s)) >= s
        masks.append(((i // (2 * s)) == (t // (2 * s))) & upper[:, None] & (~upper)[None, :])
    masks.append(i == t)
    tri = t <= i
    fwd_m = np.stack([np.tile(m, (H_A, 1)) for m in masks]).astype(np.float32)
    bwd_m = np.stack([np.tile(m[::-1, ::-1], (H_A, 1)) for m in masks]).astype(np.float32)
    head_of_row = np.repeat(np.arange(H_A), c)[:, None]
    head_of_lane = np.repeat(np.arange(H_A), DK_A)[None, :]
    head_mask = head_of_row == head_of_lane
    return (jnp.asarray(tri, BF16), jnp.asarray(tri[::-1, ::-1], BF16), jnp.asarray(fwd_m), jnp.asarray(bwd_m),
            jnp.asarray(head_mask, BF16))


def _pair_reference(b, s, backward, row):
    c = GLA_CHUNK
    ref = s if backward else s - 1
    if 2 * s >= 8:
        pieces = [jnp.broadcast_to(b[p * 2 * s + ref:p * 2 * s + ref + 1, :], (2 * s, b.shape[1]))
                  for p in range(c // (2 * s))]
        return pieces[0] if len(pieces) == 1 else jnp.concatenate(pieces, axis=0)
    pos = row % (2 * s)
    out = None
    for o in range(2 * s):
        d = ref - o
        shifted = b if d == 0 else pltpu.roll(b, (-d) % c, 0)
        out = shifted if out is None else jnp.where(pos == o, shifted, out)
    return out


def _gla_chunk(q, k, v, la, t_ref, m_ref, hm, s_ref, backward):
    c = GLA_CHUNK
    l1 = la.astype(BF16)
    r1 = la - l1.astype(F32)
    l2 = r1.astype(BF16)
    l3 = (r1 - l2.astype(F32)).astype(BF16)
    tmat = t_ref[...]
    b = _mm(tmat, l1) + _mm(tmat, l2) + _mm(tmat, l3)
    row = lax.broadcasted_iota(jnp.int32, (c, 1), 0)
    last = 0 if backward else c - 1
    b_last = b[last:last + 1, :]

    def stack_heads(a):
        ab = a.astype(BF16)
        return jnp.concatenate([ab] * H_A, axis=0) * hm

    scores = _nt(stack_heads(q), k.astype(BF16)) * m_ref[len(GLA_LEVELS)]
    for lvl, s in enumerate(GLA_LEVELS):
        is_query = (row % (2 * s) < s) if backward else (row % (2 * s) >= s)
        delta = b - _pair_reference(b, s, backward, row)
        x = jnp.exp(jnp.where(is_query, delta, -delta))
        scores = scores + _nt(stack_heads(q * x), (k * x).astype(BF16)) * m_ref[lvl]
    scores = scores.astype(BF16)
    state = s_ref[...]
    inter = _mm(stack_heads(q * jnp.exp(b)), state.astype(BF16))
    k_rest = (k * jnp.exp(b_last - b)).T
    carry = jnp.broadcast_to(jnp.exp(b_last), (2 * c, b.shape[1])).T
    outs = []
    for h in range(H_A):
        rows = slice(c * h, c * (h + 1))
        v_h = v[:, DV_A * h:DV_A * (h + 1)].astype(BF16)
        outs.append(_mm(scores[rows], v_h) + inter[rows])
        s_ref[rows, :] = state[rows] * carry[rows] + _mm(k_rest[rows].astype(BF16), v_h)
    return jnp.concatenate(outs, axis=1)


def _gla_kernel(*refs, sample):
    if sample:
        (x_ref, z_ref, wf_ref, bf_ref, wb_ref, bb_ref, tf_ref, tb_ref, mf_ref, mb_ref, hm_ref, gn_ref, sf0_ref, sb0_ref,
         o_ref, la_f, la_b, o_f, o_b, s_f, s_b) = refs
    else:
        (x_ref, z_ref, wf_ref, bf_ref, wb_ref, bb_ref, tf_ref, tb_ref, mf_ref, mb_ref, hm_ref, gn_ref,
         o_ref, sf_out, sb_out, la_f, la_b, o_f, o_b, s_f, s_b) = refs
    n_tok = x_ref.shape[0]
    n_chunks = n_tok // GLA_CHUNK
    hk, hv = H_A * DK_A, H_A * DV_A
    zb = z_ref[...].astype(BF16)
    la_f[...] = jax.nn.log_sigmoid(_mm(zb, wf_ref[...].astype(BF16)) + bf_ref[...]) / GLA_TAU
    la_b[...] = jax.nn.log_sigmoid(_mm(zb, wb_ref[...].astype(BF16)) + bb_ref[...]) / GLA_TAU
    if sample:
        s_f[...] = sf0_ref[...]
        s_b[...] = sb0_ref[...]
    else:
        s_f[...] = jnp.zeros_like(s_f)
        s_b[...] = jnp.zeros_like(s_b)
    hm = hm_ref[...]

    def step(ci, carry):
        for la_ref, t_ref, m_ref, s_ref, out_ref, cidx, backward in (
                (la_f, tf_ref, mf_ref, s_f, o_f, ci, False), (la_b, tb_ref, mb_ref, s_b, o_b, n_chunks - 1 - ci, True)):
            rows = pl.ds(pl.multiple_of(cidx * GLA_CHUNK, GLA_CHUNK), GLA_CHUNK)
            q = x_ref[rows, 0:hk] * (DK_A ** -0.5)
            k = x_ref[rows, hk:2 * hk]
            v = x_ref[rows, 2 * hk:2 * hk + hv]
            out_ref[rows, :] = _gla_chunk(q, k, v, la_ref[rows, :], t_ref, m_ref, hm, s_ref, backward)
        return carry

    lax.fori_loop(0, n_chunks, step, 0)
    if not sample:
        sf_out[...] = s_f[...]
        sb_out[...] = s_b[...]
    gain = gn_ref[...]
    for h in range(H_A):
        cols = slice(DV_A * h, DV_A * (h + 1))
        r = x_ref[:, 2 * hk + hv + DV_A * h:2 * hk + hv + DV_A * (h + 1)]
        o_ref[:, cols] = (_rms(o_f[:, cols] + o_b[:, cols]) * gain * (r * jax.nn.sigmoid(r))).astype(o_ref.dtype)


def _gla(proj, w_gf, b_gf, w_gb, b_gb, g_norm, consts, n_batch, seq, ctx=None):
    sample = ctx is not None
    hk, hv = H_A * DK_A, H_A * DV_A
    full = lambda shape: pl.BlockSpec(shape, lambda b: (0,) * len(shape))
    in_specs = [
        pl.BlockSpec((seq, 2 * hk + 2 * hv), lambda b: (b, 0)),
        pl.BlockSpec((seq, LANES), lambda b: (b, EVEN_W // LANES - 1)),
        full((LANES, hk)), full((1, hk)), full((LANES, hk)), full((1, hk)),
        full(consts[0].shape), full(consts[1].shape), full(consts[2].shape), full(consts[3].shape), full(consts[4].shape),
        full((1, DV_A)),
    ]
    args = [proj, proj, w_gf, b_gf.reshape(1, hk), w_gb, b_gb.reshape(1, hk), *consts, g_norm.reshape(1, DV_A)]
    o_spec = pl.BlockSpec((seq, hv), lambda b: (b, 0))
    o_shape = jax.ShapeDtypeStruct((n_batch * seq, hv), BF16)
    st_spec = pl.BlockSpec((None, hk, DV_A), lambda b: (b, 0, 0))
    if sample:
        in_specs += [st_spec, st_spec]
        args += [ctx[0], ctx[1]]
        out_specs, out_shape = o_spec, o_shape
    else:
        st_shape = jax.ShapeDtypeStruct((n_batch, hk, DV_A), F32)
        out_specs, out_shape = [o_spec, st_spec, st_spec], [o_shape, st_shape, st_shape]
    return pl.pallas_call(
        functools.partial(_gla_kernel, sample=sample),
        grid=(n_batch,),
        in_specs=in_specs,
        out_specs=out_specs,
        out_shape=out_shape,
        scratch_shapes=[pltpu.VMEM((seq, hk), F32), pltpu.VMEM((seq, hk), F32),
                        pltpu.VMEM((seq, hv), F32), pltpu.VMEM((seq, hv), F32),
                        pltpu.VMEM((hk, DV_A), F32), pltpu.VMEM((hk, DV_A), F32)],
        compiler_params=_params(1),
        name="gla_sample" if sample else "gla_prompt",
    )(*args)


def _axial_rope(n_tokens, dim):
    rows = n_tokens // GRID_W
    row = np.repeat(np.arange(rows), GRID_W).astype(np.float64)
    col = np.tile(np.arange(GRID_W), rows).astype(np.float64)
    n_freq = dim // 4
    inv = ROPE_THETA ** (-np.arange(n_freq) / n_freq)
    ang = np.concatenate([row[:, None] * inv, col[:, None] * inv], axis=-1)
    return np.cos(ang).astype(np.float32), np.sin(ang).astype(np.float32)


def _filter_features(n_tokens):
    t = np.linspace(0.0, 1.0, n_tokens)[:, None]
    w = 2.0 * np.pi * np.arange(n_tokens)[:, None] / n_tokens
    f = np.linspace(1e-4, FILT_BANDS - 1, FILT_BANDS)[None, :]
    z = np.concatenate([t, np.cos(f * w), -np.sin(f * w)], axis=-1)
    z = np.pad(z, ((0, 0), (0, LANES - FILT_EMB)))
    return jnp.asarray(z, F32), jnp.asarray(t, F32)


_KPE_EXPAND = np.array([(p // LANES) * (ROPE_D // 2) + p % (ROPE_D // 2) for p in range(2 * LANES)])
_QB_PERM = np.array(
    [192 * (p // NOPE_D) + p % NOPE_D for p in range(H_D * NOPE_D)]
    + [192 * (p // 32) + NOPE_D + p % 32 for p in range(H_D * 32)]
    + [192 * (p // 32) + NOPE_D + 32 + p % 32 for p in range(H_D * 32)])

EVEN_COL_BLOCKS = ((1536, 0), (384, 4), (384, 5), (384, 6))
EVEN_ROTATE = 2 * GATE_RANK
ODD_COL_BLOCKS = ((ODD_W, 0),)


def kernel(x_prompt, x_sample, state_gla_fwd, state_gla_bwd, cache_gqa_k, cache_gqa_v, cache_mla_ckv, cache_mla_kpe, c, c_ctx, w_mod, b_mod, w_in_even, w_gla_gate_f, b_gla_gate_f, w_gla_gate_b, b_gla_gate_b, g_gla_norm, g_gqa_q, g_gqa_k, w_out_even, w_in_odd, w_hy_conv, b_hy_conv, hy_skip, w_filt1, b_filt1, filt_freq, w_filt2, b_filt2, w_filt3, g_mla_q, w_mla_qb, g_mla_kv, w_mla_kvb, w_out_odd, w_ffn_in, w_ffn_out, g_final):
    n_c, n_s = BATCH * SEQ, DEC_BATCH * DEC_SEQ
    cvec = jnp.concatenate([c_ctx[None, :], c, jnp.zeros((8 - 1 - DEC_BATCH, D_MODEL), F32)], axis=0)
    mod = _modulation(cvec, w_mod, b_mod)
    xc = x_prompt.reshape(n_c, D_MODEL)
    xs = x_sample.reshape(n_s, D_MODEL)
    rows_c, rows_s = (0, 0), (1, DEC_SEQ // MOD_ROWS)

    gla_consts = _gla_constants()
    cos_b, sin_b = _axial_rope(DEC_SEQ, HD_B)
    rope_b = (jnp.asarray(np.concatenate([cos_b, cos_b], axis=1)), jnp.asarray(np.concatenate([-sin_b, sin_b], axis=1)))
    cos_d, sin_d = _axial_rope(DEC_SEQ, ROPE_D)
    rope_d = (jnp.asarray(np.tile(cos_d, (1, H_D))), jnp.asarray(np.tile(sin_d, (1, H_D))))
    kpe_expand = jnp.asarray(np.arange(ROPE_D)[:, None] == _KPE_EXPAND[None, :], BF16)
    tabs_c, tabs_s = _dft_tables(SEQ), _dft_tables(DEC_SEQ)
    z_c, t_c = _filter_features(SEQ)
    z_s, t_s = _filter_features(DEC_SEQ)
    deltas = jnp.asarray(np.abs(np.linspace(HY_MIN_DECAY, HY_MAX_DECAY, HY_W))[None, :], F32)
    w_even_all = jnp.pad(w_in_even, ((0, 0), (0, 0), (0, EVEN_W - w_in_even.shape[2])))
    w_odd_all = jnp.pad(w_in_odd, ((0, 0), (0, 0), (0, ODD_W - w_in_odd.shape[2])))

    st_gf, st_gb, st_k, st_v, st_ckv, st_kpe = [], [], [], [], [], []
    for i in range(DEPTH):
        j = i // 2
        if i % 2 == 0:
            z0 = LANES - 2 * GATE_RANK
            pad_f = jnp.zeros((LANES, H_A * DK_A), F32).at[z0:z0 + GATE_RANK].set(w_gla_gate_f[j])
            pad_b = jnp.zeros((LANES, H_A * DK_A), F32).at[z0 + GATE_RANK:LANES].set(w_gla_gate_b[j])
            pc = _in_proj(xc, mod, i, w_even_all, j, EVEN_COL_BLOCKS, EVEN_ROTATE, *rows_c)
            ps = _in_proj(xs, mod, i, w_even_all, j, EVEN_COL_BLOCKS, EVEN_ROTATE, *rows_s)
            gate_args = (pad_f, b_gla_gate_f[j], pad_b, b_gla_gate_b[j], g_gla_norm[j], gla_consts)
            a_c, s_f, s_b = _gla(pc, *gate_args, BATCH, SEQ)
            ctx_a = (state_gla_fwd[:, j].reshape(DEC_BATCH, H_A * DK_A, DV_A),
                     state_gla_bwd[:, j].reshape(DEC_BATCH, H_A * DK_A, DV_A))
            a_s = _gla(ps, *gate_args, DEC_BATCH, DEC_SEQ, ctx=ctx_a)
            b_c, k_norm = _gqa(pc, g_gqa_q[j], g_gqa_k[j], BATCH, SEQ)
            ctx_b = (cache_gqa_k[:, j].reshape(DEC_BATCH, PAST_LEN, KV_B * HD_B),
                     cache_gqa_v[:, j].reshape(DEC_BATCH, PAST_LEN, KV_B * HD_B))
            b_s = _gqa(ps, g_gqa_q[j], g_gqa_k[j], DEC_BATCH, DEC_SEQ, ctx=ctx_b, rope=rope_b)
            w_out = w_out_even
            st_gf.append(s_f.reshape(BATCH, H_A, DK_A, DV_A))
            st_gb.append(s_b.reshape(BATCH, H_A, DK_A, DV_A))
            st_k.append(k_norm.reshape(BATCH, SEQ, KV_B, HD_B))
            st_v.append(pc[:, 2304:2560].reshape(BATCH, SEQ, KV_B, HD_B))
        else:
            pc = _in_proj(xc, mod, i, w_odd_all, j, ODD_COL_BLOCKS, 0, *rows_c)
            ps = _in_proj(xs, mod, i, w_odd_all, j, ODD_COL_BLOCKS, 0, *rows_s)
            wf1 = jnp.pad(w_filt1[j], ((0, LANES - FILT_EMB), (0, 0)))
            filt_args = (wf1, b_filt1[j], filt_freq[j], w_filt2[j], b_filt2[j], w_filt3[j])
            g_c = _filter_spectrum(z_c, *filt_args, t_c, deltas, tabs_c)
            g_s = _filter_spectrum(z_s, *filt_args, t_s, deltas, tabs_s)
            b_conv = b_hy_conv[j].reshape(1, 3 * HY_W)
            a_c = _hyena(pc, w_hy_conv[j], b_conv, hy_skip[j], g_c[0], g_c[1], tabs_c, BATCH, SEQ)
            a_s = _hyena(ps, w_hy_conv[j], b_conv, hy_skip[j], g_s[0], g_s[1], tabs_s, DEC_BATCH, DEC_SEQ)
            w_qb = w_mla_qb[j][:, _QB_PERM]
            b_c, ckv_norm = _mla(pc, g_mla_q[j], w_qb, g_mla_kv[j], w_mla_kvb[j], BATCH, SEQ, expand=kpe_expand)
            b_s = _mla(ps, g_mla_q[j], w_qb, g_mla_kv[j], w_mla_kvb[j], DEC_BATCH, DEC_SEQ,
                       ctx=(cache_mla_ckv[:, j], cache_mla_kpe[:, j]), rope=rope_d, expand=kpe_expand)
            w_out = w_out_odd
            st_ckv.append(ckv_norm.reshape(BATCH, SEQ, KV_RANK))
            st_kpe.append(pc[:, 1920:1984].reshape(BATCH, SEQ, ROPE_D))
        xc = _out_proj([a_c, b_c], w_out, j, xc, mod, i, 2, *rows_c)
        xs = _out_proj([a_s, b_s], w_out, j, xs, mod, i, 2, *rows_s)
        xc = _ffn(xc, mod, i, w_ffn_in, w_ffn_out, *rows_c)
        xs = _ffn(xs, mod, i, w_ffn_in, w_ffn_out, *rows_s)
    y_prompt = _final_norm(xc, g_final).reshape(BATCH, SEQ, D_MODEL)
    y_sample = _final_norm(xs, g_final).reshape(DEC_BATCH, DEC_SEQ, D_MODEL)
    return (y_prompt, y_sample, jnp.stack(st_gf, axis=1), jnp.stack(st_gb, axis=1), jnp.stack(st_k, axis=1),
            jnp.stack(st_v, axis=1), jnp.stack(st_ckv, axis=1), jnp.stack(st_kpe, axis=1))
```

```python
import functools
import math

import numpy as np
import jax
import jax.numpy as jnp
from jax import lax
from jax.experimental import pallas as pl
from jax.experimental.pallas import tpu as pltpu

F32 = jnp.float32
BF16 = jnp.bfloat16

D_MODEL = 1024
BATCH, SEQ = 16, 256
DEC_BATCH, DEC_SEQ = 2, 1024
DEPTH = 4
PAST_LEN = 512
GRID_W = 64
HALF_W = D_MODEL // 2
H_A, DV_A, DK_A = 4, 128, 64
GATE_RANK = 16
GLA_TAU = 16.0
GLA_CHUNK = 64
HD_B, H_B, KV_B = 128, 4, 2
HY_W = HALF_W
FILT_EMB, FILT_HID = 33, 64
FILT_BANDS = (FILT_EMB - 1) // 2
HY_MIN_DECAY = math.log(1e-2) / 1.5
HY_MAX_DECAY = math.log(1e-2) / 0.3
H_D, V_D, NOPE_D, ROPE_D = 4, 128, 128, 64
Q_RANK, KV_RANK = 256, 128
FFN_H = 2816
ROPE_THETA = 10000.0
EPS = 1e-6

LANES = 128
VMEM_LIMIT = 56 * 1024 * 1024

MOD_ROWS = 1024
TM = 1024
TM_IN = 512
TM_FFN = 2048
EVEN_W = 2688
ODD_W = 2048
FFN_TN = 256
QB = 256


def _params(n_grid):
    return pltpu.CompilerParams(dimension_semantics=("arbitrary",) * n_grid, vmem_limit_bytes=VMEM_LIMIT)


def _nt(a, b):
    return lax.dot_general(a, b, (((1,), (1,)), ((), ())), preferred_element_type=F32)


def _mm(a, b):
    return jnp.dot(a, b, preferred_element_type=F32)


def _rms(x):
    return x * lax.rsqrt(jnp.mean(x * x, axis=-1, keepdims=True) + EPS)


def _mod_kernel(c_ref, w_ref, b_ref, o_ref):
    cv = c_ref[...]
    s = cv * jax.nn.sigmoid(cv)
    o_ref[...] = _mm(s.astype(BF16), w_ref[...].astype(BF16)) + b_ref[...]


def _modulation(cvec, w_mod, b_mod):
    return pl.pallas_call(
        _mod_kernel,
        grid=(DEPTH, 6),
        in_specs=[
            pl.BlockSpec((8, D_MODEL), lambda l, n: (0, 0)),
            pl.BlockSpec((None, D_MODEL, D_MODEL), lambda l, n: (l, 0, n)),
            pl.BlockSpec((None, 1, D_MODEL), lambda l, n: (l, 0, n)),
        ],
        out_specs=pl.BlockSpec((None, None, 8, D_MODEL), lambda l, n: (l, n, 0, 0)),
        out_shape=jax.ShapeDtypeStruct((DEPTH, 6, 8, D_MODEL), F32),
        compiler_params=_params(2),
        name="adaln_mod",
    )(cvec, w_mod, b_mod.reshape(DEPTH, 1, 6 * D_MODEL))


def _mod_row(row0, rstep, tile_rows, sub):
    if tile_rows >= MOD_ROWS:
        return row0 + rstep * (pl.program_id(0) * (tile_rows // MOD_ROWS) + sub)
    return row0 + rstep * (pl.program_id(0) // (MOD_ROWS // tile_rows))


def _in_proj_kernel(x_ref, sh_ref, sc_ref, wt_ref, o_ref, wb_ref, *, row_groups, row0, rstep):
    @pl.when(pl.program_id(0) == 0)
    def _():
        wb_ref[...] = jnp.zeros_like(wb_ref)
        for src, dst, size in row_groups:
            wb_ref[dst:dst + size, :] = wt_ref[src:src + size, :].astype(BF16)

    g = _mod_row(row0, rstep, x_ref.shape[0], 0)
    h = (_rms(x_ref[...]) * (1.0 + sc_ref[pl.ds(g, 1), :]) + sh_ref[pl.ds(g, 1), :]).astype(BF16)
    o_ref[...] = _nt(h, wb_ref[...])


def _in_proj(x, mod, layer, wt, w_layer, row_groups, n, row0, rstep):
    m = x.shape[0]
    return pl.pallas_call(
        functools.partial(_in_proj_kernel, row_groups=row_groups, row0=row0, rstep=rstep),
        grid=(m // TM_IN,),
        in_specs=[pl.BlockSpec((TM_IN, D_MODEL), lambda i: (i, 0)),
                  pl.BlockSpec((None, None, 8, D_MODEL), lambda i: (layer, 0, 0, 0)),
                  pl.BlockSpec((None, None, 8, D_MODEL), lambda i: (layer, 1, 0, 0)),
                  pl.BlockSpec((None, wt.shape[1], D_MODEL), lambda i: (w_layer, 0, 0), pipeline_mode=pl.Buffered(1))],
        out_specs=pl.BlockSpec((TM_IN, n), lambda i: (i, 0)),
        out_shape=jax.ShapeDtypeStruct((m, n), F32),
        scratch_shapes=[pltpu.VMEM((n, D_MODEL), BF16)],
        compiler_params=_params(1),
        name="norm_mod_proj",
    )(x, mod, mod, wt)


def _ffn_kernel(x_ref, sh_ref, sc_ref, gate_ref, wg_ref, wu_ref, wd_ref, o_ref, h_ref, *, row0, rstep):
    n_sub = x_ref.shape[0] // MOD_ROWS
    subs = [(slice(s * MOD_ROWS, (s + 1) * MOD_ROWS), _mod_row(row0, rstep, x_ref.shape[0], s)) for s in range(n_sub)]

    @pl.when(pl.program_id(1) == 0)
    def _():
        for rows, g in subs:
            x = x_ref[rows, :]
            o_ref[rows, :] = x
            h_ref[rows, :] = (_rms(x) * (1.0 + sc_ref[pl.ds(g, 1), :]) + sh_ref[pl.ds(g, 1), :]).astype(BF16)

    wg = wg_ref[...].astype(BF16)
    wu = wu_ref[...].astype(BF16)
    wd = wd_ref[...].astype(BF16)
    for rows, g in subs:
        h = h_ref[rows, :]
        a = _mm(h, wg)
        act = (a * jax.nn.sigmoid(a) * _mm(h, wu)).astype(BF16)
        o_ref[rows, :] += gate_ref[pl.ds(g, 1), :] * _mm(act, wd)


def _ffn(x, mod, layer, w_in, w_out, row0, rstep):
    m = x.shape[0]
    nj = FFN_H // FFN_TN
    mod_spec = lambda k: pl.BlockSpec((None, None, 8, D_MODEL), lambda i, j: (layer, k, 0, 0))
    return pl.pallas_call(
        functools.partial(_ffn_kernel, row0=row0, rstep=rstep),
        grid=(m // TM_FFN, nj),
        in_specs=[pl.BlockSpec((TM_FFN, D_MODEL), lambda i, j: (i, 0)), mod_spec(3), mod_spec(4), mod_spec(5),
                  pl.BlockSpec((None, D_MODEL, FFN_TN), lambda i, j: (layer, 0, j)),
                  pl.BlockSpec((None, D_MODEL, FFN_TN), lambda i, j: (layer, 0, j + nj)),
                  pl.BlockSpec((None, FFN_TN, D_MODEL), lambda i, j: (layer, j, 0))],
        out_specs=pl.BlockSpec((TM_FFN, D_MODEL), lambda i, j: (i, 0)),
        out_shape=jax.ShapeDtypeStruct((m, D_MODEL), F32),
        scratch_shapes=[pltpu.VMEM((TM_FFN, D_MODEL), BF16)],
        compiler_params=_params(2),
        name="ffn_residual",
    )(x, mod, mod, mod, w_in, w_in, w_out)


def _proj_res_kernel(*refs, n_act, row0, rstep):
    acts, ws = refs[:n_act], refs[n_act:2 * n_act]
    x_ref, gate_ref, o_ref = refs[2 * n_act:]
    g = row0 + rstep * pl.program_id(0)
    acc = _mm(acts[0][...], ws[0][...].astype(BF16))
    for a_ref, w_ref in zip(acts[1:], ws[1:]):
        acc = acc + _mm(a_ref[...], w_ref[...].astype(BF16))
    o_ref[...] = x_ref[...] + gate_ref[pl.ds(g, 1), :] * acc


def _out_proj(acts, w, w_layer, x, mod, layer, k_gate, row0, rstep):
    m = x.shape[0]
    n_act = len(acts)
    kw = acts[0].shape[1]
    act_specs = [pl.BlockSpec((TM, kw), lambda i: (i, 0)) for _ in acts]
    w_specs = [pl.BlockSpec((None, kw, D_MODEL), functools.partial(lambda i, p: (w_layer, p, 0), p=p),
                            pipeline_mode=pl.Buffered(1)) for p in range(n_act)]
    return pl.pallas_call(
        functools.partial(_proj_res_kernel, n_act=n_act, row0=row0, rstep=rstep),
        grid=(m // TM,),
        in_specs=act_specs + w_specs + [
            pl.BlockSpec((TM, D_MODEL), lambda i: (i, 0)),
            pl.BlockSpec((None, None, 8, D_MODEL), lambda i: (layer, k_gate, 0, 0)),
        ],
        out_specs=pl.BlockSpec((TM, D_MODEL), lambda i: (i, 0)),
        out_shape=jax.ShapeDtypeStruct((m, D_MODEL), F32),
        compiler_params=_params(1),
        name="out_proj_residual",
    )(*acts, *([w] * n_act), x, mod)


def _final_kernel(x_ref, g_ref, o_ref):
    o_ref[...] = _rms(x_ref[...]) * g_ref[...]


def _final_norm(x, g):
    m = x.shape[0]
    return pl.pallas_call(
        _final_kernel,
        grid=(m // TM,),
        in_specs=[pl.BlockSpec((TM, D_MODEL), lambda i: (i, 0)), pl.BlockSpec((1, D_MODEL), lambda i: (0, 0))],
        out_specs=pl.BlockSpec((TM, D_MODEL), lambda i: (i, 0)),
        out_shape=jax.ShapeDtypeStruct((m, D_MODEL), F32),
        compiler_params=_params(1),
        name="final_norm",
    )(x, g.reshape(1, D_MODEL))


def _softmax_pv(s, v):
    m = jnp.max(s, axis=-1, keepdims=True)
    e = jnp.exp(s - m)
    l = jnp.sum(e, axis=-1, keepdims=True)
    return _mm(e.astype(BF16), v) / l


def _gqa_kernel(*refs, sample):
    if sample:
        q_ref, k_ref, v_ref, gq_ref, gk_ref, ck_ref, cv_ref, cos_ref, sin_ref, o_ref, kb_ref, vb_ref = refs
    else:
        q_ref, k_ref, v_ref, gq_ref, gk_ref, o_ref, kn_ref, kb_ref, vb_ref = refs
    qi = pl.program_id(1)
    n_new = k_ref.shape[0]
    past = PAST_LEN if sample else 0

    @pl.when(qi == 0)
    def _():
        for g in range(KV_B):
            sl = slice(HD_B * g, HD_B * (g + 1))
            kn = _rms(k_ref[:, sl]) * gk_ref[...]
            if sample:
                kb_ref[0:past, sl] = ck_ref[:, sl].astype(BF16)
                vb_ref[0:past, sl] = cv_ref[:, sl].astype(BF16)
                kn = kn * cos_ref[...] + pltpu.roll(kn, HD_B // 2, 1) * sin_ref[...]
            else:
                kn_ref[:, sl] = kn
            kb_ref[past:past + n_new, sl] = kn.astype(BF16)
            vb_ref[past:past + n_new, sl] = v_ref[:, sl].astype(BF16)

    r0 = pl.multiple_of(qi * QB, QB)
    for h in range(H_B):
        g = h // (H_B // KV_B)
        gs = slice(HD_B * g, HD_B * (g + 1))
        hs = slice(HD_B * h, HD_B * (h + 1))
        qn = _rms(q_ref[:, hs]) * gq_ref[...]
        if sample:
            qn = qn * cos_ref[pl.ds(r0, QB), :] + pltpu.roll(qn, HD_B // 2, 1) * sin_ref[pl.ds(r0, QB), :]
        s = _nt(qn.astype(BF16), kb_ref[:, gs]) * (HD_B ** -0.5)
        o_ref[:, hs] = _softmax_pv(s, vb_ref[:, gs]).astype(o_ref.dtype)


def _gqa(proj, g_q, g_k, n_batch, seq, ctx=None, rope=None):
    sample = ctx is not None
    m = n_batch * seq
    nq = seq // QB
    in_specs = [
        pl.BlockSpec((QB, 512), lambda b, i: (b * nq + i, 3)),
        pl.BlockSpec((seq, 256), lambda b, i: (b, 8)),
        pl.BlockSpec((seq, 256), lambda b, i: (b, 9)),
        pl.BlockSpec((1, HD_B), lambda b, i: (0, 0)),
        pl.BlockSpec((1, HD_B), lambda b, i: (0, 0)),
    ]
    args = [proj, proj, proj, g_q.reshape(1, HD_B), g_k.reshape(1, HD_B)]
    o_spec = pl.BlockSpec((QB, 512), lambda b, i: (b * nq + i, 0))
    o_shape = jax.ShapeDtypeStruct((m, 512), BF16)
    if sample:
        in_specs += [
            pl.BlockSpec((None, PAST_LEN, 256), lambda b, i: (b, 0, 0)),
            pl.BlockSpec((None, PAST_LEN, 256), lambda b, i: (b, 0, 0)),
            pl.BlockSpec((seq, HD_B), lambda b, i: (0, 0)),
            pl.BlockSpec((seq, HD_B), lambda b, i: (0, 0)),
        ]
        args += [ctx[0], ctx[1], rope[0], rope[1]]
        out_specs, out_shape = o_spec, o_shape
    else:
        out_specs = [o_spec, pl.BlockSpec((seq, 256), lambda b, i: (b, 0))]
        out_shape = [o_shape, jax.ShapeDtypeStruct((m, 256), F32)]
    n_keys = seq + (PAST_LEN if sample else 0)
    return pl.pallas_call(
        functools.partial(_gqa_kernel, sample=sample),
        grid=(n_batch, nq),
        in_specs=in_specs,
        out_specs=out_specs,
        out_shape=out_shape,
        scratch_shapes=[pltpu.VMEM((n_keys, 256), BF16), pltpu.VMEM((n_keys, 256), BF16)],
        compiler_params=_params(2),
        name="gqa_sample" if sample else "gqa_prompt",
    )(*args)


def _mla_kernel(*refs, sample):
    if sample:
        (cq_ref, ckv_ref, kpe_ref, gq_ref, wqb_ref, gkv_ref, wkvb_ref, ex_ref, cckv_ref, ckpe_ref, c4_ref, s4_ref,
         o_ref, kv_s, kx_s) = refs
    else:
        cq_ref, ckv_ref, kpe_ref, gq_ref, wqb_ref, gkv_ref, wkvb_ref, ex_ref, o_ref, ckvn_ref, kv_s, kx_s = refs
    qi = pl.program_id(1)
    n_new = ckv_ref.shape[0]
    past = PAST_LEN if sample else 0

    def rope(x, c, s):
        x1, x2 = x[:, :LANES], x[:, LANES:]
        return jnp.concatenate([x1 * c - x2 * s, x1 * s + x2 * c], axis=1)

    @pl.when(qi == 0)
    def _():
        wkvb = wkvb_ref[...].astype(BF16)
        ckvn = _rms(ckv_ref[...]) * gkv_ref[...]
        if not sample:
            ckvn_ref[...] = ckvn
        kv_s[past:past + n_new, :] = _mm(ckvn.astype(BF16), wkvb).astype(BF16)
        kpe = kpe_ref[:, 0:ROPE_D]
        if sample:
            kv_s[0:past, :] = _mm(cckv_ref[...].astype(BF16), wkvb).astype(BF16)
            kx_s[0:past, :] = _mm(ckpe_ref[...].astype(BF16), ex_ref[...]).astype(BF16)
            k1 = kpe.astype(BF16)
            r1 = kpe - k1.astype(F32)
            k2 = r1.astype(BF16)
            k3 = (r1 - k2.astype(F32)).astype(BF16)
            kx = _mm(k1, ex_ref[...]) + _mm(k2, ex_ref[...]) + _mm(k3, ex_ref[...])
            kx_s[past:past + n_new, :] = rope(kx, c4_ref[...], s4_ref[...]).astype(BF16)
        else:
            kx_s[...] = _mm(kpe.astype(BF16), ex_ref[...]).astype(BF16)

    q = _mm((_rms(cq_ref[...]) * gq_ref[...]).astype(BF16), wqb_ref[...].astype(BF16))
    qpe = q[:, 4 * NOPE_D:]
    if sample:
        r0 = pl.multiple_of(qi * QB, QB)
        qpe = rope(qpe, c4_ref[pl.ds(r0, QB), :], s4_ref[pl.ds(r0, QB), :])
    lane_head = (lax.broadcasted_iota(jnp.int32, (1, 2 * LANES), 1) % LANES) // (ROPE_D // 2)
    scale = (NOPE_D + ROPE_D) ** -0.5
    for h in range(H_D):
        qm = jnp.where(lane_head == h, qpe, 0.0)
        s = _nt(q[:, NOPE_D * h:NOPE_D * (h + 1)].astype(BF16), kv_s[:, 256 * h:256 * h + NOPE_D])
        s = (s + _nt(qm.astype(BF16), kx_s[...])) * scale
        o_ref[:, V_D * h:V_D * (h + 1)] = _softmax_pv(s, kv_s[:, 256 * h + NOPE_D:256 * (h + 1)]).astype(o_ref.dtype)


def _mla(proj, g_q, w_qb, g_kv, w_kvb, n_batch, seq, ctx=None, rope=None, expand=None):
    sample = ctx is not None
    m = n_batch * seq
    nq = seq // QB
    in_specs = [
        pl.BlockSpec((QB, Q_RANK), lambda b, i: (b * nq + i, 6)),
        pl.BlockSpec((seq, KV_RANK), lambda b, i: (b, 14)),
        pl.BlockSpec((seq, LANES), lambda b, i: (b, 15)),
        pl.BlockSpec((1, Q_RANK), lambda b, i: (0, 0)),
        pl.BlockSpec((Q_RANK, 768), lambda b, i: (0, 0)),
        pl.BlockSpec((1, KV_RANK), lambda b, i: (0, 0)),
        pl.BlockSpec((KV_RANK, 1024), lambda b, i: (0, 0)),
        pl.BlockSpec((ROPE_D, 256), lambda b, i: (0, 0)),
    ]
    args = [proj, proj, proj, g_q.reshape(1, Q_RANK), w_qb, g_kv.reshape(1, KV_RANK), w_kvb, expand]
    o_spec = pl.BlockSpec((QB, 512), lambda b, i: (b * nq + i, 0))
    o_shape = jax.ShapeDtypeStruct((m, 512), BF16)
    if sample:
        in_specs += [
            pl.BlockSpec((None, PAST_LEN, KV_RANK), lambda b, i: (b, 0, 0)),
            pl.BlockSpec((None, PAST_LEN, ROPE_D), lambda b, i: (b, 0, 0)),
            pl.BlockSpec((seq, LANES), lambda b, i: (0, 0)),
            pl.BlockSpec((seq, LANES), lambda b, i: (0, 0)),
        ]
        args += [ctx[0], ctx[1], rope[0], rope[1]]
        out_specs, out_shape = o_spec, o_shape
    else:
        out_specs = [o_spec, pl.BlockSpec((seq, KV_RANK), lambda b, i: (b, 0))]
        out_shape = [o_shape, jax.ShapeDtypeStruct((m, KV_RANK), F32)]
    n_keys = seq + (PAST_LEN if sample else 0)
    return pl.pallas_call(
        functools.partial(_mla_kernel, sample=sample),
        grid=(n_batch, nq),
        in_specs=in_specs,
        out_specs=out_specs,
        out_shape=out_shape,
        scratch_shapes=[pltpu.VMEM((n_keys, 1024), BF16), pltpu.VMEM((n_keys, 256), BF16)],
        compiler_params=_params(2),
        name="mla_sample" if sample else "mla_prompt",
    )(*args)


def _split_bf16(x):
    hi = x.astype(BF16)
    return hi, (x - hi.astype(F32)).astype(BF16)


def _dft(table, x):
    t_hi, t_lo = _split_bf16(table)
    x_hi, x_lo = _split_bf16(x)
    return _mm(t_hi, x_hi) + _mm(t_hi, x_lo) + _mm(t_lo, x_hi)


def _filter_kernel(z_ref, wf1_ref, bf1_ref, fr_ref, wf2_ref, bf2_ref, wf3_ref, t_ref, dl_ref,
                   c_ref, s_ref, gre_ref, gim_ref):
    n_tok = z_ref.shape[0]
    fr = fr_ref[...]
    hid = jnp.sin(fr * (_mm(z_ref[...].astype(BF16), wf1_ref[...].astype(BF16)) + bf1_ref[...]))
    hid = jnp.sin(fr * (_mm(hid.astype(BF16), wf2_ref[...].astype(BF16)) + bf2_ref[...]))
    filt = _mm(hid.astype(BF16), wf3_ref[...].astype(BF16))
    decay = jnp.exp(-t_ref[...] * dl_ref[...])
    row = lax.broadcasted_iota(jnp.int32, (n_tok, 1), 0)
    h_f = filt[:, :HY_W] * decay
    h_b = jnp.where(row == 0, 0.0, filt[:, HY_W:] * decay)
    p, m = h_f + h_b, h_f - h_b
    g_re = _dft(c_ref[...], p)
    g_im = _dft(s_ref[...], m)
    sign = jnp.where(row % 2 == 0, 1.0, -1.0)
    nyquist = jnp.sum(p * sign, axis=0, keepdims=True)
    g_im = jnp.where(row == 0, nyquist, g_im)
    wk = jnp.where(row == 0, 0.5 / n_tok, 1.0 / n_tok)
    gre_ref[...] = g_re * wk
    gim_ref[...] = g_im * wk


def _filter_spectrum(z, wf1, bf1, freq, wf2, bf2, wf3, t_col, deltas, tabs):
    n_tok = z.shape[0]
    out = jax.ShapeDtypeStruct((n_tok, HY_W), F32)
    return pl.pallas_call(
        _filter_kernel,
        out_shape=[out, out],
        compiler_params=pltpu.CompilerParams(vmem_limit_bytes=VMEM_LIMIT),
        name="hyena_filter",
    )(z, wf1, bf1.reshape(1, FILT_HID), freq.reshape(1, FILT_HID), wf2, bf2.reshape(1, FILT_HID), wf3,
      t_col, deltas, tabs[0], tabs[1])


HY_CT = 256


HY_ROWS = 1024


def _hyena_kernel(u0_ref, u1_ref, u2_ref, w0_ref, w1_ref, w2_ref, b0_ref, b1_ref, b2_ref, skip_ref,
                  gre_ref, gim_ref, cf_ref, sf_ref, stf_ref, o_ref, c_ref, s_ref, st_ref):
    seq = c_ref.shape[0]
    n_rows = u0_ref.shape[0]
    n_seq = n_rows // seq
    pos = lax.broadcasted_iota(jnp.int32, (n_rows, 1), 0) % seq

    @pl.when((pl.program_id(0) == 0) & (pl.program_id(1) == 0))
    def _():
        c_ref[...] = cf_ref[...].astype(BF16)
        s_ref[...] = sf_ref[...].astype(BF16)
        st_ref[...] = stf_ref[...].astype(BF16)

    def short_conv(u_ref, w_ref, b_ref):
        x, w = u_ref[...], w_ref[...]
        prev = jnp.where(pos == 0, 0.0, pltpu.roll(x, 1, 0))
        nxt = jnp.where(pos == seq - 1, 0.0, pltpu.roll(x, n_rows - 1, 0))
        return prev * w[0:1] + x * w[1:2] + nxt * w[2:3] + b_ref[...]

    def side_by_side(a):
        return a if n_seq == 1 else jnp.concatenate([a[s * seq:(s + 1) * seq] for s in range(n_seq)], axis=1)

    def stacked(a):
        ct = a.shape[1] // n_seq
        return a if n_seq == 1 else jnp.concatenate([a[:, s * ct:(s + 1) * ct] for s in range(n_seq)], axis=0)

    x0 = short_conv(u0_ref, w0_ref, b0_ref)
    gv = short_conv(u1_ref, w1_ref, b1_ref) * short_conv(u2_ref, w2_ref, b2_ref)
    sig = side_by_side(gv).astype(BF16)
    u_re = _mm(c_ref[...], sig)
    u_im = _mm(s_ref[...], sig)
    g_re = jnp.concatenate([gre_ref[...]] * n_seq, axis=1)
    g_im = jnp.concatenate([gim_ref[...]] * n_seq, axis=1)
    bin0 = lax.broadcasted_iota(jnp.int32, (seq, 1), 0) == 0
    p_im = u_im * g_im
    y_re = u_re * g_re - jnp.where(bin0, 0.0, p_im)
    y_im = jnp.where(bin0, p_im, u_re * g_im + u_im * g_re)
    y = stacked(_mm(c_ref[...], y_re.astype(BF16)) + _mm(st_ref[...], y_im.astype(BF16)))
    o_ref[...] = (x0 * (y + gv * skip_ref[...])).astype(o_ref.dtype)


def _hyena(proj, w_conv, b_conv, skip, g_re, g_im, tabs, n_batch, seq):
    nct = HY_W // HY_CT
    u_specs = [pl.BlockSpec((HY_ROWS, HY_CT), functools.partial(lambda b, c, g: (b, g * nct + c), g=g)) for g in range(3)]
    w_specs = [pl.BlockSpec((3, HY_CT), functools.partial(lambda b, c, g: (0, g * nct + c), g=g)) for g in range(3)]
    b_specs = [pl.BlockSpec((1, HY_CT), functools.partial(lambda b, c, g: (0, g * nct + c), g=g)) for g in range(3)]
    tab_spec = pl.BlockSpec((seq, seq), lambda b, c: (0, 0))
    return pl.pallas_call(
        _hyena_kernel,
        grid=(n_batch * seq // HY_ROWS, nct),
        in_specs=u_specs + w_specs + b_specs + [
            pl.BlockSpec((1, HY_CT), lambda b, c: (0, c)),
            pl.BlockSpec((seq, HY_CT), lambda b, c: (0, c)),
            pl.BlockSpec((seq, HY_CT), lambda b, c: (0, c)),
        ] + [tab_spec] * 3,
        out_specs=pl.BlockSpec((HY_ROWS, HY_CT), lambda b, c: (b, c)),
        out_shape=jax.ShapeDtypeStruct((n_batch * seq, HY_W), BF16),
        scratch_shapes=[pltpu.VMEM((seq, seq), BF16)] * 3,
        compiler_params=_params(2),
        name="hyena_conv",
    )(proj, proj, proj, w_conv, w_conv, w_conv, b_conv, b_conv, b_conv, skip.reshape(1, HY_W), g_re, g_im, *tabs)


def _dft_tables(n_tok):
    k = np.arange(n_tok)[:, None]
    s = np.arange(n_tok)[None, :]
    ang = ((k * s) % (2 * n_tok)) * (np.pi / n_tok)
    cos_t = np.cos(ang)
    sin_f = np.where(k == 0, np.where(s % 2 == 0, 1.0, -1.0), -np.sin(ang))
    return [jnp.asarray(t, F32) for t in (cos_t, sin_f, sin_f.T)]


GLA_LEVELS = (32, 16, 8, 4, 2, 1)
GLA_SAFE_DECAY = 60.0


def _gla_constants():
    c = GLA_CHUNK
    idx = np.arange(c)
    i, t = idx[:, None], idx[None, :]
    masks = []
    for s in GLA_LEVELS:
        upper = (idx % (2 * s)) >= s
        masks.append(((i // (2 * s)) == (t // (2 * s))) & upper[:, None] & (~upper)[None, :])
    masks.append(i == t)
    tri = t <= i
    fwd_m = np.stack([np.tile(m, (H_A, 1)) for m in masks]).astype(np.float32)
    bwd_m = np.stack([np.tile(m[::-1, ::-1], (H_A, 1)) for m in masks]).astype(np.float32)
    head_of_row = np.repeat(np.arange(H_A), c)[:, None]
    head_of_lane = np.repeat(np.arange(H_A), DK_A)[None, :]
    head_mask = head_of_row == head_of_lane
    return (jnp.asarray(tri, BF16), jnp.asarray(tri[::-1, ::-1], BF16), jnp.asarray(fwd_m), jnp.asarray(bwd_m),
            jnp.asarray(head_mask, BF16))


def _pair_reference(b, s, backward, row):
    c = GLA_CHUNK
    ref = s if backward else s - 1
    if 2 * s >= 8:
        pieces = [jnp.broadcast_to(b[p * 2 * s + ref:p * 2 * s + ref + 1, :], (2 * s, b.shape[1]))
                  for p in range(c // (2 * s))]
        return pieces[0] if len(pieces) == 1 else jnp.concatenate(pieces, axis=0)
    pos = row % (2 * s)
    out = None
    for o in range(2 * s):
        d = ref - o
        shifted = b if d == 0 else pltpu.roll(b, (-d) % c, 0)
        out = shifted if out is None else jnp.where(pos == o, shifted, out)
    return out


def _gla_chunk(q, k, v, la, t_ref, m_ref, hm, s_ref, backward, single_split):
    c = GLA_CHUNK
    l1 = la.astype(BF16)
    r1 = la - l1.astype(F32)
    l2 = r1.astype(BF16)
    l3 = (r1 - l2.astype(F32)).astype(BF16)
    tmat = t_ref[...]
    b = _mm(tmat, l1) + _mm(tmat, l2) + _mm(tmat, l3)
    row = lax.broadcasted_iota(jnp.int32, (c, 1), 0)
    last = 0 if backward else c - 1
    b_last = b[last:last + 1, :]

    def stack_heads(a):
        ab = a.astype(BF16)
        return jnp.concatenate([ab] * H_A, axis=0) * hm

    q_decayed = stack_heads(q * jnp.exp(b))
    if single_split:
        scores = _nt(q_decayed, (k * jnp.exp(-b)).astype(BF16)) * jnp.sum(m_ref[...], axis=0)
    else:
        scores = _nt(stack_heads(q), k.astype(BF16)) * m_ref[len(GLA_LEVELS)]
        for lvl, s in enumerate(GLA_LEVELS):
            is_query = (row % (2 * s) < s) if backward else (row % (2 * s) >= s)
            delta = b - _pair_reference(b, s, backward, row)
            x = jnp.exp(jnp.where(is_query, delta, -delta))
            scores = scores + _nt(stack_heads(q * x), (k * x).astype(BF16)) * m_ref[lvl]
    scores = scores.astype(BF16)
    state = s_ref[...]
    inter = _mm(q_decayed, state.astype(BF16))
    k_rest = (k * jnp.exp(b_last - b)).T
    carry = jnp.broadcast_to(jnp.exp(b_last), (2 * c, b.shape[1])).T
    outs = []
    for h in range(H_A):
        rows = slice(c * h, c * (h + 1))
        v_h = v[:, DV_A * h:DV_A * (h + 1)].astype(BF16)
        outs.append(_mm(scores[rows], v_h) + inter[rows])
        s_ref[rows, :] = state[rows] * carry[rows] + _mm(k_rest[rows].astype(BF16), v_h)
    return jnp.concatenate(outs, axis=1)


def _gla_kernel(*refs, sample):
    if sample:
        (x_ref, z_ref, wf_ref, bf_ref, wb_ref, bb_ref, tf_ref, tb_ref, mf_ref, mb_ref, hm_ref, gn_ref, sf0_ref, sb0_ref,
         o_ref, la_f, la_b, o_f, o_b, s_f, s_b) = refs
    else:
        (x_ref, z_ref, wf_ref, bf_ref, wb_ref, bb_ref, tf_ref, tb_ref, mf_ref, mb_ref, hm_ref, gn_ref,
         o_ref, sf_out, sb_out, la_f, la_b, o_f, o_b, s_f, s_b) = refs
    n_tok = x_ref.shape[0]
    n_chunks = n_tok // GLA_CHUNK
    hk, hv = H_A * DK_A, H_A * DV_A
    zb = z_ref[...].astype(BF16)
    la_f[...] = jax.nn.log_sigmoid(_mm(zb, wf_ref[...].astype(BF16)) + bf_ref[...]) / GLA_TAU
    la_b[...] = jax.nn.log_sigmoid(_mm(zb, wb_ref[...].astype(BF16)) + bb_ref[...]) / GLA_TAU
    if sample:
        s_f[...] = sf0_ref[...]
        s_b[...] = sb0_ref[...]
    else:
        s_f[...] = jnp.zeros_like(s_f)
        s_b[...] = jnp.zeros_like(s_b)
    hm = hm_ref[...]

    def step(ci, carry, single_split):
        for la_ref, t_ref, m_ref, s_ref, out_ref, cidx, backward in (
                (la_f, tf_ref, mf_ref, s_f, o_f, ci, False), (la_b, tb_ref, mb_ref, s_b, o_b, n_chunks - 1 - ci, True)):
            rows = pl.ds(pl.multiple_of(cidx * GLA_CHUNK, GLA_CHUNK), GLA_CHUNK)
            q = x_ref[rows, 0:hk] * (DK_A ** -0.5)
            k = x_ref[rows, hk:2 * hk]
            v = x_ref[rows, 2 * hk:2 * hk + hv]
            out_ref[rows, :] = _gla_chunk(q, k, v, la_ref[rows, :], t_ref, m_ref, hm, s_ref, backward, single_split)
        return carry

    chunk_sums = [jnp.sum(ref[...].reshape(n_chunks, GLA_CHUNK, hk), axis=1) for ref in (la_f, la_b)]
    mild = jnp.minimum(jnp.min(chunk_sums[0]), jnp.min(chunk_sums[1])) > -GLA_SAFE_DECAY

    @pl.when(mild)
    def _():
        lax.fori_loop(0, n_chunks, functools.partial(step, single_split=True), 0, unroll=2)

    @pl.when(jnp.logical_not(mild))
    def _():
        lax.fori_loop(0, n_chunks, functools.partial(step, single_split=False), 0)
    if not sample:
        sf_out[...] = s_f[...]
        sb_out[...] = s_b[...]
    gain = gn_ref[...]
    for h in range(H_A):
        cols = slice(DV_A * h, DV_A * (h + 1))
        r = x_ref[:, 2 * hk + hv + DV_A * h:2 * hk + hv + DV_A * (h + 1)]
        o_ref[:, cols] = (_rms(o_f[:, cols] + o_b[:, cols]) * gain * (r * jax.nn.sigmoid(r))).astype(o_ref.dtype)


def _gla(proj, w_gf, b_gf, w_gb, b_gb, g_norm, consts, n_batch, seq, ctx=None):
    sample = ctx is not None
    hk, hv = H_A * DK_A, H_A * DV_A
    full = lambda shape: pl.BlockSpec(shape, lambda b: (0,) * len(shape))
    in_specs = [
        pl.BlockSpec((seq, 2 * hk + 2 * hv), lambda b: (b, 0)),
        pl.BlockSpec((seq, LANES), lambda b: (b, EVEN_W // LANES - 1)),
        full((LANES, hk)), full((1, hk)), full((LANES, hk)), full((1, hk)),
        full(consts[0].shape), full(consts[1].shape), full(consts[2].shape), full(consts[3].shape), full(consts[4].shape),
        full((1, DV_A)),
    ]
    args = [proj, proj, w_gf, b_gf.reshape(1, hk), w_gb, b_gb.reshape(1, hk), *consts, g_norm.reshape(1, DV_A)]
    o_spec = pl.BlockSpec((seq, hv), lambda b: (b, 0))
    o_shape = jax.ShapeDtypeStruct((n_batch * seq, hv), BF16)
    st_spec = pl.BlockSpec((None, hk, DV_A), lambda b: (b, 0, 0))
    if sample:
        in_specs += [st_spec, st_spec]
        args += [ctx[0], ctx[1]]
        out_specs, out_shape = o_spec, o_shape
    else:
        st_shape = jax.ShapeDtypeStruct((n_batch, hk, DV_A), F32)
        out_specs, out_shape = [o_spec, st_spec, st_spec], [o_shape, st_shape, st_shape]
    return pl.pallas_call(
        functools.partial(_gla_kernel, sample=sample),
        grid=(n_batch,),
        in_specs=in_specs,
        out_specs=out_specs,
        out_shape=out_shape,
        scratch_shapes=[pltpu.VMEM((seq, hk), F32), pltpu.VMEM((seq, hk), F32),
                        pltpu.VMEM((seq, hv), F32), pltpu.VMEM((seq, hv), F32),
                        pltpu.VMEM((hk, DV_A), F32), pltpu.VMEM((hk, DV_A), F32)],
        compiler_params=_params(1),
        name="gla_sample" if sample else "gla_prompt",
    )(*args)


def _axial_rope(n_tokens, dim):
    rows = n_tokens // GRID_W
    row = np.repeat(np.arange(rows), GRID_W).astype(np.float64)
    col = np.tile(np.arange(GRID_W), rows).astype(np.float64)
    n_freq = dim // 4
    inv = ROPE_THETA ** (-np.arange(n_freq) / n_freq)
    ang = np.concatenate([row[:, None] * inv, col[:, None] * inv], axis=-1)
    return np.cos(ang).astype(np.float32), np.sin(ang).astype(np.float32)


def _filter_features(n_tokens):
    t = np.linspace(0.0, 1.0, n_tokens)[:, None]
    w = 2.0 * np.pi * np.arange(n_tokens)[:, None] / n_tokens
    f = np.linspace(1e-4, FILT_BANDS - 1, FILT_BANDS)[None, :]
    z = np.concatenate([t, np.cos(f * w), -np.sin(f * w)], axis=-1)
    z = np.pad(z, ((0, 0), (0, LANES - FILT_EMB)))
    return jnp.asarray(z, F32), jnp.asarray(t, F32)


_KPE_EXPAND = np.array([(p // LANES) * (ROPE_D // 2) + p % (ROPE_D // 2) for p in range(2 * LANES)])
_QB_PERM = np.array(
    [192 * (p // NOPE_D) + p % NOPE_D for p in range(H_D * NOPE_D)]
    + [192 * (p // 32) + NOPE_D + p % 32 for p in range(H_D * 32)]
    + [192 * (p // 32) + NOPE_D + 32 + p % 32 for p in range(H_D * 32)])

EVEN_ROW_GROUPS = ((0, 0, 1536), (1568, 1536, 1024), (1536, EVEN_W - 2 * GATE_RANK, 2 * GATE_RANK))
ODD_ROW_GROUPS = ((0, 0, 1984),)


def kernel(x_prompt, x_sample, state_gla_fwd, state_gla_bwd, cache_gqa_k, cache_gqa_v, cache_mla_ckv, cache_mla_kpe, c, c_ctx, w_mod, b_mod, w_in_even, w_gla_gate_f, b_gla_gate_f, w_gla_gate_b, b_gla_gate_b, g_gla_norm, g_gqa_q, g_gqa_k, w_out_even, w_in_odd, w_hy_conv, b_hy_conv, hy_skip, w_filt1, b_filt1, filt_freq, w_filt2, b_filt2, w_filt3, g_mla_q, w_mla_qb, g_mla_kv, w_mla_kvb, w_out_odd, w_ffn_in, w_ffn_out, g_final):
    n_c, n_s = BATCH * SEQ, DEC_BATCH * DEC_SEQ
    cvec = jnp.concatenate([c_ctx[None, :], c, jnp.zeros((8 - 1 - DEC_BATCH, D_MODEL), F32)], axis=0)
    mod = _modulation(cvec, w_mod, b_mod)
    xc = x_prompt.reshape(n_c, D_MODEL)
    xs = x_sample.reshape(n_s, D_MODEL)
    rows_c, rows_s = (0, 0), (1, DEC_SEQ // MOD_ROWS)

    gla_consts = _gla_constants()
    cos_b, sin_b = _axial_rope(DEC_SEQ, HD_B)
    rope_b = (jnp.asarray(np.concatenate([cos_b, cos_b], axis=1)), jnp.asarray(np.concatenate([-sin_b, sin_b], axis=1)))
    cos_d, sin_d = _axial_rope(DEC_SEQ, ROPE_D)
    rope_d = (jnp.asarray(np.tile(cos_d, (1, H_D))), jnp.asarray(np.tile(sin_d, (1, H_D))))
    kpe_expand = jnp.asarray(np.arange(ROPE_D)[:, None] == _KPE_EXPAND[None, :], BF16)
    tabs_c, tabs_s = _dft_tables(SEQ), _dft_tables(DEC_SEQ)
    z_c, t_c = _filter_features(SEQ)
    z_s, t_s = _filter_features(DEC_SEQ)
    deltas = jnp.asarray(np.abs(np.linspace(HY_MIN_DECAY, HY_MAX_DECAY, HY_W))[None, :], F32)

    wt_even = jnp.swapaxes(w_in_even, 1, 2)
    wt_odd = jnp.swapaxes(w_in_odd, 1, 2)

    st_gf, st_gb, st_k, st_v, st_ckv, st_kpe = [], [], [], [], [], []
    for i in range(DEPTH):
        j = i // 2
        if i % 2 == 0:
            z0 = LANES - 2 * GATE_RANK
            pad_f = jnp.zeros((LANES, H_A * DK_A), F32).at[z0:z0 + GATE_RANK].set(w_gla_gate_f[j])
            pad_b = jnp.zeros((LANES, H_A * DK_A), F32).at[z0 + GATE_RANK:LANES].set(w_gla_gate_b[j])
            pc = _in_proj(xc, mod, i, wt_even, j, EVEN_ROW_GROUPS, EVEN_W, *rows_c)
            ps = _in_proj(xs, mod, i, wt_even, j, EVEN_ROW_GROUPS, EVEN_W, *rows_s)
            gate_args = (pad_f, b_gla_gate_f[j], pad_b, b_gla_gate_b[j], g_gla_norm[j], gla_consts)
            a_c, s_f, s_b = _gla(pc, *gate_args, BATCH, SEQ)
            ctx_a = (state_gla_fwd[:, j].reshape(DEC_BATCH, H_A * DK_A, DV_A),
                     state_gla_bwd[:, j].reshape(DEC_BATCH, H_A * DK_A, DV_A))
            a_s = _gla(ps, *gate_args, DEC_BATCH, DEC_SEQ, ctx=ctx_a)
            b_c, k_norm = _gqa(pc, g_gqa_q[j], g_gqa_k[j], BATCH, SEQ)
            ctx_b = (cache_gqa_k[:, j].reshape(DEC_BATCH, PAST_LEN, KV_B * HD_B),
                     cache_gqa_v[:, j].reshape(DEC_BATCH, PAST_LEN, KV_B * HD_B))
            b_s = _gqa(ps, g_gqa_q[j], g_gqa_k[j], DEC_BATCH, DEC_SEQ, ctx=ctx_b, rope=rope_b)
            w_out = w_out_even
            st_gf.append(s_f.reshape(BATCH, H_A, DK_A, DV_A))
            st_gb.append(s_b.reshape(BATCH, H_A, DK_A, DV_A))
            st_k.append(k_norm.reshape(BATCH, SEQ, KV_B, HD_B))
            st_v.append(pc[:, 2304:2560].reshape(BATCH, SEQ, KV_B, HD_B))
        else:
            pc = _in_proj(xc, mod, i, wt_odd, j, ODD_ROW_GROUPS, ODD_W, *rows_c)
            ps = _in_proj(xs, mod, i, wt_odd, j, ODD_ROW_GROUPS, ODD_W, *rows_s)
            wf1 = jnp.pad(w_filt1[j], ((0, LANES - FILT_EMB), (0, 0)))
            filt_args = (wf1, b_filt1[j], filt_freq[j], w_filt2[j], b_filt2[j], w_filt3[j])
            g_c = _filter_spectrum(z_c, *filt_args, t_c, deltas, tabs_c)
            g_s = _filter_spectrum(z_s, *filt_args, t_s, deltas, tabs_s)
            b_conv = b_hy_conv[j].reshape(1, 3 * HY_W)
            a_c = _hyena(pc, w_hy_conv[j], b_conv, hy_skip[j], g_c[0], g_c[1], tabs_c, BATCH, SEQ)
            a_s = _hyena(ps, w_hy_conv[j], b_conv, hy_skip[j], g_s[0], g_s[1], tabs_s, DEC_BATCH, DEC_SEQ)
            w_qb = w_mla_qb[j][:, _QB_PERM]
            b_c, ckv_norm = _mla(pc, g_mla_q[j], w_qb, g_mla_kv[j], w_mla_kvb[j], BATCH, SEQ, expand=kpe_expand)
            b_s = _mla(ps, g_mla_q[j], w_qb, g_mla_kv[j], w_mla_kvb[j], DEC_BATCH, DEC_SEQ,
                       ctx=(cache_mla_ckv[:, j], cache_mla_kpe[:, j]), rope=rope_d, expand=kpe_expand)
            w_out = w_out_odd
            st_ckv.append(ckv_norm.reshape(BATCH, SEQ, KV_RANK))
            st_kpe.append(pc[:, 1920:1984].reshape(BATCH, SEQ, ROPE_D))
        xc = _out_proj([a_c, b_c], w_out, j, xc, mod, i, 2, *rows_c)
        xs = _out_proj([a_s, b_s], w_out, j, xs, mod, i, 2, *rows_s)
        xc = _ffn(xc, mod, i, w_ffn_in, w_ffn_out, *rows_c)
        xs = _ffn(xs, mod, i, w_ffn_in, w_ffn_out, *rows_s)
    y_prompt = _final_norm(xc, g_final).reshape(BATCH, SEQ, D_MODEL)
    y_sample = _final_norm(xs, g_final).reshape(DEC_BATCH, DEC_SEQ, D_MODEL)
    return (y_prompt, y_sample, jnp.stack(st_gf, axis=1), jnp.stack(st_gb, axis=1), jnp.stack(st_k, axis=1),
            jnp.stack(st_v, axis=1), jnp.stack(st_ckv, axis=1), jnp.stack(st_kpe, axis=1))
```

```python
import functools
import math

import numpy as np
import jax
import jax.numpy as jnp
from jax import lax
from jax.experimental import pallas as pl
from jax.experimental.pallas import tpu as pltpu

F32 = jnp.float32
BF16 = jnp.bfloat16

D_MODEL = 1024
BATCH, SEQ = 16, 256
DEC_BATCH, DEC_SEQ = 2, 1024
DEPTH = 4
PAST_LEN = 512
GRID_W = 64
HALF_W = D_MODEL // 2
H_A, DV_A, DK_A = 4, 128, 64
GATE_RANK = 16
GLA_TAU = 16.0
GLA_CHUNK = 64
HD_B, H_B, KV_B = 128, 4, 2
HY_W = HALF_W
FILT_EMB, FILT_HID = 33, 64
FILT_BANDS = (FILT_EMB - 1) // 2
HY_MIN_DECAY = math.log(1e-2) / 1.5
HY_MAX_DECAY = math.log(1e-2) / 0.3
H_D, V_D, NOPE_D, ROPE_D = 4, 128, 128, 64
Q_RANK, KV_RANK = 256, 128
FFN_H = 2816
ROPE_THETA = 10000.0
EPS = 1e-6

LANES = 128
VMEM_LIMIT = 56 * 1024 * 1024

MOD_ROWS = 1024
TM = 1024
TM_IN = 512
TM_FFN = 2048
EVEN_W = 2688
ODD_W = 2048
FFN_TN = 256
QB = 256


def _params(n_grid):
    return pltpu.CompilerParams(dimension_semantics=("arbitrary",) * n_grid, vmem_limit_bytes=VMEM_LIMIT)


def _nt(a, b):
    return lax.dot_general(a, b, (((1,), (1,)), ((), ())), preferred_element_type=F32)


def _mm(a, b):
    return jnp.dot(a, b, preferred_element_type=F32)


def _rms(x):
    return x * lax.rsqrt(jnp.mean(x * x, axis=-1, keepdims=True) + EPS)


def _mod_kernel(c_ref, w_ref, b_ref, o_ref):
    cv = c_ref[...]
    s = cv * jax.nn.sigmoid(cv)
    o_ref[...] = _mm(s.astype(BF16), w_ref[...].astype(BF16)) + b_ref[...]


def _modulation(cvec, w_mod, b_mod):
    return pl.pallas_call(
        _mod_kernel,
        grid=(DEPTH, 6),
        in_specs=[
            pl.BlockSpec((8, D_MODEL), lambda l, n: (0, 0)),
            pl.BlockSpec((None, D_MODEL, D_MODEL), lambda l, n: (l, 0, n)),
            pl.BlockSpec((None, 1, D_MODEL), lambda l, n: (l, 0, n)),
        ],
        out_specs=pl.BlockSpec((None, None, 8, D_MODEL), lambda l, n: (l, n, 0, 0)),
        out_shape=jax.ShapeDtypeStruct((DEPTH, 6, 8, D_MODEL), F32),
        compiler_params=_params(2),
        name="adaln_mod",
    )(cvec, w_mod, b_mod.reshape(DEPTH, 1, 6 * D_MODEL))


def _mod_row(row0, rstep, tile_rows, sub):
    if tile_rows >= MOD_ROWS:
        return row0 + rstep * (pl.program_id(0) * (tile_rows // MOD_ROWS) + sub)
    return row0 + rstep * (pl.program_id(0) // (MOD_ROWS // tile_rows))


def _in_proj_kernel(x_ref, sh_ref, sc_ref, wt_ref, o_ref, keep_ref, wb_ref, *, row_groups, keep, row0, rstep):
    @pl.when(pl.program_id(0) == 0)
    def _():
        wb_ref[...] = jnp.zeros_like(wb_ref)
        for src, dst, size in row_groups:
            wb_ref[dst:dst + size, :] = wt_ref[src:src + size, :].astype(BF16)

    g = _mod_row(row0, rstep, x_ref.shape[0], 0)
    h = (_rms(x_ref[...]) * (1.0 + sc_ref[pl.ds(g, 1), :]) + sh_ref[pl.ds(g, 1), :]).astype(BF16)
    y = _nt(h, wb_ref[...])
    o_ref[...] = y.astype(o_ref.dtype)
    keep_ref[...] = y[:, keep[0]:keep[0] + keep[1]]


def _in_proj(x, mod, layer, wt, w_layer, row_groups, n, keep, row0, rstep):
    m = x.shape[0]
    return pl.pallas_call(
        functools.partial(_in_proj_kernel, row_groups=row_groups, keep=keep, row0=row0, rstep=rstep),
        grid=(m // TM_IN,),
        in_specs=[pl.BlockSpec((TM_IN, D_MODEL), lambda i: (i, 0)),
                  pl.BlockSpec((None, None, 8, D_MODEL), lambda i: (layer, 0, 0, 0)),
                  pl.BlockSpec((None, None, 8, D_MODEL), lambda i: (layer, 1, 0, 0)),
                  pl.BlockSpec((None, wt.shape[1], D_MODEL), lambda i: (w_layer, 0, 0), pipeline_mode=pl.Buffered(1))],
        out_specs=[pl.BlockSpec((TM_IN, n), lambda i: (i, 0)), pl.BlockSpec((TM_IN, keep[1]), lambda i: (i, 0))],
        out_shape=[jax.ShapeDtypeStruct((m, n), BF16), jax.ShapeDtypeStruct((m, keep[1]), F32)],
        scratch_shapes=[pltpu.VMEM((n, D_MODEL), BF16)],
        compiler_params=_params(1),
        name="norm_mod_proj",
    )(x, mod, mod, wt)


def _ffn_kernel(x_ref, sh_ref, sc_ref, gate_ref, wg_ref, wu_ref, wd_ref, o_ref, h_ref, *, row0, rstep):
    n_sub = x_ref.shape[0] // MOD_ROWS
    subs = [(slice(s * MOD_ROWS, (s + 1) * MOD_ROWS), _mod_row(row0, rstep, x_ref.shape[0], s)) for s in range(n_sub)]

    @pl.when(pl.program_id(1) == 0)
    def _():
        for rows, g in subs:
            x = x_ref[rows, :]
            o_ref[rows, :] = x
            h_ref[rows, :] = (_rms(x) * (1.0 + sc_ref[pl.ds(g, 1), :]) + sh_ref[pl.ds(g, 1), :]).astype(BF16)

    wg = wg_ref[...].astype(BF16)
    wu = wu_ref[...].astype(BF16)
    wd = wd_ref[...].astype(BF16)
    for rows, g in subs:
        h = h_ref[rows, :]
        a = _mm(h, wg)
        act = (a * jax.nn.sigmoid(a) * _mm(h, wu)).astype(BF16)
        o_ref[rows, :] += gate_ref[pl.ds(g, 1), :] * _mm(act, wd)


def _ffn(x, mod, layer, w_in, w_out, row0, rstep):
    m = x.shape[0]
    nj = FFN_H // FFN_TN
    mod_spec = lambda k: pl.BlockSpec((None, None, 8, D_MODEL), lambda i, j: (layer, k, 0, 0))
    return pl.pallas_call(
        functools.partial(_ffn_kernel, row0=row0, rstep=rstep),
        grid=(m // TM_FFN, nj),
        in_specs=[pl.BlockSpec((TM_FFN, D_MODEL), lambda i, j: (i, 0)), mod_spec(3), mod_spec(4), mod_spec(5),
                  pl.BlockSpec((None, D_MODEL, FFN_TN), lambda i, j: (layer, 0, j)),
                  pl.BlockSpec((None, D_MODEL, FFN_TN), lambda i, j: (layer, 0, j + nj)),
                  pl.BlockSpec((None, FFN_TN, D_MODEL), lambda i, j: (layer, j, 0))],
        out_specs=pl.BlockSpec((TM_FFN, D_MODEL), lambda i, j: (i, 0)),
        out_shape=jax.ShapeDtypeStruct((m, D_MODEL), F32),
        scratch_shapes=[pltpu.VMEM((TM_FFN, D_MODEL), BF16)],
        compiler_params=_params(2),
        name="ffn_residual",
    )(x, mod, mod, mod, w_in, w_in, w_out)


def _proj_res_kernel(*refs, n_act, row0, rstep):
    acts, ws = refs[:n_act], refs[n_act:2 * n_act]
    x_ref, gate_ref, o_ref = refs[2 * n_act:]
    g = row0 + rstep * pl.program_id(0)
    acc = _mm(acts[0][...], ws[0][...].astype(BF16))
    for a_ref, w_ref in zip(acts[1:], ws[1:]):
        acc = acc + _mm(a_ref[...], w_ref[...].astype(BF16))
    o_ref[...] = x_ref[...] + gate_ref[pl.ds(g, 1), :] * acc


def _out_proj(acts, w, w_layer, x, mod, layer, k_gate, row0, rstep):
    m = x.shape[0]
    n_act = len(acts)
    kw = acts[0].shape[1]
    act_specs = [pl.BlockSpec((TM, kw), lambda i: (i, 0)) for _ in acts]
    w_specs = [pl.BlockSpec((None, kw, D_MODEL), functools.partial(lambda i, p: (w_layer, p, 0), p=p),
                            pipeline_mode=pl.Buffered(1)) for p in range(n_act)]
    return pl.pallas_call(
        functools.partial(_proj_res_kernel, n_act=n_act, row0=row0, rstep=rstep),
        grid=(m // TM,),
        in_specs=act_specs + w_specs + [
            pl.BlockSpec((TM, D_MODEL), lambda i: (i, 0)),
            pl.BlockSpec((None, None, 8, D_MODEL), lambda i: (layer, k_gate, 0, 0)),
        ],
        out_specs=pl.BlockSpec((TM, D_MODEL), lambda i: (i, 0)),
        out_shape=jax.ShapeDtypeStruct((m, D_MODEL), F32),
        compiler_params=_params(1),
        name="out_proj_residual",
    )(*acts, *([w] * n_act), x, mod)


def _final_kernel(x_ref, g_ref, o_ref):
    o_ref[...] = _rms(x_ref[...]) * g_ref[...]


def _final_norm(x, g):
    m = x.shape[0]
    return pl.pallas_call(
        _final_kernel,
        grid=(m // TM,),
        in_specs=[pl.BlockSpec((TM, D_MODEL), lambda i: (i, 0)), pl.BlockSpec((1, D_MODEL), lambda i: (0, 0))],
        out_specs=pl.BlockSpec((TM, D_MODEL), lambda i: (i, 0)),
        out_shape=jax.ShapeDtypeStruct((m, D_MODEL), F32),
        compiler_params=_params(1),
        name="final_norm",
    )(x, g.reshape(1, D_MODEL))


def _softmax_pv(s, v):
    m = jnp.max(s, axis=-1, keepdims=True)
    e = jnp.exp(s - m)
    l = jnp.sum(e, axis=-1, keepdims=True)
    return _mm(e.astype(BF16), v) / l


def _gqa_kernel(*refs, sample):
    if sample:
        q_ref, k_ref, v_ref, gq_ref, gk_ref, ck_ref, cv_ref, cos_ref, sin_ref, o_ref, kb_ref, vb_ref = refs
    else:
        q_ref, k_ref, v_ref, gq_ref, gk_ref, o_ref, kn_ref, kb_ref, vb_ref = refs
    qi = pl.program_id(1)
    n_new = k_ref.shape[0]
    past = PAST_LEN if sample else 0

    @pl.when(qi == 0)
    def _():
        for g in range(KV_B):
            sl = slice(HD_B * g, HD_B * (g + 1))
            kn = _rms(k_ref[:, sl].astype(F32)) * gk_ref[...]
            if sample:
                kb_ref[0:past, sl] = ck_ref[:, sl].astype(BF16)
                vb_ref[0:past, sl] = cv_ref[:, sl].astype(BF16)
                kn = kn * cos_ref[...] + pltpu.roll(kn, HD_B // 2, 1) * sin_ref[...]
            else:
                kn_ref[:, sl] = kn
            kb_ref[past:past + n_new, sl] = kn.astype(BF16)
            vb_ref[past:past + n_new, sl] = v_ref[:, sl].astype(BF16)

    r0 = pl.multiple_of(qi * QB, QB)
    for h in range(H_B):
        g = h // (H_B // KV_B)
        gs = slice(HD_B * g, HD_B * (g + 1))
        hs = slice(HD_B * h, HD_B * (h + 1))
        qn = _rms(q_ref[:, hs].astype(F32)) * gq_ref[...]
        if sample:
            qn = qn * cos_ref[pl.ds(r0, QB), :] + pltpu.roll(qn, HD_B // 2, 1) * sin_ref[pl.ds(r0, QB), :]
        s = _nt(qn.astype(BF16), kb_ref[:, gs]) * (HD_B ** -0.5)
        o_ref[:, hs] = _softmax_pv(s, vb_ref[:, gs]).astype(o_ref.dtype)


def _gqa(proj, g_q, g_k, n_batch, seq, ctx=None, rope=None):
    sample = ctx is not None
    m = n_batch * seq
    nq = seq // QB
    in_specs = [
        pl.BlockSpec((QB, 512), lambda b, i: (b * nq + i, 3)),
        pl.BlockSpec((seq, 256), lambda b, i: (b, 8)),
        pl.BlockSpec((seq, 256), lambda b, i: (b, 9)),
        pl.BlockSpec((1, HD_B), lambda b, i: (0, 0)),
        pl.BlockSpec((1, HD_B), lambda b, i: (0, 0)),
    ]
    args = [proj, proj, proj, g_q.reshape(1, HD_B), g_k.reshape(1, HD_B)]
    o_spec = pl.BlockSpec((QB, 512), lambda b, i: (b * nq + i, 0))
    o_shape = jax.ShapeDtypeStruct((m, 512), BF16)
    if sample:
        in_specs += [
            pl.BlockSpec((None, PAST_LEN, 256), lambda b, i: (b, 0, 0)),
            pl.BlockSpec((None, PAST_LEN, 256), lambda b, i: (b, 0, 0)),
            pl.BlockSpec((seq, HD_B), lambda b, i: (0, 0)),
            pl.BlockSpec((seq, HD_B), lambda b, i: (0, 0)),
        ]
        args += [ctx[0], ctx[1], rope[0], rope[1]]
        out_specs, out_shape = o_spec, o_shape
    else:
        out_specs = [o_spec, pl.BlockSpec((seq, 256), lambda b, i: (b, 0))]
        out_shape = [o_shape, jax.ShapeDtypeStruct((m, 256), F32)]
    n_keys = seq + (PAST_LEN if sample else 0)
    return pl.pallas_call(
        functools.partial(_gqa_kernel, sample=sample),
        grid=(n_batch, nq),
        in_specs=in_specs,
        out_specs=out_specs,
        out_shape=out_shape,
        scratch_shapes=[pltpu.VMEM((n_keys, 256), BF16), pltpu.VMEM((n_keys, 256), BF16)],
        compiler_params=_params(2),
        name="gqa_sample" if sample else "gqa_prompt",
    )(*args)


def _mla_kernel(*refs, sample):
    if sample:
        (cq_ref, ckv_ref, kpe_ref, gq_ref, wqb_ref, gkv_ref, wkvb_ref, ex_ref, cckv_ref, ckpe_ref, c4_ref, s4_ref,
         o_ref, kv_s, kx_s) = refs
    else:
        cq_ref, ckv_ref, kpe_ref, gq_ref, wqb_ref, gkv_ref, wkvb_ref, ex_ref, o_ref, ckvn_ref, kv_s, kx_s = refs
    qi = pl.program_id(1)
    n_new = ckv_ref.shape[0]
    past = PAST_LEN if sample else 0

    def rope(x, c, s):
        x1, x2 = x[:, :LANES], x[:, LANES:]
        return jnp.concatenate([x1 * c - x2 * s, x1 * s + x2 * c], axis=1)

    @pl.when(qi == 0)
    def _():
        wkvb = wkvb_ref[...].astype(BF16)
        ckvn = _rms(ckv_ref[...].astype(F32)) * gkv_ref[...]
        if not sample:
            ckvn_ref[...] = ckvn
        kv_s[past:past + n_new, :] = _mm(ckvn.astype(BF16), wkvb).astype(BF16)
        kx = _mm(kpe_ref[:, 0:ROPE_D], ex_ref[...])
        if sample:
            kv_s[0:past, :] = _mm(cckv_ref[...].astype(BF16), wkvb).astype(BF16)
            kx_s[0:past, :] = _mm(ckpe_ref[...].astype(BF16), ex_ref[...]).astype(BF16)
            kx = rope(kx, c4_ref[...], s4_ref[...])
        kx_s[past:past + n_new, :] = kx.astype(BF16)

    q = _mm((_rms(cq_ref[...].astype(F32)) * gq_ref[...]).astype(BF16), wqb_ref[...].astype(BF16))
    qpe = q[:, 4 * NOPE_D:]
    if sample:
        r0 = pl.multiple_of(qi * QB, QB)
        qpe = rope(qpe, c4_ref[pl.ds(r0, QB), :], s4_ref[pl.ds(r0, QB), :])
    lane_head = (lax.broadcasted_iota(jnp.int32, (1, 2 * LANES), 1) % LANES) // (ROPE_D // 2)
    scale = (NOPE_D + ROPE_D) ** -0.5
    for h in range(H_D):
        qm = jnp.where(lane_head == h, qpe, 0.0)
        s = _nt(q[:, NOPE_D * h:NOPE_D * (h + 1)].astype(BF16), kv_s[:, 256 * h:256 * h + NOPE_D])
        s = (s + _nt(qm.astype(BF16), kx_s[...])) * scale
        o_ref[:, V_D * h:V_D * (h + 1)] = _softmax_pv(s, kv_s[:, 256 * h + NOPE_D:256 * (h + 1)]).astype(o_ref.dtype)


def _mla(proj, g_q, w_qb, g_kv, w_kvb, n_batch, seq, ctx=None, rope=None, expand=None):
    sample = ctx is not None
    m = n_batch * seq
    nq = seq // QB
    in_specs = [
        pl.BlockSpec((QB, Q_RANK), lambda b, i: (b * nq + i, 6)),
        pl.BlockSpec((seq, KV_RANK), lambda b, i: (b, 14)),
        pl.BlockSpec((seq, LANES), lambda b, i: (b, 15)),
        pl.BlockSpec((1, Q_RANK), lambda b, i: (0, 0)),
        pl.BlockSpec((Q_RANK, 768), lambda b, i: (0, 0)),
        pl.BlockSpec((1, KV_RANK), lambda b, i: (0, 0)),
        pl.BlockSpec((KV_RANK, 1024), lambda b, i: (0, 0)),
        pl.BlockSpec((ROPE_D, 256), lambda b, i: (0, 0)),
    ]
    args = [proj, proj, proj, g_q.reshape(1, Q_RANK), w_qb, g_kv.reshape(1, KV_RANK), w_kvb, expand]
    o_spec = pl.BlockSpec((QB, 512), lambda b, i: (b * nq + i, 0))
    o_shape = jax.ShapeDtypeStruct((m, 512), BF16)
    if sample:
        in_specs += [
            pl.BlockSpec((None, PAST_LEN, KV_RANK), lambda b, i: (b, 0, 0)),
            pl.BlockSpec((None, PAST_LEN, ROPE_D), lambda b, i: (b, 0, 0)),
            pl.BlockSpec((seq, LANES), lambda b, i: (0, 0)),
            pl.BlockSpec((seq, LANES), lambda b, i: (0, 0)),
        ]
        args += [ctx[0], ctx[1], rope[0], rope[1]]
        out_specs, out_shape = o_spec, o_shape
    else:
        out_specs = [o_spec, pl.BlockSpec((seq, KV_RANK), lambda b, i: (b, 0))]
        out_shape = [o_shape, jax.ShapeDtypeStruct((m, KV_RANK), F32)]
    n_keys = seq + (PAST_LEN if sample else 0)
    return pl.pallas_call(
        functools.partial(_mla_kernel, sample=sample),
        grid=(n_batch, nq),
        in_specs=in_specs,
        out_specs=out_specs,
        out_shape=out_shape,
        scratch_shapes=[pltpu.VMEM((n_keys, 1024), BF16), pltpu.VMEM((n_keys, 256), BF16)],
        compiler_params=_params(2),
        name="mla_sample" if sample else "mla_prompt",
    )(*args)


def _split_bf16(x):
    hi = x.astype(BF16)
    return hi, (x - hi.astype(F32)).astype(BF16)


def _dft(table, x):
    t_hi, t_lo = _split_bf16(table)
    x_hi, x_lo = _split_bf16(x)
    return _mm(t_hi, x_hi) + _mm(t_hi, x_lo) + _mm(t_lo, x_hi)


def _filter_kernel(z_ref, wf1_ref, bf1_ref, fr_ref, wf2_ref, bf2_ref, wf3_ref, t_ref, dl_ref,
                   c_ref, s_ref, gre_ref, gim_ref):
    n_tok = z_ref.shape[0]
    fr = fr_ref[...]
    hid = jnp.sin(fr * (_mm(z_ref[...].astype(BF16), wf1_ref[...].astype(BF16)) + bf1_ref[...]))
    hid = jnp.sin(fr * (_mm(hid.astype(BF16), wf2_ref[...].astype(BF16)) + bf2_ref[...]))
    filt = _mm(hid.astype(BF16), wf3_ref[...].astype(BF16))
    decay = jnp.exp(-t_ref[...] * dl_ref[...])
    row = lax.broadcasted_iota(jnp.int32, (n_tok, 1), 0)
    h_f = filt[:, :HY_W] * decay
    h_b = jnp.where(row == 0, 0.0, filt[:, HY_W:] * decay)
    p, m = h_f + h_b, h_f - h_b
    g_re = _dft(c_ref[...], p)
    g_im = _dft(s_ref[...], m)
    sign = jnp.where(row % 2 == 0, 1.0, -1.0)
    nyquist = jnp.sum(p * sign, axis=0, keepdims=True)
    g_im = jnp.where(row == 0, nyquist, g_im)
    wk = jnp.where(row == 0, 0.5 / n_tok, 1.0 / n_tok)
    gre_ref[...] = g_re * wk
    gim_ref[...] = g_im * wk


def _filter_spectrum(z, wf1, bf1, freq, wf2, bf2, wf3, t_col, deltas, tabs):
    n_tok = z.shape[0]
    out = jax.ShapeDtypeStruct((n_tok, HY_W), F32)
    return pl.pallas_call(
        _filter_kernel,
        out_shape=[out, out],
        compiler_params=pltpu.CompilerParams(vmem_limit_bytes=VMEM_LIMIT),
        name="hyena_filter",
    )(z, wf1, bf1.reshape(1, FILT_HID), freq.reshape(1, FILT_HID), wf2, bf2.reshape(1, FILT_HID), wf3,
      t_col, deltas, tabs[0], tabs[1])


HY_CT = 256


HY_ROWS = 1024


def _hyena_kernel(u0_ref, u1_ref, u2_ref, w0_ref, w1_ref, w2_ref, b0_ref, b1_ref, b2_ref, skip_ref,
                  gre_ref, gim_ref, cf_ref, sf_ref, stf_ref, o_ref, c_ref, s_ref, st_ref):
    seq = c_ref.shape[0]
    n_rows = u0_ref.shape[0]
    n_seq = n_rows // seq
    pos = lax.broadcasted_iota(jnp.int32, (n_rows, 1), 0) % seq

    @pl.when((pl.program_id(0) == 0) & (pl.program_id(1) == 0))
    def _():
        c_ref[...] = cf_ref[...].astype(BF16)
        s_ref[...] = sf_ref[...].astype(BF16)
        st_ref[...] = stf_ref[...].astype(BF16)

    def short_conv(u_ref, w_ref, b_ref):
        x, w = u_ref[...].astype(F32), w_ref[...]
        prev = jnp.where(pos == 0, 0.0, pltpu.roll(x, 1, 0))
        nxt = jnp.where(pos == seq - 1, 0.0, pltpu.roll(x, n_rows - 1, 0))
        return prev * w[0:1] + x * w[1:2] + nxt * w[2:3] + b_ref[...]

    def side_by_side(a):
        return a if n_seq == 1 else jnp.concatenate([a[s * seq:(s + 1) * seq] for s in range(n_seq)], axis=1)

    def stacked(a):
        ct = a.shape[1] // n_seq
        return a if n_seq == 1 else jnp.concatenate([a[:, s * ct:(s + 1) * ct] for s in range(n_seq)], axis=0)

    x0 = short_conv(u0_ref, w0_ref, b0_ref)
    gv = short_conv(u1_ref, w1_ref, b1_ref) * short_conv(u2_ref, w2_ref, b2_ref)
    sig = side_by_side(gv).astype(BF16)
    u_re = _mm(c_ref[...], sig)
    u_im = _mm(s_ref[...], sig)
    g_re = jnp.concatenate([gre_ref[...]] * n_seq, axis=1)
    g_im = jnp.concatenate([gim_ref[...]] * n_seq, axis=1)
    bin0 = lax.broadcasted_iota(jnp.int32, (seq, 1), 0) == 0
    p_im = u_im * g_im
    y_re = u_re * g_re - jnp.where(bin0, 0.0, p_im)
    y_im = jnp.where(bin0, p_im, u_re * g_im + u_im * g_re)
    y = stacked(_mm(c_ref[...], y_re.astype(BF16)) + _mm(st_ref[...], y_im.astype(BF16)))
    o_ref[...] = (x0 * (y + gv * skip_ref[...])).astype(o_ref.dtype)


def _hyena(proj, w_conv, b_conv, skip, g_re, g_im, tabs, n_batch, seq):
    nct = HY_W // HY_CT
    u_specs = [pl.BlockSpec((HY_ROWS, HY_CT), functools.partial(lambda b, c, g: (b, g * nct + c), g=g)) for g in range(3)]
    w_specs = [pl.BlockSpec((3, HY_CT), functools.partial(lambda b, c, g: (0, g * nct + c), g=g)) for g in range(3)]
    b_specs = [pl.BlockSpec((1, HY_CT), functools.partial(lambda b, c, g: (0, g * nct + c), g=g)) for g in range(3)]
    tab_spec = pl.BlockSpec((seq, seq), lambda b, c: (0, 0))
    return pl.pallas_call(
        _hyena_kernel,
        grid=(n_batch * seq // HY_ROWS, nct),
        in_specs=u_specs + w_specs + b_specs + [
            pl.BlockSpec((1, HY_CT), lambda b, c: (0, c)),
            pl.BlockSpec((seq, HY_CT), lambda b, c: (0, c)),
            pl.BlockSpec((seq, HY_CT), lambda b, c: (0, c)),
        ] + [tab_spec] * 3,
        out_specs=pl.BlockSpec((HY_ROWS, HY_CT), lambda b, c: (b, c)),
        out_shape=jax.ShapeDtypeStruct((n_batch * seq, HY_W), BF16),
        scratch_shapes=[pltpu.VMEM((seq, seq), BF16)] * 3,
        compiler_params=_params(2),
        name="hyena_conv",
    )(proj, proj, proj, w_conv, w_conv, w_conv, b_conv, b_conv, b_conv, skip.reshape(1, HY_W), g_re, g_im, *tabs)


def _dft_tables(n_tok):
    k = np.arange(n_tok)[:, None]
    s = np.arange(n_tok)[None, :]
    ang = ((k * s) % (2 * n_tok)) * (np.pi / n_tok)
    cos_t = np.cos(ang)
    sin_f = np.where(k == 0, np.where(s % 2 == 0, 1.0, -1.0), -np.sin(ang))
    return [jnp.asarray(t, F32) for t in (cos_t, sin_f, sin_f.T)]


GLA_LEVELS = (32, 16, 8, 4, 2, 1)
GLA_SAFE_DECAY = 60.0


def _gla_constants():
    c = GLA_CHUNK
    idx = np.arange(c)
    i, t = idx[:, None], idx[None, :]
    masks = []
    for s in GLA_LEVELS:
        upper = (idx % (2 * s)) >= s
        masks.append(((i // (2 * s)) == (t // (2 * s))) & upper[:, None] & (~upper)[None, :])
    masks.append(i == t)
    tri = t <= i
    fwd_m = np.stack([np.tile(m, (H_A, 1)) for m in masks]).astype(np.float32)
    bwd_m = np.stack([np.tile(m[::-1, ::-1], (H_A, 1)) for m in masks]).astype(np.float32)
    head_of_row = np.repeat(np.arange(H_A), c)[:, None]
    head_of_lane = np.repeat(np.arange(H_A), DK_A)[None, :]
    head_mask = head_of_row == head_of_lane
    return (jnp.asarray(tri, BF16), jnp.asarray(tri[::-1, ::-1], BF16), jnp.asarray(fwd_m), jnp.asarray(bwd_m),
            jnp.asarray(head_mask, BF16))


def _pair_reference(b, s, backward, row):
    c = GLA_CHUNK
    ref = s if backward else s - 1
    if 2 * s >= 8:
        pieces = [jnp.broadcast_to(b[p * 2 * s + ref:p * 2 * s + ref + 1, :], (2 * s, b.shape[1]))
                  for p in range(c // (2 * s))]
        return pieces[0] if len(pieces) == 1 else jnp.concatenate(pieces, axis=0)
    pos = row % (2 * s)
    out = None
    for o in range(2 * s):
        d = ref - o
        shifted = b if d == 0 else pltpu.roll(b, (-d) % c, 0)
        out = shifted if out is None else jnp.where(pos == o, shifted, out)
    return out


def _gla_chunk(q, k, v, la, t_ref, m_ref, hm, s_ref, backward, single_split):
    c = GLA_CHUNK
    l1 = la.astype(BF16)
    r1 = la - l1.astype(F32)
    l2 = r1.astype(BF16)
    l3 = (r1 - l2.astype(F32)).astype(BF16)
    tmat = t_ref[...]
    b = _mm(tmat, l1) + _mm(tmat, l2) + _mm(tmat, l3)
    row = lax.broadcasted_iota(jnp.int32, (c, 1), 0)
    last = 0 if backward else c - 1
    b_last = b[last:last + 1, :]

    def stack_heads(a):
        ab = a.astype(BF16)
        return jnp.concatenate([ab] * H_A, axis=0) * hm

    q_decayed = stack_heads(q * jnp.exp(b))
    if single_split:
        scores = _nt(q_decayed, (k * jnp.exp(-b)).astype(BF16)) * jnp.sum(m_ref[...], axis=0)
    else:
        scores = _nt(stack_heads(q), k.astype(BF16)) * m_ref[len(GLA_LEVELS)]
        for lvl, s in enumerate(GLA_LEVELS):
            is_query = (row % (2 * s) < s) if backward else (row % (2 * s) >= s)
            delta = b - _pair_reference(b, s, backward, row)
            x = jnp.exp(jnp.where(is_query, delta, -delta))
            scores = scores + _nt(stack_heads(q * x), (k * x).astype(BF16)) * m_ref[lvl]
    scores = scores.astype(BF16)
    state = s_ref[...]
    inter = _mm(q_decayed, state.astype(BF16))
    k_rest = (k * jnp.exp(b_last - b)).T
    carry = jnp.broadcast_to(jnp.exp(b_last), (2 * c, b.shape[1])).T
    outs = []
    for h in range(H_A):
        rows = slice(c * h, c * (h + 1))
        v_h = v[:, DV_A * h:DV_A * (h + 1)].astype(BF16)
        outs.append(_mm(scores[rows], v_h) + inter[rows])
        s_ref[rows, :] = state[rows] * carry[rows] + _mm(k_rest[rows].astype(BF16), v_h)
    return jnp.concatenate(outs, axis=1)


def _gla_kernel(*refs, sample):
    if sample:
        (x_ref, z_ref, wf_ref, bf_ref, wb_ref, bb_ref, tf_ref, tb_ref, mf_ref, mb_ref, hm_ref, gn_ref, sf0_ref, sb0_ref,
         o_ref, la_f, la_b, o_f, o_b, s_f, s_b) = refs
    else:
        (x_ref, z_ref, wf_ref, bf_ref, wb_ref, bb_ref, tf_ref, tb_ref, mf_ref, mb_ref, hm_ref, gn_ref,
         o_ref, sf_out, sb_out, la_f, la_b, o_f, o_b, s_f, s_b) = refs
    n_tok = x_ref.shape[0]
    n_chunks = n_tok // GLA_CHUNK
    hk, hv = H_A * DK_A, H_A * DV_A
    zb = z_ref[...].astype(BF16)
    la_f[...] = jax.nn.log_sigmoid(_mm(zb, wf_ref[...].astype(BF16)) + bf_ref[...]) / GLA_TAU
    la_b[...] = jax.nn.log_sigmoid(_mm(zb, wb_ref[...].astype(BF16)) + bb_ref[...]) / GLA_TAU
    if sample:
        s_f[...] = sf0_ref[...]
        s_b[...] = sb0_ref[...]
    else:
        s_f[...] = jnp.zeros_like(s_f)
        s_b[...] = jnp.zeros_like(s_b)
    hm = hm_ref[...]

    def step(ci, carry, single_split):
        for la_ref, t_ref, m_ref, s_ref, out_ref, cidx, backward in (
                (la_f, tf_ref, mf_ref, s_f, o_f, ci, False), (la_b, tb_ref, mb_ref, s_b, o_b, n_chunks - 1 - ci, True)):
            rows = pl.ds(pl.multiple_of(cidx * GLA_CHUNK, GLA_CHUNK), GLA_CHUNK)
            q = x_ref[rows, 0:hk].astype(F32) * (DK_A ** -0.5)
            k = x_ref[rows, hk:2 * hk].astype(F32)
            v = x_ref[rows, 2 * hk:2 * hk + hv]
            out_ref[rows, :] = _gla_chunk(q, k, v, la_ref[rows, :], t_ref, m_ref, hm, s_ref, backward, single_split)
        return carry

    chunk_sums = [jnp.sum(ref[...].reshape(n_chunks, GLA_CHUNK, hk), axis=1) for ref in (la_f, la_b)]
    mild = jnp.minimum(jnp.min(chunk_sums[0]), jnp.min(chunk_sums[1])) > -GLA_SAFE_DECAY

    @pl.when(mild)
    def _():
        lax.fori_loop(0, n_chunks, functools.partial(step, single_split=True), 0, unroll=2)

    @pl.when(jnp.logical_not(mild))
    def _():
        lax.fori_loop(0, n_chunks, functools.partial(step, single_split=False), 0)
    if not sample:
        sf_out[...] = s_f[...]
        sb_out[...] = s_b[...]
    gain = gn_ref[...]
    for h in range(H_A):
        cols = slice(DV_A * h, DV_A * (h + 1))
        r = x_ref[:, 2 * hk + hv + DV_A * h:2 * hk + hv + DV_A * (h + 1)].astype(F32)
        o_ref[:, cols] = (_rms(o_f[:, cols] + o_b[:, cols]) * gain * (r * jax.nn.sigmoid(r))).astype(o_ref.dtype)


def _gla(proj, w_gf, b_gf, w_gb, b_gb, g_norm, consts, n_batch, seq, ctx=None):
    sample = ctx is not None
    hk, hv = H_A * DK_A, H_A * DV_A
    full = lambda shape: pl.BlockSpec(shape, lambda b: (0,) * len(shape))
    in_specs = [
        pl.BlockSpec((seq, 2 * hk + 2 * hv), lambda b: (b, 0)),
        pl.BlockSpec((seq, LANES), lambda b: (b, EVEN_W // LANES - 1)),
        full((LANES, hk)), full((1, hk)), full((LANES, hk)), full((1, hk)),
        full(consts[0].shape), full(consts[1].shape), full(consts[2].shape), full(consts[3].shape), full(consts[4].shape),
        full((1, DV_A)),
    ]
    args = [proj, proj, w_gf, b_gf.reshape(1, hk), w_gb, b_gb.reshape(1, hk), *consts, g_norm.reshape(1, DV_A)]
    o_spec = pl.BlockSpec((seq, hv), lambda b: (b, 0))
    o_shape = jax.ShapeDtypeStruct((n_batch * seq, hv), BF16)
    st_spec = pl.BlockSpec((None, hk, DV_A), lambda b: (b, 0, 0))
    if sample:
        in_specs += [st_spec, st_spec]
        args += [ctx[0], ctx[1]]
        out_specs, out_shape = o_spec, o_shape
    else:
        st_shape = jax.ShapeDtypeStruct((n_batch, hk, DV_A), F32)
        out_specs, out_shape = [o_spec, st_spec, st_spec], [o_shape, st_shape, st_shape]
    return pl.pallas_call(
        functools.partial(_gla_kernel, sample=sample),
        grid=(n_batch,),
        in_specs=in_specs,
        out_specs=out_specs,
        out_shape=out_shape,
        scratch_shapes=[pltpu.VMEM((seq, hk), F32), pltpu.VMEM((seq, hk), F32),
                        pltpu.VMEM((seq, hv), F32), pltpu.VMEM((seq, hv), F32),
                        pltpu.VMEM((hk, DV_A), F32), pltpu.VMEM((hk, DV_A), F32)],
        compiler_params=_params(1),
        name="gla_sample" if sample else "gla_prompt",
    )(*args)


def _axial_rope(n_tokens, dim):
    rows = n_tokens // GRID_W
    row = np.repeat(np.arange(rows), GRID_W).astype(np.float64)
    col = np.tile(np.arange(GRID_W), rows).astype(np.float64)
    n_freq = dim // 4
    inv = ROPE_THETA ** (-np.arange(n_freq) / n_freq)
    ang = np.concatenate([row[:, None] * inv, col[:, None] * inv], axis=-1)
    return np.cos(ang).astype(np.float32), np.sin(ang).astype(np.float32)


def _filter_features(n_tokens):
    t = np.linspace(0.0, 1.0, n_tokens)[:, None]
    w = 2.0 * np.pi * np.arange(n_tokens)[:, None] / n_tokens
    f = np.linspace(1e-4, FILT_BANDS - 1, FILT_BANDS)[None, :]
    z = np.concatenate([t, np.cos(f * w), -np.sin(f * w)], axis=-1)
    z = np.pad(z, ((0, 0), (0, LANES - FILT_EMB)))
    return jnp.asarray(z, F32), jnp.asarray(t, F32)


_KPE_EXPAND = np.array([(p // LANES) * (ROPE_D // 2) + p % (ROPE_D // 2) for p in range(2 * LANES)])
_QB_PERM = np.array(
    [192 * (p // NOPE_D) + p % NOPE_D for p in range(H_D * NOPE_D)]
    + [192 * (p // 32) + NOPE_D + p % 32 for p in range(H_D * 32)]
    + [192 * (p // 32) + NOPE_D + 32 + p % 32 for p in range(H_D * 32)])

EVEN_ROW_GROUPS = ((0, 0, 1536), (1568, 1536, 1024), (1536, EVEN_W - 2 * GATE_RANK, 2 * GATE_RANK))
ODD_ROW_GROUPS = ((0, 0, 1984),)
EVEN_KEEP = (2304, 256)
ODD_KEEP = (1920, LANES)


def kernel(x_prompt, x_sample, state_gla_fwd, state_gla_bwd, cache_gqa_k, cache_gqa_v, cache_mla_ckv, cache_mla_kpe, c, c_ctx, w_mod, b_mod, w_in_even, w_gla_gate_f, b_gla_gate_f, w_gla_gate_b, b_gla_gate_b, g_gla_norm, g_gqa_q, g_gqa_k, w_out_even, w_in_odd, w_hy_conv, b_hy_conv, hy_skip, w_filt1, b_filt1, filt_freq, w_filt2, b_filt2, w_filt3, g_mla_q, w_mla_qb, g_mla_kv, w_mla_kvb, w_out_odd, w_ffn_in, w_ffn_out, g_final):
    n_c, n_s = BATCH * SEQ, DEC_BATCH * DEC_SEQ
    cvec = jnp.concatenate([c_ctx[None, :], c, jnp.zeros((8 - 1 - DEC_BATCH, D_MODEL), F32)], axis=0)
    mod = _modulation(cvec, w_mod, b_mod)
    xc = x_prompt.reshape(n_c, D_MODEL)
    xs = x_sample.reshape(n_s, D_MODEL)
    rows_c, rows_s = (0, 0), (1, DEC_SEQ // MOD_ROWS)

    gla_consts = _gla_constants()
    cos_b, sin_b = _axial_rope(DEC_SEQ, HD_B)
    rope_b = (jnp.asarray(np.concatenate([cos_b, cos_b], axis=1)), jnp.asarray(np.concatenate([-sin_b, sin_b], axis=1)))
    cos_d, sin_d = _axial_rope(DEC_SEQ, ROPE_D)
    rope_d = (jnp.asarray(np.tile(cos_d, (1, H_D))), jnp.asarray(np.tile(sin_d, (1, H_D))))
    kpe_expand = jnp.asarray(np.arange(ROPE_D)[:, None] == _KPE_EXPAND[None, :], BF16)
    tabs_c, tabs_s = _dft_tables(SEQ), _dft_tables(DEC_SEQ)
    z_c, t_c = _filter_features(SEQ)
    z_s, t_s = _filter_features(DEC_SEQ)
    deltas = jnp.asarray(np.abs(np.linspace(HY_MIN_DECAY, HY_MAX_DECAY, HY_W))[None, :], F32)

    wt_even = jnp.swapaxes(w_in_even, 1, 2)
    wt_odd = jnp.swapaxes(w_in_odd, 1, 2)

    st_gf, st_gb, st_k, st_v, st_ckv, st_kpe = [], [], [], [], [], []
    for i in range(DEPTH):
        j = i // 2
        if i % 2 == 0:
            z0 = LANES - 2 * GATE_RANK
            pad_f = jnp.zeros((LANES, H_A * DK_A), F32).at[z0:z0 + GATE_RANK].set(w_gla_gate_f[j])
            pad_b = jnp.zeros((LANES, H_A * DK_A), F32).at[z0 + GATE_RANK:LANES].set(w_gla_gate_b[j])
            pc, v_new = _in_proj(xc, mod, i, wt_even, j, EVEN_ROW_GROUPS, EVEN_W, EVEN_KEEP, *rows_c)
            ps, _ = _in_proj(xs, mod, i, wt_even, j, EVEN_ROW_GROUPS, EVEN_W, EVEN_KEEP, *rows_s)
            gate_args = (pad_f, b_gla_gate_f[j], pad_b, b_gla_gate_b[j], g_gla_norm[j], gla_consts)
            a_c, s_f, s_b = _gla(pc, *gate_args, BATCH, SEQ)
            ctx_a = (state_gla_fwd[:, j].reshape(DEC_BATCH, H_A * DK_A, DV_A),
                     state_gla_bwd[:, j].reshape(DEC_BATCH, H_A * DK_A, DV_A))
            a_s = _gla(ps, *gate_args, DEC_BATCH, DEC_SEQ, ctx=ctx_a)
            b_c, k_norm = _gqa(pc, g_gqa_q[j], g_gqa_k[j], BATCH, SEQ)
            ctx_b = (cache_gqa_k[:, j].reshape(DEC_BATCH, PAST_LEN, KV_B * HD_B),
                     cache_gqa_v[:, j].reshape(DEC_BATCH, PAST_LEN, KV_B * HD_B))
            b_s = _gqa(ps, g_gqa_q[j], g_gqa_k[j], DEC_BATCH, DEC_SEQ, ctx=ctx_b, rope=rope_b)
            w_out = w_out_even
            st_gf.append(s_f.reshape(BATCH, H_A, DK_A, DV_A))
            st_gb.append(s_b.reshape(BATCH, H_A, DK_A, DV_A))
            st_k.append(k_norm.reshape(BATCH, SEQ, KV_B, HD_B))
            st_v.append(v_new.reshape(BATCH, SEQ, KV_B, HD_B))
        else:
            pc, kpe_new = _in_proj(xc, mod, i, wt_odd, j, ODD_ROW_GROUPS, ODD_W, ODD_KEEP, *rows_c)
            ps, _ = _in_proj(xs, mod, i, wt_odd, j, ODD_ROW_GROUPS, ODD_W, ODD_KEEP, *rows_s)
            wf1 = jnp.pad(w_filt1[j], ((0, LANES - FILT_EMB), (0, 0)))
            filt_args = (wf1, b_filt1[j], filt_freq[j], w_filt2[j], b_filt2[j], w_filt3[j])
            g_c = _filter_spectrum(z_c, *filt_args, t_c, deltas, tabs_c)
            g_s = _filter_spectrum(z_s, *filt_args, t_s, deltas, tabs_s)
            b_conv = b_hy_conv[j].reshape(1, 3 * HY_W)
            a_c = _hyena(pc, w_hy_conv[j], b_conv, hy_skip[j], g_c[0], g_c[1], tabs_c, BATCH, SEQ)
            a_s = _hyena(ps, w_hy_conv[j], b_conv, hy_skip[j], g_s[0], g_s[1], tabs_s, DEC_BATCH, DEC_SEQ)
            w_qb = w_mla_qb[j][:, _QB_PERM]
            b_c, ckv_norm = _mla(pc, g_mla_q[j], w_qb, g_mla_kv[j], w_mla_kvb[j], BATCH, SEQ, expand=kpe_expand)
            b_s = _mla(ps, g_mla_q[j], w_qb, g_mla_kv[j], w_mla_kvb[j], DEC_BATCH, DEC_SEQ,
                       ctx=(cache_mla_ckv[:, j], cache_mla_kpe[:, j]), rope=rope_d, expand=kpe_expand)
            w_out = w_out_odd
            st_ckv.append(ckv_norm.reshape(BATCH, SEQ, KV_RANK))
            st_kpe.append(kpe_new[:, :ROPE_D].reshape(BATCH, SEQ, ROPE_D))
        xc = _out_proj([a_c, b_c], w_out, j, xc, mod, i, 2, *rows_c)
        xs = _out_proj([a_s, b_s], w_out, j, xs, mod, i, 2, *rows_s)
        xc = _ffn(xc, mod, i, w_ffn_in, w_ffn_out, *rows_c)
        xs = _ffn(xs, mod, i, w_ffn_in, w_ffn_out, *rows_s)
    y_prompt = _final_norm(xc, g_final).reshape(BATCH, SEQ, D_MODEL)
    y_sample = _final_norm(xs, g_final).reshape(DEC_BATCH, DEC_SEQ, D_MODEL)
    return (y_prompt, y_sample, jnp.stack(st_gf, axis=1), jnp.stack(st_gb, axis=1), jnp.stack(st_k, axis=1),
            jnp.stack(st_v, axis=1), jnp.stack(st_ckv, axis=1), jnp.stack(st_kpe, axis=1))
```

```python
import functools
import math

import numpy as np
import jax
import jax.numpy as jnp
from jax import lax
from jax.experimental import pallas as pl
from jax.experimental.pallas import tpu as pltpu

F32 = jnp.float32
BF16 = jnp.bfloat16

D_MODEL = 1024
BATCH, SEQ = 16, 256
DEC_BATCH, DEC_SEQ = 2, 1024
DEPTH = 4
PAST_LEN = 512
GRID_W = 64
HALF_W = D_MODEL // 2
H_A, DV_A, DK_A = 4, 128, 64
GATE_RANK = 16
GLA_TAU = 16.0
GLA_CHUNK = 64
HD_B, H_B, KV_B = 128, 4, 2
HY_W = HALF_W
FILT_EMB, FILT_HID = 33, 64
FILT_BANDS = (FILT_EMB - 1) // 2
HY_MIN_DECAY = math.log(1e-2) / 1.5
HY_MAX_DECAY = math.log(1e-2) / 0.3
H_D, V_D, NOPE_D, ROPE_D = 4, 128, 128, 64
Q_RANK, KV_RANK = 256, 128
FFN_H = 2816
ROPE_THETA = 10000.0
EPS = 1e-6

LANES = 128
VMEM_LIMIT = 56 * 1024 * 1024

MOD_ROWS = 1024
TM = 1024
TM_IN = 512
TM_FFN = 2048
EVEN_W = 2688
ODD_W = 2048
FFN_TN = 256
QB = 256


def _params(n_grid):
    return pltpu.CompilerParams(dimension_semantics=("arbitrary",) * n_grid, vmem_limit_bytes=VMEM_LIMIT)


def _nt(a, b):
    return lax.dot_general(a, b, (((1,), (1,)), ((), ())), preferred_element_type=F32)


def _mm(a, b):
    return jnp.dot(a, b, preferred_element_type=F32)


def _rms(x):
    return x * lax.rsqrt(jnp.mean(x * x, axis=-1, keepdims=True) + EPS)


def _mod_kernel(c_ref, w_ref, b_ref, o_ref):
    cv = c_ref[...]
    s = cv * jax.nn.sigmoid(cv)
    o_ref[...] = _mm(s.astype(BF16), w_ref[...].astype(BF16)) + b_ref[...]


def _modulation(cvec, w_mod, b_mod):
    return pl.pallas_call(
        _mod_kernel,
        grid=(DEPTH, 6),
        in_specs=[
            pl.BlockSpec((8, D_MODEL), lambda l, n: (0, 0)),
            pl.BlockSpec((None, D_MODEL, D_MODEL), lambda l, n: (l, 0, n)),
            pl.BlockSpec((None, 1, D_MODEL), lambda l, n: (l, 0, n)),
        ],
        out_specs=pl.BlockSpec((None, None, 8, D_MODEL), lambda l, n: (l, n, 0, 0)),
        out_shape=jax.ShapeDtypeStruct((DEPTH, 6, 8, D_MODEL), F32),
        compiler_params=_params(2),
        name="adaln_mod",
    )(cvec, w_mod, b_mod.reshape(DEPTH, 1, 6 * D_MODEL))


def _mod_row(row0, rstep, tile_rows, sub):
    if tile_rows >= MOD_ROWS:
        return row0 + rstep * (pl.program_id(0) * (tile_rows // MOD_ROWS) + sub)
    return row0 + rstep * (pl.program_id(0) // (MOD_ROWS // tile_rows))


def _in_proj_kernel(x_ref, sh_ref, sc_ref, wt_ref, o_ref, keep_ref, wb_ref, *, row_groups, keep, row0, rstep):
    @pl.when(pl.program_id(0) == 0)
    def _():
        wb_ref[...] = jnp.zeros_like(wb_ref)
        for src, dst, size in row_groups:
            wb_ref[dst:dst + size, :] = wt_ref[src:src + size, :].astype(BF16)

    g = _mod_row(row0, rstep, x_ref.shape[0], 0)
    h = (_rms(x_ref[...]) * (1.0 + sc_ref[pl.ds(g, 1), :]) + sh_ref[pl.ds(g, 1), :]).astype(BF16)
    y = _nt(h, wb_ref[...])
    o_ref[...] = y.astype(o_ref.dtype)
    keep_ref[...] = y[:, keep[0]:keep[0] + keep[1]]


def _in_proj(x, mod, layer, wt, w_layer, row_groups, n, keep, row0, rstep):
    m = x.shape[0]
    return pl.pallas_call(
        functools.partial(_in_proj_kernel, row_groups=row_groups, keep=keep, row0=row0, rstep=rstep),
        grid=(m // TM_IN,),
        in_specs=[pl.BlockSpec((TM_IN, D_MODEL), lambda i: (i, 0)),
                  pl.BlockSpec((None, None, 8, D_MODEL), lambda i: (layer, 0, 0, 0)),
                  pl.BlockSpec((None, None, 8, D_MODEL), lambda i: (layer, 1, 0, 0)),
                  pl.BlockSpec((None, wt.shape[1], D_MODEL), lambda i: (w_layer, 0, 0), pipeline_mode=pl.Buffered(1))],
        out_specs=[pl.BlockSpec((TM_IN, n), lambda i: (i, 0)), pl.BlockSpec((TM_IN, keep[1]), lambda i: (i, 0))],
        out_shape=[jax.ShapeDtypeStruct((m, n), BF16), jax.ShapeDtypeStruct((m, keep[1]), F32)],
        scratch_shapes=[pltpu.VMEM((n, D_MODEL), BF16)],
        compiler_params=_params(1),
        name="norm_mod_proj",
    )(x, mod, mod, wt)


def _ffn_kernel(x_ref, sh_ref, sc_ref, gate_ref, wg_ref, wu_ref, wd_ref, o_ref, h_ref, *, row0, rstep):
    n_sub = x_ref.shape[0] // MOD_ROWS
    subs = [(slice(s * MOD_ROWS, (s + 1) * MOD_ROWS), _mod_row(row0, rstep, x_ref.shape[0], s)) for s in range(n_sub)]

    @pl.when(pl.program_id(1) == 0)
    def _():
        for rows, g in subs:
            x = x_ref[rows, :]
            o_ref[rows, :] = x
            h_ref[rows, :] = (_rms(x) * (1.0 + sc_ref[pl.ds(g, 1), :]) + sh_ref[pl.ds(g, 1), :]).astype(BF16)

    wg = wg_ref[...].astype(BF16)
    wu = wu_ref[...].astype(BF16)
    wd = wd_ref[...].astype(BF16)
    for rows, g in subs:
        h = h_ref[rows, :]
        a = _mm(h, wg)
        act = (a * jax.nn.sigmoid(a) * _mm(h, wu)).astype(BF16)
        o_ref[rows, :] += gate_ref[pl.ds(g, 1), :] * _mm(act, wd)


def _ffn(x, mod, layer, w_in, w_out, row0, rstep):
    m = x.shape[0]
    nj = FFN_H // FFN_TN
    mod_spec = lambda k: pl.BlockSpec((None, None, 8, D_MODEL), lambda i, j: (layer, k, 0, 0))
    return pl.pallas_call(
        functools.partial(_ffn_kernel, row0=row0, rstep=rstep),
        grid=(m // TM_FFN, nj),
        in_specs=[pl.BlockSpec((TM_FFN, D_MODEL), lambda i, j: (i, 0)), mod_spec(3), mod_spec(4), mod_spec(5),
                  pl.BlockSpec((None, D_MODEL, FFN_TN), lambda i, j: (layer, 0, j)),
                  pl.BlockSpec((None, D_MODEL, FFN_TN), lambda i, j: (layer, 0, j + nj)),
                  pl.BlockSpec((None, FFN_TN, D_MODEL), lambda i, j: (layer, j, 0))],
        out_specs=pl.BlockSpec((TM_FFN, D_MODEL), lambda i, j: (i, 0)),
        out_shape=jax.ShapeDtypeStruct((m, D_MODEL), F32),
        scratch_shapes=[pltpu.VMEM((TM_FFN, D_MODEL), BF16)],
        compiler_params=_params(2),
        name="ffn_residual",
    )(x, mod, mod, mod, w_in, w_in, w_out)


def _proj_res_kernel(*refs, n_act, row0, rstep):
    acts, ws = refs[:n_act], refs[n_act:2 * n_act]
    x_ref, gate_ref, o_ref = refs[2 * n_act:]
    g = row0 + rstep * pl.program_id(0)
    acc = _mm(acts[0][...], ws[0][...].astype(BF16))
    for a_ref, w_ref in zip(acts[1:], ws[1:]):
        acc = acc + _mm(a_ref[...], w_ref[...].astype(BF16))
    o_ref[...] = x_ref[...] + gate_ref[pl.ds(g, 1), :] * acc


def _out_proj(acts, w, w_layer, x, mod, layer, k_gate, row0, rstep):
    m = x.shape[0]
    n_act = len(acts)
    kw = acts[0].shape[1]
    act_specs = [pl.BlockSpec((TM, kw), lambda i: (i, 0)) for _ in acts]
    w_specs = [pl.BlockSpec((None, kw, D_MODEL), functools.partial(lambda i, p: (w_layer, p, 0), p=p),
                            pipeline_mode=pl.Buffered(1)) for p in range(n_act)]
    return pl.pallas_call(
        functools.partial(_proj_res_kernel, n_act=n_act, row0=row0, rstep=rstep),
        grid=(m // TM,),
        in_specs=act_specs + w_specs + [
            pl.BlockSpec((TM, D_MODEL), lambda i: (i, 0)),
            pl.BlockSpec((None, None, 8, D_MODEL), lambda i: (layer, k_gate, 0, 0)),
        ],
        out_specs=pl.BlockSpec((TM, D_MODEL), lambda i: (i, 0)),
        out_shape=jax.ShapeDtypeStruct((m, D_MODEL), F32),
        compiler_params=_params(1),
        name="out_proj_residual",
    )(*acts, *([w] * n_act), x, mod)


def _final_kernel(x_ref, g_ref, o_ref):
    o_ref[...] = _rms(x_ref[...]) * g_ref[...]


def _final_norm(x, g):
    m = x.shape[0]
    return pl.pallas_call(
        _final_kernel,
        grid=(m // TM,),
        in_specs=[pl.BlockSpec((TM, D_MODEL), lambda i: (i, 0)), pl.BlockSpec((1, D_MODEL), lambda i: (0, 0))],
        out_specs=pl.BlockSpec((TM, D_MODEL), lambda i: (i, 0)),
        out_shape=jax.ShapeDtypeStruct((m, D_MODEL), F32),
        compiler_params=_params(1),
        name="final_norm",
    )(x, g.reshape(1, D_MODEL))


def _gqa_kernel(*refs, sample):
    if sample:
        q_ref, k_ref, v_ref, gq_ref, gk_ref, ck_ref, cv_ref, cos_ref, sin_ref, o_ref, kb_ref, vb_ref = refs
    else:
        q_ref, k_ref, v_ref, gq_ref, gk_ref, o_ref, kn_ref, kb_ref, vb_ref = refs
    qi = pl.program_id(1)
    n_new = k_ref.shape[0]
    past = PAST_LEN if sample else 0
    rep = H_B // KV_B

    @pl.when(qi == 0)
    def _():
        for g in range(KV_B):
            sl = slice(HD_B * g, HD_B * (g + 1))
            kn = _rms(k_ref[:, sl].astype(F32)) * gk_ref[...]
            if sample:
                kb_ref[0:past, sl] = ck_ref[:, sl].astype(BF16)
                vb_ref[g, 0:past, 0:HD_B] = cv_ref[:, sl].astype(BF16)
                kn = kn * cos_ref[...] + pltpu.roll(kn, HD_B // 2, 1) * sin_ref[...]
            else:
                kn_ref[:, sl] = kn
            kb_ref[past:past + n_new, sl] = kn.astype(BF16)
            vb_ref[g, past:past + n_new, 0:HD_B] = v_ref[:, sl].astype(BF16)
            vb_ref[g, :, HD_B:] = jnp.ones((past + n_new, HD_B), BF16)

    r0 = pl.multiple_of(qi * QB, QB)
    qs = []
    for h in range(H_B):
        qn = _rms(q_ref[:, HD_B * h:HD_B * (h + 1)].astype(F32)) * gq_ref[...]
        if sample:
            qn = qn * cos_ref[pl.ds(r0, QB), :] + pltpu.roll(qn, HD_B // 2, 1) * sin_ref[pl.ds(r0, QB), :]
        qs.append((qn * (HD_B ** -0.5)).astype(BF16))
    scores = [_nt(qs[h], kb_ref[:, HD_B * (h // rep):HD_B * (h // rep + 1)]) for h in range(H_B)]
    weights = [jnp.exp(s - jnp.max(s, axis=-1, keepdims=True)).astype(BF16) for s in scores]
    sums = [_mm(weights[h], vb_ref[h // rep]) for h in range(H_B)]
    for h in range(H_B):
        o_ref[:, HD_B * h:HD_B * (h + 1)] = (sums[h][:, :HD_B] / sums[h][:, HD_B:]).astype(o_ref.dtype)


def _gqa(proj, g_q, g_k, n_batch, seq, ctx=None, rope=None):
    sample = ctx is not None
    m = n_batch * seq
    nq = seq // QB
    in_specs = [
        pl.BlockSpec((QB, 512), lambda b, i: (b * nq + i, 3)),
        pl.BlockSpec((seq, 256), lambda b, i: (b, 8)),
        pl.BlockSpec((seq, 256), lambda b, i: (b, 9)),
        pl.BlockSpec((1, HD_B), lambda b, i: (0, 0)),
        pl.BlockSpec((1, HD_B), lambda b, i: (0, 0)),
    ]
    args = [proj, proj, proj, g_q.reshape(1, HD_B), g_k.reshape(1, HD_B)]
    o_spec = pl.BlockSpec((QB, 512), lambda b, i: (b * nq + i, 0))
    o_shape = jax.ShapeDtypeStruct((m, 512), BF16)
    if sample:
        in_specs += [
            pl.BlockSpec((None, PAST_LEN, 256), lambda b, i: (b, 0, 0)),
            pl.BlockSpec((None, PAST_LEN, 256), lambda b, i: (b, 0, 0)),
            pl.BlockSpec((seq, HD_B), lambda b, i: (0, 0)),
            pl.BlockSpec((seq, HD_B), lambda b, i: (0, 0)),
        ]
        args += [ctx[0], ctx[1], rope[0], rope[1]]
        out_specs, out_shape = o_spec, o_shape
    else:
        out_specs = [o_spec, pl.BlockSpec((seq, 256), lambda b, i: (b, 0))]
        out_shape = [o_shape, jax.ShapeDtypeStruct((m, 256), F32)]
    n_keys = seq + (PAST_LEN if sample else 0)
    return pl.pallas_call(
        functools.partial(_gqa_kernel, sample=sample),
        grid=(n_batch, nq),
        in_specs=in_specs,
        out_specs=out_specs,
        out_shape=out_shape,
        scratch_shapes=[pltpu.VMEM((n_keys, KV_B * HD_B), BF16), pltpu.VMEM((KV_B, n_keys, 2 * HD_B), BF16)],
        compiler_params=_params(2),
        name="gqa_sample" if sample else "gqa_prompt",
    )(*args)


def _mla_kernel(*refs, sample):
    if sample:
        (cq_ref, ckv_ref, kpe_ref, gq_ref, wqb_ref, gkv_ref, wkvb_ref, ex_ref, cckv_ref, ckpe_ref, c4_ref, s4_ref,
         o_ref, kv_s, kx_s) = refs
    else:
        cq_ref, ckv_ref, kpe_ref, gq_ref, wqb_ref, gkv_ref, wkvb_ref, ex_ref, o_ref, ckvn_ref, kv_s, kx_s = refs
    qi = pl.program_id(1)
    n_new = ckv_ref.shape[0]
    past = PAST_LEN if sample else 0

    def rope(x, c, s):
        x1, x2 = x[:, :LANES], x[:, LANES:]
        return jnp.concatenate([x1 * c - x2 * s, x1 * s + x2 * c], axis=1)

    hw = NOPE_D + 2 * V_D

    def stage_kv(rows, kv):
        for h in range(H_D):
            kv_s[rows, hw * h:hw * h + NOPE_D + V_D] = kv[:, 256 * h:256 * (h + 1)].astype(BF16)

    @pl.when(qi == 0)
    def _():
        wkvb = wkvb_ref[...].astype(BF16)
        ckvn = _rms(ckv_ref[...].astype(F32)) * gkv_ref[...]
        if not sample:
            ckvn_ref[...] = ckvn
        stage_kv(slice(past, past + n_new), _mm(ckvn.astype(BF16), wkvb))
        for h in range(H_D):
            kv_s[:, hw * h + NOPE_D + V_D:hw * (h + 1)] = jnp.ones((past + n_new, V_D), BF16)
        kx = _mm(kpe_ref[:, 0:ROPE_D], ex_ref[...])
        if sample:
            stage_kv(slice(0, past), _mm(cckv_ref[...].astype(BF16), wkvb))
            kx_s[0:past, :] = _mm(ckpe_ref[...].astype(BF16), ex_ref[...]).astype(BF16)
            kx = rope(kx, c4_ref[...], s4_ref[...])
        kx_s[past:past + n_new, :] = kx.astype(BF16)

    q = _mm((_rms(cq_ref[...].astype(F32)) * gq_ref[...]).astype(BF16), wqb_ref[...].astype(BF16))
    q = q * ((NOPE_D + ROPE_D) ** -0.5)
    qpe = q[:, 4 * NOPE_D:]
    if sample:
        r0 = pl.multiple_of(qi * QB, QB)
        qpe = rope(qpe, c4_ref[pl.ds(r0, QB), :], s4_ref[pl.ds(r0, QB), :])
    lane_head = (lax.broadcasted_iota(jnp.int32, (1, 2 * LANES), 1) % LANES) // (ROPE_D // 2)
    qb, qpb = q.astype(BF16), qpe.astype(BF16)
    scores = [_nt(qb[:, NOPE_D * h:NOPE_D * (h + 1)], kv_s[:, hw * h:hw * h + NOPE_D])
              + _nt(jnp.where(lane_head == h, qpb, jnp.zeros_like(qpb)), kx_s[...]) for h in range(H_D)]
    weights = [jnp.exp(s - jnp.max(s, axis=-1, keepdims=True)).astype(BF16) for s in scores]
    sums = [_mm(weights[h], kv_s[:, hw * h + NOPE_D:hw * (h + 1)]) for h in range(H_D)]
    for h in range(H_D):
        o_ref[:, V_D * h:V_D * (h + 1)] = (sums[h][:, :V_D] / sums[h][:, V_D:]).astype(o_ref.dtype)


def _mla(proj, g_q, w_qb, g_kv, w_kvb, n_batch, seq, ctx=None, rope=None, expand=None):
    sample = ctx is not None
    m = n_batch * seq
    nq = seq // QB
    in_specs = [
        pl.BlockSpec((QB, Q_RANK), lambda b, i: (b * nq + i, 6)),
        pl.BlockSpec((seq, KV_RANK), lambda b, i: (b, 14)),
        pl.BlockSpec((seq, LANES), lambda b, i: (b, 15)),
        pl.BlockSpec((1, Q_RANK), lambda b, i: (0, 0)),
        pl.BlockSpec((Q_RANK, 768), lambda b, i: (0, 0)),
        pl.BlockSpec((1, KV_RANK), lambda b, i: (0, 0)),
        pl.BlockSpec((KV_RANK, 1024), lambda b, i: (0, 0)),
        pl.BlockSpec((ROPE_D, 256), lambda b, i: (0, 0)),
    ]
    args = [proj, proj, proj, g_q.reshape(1, Q_RANK), w_qb, g_kv.reshape(1, KV_RANK), w_kvb, expand]
    o_spec = pl.BlockSpec((QB, 512), lambda b, i: (b * nq + i, 0))
    o_shape = jax.ShapeDtypeStruct((m, 512), BF16)
    if sample:
        in_specs += [
            pl.BlockSpec((None, PAST_LEN, KV_RANK), lambda b, i: (b, 0, 0)),
            pl.BlockSpec((None, PAST_LEN, ROPE_D), lambda b, i: (b, 0, 0)),
            pl.BlockSpec((seq, LANES), lambda b, i: (0, 0)),
            pl.BlockSpec((seq, LANES), lambda b, i: (0, 0)),
        ]
        args += [ctx[0], ctx[1], rope[0], rope[1]]
        out_specs, out_shape = o_spec, o_shape
    else:
        out_specs = [o_spec, pl.BlockSpec((seq, KV_RANK), lambda b, i: (b, 0))]
        out_shape = [o_shape, jax.ShapeDtypeStruct((m, KV_RANK), F32)]
    n_keys = seq + (PAST_LEN if sample else 0)
    return pl.pallas_call(
        functools.partial(_mla_kernel, sample=sample),
        grid=(n_batch, nq),
        in_specs=in_specs,
        out_specs=out_specs,
        out_shape=out_shape,
        scratch_shapes=[pltpu.VMEM((n_keys, H_D * (NOPE_D + 2 * V_D)), BF16), pltpu.VMEM((n_keys, 256), BF16)],
        compiler_params=_params(2),
        name="mla_sample" if sample else "mla_prompt",
    )(*args)


def _split_bf16(x):
    hi = x.astype(BF16)
    return hi, (x - hi.astype(F32)).astype(BF16)


def _dft(table, x):
    t_hi, t_lo = _split_bf16(table)
    x_hi, x_lo = _split_bf16(x)
    return _mm(t_hi, x_hi) + _mm(t_hi, x_lo) + _mm(t_lo, x_hi)


def _filter_kernel(z_ref, wf1_ref, bf1_ref, fr_ref, wf2_ref, bf2_ref, wf3_ref, t_ref, dl_ref,
                   c_ref, s_ref, gre_ref, gim_ref):
    n_tok = z_ref.shape[0]
    fr = fr_ref[...]
    hid = jnp.sin(fr * (_mm(z_ref[...].astype(BF16), wf1_ref[...].astype(BF16)) + bf1_ref[...]))
    hid = jnp.sin(fr * (_mm(hid.astype(BF16), wf2_ref[...].astype(BF16)) + bf2_ref[...]))
    filt = _mm(hid.astype(BF16), wf3_ref[...].astype(BF16))
    decay = jnp.exp(-t_ref[...] * dl_ref[...])
    row = lax.broadcasted_iota(jnp.int32, (n_tok, 1), 0)
    h_f = filt[:, :HY_W] * decay
    h_b = jnp.where(row == 0, 0.0, filt[:, HY_W:] * decay)
    p, m = h_f + h_b, h_f - h_b
    g_re = _dft(c_ref[...], p)
    g_im = _dft(s_ref[...], m)
    sign = jnp.where(row % 2 == 0, 1.0, -1.0)
    nyquist = jnp.sum(p * sign, axis=0, keepdims=True)
    g_im = jnp.where(row == 0, nyquist, g_im)
    wk = jnp.where(row == 0, 0.5 / n_tok, 1.0 / n_tok)
    gre_ref[...] = g_re * wk
    gim_ref[...] = g_im * wk


def _filter_spectrum(z, wf1, bf1, freq, wf2, bf2, wf3, t_col, deltas, tabs):
    n_tok = z.shape[0]
    out = jax.ShapeDtypeStruct((n_tok, HY_W), F32)
    return pl.pallas_call(
        _filter_kernel,
        out_shape=[out, out],
        compiler_params=pltpu.CompilerParams(vmem_limit_bytes=VMEM_LIMIT),
        name="hyena_filter",
    )(z, wf1, bf1.reshape(1, FILT_HID), freq.reshape(1, FILT_HID), wf2, bf2.reshape(1, FILT_HID), wf3,
      t_col, deltas, tabs[0], tabs[1])


HY_CT = 256


HY_ROWS = 1024


def _hyena_kernel(u0_ref, u1_ref, u2_ref, w0_ref, w1_ref, w2_ref, b0_ref, b1_ref, b2_ref, skip_ref,
                  gre_ref, gim_ref, cf_ref, sf_ref, stf_ref, o_ref, c_ref, s_ref, st_ref):
    seq = c_ref.shape[0]
    n_rows = u0_ref.shape[0]
    n_seq = n_rows // seq
    pos = lax.broadcasted_iota(jnp.int32, (n_rows, 1), 0) % seq

    @pl.when((pl.program_id(0) == 0) & (pl.program_id(1) == 0))
    def _():
        c_ref[...] = cf_ref[...].astype(BF16)
        s_ref[...] = sf_ref[...].astype(BF16)
        st_ref[...] = stf_ref[...].astype(BF16)

    def short_conv(u_ref, w_ref, b_ref):
        x, w = u_ref[...].astype(F32), w_ref[...]
        prev = jnp.where(pos == 0, 0.0, pltpu.roll(x, 1, 0))
        nxt = jnp.where(pos == seq - 1, 0.0, pltpu.roll(x, n_rows - 1, 0))
        return prev * w[0:1] + x * w[1:2] + nxt * w[2:3] + b_ref[...]

    def side_by_side(a):
        return a if n_seq == 1 else jnp.concatenate([a[s * seq:(s + 1) * seq] for s in range(n_seq)], axis=1)

    def stacked(a):
        ct = a.shape[1] // n_seq
        return a if n_seq == 1 else jnp.concatenate([a[:, s * ct:(s + 1) * ct] for s in range(n_seq)], axis=0)

    x0 = short_conv(u0_ref, w0_ref, b0_ref)
    gv = short_conv(u1_ref, w1_ref, b1_ref) * short_conv(u2_ref, w2_ref, b2_ref)
    sig = side_by_side(gv).astype(BF16)
    u_re = _mm(c_ref[...], sig)
    u_im = _mm(s_ref[...], sig)
    g_re = jnp.concatenate([gre_ref[...]] * n_seq, axis=1)
    g_im = jnp.concatenate([gim_ref[...]] * n_seq, axis=1)
    bin0 = lax.broadcasted_iota(jnp.int32, (seq, 1), 0) == 0
    p_im = u_im * g_im
    y_re = u_re * g_re - jnp.where(bin0, 0.0, p_im)
    y_im = jnp.where(bin0, p_im, u_re * g_im + u_im * g_re)
    y = stacked(_mm(c_ref[...], y_re.astype(BF16)) + _mm(st_ref[...], y_im.astype(BF16)))
    o_ref[...] = (x0 * (y + gv * skip_ref[...])).astype(o_ref.dtype)


def _hyena(proj, w_conv, b_conv, skip, g_re, g_im, tabs, n_batch, seq):
    nct = HY_W // HY_CT
    u_specs = [pl.BlockSpec((HY_ROWS, HY_CT), functools.partial(lambda b, c, g: (b, g * nct + c), g=g)) for g in range(3)]
    w_specs = [pl.BlockSpec((3, HY_CT), functools.partial(lambda b, c, g: (0, g * nct + c), g=g)) for g in range(3)]
    b_specs = [pl.BlockSpec((1, HY_CT), functools.partial(lambda b, c, g: (0, g * nct + c), g=g)) for g in range(3)]
    tab_spec = pl.BlockSpec((seq, seq), lambda b, c: (0, 0))
    return pl.pallas_call(
        _hyena_kernel,
        grid=(n_batch * seq // HY_ROWS, nct),
        in_specs=u_specs + w_specs + b_specs + [
            pl.BlockSpec((1, HY_CT), lambda b, c: (0, c)),
            pl.BlockSpec((seq, HY_CT), lambda b, c: (0, c)),
            pl.BlockSpec((seq, HY_CT), lambda b, c: (0, c)),
        ] + [tab_spec] * 3,
        out_specs=pl.BlockSpec((HY_ROWS, HY_CT), lambda b, c: (b, c)),
        out_shape=jax.ShapeDtypeStruct((n_batch * seq, HY_W), BF16),
        scratch_shapes=[pltpu.VMEM((seq, seq), BF16)] * 3,
        compiler_params=_params(2),
        name="hyena_conv",
    )(proj, proj, proj, w_conv, w_conv, w_conv, b_conv, b_conv, b_conv, skip.reshape(1, HY_W), g_re, g_im, *tabs)


def _dft_tables(n_tok):
    k = np.arange(n_tok)[:, None]
    s = np.arange(n_tok)[None, :]
    ang = ((k * s) % (2 * n_tok)) * (np.pi / n_tok)
    cos_t = np.cos(ang)
    sin_f = np.where(k == 0, np.where(s % 2 == 0, 1.0, -1.0), -np.sin(ang))
    return [jnp.asarray(t, F32) for t in (cos_t, sin_f, sin_f.T)]


GLA_LEVELS = (32, 16, 8, 4, 2, 1)
GLA_SAFE_DECAY = 60.0
GLA_GROUP = 2


def _gla_constants():
    c = GLA_CHUNK
    idx = np.arange(c)
    i, t = idx[:, None], idx[None, :]
    masks = []
    for s in GLA_LEVELS:
        upper = (idx % (2 * s)) >= s
        masks.append(((i // (2 * s)) == (t // (2 * s))) & upper[:, None] & (~upper)[None, :])
    masks.append(i == t)
    tri = t <= i
    fwd_m = np.stack([np.tile(m, (H_A, 1)) for m in masks]).astype(np.float32)
    bwd_m = np.stack([np.tile(m[::-1, ::-1], (H_A, 1)) for m in masks]).astype(np.float32)
    head_of_row = np.repeat(np.arange(H_A), c)[:, None]
    head_of_lane = np.repeat(np.arange(H_A), DK_A)[None, :]
    head_mask = head_of_row == head_of_lane
    return (jnp.asarray(tri, BF16), jnp.asarray(tri[::-1, ::-1], BF16), jnp.asarray(fwd_m), jnp.asarray(bwd_m),
            jnp.asarray(head_mask, BF16))


def _pair_reference(b, s, backward, row):
    c = GLA_CHUNK
    ref = s if backward else s - 1
    if 2 * s >= 8:
        pieces = [jnp.broadcast_to(b[p * 2 * s + ref:p * 2 * s + ref + 1, :], (2 * s, b.shape[1]))
                  for p in range(c // (2 * s))]
        return pieces[0] if len(pieces) == 1 else jnp.concatenate(pieces, axis=0)
    pos = row % (2 * s)
    out = None
    for o in range(2 * s):
        d = ref - o
        shifted = b if d == 0 else pltpu.roll(b, (-d) % c, 0)
        out = shifted if out is None else jnp.where(pos == o, shifted, out)
    return out


def _chunk_log_decay(la, t_ref):
    l1 = la.astype(BF16)
    r1 = la - l1.astype(F32)
    l2 = r1.astype(BF16)
    l3 = (r1 - l2.astype(F32)).astype(BF16)
    tmat = t_ref[...]
    return _mm(tmat, l1) + _mm(tmat, l2) + _mm(tmat, l3)


def _stack_heads(a, hm):
    ab = a.astype(BF16)
    return jnp.concatenate([ab] * H_A, axis=0) * hm


def _state_terms(k, v, b, b_last):
    c = GLA_CHUNK
    k_rest = (k * jnp.exp(b_last - b)).T
    carry = jnp.broadcast_to(jnp.exp(b_last), (2 * c, b.shape[1])).T
    return k_rest.astype(BF16), carry


def _gla_chunk(q, k, v, la, t_ref, m_ref, hm, s_ref, backward):
    c = GLA_CHUNK
    b = _chunk_log_decay(la, t_ref)
    row = lax.broadcasted_iota(jnp.int32, (c, 1), 0)
    last = 0 if backward else c - 1
    b_last = b[last:last + 1, :]
    scores = _nt(_stack_heads(q, hm), k.astype(BF16)) * m_ref[len(GLA_LEVELS)]
    for lvl, s in enumerate(GLA_LEVELS):
        is_query = (row % (2 * s) < s) if backward else (row % (2 * s) >= s)
        delta = b - _pair_reference(b, s, backward, row)
        x = jnp.exp(jnp.where(is_query, delta, -delta))
        scores = scores + _nt(_stack_heads(q * x, hm), (k * x).astype(BF16)) * m_ref[lvl]
    scores = scores.astype(BF16)
    state = s_ref[...]
    inter = _mm(_stack_heads(q * jnp.exp(b), hm), state.astype(BF16))
    k_rest, carry = _state_terms(k, v, b, b_last)
    outs = []
    for h in range(H_A):
        rows = slice(c * h, c * (h + 1))
        v_h = v[:, DV_A * h:DV_A * (h + 1)]
        outs.append(_mm(scores[rows], v_h) + inter[rows])
        s_ref[rows, :] = state[rows] * carry[rows] + _mm(k_rest[rows], v_h)
    return jnp.concatenate(outs, axis=1)


def _gla_local(items, hm):
    c = GLA_CHUNK
    bs = [_chunk_log_decay(la, t_ref) for _, _, _, la, t_ref, _, _ in items]
    b_lasts = [b[(0 if it[6] else c - 1):(0 if it[6] else c - 1) + 1, :] for b, it in zip(bs, items)]
    q_decayed = [_stack_heads(it[0] * jnp.exp(b), hm) for it, b in zip(items, bs)]
    k_grown = [(it[1] * jnp.exp(-b)).astype(BF16) for it, b in zip(items, bs)]
    raw = [_nt(qd, kg) for qd, kg in zip(q_decayed, k_grown)]
    scores = [(r * it[5]).astype(BF16) for r, it in zip(raw, items)]
    terms = [_state_terms(it[1], it[2], b, bl) for it, b, bl in zip(items, bs, b_lasts)]
    out = []
    for it, sc, (k_rest, carry), qd in zip(items, scores, terms, q_decayed):
        v = it[2]
        heads = [(slice(c * h, c * (h + 1)), v[:, DV_A * h:DV_A * (h + 1)]) for h in range(H_A)]
        intra = jnp.concatenate([_mm(sc[rows], v_h) for rows, v_h in heads], axis=1)
        incr = jnp.concatenate([_mm(k_rest[rows], v_h) for rows, v_h in heads], axis=0)
        out.append((intra, qd, incr, carry))
    return out


def _gla_kernel(*refs, sample):
    if sample:
        (x_ref, z_ref, wf_ref, bf_ref, wb_ref, bb_ref, tf_ref, tb_ref, mf_ref, mb_ref, hm_ref, gn_ref, sf0_ref, sb0_ref,
         o_ref, la_f, la_b, o_f, o_b, s_f, s_b, qd_f, qd_b, ds_f, ds_b, cr_f, cr_b) = refs
    else:
        (x_ref, z_ref, wf_ref, bf_ref, wb_ref, bb_ref, tf_ref, tb_ref, mf_ref, mb_ref, hm_ref, gn_ref,
         o_ref, sf_out, sb_out, la_f, la_b, o_f, o_b, s_f, s_b, qd_f, qd_b, ds_f, ds_b, cr_f, cr_b) = refs
    n_tok = x_ref.shape[0]
    n_chunks = n_tok // GLA_CHUNK
    hk, hv = H_A * DK_A, H_A * DV_A
    zb = z_ref[...].astype(BF16)
    la_f[...] = jax.nn.log_sigmoid(_mm(zb, wf_ref[...].astype(BF16)) + bf_ref[...]) / GLA_TAU
    la_b[...] = jax.nn.log_sigmoid(_mm(zb, wb_ref[...].astype(BF16)) + bb_ref[...]) / GLA_TAU
    if sample:
        s_f[...] = sf0_ref[...]
        s_b[...] = sb0_ref[...]
    else:
        s_f[...] = jnp.zeros_like(s_f)
        s_b[...] = jnp.zeros_like(s_b)
    hm = hm_ref[...]

    fwd = (la_f, tf_ref, mf_ref, s_f, o_f, qd_f, ds_f, cr_f, False)
    bwd = (la_b, tb_ref, mb_ref, s_b, o_b, qd_b, ds_b, cr_b, True)
    tri_f = jnp.sum(mf_ref[...], axis=0)
    tri_b = jnp.sum(mb_ref[...], axis=0)

    def chunk_rows(ci, backward):
        cidx = n_chunks - 1 - ci if backward else ci
        return cidx, pl.ds(pl.multiple_of(cidx * GLA_CHUNK, GLA_CHUNK), GLA_CHUNK)

    def load_qkv(rows):
        q = x_ref[rows, 0:hk].astype(F32) * (DK_A ** -0.5)
        return q, x_ref[rows, hk:2 * hk].astype(F32), x_ref[rows, 2 * hk:2 * hk + hv]

    def safe_step(ci, carry):
        for la_ref, t_ref, m_ref, s_ref, out_ref, _, _, _, backward in (fwd, bwd):
            _, rows = chunk_rows(ci, backward)
            out_ref[rows, :] = _gla_chunk(*load_qkv(rows), la_ref[rows, :], t_ref, m_ref, hm, s_ref, backward)
        return carry

    def local_step(gi, carry):
        items, dests = [], []
        for (la_ref, t_ref, _, _, out_ref, qd_ref, ds_ref, cr_ref, backward), tri in ((fwd, tri_f), (bwd, tri_b)):
            for u in range(GLA_GROUP):
                cidx, rows = chunk_rows(gi * GLA_GROUP + u, backward)
                items.append((*load_qkv(rows), la_ref[rows, :], t_ref, tri, backward))
                dests.append((out_ref, rows, qd_ref, ds_ref, cr_ref, cidx))
        for (out_ref, rows, qd_ref, ds_ref, cr_ref, cidx), (intra, qd, incr, factor) in zip(dests, _gla_local(items, hm)):
            out_ref[rows, :] = intra
            qd_ref[cidx] = qd
            ds_ref[cidx] = incr
            cr_ref[cidx] = factor
        return carry

    def scan_step(ci, carry):
        for _, _, _, s_ref, out_ref, qd_ref, ds_ref, cr_ref, backward in (fwd, bwd):
            cidx, rows = chunk_rows(ci, backward)
            state = s_ref[...]
            inter = _mm(qd_ref[cidx], state.astype(BF16))
            out_ref[rows, :] += jnp.concatenate(
                [inter[GLA_CHUNK * h:GLA_CHUNK * (h + 1)] for h in range(H_A)], axis=1)
            s_ref[...] = state * cr_ref[cidx] + ds_ref[cidx]
        return carry

    chunk_sums = [jnp.sum(ref[...].reshape(n_chunks, GLA_CHUNK, hk), axis=1) for ref in (la_f, la_b)]
    mild = jnp.minimum(jnp.min(chunk_sums[0]), jnp.min(chunk_sums[1])) > -GLA_SAFE_DECAY

    @pl.when(mild)
    def _():
        lax.fori_loop(0, n_chunks // GLA_GROUP, local_step, 0)
        lax.fori_loop(0, n_chunks, scan_step, 0, unroll=2)

    @pl.when(jnp.logical_not(mild))
    def _():
        lax.fori_loop(0, n_chunks, safe_step, 0)
    if not sample:
        sf_out[...] = s_f[...]
        sb_out[...] = s_b[...]
    gain = gn_ref[...]
    for h in range(H_A):
        cols = slice(DV_A * h, DV_A * (h + 1))
        r = x_ref[:, 2 * hk + hv + DV_A * h:2 * hk + hv + DV_A * (h + 1)].astype(F32)
        o_ref[:, cols] = (_rms(o_f[:, cols] + o_b[:, cols]) * gain * (r * jax.nn.sigmoid(r))).astype(o_ref.dtype)


def _gla(proj, w_gf, b_gf, w_gb, b_gb, g_norm, consts, n_batch, seq, ctx=None):
    sample = ctx is not None
    hk, hv = H_A * DK_A, H_A * DV_A
    n_ch = seq // GLA_CHUNK
    full = lambda shape: pl.BlockSpec(shape, lambda b: (0,) * len(shape))
    in_specs = [
        pl.BlockSpec((seq, 2 * hk + 2 * hv), lambda b: (b, 0)),
        pl.BlockSpec((seq, LANES), lambda b: (b, EVEN_W // LANES - 1)),
        full((LANES, hk)), full((1, hk)), full((LANES, hk)), full((1, hk)),
        full(consts[0].shape), full(consts[1].shape), full(consts[2].shape), full(consts[3].shape), full(consts[4].shape),
        full((1, DV_A)),
    ]
    args = [proj, proj, w_gf, b_gf.reshape(1, hk), w_gb, b_gb.reshape(1, hk), *consts, g_norm.reshape(1, DV_A)]
    o_spec = pl.BlockSpec((seq, hv), lambda b: (b, 0))
    o_shape = jax.ShapeDtypeStruct((n_batch * seq, hv), BF16)
    st_spec = pl.BlockSpec((None, hk, DV_A), lambda b: (b, 0, 0))
    if sample:
        in_specs += [st_spec, st_spec]
        args += [ctx[0], ctx[1]]
        out_specs, out_shape = o_spec, o_shape
    else:
        st_shape = jax.ShapeDtypeStruct((n_batch, hk, DV_A), F32)
        out_specs, out_shape = [o_spec, st_spec, st_spec], [o_shape, st_shape, st_shape]
    return pl.pallas_call(
        functools.partial(_gla_kernel, sample=sample),
        grid=(n_batch,),
        in_specs=in_specs,
        out_specs=out_specs,
        out_shape=out_shape,
        scratch_shapes=[pltpu.VMEM((seq, hk), F32), pltpu.VMEM((seq, hk), F32),
                        pltpu.VMEM((seq, hv), F32), pltpu.VMEM((seq, hv), F32),
                        pltpu.VMEM((hk, DV_A), F32), pltpu.VMEM((hk, DV_A), F32),
                        pltpu.VMEM((n_ch, H_A * GLA_CHUNK, hk), BF16), pltpu.VMEM((n_ch, H_A * GLA_CHUNK, hk), BF16),
                        pltpu.VMEM((n_ch, hk, DV_A), F32), pltpu.VMEM((n_ch, hk, DV_A), F32),
                        pltpu.VMEM((n_ch, hk, DV_A), F32), pltpu.VMEM((n_ch, hk, DV_A), F32)],
        compiler_params=_params(1),
        name="gla_sample" if sample else "gla_prompt",
    )(*args)


def _axial_rope(n_tokens, dim):
    rows = n_tokens // GRID_W
    row = np.repeat(np.arange(rows), GRID_W).astype(np.float64)
    col = np.tile(np.arange(GRID_W), rows).astype(np.float64)
    n_freq = dim // 4
    inv = ROPE_THETA ** (-np.arange(n_freq) / n_freq)
    ang = np.concatenate([row[:, None] * inv, col[:, None] * inv], axis=-1)
    return np.cos(ang).astype(np.float32), np.sin(ang).astype(np.float32)


def _filter_features(n_tokens):
    t = np.linspace(0.0, 1.0, n_tokens)[:, None]
    w = 2.0 * np.pi * np.arange(n_tokens)[:, None] / n_tokens
    f = np.linspace(1e-4, FILT_BANDS - 1, FILT_BANDS)[None, :]
    z = np.concatenate([t, np.cos(f * w), -np.sin(f * w)], axis=-1)
    z = np.pad(z, ((0, 0), (0, LANES - FILT_EMB)))
    return jnp.asarray(z, F32), jnp.asarray(t, F32)


_KPE_EXPAND = np.array([(p // LANES) * (ROPE_D // 2) + p % (ROPE_D // 2) for p in range(2 * LANES)])
_QB_PERM = np.array(
    [192 * (p // NOPE_D) + p % NOPE_D for p in range(H_D * NOPE_D)]
    + [192 * (p // 32) + NOPE_D + p % 32 for p in range(H_D * 32)]
    + [192 * (p // 32) + NOPE_D + 32 + p % 32 for p in range(H_D * 32)])

EVEN_ROW_GROUPS = ((0, 0, 1536), (1568, 1536, 1024), (1536, EVEN_W - 2 * GATE_RANK, 2 * GATE_RANK))
ODD_ROW_GROUPS = ((0, 0, 1984),)
EVEN_KEEP = (2304, 256)
ODD_KEEP = (1920, LANES)


def kernel(x_prompt, x_sample, state_gla_fwd, state_gla_bwd, cache_gqa_k, cache_gqa_v, cache_mla_ckv, cache_mla_kpe, c, c_ctx, w_mod, b_mod, w_in_even, w_gla_gate_f, b_gla_gate_f, w_gla_gate_b, b_gla_gate_b, g_gla_norm, g_gqa_q, g_gqa_k, w_out_even, w_in_odd, w_hy_conv, b_hy_conv, hy_skip, w_filt1, b_filt1, filt_freq, w_filt2, b_filt2, w_filt3, g_mla_q, w_mla_qb, g_mla_kv, w_mla_kvb, w_out_odd, w_ffn_in, w_ffn_out, g_final):
    n_c, n_s = BATCH * SEQ, DEC_BATCH * DEC_SEQ
    cvec = jnp.concatenate([c_ctx[None, :], c, jnp.zeros((8 - 1 - DEC_BATCH, D_MODEL), F32)], axis=0)
    mod = _modulation(cvec, w_mod, b_mod)
    xc = x_prompt.reshape(n_c, D_MODEL)
    xs = x_sample.reshape(n_s, D_MODEL)
    rows_c, rows_s = (0, 0), (1, DEC_SEQ // MOD_ROWS)

    gla_consts = _gla_constants()
    cos_b, sin_b = _axial_rope(DEC_SEQ, HD_B)
    rope_b = (jnp.asarray(np.concatenate([cos_b, cos_b], axis=1)), jnp.asarray(np.concatenate([-sin_b, sin_b], axis=1)))
    cos_d, sin_d = _axial_rope(DEC_SEQ, ROPE_D)
    rope_d = (jnp.asarray(np.tile(cos_d, (1, H_D))), jnp.asarray(np.tile(sin_d, (1, H_D))))
    kpe_expand = jnp.asarray(np.arange(ROPE_D)[:, None] == _KPE_EXPAND[None, :], BF16)
    tabs_c, tabs_s = _dft_tables(SEQ), _dft_tables(DEC_SEQ)
    z_c, t_c = _filter_features(SEQ)
    z_s, t_s = _filter_features(DEC_SEQ)
    deltas = jnp.asarray(np.abs(np.linspace(HY_MIN_DECAY, HY_MAX_DECAY, HY_W))[None, :], F32)

    wt_even = jnp.swapaxes(w_in_even, 1, 2)
    wt_odd = jnp.swapaxes(w_in_odd, 1, 2)

    st_gf, st_gb, st_k, st_v, st_ckv, st_kpe = [], [], [], [], [], []
    for i in range(DEPTH):
        j = i // 2
        if i % 2 == 0:
            z0 = LANES - 2 * GATE_RANK
            pad_f = jnp.zeros((LANES, H_A * DK_A), F32).at[z0:z0 + GATE_RANK].set(w_gla_gate_f[j])
            pad_b = jnp.zeros((LANES, H_A * DK_A), F32).at[z0 + GATE_RANK:LANES].set(w_gla_gate_b[j])
            pc, v_new = _in_proj(xc, mod, i, wt_even, j, EVEN_ROW_GROUPS, EVEN_W, EVEN_KEEP, *rows_c)
            ps, _ = _in_proj(xs, mod, i, wt_even, j, EVEN_ROW_GROUPS, EVEN_W, EVEN_KEEP, *rows_s)
            gate_args = (pad_f, b_gla_gate_f[j], pad_b, b_gla_gate_b[j], g_gla_norm[j], gla_consts)
            a_c, s_f, s_b = _gla(pc, *gate_args, BATCH, SEQ)
            ctx_a = (state_gla_fwd[:, j].reshape(DEC_BATCH, H_A * DK_A, DV_A),
                     state_gla_bwd[:, j].reshape(DEC_BATCH, H_A * DK_A, DV_A))
            a_s = _gla(ps, *gate_args, DEC_BATCH, DEC_SEQ, ctx=ctx_a)
            b_c, k_norm = _gqa(pc, g_gqa_q[j], g_gqa_k[j], BATCH, SEQ)
            ctx_b = (cache_gqa_k[:, j].reshape(DEC_BATCH, PAST_LEN, KV_B * HD_B),
                     cache_gqa_v[:, j].reshape(DEC_BATCH, PAST_LEN, KV_B * HD_B))
            b_s = _gqa(ps, g_gqa_q[j], g_gqa_k[j], DEC_BATCH, DEC_SEQ, ctx=ctx_b, rope=rope_b)
            w_out = w_out_even
            st_gf.append(s_f.reshape(BATCH, H_A, DK_A, DV_A))
            st_gb.append(s_b.reshape(BATCH, H_A, DK_A, DV_A))
            st_k.append(k_norm.reshape(BATCH, SEQ, KV_B, HD_B))
            st_v.append(v_new.reshape(BATCH, SEQ, KV_B, HD_B))
        else:
            pc, kpe_new = _in_proj(xc, mod, i, wt_odd, j, ODD_ROW_GROUPS, ODD_W, ODD_KEEP, *rows_c)
            ps, _ = _in_proj(xs, mod, i, wt_odd, j, ODD_ROW_GROUPS, ODD_W, ODD_KEEP, *rows_s)
            wf1 = jnp.pad(w_filt1[j], ((0, LANES - FILT_EMB), (0, 0)))
            filt_args = (wf1, b_filt1[j], filt_freq[j], w_filt2[j], b_filt2[j], w_filt3[j])
            g_c = _filter_spectrum(z_c, *filt_args, t_c, deltas, tabs_c)
            g_s = _filter_spectrum(z_s, *filt_args, t_s, deltas, tabs_s)
            b_conv = b_hy_conv[j].reshape(1, 3 * HY_W)
            a_c = _hyena(pc, w_hy_conv[j], b_conv, hy_skip[j], g_c[0], g_c[1], tabs_c, BATCH, SEQ)
            a_s = _hyena(ps, w_hy_conv[j], b_conv, hy_skip[j], g_s[0], g_s[1], tabs_s, DEC_BATCH, DEC_SEQ)
            w_qb = w_mla_qb[j][:, _QB_PERM]
            b_c, ckv_norm = _mla(pc, g_mla_q[j], w_qb, g_mla_kv[j], w_mla_kvb[j], BATCH, SEQ, expand=kpe_expand)
            b_s = _mla(ps, g_mla_q[j], w_qb, g_mla_kv[j], w_mla_kvb[j], DEC_BATCH, DEC_SEQ,
                       ctx=(cache_mla_ckv[:, j], cache_mla_kpe[:, j]), rope=rope_d, expand=kpe_expand)
            w_out = w_out_odd
            st_ckv.append(ckv_norm.reshape(BATCH, SEQ, KV_RANK))
            st_kpe.append(kpe_new[:, :ROPE_D].reshape(BATCH, SEQ, ROPE_D))
        xc = _out_proj([a_c, b_c], w_out, j, xc, mod, i, 2, *rows_c)
        xs = _out_proj([a_s, b_s], w_out, j, xs, mod, i, 2, *rows_s)
        xc = _ffn(xc, mod, i, w_ffn_in, w_ffn_out, *rows_c)
        xs = _ffn(xs, mod, i, w_ffn_in, w_ffn_out, *rows_s)
    y_prompt = _final_norm(xc, g_final).reshape(BATCH, SEQ, D_MODEL)
    y_sample = _final_norm(xs, g_final).reshape(DEC_BATCH, DEC_SEQ, D_MODEL)
    return (y_prompt, y_sample, jnp.stack(st_gf, axis=1), jnp.stack(st_gb, axis=1), jnp.stack(st_k, axis=1),
            jnp.stack(st_v, axis=1), jnp.stack(st_ckv, axis=1), jnp.stack(st_kpe, axis=1))
```

```python
import functools
import math

import numpy as np
import jax
import jax.numpy as jnp
from jax import lax
from jax.experimental import pallas as pl
from jax.experimental.pallas import tpu as pltpu

F32 = jnp.float32
BF16 = jnp.bfloat16

D_MODEL = 1024
BATCH, SEQ = 16, 256
DEC_BATCH, DEC_SEQ = 2, 1024
DEPTH = 4
PAST_LEN = 512
GRID_W = 64
HALF_W = D_MODEL // 2
H_A, DV_A, DK_A = 4, 128, 64
GATE_RANK = 16
GLA_TAU = 16.0
GLA_CHUNK = 64
HD_B, H_B, KV_B = 128, 4, 2
HY_W = HALF_W
FILT_EMB, FILT_HID = 33, 64
FILT_BANDS = (FILT_EMB - 1) // 2
HY_MIN_DECAY = math.log(1e-2) / 1.5
HY_MAX_DECAY = math.log(1e-2) / 0.3
H_D, V_D, NOPE_D, ROPE_D = 4, 128, 128, 64
Q_RANK, KV_RANK = 256, 128
FFN_H = 2816
ROPE_THETA = 10000.0
EPS = 1e-6

LANES = 128
VMEM_LIMIT = 56 * 1024 * 1024

MOD_ROWS = 1024
TM = 1024
TM_IN = 512
TM_FFN = 2048
EVEN_W = 2688
ODD_W = 2048
FFN_TN = 256
QB = 256


def _params(n_grid):
    return pltpu.CompilerParams(dimension_semantics=("arbitrary",) * n_grid, vmem_limit_bytes=VMEM_LIMIT)


def _nt(a, b):
    return lax.dot_general(a, b, (((1,), (1,)), ((), ())), preferred_element_type=F32)


def _mm(a, b):
    return jnp.dot(a, b, preferred_element_type=F32)


def _rms(x):
    return x * lax.rsqrt(jnp.mean(x * x, axis=-1, keepdims=True) + EPS)


def _mod_kernel(c_ref, w_ref, b_ref, o_ref):
    cv = c_ref[...]
    s = cv * jax.nn.sigmoid(cv)
    o_ref[...] = _mm(s.astype(BF16), w_ref[...].astype(BF16)) + b_ref[...]


def _modulation(cvec, w_mod, b_mod):
    return pl.pallas_call(
        _mod_kernel,
        grid=(DEPTH, 6),
        in_specs=[
            pl.BlockSpec((8, D_MODEL), lambda l, n: (0, 0)),
            pl.BlockSpec((None, D_MODEL, D_MODEL), lambda l, n: (l, 0, n)),
            pl.BlockSpec((None, 1, D_MODEL), lambda l, n: (l, 0, n)),
        ],
        out_specs=pl.BlockSpec((None, None, 8, D_MODEL), lambda l, n: (l, n, 0, 0)),
        out_shape=jax.ShapeDtypeStruct((DEPTH, 6, 8, D_MODEL), F32),
        compiler_params=_params(2),
        name="adaln_mod",
    )(cvec, w_mod, b_mod.reshape(DEPTH, 1, 6 * D_MODEL))


def _mod_row(row0, rstep, tile_rows, sub):
    if tile_rows >= MOD_ROWS:
        return row0 + rstep * (pl.program_id(0) * (tile_rows // MOD_ROWS) + sub)
    return row0 + rstep * (pl.program_id(0) // (MOD_ROWS // tile_rows))


def _in_proj_kernel(x_ref, sh_ref, sc_ref, wt_ref, o_ref, keep_ref, wb_ref, *, row_groups, keep, row0, rstep):
    @pl.when(pl.program_id(0) == 0)
    def _():
        wb_ref[...] = jnp.zeros_like(wb_ref)
        for src, dst, size in row_groups:
            wb_ref[dst:dst + size, :] = wt_ref[src:src + size, :].astype(BF16)

    g = _mod_row(row0, rstep, x_ref.shape[0], 0)
    h = (_rms(x_ref[...]) * (1.0 + sc_ref[pl.ds(g, 1), :]) + sh_ref[pl.ds(g, 1), :]).astype(BF16)
    y = _nt(h, wb_ref[...])
    o_ref[...] = y.astype(o_ref.dtype)
    keep_ref[...] = y[:, keep[0]:keep[0] + keep[1]]


def _in_proj(x, mod, layer, wt, w_layer, row_groups, n, keep, row0, rstep):
    m = x.shape[0]
    return pl.pallas_call(
        functools.partial(_in_proj_kernel, row_groups=row_groups, keep=keep, row0=row0, rstep=rstep),
        grid=(m // TM_IN,),
        in_specs=[pl.BlockSpec((TM_IN, D_MODEL), lambda i: (i, 0)),
                  pl.BlockSpec((None, None, 8, D_MODEL), lambda i: (layer, 0, 0, 0)),
                  pl.BlockSpec((None, None, 8, D_MODEL), lambda i: (layer, 1, 0, 0)),
                  pl.BlockSpec((None, wt.shape[1], D_MODEL), lambda i: (w_layer, 0, 0), pipeline_mode=pl.Buffered(1))],
        out_specs=[pl.BlockSpec((TM_IN, n), lambda i: (i, 0)), pl.BlockSpec((TM_IN, keep[1]), lambda i: (i, 0))],
        out_shape=[jax.ShapeDtypeStruct((m, n), BF16), jax.ShapeDtypeStruct((m, keep[1]), F32)],
        scratch_shapes=[pltpu.VMEM((n, D_MODEL), BF16)],
        compiler_params=_params(1),
        name="norm_mod_proj",
    )(x, mod, mod, wt)


def _ffn_kernel(x_ref, sh_ref, sc_ref, gate_ref, wg_ref, wu_ref, wd_ref, *refs, row0, rstep, final):
    (gf_ref, o_ref, h_ref) = refs if final else (None,) + refs
    n_sub = x_ref.shape[0] // MOD_ROWS
    subs = [(slice(s * MOD_ROWS, (s + 1) * MOD_ROWS), _mod_row(row0, rstep, x_ref.shape[0], s)) for s in range(n_sub)]

    @pl.when(pl.program_id(1) == 0)
    def _():
        for rows, g in subs:
            x = x_ref[rows, :]
            o_ref[rows, :] = x
            h_ref[rows, :] = (_rms(x) * (1.0 + sc_ref[pl.ds(g, 1), :]) + sh_ref[pl.ds(g, 1), :]).astype(BF16)

    wg = wg_ref[...].astype(BF16)
    wu = wu_ref[...].astype(BF16)
    wd = wd_ref[...].astype(BF16)
    for rows, g in subs:
        h = h_ref[rows, :]
        a = _mm(h, wg)
        act = (a * jax.nn.sigmoid(a) * _mm(h, wu)).astype(BF16)
        o_ref[rows, :] += gate_ref[pl.ds(g, 1), :] * _mm(act, wd)

    if final:
        @pl.when(pl.program_id(1) == pl.num_programs(1) - 1)
        def _():
            for rows, _ in subs:
                o_ref[rows, :] = _rms(o_ref[rows, :]) * gf_ref[...]


def _ffn(x, mod, layer, w_in, w_out, row0, rstep, final_gain=None):
    m = x.shape[0]
    nj = FFN_H // FFN_TN
    mod_spec = lambda k: pl.BlockSpec((None, None, 8, D_MODEL), lambda i, j: (layer, k, 0, 0))
    final = final_gain is not None
    extra_specs = [pl.BlockSpec((1, D_MODEL), lambda i, j: (0, 0))] if final else []
    extra_args = [final_gain.reshape(1, D_MODEL)] if final else []
    return pl.pallas_call(
        functools.partial(_ffn_kernel, row0=row0, rstep=rstep, final=final),
        grid=(m // TM_FFN, nj),
        in_specs=[pl.BlockSpec((TM_FFN, D_MODEL), lambda i, j: (i, 0)), mod_spec(3), mod_spec(4), mod_spec(5),
                  pl.BlockSpec((None, D_MODEL, FFN_TN), lambda i, j: (layer, 0, j)),
                  pl.BlockSpec((None, D_MODEL, FFN_TN), lambda i, j: (layer, 0, j + nj)),
                  pl.BlockSpec((None, FFN_TN, D_MODEL), lambda i, j: (layer, j, 0))] + extra_specs,
        out_specs=pl.BlockSpec((TM_FFN, D_MODEL), lambda i, j: (i, 0)),
        out_shape=jax.ShapeDtypeStruct((m, D_MODEL), F32),
        scratch_shapes=[pltpu.VMEM((TM_FFN, D_MODEL), BF16)],
        compiler_params=_params(2),
        name="ffn_residual",
    )(x, mod, mod, mod, w_in, w_in, w_out, *extra_args)


def _proj_res_kernel(*refs, n_act, row0, rstep):
    acts, ws = refs[:n_act], refs[n_act:2 * n_act]
    x_ref, gate_ref, o_ref = refs[2 * n_act:]
    g = row0 + rstep * pl.program_id(0)
    acc = _mm(acts[0][...], ws[0][...].astype(BF16))
    for a_ref, w_ref in zip(acts[1:], ws[1:]):
        acc = acc + _mm(a_ref[...], w_ref[...].astype(BF16))
    o_ref[...] = x_ref[...] + gate_ref[pl.ds(g, 1), :] * acc


def _out_proj(acts, w, w_layer, x, mod, layer, k_gate, row0, rstep):
    m = x.shape[0]
    n_act = len(acts)
    kw = acts[0].shape[1]
    act_specs = [pl.BlockSpec((TM, kw), lambda i: (i, 0)) for _ in acts]
    w_specs = [pl.BlockSpec((None, kw, D_MODEL), functools.partial(lambda i, p: (w_layer, p, 0), p=p),
                            pipeline_mode=pl.Buffered(1)) for p in range(n_act)]
    return pl.pallas_call(
        functools.partial(_proj_res_kernel, n_act=n_act, row0=row0, rstep=rstep),
        grid=(m // TM,),
        in_specs=act_specs + w_specs + [
            pl.BlockSpec((TM, D_MODEL), lambda i: (i, 0)),
            pl.BlockSpec((None, None, 8, D_MODEL), lambda i: (layer, k_gate, 0, 0)),
        ],
        out_specs=pl.BlockSpec((TM, D_MODEL), lambda i: (i, 0)),
        out_shape=jax.ShapeDtypeStruct((m, D_MODEL), F32),
        compiler_params=_params(1),
        name="out_proj_residual",
    )(*acts, *([w] * n_act), x, mod)


def _gqa_kernel(*refs, sample):
    if sample:
        q_ref, k_ref, v_ref, gq_ref, gk_ref, ck_ref, cv_ref, cos_ref, sin_ref, o_ref, kb_ref, vb_ref = refs
    else:
        q_ref, k_ref, v_ref, gq_ref, gk_ref, o_ref, kn_ref, kb_ref, vb_ref = refs
    qi = pl.program_id(1)
    n_new = k_ref.shape[0]
    past = PAST_LEN if sample else 0
    rep = H_B // KV_B

    @pl.when(qi == 0)
    def _():
        for g in range(KV_B):
            sl = slice(HD_B * g, HD_B * (g + 1))
            kn = _rms(k_ref[:, sl].astype(F32)) * gk_ref[...]
            if sample:
                kb_ref[0:past, sl] = ck_ref[:, sl].astype(BF16)
                vb_ref[g, 0:past, 0:HD_B] = cv_ref[:, sl].astype(BF16)
                kn = kn * cos_ref[...] + pltpu.roll(kn, HD_B // 2, 1) * sin_ref[...]
            else:
                kn_ref[:, sl] = kn
            kb_ref[past:past + n_new, sl] = kn.astype(BF16)
            vb_ref[g, past:past + n_new, 0:HD_B] = v_ref[:, sl].astype(BF16)
            vb_ref[g, :, HD_B:] = jnp.ones((past + n_new, HD_B), BF16)

    r0 = pl.multiple_of(qi * QB, QB)
    qs = []
    for h in range(H_B):
        qn = _rms(q_ref[:, HD_B * h:HD_B * (h + 1)].astype(F32)) * gq_ref[...]
        if sample:
            qn = qn * cos_ref[pl.ds(r0, QB), :] + pltpu.roll(qn, HD_B // 2, 1) * sin_ref[pl.ds(r0, QB), :]
        qs.append((qn * (HD_B ** -0.5)).astype(BF16))
    scores = [_nt(qs[h], kb_ref[:, HD_B * (h // rep):HD_B * (h // rep + 1)]) for h in range(H_B)]
    weights = [jnp.exp(s - jnp.max(s, axis=-1, keepdims=True)).astype(BF16) for s in scores]
    sums = [_mm(weights[h], vb_ref[h // rep]) for h in range(H_B)]
    for h in range(H_B):
        o_ref[:, HD_B * h:HD_B * (h + 1)] = (sums[h][:, :HD_B] / sums[h][:, HD_B:]).astype(o_ref.dtype)


def _gqa(proj, g_q, g_k, n_batch, seq, ctx=None, rope=None):
    sample = ctx is not None
    m = n_batch * seq
    nq = seq // QB
    in_specs = [
        pl.BlockSpec((QB, 512), lambda b, i: (b * nq + i, 3)),
        pl.BlockSpec((seq, 256), lambda b, i: (b, 8)),
        pl.BlockSpec((seq, 256), lambda b, i: (b, 9)),
        pl.BlockSpec((1, HD_B), lambda b, i: (0, 0)),
        pl.BlockSpec((1, HD_B), lambda b, i: (0, 0)),
    ]
    args = [proj, proj, proj, g_q.reshape(1, HD_B), g_k.reshape(1, HD_B)]
    o_spec = pl.BlockSpec((QB, 512), lambda b, i: (b * nq + i, 0))
    o_shape = jax.ShapeDtypeStruct((m, 512), BF16)
    if sample:
        in_specs += [
            pl.BlockSpec((None, PAST_LEN, 256), lambda b, i: (b, 0, 0)),
            pl.BlockSpec((None, PAST_LEN, 256), lambda b, i: (b, 0, 0)),
            pl.BlockSpec((seq, HD_B), lambda b, i: (0, 0)),
            pl.BlockSpec((seq, HD_B), lambda b, i: (0, 0)),
        ]
        args += [ctx[0], ctx[1], rope[0], rope[1]]
        out_specs, out_shape = o_spec, o_shape
    else:
        out_specs = [o_spec, pl.BlockSpec((seq, 256), lambda b, i: (b, 0))]
        out_shape = [o_shape, jax.ShapeDtypeStruct((m, 256), F32)]
    n_keys = seq + (PAST_LEN if sample else 0)
    return pl.pallas_call(
        functools.partial(_gqa_kernel, sample=sample),
        grid=(n_batch, nq),
        in_specs=in_specs,
        out_specs=out_specs,
        out_shape=out_shape,
        scratch_shapes=[pltpu.VMEM((n_keys, KV_B * HD_B), BF16), pltpu.VMEM((KV_B, n_keys, 2 * HD_B), BF16)],
        compiler_params=_params(2),
        name="gqa_sample" if sample else "gqa_prompt",
    )(*args)


def _mla_kernel(*refs, sample):
    if sample:
        (cq_ref, ckv_ref, kpe_ref, gq_ref, wqb_ref, gkv_ref, wkvb_ref, ex_ref, cckv_ref, ckpe_ref, c4_ref, s4_ref,
         o_ref, kv_s, kx_s) = refs
    else:
        cq_ref, ckv_ref, kpe_ref, gq_ref, wqb_ref, gkv_ref, wkvb_ref, ex_ref, o_ref, ckvn_ref, kv_s, kx_s = refs
    qi = pl.program_id(1)
    n_new = ckv_ref.shape[0]
    past = PAST_LEN if sample else 0

    def rope(x, c, s):
        x1, x2 = x[:, :LANES], x[:, LANES:]
        return jnp.concatenate([x1 * c - x2 * s, x1 * s + x2 * c], axis=1)

    hw = NOPE_D + 2 * V_D

    def stage_kv(rows, kv):
        for h in range(H_D):
            kv_s[rows, hw * h:hw * h + NOPE_D + V_D] = kv[:, 256 * h:256 * (h + 1)].astype(BF16)

    @pl.when(qi == 0)
    def _():
        wkvb = wkvb_ref[...].astype(BF16)
        ckvn = _rms(ckv_ref[...].astype(F32)) * gkv_ref[...]
        if not sample:
            ckvn_ref[...] = ckvn
        stage_kv(slice(past, past + n_new), _mm(ckvn.astype(BF16), wkvb))
        for h in range(H_D):
            kv_s[:, hw * h + NOPE_D + V_D:hw * (h + 1)] = jnp.ones((past + n_new, V_D), BF16)
        kx = _mm(kpe_ref[:, 0:ROPE_D], ex_ref[...])
        if sample:
            stage_kv(slice(0, past), _mm(cckv_ref[...].astype(BF16), wkvb))
            kx_s[0:past, :] = _mm(ckpe_ref[...].astype(BF16), ex_ref[...]).astype(BF16)
            kx = rope(kx, c4_ref[...], s4_ref[...])
        kx_s[past:past + n_new, :] = kx.astype(BF16)

    q = _mm((_rms(cq_ref[...].astype(F32)) * gq_ref[...]).astype(BF16), wqb_ref[...].astype(BF16))
    q = q * ((NOPE_D + ROPE_D) ** -0.5)
    qpe = q[:, 4 * NOPE_D:]
    if sample:
        r0 = pl.multiple_of(qi * QB, QB)
        qpe = rope(qpe, c4_ref[pl.ds(r0, QB), :], s4_ref[pl.ds(r0, QB), :])
    lane_head = (lax.broadcasted_iota(jnp.int32, (1, 2 * LANES), 1) % LANES) // (ROPE_D // 2)
    qb, qpb = q.astype(BF16), qpe.astype(BF16)
    scores = [_nt(qb[:, NOPE_D * h:NOPE_D * (h + 1)], kv_s[:, hw * h:hw * h + NOPE_D])
              + _nt(jnp.where(lane_head == h, qpb, jnp.zeros_like(qpb)), kx_s[...]) for h in range(H_D)]
    weights = [jnp.exp(s - jnp.max(s, axis=-1, keepdims=True)).astype(BF16) for s in scores]
    sums = [_mm(weights[h], kv_s[:, hw * h + NOPE_D:hw * (h + 1)]) for h in range(H_D)]
    for h in range(H_D):
        o_ref[:, V_D * h:V_D * (h + 1)] = (sums[h][:, :V_D] / sums[h][:, V_D:]).astype(o_ref.dtype)


def _mla(proj, g_q, w_qb, g_kv, w_kvb, n_batch, seq, ctx=None, rope=None, expand=None):
    sample = ctx is not None
    m = n_batch * seq
    nq = seq // QB
    in_specs = [
        pl.BlockSpec((QB, Q_RANK), lambda b, i: (b * nq + i, 6)),
        pl.BlockSpec((seq, KV_RANK), lambda b, i: (b, 14)),
        pl.BlockSpec((seq, LANES), lambda b, i: (b, 15)),
        pl.BlockSpec((1, Q_RANK), lambda b, i: (0, 0)),
        pl.BlockSpec((Q_RANK, 768), lambda b, i: (0, 0)),
        pl.BlockSpec((1, KV_RANK), lambda b, i: (0, 0)),
        pl.BlockSpec((KV_RANK, 1024), lambda b, i: (0, 0)),
        pl.BlockSpec((ROPE_D, 256), lambda b, i: (0, 0)),
    ]
    args = [proj, proj, proj, g_q.reshape(1, Q_RANK), w_qb, g_kv.reshape(1, KV_RANK), w_kvb, expand]
    o_spec = pl.BlockSpec((QB, 512), lambda b, i: (b * nq + i, 0))
    o_shape = jax.ShapeDtypeStruct((m, 512), BF16)
    if sample:
        in_specs += [
            pl.BlockSpec((None, PAST_LEN, KV_RANK), lambda b, i: (b, 0, 0)),
            pl.BlockSpec((None, PAST_LEN, ROPE_D), lambda b, i: (b, 0, 0)),
            pl.BlockSpec((seq, LANES), lambda b, i: (0, 0)),
            pl.BlockSpec((seq, LANES), lambda b, i: (0, 0)),
        ]
        args += [ctx[0], ctx[1], rope[0], rope[1]]
        out_specs, out_shape = o_spec, o_shape
    else:
        out_specs = [o_spec, pl.BlockSpec((seq, KV_RANK), lambda b, i: (b, 0))]
        out_shape = [o_shape, jax.ShapeDtypeStruct((m, KV_RANK), F32)]
    n_keys = seq + (PAST_LEN if sample else 0)
    return pl.pallas_call(
        functools.partial(_mla_kernel, sample=sample),
        grid=(n_batch, nq),
        in_specs=in_specs,
        out_specs=out_specs,
        out_shape=out_shape,
        scratch_shapes=[pltpu.VMEM((n_keys, H_D * (NOPE_D + 2 * V_D)), BF16), pltpu.VMEM((n_keys, 256), BF16)],
        compiler_params=_params(2),
        name="mla_sample" if sample else "mla_prompt",
    )(*args)


def _dft(table, x):
    return _mm(table.astype(BF16), x.astype(BF16))


def _filter_kernel(z_ref, wf1_ref, bf1_ref, fr_ref, wf2_ref, bf2_ref, wf3_ref, t_ref, dl_ref,
                   c_ref, s_ref, gre_ref, gim_ref):
    n_tok = z_ref.shape[0]
    fr = fr_ref[...]
    hid = jnp.sin(fr * (_mm(z_ref[...].astype(BF16), wf1_ref[...].astype(BF16)) + bf1_ref[...]))
    hid = jnp.sin(fr * (_mm(hid.astype(BF16), wf2_ref[...].astype(BF16)) + bf2_ref[...]))
    filt = _mm(hid.astype(BF16), wf3_ref[...].astype(BF16))
    decay = jnp.exp(-t_ref[...] * dl_ref[...])
    row = lax.broadcasted_iota(jnp.int32, (n_tok, 1), 0)
    h_f = filt[:, :HY_W] * decay
    h_b = jnp.where(row == 0, 0.0, filt[:, HY_W:] * decay)
    p, m = h_f + h_b, h_f - h_b
    g_re = _dft(c_ref[...], p)
    g_im = _dft(s_ref[...], m)
    sign = jnp.where(row % 2 == 0, 1.0, -1.0)
    nyquist = jnp.sum(p * sign, axis=0, keepdims=True)
    g_im = jnp.where(row == 0, nyquist, g_im)
    wk = jnp.where(row == 0, 0.5 / n_tok, 1.0 / n_tok)
    gre_ref[...] = g_re * wk
    gim_ref[...] = g_im * wk


def _filter_spectrum(z, wf1, bf1, freq, wf2, bf2, wf3, t_col, deltas, tabs):
    n_tok = z.shape[0]
    out = jax.ShapeDtypeStruct((n_tok, HY_W), F32)
    return pl.pallas_call(
        _filter_kernel,
        out_shape=[out, out],
        compiler_params=pltpu.CompilerParams(vmem_limit_bytes=VMEM_LIMIT),
        name="hyena_filter",
    )(z, wf1, bf1.reshape(1, FILT_HID), freq.reshape(1, FILT_HID), wf2, bf2.reshape(1, FILT_HID), wf3,
      t_col, deltas, tabs[0], tabs[1])


HY_CT = 256


HY_ROWS = 1024


def _hyena_kernel(u0_ref, u1_ref, u2_ref, w0_ref, w1_ref, w2_ref, b0_ref, b1_ref, b2_ref, skip_ref,
                  gre_ref, gim_ref, cf_ref, sf_ref, stf_ref, o_ref, c_ref, s_ref, st_ref):
    seq = c_ref.shape[0]
    n_rows = u0_ref.shape[0]
    n_seq = n_rows // seq
    pos = lax.broadcasted_iota(jnp.int32, (n_rows, 1), 0) % seq

    @pl.when((pl.program_id(0) == 0) & (pl.program_id(1) == 0))
    def _():
        c_ref[...] = cf_ref[...].astype(BF16)
        s_ref[...] = sf_ref[...].astype(BF16)
        st_ref[...] = stf_ref[...].astype(BF16)

    def short_conv(u_ref, w_ref, b_ref):
        x, w = u_ref[...].astype(F32), w_ref[...]
        prev = jnp.where(pos == 0, 0.0, pltpu.roll(x, 1, 0))
        nxt = jnp.where(pos == seq - 1, 0.0, pltpu.roll(x, n_rows - 1, 0))
        return prev * w[0:1] + x * w[1:2] + nxt * w[2:3] + b_ref[...]

    def side_by_side(a):
        return a if n_seq == 1 else jnp.concatenate([a[s * seq:(s + 1) * seq] for s in range(n_seq)], axis=1)

    def stacked(a):
        ct = a.shape[1] // n_seq
        return a if n_seq == 1 else jnp.concatenate([a[:, s * ct:(s + 1) * ct] for s in range(n_seq)], axis=0)

    x0 = short_conv(u0_ref, w0_ref, b0_ref)
    gv = short_conv(u1_ref, w1_ref, b1_ref) * short_conv(u2_ref, w2_ref, b2_ref)
    sig = side_by_side(gv).astype(BF16)
    u_re = _mm(c_ref[...], sig)
    u_im = _mm(s_ref[...], sig)
    g_re = jnp.concatenate([gre_ref[...]] * n_seq, axis=1)
    g_im = jnp.concatenate([gim_ref[...]] * n_seq, axis=1)
    bin0 = lax.broadcasted_iota(jnp.int32, (seq, 1), 0) == 0
    p_im = u_im * g_im
    y_re = u_re * g_re - jnp.where(bin0, 0.0, p_im)
    y_im = jnp.where(bin0, p_im, u_re * g_im + u_im * g_re)
    y = stacked(_mm(c_ref[...], y_re.astype(BF16)) + _mm(st_ref[...], y_im.astype(BF16)))
    o_ref[...] = (x0 * (y + gv * skip_ref[...])).astype(o_ref.dtype)


def _hyena(proj, w_conv, b_conv, skip, g_re, g_im, tabs, n_batch, seq):
    nct = HY_W // HY_CT
    u_specs = [pl.BlockSpec((HY_ROWS, HY_CT), functools.partial(lambda b, c, g: (b, g * nct + c), g=g)) for g in range(3)]
    w_specs = [pl.BlockSpec((3, HY_CT), functools.partial(lambda b, c, g: (0, g * nct + c), g=g)) for g in range(3)]
    b_specs = [pl.BlockSpec((1, HY_CT), functools.partial(lambda b, c, g: (0, g * nct + c), g=g)) for g in range(3)]
    tab_spec = pl.BlockSpec((seq, seq), lambda b, c: (0, 0))
    return pl.pallas_call(
        _hyena_kernel,
        grid=(n_batch * seq // HY_ROWS, nct),
        in_specs=u_specs + w_specs + b_specs + [
            pl.BlockSpec((1, HY_CT), lambda b, c: (0, c)),
            pl.BlockSpec((seq, HY_CT), lambda b, c: (0, c)),
            pl.BlockSpec((seq, HY_CT), lambda b, c: (0, c)),
        ] + [tab_spec] * 3,
        out_specs=pl.BlockSpec((HY_ROWS, HY_CT), lambda b, c: (b, c)),
        out_shape=jax.ShapeDtypeStruct((n_batch * seq, HY_W), BF16),
        scratch_shapes=[pltpu.VMEM((seq, seq), BF16)] * 3,
        compiler_params=_params(2),
        name="hyena_conv",
    )(proj, proj, proj, w_conv, w_conv, w_conv, b_conv, b_conv, b_conv, skip.reshape(1, HY_W), g_re, g_im, *tabs)


def _dft_tables(n_tok):
    k = np.arange(n_tok)[:, None]
    s = np.arange(n_tok)[None, :]
    ang = ((k * s) % (2 * n_tok)) * (np.pi / n_tok)
    cos_t = np.cos(ang)
    sin_f = np.where(k == 0, np.where(s % 2 == 0, 1.0, -1.0), -np.sin(ang))
    return [jnp.asarray(t, F32) for t in (cos_t, sin_f, sin_f.T)]


GLA_LEVELS = (32, 16, 8, 4, 2, 1)
GLA_SAFE_DECAY = 60.0
GLA_GROUP = 2


def _gla_constants():
    c = GLA_CHUNK
    idx = np.arange(c)
    i, t = idx[:, None], idx[None, :]
    masks = []
    for s in GLA_LEVELS:
        upper = (idx % (2 * s)) >= s
        masks.append(((i // (2 * s)) == (t // (2 * s))) & upper[:, None] & (~upper)[None, :])
    masks.append(i == t)
    tri = t <= i
    fwd_m = np.stack([np.tile(m, (H_A, 1)) for m in masks]).astype(np.float32)
    bwd_m = np.stack([np.tile(m[::-1, ::-1], (H_A, 1)) for m in masks]).astype(np.float32)
    head_of_row = np.repeat(np.arange(H_A), c)[:, None]
    head_of_lane = np.repeat(np.arange(H_A), DK_A)[None, :]
    head_mask = head_of_row == head_of_lane
    return (jnp.asarray(tri, BF16), jnp.asarray(tri[::-1, ::-1], BF16), jnp.asarray(fwd_m), jnp.asarray(bwd_m),
            jnp.asarray(head_mask, BF16))


def _pair_reference(b, s, backward, row):
    c = GLA_CHUNK
    ref = s if backward else s - 1
    if 2 * s >= 8:
        pieces = [jnp.broadcast_to(b[p * 2 * s + ref:p * 2 * s + ref + 1, :], (2 * s, b.shape[1]))
                  for p in range(c // (2 * s))]
        return pieces[0] if len(pieces) == 1 else jnp.concatenate(pieces, axis=0)
    pos = row % (2 * s)
    out = None
    for o in range(2 * s):
        d = ref - o
        shifted = b if d == 0 else pltpu.roll(b, (-d) % c, 0)
        out = shifted if out is None else jnp.where(pos == o, shifted, out)
    return out


def _chunk_log_decay(la, t_ref):
    l1 = la.astype(BF16)
    r1 = la - l1.astype(F32)
    l2 = r1.astype(BF16)
    l3 = (r1 - l2.astype(F32)).astype(BF16)
    tmat = t_ref[...]
    return _mm(tmat, l1) + _mm(tmat, l2) + _mm(tmat, l3)


def _stack_heads(a, hm):
    ab = a.astype(BF16)
    return jnp.concatenate([ab] * H_A, axis=0) * hm


def _state_terms(k, v, b, b_last):
    c = GLA_CHUNK
    k_rest = (k * jnp.exp(b_last - b)).T
    carry = jnp.broadcast_to(jnp.exp(b_last), (2 * c, b.shape[1])).T
    return k_rest.astype(BF16), carry


def _gla_chunk(q, k, v, la, t_ref, m_ref, hm, s_ref, backward):
    c = GLA_CHUNK
    b = _chunk_log_decay(la, t_ref)
    row = lax.broadcasted_iota(jnp.int32, (c, 1), 0)
    last = 0 if backward else c - 1
    b_last = b[last:last + 1, :]
    scores = _nt(_stack_heads(q, hm), k.astype(BF16)) * m_ref[len(GLA_LEVELS)]
    for lvl, s in enumerate(GLA_LEVELS):
        is_query = (row % (2 * s) < s) if backward else (row % (2 * s) >= s)
        delta = b - _pair_reference(b, s, backward, row)
        x = jnp.exp(jnp.where(is_query, delta, -delta))
        scores = scores + _nt(_stack_heads(q * x, hm), (k * x).astype(BF16)) * m_ref[lvl]
    scores = scores.astype(BF16)
    state = s_ref[...]
    inter = _mm(_stack_heads(q * jnp.exp(b), hm), state.astype(BF16))
    k_rest, carry = _state_terms(k, v, b, b_last)
    outs = []
    for h in range(H_A):
        rows = slice(c * h, c * (h + 1))
        v_h = v[:, DV_A * h:DV_A * (h + 1)]
        outs.append(_mm(scores[rows], v_h) + inter[rows])
        s_ref[rows, :] = state[rows] * carry[rows] + _mm(k_rest[rows], v_h)
    return jnp.concatenate(outs, axis=1)


def _gla_local(items, hm):
    c = GLA_CHUNK
    bs = [_chunk_log_decay(la, t_ref) for _, _, _, la, t_ref, _, _ in items]
    b_lasts = [b[(0 if it[6] else c - 1):(0 if it[6] else c - 1) + 1, :] for b, it in zip(bs, items)]
    q_decayed = [_stack_heads(it[0] * jnp.exp(b), hm) for it, b in zip(items, bs)]
    k_grown = [(it[1] * jnp.exp(-b)).astype(BF16) for it, b in zip(items, bs)]
    raw = [_nt(qd, kg) for qd, kg in zip(q_decayed, k_grown)]
    scores = [(r * it[5]).astype(BF16) for r, it in zip(raw, items)]
    terms = [_state_terms(it[1], it[2], b, bl) for it, b, bl in zip(items, bs, b_lasts)]
    out = []
    for it, sc, (k_rest, carry), qd in zip(items, scores, terms, q_decayed):
        v = it[2]
        heads = [(slice(c * h, c * (h + 1)), v[:, DV_A * h:DV_A * (h + 1)]) for h in range(H_A)]
        intra = jnp.concatenate([_mm(sc[rows], v_h) for rows, v_h in heads], axis=1)
        incr = jnp.concatenate([_mm(k_rest[rows], v_h) for rows, v_h in heads], axis=0)
        out.append((intra, qd, incr, carry))
    return out


def _gla_kernel(*refs, sample):
    if sample:
        (x_ref, z_ref, wf_ref, bf_ref, wb_ref, bb_ref, tf_ref, tb_ref, mf_ref, mb_ref, hm_ref, gn_ref, sf0_ref, sb0_ref,
         o_ref, la_f, la_b, o_f, o_b, s_f, s_b, qd_f, qd_b, ds_f, ds_b, cr_f, cr_b) = refs
    else:
        (x_ref, z_ref, wf_ref, bf_ref, wb_ref, bb_ref, tf_ref, tb_ref, mf_ref, mb_ref, hm_ref, gn_ref,
         o_ref, sf_out, sb_out, la_f, la_b, o_f, o_b, s_f, s_b, qd_f, qd_b, ds_f, ds_b, cr_f, cr_b) = refs
    n_tok = x_ref.shape[0]
    n_chunks = n_tok // GLA_CHUNK
    hk, hv = H_A * DK_A, H_A * DV_A
    zb = z_ref[...].astype(BF16)

    def log_sigmoid(t):
        return jnp.minimum(t, 0.0) - jnp.log(1.0 + jnp.exp(-jnp.abs(t)))

    la_f[...] = log_sigmoid(_mm(zb, wf_ref[...].astype(BF16)) + bf_ref[...]) / GLA_TAU
    la_b[...] = log_sigmoid(_mm(zb, wb_ref[...].astype(BF16)) + bb_ref[...]) / GLA_TAU
    if sample:
        s_f[...] = sf0_ref[...]
        s_b[...] = sb0_ref[...]
    else:
        s_f[...] = jnp.zeros_like(s_f)
        s_b[...] = jnp.zeros_like(s_b)
    hm = hm_ref[...]

    fwd = (la_f, tf_ref, mf_ref, s_f, o_f, qd_f, ds_f, cr_f, False)
    bwd = (la_b, tb_ref, mb_ref, s_b, o_b, qd_b, ds_b, cr_b, True)
    tri_f = jnp.sum(mf_ref[...], axis=0)
    tri_b = jnp.sum(mb_ref[...], axis=0)

    def chunk_rows(ci, backward):
        cidx = n_chunks - 1 - ci if backward else ci
        return cidx, pl.ds(pl.multiple_of(cidx * GLA_CHUNK, GLA_CHUNK), GLA_CHUNK)

    def load_qkv(rows):
        q = x_ref[rows, 0:hk].astype(F32) * (DK_A ** -0.5)
        return q, x_ref[rows, hk:2 * hk].astype(F32), x_ref[rows, 2 * hk:2 * hk + hv]

    def safe_step(ci, carry):
        for la_ref, t_ref, m_ref, s_ref, out_ref, _, _, _, backward in (fwd, bwd):
            _, rows = chunk_rows(ci, backward)
            out_ref[rows, :] = _gla_chunk(*load_qkv(rows), la_ref[rows, :], t_ref, m_ref, hm, s_ref, backward)
        return carry

    def local_step(gi, carry):
        items, dests = [], []
        for (la_ref, t_ref, _, _, out_ref, qd_ref, ds_ref, cr_ref, backward), tri in ((fwd, tri_f), (bwd, tri_b)):
            for u in range(GLA_GROUP):
                cidx, rows = chunk_rows(gi * GLA_GROUP + u, backward)
                items.append((*load_qkv(rows), la_ref[rows, :], t_ref, tri, backward))
                dests.append((out_ref, rows, qd_ref, ds_ref, cr_ref, cidx))
        for (out_ref, rows, qd_ref, ds_ref, cr_ref, cidx), (intra, qd, incr, factor) in zip(dests, _gla_local(items, hm)):
            out_ref[rows, :] = intra
            qd_ref[cidx] = qd
            ds_ref[cidx] = incr
            cr_ref[cidx] = factor
        return carry

    def scan_step(ci, carry):
        for _, _, _, s_ref, out_ref, qd_ref, ds_ref, cr_ref, backward in (fwd, bwd):
            cidx, rows = chunk_rows(ci, backward)
            state = s_ref[...]
            inter = _mm(qd_ref[cidx], state.astype(BF16))
            out_ref[rows, :] += jnp.concatenate(
                [inter[GLA_CHUNK * h:GLA_CHUNK * (h + 1)] for h in range(H_A)], axis=1)
            s_ref[...] = state * cr_ref[cidx] + ds_ref[cidx]
        return carry

    chunk_sums = [jnp.sum(ref[...].reshape(n_chunks, GLA_CHUNK, hk), axis=1) for ref in (la_f, la_b)]
    mild = jnp.minimum(jnp.min(chunk_sums[0]), jnp.min(chunk_sums[1])) > -GLA_SAFE_DECAY

    @pl.when(mild)
    def _():
        lax.fori_loop(0, n_chunks // GLA_GROUP, local_step, 0, unroll=2)
        lax.fori_loop(0, n_chunks, scan_step, 0, unroll=2)

    @pl.when(jnp.logical_not(mild))
    def _():
        lax.fori_loop(0, n_chunks, safe_step, 0)
    if not sample:
        sf_out[...] = s_f[...]
        sb_out[...] = s_b[...]
    gain = gn_ref[...]
    for h in range(H_A):
        cols = slice(DV_A * h, DV_A * (h + 1))
        r = x_ref[:, 2 * hk + hv + DV_A * h:2 * hk + hv + DV_A * (h + 1)].astype(F32)
        o_ref[:, cols] = (_rms(o_f[:, cols] + o_b[:, cols]) * gain * (r * jax.nn.sigmoid(r))).astype(o_ref.dtype)


def _gla(proj, w_gf, b_gf, w_gb, b_gb, g_norm, consts, n_batch, seq, ctx=None):
    sample = ctx is not None
    hk, hv = H_A * DK_A, H_A * DV_A
    n_ch = seq // GLA_CHUNK
    full = lambda shape: pl.BlockSpec(shape, lambda b: (0,) * len(shape))
    in_specs = [
        pl.BlockSpec((seq, 2 * hk + 2 * hv), lambda b: (b, 0)),
        pl.BlockSpec((seq, LANES), lambda b: (b, EVEN_W // LANES - 1)),
        full((LANES, hk)), full((1, hk)), full((LANES, hk)), full((1, hk)),
        full(consts[0].shape), full(consts[1].shape), full(consts[2].shape), full(consts[3].shape), full(consts[4].shape),
        full((1, DV_A)),
    ]
    args = [proj, proj, w_gf, b_gf.reshape(1, hk), w_gb, b_gb.reshape(1, hk), *consts, g_norm.reshape(1, DV_A)]
    o_spec = pl.BlockSpec((seq, hv), lambda b: (b, 0))
    o_shape = jax.ShapeDtypeStruct((n_batch * seq, hv), BF16)
    st_spec = pl.BlockSpec((None, hk, DV_A), lambda b: (b, 0, 0))
    if sample:
        in_specs += [st_spec, st_spec]
        args += [ctx[0], ctx[1]]
        out_specs, out_shape = o_spec, o_shape
    else:
        st_shape = jax.ShapeDtypeStruct((n_batch, hk, DV_A), F32)
        out_specs, out_shape = [o_spec, st_spec, st_spec], [o_shape, st_shape, st_shape]
    return pl.pallas_call(
        functools.partial(_gla_kernel, sample=sample),
        grid=(n_batch,),
        in_specs=in_specs,
        out_specs=out_specs,
        out_shape=out_shape,
        scratch_shapes=[pltpu.VMEM((seq, hk), F32), pltpu.VMEM((seq, hk), F32),
                        pltpu.VMEM((seq, hv), F32), pltpu.VMEM((seq, hv), F32),
                        pltpu.VMEM((hk, DV_A), F32), pltpu.VMEM((hk, DV_A), F32),
                        pltpu.VMEM((n_ch, H_A * GLA_CHUNK, hk), BF16), pltpu.VMEM((n_ch, H_A * GLA_CHUNK, hk), BF16),
                        pltpu.VMEM((n_ch, hk, DV_A), F32), pltpu.VMEM((n_ch, hk, DV_A), F32),
                        pltpu.VMEM((n_ch, hk, DV_A), F32), pltpu.VMEM((n_ch, hk, DV_A), F32)],
        compiler_params=_params(1),
        name="gla_sample" if sample else "gla_prompt",
    )(*args)


def _axial_rope(n_tokens, dim):
    rows = n_tokens // GRID_W
    row = np.repeat(np.arange(rows), GRID_W).astype(np.float64)
    col = np.tile(np.arange(GRID_W), rows).astype(np.float64)
    n_freq = dim // 4
    inv = ROPE_THETA ** (-np.arange(n_freq) / n_freq)
    ang = np.concatenate([row[:, None] * inv, col[:, None] * inv], axis=-1)
    return np.cos(ang).astype(np.float32), np.sin(ang).astype(np.float32)


def _filter_features(n_tokens):
    t = np.linspace(0.0, 1.0, n_tokens)[:, None]
    w = 2.0 * np.pi * np.arange(n_tokens)[:, None] / n_tokens
    f = np.linspace(1e-4, FILT_BANDS - 1, FILT_BANDS)[None, :]
    z = np.concatenate([t, np.cos(f * w), -np.sin(f * w)], axis=-1)
    z = np.pad(z, ((0, 0), (0, LANES - FILT_EMB)))
    return jnp.asarray(z, F32), jnp.asarray(t, F32)


_KPE_EXPAND = np.array([(p // LANES) * (ROPE_D // 2) + p % (ROPE_D // 2) for p in range(2 * LANES)])
_QB_PERM = np.array(
    [192 * (p // NOPE_D) + p % NOPE_D for p in range(H_D * NOPE_D)]
    + [192 * (p // 32) + NOPE_D + p % 32 for p in range(H_D * 32)]
    + [192 * (p // 32) + NOPE_D + 32 + p % 32 for p in range(H_D * 32)])

EVEN_ROW_GROUPS = ((0, 0, 1536), (1568, 1536, 1024), (1536, EVEN_W - 2 * GATE_RANK, 2 * GATE_RANK))
ODD_ROW_GROUPS = ((0, 0, 1984),)
EVEN_KEEP = (2304, 256)
ODD_KEEP = (1920, LANES)


def kernel(x_prompt, x_sample, state_gla_fwd, state_gla_bwd, cache_gqa_k, cache_gqa_v, cache_mla_ckv, cache_mla_kpe, c, c_ctx, w_mod, b_mod, w_in_even, w_gla_gate_f, b_gla_gate_f, w_gla_gate_b, b_gla_gate_b, g_gla_norm, g_gqa_q, g_gqa_k, w_out_even, w_in_odd, w_hy_conv, b_hy_conv, hy_skip, w_filt1, b_filt1, filt_freq, w_filt2, b_filt2, w_filt3, g_mla_q, w_mla_qb, g_mla_kv, w_mla_kvb, w_out_odd, w_ffn_in, w_ffn_out, g_final):
    n_c, n_s = BATCH * SEQ, DEC_BATCH * DEC_SEQ
    cvec = jnp.concatenate([c_ctx[None, :], c, jnp.zeros((8 - 1 - DEC_BATCH, D_MODEL), F32)], axis=0)
    mod = _modulation(cvec, w_mod, b_mod)
    xc = x_prompt.reshape(n_c, D_MODEL)
    xs = x_sample.reshape(n_s, D_MODEL)
    rows_c, rows_s = (0, 0), (1, DEC_SEQ // MOD_ROWS)

    gla_consts = _gla_constants()
    cos_b, sin_b = _axial_rope(DEC_SEQ, HD_B)
    rope_b = (jnp.asarray(np.concatenate([cos_b, cos_b], axis=1)), jnp.asarray(np.concatenate([-sin_b, sin_b], axis=1)))
    cos_d, sin_d = _axial_rope(DEC_SEQ, ROPE_D)
    rope_d = (jnp.asarray(np.tile(cos_d, (1, H_D))), jnp.asarray(np.tile(sin_d, (1, H_D))))
    kpe_expand = jnp.asarray(np.arange(ROPE_D)[:, None] == _KPE_EXPAND[None, :], BF16)
    tabs_c, tabs_s = _dft_tables(SEQ), _dft_tables(DEC_SEQ)
    z_c, t_c = _filter_features(SEQ)
    z_s, t_s = _filter_features(DEC_SEQ)
    deltas = jnp.asarray(np.abs(np.linspace(HY_MIN_DECAY, HY_MAX_DECAY, HY_W))[None, :], F32)

    wt_even = jnp.swapaxes(w_in_even, 1, 2)
    wt_odd = jnp.swapaxes(w_in_odd, 1, 2)

    st_gf, st_gb, st_k, st_v, st_ckv, st_kpe = [], [], [], [], [], []
    for i in range(DEPTH):
        j = i // 2
        if i % 2 == 0:
            z0 = LANES - 2 * GATE_RANK
            pad_f = jnp.zeros((LANES, H_A * DK_A), F32).at[z0:z0 + GATE_RANK].set(w_gla_gate_f[j])
            pad_b = jnp.zeros((LANES, H_A * DK_A), F32).at[z0 + GATE_RANK:LANES].set(w_gla_gate_b[j])
            pc, v_new = _in_proj(xc, mod, i, wt_even, j, EVEN_ROW_GROUPS, EVEN_W, EVEN_KEEP, *rows_c)
            ps, _ = _in_proj(xs, mod, i, wt_even, j, EVEN_ROW_GROUPS, EVEN_W, EVEN_KEEP, *rows_s)
            gate_args = (pad_f, b_gla_gate_f[j], pad_b, b_gla_gate_b[j], g_gla_norm[j], gla_consts)
            a_c, s_f, s_b = _gla(pc, *gate_args, BATCH, SEQ)
            ctx_a = (state_gla_fwd[:, j].reshape(DEC_BATCH, H_A * DK_A, DV_A),
                     state_gla_bwd[:, j].reshape(DEC_BATCH, H_A * DK_A, DV_A))
            a_s = _gla(ps, *gate_args, DEC_BATCH, DEC_SEQ, ctx=ctx_a)
            b_c, k_norm = _gqa(pc, g_gqa_q[j], g_gqa_k[j], BATCH, SEQ)
            ctx_b = (cache_gqa_k[:, j].reshape(DEC_BATCH, PAST_LEN, KV_B * HD_B),
                     cache_gqa_v[:, j].reshape(DEC_BATCH, PAST_LEN, KV_B * HD_B))
            b_s = _gqa(ps, g_gqa_q[j], g_gqa_k[j], DEC_BATCH, DEC_SEQ, ctx=ctx_b, rope=rope_b)
            w_out = w_out_even
            st_gf.append(s_f.reshape(BATCH, H_A, DK_A, DV_A))
            st_gb.append(s_b.reshape(BATCH, H_A, DK_A, DV_A))
            st_k.append(k_norm.reshape(BATCH, SEQ, KV_B, HD_B))
            st_v.append(v_new.reshape(BATCH, SEQ, KV_B, HD_B))
        else:
            pc, kpe_new = _in_proj(xc, mod, i, wt_odd, j, ODD_ROW_GROUPS, ODD_W, ODD_KEEP, *rows_c)
            ps, _ = _in_proj(xs, mod, i, wt_odd, j, ODD_ROW_GROUPS, ODD_W, ODD_KEEP, *rows_s)
            wf1 = jnp.pad(w_filt1[j], ((0, LANES - FILT_EMB), (0, 0)))
            filt_args = (wf1, b_filt1[j], filt_freq[j], w_filt2[j], b_filt2[j], w_filt3[j])
            g_c = _filter_spectrum(z_c, *filt_args, t_c, deltas, tabs_c)
            g_s = _filter_spectrum(z_s, *filt_args, t_s, deltas, tabs_s)
            b_conv = b_hy_conv[j].reshape(1, 3 * HY_W)
            a_c = _hyena(pc, w_hy_conv[j], b_conv, hy_skip[j], g_c[0], g_c[1], tabs_c, BATCH, SEQ)
            a_s = _hyena(ps, w_hy_conv[j], b_conv, hy_skip[j], g_s[0], g_s[1], tabs_s, DEC_BATCH, DEC_SEQ)
            w_qb = w_mla_qb[j][:, _QB_PERM]
            b_c, ckv_norm = _mla(pc, g_mla_q[j], w_qb, g_mla_kv[j], w_mla_kvb[j], BATCH, SEQ, expand=kpe_expand)
            b_s = _mla(ps, g_mla_q[j], w_qb, g_mla_kv[j], w_mla_kvb[j], DEC_BATCH, DEC_SEQ,
                       ctx=(cache_mla_ckv[:, j], cache_mla_kpe[:, j]), rope=rope_d, expand=kpe_expand)
            w_out = w_out_odd
            st_ckv.append(ckv_norm.reshape(BATCH, SEQ, KV_RANK))
            st_kpe.append(kpe_new[:, :ROPE_D].reshape(BATCH, SEQ, ROPE_D))
        xc = _out_proj([a_c, b_c], w_out, j, xc, mod, i, 2, *rows_c)
        xs = _out_proj([a_s, b_s], w_out, j, xs, mod, i, 2, *rows_s)
        last = g_final if i == DEPTH - 1 else None
        xc = _ffn(xc, mod, i, w_ffn_in, w_ffn_out, *rows_c, final_gain=last)
        xs = _ffn(xs, mod, i, w_ffn_in, w_ffn_out, *rows_s, final_gain=last)
    y_prompt = xc.reshape(BATCH, SEQ, D_MODEL)
    y_sample = xs.reshape(DEC_BATCH, DEC_SEQ, D_MODEL)
    return (y_prompt, y_sample, jnp.stack(st_gf, axis=1), jnp.stack(st_gb, axis=1), jnp.stack(st_k, axis=1),
            jnp.stack(st_v, axis=1), jnp.stack(st_ckv, axis=1), jnp.stack(st_kpe, axis=1))
```

```python
import functools
import math

import numpy as np
import jax
import jax.numpy as jnp
from jax import lax
from jax.experimental import pallas as pl
from jax.experimental.pallas import tpu as pltpu

F32 = jnp.float32
BF16 = jnp.bfloat16

D_MODEL = 1024
BATCH, SEQ = 16, 256
DEC_BATCH, DEC_SEQ = 2, 1024
DEPTH = 4
PAST_LEN = 512
GRID_W = 64
HALF_W = D_MODEL // 2
H_A, DV_A, DK_A = 4, 128, 64
GATE_RANK = 16
GLA_TAU = 16.0
GLA_CHUNK = 64
HD_B, H_B, KV_B = 128, 4, 2
HY_W = HALF_W
FILT_EMB, FILT_HID = 33, 64
FILT_BANDS = (FILT_EMB - 1) // 2
HY_MIN_DECAY = math.log(1e-2) / 1.5
HY_MAX_DECAY = math.log(1e-2) / 0.3
H_D, V_D, NOPE_D, ROPE_D = 4, 128, 128, 64
Q_RANK, KV_RANK = 256, 128
FFN_H = 2816
ROPE_THETA = 10000.0
EPS = 1e-6

LANES = 128
VMEM_LIMIT = 56 * 1024 * 1024

MOD_ROWS = 1024
TM = 1024
TM_IN = 512
TM_FFN = 2048
EVEN_W = 2688
ODD_W = 2048
FFN_TN = 256
QB = 256


def _params(n_grid):
    return pltpu.CompilerParams(dimension_semantics=("arbitrary",) * n_grid, vmem_limit_bytes=VMEM_LIMIT)


def _nt(a, b):
    return lax.dot_general(a, b, (((1,), (1,)), ((), ())), preferred_element_type=F32)


def _mm(a, b):
    return jnp.dot(a, b, preferred_element_type=F32)


def _rms(x):
    return x * lax.rsqrt(jnp.mean(x * x, axis=-1, keepdims=True) + EPS)


def _mod_kernel(c_ref, w_ref, b_ref, o_ref):
    cv = c_ref[...]
    s = cv * jax.nn.sigmoid(cv)
    o_ref[...] = _mm(s.astype(BF16), w_ref[...].astype(BF16)) + b_ref[...]


def _modulation(cvec, w_mod, b_mod):
    return pl.pallas_call(
        _mod_kernel,
        grid=(DEPTH, 6),
        in_specs=[
            pl.BlockSpec((8, D_MODEL), lambda l, n: (0, 0)),
            pl.BlockSpec((None, D_MODEL, D_MODEL), lambda l, n: (l, 0, n)),
            pl.BlockSpec((None, 1, D_MODEL), lambda l, n: (l, 0, n)),
        ],
        out_specs=pl.BlockSpec((None, None, 8, D_MODEL), lambda l, n: (l, n, 0, 0)),
        out_shape=jax.ShapeDtypeStruct((DEPTH, 6, 8, D_MODEL), F32),
        compiler_params=_params(2),
        name="adaln_mod",
    )(cvec, w_mod, b_mod.reshape(DEPTH, 1, 6 * D_MODEL))


def _mod_row(row0, rstep, tile_rows, sub):
    if tile_rows >= MOD_ROWS:
        return row0 + rstep * (pl.program_id(0) * (tile_rows // MOD_ROWS) + sub)
    return row0 + rstep * (pl.program_id(0) // (MOD_ROWS // tile_rows))


def _in_proj_kernel(xc_ref, xs_ref, sh_ref, sc_ref, wt_ref, oc_ref, keep_ref, os_ref, wb_ref, *, row_groups, keep, n_c):
    i = pl.program_id(0)

    @pl.when(i == 0)
    def _():
        wb_ref[...] = jnp.zeros_like(wb_ref)
        for src, dst, size in row_groups:
            wb_ref[dst:dst + size, :] = wt_ref[src:src + size, :].astype(BF16)

    def project(x_ref, g):
        h = (_rms(x_ref[...]) * (1.0 + sc_ref[pl.ds(g, 1), :]) + sh_ref[pl.ds(g, 1), :]).astype(BF16)
        return _nt(h, wb_ref[...])

    @pl.when(i < n_c)
    def _():
        y = project(xc_ref, 0)
        oc_ref[...] = y.astype(oc_ref.dtype)
        keep_ref[...] = y[:, keep[0]:keep[0] + keep[1]]

    @pl.when(i >= n_c)
    def _():
        os_ref[...] = project(xs_ref, 1 + (i - n_c) // (MOD_ROWS // TM_IN)).astype(os_ref.dtype)


def _in_proj(xc, xs, mod, layer, wt, w_layer, row_groups, n, keep):
    n_c, n_s = xc.shape[0] // TM_IN, xs.shape[0] // TM_IN
    c_idx = lambda i: (jnp.minimum(i, n_c - 1), 0)
    s_idx = lambda i: (jnp.maximum(i - n_c, 0), 0)
    return pl.pallas_call(
        functools.partial(_in_proj_kernel, row_groups=row_groups, keep=keep, n_c=n_c),
        grid=(n_c + n_s,),
        in_specs=[pl.BlockSpec((TM_IN, D_MODEL), c_idx), pl.BlockSpec((TM_IN, D_MODEL), s_idx),
                  pl.BlockSpec((None, None, 8, D_MODEL), lambda i: (layer, 0, 0, 0)),
                  pl.BlockSpec((None, None, 8, D_MODEL), lambda i: (layer, 1, 0, 0)),
                  pl.BlockSpec((None, wt.shape[1], D_MODEL), lambda i: (w_layer, 0, 0), pipeline_mode=pl.Buffered(1))],
        out_specs=[pl.BlockSpec((TM_IN, n), c_idx), pl.BlockSpec((TM_IN, keep[1]), c_idx), pl.BlockSpec((TM_IN, n), s_idx)],
        out_shape=[jax.ShapeDtypeStruct((xc.shape[0], n), BF16), jax.ShapeDtypeStruct((xc.shape[0], keep[1]), F32),
                   jax.ShapeDtypeStruct((xs.shape[0], n), BF16)],
        scratch_shapes=[pltpu.VMEM((n, D_MODEL), BF16)],
        compiler_params=_params(1),
        name="norm_mod_proj",
    )(xc, xs, mod, mod, wt)


def _ffn_kernel(x_ref, sh_ref, sc_ref, gate_ref, wg_ref, wu_ref, wd_ref, *refs, row0, rstep, final):
    (gf_ref, o_ref, h_ref) = refs if final else (None,) + refs
    n_sub = x_ref.shape[0] // MOD_ROWS
    subs = [(slice(s * MOD_ROWS, (s + 1) * MOD_ROWS), _mod_row(row0, rstep, x_ref.shape[0], s)) for s in range(n_sub)]

    @pl.when(pl.program_id(1) == 0)
    def _():
        for rows, g in subs:
            x = x_ref[rows, :]
            o_ref[rows, :] = x
            h_ref[rows, :] = (_rms(x) * (1.0 + sc_ref[pl.ds(g, 1), :]) + sh_ref[pl.ds(g, 1), :]).astype(BF16)

    wg = wg_ref[...].astype(BF16)
    wu = wu_ref[...].astype(BF16)
    wd = wd_ref[...].astype(BF16)
    for rows, g in subs:
        h = h_ref[rows, :]
        a = _mm(h, wg)
        act = (a * jax.nn.sigmoid(a) * _mm(h, wu)).astype(BF16)
        o_ref[rows, :] += gate_ref[pl.ds(g, 1), :] * _mm(act, wd)

    if final:
        @pl.when(pl.program_id(1) == pl.num_programs(1) - 1)
        def _():
            for rows, _ in subs:
                o_ref[rows, :] = _rms(o_ref[rows, :]) * gf_ref[...]


def _ffn(x, mod, layer, w_in, w_out, row0, rstep, final_gain=None):
    m = x.shape[0]
    nj = FFN_H // FFN_TN
    mod_spec = lambda k: pl.BlockSpec((None, None, 8, D_MODEL), lambda i, j: (layer, k, 0, 0))
    final = final_gain is not None
    extra_specs = [pl.BlockSpec((1, D_MODEL), lambda i, j: (0, 0))] if final else []
    extra_args = [final_gain.reshape(1, D_MODEL)] if final else []
    return pl.pallas_call(
        functools.partial(_ffn_kernel, row0=row0, rstep=rstep, final=final),
        grid=(m // TM_FFN, nj),
        in_specs=[pl.BlockSpec((TM_FFN, D_MODEL), lambda i, j: (i, 0)), mod_spec(3), mod_spec(4), mod_spec(5),
                  pl.BlockSpec((None, D_MODEL, FFN_TN), lambda i, j: (layer, 0, j)),
                  pl.BlockSpec((None, D_MODEL, FFN_TN), lambda i, j: (layer, 0, j + nj)),
                  pl.BlockSpec((None, FFN_TN, D_MODEL), lambda i, j: (layer, j, 0))] + extra_specs,
        out_specs=pl.BlockSpec((TM_FFN, D_MODEL), lambda i, j: (i, 0)),
        out_shape=jax.ShapeDtypeStruct((m, D_MODEL), F32),
        scratch_shapes=[pltpu.VMEM((TM_FFN, D_MODEL), BF16)],
        compiler_params=_params(2),
        name="ffn_residual",
    )(x, mod, mod, mod, w_in, w_in, w_out, *extra_args)


def _proj_res_kernel(ac0_ref, ac1_ref, as0_ref, as1_ref, w0_ref, w1_ref, xc_ref, xs_ref, gate_ref, oc_ref, os_ref, *, n_c):
    i = pl.program_id(0)

    def mix(a0_ref, a1_ref, x_ref, g):
        acc = _mm(a0_ref[...], w0_ref[...].astype(BF16)) + _mm(a1_ref[...], w1_ref[...].astype(BF16))
        return x_ref[...] + gate_ref[pl.ds(g, 1), :] * acc

    @pl.when(i < n_c)
    def _():
        oc_ref[...] = mix(ac0_ref, ac1_ref, xc_ref, 0)

    @pl.when(i >= n_c)
    def _():
        os_ref[...] = mix(as0_ref, as1_ref, xs_ref, 1 + (i - n_c))


def _out_proj(acts_c, acts_s, w, w_layer, xc, xs, mod, layer, k_gate):
    n_c, n_s = xc.shape[0] // TM, xs.shape[0] // TM
    kw = acts_c[0].shape[1]
    c_idx = lambda i: (jnp.minimum(i, n_c - 1), 0)
    s_idx = lambda i: (jnp.maximum(i - n_c, 0), 0)
    w_specs = [pl.BlockSpec((None, kw, D_MODEL), functools.partial(lambda i, p: (w_layer, p, 0), p=p),
                            pipeline_mode=pl.Buffered(1)) for p in range(2)]
    return pl.pallas_call(
        functools.partial(_proj_res_kernel, n_c=n_c),
        grid=(n_c + n_s,),
        in_specs=[pl.BlockSpec((TM, kw), c_idx)] * 2 + [pl.BlockSpec((TM, kw), s_idx)] * 2 + w_specs + [
            pl.BlockSpec((TM, D_MODEL), c_idx), pl.BlockSpec((TM, D_MODEL), s_idx),
            pl.BlockSpec((None, None, 8, D_MODEL), lambda i: (layer, k_gate, 0, 0)),
        ],
        out_specs=[pl.BlockSpec((TM, D_MODEL), c_idx), pl.BlockSpec((TM, D_MODEL), s_idx)],
        out_shape=[jax.ShapeDtypeStruct(xc.shape, F32), jax.ShapeDtypeStruct(xs.shape, F32)],
        compiler_params=_params(1),
        name="out_proj_residual",
    )(*acts_c, *acts_s, w, w, xc, xs, mod)


def _gqa_kernel(*refs, sample):
    if sample:
        q_ref, k_ref, v_ref, gq_ref, gk_ref, ck_ref, cv_ref, cos_ref, sin_ref, o_ref, kb_ref, vb_ref = refs
    else:
        q_ref, k_ref, v_ref, gq_ref, gk_ref, o_ref, kn_ref, kb_ref, vb_ref = refs
    qi = pl.program_id(1)
    n_new = k_ref.shape[0]
    past = PAST_LEN if sample else 0
    rep = H_B // KV_B

    @pl.when(qi == 0)
    def _():
        for g in range(KV_B):
            sl = slice(HD_B * g, HD_B * (g + 1))
            kn = _rms(k_ref[:, sl].astype(F32)) * gk_ref[...]
            if sample:
                kb_ref[0:past, sl] = ck_ref[:, sl].astype(BF16)
                vb_ref[g, 0:past, 0:HD_B] = cv_ref[:, sl].astype(BF16)
                kn = kn * cos_ref[...] + pltpu.roll(kn, HD_B // 2, 1) * sin_ref[...]
            else:
                kn_ref[:, sl] = kn
            kb_ref[past:past + n_new, sl] = kn.astype(BF16)
            vb_ref[g, past:past + n_new, 0:HD_B] = v_ref[:, sl].astype(BF16)
            vb_ref[g, :, HD_B:] = jnp.ones((past + n_new, HD_B), BF16)

    r0 = pl.multiple_of(qi * QB, QB)
    qs = []
    for h in range(H_B):
        qn = _rms(q_ref[:, HD_B * h:HD_B * (h + 1)].astype(F32)) * gq_ref[...]
        if sample:
            qn = qn * cos_ref[pl.ds(r0, QB), :] + pltpu.roll(qn, HD_B // 2, 1) * sin_ref[pl.ds(r0, QB), :]
        qs.append((qn * (HD_B ** -0.5)).astype(BF16))
    scores = [_nt(qs[h], kb_ref[:, HD_B * (h // rep):HD_B * (h // rep + 1)]) for h in range(H_B)]
    weights = [jnp.exp(s - jnp.max(s, axis=-1, keepdims=True)).astype(BF16) for s in scores]
    sums = [_mm(weights[h], vb_ref[h // rep]) for h in range(H_B)]
    for h in range(H_B):
        o_ref[:, HD_B * h:HD_B * (h + 1)] = (sums[h][:, :HD_B] / sums[h][:, HD_B:]).astype(o_ref.dtype)


def _gqa(proj, g_q, g_k, n_batch, seq, ctx=None, rope=None):
    sample = ctx is not None
    m = n_batch * seq
    nq = seq // QB
    in_specs = [
        pl.BlockSpec((QB, 512), lambda b, i: (b * nq + i, 3)),
        pl.BlockSpec((seq, 256), lambda b, i: (b, 8)),
        pl.BlockSpec((seq, 256), lambda b, i: (b, 9)),
        pl.BlockSpec((1, HD_B), lambda b, i: (0, 0)),
        pl.BlockSpec((1, HD_B), lambda b, i: (0, 0)),
    ]
    args = [proj, proj, proj, g_q.reshape(1, HD_B), g_k.reshape(1, HD_B)]
    o_spec = pl.BlockSpec((QB, 512), lambda b, i: (b * nq + i, 0))
    o_shape = jax.ShapeDtypeStruct((m, 512), BF16)
    if sample:
        in_specs += [
            pl.BlockSpec((None, PAST_LEN, 256), lambda b, i: (b, 0, 0)),
            pl.BlockSpec((None, PAST_LEN, 256), lambda b, i: (b, 0, 0)),
            pl.BlockSpec((seq, HD_B), lambda b, i: (0, 0)),
            pl.BlockSpec((seq, HD_B), lambda b, i: (0, 0)),
        ]
        args += [ctx[0], ctx[1], rope[0], rope[1]]
        out_specs, out_shape = o_spec, o_shape
    else:
        out_specs = [o_spec, pl.BlockSpec((seq, 256), lambda b, i: (b, 0))]
        out_shape = [o_shape, jax.ShapeDtypeStruct((m, 256), F32)]
    n_keys = seq + (PAST_LEN if sample else 0)
    return pl.pallas_call(
        functools.partial(_gqa_kernel, sample=sample),
        grid=(n_batch, nq),
        in_specs=in_specs,
        out_specs=out_specs,
        out_shape=out_shape,
        scratch_shapes=[pltpu.VMEM((n_keys, KV_B * HD_B), BF16), pltpu.VMEM((KV_B, n_keys, 2 * HD_B), BF16)],
        compiler_params=_params(2),
        name="gqa_sample" if sample else "gqa_prompt",
    )(*args)


def _mla_kernel(*refs, sample):
    if sample:
        (cq_ref, ckv_ref, kpe_ref, gq_ref, wqb_ref, gkv_ref, wkvb_ref, ex_ref, cckv_ref, ckpe_ref, c4_ref, s4_ref,
         o_ref, kv_s, kx_s) = refs
    else:
        cq_ref, ckv_ref, kpe_ref, gq_ref, wqb_ref, gkv_ref, wkvb_ref, ex_ref, o_ref, ckvn_ref, kv_s, kx_s = refs
    qi = pl.program_id(1)
    n_new = ckv_ref.shape[0]
    past = PAST_LEN if sample else 0

    def rope(x, c, s):
        x1, x2 = x[:, :LANES], x[:, LANES:]
        return jnp.concatenate([x1 * c - x2 * s, x1 * s + x2 * c], axis=1)

    hw = NOPE_D + 2 * V_D

    def stage_kv(rows, kv):
        for h in range(H_D):
            kv_s[rows, hw * h:hw * h + NOPE_D + V_D] = kv[:, 256 * h:256 * (h + 1)].astype(BF16)

    @pl.when(qi == 0)
    def _():
        wkvb = wkvb_ref[...].astype(BF16)
        ckvn = _rms(ckv_ref[...].astype(F32)) * gkv_ref[...]
        if not sample:
            ckvn_ref[...] = ckvn
        stage_kv(slice(past, past + n_new), _mm(ckvn.astype(BF16), wkvb))
        for h in range(H_D):
            kv_s[:, hw * h + NOPE_D + V_D:hw * (h + 1)] = jnp.ones((past + n_new, V_D), BF16)
        kx = _mm(kpe_ref[:, 0:ROPE_D], ex_ref[...])
        if sample:
            stage_kv(slice(0, past), _mm(cckv_ref[...].astype(BF16), wkvb))
            kx_s[0:past, :] = _mm(ckpe_ref[...].astype(BF16), ex_ref[...]).astype(BF16)
            kx = rope(kx, c4_ref[...], s4_ref[...])
        kx_s[past:past + n_new, :] = kx.astype(BF16)

    q = _mm((_rms(cq_ref[...].astype(F32)) * gq_ref[...]).astype(BF16), wqb_ref[...].astype(BF16))
    q = q * ((NOPE_D + ROPE_D) ** -0.5)
    qpe = q[:, 4 * NOPE_D:]
    if sample:
        r0 = pl.multiple_of(qi * QB, QB)
        qpe = rope(qpe, c4_ref[pl.ds(r0, QB), :], s4_ref[pl.ds(r0, QB), :])
    lane_head = (lax.broadcasted_iota(jnp.int32, (1, 2 * LANES), 1) % LANES) // (ROPE_D // 2)
    qb, qpb = q.astype(BF16), qpe.astype(BF16)
    scores = [_nt(qb[:, NOPE_D * h:NOPE_D * (h + 1)], kv_s[:, hw * h:hw * h + NOPE_D])
              + _nt(jnp.where(lane_head == h, qpb, jnp.zeros_like(qpb)), kx_s[...]) for h in range(H_D)]
    weights = [jnp.exp(s - jnp.max(s, axis=-1, keepdims=True)).astype(BF16) for s in scores]
    sums = [_mm(weights[h], kv_s[:, hw * h + NOPE_D:hw * (h + 1)]) for h in range(H_D)]
    for h in range(H_D):
        o_ref[:, V_D * h:V_D * (h + 1)] = (sums[h][:, :V_D] / sums[h][:, V_D:]).astype(o_ref.dtype)


def _mla(proj, g_q, w_qb, g_kv, w_kvb, n_batch, seq, ctx=None, rope=None, expand=None):
    sample = ctx is not None
    m = n_batch * seq
    nq = seq // QB
    in_specs = [
        pl.BlockSpec((QB, Q_RANK), lambda b, i: (b * nq + i, 6)),
        pl.BlockSpec((seq, KV_RANK), lambda b, i: (b, 14)),
        pl.BlockSpec((seq, LANES), lambda b, i: (b, 15)),
        pl.BlockSpec((1, Q_RANK), lambda b, i: (0, 0)),
        pl.BlockSpec((Q_RANK, 768), lambda b, i: (0, 0)),
        pl.BlockSpec((1, KV_RANK), lambda b, i: (0, 0)),
        pl.BlockSpec((KV_RANK, 1024), lambda b, i: (0, 0)),
        pl.BlockSpec((ROPE_D, 256), lambda b, i: (0, 0)),
    ]
    args = [proj, proj, proj, g_q.reshape(1, Q_RANK), w_qb, g_kv.reshape(1, KV_RANK), w_kvb, expand]
    o_spec = pl.BlockSpec((QB, 512), lambda b, i: (b * nq + i, 0))
    o_shape = jax.ShapeDtypeStruct((m, 512), BF16)
    if sample:
        in_specs += [
            pl.BlockSpec((None, PAST_LEN, KV_RANK), lambda b, i: (b, 0, 0)),
            pl.BlockSpec((None, PAST_LEN, ROPE_D), lambda b, i: (b, 0, 0)),
            pl.BlockSpec((seq, LANES), lambda b, i: (0, 0)),
            pl.BlockSpec((seq, LANES), lambda b, i: (0, 0)),
        ]
        args += [ctx[0], ctx[1], rope[0], rope[1]]
        out_specs, out_shape = o_spec, o_shape
    else:
        out_specs = [o_spec, pl.BlockSpec((seq, KV_RANK), lambda b, i: (b, 0))]
        out_shape = [o_shape, jax.ShapeDtypeStruct((m, KV_RANK), F32)]
    n_keys = seq + (PAST_LEN if sample else 0)
    return pl.pallas_call(
        functools.partial(_mla_kernel, sample=sample),
        grid=(n_batch, nq),
        in_specs=in_specs,
        out_specs=out_specs,
        out_shape=out_shape,
        scratch_shapes=[pltpu.VMEM((n_keys, H_D * (NOPE_D + 2 * V_D)), BF16), pltpu.VMEM((n_keys, 256), BF16)],
        compiler_params=_params(2),
        name="mla_sample" if sample else "mla_prompt",
    )(*args)


def _dft(table, x):
    return _mm(table.astype(BF16), x.astype(BF16))


def _filter_kernel(z_ref, wf1_ref, bf1_ref, fr_ref, wf2_ref, bf2_ref, wf3_ref, t_ref, dl_ref,
                   c_ref, s_ref, gre_ref, gim_ref):
    n_tok = z_ref.shape[0]
    fr = fr_ref[...]
    hid = jnp.sin(fr * (_mm(z_ref[...].astype(BF16), wf1_ref[...].astype(BF16)) + bf1_ref[...]))
    hid = jnp.sin(fr * (_mm(hid.astype(BF16), wf2_ref[...].astype(BF16)) + bf2_ref[...]))
    filt = _mm(hid.astype(BF16), wf3_ref[...].astype(BF16))
    decay = jnp.exp(-t_ref[...] * dl_ref[...])
    row = lax.broadcasted_iota(jnp.int32, (n_tok, 1), 0)
    h_f = filt[:, :HY_W] * decay
    h_b = jnp.where(row == 0, 0.0, filt[:, HY_W:] * decay)
    p, m = h_f + h_b, h_f - h_b
    g_re = _dft(c_ref[...], p)
    g_im = _dft(s_ref[...], m)
    sign = jnp.where(row % 2 == 0, 1.0, -1.0)
    nyquist = jnp.sum(p * sign, axis=0, keepdims=True)
    g_im = jnp.where(row == 0, nyquist, g_im)
    wk = jnp.where(row == 0, 0.5 / n_tok, 1.0 / n_tok)
    gre_ref[...] = g_re * wk
    gim_ref[...] = g_im * wk


def _filter_spectrum(z, wf1, bf1, freq, wf2, bf2, wf3, t_col, deltas, tabs):
    n_tok = z.shape[0]
    out = jax.ShapeDtypeStruct((n_tok, HY_W), F32)
    return pl.pallas_call(
        _filter_kernel,
        out_shape=[out, out],
        compiler_params=pltpu.CompilerParams(vmem_limit_bytes=VMEM_LIMIT),
        name="hyena_filter",
    )(z, wf1, bf1.reshape(1, FILT_HID), freq.reshape(1, FILT_HID), wf2, bf2.reshape(1, FILT_HID), wf3,
      t_col, deltas, tabs[0], tabs[1])


HY_CT = 256


HY_ROWS = 1024


def _hyena_kernel(u0_ref, u1_ref, u2_ref, w0_ref, w1_ref, w2_ref, b0_ref, b1_ref, b2_ref, skip_ref,
                  gre_ref, gim_ref, cf_ref, sf_ref, stf_ref, o_ref, c_ref, s_ref, st_ref):
    seq = c_ref.shape[0]
    n_rows = u0_ref.shape[0]
    n_seq = n_rows // seq
    pos = lax.broadcasted_iota(jnp.int32, (n_rows, 1), 0) % seq

    @pl.when((pl.program_id(0) == 0) & (pl.program_id(1) == 0))
    def _():
        c_ref[...] = cf_ref[...].astype(BF16)
        s_ref[...] = sf_ref[...].astype(BF16)
        st_ref[...] = stf_ref[...].astype(BF16)

    def short_conv(u_ref, w_ref, b_ref):
        x, w = u_ref[...].astype(F32), w_ref[...]
        prev = jnp.where(pos == 0, 0.0, pltpu.roll(x, 1, 0))
        nxt = jnp.where(pos == seq - 1, 0.0, pltpu.roll(x, n_rows - 1, 0))
        return prev * w[0:1] + x * w[1:2] + nxt * w[2:3] + b_ref[...]

    def side_by_side(a):
        return a if n_seq == 1 else jnp.concatenate([a[s * seq:(s + 1) * seq] for s in range(n_seq)], axis=1)

    def stacked(a):
        ct = a.shape[1] // n_seq
        return a if n_seq == 1 else jnp.concatenate([a[:, s * ct:(s + 1) * ct] for s in range(n_seq)], axis=0)

    x0 = short_conv(u0_ref, w0_ref, b0_ref)
    gv = short_conv(u1_ref, w1_ref, b1_ref) * short_conv(u2_ref, w2_ref, b2_ref)
    sig = side_by_side(gv).astype(BF16)
    u_re = _mm(c_ref[...], sig)
    u_im = _mm(s_ref[...], sig)
    g_re = jnp.concatenate([gre_ref[...]] * n_seq, axis=1)
    g_im = jnp.concatenate([gim_ref[...]] * n_seq, axis=1)
    bin0 = lax.broadcasted_iota(jnp.int32, (seq, 1), 0) == 0
    p_im = u_im * g_im
    y_re = u_re * g_re - jnp.where(bin0, 0.0, p_im)
    y_im = jnp.where(bin0, p_im, u_re * g_im + u_im * g_re)
    y = stacked(_mm(c_ref[...], y_re.astype(BF16)) + _mm(st_ref[...], y_im.astype(BF16)))
    o_ref[...] = (x0 * (y + gv * skip_ref[...])).astype(o_ref.dtype)


def _hyena(proj, w_conv, b_conv, skip, g_re, g_im, tabs, n_batch, seq):
    nct = HY_W // HY_CT
    u_specs = [pl.BlockSpec((HY_ROWS, HY_CT), functools.partial(lambda b, c, g: (b, g * nct + c), g=g)) for g in range(3)]
    w_specs = [pl.BlockSpec((3, HY_CT), functools.partial(lambda b, c, g: (0, g * nct + c), g=g)) for g in range(3)]
    b_specs = [pl.BlockSpec((1, HY_CT), functools.partial(lambda b, c, g: (0, g * nct + c), g=g)) for g in range(3)]
    tab_spec = pl.BlockSpec((seq, seq), lambda b, c: (0, 0))
    return pl.pallas_call(
        _hyena_kernel,
        grid=(n_batch * seq // HY_ROWS, nct),
        in_specs=u_specs + w_specs + b_specs + [
            pl.BlockSpec((1, HY_CT), lambda b, c: (0, c)),
            pl.BlockSpec((seq, HY_CT), lambda b, c: (0, c)),
            pl.BlockSpec((seq, HY_CT), lambda b, c: (0, c)),
        ] + [tab_spec] * 3,
        out_specs=pl.BlockSpec((HY_ROWS, HY_CT), lambda b, c: (b, c)),
        out_shape=jax.ShapeDtypeStruct((n_batch * seq, HY_W), BF16),
        scratch_shapes=[pltpu.VMEM((seq, seq), BF16)] * 3,
        compiler_params=_params(2),
        name="hyena_conv",
    )(proj, proj, proj, w_conv, w_conv, w_conv, b_conv, b_conv, b_conv, skip.reshape(1, HY_W), g_re, g_im, *tabs)


def _dft_tables(n_tok):
    k = np.arange(n_tok)[:, None]
    s = np.arange(n_tok)[None, :]
    ang = ((k * s) % (2 * n_tok)) * (np.pi / n_tok)
    cos_t = np.cos(ang)
    sin_f = np.where(k == 0, np.where(s % 2 == 0, 1.0, -1.0), -np.sin(ang))
    return [jnp.asarray(t, F32) for t in (cos_t, sin_f, sin_f.T)]


GLA_LEVELS = (32, 16, 8, 4, 2, 1)
GLA_SAFE_DECAY = 60.0
GLA_GROUP = 2


def _gla_constants():
    c = GLA_CHUNK
    idx = np.arange(c)
    i, t = idx[:, None], idx[None, :]
    masks = []
    for s in GLA_LEVELS:
        upper = (idx % (2 * s)) >= s
        masks.append(((i // (2 * s)) == (t // (2 * s))) & upper[:, None] & (~upper)[None, :])
    masks.append(i == t)
    tri = t <= i
    fwd_m = np.stack([np.tile(m, (H_A, 1)) for m in masks]).astype(np.float32)
    bwd_m = np.stack([np.tile(m[::-1, ::-1], (H_A, 1)) for m in masks]).astype(np.float32)
    head_of_row = np.repeat(np.arange(H_A), c)[:, None]
    head_of_lane = np.repeat(np.arange(H_A), DK_A)[None, :]
    head_mask = head_of_row == head_of_lane
    return (jnp.asarray(tri, BF16), jnp.asarray(tri[::-1, ::-1], BF16), jnp.asarray(fwd_m), jnp.asarray(bwd_m),
            jnp.asarray(head_mask, BF16))


def _pair_reference(b, s, backward, row):
    c = GLA_CHUNK
    ref = s if backward else s - 1
    if 2 * s >= 8:
        pieces = [jnp.broadcast_to(b[p * 2 * s + ref:p * 2 * s + ref + 1, :], (2 * s, b.shape[1]))
                  for p in range(c // (2 * s))]
        return pieces[0] if len(pieces) == 1 else jnp.concatenate(pieces, axis=0)
    pos = row % (2 * s)
    out = None
    for o in range(2 * s):
        d = ref - o
        shifted = b if d == 0 else pltpu.roll(b, (-d) % c, 0)
        out = shifted if out is None else jnp.where(pos == o, shifted, out)
    return out


def _chunk_log_decay(la, t_ref):
    l1 = la.astype(BF16)
    r1 = la - l1.astype(F32)
    l2 = r1.astype(BF16)
    l3 = (r1 - l2.astype(F32)).astype(BF16)
    tmat = t_ref[...]
    return _mm(tmat, l1) + _mm(tmat, l2) + _mm(tmat, l3)


def _stack_heads(a, hm):
    ab = a.astype(BF16)
    return jnp.concatenate([ab] * H_A, axis=0) * hm


def _state_terms(k, v, b, b_last):
    c = GLA_CHUNK
    k_rest = (k * jnp.exp(b_last - b)).T
    carry = jnp.broadcast_to(jnp.exp(b_last), (2 * c, b.shape[1])).T
    return k_rest.astype(BF16), carry


def _gla_chunk(q, k, v, la, t_ref, m_ref, hm, s_ref, backward):
    c = GLA_CHUNK
    b = _chunk_log_decay(la, t_ref)
    row = lax.broadcasted_iota(jnp.int32, (c, 1), 0)
    last = 0 if backward else c - 1
    b_last = b[last:last + 1, :]
    scores = _nt(_stack_heads(q, hm), k.astype(BF16)) * m_ref[len(GLA_LEVELS)]
    for lvl, s in enumerate(GLA_LEVELS):
        is_query = (row % (2 * s) < s) if backward else (row % (2 * s) >= s)
        delta = b - _pair_reference(b, s, backward, row)
        x = jnp.exp(jnp.where(is_query, delta, -delta))
        scores = scores + _nt(_stack_heads(q * x, hm), (k * x).astype(BF16)) * m_ref[lvl]
    scores = scores.astype(BF16)
    state = s_ref[...]
    inter = _mm(_stack_heads(q * jnp.exp(b), hm), state.astype(BF16))
    k_rest, carry = _state_terms(k, v, b, b_last)
    outs = []
    for h in range(H_A):
        rows = slice(c * h, c * (h + 1))
        v_h = v[:, DV_A * h:DV_A * (h + 1)]
        outs.append(_mm(scores[rows], v_h) + inter[rows])
        s_ref[rows, :] = state[rows] * carry[rows] + _mm(k_rest[rows], v_h)
    return jnp.concatenate(outs, axis=1)


def _gla_local(items, hm):
    c = GLA_CHUNK
    bs = [_chunk_log_decay(la, t_ref) for _, _, _, la, t_ref, _, _ in items]
    b_lasts = [b[(0 if it[6] else c - 1):(0 if it[6] else c - 1) + 1, :] for b, it in zip(bs, items)]
    q_decayed = [_stack_heads(it[0] * jnp.exp(b), hm) for it, b in zip(items, bs)]
    k_grown = [(it[1] * jnp.exp(-b)).astype(BF16) for it, b in zip(items, bs)]
    raw = [_nt(qd, kg) for qd, kg in zip(q_decayed, k_grown)]
    scores = [(r * it[5]).astype(BF16) for r, it in zip(raw, items)]
    terms = [_state_terms(it[1], it[2], b, bl) for it, b, bl in zip(items, bs, b_lasts)]
    out = []
    for it, sc, (k_rest, carry), qd in zip(items, scores, terms, q_decayed):
        v = it[2]
        heads = [(slice(c * h, c * (h + 1)), v[:, DV_A * h:DV_A * (h + 1)]) for h in range(H_A)]
        intra = jnp.concatenate([_mm(sc[rows], v_h) for rows, v_h in heads], axis=1)
        incr = jnp.concatenate([_mm(k_rest[rows], v_h) for rows, v_h in heads], axis=0)
        out.append((intra, qd, incr, carry))
    return out


def _gla_kernel(*refs, sample):
    if sample:
        (x_ref, z_ref, wf_ref, bf_ref, wb_ref, bb_ref, tf_ref, tb_ref, mf_ref, mb_ref, hm_ref, gn_ref, sf0_ref, sb0_ref,
         o_ref, la_f, la_b, o_f, o_b, s_f, s_b, qd_f, qd_b, ds_f, ds_b, cr_f, cr_b) = refs
    else:
        (x_ref, z_ref, wf_ref, bf_ref, wb_ref, bb_ref, tf_ref, tb_ref, mf_ref, mb_ref, hm_ref, gn_ref,
         o_ref, sf_out, sb_out, la_f, la_b, o_f, o_b, s_f, s_b, qd_f, qd_b, ds_f, ds_b, cr_f, cr_b) = refs
    n_tok = x_ref.shape[0]
    n_chunks = n_tok // GLA_CHUNK
    hk, hv = H_A * DK_A, H_A * DV_A
    zb = z_ref[...].astype(BF16)

    def log_sigmoid(t):
        return jnp.minimum(t, 0.0) - jnp.log(1.0 + jnp.exp(-jnp.abs(t)))

    la_f[...] = log_sigmoid(_mm(zb, wf_ref[...].astype(BF16)) + bf_ref[...]) / GLA_TAU
    la_b[...] = log_sigmoid(_mm(zb, wb_ref[...].astype(BF16)) + bb_ref[...]) / GLA_TAU
    if sample:
        s_f[...] = sf0_ref[...]
        s_b[...] = sb0_ref[...]
    else:
        s_f[...] = jnp.zeros_like(s_f)
        s_b[...] = jnp.zeros_like(s_b)
    hm = hm_ref[...]

    fwd = (la_f, tf_ref, mf_ref, s_f, o_f, qd_f, ds_f, cr_f, False)
    bwd = (la_b, tb_ref, mb_ref, s_b, o_b, qd_b, ds_b, cr_b, True)
    tri_f = jnp.sum(mf_ref[...], axis=0)
    tri_b = jnp.sum(mb_ref[...], axis=0)

    def chunk_rows(ci, backward):
        cidx = n_chunks - 1 - ci if backward else ci
        return cidx, pl.ds(pl.multiple_of(cidx * GLA_CHUNK, GLA_CHUNK), GLA_CHUNK)

    def load_qkv(rows):
        q = x_ref[rows, 0:hk].astype(F32) * (DK_A ** -0.5)
        return q, x_ref[rows, hk:2 * hk].astype(F32), x_ref[rows, 2 * hk:2 * hk + hv]

    def safe_step(ci, carry):
        for la_ref, t_ref, m_ref, s_ref, out_ref, _, _, _, backward in (fwd, bwd):
            _, rows = chunk_rows(ci, backward)
            out_ref[rows, :] = _gla_chunk(*load_qkv(rows), la_ref[rows, :], t_ref, m_ref, hm, s_ref, backward)
        return carry

    def local_step(gi, carry):
        items, dests = [], []
        for (la_ref, t_ref, _, _, out_ref, qd_ref, ds_ref, cr_ref, backward), tri in ((fwd, tri_f), (bwd, tri_b)):
            for u in range(GLA_GROUP):
                cidx, rows = chunk_rows(gi * GLA_GROUP + u, backward)
                items.append((*load_qkv(rows), la_ref[rows, :], t_ref, tri, backward))
                dests.append((out_ref, rows, qd_ref, ds_ref, cr_ref, cidx))
        for (out_ref, rows, qd_ref, ds_ref, cr_ref, cidx), (intra, qd, incr, factor) in zip(dests, _gla_local(items, hm)):
            out_ref[rows, :] = intra
            qd_ref[cidx] = qd
            ds_ref[cidx] = incr
            cr_ref[cidx] = factor
        return carry

    def scan_step(ci, carry):
        for _, _, _, s_ref, out_ref, qd_ref, ds_ref, cr_ref, backward in (fwd, bwd):
            cidx, rows = chunk_rows(ci, backward)
            state = s_ref[...]
            inter = _mm(qd_ref[cidx], state.astype(BF16))
            out_ref[rows, :] += jnp.concatenate(
                [inter[GLA_CHUNK * h:GLA_CHUNK * (h + 1)] for h in range(H_A)], axis=1)
            s_ref[...] = state * cr_ref[cidx] + ds_ref[cidx]
        return carry

    chunk_sums = [jnp.sum(ref[...].reshape(n_chunks, GLA_CHUNK, hk), axis=1) for ref in (la_f, la_b)]
    mild = jnp.minimum(jnp.min(chunk_sums[0]), jnp.min(chunk_sums[1])) > -GLA_SAFE_DECAY

    @pl.when(mild)
    def _():
        lax.fori_loop(0, n_chunks // GLA_GROUP, local_step, 0, unroll=2)
        lax.fori_loop(0, n_chunks, scan_step, 0, unroll=2)

    @pl.when(jnp.logical_not(mild))
    def _():
        lax.fori_loop(0, n_chunks, safe_step, 0)
    if not sample:
        sf_out[...] = s_f[...]
        sb_out[...] = s_b[...]
    gain = gn_ref[...]
    for h in range(H_A):
        cols = slice(DV_A * h, DV_A * (h + 1))
        r = x_ref[:, 2 * hk + hv + DV_A * h:2 * hk + hv + DV_A * (h + 1)].astype(F32)
        o_ref[:, cols] = (_rms(o_f[:, cols] + o_b[:, cols]) * gain * (r * jax.nn.sigmoid(r))).astype(o_ref.dtype)


def _gla(proj, w_gf, b_gf, w_gb, b_gb, g_norm, consts, n_batch, seq, ctx=None):
    sample = ctx is not None
    hk, hv = H_A * DK_A, H_A * DV_A
    n_ch = seq // GLA_CHUNK
    full = lambda shape: pl.BlockSpec(shape, lambda b: (0,) * len(shape))
    in_specs = [
        pl.BlockSpec((seq, 2 * hk + 2 * hv), lambda b: (b, 0)),
        pl.BlockSpec((seq, LANES), lambda b: (b, EVEN_W // LANES - 1)),
        full((LANES, hk)), full((1, hk)), full((LANES, hk)), full((1, hk)),
        full(consts[0].shape), full(consts[1].shape), full(consts[2].shape), full(consts[3].shape), full(consts[4].shape),
        full((1, DV_A)),
    ]
    args = [proj, proj, w_gf, b_gf.reshape(1, hk), w_gb, b_gb.reshape(1, hk), *consts, g_norm.reshape(1, DV_A)]
    o_spec = pl.BlockSpec((seq, hv), lambda b: (b, 0))
    o_shape = jax.ShapeDtypeStruct((n_batch * seq, hv), BF16)
    st_spec = pl.BlockSpec((None, hk, DV_A), lambda b: (b, 0, 0))
    if sample:
        in_specs += [st_spec, st_spec]
        args += [ctx[0], ctx[1]]
        out_specs, out_shape = o_spec, o_shape
    else:
        st_shape = jax.ShapeDtypeStruct((n_batch, hk, DV_A), F32)
        out_specs, out_shape = [o_spec, st_spec, st_spec], [o_shape, st_shape, st_shape]
    return pl.pallas_call(
        functools.partial(_gla_kernel, sample=sample),
        grid=(n_batch,),
        in_specs=in_specs,
        out_specs=out_specs,
        out_shape=out_shape,
        scratch_shapes=[pltpu.VMEM((seq, hk), F32), pltpu.VMEM((seq, hk), F32),
                        pltpu.VMEM((seq, hv), F32), pltpu.VMEM((seq, hv), F32),
                        pltpu.VMEM((hk, DV_A), F32), pltpu.VMEM((hk, DV_A), F32),
                        pltpu.VMEM((n_ch, H_A * GLA_CHUNK, hk), BF16), pltpu.VMEM((n_ch, H_A * GLA_CHUNK, hk), BF16),
                        pltpu.VMEM((n_ch, hk, DV_A), F32), pltpu.VMEM((n_ch, hk, DV_A), F32),
                        pltpu.VMEM((n_ch, hk, DV_A), F32), pltpu.VMEM((n_ch, hk, DV_A), F32)],
        compiler_params=_params(1),
        name="gla_sample" if sample else "gla_prompt",
    )(*args)


def _axial_rope(n_tokens, dim):
    rows = n_tokens // GRID_W
    row = np.repeat(np.arange(rows), GRID_W).astype(np.float64)
    col = np.tile(np.arange(GRID_W), rows).astype(np.float64)
    n_freq = dim // 4
    inv = ROPE_THETA ** (-np.arange(n_freq) / n_freq)
    ang = np.concatenate([row[:, None] * inv, col[:, None] * inv], axis=-1)
    return np.cos(ang).astype(np.float32), np.sin(ang).astype(np.float32)


def _filter_features(n_tokens):
    t = np.linspace(0.0, 1.0, n_tokens)[:, None]
    w = 2.0 * np.pi * np.arange(n_tokens)[:, None] / n_tokens
    f = np.linspace(1e-4, FILT_BANDS - 1, FILT_BANDS)[None, :]
    z = np.concatenate([t, np.cos(f * w), -np.sin(f * w)], axis=-1)
    z = np.pad(z, ((0, 0), (0, LANES - FILT_EMB)))
    return jnp.asarray(z, F32), jnp.asarray(t, F32)


_KPE_EXPAND = np.array([(p // LANES) * (ROPE_D // 2) + p % (ROPE_D // 2) for p in range(2 * LANES)])
_QB_PERM = np.array(
    [192 * (p // NOPE_D) + p % NOPE_D for p in range(H_D * NOPE_D)]
    + [192 * (p // 32) + NOPE_D + p % 32 for p in range(H_D * 32)]
    + [192 * (p // 32) + NOPE_D + 32 + p % 32 for p in range(H_D * 32)])

EVEN_ROW_GROUPS = ((0, 0, 1536), (1568, 1536, 1024), (1536, EVEN_W - 2 * GATE_RANK, 2 * GATE_RANK))
ODD_ROW_GROUPS = ((0, 0, 1984),)
EVEN_KEEP = (2304, 256)
ODD_KEEP = (1920, LANES)


def kernel(x_prompt, x_sample, state_gla_fwd, state_gla_bwd, cache_gqa_k, cache_gqa_v, cache_mla_ckv, cache_mla_kpe, c, c_ctx, w_mod, b_mod, w_in_even, w_gla_gate_f, b_gla_gate_f, w_gla_gate_b, b_gla_gate_b, g_gla_norm, g_gqa_q, g_gqa_k, w_out_even, w_in_odd, w_hy_conv, b_hy_conv, hy_skip, w_filt1, b_filt1, filt_freq, w_filt2, b_filt2, w_filt3, g_mla_q, w_mla_qb, g_mla_kv, w_mla_kvb, w_out_odd, w_ffn_in, w_ffn_out, g_final):
    n_c, n_s = BATCH * SEQ, DEC_BATCH * DEC_SEQ
    cvec = jnp.concatenate([c_ctx[None, :], c, jnp.zeros((8 - 1 - DEC_BATCH, D_MODEL), F32)], axis=0)
    mod = _modulation(cvec, w_mod, b_mod)
    xc = x_prompt.reshape(n_c, D_MODEL)
    xs = x_sample.reshape(n_s, D_MODEL)
    rows_c, rows_s = (0, 0), (1, DEC_SEQ // MOD_ROWS)

    gla_consts = _gla_constants()
    cos_b, sin_b = _axial_rope(DEC_SEQ, HD_B)
    rope_b = (jnp.asarray(np.concatenate([cos_b, cos_b], axis=1)), jnp.asarray(np.concatenate([-sin_b, sin_b], axis=1)))
    cos_d, sin_d = _axial_rope(DEC_SEQ, ROPE_D)
    rope_d = (jnp.asarray(np.tile(cos_d, (1, H_D))), jnp.asarray(np.tile(sin_d, (1, H_D))))
    kpe_expand = jnp.asarray(np.arange(ROPE_D)[:, None] == _KPE_EXPAND[None, :], BF16)
    tabs_c, tabs_s = _dft_tables(SEQ), _dft_tables(DEC_SEQ)
    z_c, t_c = _filter_features(SEQ)
    z_s, t_s = _filter_features(DEC_SEQ)
    deltas = jnp.asarray(np.abs(np.linspace(HY_MIN_DECAY, HY_MAX_DECAY, HY_W))[None, :], F32)

    wt_even = jnp.swapaxes(w_in_even, 1, 2)
    wt_odd = jnp.swapaxes(w_in_odd, 1, 2)

    st_gf, st_gb, st_k, st_v, st_ckv, st_kpe = [], [], [], [], [], []
    for i in range(DEPTH):
        j = i // 2
        if i % 2 == 0:
            z0 = LANES - 2 * GATE_RANK
            pad_f = jnp.zeros((LANES, H_A * DK_A), F32).at[z0:z0 + GATE_RANK].set(w_gla_gate_f[j])
            pad_b = jnp.zeros((LANES, H_A * DK_A), F32).at[z0 + GATE_RANK:LANES].set(w_gla_gate_b[j])
            pc, v_new, ps = _in_proj(xc, xs, mod, i, wt_even, j, EVEN_ROW_GROUPS, EVEN_W, EVEN_KEEP)
            gate_args = (pad_f, b_gla_gate_f[j], pad_b, b_gla_gate_b[j], g_gla_norm[j], gla_consts)
            a_c, s_f, s_b = _gla(pc, *gate_args, BATCH, SEQ)
            ctx_a = (state_gla_fwd[:, j].reshape(DEC_BATCH, H_A * DK_A, DV_A),
                     state_gla_bwd[:, j].reshape(DEC_BATCH, H_A * DK_A, DV_A))
            a_s = _gla(ps, *gate_args, DEC_BATCH, DEC_SEQ, ctx=ctx_a)
            b_c, k_norm = _gqa(pc, g_gqa_q[j], g_gqa_k[j], BATCH, SEQ)
            ctx_b = (cache_gqa_k[:, j].reshape(DEC_BATCH, PAST_LEN, KV_B * HD_B),
                     cache_gqa_v[:, j].reshape(DEC_BATCH, PAST_LEN, KV_B * HD_B))
            b_s = _gqa(ps, g_gqa_q[j], g_gqa_k[j], DEC_BATCH, DEC_SEQ, ctx=ctx_b, rope=rope_b)
            w_out = w_out_even
            st_gf.append(s_f.reshape(BATCH, H_A, DK_A, DV_A))
            st_gb.append(s_b.reshape(BATCH, H_A, DK_A, DV_A))
            st_k.append(k_norm.reshape(BATCH, SEQ, KV_B, HD_B))
            st_v.append(v_new.reshape(BATCH, SEQ, KV_B, HD_B))
        else:
            pc, kpe_new, ps = _in_proj(xc, xs, mod, i, wt_odd, j, ODD_ROW_GROUPS, ODD_W, ODD_KEEP)
            wf1 = jnp.pad(w_filt1[j], ((0, LANES - FILT_EMB), (0, 0)))
            filt_args = (wf1, b_filt1[j], filt_freq[j], w_filt2[j], b_filt2[j], w_filt3[j])
            g_c = _filter_spectrum(z_c, *filt_args, t_c, deltas, tabs_c)
            g_s = _filter_spectrum(z_s, *filt_args, t_s, deltas, tabs_s)
            b_conv = b_hy_conv[j].reshape(1, 3 * HY_W)
            a_c = _hyena(pc, w_hy_conv[j], b_conv, hy_skip[j], g_c[0], g_c[1], tabs_c, BATCH, SEQ)
            a_s = _hyena(ps, w_hy_conv[j], b_conv, hy_skip[j], g_s[0], g_s[1], tabs_s, DEC_BATCH, DEC_SEQ)
            w_qb = w_mla_qb[j][:, _QB_PERM]
            b_c, ckv_norm = _mla(pc, g_mla_q[j], w_qb, g_mla_kv[j], w_mla_kvb[j], BATCH, SEQ, expand=kpe_expand)
            b_s = _mla(ps, g_mla_q[j], w_qb, g_mla_kv[j], w_mla_kvb[j], DEC_BATCH, DEC_SEQ,
                       ctx=(cache_mla_ckv[:, j], cache_mla_kpe[:, j]), rope=rope_d, expand=kpe_expand)
            w_out = w_out_odd
            st_ckv.append(ckv_norm.reshape(BATCH, SEQ, KV_RANK))
            st_kpe.append(kpe_new[:, :ROPE_D].reshape(BATCH, SEQ, ROPE_D))
        xc, xs = _out_proj([a_c, b_c], [a_s, b_s], w_out, j, xc, xs, mod, i, 2)
        last = g_final if i == DEPTH - 1 else None
        xc = _ffn(xc, mod, i, w_ffn_in, w_ffn_out, *rows_c, final_gain=last)
        xs = _ffn(xs, mod, i, w_ffn_in, w_ffn_out, *rows_s, final_gain=last)
    y_prompt = xc.reshape(BATCH, SEQ, D_MODEL)
    y_sample = xs.reshape(DEC_BATCH, DEC_SEQ, D_MODEL)
    return (y_prompt, y_sample, jnp.stack(st_gf, axis=1), jnp.stack(st_gb, axis=1), jnp.stack(st_k, axis=1),
            jnp.stack(st_v, axis=1), jnp.stack(st_ckv, axis=1), jnp.stack(st_kpe, axis=1))
```

```python
import functools
import math

import numpy as np
import jax
import jax.numpy as jnp
from jax import lax
from jax.experimental import pallas as pl
from jax.experimental.pallas import tpu as pltpu

F32 = jnp.float32
BF16 = jnp.bfloat16

D_MODEL = 1024
BATCH, SEQ = 16, 256
DEC_BATCH, DEC_SEQ = 2, 1024
DEPTH = 4
PAST_LEN = 512
GRID_W = 64
HALF_W = D_MODEL // 2
H_A, DV_A, DK_A = 4, 128, 64
GATE_RANK = 16
GLA_TAU = 16.0
GLA_CHUNK = 64
HD_B, H_B, KV_B = 128, 4, 2
HY_W = HALF_W
FILT_EMB, FILT_HID = 33, 64
FILT_BANDS = (FILT_EMB - 1) // 2
HY_MIN_DECAY = math.log(1e-2) / 1.5
HY_MAX_DECAY = math.log(1e-2) / 0.3
H_D, V_D, NOPE_D, ROPE_D = 4, 128, 128, 64
Q_RANK, KV_RANK = 256, 128
FFN_H = 2816
ROPE_THETA = 10000.0
EPS = 1e-6

LANES = 128
VMEM_LIMIT = 56 * 1024 * 1024

MOD_ROWS = 1024
TM = 1024
TM_IN = 512
TM_FFN = 2048
EVEN_W = 2688
ODD_W = 2048
FFN_TN = 256
QB = 256


def _params(n_grid):
    return pltpu.CompilerParams(dimension_semantics=("arbitrary",) * n_grid, vmem_limit_bytes=VMEM_LIMIT)


def _nt(a, b):
    return lax.dot_general(a, b, (((1,), (1,)), ((), ())), preferred_element_type=F32)


def _mm(a, b):
    return jnp.dot(a, b, preferred_element_type=F32)


def _rms(x):
    return x * lax.rsqrt(jnp.mean(x * x, axis=-1, keepdims=True) + EPS)


def _mod_kernel(c_ref, w_ref, b_ref, o_ref):
    cv = c_ref[...]
    s = cv * jax.nn.sigmoid(cv)
    o_ref[...] = _mm(s.astype(BF16), w_ref[...].astype(BF16)) + b_ref[...]


def _modulation(cvec, w_mod, b_mod):
    return pl.pallas_call(
        _mod_kernel,
        grid=(DEPTH, 6),
        in_specs=[
            pl.BlockSpec((8, D_MODEL), lambda l, n: (0, 0)),
            pl.BlockSpec((None, D_MODEL, D_MODEL), lambda l, n: (l, 0, n)),
            pl.BlockSpec((None, 1, D_MODEL), lambda l, n: (l, 0, n)),
        ],
        out_specs=pl.BlockSpec((None, None, 8, D_MODEL), lambda l, n: (l, n, 0, 0)),
        out_shape=jax.ShapeDtypeStruct((DEPTH, 6, 8, D_MODEL), F32),
        compiler_params=_params(2),
        name="adaln_mod",
    )(cvec, w_mod, b_mod.reshape(DEPTH, 1, 6 * D_MODEL))


def _mod_row(row0, rstep, tile_rows, sub):
    if tile_rows >= MOD_ROWS:
        return row0 + rstep * (pl.program_id(0) * (tile_rows // MOD_ROWS) + sub)
    return row0 + rstep * (pl.program_id(0) // (MOD_ROWS // tile_rows))


def _in_proj_kernel(xc_ref, xs_ref, sh_ref, sc_ref, wt_ref, oc_ref, keep_ref, os_ref, wb_ref, *, row_groups, keep, n_c):
    i = pl.program_id(0)

    @pl.when(i == 0)
    def _():
        wb_ref[...] = jnp.zeros_like(wb_ref)
        for src, dst, size in row_groups:
            wb_ref[dst:dst + size, :] = wt_ref[src:src + size, :].astype(BF16)

    def project(x_ref, g):
        h = (_rms(x_ref[...]) * (1.0 + sc_ref[pl.ds(g, 1), :]) + sh_ref[pl.ds(g, 1), :]).astype(BF16)
        return _nt(h, wb_ref[...])

    @pl.when(i < n_c)
    def _():
        y = project(xc_ref, 0)
        oc_ref[...] = y.astype(oc_ref.dtype)
        keep_ref[...] = y[:, keep[0]:keep[0] + keep[1]]

    @pl.when(i >= n_c)
    def _():
        os_ref[...] = project(xs_ref, 1 + (i - n_c) // (MOD_ROWS // TM_IN)).astype(os_ref.dtype)


def _in_proj(xc, xs, mod, layer, wt, w_layer, row_groups, n, keep):
    n_c, n_s = xc.shape[0] // TM_IN, xs.shape[0] // TM_IN
    c_idx = lambda i: (jnp.minimum(i, n_c - 1), 0)
    s_idx = lambda i: (jnp.maximum(i - n_c, 0), 0)
    return pl.pallas_call(
        functools.partial(_in_proj_kernel, row_groups=row_groups, keep=keep, n_c=n_c),
        grid=(n_c + n_s,),
        in_specs=[pl.BlockSpec((TM_IN, D_MODEL), c_idx), pl.BlockSpec((TM_IN, D_MODEL), s_idx),
                  pl.BlockSpec((None, None, 8, D_MODEL), lambda i: (layer, 0, 0, 0)),
                  pl.BlockSpec((None, None, 8, D_MODEL), lambda i: (layer, 1, 0, 0)),
                  pl.BlockSpec((None, wt.shape[1], D_MODEL), lambda i: (w_layer, 0, 0), pipeline_mode=pl.Buffered(1))],
        out_specs=[pl.BlockSpec((TM_IN, n), c_idx), pl.BlockSpec((TM_IN, keep[1]), c_idx), pl.BlockSpec((TM_IN, n), s_idx)],
        out_shape=[jax.ShapeDtypeStruct((xc.shape[0], n), BF16), jax.ShapeDtypeStruct((xc.shape[0], keep[1]), F32),
                   jax.ShapeDtypeStruct((xs.shape[0], n), BF16)],
        scratch_shapes=[pltpu.VMEM((n, D_MODEL), BF16)],
        compiler_params=_params(1),
        name="norm_mod_proj",
    )(xc, xs, mod, mod, wt)


def _ffn_kernel(x_ref, sh_ref, sc_ref, gate_ref, wg_ref, wu_ref, wd_ref, *refs, row0, rstep, final):
    (gf_ref, o_ref, h_ref) = refs if final else (None,) + refs
    n_sub = x_ref.shape[0] // MOD_ROWS
    subs = [(slice(s * MOD_ROWS, (s + 1) * MOD_ROWS), _mod_row(row0, rstep, x_ref.shape[0], s)) for s in range(n_sub)]

    @pl.when(pl.program_id(1) == 0)
    def _():
        for rows, g in subs:
            x = x_ref[rows, :]
            o_ref[rows, :] = x
            h_ref[rows, :] = (_rms(x) * (1.0 + sc_ref[pl.ds(g, 1), :]) + sh_ref[pl.ds(g, 1), :]).astype(BF16)

    wg = wg_ref[...].astype(BF16)
    wu = wu_ref[...].astype(BF16)
    wd = wd_ref[...].astype(BF16)
    for rows, g in subs:
        h = h_ref[rows, :]
        a = _mm(h, wg)
        act = (a * jax.nn.sigmoid(a) * _mm(h, wu)).astype(BF16)
        o_ref[rows, :] += gate_ref[pl.ds(g, 1), :] * _mm(act, wd)

    if final:
        @pl.when(pl.program_id(1) == pl.num_programs(1) - 1)
        def _():
            for rows, _ in subs:
                o_ref[rows, :] = _rms(o_ref[rows, :]) * gf_ref[...]


def _ffn(x, mod, layer, w_in, w_out, row0, rstep, final_gain=None):
    m = x.shape[0]
    nj = FFN_H // FFN_TN
    mod_spec = lambda k: pl.BlockSpec((None, None, 8, D_MODEL), lambda i, j: (layer, k, 0, 0))
    final = final_gain is not None
    extra_specs = [pl.BlockSpec((1, D_MODEL), lambda i, j: (0, 0))] if final else []
    extra_args = [final_gain.reshape(1, D_MODEL)] if final else []
    return pl.pallas_call(
        functools.partial(_ffn_kernel, row0=row0, rstep=rstep, final=final),
        grid=(m // TM_FFN, nj),
        in_specs=[pl.BlockSpec((TM_FFN, D_MODEL), lambda i, j: (i, 0)), mod_spec(3), mod_spec(4), mod_spec(5),
                  pl.BlockSpec((None, D_MODEL, FFN_TN), lambda i, j: (layer, 0, j)),
                  pl.BlockSpec((None, D_MODEL, FFN_TN), lambda i, j: (layer, 0, j + nj)),
                  pl.BlockSpec((None, FFN_TN, D_MODEL), lambda i, j: (layer, j, 0))] + extra_specs,
        out_specs=pl.BlockSpec((TM_FFN, D_MODEL), lambda i, j: (i, 0)),
        out_shape=jax.ShapeDtypeStruct((m, D_MODEL), F32),
        scratch_shapes=[pltpu.VMEM((TM_FFN, D_MODEL), BF16)],
        compiler_params=_params(2),
        name="ffn_residual",
    )(x, mod, mod, mod, w_in, w_in, w_out, *extra_args)


def _proj_res_kernel(ac0_ref, ac1_ref, as0_ref, as1_ref, w0_ref, w1_ref, xc_ref, xs_ref, gate_ref, oc_ref, os_ref, *, n_c):
    i = pl.program_id(0)

    def mix(a0_ref, a1_ref, x_ref, g):
        acc = _mm(a0_ref[...], w0_ref[...].astype(BF16)) + _mm(a1_ref[...], w1_ref[...].astype(BF16))
        return x_ref[...] + gate_ref[pl.ds(g, 1), :] * acc

    @pl.when(i < n_c)
    def _():
        oc_ref[...] = mix(ac0_ref, ac1_ref, xc_ref, 0)

    @pl.when(i >= n_c)
    def _():
        os_ref[...] = mix(as0_ref, as1_ref, xs_ref, 1 + (i - n_c))


def _out_proj(acts_c, acts_s, w, w_layer, xc, xs, mod, layer, k_gate):
    n_c, n_s = xc.shape[0] // TM, xs.shape[0] // TM
    kw = acts_c[0].shape[1]
    c_idx = lambda i: (jnp.minimum(i, n_c - 1), 0)
    s_idx = lambda i: (jnp.maximum(i - n_c, 0), 0)
    w_specs = [pl.BlockSpec((None, kw, D_MODEL), functools.partial(lambda i, p: (w_layer, p, 0), p=p),
                            pipeline_mode=pl.Buffered(1)) for p in range(2)]
    return pl.pallas_call(
        functools.partial(_proj_res_kernel, n_c=n_c),
        grid=(n_c + n_s,),
        in_specs=[pl.BlockSpec((TM, kw), c_idx)] * 2 + [pl.BlockSpec((TM, kw), s_idx)] * 2 + w_specs + [
            pl.BlockSpec((TM, D_MODEL), c_idx), pl.BlockSpec((TM, D_MODEL), s_idx),
            pl.BlockSpec((None, None, 8, D_MODEL), lambda i: (layer, k_gate, 0, 0)),
        ],
        out_specs=[pl.BlockSpec((TM, D_MODEL), c_idx), pl.BlockSpec((TM, D_MODEL), s_idx)],
        out_shape=[jax.ShapeDtypeStruct(xc.shape, F32), jax.ShapeDtypeStruct(xs.shape, F32)],
        compiler_params=_params(1),
        name="out_proj_residual",
    )(*acts_c, *acts_s, w, w, xc, xs, mod)


def _gqa_kernel(*refs, sample, has_prev=False):
    if sample:
        q_ref, k_ref, v_ref, gq_ref, gk_ref, ck_ref, cv_ref, cos_ref, sin_ref, o_ref, kb_ref, vb_ref = refs
    else:
        n_in = 8 if has_prev else 6
        q_ref, k_ref, v_ref, gq_ref, gk_ref, vf_ref = refs[:6]
        o_ref, kc_ref, vc_ref, kb_ref, vb_ref = refs[n_in:]
    qi = pl.program_id(1)
    n_new = k_ref.shape[0]
    past = PAST_LEN if sample else 0
    rep = H_B // KV_B

    @pl.when(qi == 0)
    def _():
        for g in range(KV_B):
            sl = slice(HD_B * g, HD_B * (g + 1))
            kn = _rms(k_ref[:, sl].astype(F32)) * gk_ref[...]
            if sample:
                kb_ref[0:past, sl] = ck_ref[:, g, :].astype(BF16)
                vb_ref[g, 0:past, 0:HD_B] = cv_ref[:, g, :].astype(BF16)
                kn = kn * cos_ref[...] + pltpu.roll(kn, HD_B // 2, 1) * sin_ref[...]
            else:
                kc_ref[:, g, :] = kn
                vc_ref[:, g, :] = vf_ref[:, sl]
            kb_ref[past:past + n_new, sl] = kn.astype(BF16)
            vb_ref[g, past:past + n_new, 0:HD_B] = v_ref[:, sl].astype(BF16)
            vb_ref[g, :, HD_B:] = jnp.ones((past + n_new, HD_B), BF16)

    r0 = pl.multiple_of(qi * QB, QB)
    qs = []
    for h in range(H_B):
        qn = _rms(q_ref[:, HD_B * h:HD_B * (h + 1)].astype(F32)) * gq_ref[...]
        if sample:
            qn = qn * cos_ref[pl.ds(r0, QB), :] + pltpu.roll(qn, HD_B // 2, 1) * sin_ref[pl.ds(r0, QB), :]
        qs.append((qn * (HD_B ** -0.5)).astype(BF16))
    scores = [_nt(qs[h], kb_ref[:, HD_B * (h // rep):HD_B * (h // rep + 1)]) for h in range(H_B)]
    weights = [jnp.exp(s - jnp.max(s, axis=-1, keepdims=True)).astype(BF16) for s in scores]
    sums = [_mm(weights[h], vb_ref[h // rep]) for h in range(H_B)]
    for h in range(H_B):
        o_ref[:, HD_B * h:HD_B * (h + 1)] = (sums[h][:, :HD_B] / sums[h][:, HD_B:]).astype(o_ref.dtype)


def _gqa(proj, g_q, g_k, n_batch, seq, ctx=None, rope=None, v_f32=None, slot=0, prev=None):
    sample = ctx is not None
    m = n_batch * seq
    nq = seq // QB
    n_even = (DEPTH + 1) // 2
    in_specs = [
        pl.BlockSpec((QB, 512), lambda b, i: (b * nq + i, 3)),
        pl.BlockSpec((seq, 256), lambda b, i: (b, 8)),
        pl.BlockSpec((seq, 256), lambda b, i: (b, 9)),
        pl.BlockSpec((1, HD_B), lambda b, i: (0, 0)),
        pl.BlockSpec((1, HD_B), lambda b, i: (0, 0)),
    ]
    args = [proj, proj, proj, g_q.reshape(1, HD_B), g_k.reshape(1, HD_B)]
    o_spec = pl.BlockSpec((QB, 512), lambda b, i: (b * nq + i, 0))
    o_shape = jax.ShapeDtypeStruct((m, 512), BF16)
    aliases = {}
    if sample:
        cache_spec = pl.BlockSpec((None, None, PAST_LEN, KV_B, HD_B), lambda b, i: (b, slot, 0, 0, 0))
        in_specs += [
            cache_spec, cache_spec,
            pl.BlockSpec((seq, HD_B), lambda b, i: (0, 0)),
            pl.BlockSpec((seq, HD_B), lambda b, i: (0, 0)),
        ]
        args += [ctx[0], ctx[1], rope[0], rope[1]]
        out_specs, out_shape = o_spec, o_shape
    else:
        in_specs.append(pl.BlockSpec((seq, KV_B * HD_B), lambda b, i: (b, 0)))
        args.append(v_f32)
        if prev is not None:
            in_specs += [pl.BlockSpec(memory_space=pl.ANY)] * 2
            aliases = {len(args): 1, len(args) + 1: 2}
            args += list(prev)
        new_spec = pl.BlockSpec((None, None, seq, KV_B, HD_B), lambda b, i: (b, slot, 0, 0, 0))
        new_shape = jax.ShapeDtypeStruct((n_batch, n_even, seq, KV_B, HD_B), F32)
        out_specs, out_shape = [o_spec, new_spec, new_spec], [o_shape, new_shape, new_shape]
    n_keys = seq + (PAST_LEN if sample else 0)
    return pl.pallas_call(
        functools.partial(_gqa_kernel, sample=sample, has_prev=prev is not None),
        grid=(n_batch, nq),
        in_specs=in_specs,
        out_specs=out_specs,
        out_shape=out_shape,
        input_output_aliases=aliases,
        scratch_shapes=[pltpu.VMEM((n_keys, KV_B * HD_B), BF16), pltpu.VMEM((KV_B, n_keys, 2 * HD_B), BF16)],
        compiler_params=_params(2),
        name="gqa_sample" if sample else "gqa_prompt",
    )(*args)


def _mla_kernel(*refs, sample):
    if sample:
        (cq_ref, ckv_ref, kpe_ref, gq_ref, wqb_ref, gkv_ref, wkvb_ref, ex_ref, cckv_ref, ckpe_ref, c4_ref, s4_ref,
         o_ref, kv_s, kx_s) = refs
    else:
        cq_ref, ckv_ref, kpe_ref, gq_ref, wqb_ref, gkv_ref, wkvb_ref, ex_ref, o_ref, ckvn_ref, kv_s, kx_s = refs
    qi = pl.program_id(1)
    n_new = ckv_ref.shape[0]
    past = PAST_LEN if sample else 0

    def rope(x, c, s):
        x1, x2 = x[:, :LANES], x[:, LANES:]
        return jnp.concatenate([x1 * c - x2 * s, x1 * s + x2 * c], axis=1)

    hw = NOPE_D + 2 * V_D

    def stage_kv(rows, kv):
        for h in range(H_D):
            kv_s[rows, hw * h:hw * h + NOPE_D + V_D] = kv[:, 256 * h:256 * (h + 1)].astype(BF16)

    @pl.when(qi == 0)
    def _():
        wkvb = wkvb_ref[...].astype(BF16)
        ckvn = _rms(ckv_ref[...].astype(F32)) * gkv_ref[...]
        if not sample:
            ckvn_ref[...] = ckvn
        stage_kv(slice(past, past + n_new), _mm(ckvn.astype(BF16), wkvb))
        for h in range(H_D):
            kv_s[:, hw * h + NOPE_D + V_D:hw * (h + 1)] = jnp.ones((past + n_new, V_D), BF16)
        kx = _mm(kpe_ref[:, 0:ROPE_D], ex_ref[...])
        if sample:
            stage_kv(slice(0, past), _mm(cckv_ref[...].astype(BF16), wkvb))
            kx_s[0:past, :] = _mm(ckpe_ref[...].astype(BF16), ex_ref[...]).astype(BF16)
            kx = rope(kx, c4_ref[...], s4_ref[...])
        kx_s[past:past + n_new, :] = kx.astype(BF16)

    q = _mm((_rms(cq_ref[...].astype(F32)) * gq_ref[...]).astype(BF16), wqb_ref[...].astype(BF16))
    q = q * ((NOPE_D + ROPE_D) ** -0.5)
    qpe = q[:, 4 * NOPE_D:]
    if sample:
        r0 = pl.multiple_of(qi * QB, QB)
        qpe = rope(qpe, c4_ref[pl.ds(r0, QB), :], s4_ref[pl.ds(r0, QB), :])
    lane_head = (lax.broadcasted_iota(jnp.int32, (1, 2 * LANES), 1) % LANES) // (ROPE_D // 2)
    qb, qpb = q.astype(BF16), qpe.astype(BF16)
    scores = [_nt(qb[:, NOPE_D * h:NOPE_D * (h + 1)], kv_s[:, hw * h:hw * h + NOPE_D])
              + _nt(jnp.where(lane_head == h, qpb, jnp.zeros_like(qpb)), kx_s[...]) for h in range(H_D)]
    weights = [jnp.exp(s - jnp.max(s, axis=-1, keepdims=True)).astype(BF16) for s in scores]
    sums = [_mm(weights[h], kv_s[:, hw * h + NOPE_D:hw * (h + 1)]) for h in range(H_D)]
    for h in range(H_D):
        o_ref[:, V_D * h:V_D * (h + 1)] = (sums[h][:, :V_D] / sums[h][:, V_D:]).astype(o_ref.dtype)


def _mla(proj, g_q, w_qb, g_kv, w_kvb, n_batch, seq, ctx=None, rope=None, expand=None):
    sample = ctx is not None
    m = n_batch * seq
    nq = seq // QB
    in_specs = [
        pl.BlockSpec((QB, Q_RANK), lambda b, i: (b * nq + i, 6)),
        pl.BlockSpec((seq, KV_RANK), lambda b, i: (b, 14)),
        pl.BlockSpec((seq, LANES), lambda b, i: (b, 15)),
        pl.BlockSpec((1, Q_RANK), lambda b, i: (0, 0)),
        pl.BlockSpec((Q_RANK, 768), lambda b, i: (0, 0)),
        pl.BlockSpec((1, KV_RANK), lambda b, i: (0, 0)),
        pl.BlockSpec((KV_RANK, 1024), lambda b, i: (0, 0)),
        pl.BlockSpec((ROPE_D, 256), lambda b, i: (0, 0)),
    ]
    args = [proj, proj, proj, g_q.reshape(1, Q_RANK), w_qb, g_kv.reshape(1, KV_RANK), w_kvb, expand]
    o_spec = pl.BlockSpec((QB, 512), lambda b, i: (b * nq + i, 0))
    o_shape = jax.ShapeDtypeStruct((m, 512), BF16)
    if sample:
        in_specs += [
            pl.BlockSpec((None, PAST_LEN, KV_RANK), lambda b, i: (b, 0, 0)),
            pl.BlockSpec((None, PAST_LEN, ROPE_D), lambda b, i: (b, 0, 0)),
            pl.BlockSpec((seq, LANES), lambda b, i: (0, 0)),
            pl.BlockSpec((seq, LANES), lambda b, i: (0, 0)),
        ]
        args += [ctx[0], ctx[1], rope[0], rope[1]]
        out_specs, out_shape = o_spec, o_shape
    else:
        out_specs = [o_spec, pl.BlockSpec((seq, KV_RANK), lambda b, i: (b, 0))]
        out_shape = [o_shape, jax.ShapeDtypeStruct((m, KV_RANK), F32)]
    n_keys = seq + (PAST_LEN if sample else 0)
    return pl.pallas_call(
        functools.partial(_mla_kernel, sample=sample),
        grid=(n_batch, nq),
        in_specs=in_specs,
        out_specs=out_specs,
        out_shape=out_shape,
        scratch_shapes=[pltpu.VMEM((n_keys, H_D * (NOPE_D + 2 * V_D)), BF16), pltpu.VMEM((n_keys, 256), BF16)],
        compiler_params=_params(2),
        name="mla_sample" if sample else "mla_prompt",
    )(*args)


def _dft(table, x):
    return _mm(table.astype(BF16), x.astype(BF16))


def _filter_kernel(z_ref, wf1_ref, bf1_ref, fr_ref, wf2_ref, bf2_ref, wf3_ref, t_ref, dl_ref,
                   c_ref, s_ref, gre_ref, gim_ref):
    n_tok = z_ref.shape[0]
    fr = fr_ref[...]
    hid = jnp.sin(fr * (_mm(z_ref[...].astype(BF16), wf1_ref[...].astype(BF16)) + bf1_ref[...]))
    hid = jnp.sin(fr * (_mm(hid.astype(BF16), wf2_ref[...].astype(BF16)) + bf2_ref[...]))
    filt = _mm(hid.astype(BF16), wf3_ref[...].astype(BF16))
    decay = jnp.exp(-t_ref[...] * dl_ref[...])
    row = lax.broadcasted_iota(jnp.int32, (n_tok, 1), 0)
    h_f = filt[:, :HY_W] * decay
    h_b = jnp.where(row == 0, 0.0, filt[:, HY_W:] * decay)
    p, m = h_f + h_b, h_f - h_b
    g_re = _dft(c_ref[...], p)
    g_im = _dft(s_ref[...], m)
    sign = jnp.where(row % 2 == 0, 1.0, -1.0)
    nyquist = jnp.sum(p * sign, axis=0, keepdims=True)
    g_im = jnp.where(row == 0, nyquist, g_im)
    wk = jnp.where(row == 0, 0.5 / n_tok, 1.0 / n_tok)
    gre_ref[...] = g_re * wk
    gim_ref[...] = g_im * wk


def _filter_spectrum(z, wf1, bf1, freq, wf2, bf2, wf3, t_col, deltas, tabs):
    n_tok = z.shape[0]
    out = jax.ShapeDtypeStruct((n_tok, HY_W), F32)
    return pl.pallas_call(
        _filter_kernel,
        out_shape=[out, out],
        compiler_params=pltpu.CompilerParams(vmem_limit_bytes=VMEM_LIMIT),
        name="hyena_filter",
    )(z, wf1, bf1.reshape(1, FILT_HID), freq.reshape(1, FILT_HID), wf2, bf2.reshape(1, FILT_HID), wf3,
      t_col, deltas, tabs[0], tabs[1])


HY_CT = 256


HY_ROWS = 1024


def _hyena_kernel(u0_ref, u1_ref, u2_ref, w0_ref, w1_ref, w2_ref, b0_ref, b1_ref, b2_ref, skip_ref,
                  gre_ref, gim_ref, cf_ref, sf_ref, stf_ref, o_ref, c_ref, s_ref, st_ref):
    seq = c_ref.shape[0]
    n_rows = u0_ref.shape[0]
    n_seq = n_rows // seq
    pos = lax.broadcasted_iota(jnp.int32, (n_rows, 1), 0) % seq

    @pl.when((pl.program_id(0) == 0) & (pl.program_id(1) == 0))
    def _():
        c_ref[...] = cf_ref[...].astype(BF16)
        s_ref[...] = sf_ref[...].astype(BF16)
        st_ref[...] = stf_ref[...].astype(BF16)

    def short_conv(u_ref, w_ref, b_ref):
        x, w = u_ref[...].astype(F32), w_ref[...]
        prev = jnp.where(pos == 0, 0.0, pltpu.roll(x, 1, 0))
        nxt = jnp.where(pos == seq - 1, 0.0, pltpu.roll(x, n_rows - 1, 0))
        return prev * w[0:1] + x * w[1:2] + nxt * w[2:3] + b_ref[...]

    def side_by_side(a):
        return a if n_seq == 1 else jnp.concatenate([a[s * seq:(s + 1) * seq] for s in range(n_seq)], axis=1)

    def stacked(a):
        ct = a.shape[1] // n_seq
        return a if n_seq == 1 else jnp.concatenate([a[:, s * ct:(s + 1) * ct] for s in range(n_seq)], axis=0)

    x0 = short_conv(u0_ref, w0_ref, b0_ref)
    gv = short_conv(u1_ref, w1_ref, b1_ref) * short_conv(u2_ref, w2_ref, b2_ref)
    sig = side_by_side(gv).astype(BF16)
    u_re = _mm(c_ref[...], sig)
    u_im = _mm(s_ref[...], sig)
    g_re = jnp.concatenate([gre_ref[...]] * n_seq, axis=1)
    g_im = jnp.concatenate([gim_ref[...]] * n_seq, axis=1)
    bin0 = lax.broadcasted_iota(jnp.int32, (seq, 1), 0) == 0
    p_im = u_im * g_im
    y_re = u_re * g_re - jnp.where(bin0, 0.0, p_im)
    y_im = jnp.where(bin0, p_im, u_re * g_im + u_im * g_re)
    y = stacked(_mm(c_ref[...], y_re.astype(BF16)) + _mm(st_ref[...], y_im.astype(BF16)))
    o_ref[...] = (x0 * (y + gv * skip_ref[...])).astype(o_ref.dtype)


def _hyena(proj, w_conv, b_conv, skip, g_re, g_im, tabs, n_batch, seq):
    nct = HY_W // HY_CT
    u_specs = [pl.BlockSpec((HY_ROWS, HY_CT), functools.partial(lambda b, c, g: (b, g * nct + c), g=g)) for g in range(3)]
    w_specs = [pl.BlockSpec((3, HY_CT), functools.partial(lambda b, c, g: (0, g * nct + c), g=g)) for g in range(3)]
    b_specs = [pl.BlockSpec((1, HY_CT), functools.partial(lambda b, c, g: (0, g * nct + c), g=g)) for g in range(3)]
    tab_spec = pl.BlockSpec((seq, seq), lambda b, c: (0, 0))
    return pl.pallas_call(
        _hyena_kernel,
        grid=(n_batch * seq // HY_ROWS, nct),
        in_specs=u_specs + w_specs + b_specs + [
            pl.BlockSpec((1, HY_CT), lambda b, c: (0, c)),
            pl.BlockSpec((seq, HY_CT), lambda b, c: (0, c)),
            pl.BlockSpec((seq, HY_CT), lambda b, c: (0, c)),
        ] + [tab_spec] * 3,
        out_specs=pl.BlockSpec((HY_ROWS, HY_CT), lambda b, c: (b, c)),
        out_shape=jax.ShapeDtypeStruct((n_batch * seq, HY_W), BF16),
        scratch_shapes=[pltpu.VMEM((seq, seq), BF16)] * 3,
        compiler_params=_params(2),
        name="hyena_conv",
    )(proj, proj, proj, w_conv, w_conv, w_conv, b_conv, b_conv, b_conv, skip.reshape(1, HY_W), g_re, g_im, *tabs)


def _dft_tables(n_tok):
    k = np.arange(n_tok)[:, None]
    s = np.arange(n_tok)[None, :]
    ang = ((k * s) % (2 * n_tok)) * (np.pi / n_tok)
    cos_t = np.cos(ang)
    sin_f = np.where(k == 0, np.where(s % 2 == 0, 1.0, -1.0), -np.sin(ang))
    return [jnp.asarray(t, F32) for t in (cos_t, sin_f, sin_f.T)]


GLA_LEVELS = (32, 16, 8, 4, 2, 1)
GLA_SAFE_DECAY = 60.0
GLA_GROUP = 2


def _gla_constants():
    c = GLA_CHUNK
    idx = np.arange(c)
    i, t = idx[:, None], idx[None, :]
    masks = []
    for s in GLA_LEVELS:
        upper = (idx % (2 * s)) >= s
        masks.append(((i // (2 * s)) == (t // (2 * s))) & upper[:, None] & (~upper)[None, :])
    masks.append(i == t)
    tri = t <= i
    fwd_m = np.stack([np.tile(m, (H_A, 1)) for m in masks]).astype(np.float32)
    bwd_m = np.stack([np.tile(m[::-1, ::-1], (H_A, 1)) for m in masks]).astype(np.float32)
    head_of_row = np.repeat(np.arange(H_A), c)[:, None]
    head_of_lane = np.repeat(np.arange(H_A), DK_A)[None, :]
    head_mask = head_of_row == head_of_lane
    return (jnp.asarray(tri, BF16), jnp.asarray(tri[::-1, ::-1], BF16), jnp.asarray(fwd_m), jnp.asarray(bwd_m),
            jnp.asarray(head_mask, BF16))


def _pair_reference(b, s, backward, row):
    c = GLA_CHUNK
    ref = s if backward else s - 1
    if 2 * s >= 8:
        pieces = [jnp.broadcast_to(b[p * 2 * s + ref:p * 2 * s + ref + 1, :], (2 * s, b.shape[1]))
                  for p in range(c // (2 * s))]
        return pieces[0] if len(pieces) == 1 else jnp.concatenate(pieces, axis=0)
    pos = row % (2 * s)
    out = None
    for o in range(2 * s):
        d = ref - o
        shifted = b if d == 0 else pltpu.roll(b, (-d) % c, 0)
        out = shifted if out is None else jnp.where(pos == o, shifted, out)
    return out


def _chunk_log_decay(la, t_ref):
    l1 = la.astype(BF16)
    r1 = la - l1.astype(F32)
    l2 = r1.astype(BF16)
    l3 = (r1 - l2.astype(F32)).astype(BF16)
    tmat = t_ref[...]
    return _mm(tmat, l1) + _mm(tmat, l2) + _mm(tmat, l3)


def _stack_heads(a, hm):
    ab = a.astype(BF16)
    return jnp.concatenate([ab] * H_A, axis=0) * hm


def _state_terms(k, v, b, b_last):
    c = GLA_CHUNK
    k_rest = (k * jnp.exp(b_last - b)).T
    carry = jnp.broadcast_to(jnp.exp(b_last), (2 * c, b.shape[1])).T
    return k_rest.astype(BF16), carry


def _gla_chunk(q, k, v, la, t_ref, m_ref, hm, s_ref, backward):
    c = GLA_CHUNK
    b = _chunk_log_decay(la, t_ref)
    row = lax.broadcasted_iota(jnp.int32, (c, 1), 0)
    last = 0 if backward else c - 1
    b_last = b[last:last + 1, :]
    scores = _nt(_stack_heads(q, hm), k.astype(BF16)) * m_ref[len(GLA_LEVELS)]
    for lvl, s in enumerate(GLA_LEVELS):
        is_query = (row % (2 * s) < s) if backward else (row % (2 * s) >= s)
        delta = b - _pair_reference(b, s, backward, row)
        x = jnp.exp(jnp.where(is_query, delta, -delta))
        scores = scores + _nt(_stack_heads(q * x, hm), (k * x).astype(BF16)) * m_ref[lvl]
    scores = scores.astype(BF16)
    state = s_ref[...]
    inter = _mm(_stack_heads(q * jnp.exp(b), hm), state.astype(BF16))
    k_rest, carry = _state_terms(k, v, b, b_last)
    outs = []
    for h in range(H_A):
        rows = slice(c * h, c * (h + 1))
        v_h = v[:, DV_A * h:DV_A * (h + 1)]
        outs.append(_mm(scores[rows], v_h) + inter[rows])
        s_ref[rows, :] = state[rows] * carry[rows] + _mm(k_rest[rows], v_h)
    return jnp.concatenate(outs, axis=1)


def _gla_local(items, hm):
    c = GLA_CHUNK
    bs = [_chunk_log_decay(la, t_ref) for _, _, _, la, t_ref, _, _ in items]
    b_lasts = [b[(0 if it[6] else c - 1):(0 if it[6] else c - 1) + 1, :] for b, it in zip(bs, items)]
    q_decayed = [_stack_heads(it[0] * jnp.exp(b), hm) for it, b in zip(items, bs)]
    k_grown = [(it[1] * jnp.exp(-b)).astype(BF16) for it, b in zip(items, bs)]
    raw = [_nt(qd, kg) for qd, kg in zip(q_decayed, k_grown)]
    scores = [(r * it[5]).astype(BF16) for r, it in zip(raw, items)]
    terms = [_state_terms(it[1], it[2], b, bl) for it, b, bl in zip(items, bs, b_lasts)]
    out = []
    for it, sc, (k_rest, carry), qd in zip(items, scores, terms, q_decayed):
        v = it[2]
        heads = [(slice(c * h, c * (h + 1)), v[:, DV_A * h:DV_A * (h + 1)]) for h in range(H_A)]
        intra = jnp.concatenate([_mm(sc[rows], v_h) for rows, v_h in heads], axis=1)
        incr = jnp.concatenate([_mm(k_rest[rows], v_h) for rows, v_h in heads], axis=0)
        out.append((intra, qd, incr, carry))
    return out


def _gla_kernel(*refs, sample):
    if sample:
        (x_ref, z_ref, wf_ref, bf_ref, wb_ref, bb_ref, tf_ref, tb_ref, mf_ref, mb_ref, hm_ref, gn_ref, sf0_ref, sb0_ref,
         o_ref, la_f, la_b, o_f, o_b, s_f, s_b, qd_f, qd_b, ds_f, ds_b, cr_f, cr_b) = refs
    else:
        (x_ref, z_ref, wf_ref, bf_ref, wb_ref, bb_ref, tf_ref, tb_ref, mf_ref, mb_ref, hm_ref, gn_ref,
         o_ref, sf_out, sb_out, la_f, la_b, o_f, o_b, s_f, s_b, qd_f, qd_b, ds_f, ds_b, cr_f, cr_b) = refs
    n_tok = x_ref.shape[0]
    n_chunks = n_tok // GLA_CHUNK
    hk, hv = H_A * DK_A, H_A * DV_A
    zb = z_ref[...].astype(BF16)

    def log_sigmoid(t):
        return jnp.minimum(t, 0.0) - jnp.log(1.0 + jnp.exp(-jnp.abs(t)))

    la_f[...] = log_sigmoid(_mm(zb, wf_ref[...].astype(BF16)) + bf_ref[...]) / GLA_TAU
    la_b[...] = log_sigmoid(_mm(zb, wb_ref[...].astype(BF16)) + bb_ref[...]) / GLA_TAU
    if sample:
        s_f[...] = sf0_ref[...]
        s_b[...] = sb0_ref[...]
    else:
        s_f[...] = jnp.zeros_like(s_f)
        s_b[...] = jnp.zeros_like(s_b)
    hm = hm_ref[...]

    fwd = (la_f, tf_ref, mf_ref, s_f, o_f, qd_f, ds_f, cr_f, False)
    bwd = (la_b, tb_ref, mb_ref, s_b, o_b, qd_b, ds_b, cr_b, True)
    tri_f = jnp.sum(mf_ref[...], axis=0)
    tri_b = jnp.sum(mb_ref[...], axis=0)

    def chunk_rows(ci, backward):
        cidx = n_chunks - 1 - ci if backward else ci
        return cidx, pl.ds(pl.multiple_of(cidx * GLA_CHUNK, GLA_CHUNK), GLA_CHUNK)

    def load_qkv(rows):
        q = x_ref[rows, 0:hk].astype(F32) * (DK_A ** -0.5)
        return q, x_ref[rows, hk:2 * hk].astype(F32), x_ref[rows, 2 * hk:2 * hk + hv]

    def safe_step(ci, carry):
        for la_ref, t_ref, m_ref, s_ref, out_ref, _, _, _, backward in (fwd, bwd):
            _, rows = chunk_rows(ci, backward)
            out_ref[rows, :] = _gla_chunk(*load_qkv(rows), la_ref[rows, :], t_ref, m_ref, hm, s_ref, backward)
        return carry

    def local_step(gi, carry):
        items, dests = [], []
        for (la_ref, t_ref, _, _, out_ref, qd_ref, ds_ref, cr_ref, backward), tri in ((fwd, tri_f), (bwd, tri_b)):
            for u in range(GLA_GROUP):
                cidx, rows = chunk_rows(gi * GLA_GROUP + u, backward)
                items.append((*load_qkv(rows), la_ref[rows, :], t_ref, tri, backward))
                dests.append((out_ref, rows, qd_ref, ds_ref, cr_ref, cidx))
        for (out_ref, rows, qd_ref, ds_ref, cr_ref, cidx), (intra, qd, incr, factor) in zip(dests, _gla_local(items, hm)):
            out_ref[rows, :] = intra
            qd_ref[cidx] = qd
            ds_ref[cidx] = incr
            cr_ref[cidx] = factor
        return carry

    def scan_step(ci, carry):
        for _, _, _, s_ref, out_ref, qd_ref, ds_ref, cr_ref, backward in (fwd, bwd):
            cidx, rows = chunk_rows(ci, backward)
            state = s_ref[...]
            inter = _mm(qd_ref[cidx], state.astype(BF16))
            out_ref[rows, :] += jnp.concatenate(
                [inter[GLA_CHUNK * h:GLA_CHUNK * (h + 1)] for h in range(H_A)], axis=1)
            s_ref[...] = state * cr_ref[cidx] + ds_ref[cidx]
        return carry

    chunk_sums = [jnp.sum(ref[...].reshape(n_chunks, GLA_CHUNK, hk), axis=1) for ref in (la_f, la_b)]
    mild = jnp.minimum(jnp.min(chunk_sums[0]), jnp.min(chunk_sums[1])) > -GLA_SAFE_DECAY

    @pl.when(mild)
    def _():
        lax.fori_loop(0, n_chunks // GLA_GROUP, local_step, 0, unroll=2)
        lax.fori_loop(0, n_chunks, scan_step, 0, unroll=2)

    @pl.when(jnp.logical_not(mild))
    def _():
        lax.fori_loop(0, n_chunks, safe_step, 0)
    if not sample:
        sf_out[...] = s_f[...]
        sb_out[...] = s_b[...]
    gain = gn_ref[...]
    for h in range(H_A):
        cols = slice(DV_A * h, DV_A * (h + 1))
        r = x_ref[:, 2 * hk + hv + DV_A * h:2 * hk + hv + DV_A * (h + 1)].astype(F32)
        o_ref[:, cols] = (_rms(o_f[:, cols] + o_b[:, cols]) * gain * (r * jax.nn.sigmoid(r))).astype(o_ref.dtype)


def _gla(proj, w_gf, b_gf, w_gb, b_gb, g_norm, consts, n_batch, seq, ctx=None):
    sample = ctx is not None
    hk, hv = H_A * DK_A, H_A * DV_A
    n_ch = seq // GLA_CHUNK
    full = lambda shape: pl.BlockSpec(shape, lambda b: (0,) * len(shape))
    in_specs = [
        pl.BlockSpec((seq, 2 * hk + 2 * hv), lambda b: (b, 0)),
        pl.BlockSpec((seq, LANES), lambda b: (b, EVEN_W // LANES - 1)),
        full((LANES, hk)), full((1, hk)), full((LANES, hk)), full((1, hk)),
        full(consts[0].shape), full(consts[1].shape), full(consts[2].shape), full(consts[3].shape), full(consts[4].shape),
        full((1, DV_A)),
    ]
    args = [proj, proj, w_gf, b_gf.reshape(1, hk), w_gb, b_gb.reshape(1, hk), *consts, g_norm.reshape(1, DV_A)]
    o_spec = pl.BlockSpec((seq, hv), lambda b: (b, 0))
    o_shape = jax.ShapeDtypeStruct((n_batch * seq, hv), BF16)
    st_spec = pl.BlockSpec((None, hk, DV_A), lambda b: (b, 0, 0))
    if sample:
        in_specs += [st_spec, st_spec]
        args += [ctx[0], ctx[1]]
        out_specs, out_shape = o_spec, o_shape
    else:
        st_shape = jax.ShapeDtypeStruct((n_batch, hk, DV_A), F32)
        out_specs, out_shape = [o_spec, st_spec, st_spec], [o_shape, st_shape, st_shape]
    return pl.pallas_call(
        functools.partial(_gla_kernel, sample=sample),
        grid=(n_batch,),
        in_specs=in_specs,
        out_specs=out_specs,
        out_shape=out_shape,
        scratch_shapes=[pltpu.VMEM((seq, hk), F32), pltpu.VMEM((seq, hk), F32),
                        pltpu.VMEM((seq, hv), F32), pltpu.VMEM((seq, hv), F32),
                        pltpu.VMEM((hk, DV_A), F32), pltpu.VMEM((hk, DV_A), F32),
                        pltpu.VMEM((n_ch, H_A * GLA_CHUNK, hk), BF16), pltpu.VMEM((n_ch, H_A * GLA_CHUNK, hk), BF16),
                        pltpu.VMEM((n_ch, hk, DV_A), F32), pltpu.VMEM((n_ch, hk, DV_A), F32),
                        pltpu.VMEM((n_ch, hk, DV_A), F32), pltpu.VMEM((n_ch, hk, DV_A), F32)],
        compiler_params=_params(1),
        name="gla_sample" if sample else "gla_prompt",
    )(*args)


def _axial_rope(n_tokens, dim):
    rows = n_tokens // GRID_W
    row = np.repeat(np.arange(rows), GRID_W).astype(np.float64)
    col = np.tile(np.arange(GRID_W), rows).astype(np.float64)
    n_freq = dim // 4
    inv = ROPE_THETA ** (-np.arange(n_freq) / n_freq)
    ang = np.concatenate([row[:, None] * inv, col[:, None] * inv], axis=-1)
    return np.cos(ang).astype(np.float32), np.sin(ang).astype(np.float32)


def _filter_features(n_tokens):
    t = np.linspace(0.0, 1.0, n_tokens)[:, None]
    w = 2.0 * np.pi * np.arange(n_tokens)[:, None] / n_tokens
    f = np.linspace(1e-4, FILT_BANDS - 1, FILT_BANDS)[None, :]
    z = np.concatenate([t, np.cos(f * w), -np.sin(f * w)], axis=-1)
    z = np.pad(z, ((0, 0), (0, LANES - FILT_EMB)))
    return jnp.asarray(z, F32), jnp.asarray(t, F32)


_KPE_EXPAND = np.array([(p // LANES) * (ROPE_D // 2) + p % (ROPE_D // 2) for p in range(2 * LANES)])
_QB_PERM = np.array(
    [192 * (p // NOPE_D) + p % NOPE_D for p in range(H_D * NOPE_D)]
    + [192 * (p // 32) + NOPE_D + p % 32 for p in range(H_D * 32)]
    + [192 * (p // 32) + NOPE_D + 32 + p % 32 for p in range(H_D * 32)])

EVEN_ROW_GROUPS = ((0, 0, 1536), (1568, 1536, 1024), (1536, EVEN_W - 2 * GATE_RANK, 2 * GATE_RANK))
ODD_ROW_GROUPS = ((0, 0, 1984),)
EVEN_KEEP = (2304, 256)
ODD_KEEP = (1920, LANES)


def kernel(x_prompt, x_sample, state_gla_fwd, state_gla_bwd, cache_gqa_k, cache_gqa_v, cache_mla_ckv, cache_mla_kpe, c, c_ctx, w_mod, b_mod, w_in_even, w_gla_gate_f, b_gla_gate_f, w_gla_gate_b, b_gla_gate_b, g_gla_norm, g_gqa_q, g_gqa_k, w_out_even, w_in_odd, w_hy_conv, b_hy_conv, hy_skip, w_filt1, b_filt1, filt_freq, w_filt2, b_filt2, w_filt3, g_mla_q, w_mla_qb, g_mla_kv, w_mla_kvb, w_out_odd, w_ffn_in, w_ffn_out, g_final):
    n_c, n_s = BATCH * SEQ, DEC_BATCH * DEC_SEQ
    cvec = jnp.concatenate([c_ctx[None, :], c, jnp.zeros((8 - 1 - DEC_BATCH, D_MODEL), F32)], axis=0)
    mod = _modulation(cvec, w_mod, b_mod)
    xc = x_prompt.reshape(n_c, D_MODEL)
    xs = x_sample.reshape(n_s, D_MODEL)
    rows_c, rows_s = (0, 0), (1, DEC_SEQ // MOD_ROWS)

    gla_consts = _gla_constants()
    cos_b, sin_b = _axial_rope(DEC_SEQ, HD_B)
    rope_b = (jnp.asarray(np.concatenate([cos_b, cos_b], axis=1)), jnp.asarray(np.concatenate([-sin_b, sin_b], axis=1)))
    cos_d, sin_d = _axial_rope(DEC_SEQ, ROPE_D)
    rope_d = (jnp.asarray(np.tile(cos_d, (1, H_D))), jnp.asarray(np.tile(sin_d, (1, H_D))))
    kpe_expand = jnp.asarray(np.arange(ROPE_D)[:, None] == _KPE_EXPAND[None, :], BF16)
    tabs_c, tabs_s = _dft_tables(SEQ), _dft_tables(DEC_SEQ)
    z_c, t_c = _filter_features(SEQ)
    z_s, t_s = _filter_features(DEC_SEQ)
    deltas = jnp.asarray(np.abs(np.linspace(HY_MIN_DECAY, HY_MAX_DECAY, HY_W))[None, :], F32)

    wt_even = jnp.swapaxes(w_in_even, 1, 2)
    wt_odd = jnp.swapaxes(w_in_odd, 1, 2)

    st_gf, st_gb, st_ckv, st_kpe = [], [], [], []
    new_kv = None
    for i in range(DEPTH):
        j = i // 2
        if i % 2 == 0:
            z0 = LANES - 2 * GATE_RANK
            pad_f = jnp.zeros((LANES, H_A * DK_A), F32).at[z0:z0 + GATE_RANK].set(w_gla_gate_f[j])
            pad_b = jnp.zeros((LANES, H_A * DK_A), F32).at[z0 + GATE_RANK:LANES].set(w_gla_gate_b[j])
            pc, v_new, ps = _in_proj(xc, xs, mod, i, wt_even, j, EVEN_ROW_GROUPS, EVEN_W, EVEN_KEEP)
            gate_args = (pad_f, b_gla_gate_f[j], pad_b, b_gla_gate_b[j], g_gla_norm[j], gla_consts)
            a_c, s_f, s_b = _gla(pc, *gate_args, BATCH, SEQ)
            ctx_a = (state_gla_fwd[:, j].reshape(DEC_BATCH, H_A * DK_A, DV_A),
                     state_gla_bwd[:, j].reshape(DEC_BATCH, H_A * DK_A, DV_A))
            a_s = _gla(ps, *gate_args, DEC_BATCH, DEC_SEQ, ctx=ctx_a)
            b_c, *new_kv = _gqa(pc, g_gqa_q[j], g_gqa_k[j], BATCH, SEQ, v_f32=v_new, slot=j, prev=new_kv)
            b_s = _gqa(ps, g_gqa_q[j], g_gqa_k[j], DEC_BATCH, DEC_SEQ, ctx=(cache_gqa_k, cache_gqa_v), rope=rope_b, slot=j)
            w_out = w_out_even
            st_gf.append(s_f.reshape(BATCH, H_A, DK_A, DV_A))
            st_gb.append(s_b.reshape(BATCH, H_A, DK_A, DV_A))
        else:
            pc, kpe_new, ps = _in_proj(xc, xs, mod, i, wt_odd, j, ODD_ROW_GROUPS, ODD_W, ODD_KEEP)
            wf1 = jnp.pad(w_filt1[j], ((0, LANES - FILT_EMB), (0, 0)))
            filt_args = (wf1, b_filt1[j], filt_freq[j], w_filt2[j], b_filt2[j], w_filt3[j])
            g_c = _filter_spectrum(z_c, *filt_args, t_c, deltas, tabs_c)
            g_s = _filter_spectrum(z_s, *filt_args, t_s, deltas, tabs_s)
            b_conv = b_hy_conv[j].reshape(1, 3 * HY_W)
            a_c = _hyena(pc, w_hy_conv[j], b_conv, hy_skip[j], g_c[0], g_c[1], tabs_c, BATCH, SEQ)
            a_s = _hyena(ps, w_hy_conv[j], b_conv, hy_skip[j], g_s[0], g_s[1], tabs_s, DEC_BATCH, DEC_SEQ)
            w_qb = w_mla_qb[j][:, _QB_PERM]
            b_c, ckv_norm = _mla(pc, g_mla_q[j], w_qb, g_mla_kv[j], w_mla_kvb[j], BATCH, SEQ, expand=kpe_expand)
            b_s = _mla(ps, g_mla_q[j], w_qb, g_mla_kv[j], w_mla_kvb[j], DEC_BATCH, DEC_SEQ,
                       ctx=(cache_mla_ckv[:, j], cache_mla_kpe[:, j]), rope=rope_d, expand=kpe_expand)
            w_out = w_out_odd
            st_ckv.append(ckv_norm.reshape(BATCH, SEQ, KV_RANK))
            st_kpe.append(kpe_new[:, :ROPE_D].reshape(BATCH, SEQ, ROPE_D))
        xc, xs = _out_proj([a_c, b_c], [a_s, b_s], w_out, j, xc, xs, mod, i, 2)
        last = g_final if i == DEPTH - 1 else None
        xc = _ffn(xc, mod, i, w_ffn_in, w_ffn_out, *rows_c, final_gain=last)
        xs = _ffn(xs, mod, i, w_ffn_in, w_ffn_out, *rows_s, final_gain=last)
    y_prompt = xc.reshape(BATCH, SEQ, D_MODEL)
    y_sample = xs.reshape(DEC_BATCH, DEC_SEQ, D_MODEL)
    return (y_prompt, y_sample, jnp.stack(st_gf, axis=1), jnp.stack(st_gb, axis=1), new_kv[0], new_kv[1],
            jnp.stack(st_ckv, axis=1), jnp.stack(st_kpe, axis=1))
```

```python
import functools
import math

import numpy as np
import jax
import jax.numpy as jnp
from jax import lax
from jax.experimental import pallas as pl
from jax.experimental.pallas import tpu as pltpu

F32 = jnp.float32
BF16 = jnp.bfloat16

D_MODEL = 1024
BATCH, SEQ = 16, 256
DEC_BATCH, DEC_SEQ = 2, 1024
DEPTH = 4
PAST_LEN = 512
GRID_W = 64
HALF_W = D_MODEL // 2
H_A, DV_A, DK_A = 4, 128, 64
GATE_RANK = 16
GLA_TAU = 16.0
GLA_CHUNK = 64
HD_B, H_B, KV_B = 128, 4, 2
HY_W = HALF_W
FILT_EMB, FILT_HID = 33, 64
FILT_BANDS = (FILT_EMB - 1) // 2
HY_MIN_DECAY = math.log(1e-2) / 1.5
HY_MAX_DECAY = math.log(1e-2) / 0.3
H_D, V_D, NOPE_D, ROPE_D = 4, 128, 128, 64
Q_RANK, KV_RANK = 256, 128
FFN_H = 2816
ROPE_THETA = 10000.0
EPS = 1e-6

LANES = 128
VMEM_LIMIT = 56 * 1024 * 1024

MOD_ROWS = 1024
TM = 1024
TM_IN = 512
TM_FFN = 2048
EVEN_W = 2688
ODD_W = 2048
FFN_TN = 256
QB = 256


def _params(n_grid):
    return pltpu.CompilerParams(dimension_semantics=("arbitrary",) * n_grid, vmem_limit_bytes=VMEM_LIMIT)


def _nt(a, b):
    return lax.dot_general(a, b, (((1,), (1,)), ((), ())), preferred_element_type=F32)


def _mm(a, b):
    return jnp.dot(a, b, preferred_element_type=F32)


def _rms(x):
    return x * lax.rsqrt(jnp.mean(x * x, axis=-1, keepdims=True) + EPS)


def _mod_kernel(c_ref, w_ref, b_ref, o_ref):
    cv = c_ref[...]
    s = cv * jax.nn.sigmoid(cv)
    o_ref[...] = _mm(s.astype(BF16), w_ref[...].astype(BF16)) + b_ref[...]


def _modulation(cvec, w_mod, b_mod):
    return pl.pallas_call(
        _mod_kernel,
        grid=(DEPTH, 6),
        in_specs=[
            pl.BlockSpec((8, D_MODEL), lambda l, n: (0, 0)),
            pl.BlockSpec((None, D_MODEL, D_MODEL), lambda l, n: (l, 0, n)),
            pl.BlockSpec((None, 1, D_MODEL), lambda l, n: (l, 0, n)),
        ],
        out_specs=pl.BlockSpec((None, None, 8, D_MODEL), lambda l, n: (l, n, 0, 0)),
        out_shape=jax.ShapeDtypeStruct((DEPTH, 6, 8, D_MODEL), F32),
        compiler_params=_params(2),
        name="adaln_mod",
    )(cvec, w_mod, b_mod.reshape(DEPTH, 1, 6 * D_MODEL))


def _mod_row(row0, rstep, tile_rows, sub):
    if tile_rows >= MOD_ROWS:
        return row0 + rstep * (pl.program_id(0) * (tile_rows // MOD_ROWS) + sub)
    return row0 + rstep * (pl.program_id(0) // (MOD_ROWS // tile_rows))


def _in_proj_kernel(xc_ref, xs_ref, sh_ref, sc_ref, wt_ref, oc_ref, keep_ref, os_ref, wb_ref, *, row_groups, keep, n_c):
    i = pl.program_id(0)

    @pl.when(i == 0)
    def _():
        wb_ref[...] = jnp.zeros_like(wb_ref)
        for src, dst, size in row_groups:
            wb_ref[dst:dst + size, :] = wt_ref[src:src + size, :].astype(BF16)

    def project(x_ref, g):
        h = (_rms(x_ref[...]) * (1.0 + sc_ref[pl.ds(g, 1), :]) + sh_ref[pl.ds(g, 1), :]).astype(BF16)
        return _nt(h, wb_ref[...])

    @pl.when(i < n_c)
    def _():
        y = project(xc_ref, 0)
        oc_ref[...] = y.astype(oc_ref.dtype)
        keep_ref[...] = y[:, keep[0]:keep[0] + keep[1]]

    @pl.when(i >= n_c)
    def _():
        os_ref[...] = project(xs_ref, 1 + (i - n_c) // (MOD_ROWS // TM_IN)).astype(os_ref.dtype)


def _in_proj(xc, xs, mod, layer, wt, w_layer, row_groups, n, keep):
    n_c, n_s = xc.shape[0] // TM_IN, xs.shape[0] // TM_IN
    c_idx = lambda i: (jnp.minimum(i, n_c - 1), 0)
    s_idx = lambda i: (jnp.maximum(i - n_c, 0), 0)
    return pl.pallas_call(
        functools.partial(_in_proj_kernel, row_groups=row_groups, keep=keep, n_c=n_c),
        grid=(n_c + n_s,),
        in_specs=[pl.BlockSpec((TM_IN, D_MODEL), c_idx), pl.BlockSpec((TM_IN, D_MODEL), s_idx),
                  pl.BlockSpec((None, None, 8, D_MODEL), lambda i: (layer, 0, 0, 0)),
                  pl.BlockSpec((None, None, 8, D_MODEL), lambda i: (layer, 1, 0, 0)),
                  pl.BlockSpec((None, wt.shape[1], D_MODEL), lambda i: (w_layer, 0, 0), pipeline_mode=pl.Buffered(1))],
        out_specs=[pl.BlockSpec((TM_IN, n), c_idx), pl.BlockSpec((TM_IN, keep[1]), c_idx), pl.BlockSpec((TM_IN, n), s_idx)],
        out_shape=[jax.ShapeDtypeStruct((xc.shape[0], n), BF16), jax.ShapeDtypeStruct((xc.shape[0], keep[1]), F32),
                   jax.ShapeDtypeStruct((xs.shape[0], n), BF16)],
        scratch_shapes=[pltpu.VMEM((n, D_MODEL), BF16)],
        compiler_params=_params(1),
        name="norm_mod_proj",
    )(xc, xs, mod, mod, wt)


def _ffn_kernel(x_ref, sh_ref, sc_ref, gate_ref, wg_ref, wu_ref, wd_ref, *refs, row0, rstep, final):
    (gf_ref, o_ref, h_ref) = refs if final else (None,) + refs
    n_sub = x_ref.shape[0] // MOD_ROWS
    subs = [(slice(s * MOD_ROWS, (s + 1) * MOD_ROWS), _mod_row(row0, rstep, x_ref.shape[0], s)) for s in range(n_sub)]

    @pl.when(pl.program_id(1) == 0)
    def _():
        for rows, g in subs:
            x = x_ref[rows, :]
            o_ref[rows, :] = x
            h_ref[rows, :] = (_rms(x) * (1.0 + sc_ref[pl.ds(g, 1), :]) + sh_ref[pl.ds(g, 1), :]).astype(BF16)

    wg = wg_ref[...].astype(BF16)
    wu = wu_ref[...].astype(BF16)
    wd = wd_ref[...].astype(BF16)
    for rows, g in subs:
        h = h_ref[rows, :]
        a = _mm(h, wg)
        act = (a * jax.nn.sigmoid(a) * _mm(h, wu)).astype(BF16)
        o_ref[rows, :] += gate_ref[pl.ds(g, 1), :] * _mm(act, wd)

    if final:
        @pl.when(pl.program_id(1) == pl.num_programs(1) - 1)
        def _():
            for rows, _ in subs:
                o_ref[rows, :] = _rms(o_ref[rows, :]) * gf_ref[...]


def _ffn(x, mod, layer, w_in, w_out, row0, rstep, final_gain=None):
    m = x.shape[0]
    nj = FFN_H // FFN_TN
    mod_spec = lambda k: pl.BlockSpec((None, None, 8, D_MODEL), lambda i, j: (layer, k, 0, 0))
    final = final_gain is not None
    extra_specs = [pl.BlockSpec((1, D_MODEL), lambda i, j: (0, 0))] if final else []
    extra_args = [final_gain.reshape(1, D_MODEL)] if final else []
    return pl.pallas_call(
        functools.partial(_ffn_kernel, row0=row0, rstep=rstep, final=final),
        grid=(m // TM_FFN, nj),
        in_specs=[pl.BlockSpec((TM_FFN, D_MODEL), lambda i, j: (i, 0)), mod_spec(3), mod_spec(4), mod_spec(5),
                  pl.BlockSpec((None, D_MODEL, FFN_TN), lambda i, j: (layer, 0, j)),
                  pl.BlockSpec((None, D_MODEL, FFN_TN), lambda i, j: (layer, 0, j + nj)),
                  pl.BlockSpec((None, FFN_TN, D_MODEL), lambda i, j: (layer, j, 0))] + extra_specs,
        out_specs=pl.BlockSpec((TM_FFN, D_MODEL), lambda i, j: (i, 0)),
        out_shape=jax.ShapeDtypeStruct((m, D_MODEL), F32),
        scratch_shapes=[pltpu.VMEM((TM_FFN, D_MODEL), BF16)],
        compiler_params=_params(2),
        name="ffn_residual",
    )(x, mod, mod, mod, w_in, w_in, w_out, *extra_args)


def _proj_res_kernel(ac0_ref, ac1_ref, as0_ref, as1_ref, w0_ref, w1_ref, xc_ref, xs_ref, gate_ref, oc_ref, os_ref, *, n_c):
    i = pl.program_id(0)

    def mix(a0_ref, a1_ref, x_ref, g):
        acc = _mm(a0_ref[...], w0_ref[...].astype(BF16)) + _mm(a1_ref[...], w1_ref[...].astype(BF16))
        return x_ref[...] + gate_ref[pl.ds(g, 1), :] * acc

    @pl.when(i < n_c)
    def _():
        oc_ref[...] = mix(ac0_ref, ac1_ref, xc_ref, 0)

    @pl.when(i >= n_c)
    def _():
        os_ref[...] = mix(as0_ref, as1_ref, xs_ref, 1 + (i - n_c))


def _out_proj(acts_c, acts_s, w, w_layer, xc, xs, mod, layer, k_gate):
    n_c, n_s = xc.shape[0] // TM, xs.shape[0] // TM
    kw = acts_c[0].shape[1]
    c_idx = lambda i: (jnp.minimum(i, n_c - 1), 0)
    s_idx = lambda i: (jnp.maximum(i - n_c, 0), 0)
    w_specs = [pl.BlockSpec((None, kw, D_MODEL), functools.partial(lambda i, p: (w_layer, p, 0), p=p),
                            pipeline_mode=pl.Buffered(1)) for p in range(2)]
    return pl.pallas_call(
        functools.partial(_proj_res_kernel, n_c=n_c),
        grid=(n_c + n_s,),
        in_specs=[pl.BlockSpec((TM, kw), c_idx)] * 2 + [pl.BlockSpec((TM, kw), s_idx)] * 2 + w_specs + [
            pl.BlockSpec((TM, D_MODEL), c_idx), pl.BlockSpec((TM, D_MODEL), s_idx),
            pl.BlockSpec((None, None, 8, D_MODEL), lambda i: (layer, k_gate, 0, 0)),
        ],
        out_specs=[pl.BlockSpec((TM, D_MODEL), c_idx), pl.BlockSpec((TM, D_MODEL), s_idx)],
        out_shape=[jax.ShapeDtypeStruct(xc.shape, F32), jax.ShapeDtypeStruct(xs.shape, F32)],
        compiler_params=_params(1),
        name="out_proj_residual",
    )(*acts_c, *acts_s, w, w, xc, xs, mod)


def _gqa_kernel(*refs, sample, has_prev=False):
    if sample:
        q_ref, k_ref, v_ref, gq_ref, gk_ref, ck_ref, cv_ref, cos_ref, sin_ref, o_ref, kb_ref, vb_ref = refs
    else:
        n_in = 8 if has_prev else 6
        q_ref, k_ref, v_ref, gq_ref, gk_ref, vf_ref = refs[:6]
        o_ref, kc_ref, vc_ref, kb_ref, vb_ref = refs[n_in:]
    qi = pl.program_id(1)
    n_new = k_ref.shape[0]
    past = PAST_LEN if sample else 0
    rep = H_B // KV_B

    @pl.when(qi == 0)
    def _():
        for g in range(KV_B):
            sl = slice(HD_B * g, HD_B * (g + 1))
            kn = _rms(k_ref[:, sl].astype(F32)) * gk_ref[...]
            if sample:
                kb_ref[0:past, sl] = ck_ref[:, g, :].astype(BF16)
                vb_ref[g, 0:past, 0:HD_B] = cv_ref[:, g, :].astype(BF16)
                kn = kn * cos_ref[...] + pltpu.roll(kn, HD_B // 2, 1) * sin_ref[...]
            else:
                kc_ref[:, g, :] = kn
                vc_ref[:, g, :] = vf_ref[:, sl]
            kb_ref[past:past + n_new, sl] = kn.astype(BF16)
            vb_ref[g, past:past + n_new, 0:HD_B] = v_ref[:, sl].astype(BF16)
            vb_ref[g, :, HD_B:] = jnp.ones((past + n_new, HD_B), BF16)

    r0 = pl.multiple_of(qi * QB, QB)
    qs = []
    for h in range(H_B):
        qn = _rms(q_ref[:, HD_B * h:HD_B * (h + 1)].astype(F32)) * gq_ref[...]
        if sample:
            qn = qn * cos_ref[pl.ds(r0, QB), :] + pltpu.roll(qn, HD_B // 2, 1) * sin_ref[pl.ds(r0, QB), :]
        qs.append((qn * (HD_B ** -0.5)).astype(BF16))
    scores = [_nt(qs[h], kb_ref[:, HD_B * (h // rep):HD_B * (h // rep + 1)]) for h in range(H_B)]
    weights = [jnp.exp(s - jnp.max(s, axis=-1, keepdims=True)).astype(BF16) for s in scores]
    sums = [_mm(weights[h], vb_ref[h // rep]) for h in range(H_B)]
    for h in range(H_B):
        o_ref[:, HD_B * h:HD_B * (h + 1)] = (sums[h][:, :HD_B] / sums[h][:, HD_B:]).astype(o_ref.dtype)


def _gqa(proj, g_q, g_k, n_batch, seq, ctx=None, rope=None, v_f32=None, slot=0, prev=None):
    sample = ctx is not None
    m = n_batch * seq
    nq = seq // QB
    n_even = (DEPTH + 1) // 2
    in_specs = [
        pl.BlockSpec((QB, 512), lambda b, i: (b * nq + i, 3)),
        pl.BlockSpec((seq, 256), lambda b, i: (b, 8)),
        pl.BlockSpec((seq, 256), lambda b, i: (b, 9)),
        pl.BlockSpec((1, HD_B), lambda b, i: (0, 0)),
        pl.BlockSpec((1, HD_B), lambda b, i: (0, 0)),
    ]
    args = [proj, proj, proj, g_q.reshape(1, HD_B), g_k.reshape(1, HD_B)]
    o_spec = pl.BlockSpec((QB, 512), lambda b, i: (b * nq + i, 0))
    o_shape = jax.ShapeDtypeStruct((m, 512), BF16)
    aliases = {}
    if sample:
        cache_spec = pl.BlockSpec((None, None, PAST_LEN, KV_B, HD_B), lambda b, i: (b, slot, 0, 0, 0))
        in_specs += [
            cache_spec, cache_spec,
            pl.BlockSpec((seq, HD_B), lambda b, i: (0, 0)),
            pl.BlockSpec((seq, HD_B), lambda b, i: (0, 0)),
        ]
        args += [ctx[0], ctx[1], rope[0], rope[1]]
        out_specs, out_shape = o_spec, o_shape
    else:
        in_specs.append(pl.BlockSpec((seq, KV_B * HD_B), lambda b, i: (b, 0)))
        args.append(v_f32)
        if prev is not None:
            in_specs += [pl.BlockSpec(memory_space=pl.ANY)] * 2
            aliases = {len(args): 1, len(args) + 1: 2}
            args += list(prev)
        new_spec = pl.BlockSpec((None, None, seq, KV_B, HD_B), lambda b, i: (b, slot, 0, 0, 0))
        new_shape = jax.ShapeDtypeStruct((n_batch, n_even, seq, KV_B, HD_B), F32)
        out_specs, out_shape = [o_spec, new_spec, new_spec], [o_shape, new_shape, new_shape]
    n_keys = seq + (PAST_LEN if sample else 0)
    return pl.pallas_call(
        functools.partial(_gqa_kernel, sample=sample, has_prev=prev is not None),
        grid=(n_batch, nq),
        in_specs=in_specs,
        out_specs=out_specs,
        out_shape=out_shape,
        input_output_aliases=aliases,
        scratch_shapes=[pltpu.VMEM((n_keys, KV_B * HD_B), BF16), pltpu.VMEM((KV_B, n_keys, 2 * HD_B), BF16)],
        compiler_params=_params(2),
        name="gqa_sample" if sample else "gqa_prompt",
    )(*args)


MLA_QW = 2 * LANES
MLA_HW = 4 * LANES


def _rotate_pairs(x, cos_t, sin_lo, sin_hi):
    w = x.shape[1]
    return x * cos_t + pltpu.roll(x, ROPE_D // 2, 1) * sin_hi + pltpu.roll(x, w - ROPE_D // 2, 1) * sin_lo


def _mla_kernel(*refs, sample):
    if sample:
        (cq_ref, ckv_ref, kpe_ref, gq_ref, wqb_ref, gkv_ref, wkvb_ref, cckv_ref, ckpe_ref,
         qc_ref, ql_ref, qh_ref, kc_ref, kl_ref, kh_ref, o_ref, kv_s) = refs
    else:
        cq_ref, ckv_ref, kpe_ref, gq_ref, wqb_ref, gkv_ref, wkvb_ref, o_ref, ckvn_ref, kv_s = refs
    qi = pl.program_id(1)
    n_new = ckv_ref.shape[0]
    past = PAST_LEN if sample else 0

    def stage_kv(rows, kv, kpe_block):
        for h in range(H_D):
            kv_s[rows, MLA_HW * h:MLA_HW * h + NOPE_D] = kv[:, 256 * h:256 * h + NOPE_D].astype(BF16)
            kv_s[rows, MLA_HW * h + NOPE_D:MLA_HW * h + MLA_QW] = kpe_block
            kv_s[rows, MLA_HW * h + MLA_QW:MLA_HW * h + MLA_QW + V_D] = kv[:, 256 * h + NOPE_D:256 * (h + 1)].astype(BF16)

    @pl.when(qi == 0)
    def _():
        wkvb = wkvb_ref[...].astype(BF16)
        ckvn = _rms(ckv_ref[...].astype(F32)) * gkv_ref[...]
        if not sample:
            ckvn_ref[...] = ckvn
        kpe = kpe_ref[...]
        if sample:
            ctx_kpe = jnp.concatenate([ckpe_ref[...], jnp.zeros((past, LANES - ROPE_D), F32)], axis=1)
            stage_kv(slice(0, past), _mm(cckv_ref[...].astype(BF16), wkvb), ctx_kpe.astype(BF16))
            kpe = _rotate_pairs(kpe.astype(F32), kc_ref[...], kl_ref[...], kh_ref[...]).astype(BF16)
        stage_kv(slice(past, past + n_new), _mm(ckvn.astype(BF16), wkvb), kpe)
        for h in range(H_D):
            kv_s[:, MLA_HW * h + MLA_QW + V_D:MLA_HW * (h + 1)] = jnp.ones((past + n_new, V_D), BF16)

    q = _mm((_rms(cq_ref[...].astype(F32)) * gq_ref[...]).astype(BF16), wqb_ref[...].astype(BF16))
    q = q * ((NOPE_D + ROPE_D) ** -0.5)
    qs = []
    for h in range(H_D):
        q_h = q[:, MLA_QW * h:MLA_QW * (h + 1)]
        if sample:
            rows = pl.ds(pl.multiple_of(qi * QB, QB), QB)
            q_h = _rotate_pairs(q_h, qc_ref[rows, :], ql_ref[rows, :], qh_ref[rows, :])
        qs.append(q_h.astype(BF16))
    scores = [_nt(qs[h], kv_s[:, MLA_HW * h:MLA_HW * h + MLA_QW]) for h in range(H_D)]
    weights = [jnp.exp(s - jnp.max(s, axis=-1, keepdims=True)).astype(BF16) for s in scores]
    sums = [_mm(weights[h], kv_s[:, MLA_HW * h + MLA_QW:MLA_HW * (h + 1)]) for h in range(H_D)]
    for h in range(H_D):
        o_ref[:, V_D * h:V_D * (h + 1)] = (sums[h][:, :V_D] / sums[h][:, V_D:]).astype(o_ref.dtype)


def _mla(proj, g_q, w_qb, g_kv, w_kvb, n_batch, seq, ctx=None, rope=None):
    sample = ctx is not None
    m = n_batch * seq
    nq = seq // QB
    in_specs = [
        pl.BlockSpec((QB, Q_RANK), lambda b, i: (b * nq + i, 6)),
        pl.BlockSpec((seq, KV_RANK), lambda b, i: (b, 14)),
        pl.BlockSpec((seq, LANES), lambda b, i: (b, 15)),
        pl.BlockSpec((1, Q_RANK), lambda b, i: (0, 0)),
        pl.BlockSpec((Q_RANK, H_D * MLA_QW), lambda b, i: (0, 0)),
        pl.BlockSpec((1, KV_RANK), lambda b, i: (0, 0)),
        pl.BlockSpec((KV_RANK, 1024), lambda b, i: (0, 0)),
    ]
    args = [proj, proj, proj, g_q.reshape(1, Q_RANK), w_qb, g_kv.reshape(1, KV_RANK), w_kvb]
    o_spec = pl.BlockSpec((QB, 512), lambda b, i: (b * nq + i, 0))
    o_shape = jax.ShapeDtypeStruct((m, 512), BF16)
    if sample:
        in_specs += [
            pl.BlockSpec((None, PAST_LEN, KV_RANK), lambda b, i: (b, 0, 0)),
            pl.BlockSpec((None, PAST_LEN, ROPE_D), lambda b, i: (b, 0, 0)),
        ] + [pl.BlockSpec((seq, MLA_QW), lambda b, i: (0, 0))] * 3 + [pl.BlockSpec((seq, LANES), lambda b, i: (0, 0))] * 3
        args += [ctx[0], ctx[1], *rope]
        out_specs, out_shape = o_spec, o_shape
    else:
        out_specs = [o_spec, pl.BlockSpec((seq, KV_RANK), lambda b, i: (b, 0))]
        out_shape = [o_shape, jax.ShapeDtypeStruct((m, KV_RANK), F32)]
    n_keys = seq + (PAST_LEN if sample else 0)
    return pl.pallas_call(
        functools.partial(_mla_kernel, sample=sample),
        grid=(n_batch, nq),
        in_specs=in_specs,
        out_specs=out_specs,
        out_shape=out_shape,
        scratch_shapes=[pltpu.VMEM((n_keys, H_D * MLA_HW), BF16)],
        compiler_params=_params(2),
        name="mla_sample" if sample else "mla_prompt",
    )(*args)


def _dft(table, x):
    return _mm(table.astype(BF16), x.astype(BF16))


def _filter_kernel(z_ref, wf1_ref, bf1_ref, fr_ref, wf2_ref, bf2_ref, wf3_ref, t_ref, dl_ref,
                   c_ref, s_ref, gre_ref, gim_ref):
    n_tok = z_ref.shape[0]
    fr = fr_ref[...]
    hid = jnp.sin(fr * (_mm(z_ref[...].astype(BF16), wf1_ref[...].astype(BF16)) + bf1_ref[...]))
    hid = jnp.sin(fr * (_mm(hid.astype(BF16), wf2_ref[...].astype(BF16)) + bf2_ref[...]))
    filt = _mm(hid.astype(BF16), wf3_ref[...].astype(BF16))
    decay = jnp.exp(-t_ref[...] * dl_ref[...])
    row = lax.broadcasted_iota(jnp.int32, (n_tok, 1), 0)
    h_f = filt[:, :HY_W] * decay
    h_b = jnp.where(row == 0, 0.0, filt[:, HY_W:] * decay)
    p, m = h_f + h_b, h_f - h_b
    g_re = _dft(c_ref[...], p)
    g_im = _dft(s_ref[...], m)
    sign = jnp.where(row % 2 == 0, 1.0, -1.0)
    nyquist = jnp.sum(p * sign, axis=0, keepdims=True)
    g_im = jnp.where(row == 0, nyquist, g_im)
    wk = jnp.where(row == 0, 0.5 / n_tok, 1.0 / n_tok)
    gre_ref[...] = g_re * wk
    gim_ref[...] = g_im * wk


def _filter_spectrum(z, wf1, bf1, freq, wf2, bf2, wf3, t_col, deltas, tabs):
    n_tok = z.shape[0]
    out = jax.ShapeDtypeStruct((n_tok, HY_W), F32)
    return pl.pallas_call(
        _filter_kernel,
        out_shape=[out, out],
        compiler_params=pltpu.CompilerParams(vmem_limit_bytes=VMEM_LIMIT),
        name="hyena_filter",
    )(z, wf1, bf1.reshape(1, FILT_HID), freq.reshape(1, FILT_HID), wf2, bf2.reshape(1, FILT_HID), wf3,
      t_col, deltas, tabs[0], tabs[1])


HY_CT = 256


HY_ROWS = 1024


def _hyena_kernel(u0_ref, u1_ref, u2_ref, w0_ref, w1_ref, w2_ref, b0_ref, b1_ref, b2_ref, skip_ref,
                  gre_ref, gim_ref, cf_ref, sf_ref, stf_ref, o_ref, c_ref, s_ref, st_ref):
    seq = c_ref.shape[0]
    n_rows = u0_ref.shape[0]
    n_seq = n_rows // seq
    pos = lax.broadcasted_iota(jnp.int32, (n_rows, 1), 0) % seq

    @pl.when((pl.program_id(0) == 0) & (pl.program_id(1) == 0))
    def _():
        c_ref[...] = cf_ref[...].astype(BF16)
        s_ref[...] = sf_ref[...].astype(BF16)
        st_ref[...] = stf_ref[...].astype(BF16)

    def short_conv(u_ref, w_ref, b_ref):
        x, w = u_ref[...].astype(F32), w_ref[...]
        prev = jnp.where(pos == 0, 0.0, pltpu.roll(x, 1, 0))
        nxt = jnp.where(pos == seq - 1, 0.0, pltpu.roll(x, n_rows - 1, 0))
        return prev * w[0:1] + x * w[1:2] + nxt * w[2:3] + b_ref[...]

    def side_by_side(a):
        return a if n_seq == 1 else jnp.concatenate([a[s * seq:(s + 1) * seq] for s in range(n_seq)], axis=1)

    def stacked(a):
        ct = a.shape[1] // n_seq
        return a if n_seq == 1 else jnp.concatenate([a[:, s * ct:(s + 1) * ct] for s in range(n_seq)], axis=0)

    x0 = short_conv(u0_ref, w0_ref, b0_ref)
    gv = short_conv(u1_ref, w1_ref, b1_ref) * short_conv(u2_ref, w2_ref, b2_ref)
    sig = side_by_side(gv).astype(BF16)
    u_re = _mm(c_ref[...], sig)
    u_im = _mm(s_ref[...], sig)
    g_re = jnp.concatenate([gre_ref[...]] * n_seq, axis=1)
    g_im = jnp.concatenate([gim_ref[...]] * n_seq, axis=1)
    bin0 = lax.broadcasted_iota(jnp.int32, (seq, 1), 0) == 0
    p_im = u_im * g_im
    y_re = u_re * g_re - jnp.where(bin0, 0.0, p_im)
    y_im = jnp.where(bin0, p_im, u_re * g_im + u_im * g_re)
    y = stacked(_mm(c_ref[...], y_re.astype(BF16)) + _mm(st_ref[...], y_im.astype(BF16)))
    o_ref[...] = (x0 * (y + gv * skip_ref[...])).astype(o_ref.dtype)


def _hyena(proj, w_conv, b_conv, skip, g_re, g_im, tabs, n_batch, seq):
    nct = HY_W // HY_CT
    u_specs = [pl.BlockSpec((HY_ROWS, HY_CT), functools.partial(lambda b, c, g: (b, g * nct + c), g=g)) for g in range(3)]
    w_specs = [pl.BlockSpec((3, HY_CT), functools.partial(lambda b, c, g: (0, g * nct + c), g=g)) for g in range(3)]
    b_specs = [pl.BlockSpec((1, HY_CT), functools.partial(lambda b, c, g: (0, g * nct + c), g=g)) for g in range(3)]
    tab_spec = pl.BlockSpec((seq, seq), lambda b, c: (0, 0))
    return pl.pallas_call(
        _hyena_kernel,
        grid=(n_batch * seq // HY_ROWS, nct),
        in_specs=u_specs + w_specs + b_specs + [
            pl.BlockSpec((1, HY_CT), lambda b, c: (0, c)),
            pl.BlockSpec((seq, HY_CT), lambda b, c: (0, c)),
            pl.BlockSpec((seq, HY_CT), lambda b, c: (0, c)),
        ] + [tab_spec] * 3,
        out_specs=pl.BlockSpec((HY_ROWS, HY_CT), lambda b, c: (b, c)),
        out_shape=jax.ShapeDtypeStruct((n_batch * seq, HY_W), BF16),
        scratch_shapes=[pltpu.VMEM((seq, seq), BF16)] * 3,
        compiler_params=_params(2),
        name="hyena_conv",
    )(proj, proj, proj, w_conv, w_conv, w_conv, b_conv, b_conv, b_conv, skip.reshape(1, HY_W), g_re, g_im, *tabs)


def _dft_tables(n_tok):
    k = np.arange(n_tok)[:, None]
    s = np.arange(n_tok)[None, :]
    ang = ((k * s) % (2 * n_tok)) * (np.pi / n_tok)
    cos_t = np.cos(ang)
    sin_f = np.where(k == 0, np.where(s % 2 == 0, 1.0, -1.0), -np.sin(ang))
    return [jnp.asarray(t, F32) for t in (cos_t, sin_f, sin_f.T)]


GLA_LEVELS = (32, 16, 8, 4, 2, 1)
GLA_SAFE_DECAY = 60.0
GLA_GROUP = 2


def _gla_constants():
    c = GLA_CHUNK
    idx = np.arange(c)
    i, t = idx[:, None], idx[None, :]
    masks = []
    for s in GLA_LEVELS:
        upper = (idx % (2 * s)) >= s
        masks.append(((i // (2 * s)) == (t // (2 * s))) & upper[:, None] & (~upper)[None, :])
    masks.append(i == t)
    tri = t <= i
    fwd_m = np.stack([np.tile(m, (H_A, 1)) for m in masks]).astype(np.float32)
    bwd_m = np.stack([np.tile(m[::-1, ::-1], (H_A, 1)) for m in masks]).astype(np.float32)
    head_of_row = np.repeat(np.arange(H_A), c)[:, None]
    head_of_lane = np.repeat(np.arange(H_A), DK_A)[None, :]
    head_mask = head_of_row == head_of_lane
    return (jnp.asarray(tri, BF16), jnp.asarray(tri[::-1, ::-1], BF16), jnp.asarray(fwd_m), jnp.asarray(bwd_m),
            jnp.asarray(head_mask, BF16))


def _pair_reference(b, s, backward, row):
    c = GLA_CHUNK
    ref = s if backward else s - 1
    if 2 * s >= 8:
        pieces = [jnp.broadcast_to(b[p * 2 * s + ref:p * 2 * s + ref + 1, :], (2 * s, b.shape[1]))
                  for p in range(c // (2 * s))]
        return pieces[0] if len(pieces) == 1 else jnp.concatenate(pieces, axis=0)
    pos = row % (2 * s)
    out = None
    for o in range(2 * s):
        d = ref - o
        shifted = b if d == 0 else pltpu.roll(b, (-d) % c, 0)
        out = shifted if out is None else jnp.where(pos == o, shifted, out)
    return out


def _chunk_log_decay(la, t_ref):
    l1 = la.astype(BF16)
    r1 = la - l1.astype(F32)
    l2 = r1.astype(BF16)
    l3 = (r1 - l2.astype(F32)).astype(BF16)
    tmat = t_ref[...]
    return _mm(tmat, l1) + _mm(tmat, l2) + _mm(tmat, l3)


def _stack_heads(a, hm):
    ab = a.astype(BF16)
    return jnp.concatenate([ab] * H_A, axis=0) * hm


def _state_terms(k, v, b, b_last):
    c = GLA_CHUNK
    k_rest = (k * jnp.exp(b_last - b)).T
    carry = jnp.broadcast_to(jnp.exp(b_last), (2 * c, b.shape[1])).T
    return k_rest.astype(BF16), carry


def _gla_chunk(q, k, v, la, t_ref, m_ref, hm, s_ref, backward):
    c = GLA_CHUNK
    b = _chunk_log_decay(la, t_ref)
    row = lax.broadcasted_iota(jnp.int32, (c, 1), 0)
    last = 0 if backward else c - 1
    b_last = b[last:last + 1, :]
    scores = _nt(_stack_heads(q, hm), k.astype(BF16)) * m_ref[len(GLA_LEVELS)]
    for lvl, s in enumerate(GLA_LEVELS):
        is_query = (row % (2 * s) < s) if backward else (row % (2 * s) >= s)
        delta = b - _pair_reference(b, s, backward, row)
        x = jnp.exp(jnp.where(is_query, delta, -delta))
        scores = scores + _nt(_stack_heads(q * x, hm), (k * x).astype(BF16)) * m_ref[lvl]
    scores = scores.astype(BF16)
    state = s_ref[...]
    inter = _mm(_stack_heads(q * jnp.exp(b), hm), state.astype(BF16))
    k_rest, carry = _state_terms(k, v, b, b_last)
    outs = []
    for h in range(H_A):
        rows = slice(c * h, c * (h + 1))
        v_h = v[:, DV_A * h:DV_A * (h + 1)]
        outs.append(_mm(scores[rows], v_h) + inter[rows])
        s_ref[rows, :] = state[rows] * carry[rows] + _mm(k_rest[rows], v_h)
    return jnp.concatenate(outs, axis=1)


def _gla_local(items, hm):
    c = GLA_CHUNK
    bs = [_chunk_log_decay(la, t_ref) for _, _, _, la, t_ref, _, _ in items]
    b_lasts = [b[(0 if it[6] else c - 1):(0 if it[6] else c - 1) + 1, :] for b, it in zip(bs, items)]
    q_decayed = [_stack_heads(it[0] * jnp.exp(b), hm) for it, b in zip(items, bs)]
    k_grown = [(it[1] * jnp.exp(-b)).astype(BF16) for it, b in zip(items, bs)]
    raw = [_nt(qd, kg) for qd, kg in zip(q_decayed, k_grown)]
    scores = [(r * it[5]).astype(BF16) for r, it in zip(raw, items)]
    terms = [_state_terms(it[1], it[2], b, bl) for it, b, bl in zip(items, bs, b_lasts)]
    out = []
    for it, sc, (k_rest, carry), qd in zip(items, scores, terms, q_decayed):
        v = it[2]
        heads = [(slice(c * h, c * (h + 1)), v[:, DV_A * h:DV_A * (h + 1)]) for h in range(H_A)]
        intra = jnp.concatenate([_mm(sc[rows], v_h) for rows, v_h in heads], axis=1)
        incr = jnp.concatenate([_mm(k_rest[rows], v_h) for rows, v_h in heads], axis=0)
        out.append((intra, qd, incr, carry))
    return out


def _gla_kernel(*refs, sample):
    if sample:
        (x_ref, z_ref, wf_ref, bf_ref, wb_ref, bb_ref, tf_ref, tb_ref, mf_ref, mb_ref, hm_ref, gn_ref, sf0_ref, sb0_ref,
         o_ref, la_f, la_b, o_f, o_b, s_f, s_b, qd_f, qd_b, ds_f, ds_b, cr_f, cr_b) = refs
    else:
        (x_ref, z_ref, wf_ref, bf_ref, wb_ref, bb_ref, tf_ref, tb_ref, mf_ref, mb_ref, hm_ref, gn_ref,
         o_ref, sf_out, sb_out, la_f, la_b, o_f, o_b, s_f, s_b, qd_f, qd_b, ds_f, ds_b, cr_f, cr_b) = refs
    n_tok = x_ref.shape[0]
    n_chunks = n_tok // GLA_CHUNK
    hk, hv = H_A * DK_A, H_A * DV_A
    zb = z_ref[...].astype(BF16)

    def log_sigmoid(t):
        return jnp.minimum(t, 0.0) - jnp.log(1.0 + jnp.exp(-jnp.abs(t)))

    la_f[...] = log_sigmoid(_mm(zb, wf_ref[...].astype(BF16)) + bf_ref[...]) / GLA_TAU
    la_b[...] = log_sigmoid(_mm(zb, wb_ref[...].astype(BF16)) + bb_ref[...]) / GLA_TAU
    if sample:
        s_f[...] = sf0_ref[...]
        s_b[...] = sb0_ref[...]
    else:
        s_f[...] = jnp.zeros_like(s_f)
        s_b[...] = jnp.zeros_like(s_b)
    hm = hm_ref[...]

    fwd = (la_f, tf_ref, mf_ref, s_f, o_f, qd_f, ds_f, cr_f, False)
    bwd = (la_b, tb_ref, mb_ref, s_b, o_b, qd_b, ds_b, cr_b, True)
    tri_f = jnp.sum(mf_ref[...], axis=0)
    tri_b = jnp.sum(mb_ref[...], axis=0)

    def chunk_rows(ci, backward):
        cidx = n_chunks - 1 - ci if backward else ci
        return cidx, pl.ds(pl.multiple_of(cidx * GLA_CHUNK, GLA_CHUNK), GLA_CHUNK)

    def load_qkv(rows):
        q = x_ref[rows, 0:hk].astype(F32) * (DK_A ** -0.5)
        return q, x_ref[rows, hk:2 * hk].astype(F32), x_ref[rows, 2 * hk:2 * hk + hv]

    def safe_step(ci, carry):
        for la_ref, t_ref, m_ref, s_ref, out_ref, _, _, _, backward in (fwd, bwd):
            _, rows = chunk_rows(ci, backward)
            out_ref[rows, :] = _gla_chunk(*load_qkv(rows), la_ref[rows, :], t_ref, m_ref, hm, s_ref, backward)
        return carry

    def local_step(gi, carry):
        items, dests = [], []
        for (la_ref, t_ref, _, _, out_ref, qd_ref, ds_ref, cr_ref, backward), tri in ((fwd, tri_f), (bwd, tri_b)):
            for u in range(GLA_GROUP):
                cidx, rows = chunk_rows(gi * GLA_GROUP + u, backward)
                items.append((*load_qkv(rows), la_ref[rows, :], t_ref, tri, backward))
                dests.append((out_ref, rows, qd_ref, ds_ref, cr_ref, cidx))
        for (out_ref, rows, qd_ref, ds_ref, cr_ref, cidx), (intra, qd, incr, factor) in zip(dests, _gla_local(items, hm)):
            out_ref[rows, :] = intra
            qd_ref[cidx] = qd
            ds_ref[cidx] = incr
            cr_ref[cidx] = factor
        return carry

    def scan_step(ci, carry):
        for _, _, _, s_ref, out_ref, qd_ref, ds_ref, cr_ref, backward in (fwd, bwd):
            cidx, rows = chunk_rows(ci, backward)
            state = s_ref[...]
            inter = _mm(qd_ref[cidx], state.astype(BF16))
            out_ref[rows, :] += jnp.concatenate(
                [inter[GLA_CHUNK * h:GLA_CHUNK * (h + 1)] for h in range(H_A)], axis=1)
            s_ref[...] = state * cr_ref[cidx] + ds_ref[cidx]
        return carry

    chunk_sums = [jnp.sum(ref[...].reshape(n_chunks, GLA_CHUNK, hk), axis=1) for ref in (la_f, la_b)]
    mild = jnp.minimum(jnp.min(chunk_sums[0]), jnp.min(chunk_sums[1])) > -GLA_SAFE_DECAY

    @pl.when(mild)
    def _():
        lax.fori_loop(0, n_chunks // GLA_GROUP, local_step, 0, unroll=2)
        lax.fori_loop(0, n_chunks, scan_step, 0, unroll=2)

    @pl.when(jnp.logical_not(mild))
    def _():
        lax.fori_loop(0, n_chunks, safe_step, 0)
    if not sample:
        sf_out[...] = s_f[...]
        sb_out[...] = s_b[...]
    gain = gn_ref[...]
    for h in range(H_A):
        cols = slice(DV_A * h, DV_A * (h + 1))
        r = x_ref[:, 2 * hk + hv + DV_A * h:2 * hk + hv + DV_A * (h + 1)].astype(F32)
        o_ref[:, cols] = (_rms(o_f[:, cols] + o_b[:, cols]) * gain * (r * jax.nn.sigmoid(r))).astype(o_ref.dtype)


def _gla(proj, w_gf, b_gf, w_gb, b_gb, g_norm, consts, n_batch, seq, ctx=None):
    sample = ctx is not None
    hk, hv = H_A * DK_A, H_A * DV_A
    n_ch = seq // GLA_CHUNK
    full = lambda shape: pl.BlockSpec(shape, lambda b: (0,) * len(shape))
    in_specs = [
        pl.BlockSpec((seq, 2 * hk + 2 * hv), lambda b: (b, 0)),
        pl.BlockSpec((seq, LANES), lambda b: (b, EVEN_W // LANES - 1)),
        full((LANES, hk)), full((1, hk)), full((LANES, hk)), full((1, hk)),
        full(consts[0].shape), full(consts[1].shape), full(consts[2].shape), full(consts[3].shape), full(consts[4].shape),
        full((1, DV_A)),
    ]
    args = [proj, proj, w_gf, b_gf.reshape(1, hk), w_gb, b_gb.reshape(1, hk), *consts, g_norm.reshape(1, DV_A)]
    o_spec = pl.BlockSpec((seq, hv), lambda b: (b, 0))
    o_shape = jax.ShapeDtypeStruct((n_batch * seq, hv), BF16)
    st_spec = pl.BlockSpec((None, hk, DV_A), lambda b: (b, 0, 0))
    if sample:
        in_specs += [st_spec, st_spec]
        args += [ctx[0], ctx[1]]
        out_specs, out_shape = o_spec, o_shape
    else:
        st_shape = jax.ShapeDtypeStruct((n_batch, hk, DV_A), F32)
        out_specs, out_shape = [o_spec, st_spec, st_spec], [o_shape, st_shape, st_shape]
    return pl.pallas_call(
        functools.partial(_gla_kernel, sample=sample),
        grid=(n_batch,),
        in_specs=in_specs,
        out_specs=out_specs,
        out_shape=out_shape,
        scratch_shapes=[pltpu.VMEM((seq, hk), F32), pltpu.VMEM((seq, hk), F32),
                        pltpu.VMEM((seq, hv), F32), pltpu.VMEM((seq, hv), F32),
                        pltpu.VMEM((hk, DV_A), F32), pltpu.VMEM((hk, DV_A), F32),
                        pltpu.VMEM((n_ch, H_A * GLA_CHUNK, hk), BF16), pltpu.VMEM((n_ch, H_A * GLA_CHUNK, hk), BF16),
                        pltpu.VMEM((n_ch, hk, DV_A), F32), pltpu.VMEM((n_ch, hk, DV_A), F32),
                        pltpu.VMEM((n_ch, hk, DV_A), F32), pltpu.VMEM((n_ch, hk, DV_A), F32)],
        compiler_params=_params(1),
        name="gla_sample" if sample else "gla_prompt",
    )(*args)


def _axial_rope(n_tokens, dim):
    rows = n_tokens // GRID_W
    row = np.repeat(np.arange(rows), GRID_W).astype(np.float64)
    col = np.tile(np.arange(GRID_W), rows).astype(np.float64)
    n_freq = dim // 4
    inv = ROPE_THETA ** (-np.arange(n_freq) / n_freq)
    ang = np.concatenate([row[:, None] * inv, col[:, None] * inv], axis=-1)
    return np.cos(ang).astype(np.float32), np.sin(ang).astype(np.float32)


def _filter_features(n_tokens):
    t = np.linspace(0.0, 1.0, n_tokens)[:, None]
    w = 2.0 * np.pi * np.arange(n_tokens)[:, None] / n_tokens
    f = np.linspace(1e-4, FILT_BANDS - 1, FILT_BANDS)[None, :]
    z = np.concatenate([t, np.cos(f * w), -np.sin(f * w)], axis=-1)
    z = np.pad(z, ((0, 0), (0, LANES - FILT_EMB)))
    return jnp.asarray(z, F32), jnp.asarray(t, F32)


_QB_ZERO = H_D * (NOPE_D + ROPE_D)
_QB_PERM = np.array([(NOPE_D + ROPE_D) * (p // MLA_QW) + p % MLA_QW if p % MLA_QW < NOPE_D + ROPE_D else _QB_ZERO
                     for p in range(H_D * MLA_QW)])


def _mla_rope_tables(cos_d, sin_d):
    n, half = cos_d.shape
    zeros = np.zeros((n, half), np.float32)

    def lanes(pre, width):
        pad = np.zeros((n, width - pre.shape[1] - 2 * half), np.float32)
        build = lambda first, second, lead: np.concatenate([lead, first, second, pad], axis=1)
        return (build(cos_d, cos_d, pre), build(-sin_d, zeros, 0 * pre), build(zeros, sin_d, 0 * pre))

    q_tabs = lanes(np.ones((n, NOPE_D), np.float32), MLA_QW)
    k_tabs = lanes(np.zeros((n, 0), np.float32), LANES)
    return tuple(jnp.asarray(t) for t in q_tabs + k_tabs)

EVEN_ROW_GROUPS = ((0, 0, 1536), (1568, 1536, 1024), (1536, EVEN_W - 2 * GATE_RANK, 2 * GATE_RANK))
ODD_ROW_GROUPS = ((0, 0, 1984),)
EVEN_KEEP = (2304, 256)
ODD_KEEP = (1920, LANES)


def kernel(x_prompt, x_sample, state_gla_fwd, state_gla_bwd, cache_gqa_k, cache_gqa_v, cache_mla_ckv, cache_mla_kpe, c, c_ctx, w_mod, b_mod, w_in_even, w_gla_gate_f, b_gla_gate_f, w_gla_gate_b, b_gla_gate_b, g_gla_norm, g_gqa_q, g_gqa_k, w_out_even, w_in_odd, w_hy_conv, b_hy_conv, hy_skip, w_filt1, b_filt1, filt_freq, w_filt2, b_filt2, w_filt3, g_mla_q, w_mla_qb, g_mla_kv, w_mla_kvb, w_out_odd, w_ffn_in, w_ffn_out, g_final):
    n_c, n_s = BATCH * SEQ, DEC_BATCH * DEC_SEQ
    cvec = jnp.concatenate([c_ctx[None, :], c, jnp.zeros((8 - 1 - DEC_BATCH, D_MODEL), F32)], axis=0)
    mod = _modulation(cvec, w_mod, b_mod)
    xc = x_prompt.reshape(n_c, D_MODEL)
    xs = x_sample.reshape(n_s, D_MODEL)
    rows_c, rows_s = (0, 0), (1, DEC_SEQ // MOD_ROWS)

    gla_consts = _gla_constants()
    cos_b, sin_b = _axial_rope(DEC_SEQ, HD_B)
    rope_b = (jnp.asarray(np.concatenate([cos_b, cos_b], axis=1)), jnp.asarray(np.concatenate([-sin_b, sin_b], axis=1)))
    cos_d, sin_d = _axial_rope(DEC_SEQ, ROPE_D)
    rope_d = _mla_rope_tables(cos_d, sin_d)
    w_qb_all = jnp.pad(w_mla_qb, ((0, 0), (0, 0), (0, 1)))[:, :, _QB_PERM]
    tabs_c, tabs_s = _dft_tables(SEQ), _dft_tables(DEC_SEQ)
    z_c, t_c = _filter_features(SEQ)
    z_s, t_s = _filter_features(DEC_SEQ)
    deltas = jnp.asarray(np.abs(np.linspace(HY_MIN_DECAY, HY_MAX_DECAY, HY_W))[None, :], F32)

    wt_even = jnp.swapaxes(w_in_even, 1, 2)
    wt_odd = jnp.swapaxes(w_in_odd, 1, 2)

    st_gf, st_gb, st_ckv, st_kpe = [], [], [], []
    new_kv = None
    for i in range(DEPTH):
        j = i // 2
        if i % 2 == 0:
            z0 = LANES - 2 * GATE_RANK
            pad_f = jnp.zeros((LANES, H_A * DK_A), F32).at[z0:z0 + GATE_RANK].set(w_gla_gate_f[j])
            pad_b = jnp.zeros((LANES, H_A * DK_A), F32).at[z0 + GATE_RANK:LANES].set(w_gla_gate_b[j])
            pc, v_new, ps = _in_proj(xc, xs, mod, i, wt_even, j, EVEN_ROW_GROUPS, EVEN_W, EVEN_KEEP)
            gate_args = (pad_f, b_gla_gate_f[j], pad_b, b_gla_gate_b[j], g_gla_norm[j], gla_consts)
            a_c, s_f, s_b = _gla(pc, *gate_args, BATCH, SEQ)
            ctx_a = (state_gla_fwd[:, j].reshape(DEC_BATCH, H_A * DK_A, DV_A),
                     state_gla_bwd[:, j].reshape(DEC_BATCH, H_A * DK_A, DV_A))
            a_s = _gla(ps, *gate_args, DEC_BATCH, DEC_SEQ, ctx=ctx_a)
            b_c, *new_kv = _gqa(pc, g_gqa_q[j], g_gqa_k[j], BATCH, SEQ, v_f32=v_new, slot=j, prev=new_kv)
            b_s = _gqa(ps, g_gqa_q[j], g_gqa_k[j], DEC_BATCH, DEC_SEQ, ctx=(cache_gqa_k, cache_gqa_v), rope=rope_b, slot=j)
            w_out = w_out_even
            st_gf.append(s_f.reshape(BATCH, H_A, DK_A, DV_A))
            st_gb.append(s_b.reshape(BATCH, H_A, DK_A, DV_A))
        else:
            pc, kpe_new, ps = _in_proj(xc, xs, mod, i, wt_odd, j, ODD_ROW_GROUPS, ODD_W, ODD_KEEP)
            wf1 = jnp.pad(w_filt1[j], ((0, LANES - FILT_EMB), (0, 0)))
            filt_args = (wf1, b_filt1[j], filt_freq[j], w_filt2[j], b_filt2[j], w_filt3[j])
            g_c = _filter_spectrum(z_c, *filt_args, t_c, deltas, tabs_c)
            g_s = _filter_spectrum(z_s, *filt_args, t_s, deltas, tabs_s)
            b_conv = b_hy_conv[j].reshape(1, 3 * HY_W)
            a_c = _hyena(pc, w_hy_conv[j], b_conv, hy_skip[j], g_c[0], g_c[1], tabs_c, BATCH, SEQ)
            a_s = _hyena(ps, w_hy_conv[j], b_conv, hy_skip[j], g_s[0], g_s[1], tabs_s, DEC_BATCH, DEC_SEQ)
            w_qb = w_qb_all[j]
            b_c, ckv_norm = _mla(pc, g_mla_q[j], w_qb, g_mla_kv[j], w_mla_kvb[j], BATCH, SEQ)
            b_s = _mla(ps, g_mla_q[j], w_qb, g_mla_kv[j], w_mla_kvb[j], DEC_BATCH, DEC_SEQ,
                       ctx=(cache_mla_ckv[:, j], cache_mla_kpe[:, j]), rope=rope_d)
            w_out = w_out_odd
            st_ckv.append(ckv_norm.reshape(BATCH, SEQ, KV_RANK))
            st_kpe.append(kpe_new[:, :ROPE_D].reshape(BATCH, SEQ, ROPE_D))
        xc, xs = _out_proj([a_c, b_c], [a_s, b_s], w_out, j, xc, xs, mod, i, 2)
        last = g_final if i == DEPTH - 1 else None
        xc = _ffn(xc, mod, i, w_ffn_in, w_ffn_out, *rows_c, final_gain=last)
        xs = _ffn(xs, mod, i, w_ffn_in, w_ffn_out, *rows_s, final_gain=last)
    y_prompt = xc.reshape(BATCH, SEQ, D_MODEL)
    y_sample = xs.reshape(DEC_BATCH, DEC_SEQ, D_MODEL)
    return (y_prompt, y_sample, jnp.stack(st_gf, axis=1), jnp.stack(st_gb, axis=1), new_kv[0], new_kv[1],
            jnp.stack(st_ckv, axis=1), jnp.stack(st_kpe, axis=1))
```

```python
import functools
import math

import numpy as np
import jax
import jax.numpy as jnp
from jax import lax
from jax.experimental import pallas as pl
from jax.experimental.pallas import tpu as pltpu

F32 = jnp.float32
BF16 = jnp.bfloat16

D_MODEL = 1024
BATCH, SEQ = 16, 256
DEC_BATCH, DEC_SEQ = 2, 1024
DEPTH = 4
PAST_LEN = 512
GRID_W = 64
HALF_W = D_MODEL // 2
H_A, DV_A, DK_A = 4, 128, 64
GATE_RANK = 16
GLA_TAU = 16.0
GLA_CHUNK = 64
HD_B, H_B, KV_B = 128, 4, 2
HY_W = HALF_W
FILT_EMB, FILT_HID = 33, 64
FILT_BANDS = (FILT_EMB - 1) // 2
HY_MIN_DECAY = math.log(1e-2) / 1.5
HY_MAX_DECAY = math.log(1e-2) / 0.3
H_D, V_D, NOPE_D, ROPE_D = 4, 128, 128, 64
Q_RANK, KV_RANK = 256, 128
FFN_H = 2816
ROPE_THETA = 10000.0
EPS = 1e-6

LANES = 128
VMEM_LIMIT = 56 * 1024 * 1024

MOD_ROWS = 1024
TM = 1024
TM_IN = 512
TM_FFN = 2048
EVEN_W = 2688
ODD_W = 2048
FFN_TN = 256
QB = 256


def _params(n_grid):
    return pltpu.CompilerParams(dimension_semantics=("arbitrary",) * n_grid, vmem_limit_bytes=VMEM_LIMIT)


def _nt(a, b):
    return lax.dot_general(a, b, (((1,), (1,)), ((), ())), preferred_element_type=F32)


def _mm(a, b):
    return jnp.dot(a, b, preferred_element_type=F32)


def _rms(x):
    return x * lax.rsqrt(jnp.mean(x * x, axis=-1, keepdims=True) + EPS)


def _mod_kernel(c_ref, w_ref, b_ref, o_ref):
    cv = c_ref[...]
    s = cv * jax.nn.sigmoid(cv)
    o_ref[...] = _mm(s.astype(BF16), w_ref[...].astype(BF16)) + b_ref[...]


def _modulation(cvec, w_mod, b_mod):
    return pl.pallas_call(
        _mod_kernel,
        grid=(DEPTH, 6),
        in_specs=[
            pl.BlockSpec((8, D_MODEL), lambda l, n: (0, 0)),
            pl.BlockSpec((None, D_MODEL, D_MODEL), lambda l, n: (l, 0, n)),
            pl.BlockSpec((None, 1, D_MODEL), lambda l, n: (l, 0, n)),
        ],
        out_specs=pl.BlockSpec((None, None, 8, D_MODEL), lambda l, n: (l, n, 0, 0)),
        out_shape=jax.ShapeDtypeStruct((DEPTH, 6, 8, D_MODEL), F32),
        compiler_params=_params(2),
        name="adaln_mod",
    )(cvec, w_mod, b_mod.reshape(DEPTH, 1, 6 * D_MODEL))


def _mod_row(row0, rstep, tile_rows, sub):
    if tile_rows >= MOD_ROWS:
        return row0 + rstep * (pl.program_id(0) * (tile_rows // MOD_ROWS) + sub)
    return row0 + rstep * (pl.program_id(0) // (MOD_ROWS // tile_rows))


def _in_proj_kernel(xc_ref, xs_ref, sh_ref, sc_ref, wt_ref, oc_ref, keep_ref, os_ref, wb_ref, *, row_groups, keep, n_c):
    i = pl.program_id(0)

    @pl.when(i == 0)
    def _():
        wb_ref[...] = jnp.zeros_like(wb_ref)
        for src, dst, size in row_groups:
            wb_ref[dst:dst + size, :] = wt_ref[src:src + size, :].astype(BF16)

    def project(x_ref, g):
        h = (_rms(x_ref[...]) * (1.0 + sc_ref[pl.ds(g, 1), :]) + sh_ref[pl.ds(g, 1), :]).astype(BF16)
        return _nt(h, wb_ref[...])

    @pl.when(i < n_c)
    def _():
        y = project(xc_ref, 0)
        oc_ref[...] = y.astype(oc_ref.dtype)
        keep_ref[...] = y[:, keep[0]:keep[0] + keep[1]]

    @pl.when(i >= n_c)
    def _():
        os_ref[...] = project(xs_ref, 1 + (i - n_c) // (MOD_ROWS // TM_IN)).astype(os_ref.dtype)


def _in_proj(xc, xs, mod, layer, wt, w_layer, row_groups, n, keep):
    n_c, n_s = xc.shape[0] // TM_IN, xs.shape[0] // TM_IN
    c_idx = lambda i: (jnp.minimum(i, n_c - 1), 0)
    s_idx = lambda i: (jnp.maximum(i - n_c, 0), 0)
    return pl.pallas_call(
        functools.partial(_in_proj_kernel, row_groups=row_groups, keep=keep, n_c=n_c),
        grid=(n_c + n_s,),
        in_specs=[pl.BlockSpec((TM_IN, D_MODEL), c_idx), pl.BlockSpec((TM_IN, D_MODEL), s_idx),
                  pl.BlockSpec((None, None, 8, D_MODEL), lambda i: (layer, 0, 0, 0)),
                  pl.BlockSpec((None, None, 8, D_MODEL), lambda i: (layer, 1, 0, 0)),
                  pl.BlockSpec((None, wt.shape[1], D_MODEL), lambda i: (w_layer, 0, 0), pipeline_mode=pl.Buffered(1))],
        out_specs=[pl.BlockSpec((TM_IN, n), c_idx), pl.BlockSpec((TM_IN, keep[1]), c_idx), pl.BlockSpec((TM_IN, n), s_idx)],
        out_shape=[jax.ShapeDtypeStruct((xc.shape[0], n), BF16), jax.ShapeDtypeStruct((xc.shape[0], keep[1]), F32),
                   jax.ShapeDtypeStruct((xs.shape[0], n), BF16)],
        scratch_shapes=[pltpu.VMEM((n, D_MODEL), BF16)],
        compiler_params=_params(1),
        name="norm_mod_proj",
    )(xc, xs, mod, mod, wt)


def _ffn_kernel(x_ref, sh_ref, sc_ref, gate_ref, wg_ref, wu_ref, wd_ref, *refs, row0, rstep, final):
    (gf_ref, o_ref, h_ref) = refs if final else (None,) + refs
    n_sub = x_ref.shape[0] // MOD_ROWS
    subs = [(slice(s * MOD_ROWS, (s + 1) * MOD_ROWS), _mod_row(row0, rstep, x_ref.shape[0], s)) for s in range(n_sub)]

    @pl.when(pl.program_id(1) == 0)
    def _():
        for rows, g in subs:
            x = x_ref[rows, :]
            o_ref[rows, :] = x
            h_ref[rows, :] = (_rms(x) * (1.0 + sc_ref[pl.ds(g, 1), :]) + sh_ref[pl.ds(g, 1), :]).astype(BF16)

    wg = wg_ref[...].astype(BF16)
    wu = wu_ref[...].astype(BF16)
    wd = wd_ref[...].astype(BF16)
    for rows, g in subs:
        h = h_ref[rows, :]
        a = _mm(h, wg)
        act = (a * jax.nn.sigmoid(a) * _mm(h, wu)).astype(BF16)
        o_ref[rows, :] += gate_ref[pl.ds(g, 1), :] * _mm(act, wd)

    if final:
        @pl.when(pl.program_id(1) == pl.num_programs(1) - 1)
        def _():
            for rows, _ in subs:
                o_ref[rows, :] = _rms(o_ref[rows, :]) * gf_ref[...]


def _ffn(x, mod, layer, w_in, w_out, row0, rstep, final_gain=None):
    m = x.shape[0]
    nj = FFN_H // FFN_TN
    mod_spec = lambda k: pl.BlockSpec((None, None, 8, D_MODEL), lambda i, j: (layer, k, 0, 0))
    final = final_gain is not None
    extra_specs = [pl.BlockSpec((1, D_MODEL), lambda i, j: (0, 0))] if final else []
    extra_args = [final_gain.reshape(1, D_MODEL)] if final else []
    return pl.pallas_call(
        functools.partial(_ffn_kernel, row0=row0, rstep=rstep, final=final),
        grid=(m // TM_FFN, nj),
        in_specs=[pl.BlockSpec((TM_FFN, D_MODEL), lambda i, j: (i, 0)), mod_spec(3), mod_spec(4), mod_spec(5),
                  pl.BlockSpec((None, D_MODEL, FFN_TN), lambda i, j: (layer, 0, j)),
                  pl.BlockSpec((None, D_MODEL, FFN_TN), lambda i, j: (layer, 0, j + nj)),
                  pl.BlockSpec((None, FFN_TN, D_MODEL), lambda i, j: (layer, j, 0))] + extra_specs,
        out_specs=pl.BlockSpec((TM_FFN, D_MODEL), lambda i, j: (i, 0)),
        out_shape=jax.ShapeDtypeStruct((m, D_MODEL), F32),
        scratch_shapes=[pltpu.VMEM((TM_FFN, D_MODEL), BF16)],
        compiler_params=_params(2),
        name="ffn_residual",
    )(x, mod, mod, mod, w_in, w_in, w_out, *extra_args)


def _proj_res_kernel(ac0_ref, ac1_ref, as0_ref, as1_ref, w0_ref, w1_ref, xc_ref, xs_ref, gate_ref, oc_ref, os_ref, *, n_c):
    i = pl.program_id(0)

    def mix(a0_ref, a1_ref, x_ref, g):
        acc = _mm(a0_ref[...], w0_ref[...].astype(BF16)) + _mm(a1_ref[...], w1_ref[...].astype(BF16))
        return x_ref[...] + gate_ref[pl.ds(g, 1), :] * acc

    @pl.when(i < n_c)
    def _():
        oc_ref[...] = mix(ac0_ref, ac1_ref, xc_ref, 0)

    @pl.when(i >= n_c)
    def _():
        os_ref[...] = mix(as0_ref, as1_ref, xs_ref, 1 + (i - n_c))


def _out_proj(acts_c, acts_s, w, w_layer, xc, xs, mod, layer, k_gate):
    n_c, n_s = xc.shape[0] // TM, xs.shape[0] // TM
    kw = acts_c[0].shape[1]
    c_idx = lambda i: (jnp.minimum(i, n_c - 1), 0)
    s_idx = lambda i: (jnp.maximum(i - n_c, 0), 0)
    w_specs = [pl.BlockSpec((None, kw, D_MODEL), functools.partial(lambda i, p: (w_layer, p, 0), p=p),
                            pipeline_mode=pl.Buffered(1)) for p in range(2)]
    return pl.pallas_call(
        functools.partial(_proj_res_kernel, n_c=n_c),
        grid=(n_c + n_s,),
        in_specs=[pl.BlockSpec((TM, kw), c_idx)] * 2 + [pl.BlockSpec((TM, kw), s_idx)] * 2 + w_specs + [
            pl.BlockSpec((TM, D_MODEL), c_idx), pl.BlockSpec((TM, D_MODEL), s_idx),
            pl.BlockSpec((None, None, 8, D_MODEL), lambda i: (layer, k_gate, 0, 0)),
        ],
        out_specs=[pl.BlockSpec((TM, D_MODEL), c_idx), pl.BlockSpec((TM, D_MODEL), s_idx)],
        out_shape=[jax.ShapeDtypeStruct(xc.shape, F32), jax.ShapeDtypeStruct(xs.shape, F32)],
        compiler_params=_params(1),
        name="out_proj_residual",
    )(*acts_c, *acts_s, w, w, xc, xs, mod)


def _gqa_kernel(*refs, sample, has_prev=False, slot=0):
    if sample:
        q_ref, k_ref, v_ref, gq_ref, gk_ref, ck_ref, cv_ref, cos_ref, sin_ref, o_ref, kb_ref, vb_ref = refs
    else:
        n_in = 8 if has_prev else 6
        q_ref, k_ref, v_ref, gq_ref, gk_ref, vf_ref = refs[:6]
        o_ref, kc_ref, vc_ref, kb_ref, vb_ref = refs[n_in:]
        if not has_prev:
            stacks = (kc_ref, vc_ref)
            kc_ref, vc_ref = kc_ref.at[slot], vc_ref.at[slot]
    qi = pl.program_id(1)
    n_new = k_ref.shape[0]
    past = PAST_LEN if sample else 0
    rep = H_B // KV_B

    @pl.when(qi == 0)
    def _():
        if not (sample or has_prev):
            for ref in stacks:
                for other in range(ref.shape[0]):
                    if other != slot:
                        ref[other] = jnp.zeros(ref.shape[1:], ref.dtype)
        for g in range(KV_B):
            sl = slice(HD_B * g, HD_B * (g + 1))
            kn = _rms(k_ref[:, sl].astype(F32)) * gk_ref[...]
            if sample:
                kb_ref[0:past, sl] = ck_ref[:, g, :].astype(BF16)
                vb_ref[g, 0:past, 0:HD_B] = cv_ref[:, g, :].astype(BF16)
                kn = kn * cos_ref[...] + pltpu.roll(kn, HD_B // 2, 1) * sin_ref[...]
            else:
                kc_ref[:, g, :] = kn
                vc_ref[:, g, :] = vf_ref[:, sl]
            kb_ref[past:past + n_new, sl] = kn.astype(BF16)
            vb_ref[g, past:past + n_new, 0:HD_B] = v_ref[:, sl].astype(BF16)
            vb_ref[g, :, HD_B:] = jnp.ones((past + n_new, HD_B), BF16)

    r0 = pl.multiple_of(qi * QB, QB)
    qs = []
    for h in range(H_B):
        qn = _rms(q_ref[:, HD_B * h:HD_B * (h + 1)].astype(F32)) * gq_ref[...]
        if sample:
            qn = qn * cos_ref[pl.ds(r0, QB), :] + pltpu.roll(qn, HD_B // 2, 1) * sin_ref[pl.ds(r0, QB), :]
        qs.append((qn * (HD_B ** -0.5)).astype(BF16))
    scores = [_nt(qs[h], kb_ref[:, HD_B * (h // rep):HD_B * (h // rep + 1)]) for h in range(H_B)]
    weights = [jnp.exp(s - jnp.max(s, axis=-1, keepdims=True)).astype(BF16) for s in scores]
    sums = [_mm(weights[h], vb_ref[h // rep]) for h in range(H_B)]
    for h in range(H_B):
        o_ref[:, HD_B * h:HD_B * (h + 1)] = (sums[h][:, :HD_B] / sums[h][:, HD_B:]).astype(o_ref.dtype)


def _gqa(proj, g_q, g_k, n_batch, seq, ctx=None, rope=None, v_f32=None, slot=0, prev=None):
    sample = ctx is not None
    m = n_batch * seq
    nq = seq // QB
    n_even = (DEPTH + 1) // 2
    in_specs = [
        pl.BlockSpec((QB, 512), lambda b, i: (b * nq + i, 3)),
        pl.BlockSpec((seq, 256), lambda b, i: (b, 8)),
        pl.BlockSpec((seq, 256), lambda b, i: (b, 9)),
        pl.BlockSpec((1, HD_B), lambda b, i: (0, 0)),
        pl.BlockSpec((1, HD_B), lambda b, i: (0, 0)),
    ]
    args = [proj, proj, proj, g_q.reshape(1, HD_B), g_k.reshape(1, HD_B)]
    o_spec = pl.BlockSpec((QB, 512), lambda b, i: (b * nq + i, 0))
    o_shape = jax.ShapeDtypeStruct((m, 512), BF16)
    aliases = {}
    if sample:
        cache_spec = pl.BlockSpec((None, None, PAST_LEN, KV_B, HD_B), lambda b, i: (b, slot, 0, 0, 0))
        in_specs += [
            cache_spec, cache_spec,
            pl.BlockSpec((seq, HD_B), lambda b, i: (0, 0)),
            pl.BlockSpec((seq, HD_B), lambda b, i: (0, 0)),
        ]
        args += [ctx[0], ctx[1], rope[0], rope[1]]
        out_specs, out_shape = o_spec, o_shape
    else:
        in_specs.append(pl.BlockSpec((seq, KV_B * HD_B), lambda b, i: (b, 0)))
        args.append(v_f32)
        if prev is not None:
            in_specs += [pl.BlockSpec(memory_space=pl.ANY)] * 2
            aliases = {len(args): 1, len(args) + 1: 2}
            args += list(prev)
        if prev is None:
            new_spec = pl.BlockSpec((None, n_even, seq, KV_B, HD_B), lambda b, i: (b, 0, 0, 0, 0))
        else:
            new_spec = pl.BlockSpec((None, None, seq, KV_B, HD_B), lambda b, i: (b, slot, 0, 0, 0))
        new_shape = jax.ShapeDtypeStruct((n_batch, n_even, seq, KV_B, HD_B), F32)
        out_specs, out_shape = [o_spec, new_spec, new_spec], [o_shape, new_shape, new_shape]
    n_keys = seq + (PAST_LEN if sample else 0)
    return pl.pallas_call(
        functools.partial(_gqa_kernel, sample=sample, has_prev=prev is not None, slot=slot),
        grid=(n_batch, nq),
        in_specs=in_specs,
        out_specs=out_specs,
        out_shape=out_shape,
        input_output_aliases=aliases,
        scratch_shapes=[pltpu.VMEM((n_keys, KV_B * HD_B), BF16), pltpu.VMEM((KV_B, n_keys, 2 * HD_B), BF16)],
        compiler_params=_params(2),
        name="gqa_sample" if sample else "gqa_prompt",
    )(*args)


MLA_QW = 2 * LANES
MLA_HW = 4 * LANES


def _rotate_pairs(x, cos_t, sin_lo, sin_hi):
    w = x.shape[1]
    return x * cos_t + pltpu.roll(x, ROPE_D // 2, 1) * sin_hi + pltpu.roll(x, w - ROPE_D // 2, 1) * sin_lo


def _mla_kernel(*refs, sample):
    if sample:
        (cq_ref, ckv_ref, kpe_ref, gq_ref, wqb_ref, gkv_ref, wkvb_ref, cckv_ref, ckpe_ref,
         qc_ref, ql_ref, qh_ref, kc_ref, kl_ref, kh_ref, o_ref, kv_s) = refs
    else:
        cq_ref, ckv_ref, kpe_ref, gq_ref, wqb_ref, gkv_ref, wkvb_ref, o_ref, ckvn_ref, kv_s = refs
    qi = pl.program_id(1)
    n_new = ckv_ref.shape[0]
    past = PAST_LEN if sample else 0

    def stage_kv(rows, kv, kpe_block):
        for h in range(H_D):
            kv_s[rows, MLA_HW * h:MLA_HW * h + NOPE_D] = kv[:, 256 * h:256 * h + NOPE_D].astype(BF16)
            kv_s[rows, MLA_HW * h + NOPE_D:MLA_HW * h + MLA_QW] = kpe_block
            kv_s[rows, MLA_HW * h + MLA_QW:MLA_HW * h + MLA_QW + V_D] = kv[:, 256 * h + NOPE_D:256 * (h + 1)].astype(BF16)

    @pl.when(qi == 0)
    def _():
        wkvb = wkvb_ref[...].astype(BF16)
        ckvn = _rms(ckv_ref[...].astype(F32)) * gkv_ref[...]
        if not sample:
            ckvn_ref[...] = ckvn
        kpe = kpe_ref[...]
        if sample:
            ctx_kpe = jnp.concatenate([ckpe_ref[...], jnp.zeros((past, LANES - ROPE_D), F32)], axis=1)
            stage_kv(slice(0, past), _mm(cckv_ref[...].astype(BF16), wkvb), ctx_kpe.astype(BF16))
            kpe = _rotate_pairs(kpe.astype(F32), kc_ref[...], kl_ref[...], kh_ref[...]).astype(BF16)
        stage_kv(slice(past, past + n_new), _mm(ckvn.astype(BF16), wkvb), kpe)
        for h in range(H_D):
            kv_s[:, MLA_HW * h + MLA_QW + V_D:MLA_HW * (h + 1)] = jnp.ones((past + n_new, V_D), BF16)

    q = _mm((_rms(cq_ref[...].astype(F32)) * gq_ref[...]).astype(BF16), wqb_ref[...].astype(BF16))
    q = q * ((NOPE_D + ROPE_D) ** -0.5)
    qs = []
    for h in range(H_D):
        q_h = q[:, MLA_QW * h:MLA_QW * (h + 1)]
        if sample:
            rows = pl.ds(pl.multiple_of(qi * QB, QB), QB)
            q_h = _rotate_pairs(q_h, qc_ref[rows, :], ql_ref[rows, :], qh_ref[rows, :])
        qs.append(q_h.astype(BF16))
    scores = [_nt(qs[h], kv_s[:, MLA_HW * h:MLA_HW * h + MLA_QW]) for h in range(H_D)]
    weights = [jnp.exp(s - jnp.max(s, axis=-1, keepdims=True)).astype(BF16) for s in scores]
    sums = [_mm(weights[h], kv_s[:, MLA_HW * h + MLA_QW:MLA_HW * (h + 1)]) for h in range(H_D)]
    for h in range(H_D):
        o_ref[:, V_D * h:V_D * (h + 1)] = (sums[h][:, :V_D] / sums[h][:, V_D:]).astype(o_ref.dtype)


def _mla(proj, g_q, w_qb, g_kv, w_kvb, n_batch, seq, ctx=None, rope=None):
    sample = ctx is not None
    m = n_batch * seq
    nq = seq // QB
    in_specs = [
        pl.BlockSpec((QB, Q_RANK), lambda b, i: (b * nq + i, 6)),
        pl.BlockSpec((seq, KV_RANK), lambda b, i: (b, 14)),
        pl.BlockSpec((seq, LANES), lambda b, i: (b, 15)),
        pl.BlockSpec((1, Q_RANK), lambda b, i: (0, 0)),
        pl.BlockSpec((Q_RANK, H_D * MLA_QW), lambda b, i: (0, 0)),
        pl.BlockSpec((1, KV_RANK), lambda b, i: (0, 0)),
        pl.BlockSpec((KV_RANK, 1024), lambda b, i: (0, 0)),
    ]
    args = [proj, proj, proj, g_q.reshape(1, Q_RANK), w_qb, g_kv.reshape(1, KV_RANK), w_kvb]
    o_spec = pl.BlockSpec((QB, 512), lambda b, i: (b * nq + i, 0))
    o_shape = jax.ShapeDtypeStruct((m, 512), BF16)
    if sample:
        in_specs += [
            pl.BlockSpec((None, PAST_LEN, KV_RANK), lambda b, i: (b, 0, 0)),
            pl.BlockSpec((None, PAST_LEN, ROPE_D), lambda b, i: (b, 0, 0)),
        ] + [pl.BlockSpec((seq, MLA_QW), lambda b, i: (0, 0))] * 3 + [pl.BlockSpec((seq, LANES), lambda b, i: (0, 0))] * 3
        args += [ctx[0], ctx[1], *rope]
        out_specs, out_shape = o_spec, o_shape
    else:
        out_specs = [o_spec, pl.BlockSpec((seq, KV_RANK), lambda b, i: (b, 0))]
        out_shape = [o_shape, jax.ShapeDtypeStruct((m, KV_RANK), F32)]
    n_keys = seq + (PAST_LEN if sample else 0)
    return pl.pallas_call(
        functools.partial(_mla_kernel, sample=sample),
        grid=(n_batch, nq),
        in_specs=in_specs,
        out_specs=out_specs,
        out_shape=out_shape,
        scratch_shapes=[pltpu.VMEM((n_keys, H_D * MLA_HW), BF16)],
        compiler_params=_params(2),
        name="mla_sample" if sample else "mla_prompt",
    )(*args)


def _dft(table, x):
    return _mm(table.astype(BF16), x.astype(BF16))


def _filter_kernel(z_ref, wf1_ref, bf1_ref, fr_ref, wf2_ref, bf2_ref, wf3_ref, t_ref, dl_ref,
                   c_ref, s_ref, gre_ref, gim_ref):
    n_tok = z_ref.shape[0]
    fr = fr_ref[...]
    hid = jnp.sin(fr * (_mm(z_ref[...].astype(BF16), wf1_ref[...].astype(BF16)) + bf1_ref[...]))
    hid = jnp.sin(fr * (_mm(hid.astype(BF16), wf2_ref[...].astype(BF16)) + bf2_ref[...]))
    filt = _mm(hid.astype(BF16), wf3_ref[...].astype(BF16))
    decay = jnp.exp(-t_ref[...] * dl_ref[...])
    row = lax.broadcasted_iota(jnp.int32, (n_tok, 1), 0)
    h_f = filt[:, :HY_W] * decay
    h_b = jnp.where(row == 0, 0.0, filt[:, HY_W:] * decay)
    p, m = h_f + h_b, h_f - h_b
    g_re = _dft(c_ref[...], p)
    g_im = _dft(s_ref[...], m)
    sign = jnp.where(row % 2 == 0, 1.0, -1.0)
    nyquist = jnp.sum(p * sign, axis=0, keepdims=True)
    g_im = jnp.where(row == 0, nyquist, g_im)
    wk = jnp.where(row == 0, 0.5 / n_tok, 1.0 / n_tok)
    gre_ref[...] = g_re * wk
    gim_ref[...] = g_im * wk


def _filter_spectrum(z, wf1, bf1, freq, wf2, bf2, wf3, t_col, deltas, tabs):
    n_tok = z.shape[0]
    out = jax.ShapeDtypeStruct((n_tok, HY_W), F32)
    return pl.pallas_call(
        _filter_kernel,
        out_shape=[out, out],
        compiler_params=pltpu.CompilerParams(vmem_limit_bytes=VMEM_LIMIT),
        name="hyena_filter",
    )(z, wf1, bf1.reshape(1, FILT_HID), freq.reshape(1, FILT_HID), wf2, bf2.reshape(1, FILT_HID), wf3,
      t_col, deltas, tabs[0], tabs[1])


HY_CT = 256


HY_ROWS = 1024


def _hyena_kernel(u0_ref, u1_ref, u2_ref, w0_ref, w1_ref, w2_ref, b0_ref, b1_ref, b2_ref, skip_ref,
                  gre_ref, gim_ref, cf_ref, sf_ref, stf_ref, o_ref, c_ref, s_ref, st_ref):
    seq = c_ref.shape[0]
    n_rows = u0_ref.shape[0]
    n_seq = n_rows // seq
    pos = lax.broadcasted_iota(jnp.int32, (n_rows, 1), 0) % seq

    @pl.when((pl.program_id(0) == 0) & (pl.program_id(1) == 0))
    def _():
        c_ref[...] = cf_ref[...].astype(BF16)
        s_ref[...] = sf_ref[...].astype(BF16)
        st_ref[...] = stf_ref[...].astype(BF16)

    def short_conv(u_ref, w_ref, b_ref):
        x, w = u_ref[...].astype(F32), w_ref[...]
        prev = jnp.where(pos == 0, 0.0, pltpu.roll(x, 1, 0))
        nxt = jnp.where(pos == seq - 1, 0.0, pltpu.roll(x, n_rows - 1, 0))
        return prev * w[0:1] + x * w[1:2] + nxt * w[2:3] + b_ref[...]

    def side_by_side(a):
        return a if n_seq == 1 else jnp.concatenate([a[s * seq:(s + 1) * seq] for s in range(n_seq)], axis=1)

    def stacked(a):
        ct = a.shape[1] // n_seq
        return a if n_seq == 1 else jnp.concatenate([a[:, s * ct:(s + 1) * ct] for s in range(n_seq)], axis=0)

    x0 = short_conv(u0_ref, w0_ref, b0_ref)
    gv = short_conv(u1_ref, w1_ref, b1_ref) * short_conv(u2_ref, w2_ref, b2_ref)
    sig = side_by_side(gv).astype(BF16)
    u_re = _mm(c_ref[...], sig)
    u_im = _mm(s_ref[...], sig)
    g_re = jnp.concatenate([gre_ref[...]] * n_seq, axis=1)
    g_im = jnp.concatenate([gim_ref[...]] * n_seq, axis=1)
    bin0 = lax.broadcasted_iota(jnp.int32, (seq, 1), 0) == 0
    p_im = u_im * g_im
    y_re = u_re * g_re - jnp.where(bin0, 0.0, p_im)
    y_im = jnp.where(bin0, p_im, u_re * g_im + u_im * g_re)
    y = stacked(_mm(c_ref[...], y_re.astype(BF16)) + _mm(st_ref[...], y_im.astype(BF16)))
    o_ref[...] = (x0 * (y + gv * skip_ref[...])).astype(o_ref.dtype)


def _hyena(proj, w_conv, b_conv, skip, g_re, g_im, tabs, n_batch, seq):
    nct = HY_W // HY_CT
    u_specs = [pl.BlockSpec((HY_ROWS, HY_CT), functools.partial(lambda b, c, g: (b, g * nct + c), g=g)) for g in range(3)]
    w_specs = [pl.BlockSpec((3, HY_CT), functools.partial(lambda b, c, g: (0, g * nct + c), g=g)) for g in range(3)]
    b_specs = [pl.BlockSpec((1, HY_CT), functools.partial(lambda b, c, g: (0, g * nct + c), g=g)) for g in range(3)]
    tab_spec = pl.BlockSpec((seq, seq), lambda b, c: (0, 0))
    return pl.pallas_call(
        _hyena_kernel,
        grid=(n_batch * seq // HY_ROWS, nct),
        in_specs=u_specs + w_specs + b_specs + [
            pl.BlockSpec((1, HY_CT), lambda b, c: (0, c)),
            pl.BlockSpec((seq, HY_CT), lambda b, c: (0, c)),
            pl.BlockSpec((seq, HY_CT), lambda b, c: (0, c)),
        ] + [tab_spec] * 3,
        out_specs=pl.BlockSpec((HY_ROWS, HY_CT), lambda b, c: (b, c)),
        out_shape=jax.ShapeDtypeStruct((n_batch * seq, HY_W), BF16),
        scratch_shapes=[pltpu.VMEM((seq, seq), BF16)] * 3,
        compiler_params=_params(2),
        name="hyena_conv",
    )(proj, proj, proj, w_conv, w_conv, w_conv, b_conv, b_conv, b_conv, skip.reshape(1, HY_W), g_re, g_im, *tabs)


def _dft_tables(n_tok):
    k = np.arange(n_tok)[:, None]
    s = np.arange(n_tok)[None, :]
    ang = ((k * s) % (2 * n_tok)) * (np.pi / n_tok)
    cos_t = np.cos(ang)
    sin_f = np.where(k == 0, np.where(s % 2 == 0, 1.0, -1.0), -np.sin(ang))
    return [jnp.asarray(t, F32) for t in (cos_t, sin_f, sin_f.T)]


GLA_LEVELS = (32, 16, 8, 4, 2, 1)
GLA_SAFE_DECAY = 60.0
GLA_GROUP = 2


def _gla_constants():
    c = GLA_CHUNK
    idx = np.arange(c)
    i, t = idx[:, None], idx[None, :]
    masks = []
    for s in GLA_LEVELS:
        upper = (idx % (2 * s)) >= s
        masks.append(((i // (2 * s)) == (t // (2 * s))) & upper[:, None] & (~upper)[None, :])
    masks.append(i == t)
    tri = t <= i
    fwd_m = np.stack([np.tile(m, (H_A, 1)) for m in masks]).astype(np.float32)
    bwd_m = np.stack([np.tile(m[::-1, ::-1], (H_A, 1)) for m in masks]).astype(np.float32)
    head_of_row = np.repeat(np.arange(H_A), c)[:, None]
    head_of_lane = np.repeat(np.arange(H_A), DK_A)[None, :]
    head_mask = head_of_row == head_of_lane
    return (jnp.asarray(tri, BF16), jnp.asarray(tri[::-1, ::-1], BF16), jnp.asarray(fwd_m), jnp.asarray(bwd_m),
            jnp.asarray(head_mask, BF16))


def _pair_reference(b, s, backward, row):
    c = GLA_CHUNK
    ref = s if backward else s - 1
    if 2 * s >= 8:
        pieces = [jnp.broadcast_to(b[p * 2 * s + ref:p * 2 * s + ref + 1, :], (2 * s, b.shape[1]))
                  for p in range(c // (2 * s))]
        return pieces[0] if len(pieces) == 1 else jnp.concatenate(pieces, axis=0)
    pos = row % (2 * s)
    out = None
    for o in range(2 * s):
        d = ref - o
        shifted = b if d == 0 else pltpu.roll(b, (-d) % c, 0)
        out = shifted if out is None else jnp.where(pos == o, shifted, out)
    return out


def _chunk_log_decay(la, t_ref):
    l1 = la.astype(BF16)
    r1 = la - l1.astype(F32)
    l2 = r1.astype(BF16)
    l3 = (r1 - l2.astype(F32)).astype(BF16)
    tmat = t_ref[...]
    return _mm(tmat, l1) + _mm(tmat, l2) + _mm(tmat, l3)


def _stack_heads(a, hm):
    ab = a.astype(BF16)
    return jnp.concatenate([ab] * H_A, axis=0) * hm


def _state_terms(k, v, b, b_last):
    c = GLA_CHUNK
    k_rest = (k * jnp.exp(b_last - b)).T
    carry = jnp.broadcast_to(jnp.exp(b_last), (2 * c, b.shape[1])).T
    return k_rest.astype(BF16), carry


def _gla_chunk(q, k, v, la, t_ref, m_ref, hm, s_ref, backward):
    c = GLA_CHUNK
    b = _chunk_log_decay(la, t_ref)
    row = lax.broadcasted_iota(jnp.int32, (c, 1), 0)
    last = 0 if backward else c - 1
    b_last = b[last:last + 1, :]
    scores = _nt(_stack_heads(q, hm), k.astype(BF16)) * m_ref[len(GLA_LEVELS)]
    for lvl, s in enumerate(GLA_LEVELS):
        is_query = (row % (2 * s) < s) if backward else (row % (2 * s) >= s)
        delta = b - _pair_reference(b, s, backward, row)
        x = jnp.exp(jnp.where(is_query, delta, -delta))
        scores = scores + _nt(_stack_heads(q * x, hm), (k * x).astype(BF16)) * m_ref[lvl]
    scores = scores.astype(BF16)
    state = s_ref[...]
    inter = _mm(_stack_heads(q * jnp.exp(b), hm), state.astype(BF16))
    k_rest, carry = _state_terms(k, v, b, b_last)
    outs = []
    for h in range(H_A):
        rows = slice(c * h, c * (h + 1))
        v_h = v[:, DV_A * h:DV_A * (h + 1)]
        outs.append(_mm(scores[rows], v_h) + inter[rows])
        s_ref[rows, :] = state[rows] * carry[rows] + _mm(k_rest[rows], v_h)
    return jnp.concatenate(outs, axis=1)


def _gla_local(items, hm):
    c = GLA_CHUNK
    bs = [_chunk_log_decay(la, t_ref) for _, _, _, la, t_ref, _, _ in items]
    b_lasts = [b[(0 if it[6] else c - 1):(0 if it[6] else c - 1) + 1, :] for b, it in zip(bs, items)]
    q_decayed = [_stack_heads(it[0] * jnp.exp(b), hm) for it, b in zip(items, bs)]
    k_grown = [(it[1] * jnp.exp(-b)).astype(BF16) for it, b in zip(items, bs)]
    raw = [_nt(qd, kg) for qd, kg in zip(q_decayed, k_grown)]
    scores = [(r * it[5]).astype(BF16) for r, it in zip(raw, items)]
    terms = [_state_terms(it[1], it[2], b, bl) for it, b, bl in zip(items, bs, b_lasts)]
    out = []
    for it, sc, (k_rest, carry), qd in zip(items, scores, terms, q_decayed):
        v = it[2]
        heads = [(slice(c * h, c * (h + 1)), v[:, DV_A * h:DV_A * (h + 1)]) for h in range(H_A)]
        intra = jnp.concatenate([_mm(sc[rows], v_h) for rows, v_h in heads], axis=1)
        incr = jnp.concatenate([_mm(k_rest[rows], v_h) for rows, v_h in heads], axis=0)
        out.append((intra, qd, incr, carry))
    return out


def _gla_kernel(*refs, sample):
    if sample:
        (x_ref, z_ref, wf_ref, bf_ref, wb_ref, bb_ref, tf_ref, tb_ref, mf_ref, mb_ref, hm_ref, gn_ref, sf0_ref, sb0_ref,
         o_ref, la_f, la_b, o_f, o_b, s_f, s_b, qd_f, qd_b, ds_f, ds_b, cr_f, cr_b) = refs
    else:
        (x_ref, z_ref, wf_ref, bf_ref, wb_ref, bb_ref, tf_ref, tb_ref, mf_ref, mb_ref, hm_ref, gn_ref,
         o_ref, sf_out, sb_out, la_f, la_b, o_f, o_b, s_f, s_b, qd_f, qd_b, ds_f, ds_b, cr_f, cr_b) = refs
    n_tok = x_ref.shape[0]
    n_chunks = n_tok // GLA_CHUNK
    hk, hv = H_A * DK_A, H_A * DV_A
    zb = z_ref[...].astype(BF16)

    def log_sigmoid(t):
        return jnp.minimum(t, 0.0) - jnp.log(1.0 + jnp.exp(-jnp.abs(t)))

    la_f[...] = log_sigmoid(_mm(zb, wf_ref[...].astype(BF16)) + bf_ref[...]) / GLA_TAU
    la_b[...] = log_sigmoid(_mm(zb, wb_ref[...].astype(BF16)) + bb_ref[...]) / GLA_TAU
    if sample:
        s_f[...] = sf0_ref[...]
        s_b[...] = sb0_ref[...]
    else:
        s_f[...] = jnp.zeros_like(s_f)
        s_b[...] = jnp.zeros_like(s_b)
    hm = hm_ref[...]

    fwd = (la_f, tf_ref, mf_ref, s_f, o_f, qd_f, ds_f, cr_f, False)
    bwd = (la_b, tb_ref, mb_ref, s_b, o_b, qd_b, ds_b, cr_b, True)
    tri_f = jnp.sum(mf_ref[...], axis=0)
    tri_b = jnp.sum(mb_ref[...], axis=0)

    def chunk_rows(ci, backward):
        cidx = n_chunks - 1 - ci if backward else ci
        return cidx, pl.ds(pl.multiple_of(cidx * GLA_CHUNK, GLA_CHUNK), GLA_CHUNK)

    def load_qkv(rows):
        q = x_ref[rows, 0:hk].astype(F32) * (DK_A ** -0.5)
        return q, x_ref[rows, hk:2 * hk].astype(F32), x_ref[rows, 2 * hk:2 * hk + hv]

    def safe_step(ci, carry):
        for la_ref, t_ref, m_ref, s_ref, out_ref, _, _, _, backward in (fwd, bwd):
            _, rows = chunk_rows(ci, backward)
            out_ref[rows, :] = _gla_chunk(*load_qkv(rows), la_ref[rows, :], t_ref, m_ref, hm, s_ref, backward)
        return carry

    def local_step(gi, carry):
        items, dests = [], []
        for (la_ref, t_ref, _, _, out_ref, qd_ref, ds_ref, cr_ref, backward), tri in ((fwd, tri_f), (bwd, tri_b)):
            for u in range(GLA_GROUP):
                cidx, rows = chunk_rows(gi * GLA_GROUP + u, backward)
                items.append((*load_qkv(rows), la_ref[rows, :], t_ref, tri, backward))
                dests.append((out_ref, rows, qd_ref, ds_ref, cr_ref, cidx))
        for (out_ref, rows, qd_ref, ds_ref, cr_ref, cidx), (intra, qd, incr, factor) in zip(dests, _gla_local(items, hm)):
            out_ref[rows, :] = intra
            qd_ref[cidx] = qd
            ds_ref[cidx] = incr
            cr_ref[cidx] = factor
        return carry

    def scan_step(ci, carry):
        for _, _, _, s_ref, out_ref, qd_ref, ds_ref, cr_ref, backward in (fwd, bwd):
            cidx, rows = chunk_rows(ci, backward)
            state = s_ref[...]
            inter = _mm(qd_ref[cidx], state.astype(BF16))
            out_ref[rows, :] += jnp.concatenate(
                [inter[GLA_CHUNK * h:GLA_CHUNK * (h + 1)] for h in range(H_A)], axis=1)
            s_ref[...] = state * cr_ref[cidx] + ds_ref[cidx]
        return carry

    chunk_sums = [jnp.sum(ref[...].reshape(n_chunks, GLA_CHUNK, hk), axis=1) for ref in (la_f, la_b)]
    mild = jnp.minimum(jnp.min(chunk_sums[0]), jnp.min(chunk_sums[1])) > -GLA_SAFE_DECAY

    @pl.when(mild)
    def _():
        lax.fori_loop(0, n_chunks // GLA_GROUP, local_step, 0, unroll=2)
        lax.fori_loop(0, n_chunks, scan_step, 0, unroll=2)

    @pl.when(jnp.logical_not(mild))
    def _():
        lax.fori_loop(0, n_chunks, safe_step, 0)
    if not sample:
        sf_out[...] = s_f[...]
        sb_out[...] = s_b[...]
    gain = gn_ref[...]
    for h in range(H_A):
        cols = slice(DV_A * h, DV_A * (h + 1))
        r = x_ref[:, 2 * hk + hv + DV_A * h:2 * hk + hv + DV_A * (h + 1)].astype(F32)
        o_ref[:, cols] = (_rms(o_f[:, cols] + o_b[:, cols]) * gain * (r * jax.nn.sigmoid(r))).astype(o_ref.dtype)


def _gla(proj, w_gf, b_gf, w_gb, b_gb, g_norm, consts, n_batch, seq, ctx=None):
    sample = ctx is not None
    hk, hv = H_A * DK_A, H_A * DV_A
    n_ch = seq // GLA_CHUNK
    full = lambda shape: pl.BlockSpec(shape, lambda b: (0,) * len(shape))
    in_specs = [
        pl.BlockSpec((seq, 2 * hk + 2 * hv), lambda b: (b, 0)),
        pl.BlockSpec((seq, LANES), lambda b: (b, EVEN_W // LANES - 1)),
        full((LANES, hk)), full((1, hk)), full((LANES, hk)), full((1, hk)),
        full(consts[0].shape), full(consts[1].shape), full(consts[2].shape), full(consts[3].shape), full(consts[4].shape),
        full((1, DV_A)),
    ]
    args = [proj, proj, w_gf, b_gf.reshape(1, hk), w_gb, b_gb.reshape(1, hk), *consts, g_norm.reshape(1, DV_A)]
    o_spec = pl.BlockSpec((seq, hv), lambda b: (b, 0))
    o_shape = jax.ShapeDtypeStruct((n_batch * seq, hv), BF16)
    st_spec = pl.BlockSpec((None, hk, DV_A), lambda b: (b, 0, 0))
    if sample:
        in_specs += [st_spec, st_spec]
        args += [ctx[0], ctx[1]]
        out_specs, out_shape = o_spec, o_shape
    else:
        st_shape = jax.ShapeDtypeStruct((n_batch, hk, DV_A), F32)
        out_specs, out_shape = [o_spec, st_spec, st_spec], [o_shape, st_shape, st_shape]
    return pl.pallas_call(
        functools.partial(_gla_kernel, sample=sample),
        grid=(n_batch,),
        in_specs=in_specs,
        out_specs=out_specs,
        out_shape=out_shape,
        scratch_shapes=[pltpu.VMEM((seq, hk), F32), pltpu.VMEM((seq, hk), F32),
                        pltpu.VMEM((seq, hv), F32), pltpu.VMEM((seq, hv), F32),
                        pltpu.VMEM((hk, DV_A), F32), pltpu.VMEM((hk, DV_A), F32),
                        pltpu.VMEM((n_ch, H_A * GLA_CHUNK, hk), BF16), pltpu.VMEM((n_ch, H_A * GLA_CHUNK, hk), BF16),
                        pltpu.VMEM((n_ch, hk, DV_A), F32), pltpu.VMEM((n_ch, hk, DV_A), F32),
                        pltpu.VMEM((n_ch, hk, DV_A), F32), pltpu.VMEM((n_ch, hk, DV_A), F32)],
        compiler_params=_params(1),
        name="gla_sample" if sample else "gla_prompt",
    )(*args)


def _axial_rope(n_tokens, dim):
    rows = n_tokens // GRID_W
    row = np.repeat(np.arange(rows), GRID_W).astype(np.float64)
    col = np.tile(np.arange(GRID_W), rows).astype(np.float64)
    n_freq = dim // 4
    inv = ROPE_THETA ** (-np.arange(n_freq) / n_freq)
    ang = np.concatenate([row[:, None] * inv, col[:, None] * inv], axis=-1)
    return np.cos(ang).astype(np.float32), np.sin(ang).astype(np.float32)


def _filter_features(n_tokens):
    t = np.linspace(0.0, 1.0, n_tokens)[:, None]
    w = 2.0 * np.pi * np.arange(n_tokens)[:, None] / n_tokens
    f = np.linspace(1e-4, FILT_BANDS - 1, FILT_BANDS)[None, :]
    z = np.concatenate([t, np.cos(f * w), -np.sin(f * w)], axis=-1)
    z = np.pad(z, ((0, 0), (0, LANES - FILT_EMB)))
    return jnp.asarray(z, F32), jnp.asarray(t, F32)


_QB_ZERO = H_D * (NOPE_D + ROPE_D)
_QB_PERM = np.array([(NOPE_D + ROPE_D) * (p // MLA_QW) + p % MLA_QW if p % MLA_QW < NOPE_D + ROPE_D else _QB_ZERO
                     for p in range(H_D * MLA_QW)])


def _mla_rope_tables(cos_d, sin_d):
    n, half = cos_d.shape
    zeros = np.zeros((n, half), np.float32)

    def lanes(pre, width):
        pad = np.zeros((n, width - pre.shape[1] - 2 * half), np.float32)
        build = lambda first, second, lead: np.concatenate([lead, first, second, pad], axis=1)
        return (build(cos_d, cos_d, pre), build(-sin_d, zeros, 0 * pre), build(zeros, sin_d, 0 * pre))

    q_tabs = lanes(np.ones((n, NOPE_D), np.float32), MLA_QW)
    k_tabs = lanes(np.zeros((n, 0), np.float32), LANES)
    return tuple(jnp.asarray(t) for t in q_tabs + k_tabs)

EVEN_ROW_GROUPS = ((0, 0, 1536), (1568, 1536, 1024), (1536, EVEN_W - 2 * GATE_RANK, 2 * GATE_RANK))
ODD_ROW_GROUPS = ((0, 0, 1984),)
EVEN_KEEP = (2304, 256)
ODD_KEEP = (1920, LANES)


def kernel(x_prompt, x_sample, state_gla_fwd, state_gla_bwd, cache_gqa_k, cache_gqa_v, cache_mla_ckv, cache_mla_kpe, c, c_ctx, w_mod, b_mod, w_in_even, w_gla_gate_f, b_gla_gate_f, w_gla_gate_b, b_gla_gate_b, g_gla_norm, g_gqa_q, g_gqa_k, w_out_even, w_in_odd, w_hy_conv, b_hy_conv, hy_skip, w_filt1, b_filt1, filt_freq, w_filt2, b_filt2, w_filt3, g_mla_q, w_mla_qb, g_mla_kv, w_mla_kvb, w_out_odd, w_ffn_in, w_ffn_out, g_final):
    n_c, n_s = BATCH * SEQ, DEC_BATCH * DEC_SEQ
    cvec = jnp.concatenate([c_ctx[None, :], c, jnp.zeros((8 - 1 - DEC_BATCH, D_MODEL), F32)], axis=0)
    mod = _modulation(cvec, w_mod, b_mod)
    xc = x_prompt.reshape(n_c, D_MODEL)
    xs = x_sample.reshape(n_s, D_MODEL)
    rows_c, rows_s = (0, 0), (1, DEC_SEQ // MOD_ROWS)

    gla_consts = _gla_constants()
    cos_b, sin_b = _axial_rope(DEC_SEQ, HD_B)
    rope_b = (jnp.asarray(np.concatenate([cos_b, cos_b], axis=1)), jnp.asarray(np.concatenate([-sin_b, sin_b], axis=1)))
    cos_d, sin_d = _axial_rope(DEC_SEQ, ROPE_D)
    rope_d = _mla_rope_tables(cos_d, sin_d)
    w_qb_all = jnp.pad(w_mla_qb, ((0, 0), (0, 0), (0, 1)))[:, :, _QB_PERM]
    tabs_c, tabs_s = _dft_tables(SEQ), _dft_tables(DEC_SEQ)
    z_c, t_c = _filter_features(SEQ)
    z_s, t_s = _filter_features(DEC_SEQ)
    deltas = jnp.asarray(np.abs(np.linspace(HY_MIN_DECAY, HY_MAX_DECAY, HY_W))[None, :], F32)

    wt_even = jnp.swapaxes(w_in_even, 1, 2)
    wt_odd = jnp.swapaxes(w_in_odd, 1, 2)

    st_gf, st_gb, st_ckv, st_kpe = [], [], [], []
    new_kv = None
    for i in range(DEPTH):
        j = i // 2
        if i % 2 == 0:
            z0 = LANES - 2 * GATE_RANK
            pad_f = jnp.zeros((LANES, H_A * DK_A), F32).at[z0:z0 + GATE_RANK].set(w_gla_gate_f[j])
            pad_b = jnp.zeros((LANES, H_A * DK_A), F32).at[z0 + GATE_RANK:LANES].set(w_gla_gate_b[j])
            pc, v_new, ps = _in_proj(xc, xs, mod, i, wt_even, j, EVEN_ROW_GROUPS, EVEN_W, EVEN_KEEP)
            gate_args = (pad_f, b_gla_gate_f[j], pad_b, b_gla_gate_b[j], g_gla_norm[j], gla_consts)
            a_c, s_f, s_b = _gla(pc, *gate_args, BATCH, SEQ)
            ctx_a = (state_gla_fwd[:, j].reshape(DEC_BATCH, H_A * DK_A, DV_A),
                     state_gla_bwd[:, j].reshape(DEC_BATCH, H_A * DK_A, DV_A))
            a_s = _gla(ps, *gate_args, DEC_BATCH, DEC_SEQ, ctx=ctx_a)
            b_c, *new_kv = _gqa(pc, g_gqa_q[j], g_gqa_k[j], BATCH, SEQ, v_f32=v_new, slot=j, prev=new_kv)
            b_s = _gqa(ps, g_gqa_q[j], g_gqa_k[j], DEC_BATCH, DEC_SEQ, ctx=(cache_gqa_k, cache_gqa_v), rope=rope_b, slot=j)
            w_out = w_out_even
            st_gf.append(s_f.reshape(BATCH, H_A, DK_A, DV_A))
            st_gb.append(s_b.reshape(BATCH, H_A, DK_A, DV_A))
        else:
            pc, kpe_new, ps = _in_proj(xc, xs, mod, i, wt_odd, j, ODD_ROW_GROUPS, ODD_W, ODD_KEEP)
            wf1 = jnp.pad(w_filt1[j], ((0, LANES - FILT_EMB), (0, 0)))
            filt_args = (wf1, b_filt1[j], filt_freq[j], w_filt2[j], b_filt2[j], w_filt3[j])
            g_c = _filter_spectrum(z_c, *filt_args, t_c, deltas, tabs_c)
            g_s = _filter_spectrum(z_s, *filt_args, t_s, deltas, tabs_s)
            b_conv = b_hy_conv[j].reshape(1, 3 * HY_W)
            a_c = _hyena(pc, w_hy_conv[j], b_conv, hy_skip[j], g_c[0], g_c[1], tabs_c, BATCH, SEQ)
            a_s = _hyena(ps, w_hy_conv[j], b_conv, hy_skip[j], g_s[0], g_s[1], tabs_s, DEC_BATCH, DEC_SEQ)
            w_qb = w_qb_all[j]
            b_c, ckv_norm = _mla(pc, g_mla_q[j], w_qb, g_mla_kv[j], w_mla_kvb[j], BATCH, SEQ)
            b_s = _mla(ps, g_mla_q[j], w_qb, g_mla_kv[j], w_mla_kvb[j], DEC_BATCH, DEC_SEQ,
                       ctx=(cache_mla_ckv[:, j], cache_mla_kpe[:, j]), rope=rope_d)
            w_out = w_out_odd
            st_ckv.append(ckv_norm.reshape(BATCH, SEQ, KV_RANK))
            st_kpe.append(kpe_new[:, :ROPE_D].reshape(BATCH, SEQ, ROPE_D))
        xc, xs = _out_proj([a_c, b_c], [a_s, b_s], w_out, j, xc, xs, mod, i, 2)
        last = g_final if i == DEPTH - 1 else None
        xc = _ffn(xc, mod, i, w_ffn_in, w_ffn_out, *rows_c, final_gain=last)
        xs = _ffn(xs, mod, i, w_ffn_in, w_ffn_out, *rows_s, final_gain=last)
    y_prompt = xc.reshape(BATCH, SEQ, D_MODEL)
    y_sample = xs.reshape(DEC_BATCH, DEC_SEQ, D_MODEL)
    return (y_prompt, y_sample, jnp.stack(st_gf, axis=1), jnp.stack(st_gb, axis=1), new_kv[0], new_kv[1],
            jnp.stack(st_ckv, axis=1), jnp.stack(st_kpe, axis=1))
```

```python
import functools
import math

import numpy as np
import jax
import jax.numpy as jnp
from jax import lax
from jax.experimental import pallas as pl
from jax.experimental.pallas import tpu as pltpu

F32 = jnp.float32
BF16 = jnp.bfloat16

D_MODEL = 1024
BATCH, SEQ = 16, 256
DEC_BATCH, DEC_SEQ = 2, 1024
DEPTH = 4
PAST_LEN = 512
GRID_W = 64
HALF_W = D_MODEL // 2
H_A, DV_A, DK_A = 4, 128, 64
GATE_RANK = 16
GLA_TAU = 16.0
GLA_CHUNK = 64
HD_B, H_B, KV_B = 128, 4, 2
HY_W = HALF_W
FILT_EMB, FILT_HID = 33, 64
FILT_BANDS = (FILT_EMB - 1) // 2
HY_MIN_DECAY = math.log(1e-2) / 1.5
HY_MAX_DECAY = math.log(1e-2) / 0.3
H_D, V_D, NOPE_D, ROPE_D = 4, 128, 128, 64
Q_RANK, KV_RANK = 256, 128
FFN_H = 2816
ROPE_THETA = 10000.0
EPS = 1e-6

LANES = 128
VMEM_LIMIT = 56 * 1024 * 1024

MOD_ROWS = 1024
TM = 1024
TM_IN = 512
TM_FFN = 2048
EVEN_W = 2688
ODD_W = 2048
FFN_TN = 256
QB = 256


def _params(n_grid):
    return pltpu.CompilerParams(dimension_semantics=("arbitrary",) * n_grid, vmem_limit_bytes=VMEM_LIMIT)


def _nt(a, b):
    return lax.dot_general(a, b, (((1,), (1,)), ((), ())), preferred_element_type=F32)


def _mm(a, b):
    return jnp.dot(a, b, preferred_element_type=F32)


def _rms(x):
    return x * lax.rsqrt(jnp.mean(x * x, axis=-1, keepdims=True) + EPS)


def _mod_kernel(c_ref, w_ref, b_ref, o_ref):
    cv = c_ref[...]
    s = cv * jax.nn.sigmoid(cv)
    o_ref[...] = _mm(s.astype(BF16), w_ref[...].astype(BF16)) + b_ref[...]


def _modulation(cvec, w_mod, b_mod):
    return pl.pallas_call(
        _mod_kernel,
        grid=(DEPTH, 6),
        in_specs=[
            pl.BlockSpec((8, D_MODEL), lambda l, n: (0, 0)),
            pl.BlockSpec((None, D_MODEL, D_MODEL), lambda l, n: (l, 0, n)),
            pl.BlockSpec((None, 1, D_MODEL), lambda l, n: (l, 0, n)),
        ],
        out_specs=pl.BlockSpec((None, None, 8, D_MODEL), lambda l, n: (l, n, 0, 0)),
        out_shape=jax.ShapeDtypeStruct((DEPTH, 6, 8, D_MODEL), F32),
        compiler_params=_params(2),
        name="adaln_mod",
    )(cvec, w_mod, b_mod.reshape(DEPTH, 1, 6 * D_MODEL))


def _mod_row(row0, rstep, tile_rows, sub):
    if tile_rows >= MOD_ROWS:
        return row0 + rstep * (pl.program_id(0) * (tile_rows // MOD_ROWS) + sub)
    return row0 + rstep * (pl.program_id(0) // (MOD_ROWS // tile_rows))


def _in_proj_kernel(xc_ref, xs_ref, sh_ref, sc_ref, wt_ref, oc_ref, keep_ref, os_ref, wb_ref, *, row_groups, keep, n_c):
    i = pl.program_id(0)

    @pl.when(i == 0)
    def _():
        wb_ref[...] = jnp.zeros_like(wb_ref)
        for src, dst, size in row_groups:
            wb_ref[dst:dst + size, :] = wt_ref[src:src + size, :].astype(BF16)

    def project(x_ref, g):
        h = (_rms(x_ref[...]) * (1.0 + sc_ref[pl.ds(g, 1), :]) + sh_ref[pl.ds(g, 1), :]).astype(BF16)
        return _nt(h, wb_ref[...])

    @pl.when(i < n_c)
    def _():
        y = project(xc_ref, 0)
        oc_ref[...] = y.astype(oc_ref.dtype)
        keep_ref[...] = y[:, keep[0]:keep[0] + keep[1]]

    @pl.when(i >= n_c)
    def _():
        os_ref[...] = project(xs_ref, 1 + (i - n_c) // (MOD_ROWS // TM_IN)).astype(os_ref.dtype)


def _in_proj(xc, xs, mod, layer, wt, w_layer, row_groups, n, keep):
    n_c, n_s = xc.shape[0] // TM_IN, xs.shape[0] // TM_IN
    c_idx = lambda i: (jnp.minimum(i, n_c - 1), 0)
    s_idx = lambda i: (jnp.maximum(i - n_c, 0), 0)
    return pl.pallas_call(
        functools.partial(_in_proj_kernel, row_groups=row_groups, keep=keep, n_c=n_c),
        grid=(n_c + n_s,),
        in_specs=[pl.BlockSpec((TM_IN, D_MODEL), c_idx), pl.BlockSpec((TM_IN, D_MODEL), s_idx),
                  pl.BlockSpec((None, None, 8, D_MODEL), lambda i: (layer, 0, 0, 0)),
                  pl.BlockSpec((None, None, 8, D_MODEL), lambda i: (layer, 1, 0, 0)),
                  pl.BlockSpec((None, wt.shape[1], D_MODEL), lambda i: (w_layer, 0, 0), pipeline_mode=pl.Buffered(1))],
        out_specs=[pl.BlockSpec((TM_IN, n), c_idx), pl.BlockSpec((TM_IN, keep[1]), c_idx), pl.BlockSpec((TM_IN, n), s_idx)],
        out_shape=[jax.ShapeDtypeStruct((xc.shape[0], n), BF16), jax.ShapeDtypeStruct((xc.shape[0], keep[1]), F32),
                   jax.ShapeDtypeStruct((xs.shape[0], n), BF16)],
        scratch_shapes=[pltpu.VMEM((n, D_MODEL), BF16)],
        compiler_params=_params(1),
        name="norm_mod_proj",
    )(xc, xs, mod, mod, wt)


def _ffn_kernel(x_ref, sh_ref, sc_ref, gate_ref, wg_ref, wu_ref, wd_ref, *refs, row0, rstep, final):
    (gf_ref, o_ref, h_ref) = refs if final else (None,) + refs
    n_sub = x_ref.shape[0] // MOD_ROWS
    subs = [(slice(s * MOD_ROWS, (s + 1) * MOD_ROWS), _mod_row(row0, rstep, x_ref.shape[0], s)) for s in range(n_sub)]

    @pl.when(pl.program_id(1) == 0)
    def _():
        for rows, g in subs:
            x = x_ref[rows, :]
            o_ref[rows, :] = x
            h_ref[rows, :] = (_rms(x) * (1.0 + sc_ref[pl.ds(g, 1), :]) + sh_ref[pl.ds(g, 1), :]).astype(BF16)

    wg = wg_ref[...].astype(BF16)
    wu = wu_ref[...].astype(BF16)
    wd = wd_ref[...].astype(BF16)
    for rows, g in subs:
        h = h_ref[rows, :]
        a = _mm(h, wg)
        act = (a * jax.nn.sigmoid(a) * _mm(h, wu)).astype(BF16)
        o_ref[rows, :] += gate_ref[pl.ds(g, 1), :] * _mm(act, wd)

    if final:
        @pl.when(pl.program_id(1) == pl.num_programs(1) - 1)
        def _():
            for rows, _ in subs:
                o_ref[rows, :] = _rms(o_ref[rows, :]) * gf_ref[...]


def _ffn(x, mod, layer, w_in, w_out, row0, rstep, final_gain=None):
    m = x.shape[0]
    nj = FFN_H // FFN_TN
    mod_spec = lambda k: pl.BlockSpec((None, None, 8, D_MODEL), lambda i, j: (layer, k, 0, 0))
    final = final_gain is not None
    extra_specs = [pl.BlockSpec((1, D_MODEL), lambda i, j: (0, 0))] if final else []
    extra_args = [final_gain.reshape(1, D_MODEL)] if final else []
    return pl.pallas_call(
        functools.partial(_ffn_kernel, row0=row0, rstep=rstep, final=final),
        grid=(m // TM_FFN, nj),
        in_specs=[pl.BlockSpec((TM_FFN, D_MODEL), lambda i, j: (i, 0)), mod_spec(3), mod_spec(4), mod_spec(5),
                  pl.BlockSpec((None, D_MODEL, FFN_TN), lambda i, j: (layer, 0, j)),
                  pl.BlockSpec((None, D_MODEL, FFN_TN), lambda i, j: (layer, 0, j + nj)),
                  pl.BlockSpec((None, FFN_TN, D_MODEL), lambda i, j: (layer, j, 0))] + extra_specs,
        out_specs=pl.BlockSpec((TM_FFN, D_MODEL), lambda i, j: (i, 0)),
        out_shape=jax.ShapeDtypeStruct((m, D_MODEL), F32),
        scratch_shapes=[pltpu.VMEM((TM_FFN, D_MODEL), BF16)],
        compiler_params=_params(2),
        name="ffn_residual",
    )(x, mod, mod, mod, w_in, w_in, w_out, *extra_args)


def _proj_res_kernel(ac0_ref, ac1_ref, as0_ref, as1_ref, w0_ref, w1_ref, xc_ref, xs_ref, gate_ref, oc_ref, os_ref, *, n_c):
    i = pl.program_id(0)

    def mix(a0_ref, a1_ref, x_ref, g):
        acc = _mm(a0_ref[...], w0_ref[...].astype(BF16)) + _mm(a1_ref[...], w1_ref[...].astype(BF16))
        return x_ref[...] + gate_ref[pl.ds(g, 1), :] * acc

    @pl.when(i < n_c)
    def _():
        oc_ref[...] = mix(ac0_ref, ac1_ref, xc_ref, 0)

    @pl.when(i >= n_c)
    def _():
        os_ref[...] = mix(as0_ref, as1_ref, xs_ref, 1 + (i - n_c))


def _out_proj(acts_c, acts_s, w, w_layer, xc, xs, mod, layer, k_gate):
    n_c, n_s = xc.shape[0] // TM, xs.shape[0] // TM
    kw = acts_c[0].shape[1]
    c_idx = lambda i: (jnp.minimum(i, n_c - 1), 0)
    s_idx = lambda i: (jnp.maximum(i - n_c, 0), 0)
    w_specs = [pl.BlockSpec((None, kw, D_MODEL), functools.partial(lambda i, p: (w_layer, p, 0), p=p),
                            pipeline_mode=pl.Buffered(1)) for p in range(2)]
    return pl.pallas_call(
        functools.partial(_proj_res_kernel, n_c=n_c),
        grid=(n_c + n_s,),
        in_specs=[pl.BlockSpec((TM, kw), c_idx)] * 2 + [pl.BlockSpec((TM, kw), s_idx)] * 2 + w_specs + [
            pl.BlockSpec((TM, D_MODEL), c_idx), pl.BlockSpec((TM, D_MODEL), s_idx),
            pl.BlockSpec((None, None, 8, D_MODEL), lambda i: (layer, k_gate, 0, 0)),
        ],
        out_specs=[pl.BlockSpec((TM, D_MODEL), c_idx), pl.BlockSpec((TM, D_MODEL), s_idx)],
        out_shape=[jax.ShapeDtypeStruct(xc.shape, F32), jax.ShapeDtypeStruct(xs.shape, F32)],
        compiler_params=_params(1),
        name="out_proj_residual",
    )(*acts_c, *acts_s, w, w, xc, xs, mod)


def _gqa_kernel(*refs, sample, has_prev=False, slot=0):
    if sample:
        q_ref, k_ref, v_ref, gq_ref, gk_ref, ck_ref, cv_ref, cos_ref, sin_ref, o_ref, kb_ref, vb_ref = refs
    else:
        n_in = 8 if has_prev else 6
        q_ref, k_ref, v_ref, gq_ref, gk_ref, vf_ref = refs[:6]
        o_ref, kc_ref, vc_ref, kb_ref, vb_ref = refs[n_in:]
        if not has_prev:
            stacks = (kc_ref, vc_ref)
            kc_ref, vc_ref = kc_ref.at[slot], vc_ref.at[slot]
    qi = pl.program_id(1)
    n_new = k_ref.shape[0]
    past = PAST_LEN if sample else 0
    rep = H_B // KV_B

    @pl.when(qi == 0)
    def _():
        if not (sample or has_prev):
            for ref in stacks:
                for other in range(ref.shape[0]):
                    if other != slot:
                        ref[other] = jnp.zeros(ref.shape[1:], ref.dtype)
        for g in range(KV_B):
            sl = slice(HD_B * g, HD_B * (g + 1))
            kn = _rms(k_ref[:, sl].astype(F32)) * gk_ref[...]
            if sample:
                kb_ref[0:past, sl] = ck_ref[:, g, :].astype(BF16)
                vb_ref[g, 0:past, 0:HD_B] = cv_ref[:, g, :].astype(BF16)
                kn = kn * cos_ref[...] + pltpu.roll(kn, HD_B // 2, 1) * sin_ref[...]
            else:
                kc_ref[:, g, :] = kn
                vc_ref[:, g, :] = vf_ref[:, sl]
            kb_ref[past:past + n_new, sl] = kn.astype(BF16)
            vb_ref[g, past:past + n_new, 0:HD_B] = v_ref[:, sl].astype(BF16)
            vb_ref[g, :, HD_B:] = jnp.ones((past + n_new, HD_B), BF16)

    r0 = pl.multiple_of(qi * QB, QB)
    qs = []
    for h in range(H_B):
        qn = _rms(q_ref[:, HD_B * h:HD_B * (h + 1)].astype(F32)) * gq_ref[...]
        if sample:
            qn = qn * cos_ref[pl.ds(r0, QB), :] + pltpu.roll(qn, HD_B // 2, 1) * sin_ref[pl.ds(r0, QB), :]
        qs.append((qn * (HD_B ** -0.5)).astype(BF16))
    scores = [_nt(qs[h], kb_ref[:, HD_B * (h // rep):HD_B * (h // rep + 1)]) for h in range(H_B)]
    weights = [jnp.exp(s - jnp.max(s, axis=-1, keepdims=True)).astype(BF16) for s in scores]
    sums = [_mm(weights[h], vb_ref[h // rep]) for h in range(H_B)]
    for h in range(H_B):
        o_ref[:, HD_B * h:HD_B * (h + 1)] = (sums[h][:, :HD_B] / sums[h][:, HD_B:]).astype(o_ref.dtype)


def _gqa(proj, g_q, g_k, n_batch, seq, ctx=None, rope=None, v_f32=None, slot=0, prev=None):
    sample = ctx is not None
    m = n_batch * seq
    nq = seq // QB
    n_even = (DEPTH + 1) // 2
    in_specs = [
        pl.BlockSpec((QB, 512), lambda b, i: (b * nq + i, 3)),
        pl.BlockSpec((seq, 256), lambda b, i: (b, 8)),
        pl.BlockSpec((seq, 256), lambda b, i: (b, 9)),
        pl.BlockSpec((1, HD_B), lambda b, i: (0, 0)),
        pl.BlockSpec((1, HD_B), lambda b, i: (0, 0)),
    ]
    args = [proj, proj, proj, g_q.reshape(1, HD_B), g_k.reshape(1, HD_B)]
    o_spec = pl.BlockSpec((QB, 512), lambda b, i: (b * nq + i, 0))
    o_shape = jax.ShapeDtypeStruct((m, 512), BF16)
    aliases = {}
    if sample:
        cache_spec = pl.BlockSpec((None, None, PAST_LEN, KV_B, HD_B), lambda b, i: (b, slot, 0, 0, 0))
        in_specs += [
            cache_spec, cache_spec,
            pl.BlockSpec((seq, HD_B), lambda b, i: (0, 0)),
            pl.BlockSpec((seq, HD_B), lambda b, i: (0, 0)),
        ]
        args += [ctx[0], ctx[1], rope[0], rope[1]]
        out_specs, out_shape = o_spec, o_shape
    else:
        in_specs.append(pl.BlockSpec((seq, KV_B * HD_B), lambda b, i: (b, 0)))
        args.append(v_f32)
        if prev is not None:
            in_specs += [pl.BlockSpec(memory_space=pl.ANY)] * 2
            aliases = {len(args): 1, len(args) + 1: 2}
            args += list(prev)
        if prev is None:
            new_spec = pl.BlockSpec((None, n_even, seq, KV_B, HD_B), lambda b, i: (b, 0, 0, 0, 0))
        else:
            new_spec = pl.BlockSpec((None, None, seq, KV_B, HD_B), lambda b, i: (b, slot, 0, 0, 0))
        new_shape = jax.ShapeDtypeStruct((n_batch, n_even, seq, KV_B, HD_B), F32)
        out_specs, out_shape = [o_spec, new_spec, new_spec], [o_shape, new_shape, new_shape]
    n_keys = seq + (PAST_LEN if sample else 0)
    return pl.pallas_call(
        functools.partial(_gqa_kernel, sample=sample, has_prev=prev is not None, slot=slot),
        grid=(n_batch, nq),
        in_specs=in_specs,
        out_specs=out_specs,
        out_shape=out_shape,
        input_output_aliases=aliases,
        scratch_shapes=[pltpu.VMEM((n_keys, KV_B * HD_B), BF16), pltpu.VMEM((KV_B, n_keys, 2 * HD_B), BF16)],
        compiler_params=_params(2),
        name="gqa_sample" if sample else "gqa_prompt",
    )(*args)


MLA_QW = 2 * LANES
MLA_HW = 4 * LANES


def _rotate_pairs(x, cos_t, sin_lo, sin_hi):
    w = x.shape[1]
    return x * cos_t + pltpu.roll(x, ROPE_D // 2, 1) * sin_hi + pltpu.roll(x, w - ROPE_D // 2, 1) * sin_lo


def _mla_kernel(*refs, sample):
    if sample:
        (cq_ref, ckv_ref, kpe_ref, gq_ref, wqb_ref, gkv_ref, wkvb_ref, cckv_ref, ckpe_ref,
         qc_ref, ql_ref, qh_ref, kc_ref, kl_ref, kh_ref, o_ref, kv_s) = refs
    else:
        cq_ref, ckv_ref, kpe_ref, gq_ref, wqb_ref, gkv_ref, wkvb_ref, o_ref, ckvn_ref, kv_s = refs
    qi = pl.program_id(1)
    n_new = ckv_ref.shape[0]
    past = PAST_LEN if sample else 0

    def stage_kv(rows, kv, kpe_block):
        for h in range(H_D):
            kv_s[rows, MLA_HW * h:MLA_HW * h + NOPE_D] = kv[:, 256 * h:256 * h + NOPE_D].astype(BF16)
            kv_s[rows, MLA_HW * h + NOPE_D:MLA_HW * h + MLA_QW] = kpe_block
            kv_s[rows, MLA_HW * h + MLA_QW:MLA_HW * h + MLA_QW + V_D] = kv[:, 256 * h + NOPE_D:256 * (h + 1)].astype(BF16)

    @pl.when(qi == 0)
    def _():
        wkvb = wkvb_ref[...].astype(BF16)
        ckvn = _rms(ckv_ref[...].astype(F32)) * gkv_ref[...]
        if not sample:
            ckvn_ref[...] = ckvn
        kpe = kpe_ref[...]
        if sample:
            ctx_kpe = jnp.concatenate([ckpe_ref[...], jnp.zeros((past, LANES - ROPE_D), F32)], axis=1)
            stage_kv(slice(0, past), _mm(cckv_ref[...].astype(BF16), wkvb), ctx_kpe.astype(BF16))
            kpe = _rotate_pairs(kpe.astype(F32), kc_ref[...], kl_ref[...], kh_ref[...]).astype(BF16)
        stage_kv(slice(past, past + n_new), _mm(ckvn.astype(BF16), wkvb), kpe)
        for h in range(H_D):
            kv_s[:, MLA_HW * h + MLA_QW + V_D:MLA_HW * (h + 1)] = jnp.ones((past + n_new, V_D), BF16)

    q = _mm((_rms(cq_ref[...].astype(F32)) * gq_ref[...]).astype(BF16), wqb_ref[...].astype(BF16))
    q = q * ((NOPE_D + ROPE_D) ** -0.5)
    qs = []
    for h in range(H_D):
        q_h = q[:, MLA_QW * h:MLA_QW * (h + 1)]
        if sample:
            rows = pl.ds(pl.multiple_of(qi * QB, QB), QB)
            q_h = _rotate_pairs(q_h, qc_ref[rows, :], ql_ref[rows, :], qh_ref[rows, :])
        qs.append(q_h.astype(BF16))
    scores = [_nt(qs[h], kv_s[:, MLA_HW * h:MLA_HW * h + MLA_QW]) for h in range(H_D)]
    weights = [jnp.exp(s - jnp.max(s, axis=-1, keepdims=True)).astype(BF16) for s in scores]
    sums = [_mm(weights[h], kv_s[:, MLA_HW * h + MLA_QW:MLA_HW * (h + 1)]) for h in range(H_D)]
    for h in range(H_D):
        o_ref[:, V_D * h:V_D * (h + 1)] = (sums[h][:, :V_D] / sums[h][:, V_D:]).astype(o_ref.dtype)


def _mla(proj, g_q, w_qb, g_kv, w_kvb, n_batch, seq, ctx=None, rope=None):
    sample = ctx is not None
    m = n_batch * seq
    nq = seq // QB
    in_specs = [
        pl.BlockSpec((QB, Q_RANK), lambda b, i: (b * nq + i, 6)),
        pl.BlockSpec((seq, KV_RANK), lambda b, i: (b, 14)),
        pl.BlockSpec((seq, LANES), lambda b, i: (b, 15)),
        pl.BlockSpec((1, Q_RANK), lambda b, i: (0, 0)),
        pl.BlockSpec((Q_RANK, H_D * MLA_QW), lambda b, i: (0, 0)),
        pl.BlockSpec((1, KV_RANK), lambda b, i: (0, 0)),
        pl.BlockSpec((KV_RANK, 1024), lambda b, i: (0, 0)),
    ]
    args = [proj, proj, proj, g_q.reshape(1, Q_RANK), w_qb, g_kv.reshape(1, KV_RANK), w_kvb]
    o_spec = pl.BlockSpec((QB, 512), lambda b, i: (b * nq + i, 0))
    o_shape = jax.ShapeDtypeStruct((m, 512), BF16)
    if sample:
        in_specs += [
            pl.BlockSpec((None, PAST_LEN, KV_RANK), lambda b, i: (b, 0, 0)),
            pl.BlockSpec((None, PAST_LEN, ROPE_D), lambda b, i: (b, 0, 0)),
        ] + [pl.BlockSpec((seq, MLA_QW), lambda b, i: (0, 0))] * 3 + [pl.BlockSpec((seq, LANES), lambda b, i: (0, 0))] * 3
        args += [ctx[0], ctx[1], *rope]
        out_specs, out_shape = o_spec, o_shape
    else:
        out_specs = [o_spec, pl.BlockSpec((seq, KV_RANK), lambda b, i: (b, 0))]
        out_shape = [o_shape, jax.ShapeDtypeStruct((m, KV_RANK), F32)]
    n_keys = seq + (PAST_LEN if sample else 0)
    return pl.pallas_call(
        functools.partial(_mla_kernel, sample=sample),
        grid=(n_batch, nq),
        in_specs=in_specs,
        out_specs=out_specs,
        out_shape=out_shape,
        scratch_shapes=[pltpu.VMEM((n_keys, H_D * MLA_HW), BF16)],
        compiler_params=_params(2),
        name="mla_sample" if sample else "mla_prompt",
    )(*args)


def _dft(table, x):
    return _mm(table.astype(BF16), x.astype(BF16))


def _filter_kernel(z_ref, wf1_ref, bf1_ref, fr_ref, wf2_ref, bf2_ref, wf3_ref, t_ref, dl_ref,
                   c_ref, s_ref, gre_ref, gim_ref):
    n_tok = z_ref.shape[0]
    fr = fr_ref[...]
    hid = jnp.sin(fr * (_mm(z_ref[...].astype(BF16), wf1_ref[...].astype(BF16)) + bf1_ref[...]))
    hid = jnp.sin(fr * (_mm(hid.astype(BF16), wf2_ref[...].astype(BF16)) + bf2_ref[...]))
    filt = _mm(hid.astype(BF16), wf3_ref[...].astype(BF16))
    decay = jnp.exp(-t_ref[...] * dl_ref[...])
    row = lax.broadcasted_iota(jnp.int32, (n_tok, 1), 0)
    h_f = filt[:, :HY_W] * decay
    h_b = jnp.where(row == 0, 0.0, filt[:, HY_W:] * decay)
    p, m = h_f + h_b, h_f - h_b
    g_re = _dft(c_ref[...], p)
    g_im = _dft(s_ref[...], m)
    sign = jnp.where(row % 2 == 0, 1.0, -1.0)
    nyquist = jnp.sum(p * sign, axis=0, keepdims=True)
    g_im = jnp.where(row == 0, nyquist, g_im)
    wk = jnp.where(row == 0, 0.5 / n_tok, 1.0 / n_tok)
    gre_ref[...] = g_re * wk
    gim_ref[...] = g_im * wk


def _filter_spectrum(z, wf1, bf1, freq, wf2, bf2, wf3, t_col, deltas, tabs):
    n_tok = z.shape[0]
    out = jax.ShapeDtypeStruct((n_tok, HY_W), F32)
    return pl.pallas_call(
        _filter_kernel,
        out_shape=[out, out],
        compiler_params=pltpu.CompilerParams(vmem_limit_bytes=VMEM_LIMIT),
        name="hyena_filter",
    )(z, wf1, bf1.reshape(1, FILT_HID), freq.reshape(1, FILT_HID), wf2, bf2.reshape(1, FILT_HID), wf3,
      t_col, deltas, tabs[0], tabs[1])


HY_ROWS = 1024


def _hyena_channels(seq):
    return HY_W if seq <= 256 else HY_W // 2


def _hyena_kernel(u0_ref, u1_ref, u2_ref, w0_ref, w1_ref, w2_ref, b0_ref, b1_ref, b2_ref, skip_ref,
                  gre_ref, gim_ref, cf_ref, sf_ref, stf_ref, o_ref, c_ref, s_ref, st_ref):
    seq = c_ref.shape[0]
    n_rows = u0_ref.shape[0]
    n_seq = n_rows // seq
    pos = lax.broadcasted_iota(jnp.int32, (n_rows, 1), 0) % seq

    @pl.when((pl.program_id(0) == 0) & (pl.program_id(1) == 0))
    def _():
        c_ref[...] = cf_ref[...].astype(BF16)
        s_ref[...] = sf_ref[...].astype(BF16)
        st_ref[...] = stf_ref[...].astype(BF16)

    def short_conv(u_ref, w_ref, b_ref):
        x, w = u_ref[...].astype(F32), w_ref[...]
        prev = jnp.where(pos == 0, 0.0, pltpu.roll(x, 1, 0))
        nxt = jnp.where(pos == seq - 1, 0.0, pltpu.roll(x, n_rows - 1, 0))
        return prev * w[0:1] + x * w[1:2] + nxt * w[2:3] + b_ref[...]

    def side_by_side(a):
        return a if n_seq == 1 else jnp.concatenate([a[s * seq:(s + 1) * seq] for s in range(n_seq)], axis=1)

    def stacked(a):
        ct = a.shape[1] // n_seq
        return a if n_seq == 1 else jnp.concatenate([a[:, s * ct:(s + 1) * ct] for s in range(n_seq)], axis=0)

    x0 = short_conv(u0_ref, w0_ref, b0_ref)
    gv = short_conv(u1_ref, w1_ref, b1_ref) * short_conv(u2_ref, w2_ref, b2_ref)
    sig = side_by_side(gv).astype(BF16)
    u_re = _mm(c_ref[...], sig)
    u_im = _mm(s_ref[...], sig)
    g_re = jnp.concatenate([gre_ref[...]] * n_seq, axis=1)
    g_im = jnp.concatenate([gim_ref[...]] * n_seq, axis=1)
    bin0 = lax.broadcasted_iota(jnp.int32, (seq, 1), 0) == 0
    p_im = u_im * g_im
    y_re = u_re * g_re - jnp.where(bin0, 0.0, p_im)
    y_im = jnp.where(bin0, p_im, u_re * g_im + u_im * g_re)
    y = stacked(_mm(c_ref[...], y_re.astype(BF16)) + _mm(st_ref[...], y_im.astype(BF16)))
    o_ref[...] = (x0 * (y + gv * skip_ref[...])).astype(o_ref.dtype)


def _hyena(proj, w_conv, b_conv, skip, g_re, g_im, tabs, n_batch, seq):
    ct = _hyena_channels(seq)
    nct = HY_W // ct
    u_specs = [pl.BlockSpec((HY_ROWS, ct), functools.partial(lambda b, c, g: (b, g * nct + c), g=g)) for g in range(3)]
    w_specs = [pl.BlockSpec((3, ct), functools.partial(lambda b, c, g: (0, g * nct + c), g=g)) for g in range(3)]
    b_specs = [pl.BlockSpec((1, ct), functools.partial(lambda b, c, g: (0, g * nct + c), g=g)) for g in range(3)]
    tab_spec = pl.BlockSpec((seq, seq), lambda b, c: (0, 0))
    return pl.pallas_call(
        _hyena_kernel,
        grid=(n_batch * seq // HY_ROWS, nct),
        in_specs=u_specs + w_specs + b_specs + [
            pl.BlockSpec((1, ct), lambda b, c: (0, c)),
            pl.BlockSpec((seq, ct), lambda b, c: (0, c)),
            pl.BlockSpec((seq, ct), lambda b, c: (0, c)),
        ] + [tab_spec] * 3,
        out_specs=pl.BlockSpec((HY_ROWS, ct), lambda b, c: (b, c)),
        out_shape=jax.ShapeDtypeStruct((n_batch * seq, HY_W), BF16),
        scratch_shapes=[pltpu.VMEM((seq, seq), BF16)] * 3,
        compiler_params=_params(2),
        name="hyena_conv",
    )(proj, proj, proj, w_conv, w_conv, w_conv, b_conv, b_conv, b_conv, skip.reshape(1, HY_W), g_re, g_im, *tabs)


def _dft_tables(n_tok):
    k = np.arange(n_tok)[:, None]
    s = np.arange(n_tok)[None, :]
    ang = ((k * s) % (2 * n_tok)) * (np.pi / n_tok)
    cos_t = np.cos(ang)
    sin_f = np.where(k == 0, np.where(s % 2 == 0, 1.0, -1.0), -np.sin(ang))
    return [jnp.asarray(t, F32) for t in (cos_t, sin_f, sin_f.T)]


GLA_LEVELS = (32, 16, 8, 4, 2, 1)
GLA_SAFE_DECAY = 60.0
GLA_GROUP = 2


def _gla_constants():
    c = GLA_CHUNK
    idx = np.arange(c)
    i, t = idx[:, None], idx[None, :]
    masks = []
    for s in GLA_LEVELS:
        upper = (idx % (2 * s)) >= s
        masks.append(((i // (2 * s)) == (t // (2 * s))) & upper[:, None] & (~upper)[None, :])
    masks.append(i == t)
    tri = t <= i
    fwd_m = np.stack([np.tile(m, (H_A, 1)) for m in masks]).astype(np.float32)
    bwd_m = np.stack([np.tile(m[::-1, ::-1], (H_A, 1)) for m in masks]).astype(np.float32)
    head_of_row = np.repeat(np.arange(H_A), c)[:, None]
    head_of_lane = np.repeat(np.arange(H_A), DK_A)[None, :]
    head_mask = head_of_row == head_of_lane
    return (jnp.asarray(tri, BF16), jnp.asarray(tri[::-1, ::-1], BF16), jnp.asarray(fwd_m), jnp.asarray(bwd_m),
            jnp.asarray(head_mask, BF16))


def _pair_reference(b, s, backward, row):
    c = GLA_CHUNK
    ref = s if backward else s - 1
    if 2 * s >= 8:
        pieces = [jnp.broadcast_to(b[p * 2 * s + ref:p * 2 * s + ref + 1, :], (2 * s, b.shape[1]))
                  for p in range(c // (2 * s))]
        return pieces[0] if len(pieces) == 1 else jnp.concatenate(pieces, axis=0)
    pos = row % (2 * s)
    out = None
    for o in range(2 * s):
        d = ref - o
        shifted = b if d == 0 else pltpu.roll(b, (-d) % c, 0)
        out = shifted if out is None else jnp.where(pos == o, shifted, out)
    return out


def _chunk_log_decay(la, t_ref):
    l1 = la.astype(BF16)
    r1 = la - l1.astype(F32)
    l2 = r1.astype(BF16)
    l3 = (r1 - l2.astype(F32)).astype(BF16)
    tmat = t_ref[...]
    return _mm(tmat, l1) + _mm(tmat, l2) + _mm(tmat, l3)


def _stack_heads(a, hm):
    ab = a.astype(BF16)
    return jnp.concatenate([ab] * H_A, axis=0) * hm


def _state_terms(k, v, b, b_last):
    c = GLA_CHUNK
    k_rest = (k * jnp.exp(b_last - b)).T
    carry = jnp.broadcast_to(jnp.exp(b_last), (2 * c, b.shape[1])).T
    return k_rest.astype(BF16), carry


def _gla_chunk(q, k, v, la, t_ref, m_ref, hm, s_ref, backward):
    c = GLA_CHUNK
    b = _chunk_log_decay(la, t_ref)
    row = lax.broadcasted_iota(jnp.int32, (c, 1), 0)
    last = 0 if backward else c - 1
    b_last = b[last:last + 1, :]
    scores = _nt(_stack_heads(q, hm), k.astype(BF16)) * m_ref[len(GLA_LEVELS)]
    for lvl, s in enumerate(GLA_LEVELS):
        is_query = (row % (2 * s) < s) if backward else (row % (2 * s) >= s)
        delta = b - _pair_reference(b, s, backward, row)
        x = jnp.exp(jnp.where(is_query, delta, -delta))
        scores = scores + _nt(_stack_heads(q * x, hm), (k * x).astype(BF16)) * m_ref[lvl]
    scores = scores.astype(BF16)
    state = s_ref[...]
    inter = _mm(_stack_heads(q * jnp.exp(b), hm), state.astype(BF16))
    k_rest, carry = _state_terms(k, v, b, b_last)
    outs = []
    for h in range(H_A):
        rows = slice(c * h, c * (h + 1))
        v_h = v[:, DV_A * h:DV_A * (h + 1)]
        outs.append(_mm(scores[rows], v_h) + inter[rows])
        s_ref[rows, :] = state[rows] * carry[rows] + _mm(k_rest[rows], v_h)
    return jnp.concatenate(outs, axis=1)


def _gla_local(items, hm):
    c = GLA_CHUNK
    bs = [_chunk_log_decay(la, t_ref) for _, _, _, la, t_ref, _, _ in items]
    b_lasts = [b[(0 if it[6] else c - 1):(0 if it[6] else c - 1) + 1, :] for b, it in zip(bs, items)]
    q_decayed = [_stack_heads(it[0] * jnp.exp(b), hm) for it, b in zip(items, bs)]
    k_grown = [(it[1] * jnp.exp(-b)).astype(BF16) for it, b in zip(items, bs)]
    raw = [_nt(qd, kg) for qd, kg in zip(q_decayed, k_grown)]
    masked = [r * it[5] for r, it in zip(raw, items)]
    terms = [_state_terms(it[1], it[2], b, bl) for it, b, bl in zip(items, bs, b_lasts)]
    out = []
    for n in range(0, len(items), 2):
        v = items[n][2]
        heads = [(slice(c * h, c * (h + 1)), v[:, DV_A * h:DV_A * (h + 1)]) for h in range(H_A)]
        both = (masked[n] + masked[n + 1]).astype(BF16)
        intra = jnp.concatenate([_mm(both[rows], v_h) for rows, v_h in heads], axis=1)
        for m in (n, n + 1):
            k_rest, carry = terms[m]
            incr = jnp.concatenate([_mm(k_rest[rows], v_h) for rows, v_h in heads], axis=0)
            out.append((intra if m == n else None, q_decayed[m], incr, carry))
    return out


def _gla_kernel(*refs, sample):
    if sample:
        (x_ref, z_ref, wf_ref, bf_ref, wb_ref, bb_ref, tf_ref, tb_ref, mf_ref, mb_ref, hm_ref, gn_ref, sf0_ref, sb0_ref,
         o_ref, la_f, la_b, o_f, o_b, s_f, s_b, qd_f, qd_b, ds_f, ds_b, cr_f, cr_b) = refs
    else:
        (x_ref, z_ref, wf_ref, bf_ref, wb_ref, bb_ref, tf_ref, tb_ref, mf_ref, mb_ref, hm_ref, gn_ref,
         o_ref, sf_out, sb_out, la_f, la_b, o_f, o_b, s_f, s_b, qd_f, qd_b, ds_f, ds_b, cr_f, cr_b) = refs
    n_tok = x_ref.shape[0]
    n_chunks = n_tok // GLA_CHUNK
    hk, hv = H_A * DK_A, H_A * DV_A
    zb = z_ref[...].astype(BF16)

    def log_sigmoid(t):
        return jnp.minimum(t, 0.0) - jnp.log(1.0 + jnp.exp(-jnp.abs(t)))

    la_f[...] = log_sigmoid(_mm(zb, wf_ref[...].astype(BF16)) + bf_ref[...]) / GLA_TAU
    la_b[...] = log_sigmoid(_mm(zb, wb_ref[...].astype(BF16)) + bb_ref[...]) / GLA_TAU
    if sample:
        s_f[...] = sf0_ref[...]
        s_b[...] = sb0_ref[...]
    else:
        s_f[...] = jnp.zeros_like(s_f)
        s_b[...] = jnp.zeros_like(s_b)
    hm = hm_ref[...]

    fwd = (la_f, tf_ref, mf_ref, s_f, o_f, qd_f, ds_f, cr_f, False)
    bwd = (la_b, tb_ref, mb_ref, s_b, o_b, qd_b, ds_b, cr_b, True)
    tri_f = jnp.sum(mf_ref[...], axis=0)
    tri_b = jnp.sum(mb_ref[...], axis=0)

    def chunk_rows(ci, backward):
        cidx = n_chunks - 1 - ci if backward else ci
        return cidx, pl.ds(pl.multiple_of(cidx * GLA_CHUNK, GLA_CHUNK), GLA_CHUNK)

    def load_qkv(rows):
        q = x_ref[rows, 0:hk].astype(F32) * (DK_A ** -0.5)
        return q, x_ref[rows, hk:2 * hk].astype(F32), x_ref[rows, 2 * hk:2 * hk + hv]

    def safe_step(ci, carry):
        for la_ref, t_ref, m_ref, s_ref, out_ref, _, _, _, backward in (fwd, bwd):
            _, rows = chunk_rows(ci, backward)
            out_ref[rows, :] = _gla_chunk(*load_qkv(rows), la_ref[rows, :], t_ref, m_ref, hm, s_ref, backward)
        return carry

    def local_step(gi, carry):
        items, dests = [], []
        for u in range(GLA_GROUP):
            cidx = gi * GLA_GROUP + u
            rows = pl.ds(pl.multiple_of(cidx * GLA_CHUNK, GLA_CHUNK), GLA_CHUNK)
            qkv = load_qkv(rows)
            for (la_ref, t_ref, _, _, out_ref, qd_ref, ds_ref, cr_ref, backward), tri in ((fwd, tri_f), (bwd, tri_b)):
                items.append((*qkv, la_ref[rows, :], t_ref, tri, backward))
                dests.append((out_ref, rows, qd_ref, ds_ref, cr_ref, cidx))
        for (out_ref, rows, qd_ref, ds_ref, cr_ref, cidx), (intra, qd, incr, factor) in zip(dests, _gla_local(items, hm)):
            out_ref[rows, :] = jnp.zeros((GLA_CHUNK, hv), F32) if intra is None else intra
            qd_ref[cidx] = qd
            ds_ref[cidx] = incr
            cr_ref[cidx] = factor
        return carry

    def scan_step(ci, carry):
        for _, _, _, s_ref, out_ref, qd_ref, ds_ref, cr_ref, backward in (fwd, bwd):
            cidx, rows = chunk_rows(ci, backward)
            state = s_ref[...]
            inter = _mm(qd_ref[cidx], state.astype(BF16))
            out_ref[rows, :] += jnp.concatenate(
                [inter[GLA_CHUNK * h:GLA_CHUNK * (h + 1)] for h in range(H_A)], axis=1)
            s_ref[...] = state * cr_ref[cidx] + ds_ref[cidx]
        return carry

    chunk_sums = [jnp.sum(ref[...].reshape(n_chunks, GLA_CHUNK, hk), axis=1) for ref in (la_f, la_b)]
    mild = jnp.minimum(jnp.min(chunk_sums[0]), jnp.min(chunk_sums[1])) > -GLA_SAFE_DECAY

    @pl.when(mild)
    def _():
        lax.fori_loop(0, n_chunks // GLA_GROUP, local_step, 0, unroll=2)
        lax.fori_loop(0, n_chunks, scan_step, 0, unroll=2)

    @pl.when(jnp.logical_not(mild))
    def _():
        lax.fori_loop(0, n_chunks, safe_step, 0)
    if not sample:
        sf_out[...] = s_f[...]
        sb_out[...] = s_b[...]
    gain = gn_ref[...]
    for h in range(H_A):
        cols = slice(DV_A * h, DV_A * (h + 1))
        r = x_ref[:, 2 * hk + hv + DV_A * h:2 * hk + hv + DV_A * (h + 1)].astype(F32)
        o_ref[:, cols] = (_rms(o_f[:, cols] + o_b[:, cols]) * gain * (r * jax.nn.sigmoid(r))).astype(o_ref.dtype)


def _gla(proj, w_gf, b_gf, w_gb, b_gb, g_norm, consts, n_batch, seq, ctx=None):
    sample = ctx is not None
    hk, hv = H_A * DK_A, H_A * DV_A
    n_ch = seq // GLA_CHUNK
    full = lambda shape: pl.BlockSpec(shape, lambda b: (0,) * len(shape))
    in_specs = [
        pl.BlockSpec((seq, 2 * hk + 2 * hv), lambda b: (b, 0)),
        pl.BlockSpec((seq, LANES), lambda b: (b, EVEN_W // LANES - 1)),
        full((LANES, hk)), full((1, hk)), full((LANES, hk)), full((1, hk)),
        full(consts[0].shape), full(consts[1].shape), full(consts[2].shape), full(consts[3].shape), full(consts[4].shape),
        full((1, DV_A)),
    ]
    args = [proj, proj, w_gf, b_gf.reshape(1, hk), w_gb, b_gb.reshape(1, hk), *consts, g_norm.reshape(1, DV_A)]
    o_spec = pl.BlockSpec((seq, hv), lambda b: (b, 0))
    o_shape = jax.ShapeDtypeStruct((n_batch * seq, hv), BF16)
    st_spec = pl.BlockSpec((None, hk, DV_A), lambda b: (b, 0, 0))
    if sample:
        in_specs += [st_spec, st_spec]
        args += [ctx[0], ctx[1]]
        out_specs, out_shape = o_spec, o_shape
    else:
        st_shape = jax.ShapeDtypeStruct((n_batch, hk, DV_A), F32)
        out_specs, out_shape = [o_spec, st_spec, st_spec], [o_shape, st_shape, st_shape]
    return pl.pallas_call(
        functools.partial(_gla_kernel, sample=sample),
        grid=(n_batch,),
        in_specs=in_specs,
        out_specs=out_specs,
        out_shape=out_shape,
        scratch_shapes=[pltpu.VMEM((seq, hk), F32), pltpu.VMEM((seq, hk), F32),
                        pltpu.VMEM((seq, hv), F32), pltpu.VMEM((seq, hv), F32),
                        pltpu.VMEM((hk, DV_A), F32), pltpu.VMEM((hk, DV_A), F32),
                        pltpu.VMEM((n_ch, H_A * GLA_CHUNK, hk), BF16), pltpu.VMEM((n_ch, H_A * GLA_CHUNK, hk), BF16),
                        pltpu.VMEM((n_ch, hk, DV_A), F32), pltpu.VMEM((n_ch, hk, DV_A), F32),
                        pltpu.VMEM((n_ch, hk, DV_A), F32), pltpu.VMEM((n_ch, hk, DV_A), F32)],
        compiler_params=_params(1),
        name="gla_sample" if sample else "gla_prompt",
    )(*args)


def _axial_rope(n_tokens, dim):
    rows = n_tokens // GRID_W
    row = np.repeat(np.arange(rows), GRID_W).astype(np.float64)
    col = np.tile(np.arange(GRID_W), rows).astype(np.float64)
    n_freq = dim // 4
    inv = ROPE_THETA ** (-np.arange(n_freq) / n_freq)
    ang = np.concatenate([row[:, None] * inv, col[:, None] * inv], axis=-1)
    return np.cos(ang).astype(np.float32), np.sin(ang).astype(np.float32)


def _filter_features(n_tokens):
    t = np.linspace(0.0, 1.0, n_tokens)[:, None]
    w = 2.0 * np.pi * np.arange(n_tokens)[:, None] / n_tokens
    f = np.linspace(1e-4, FILT_BANDS - 1, FILT_BANDS)[None, :]
    z = np.concatenate([t, np.cos(f * w), -np.sin(f * w)], axis=-1)
    z = np.pad(z, ((0, 0), (0, LANES - FILT_EMB)))
    return jnp.asarray(z, F32), jnp.asarray(t, F32)


_QB_ZERO = H_D * (NOPE_D + ROPE_D)
_QB_PERM = np.array([(NOPE_D + ROPE_D) * (p // MLA_QW) + p % MLA_QW if p % MLA_QW < NOPE_D + ROPE_D else _QB_ZERO
                     for p in range(H_D * MLA_QW)])


def _mla_rope_tables(cos_d, sin_d):
    n, half = cos_d.shape
    zeros = np.zeros((n, half), np.float32)

    def lanes(pre, width):
        pad = np.zeros((n, width - pre.shape[1] - 2 * half), np.float32)
        build = lambda first, second, lead: np.concatenate([lead, first, second, pad], axis=1)
        return (build(cos_d, cos_d, pre), build(-sin_d, zeros, 0 * pre), build(zeros, sin_d, 0 * pre))

    q_tabs = lanes(np.ones((n, NOPE_D), np.float32), MLA_QW)
    k_tabs = lanes(np.zeros((n, 0), np.float32), LANES)
    return tuple(jnp.asarray(t) for t in q_tabs + k_tabs)

EVEN_ROW_GROUPS = ((0, 0, 1536), (1568, 1536, 1024), (1536, EVEN_W - 2 * GATE_RANK, 2 * GATE_RANK))
ODD_ROW_GROUPS = ((0, 0, 1984),)
EVEN_KEEP = (2304, 256)
ODD_KEEP = (1920, LANES)


def kernel(x_prompt, x_sample, state_gla_fwd, state_gla_bwd, cache_gqa_k, cache_gqa_v, cache_mla_ckv, cache_mla_kpe, c, c_ctx, w_mod, b_mod, w_in_even, w_gla_gate_f, b_gla_gate_f, w_gla_gate_b, b_gla_gate_b, g_gla_norm, g_gqa_q, g_gqa_k, w_out_even, w_in_odd, w_hy_conv, b_hy_conv, hy_skip, w_filt1, b_filt1, filt_freq, w_filt2, b_filt2, w_filt3, g_mla_q, w_mla_qb, g_mla_kv, w_mla_kvb, w_out_odd, w_ffn_in, w_ffn_out, g_final):
    n_c, n_s = BATCH * SEQ, DEC_BATCH * DEC_SEQ
    cvec = jnp.concatenate([c_ctx[None, :], c, jnp.zeros((8 - 1 - DEC_BATCH, D_MODEL), F32)], axis=0)
    mod = _modulation(cvec, w_mod, b_mod)
    xc = x_prompt.reshape(n_c, D_MODEL)
    xs = x_sample.reshape(n_s, D_MODEL)
    rows_c, rows_s = (0, 0), (1, DEC_SEQ // MOD_ROWS)

    gla_consts = _gla_constants()
    cos_b, sin_b = _axial_rope(DEC_SEQ, HD_B)
    rope_b = (jnp.asarray(np.concatenate([cos_b, cos_b], axis=1)), jnp.asarray(np.concatenate([-sin_b, sin_b], axis=1)))
    cos_d, sin_d = _axial_rope(DEC_SEQ, ROPE_D)
    rope_d = _mla_rope_tables(cos_d, sin_d)
    w_qb_all = jnp.pad(w_mla_qb, ((0, 0), (0, 0), (0, 1)))[:, :, _QB_PERM]
    tabs_c, tabs_s = _dft_tables(SEQ), _dft_tables(DEC_SEQ)
    z_c, t_c = _filter_features(SEQ)
    z_s, t_s = _filter_features(DEC_SEQ)
    deltas = jnp.asarray(np.abs(np.linspace(HY_MIN_DECAY, HY_MAX_DECAY, HY_W))[None, :], F32)

    wt_even = jnp.swapaxes(w_in_even, 1, 2)
    wt_odd = jnp.swapaxes(w_in_odd, 1, 2)

    st_gf, st_gb, st_ckv, st_kpe = [], [], [], []
    new_kv = None
    for i in range(DEPTH):
        j = i // 2
        if i % 2 == 0:
            z0 = LANES - 2 * GATE_RANK
            pad_f = jnp.zeros((LANES, H_A * DK_A), F32).at[z0:z0 + GATE_RANK].set(w_gla_gate_f[j])
            pad_b = jnp.zeros((LANES, H_A * DK_A), F32).at[z0 + GATE_RANK:LANES].set(w_gla_gate_b[j])
            pc, v_new, ps = _in_proj(xc, xs, mod, i, wt_even, j, EVEN_ROW_GROUPS, EVEN_W, EVEN_KEEP)
            gate_args = (pad_f, b_gla_gate_f[j], pad_b, b_gla_gate_b[j], g_gla_norm[j], gla_consts)
            a_c, s_f, s_b = _gla(pc, *gate_args, BATCH, SEQ)
            ctx_a = (state_gla_fwd[:, j].reshape(DEC_BATCH, H_A * DK_A, DV_A),
                     state_gla_bwd[:, j].reshape(DEC_BATCH, H_A * DK_A, DV_A))
            a_s = _gla(ps, *gate_args, DEC_BATCH, DEC_SEQ, ctx=ctx_a)
            b_c, *new_kv = _gqa(pc, g_gqa_q[j], g_gqa_k[j], BATCH, SEQ, v_f32=v_new, slot=j, prev=new_kv)
            b_s = _gqa(ps, g_gqa_q[j], g_gqa_k[j], DEC_BATCH, DEC_SEQ, ctx=(cache_gqa_k, cache_gqa_v), rope=rope_b, slot=j)
            w_out = w_out_even
            st_gf.append(s_f.reshape(BATCH, H_A, DK_A, DV_A))
            st_gb.append(s_b.reshape(BATCH, H_A, DK_A, DV_A))
        else:
            pc, kpe_new, ps = _in_proj(xc, xs, mod, i, wt_odd, j, ODD_ROW_GROUPS, ODD_W, ODD_KEEP)
            wf1 = jnp.pad(w_filt1[j], ((0, LANES - FILT_EMB), (0, 0)))
            filt_args = (wf1, b_filt1[j], filt_freq[j], w_filt2[j], b_filt2[j], w_filt3[j])
            g_c = _filter_spectrum(z_c, *filt_args, t_c, deltas, tabs_c)
            g_s = _filter_spectrum(z_s, *filt_args, t_s, deltas, tabs_s)
            b_conv = b_hy_conv[j].reshape(1, 3 * HY_W)
            a_c = _hyena(pc, w_hy_conv[j], b_conv, hy_skip[j], g_c[0], g_c[1], tabs_c, BATCH, SEQ)
            a_s = _hyena(ps, w_hy_conv[j], b_conv, hy_skip[j], g_s[0], g_s[1], tabs_s, DEC_BATCH, DEC_SEQ)
            w_qb = w_qb_all[j]
            b_c, ckv_norm = _mla(pc, g_mla_q[j], w_qb, g_mla_kv[j], w_mla_kvb[j], BATCH, SEQ)
            b_s = _mla(ps, g_mla_q[j], w_qb, g_mla_kv[j], w_mla_kvb[j], DEC_BATCH, DEC_SEQ,
                       ctx=(cache_mla_ckv[:, j], cache_mla_kpe[:, j]), rope=rope_d)
            w_out = w_out_odd
            st_ckv.append(ckv_norm.reshape(BATCH, SEQ, KV_RANK))
            st_kpe.append(kpe_new[:, :ROPE_D].reshape(BATCH, SEQ, ROPE_D))
        xc, xs = _out_proj([a_c, b_c], [a_s, b_s], w_out, j, xc, xs, mod, i, 2)
        last = g_final if i == DEPTH - 1 else None
        xc = _ffn(xc, mod, i, w_ffn_in, w_ffn_out, *rows_c, final_gain=last)
        xs = _ffn(xs, mod, i, w_ffn_in, w_ffn_out, *rows_s, final_gain=last)
    y_prompt = xc.reshape(BATCH, SEQ, D_MODEL)
    y_sample = xs.reshape(DEC_BATCH, DEC_SEQ, D_MODEL)
    return (y_prompt, y_sample, jnp.stack(st_gf, axis=1), jnp.stack(st_gb, axis=1), new_kv[0], new_kv[1],
            jnp.stack(st_ckv, axis=1), jnp.stack(st_kpe, axis=1))
```

```python
import functools
import math

import numpy as np
import jax
import jax.numpy as jnp
from jax import lax
from jax.experimental import pallas as pl
from jax.experimental.pallas import tpu as pltpu

F32 = jnp.float32
BF16 = jnp.bfloat16

D_MODEL = 1024
BATCH, SEQ = 16, 256
DEC_BATCH, DEC_SEQ = 2, 1024
DEPTH = 4
PAST_LEN = 512
GRID_W = 64
HALF_W = D_MODEL // 2
H_A, DV_A, DK_A = 4, 128, 64
GATE_RANK = 16
GLA_TAU = 16.0
GLA_CHUNK = 64
HD_B, H_B, KV_B = 128, 4, 2
HY_W = HALF_W
FILT_EMB, FILT_HID = 33, 64
FILT_BANDS = (FILT_EMB - 1) // 2
HY_MIN_DECAY = math.log(1e-2) / 1.5
HY_MAX_DECAY = math.log(1e-2) / 0.3
H_D, V_D, NOPE_D, ROPE_D = 4, 128, 128, 64
Q_RANK, KV_RANK = 256, 128
FFN_H = 2816
ROPE_THETA = 10000.0
EPS = 1e-6

LANES = 128
VMEM_LIMIT = 56 * 1024 * 1024

MOD_ROWS = 1024
TM = 1024
TM_IN = 512
TM_FFN = 2048
EVEN_W = 2688
ODD_W = 2048
FFN_TN = 256
QB = 256


def _params(n_grid):
    return pltpu.CompilerParams(dimension_semantics=("arbitrary",) * n_grid, vmem_limit_bytes=VMEM_LIMIT)


def _nt(a, b):
    return lax.dot_general(a, b, (((1,), (1,)), ((), ())), preferred_element_type=F32)


def _mm(a, b):
    return jnp.dot(a, b, preferred_element_type=F32)


def _rms(x):
    return x * lax.rsqrt(jnp.mean(x * x, axis=-1, keepdims=True) + EPS)


def _mod_kernel(c_ref, w_ref, b_ref, o_ref):
    cv = c_ref[...]
    s = cv * jax.nn.sigmoid(cv)
    o_ref[...] = _mm(s.astype(BF16), w_ref[...].astype(BF16)) + b_ref[...]


def _modulation(cvec, w_mod, b_mod):
    return pl.pallas_call(
        _mod_kernel,
        grid=(DEPTH, 6),
        in_specs=[
            pl.BlockSpec((8, D_MODEL), lambda l, n: (0, 0)),
            pl.BlockSpec((None, D_MODEL, D_MODEL), lambda l, n: (l, 0, n)),
            pl.BlockSpec((None, 1, D_MODEL), lambda l, n: (l, 0, n)),
        ],
        out_specs=pl.BlockSpec((None, None, 8, D_MODEL), lambda l, n: (l, n, 0, 0)),
        out_shape=jax.ShapeDtypeStruct((DEPTH, 6, 8, D_MODEL), F32),
        compiler_params=_params(2),
        name="adaln_mod",
    )(cvec, w_mod, b_mod.reshape(DEPTH, 1, 6 * D_MODEL))


N_PROMPT = BATCH * SEQ
N_SAMPLE = DEC_BATCH * DEC_SEQ


def _stream_index_maps(tile_rows, s_row0):
    n_c = N_PROMPT // tile_rows
    return (lambda i: (jnp.minimum(i, n_c - 1), 0)), (lambda i: (s_row0 // tile_rows + jnp.maximum(i - n_c, 0), 0))


def _in_proj_kernel(xc_ref, xs_ref, sh_ref, sc_ref, wt_ref, oc_ref, keep_ref, os_ref, wb_ref, *, row_groups, keep, n_c):
    i = pl.program_id(0)

    @pl.when(i == 0)
    def _():
        wb_ref[...] = jnp.zeros_like(wb_ref)
        for src, dst, size in row_groups:
            wb_ref[dst:dst + size, :] = wt_ref[src:src + size, :].astype(BF16)

    def project(x_ref, g):
        h = (_rms(x_ref[...]) * (1.0 + sc_ref[pl.ds(g, 1), :]) + sh_ref[pl.ds(g, 1), :]).astype(BF16)
        return _nt(h, wb_ref[...])

    @pl.when(i < n_c)
    def _():
        y = project(xc_ref, 0)
        oc_ref[...] = y.astype(oc_ref.dtype)
        keep_ref[...] = y[:, keep[0]:keep[0] + keep[1]]

    @pl.when(i >= n_c)
    def _():
        os_ref[...] = project(xs_ref, 1 + (i - n_c) // (MOD_ROWS // TM_IN)).astype(os_ref.dtype)


def _in_proj(xc, xs, s_row0, mod, layer, wt, w_layer, row_groups, n, keep):
    n_c, n_s = N_PROMPT // TM_IN, N_SAMPLE // TM_IN
    xc_idx, xs_idx = _stream_index_maps(TM_IN, s_row0)
    c_idx, s_idx = _stream_index_maps(TM_IN, 0)
    return pl.pallas_call(
        functools.partial(_in_proj_kernel, row_groups=row_groups, keep=keep, n_c=n_c),
        grid=(n_c + n_s,),
        in_specs=[pl.BlockSpec((TM_IN, D_MODEL), xc_idx), pl.BlockSpec((TM_IN, D_MODEL), xs_idx),
                  pl.BlockSpec((None, None, 8, D_MODEL), lambda i: (layer, 0, 0, 0)),
                  pl.BlockSpec((None, None, 8, D_MODEL), lambda i: (layer, 1, 0, 0)),
                  pl.BlockSpec((None, wt.shape[1], D_MODEL), lambda i: (w_layer, 0, 0), pipeline_mode=pl.Buffered(1))],
        out_specs=[pl.BlockSpec((TM_IN, n), c_idx), pl.BlockSpec((TM_IN, keep[1]), c_idx), pl.BlockSpec((TM_IN, n), s_idx)],
        out_shape=[jax.ShapeDtypeStruct((N_PROMPT, n), BF16), jax.ShapeDtypeStruct((N_PROMPT, keep[1]), F32),
                   jax.ShapeDtypeStruct((N_SAMPLE, n), BF16)],
        scratch_shapes=[pltpu.VMEM((n, D_MODEL), BF16)],
        compiler_params=_params(1),
        name="norm_mod_proj",
    )(xc, xs, mod, mod, wt)


def _ffn_kernel(x_ref, sh_ref, sc_ref, gate_ref, wg_ref, wu_ref, wd_ref, *refs, first_sub, final):
    (gf_ref, o_ref, h_ref) = refs if final else (None,) + refs
    n_sub = x_ref.shape[0] // MOD_ROWS
    subs = [(slice(s * MOD_ROWS, (s + 1) * MOD_ROWS),
             jnp.maximum(first_sub + pl.program_id(0) * n_sub + s - (N_PROMPT // MOD_ROWS - 1), 0)) for s in range(n_sub)]

    @pl.when(pl.program_id(1) == 0)
    def _():
        for rows, g in subs:
            x = x_ref[rows, :]
            o_ref[rows, :] = x
            h_ref[rows, :] = (_rms(x) * (1.0 + sc_ref[pl.ds(g, 1), :]) + sh_ref[pl.ds(g, 1), :]).astype(BF16)

    wg = wg_ref[...].astype(BF16)
    wu = wu_ref[...].astype(BF16)
    wd = wd_ref[...].astype(BF16)
    for rows, g in subs:
        h = h_ref[rows, :]
        a = _mm(h, wg)
        act = (a * jax.nn.sigmoid(a) * _mm(h, wu)).astype(BF16)
        o_ref[rows, :] += gate_ref[pl.ds(g, 1), :] * _mm(act, wd)

    if final:
        @pl.when(pl.program_id(1) == pl.num_programs(1) - 1)
        def _():
            for rows, _ in subs:
                o_ref[rows, :] = _rms(o_ref[rows, :]) * gf_ref[...]


def _ffn(x, row0, m, mod, layer, w_in, w_out, final_gain=None):
    nj = FFN_H // FFN_TN
    tile0 = row0 // TM_FFN
    mod_spec = lambda k: pl.BlockSpec((None, None, 8, D_MODEL), lambda i, j: (layer, k, 0, 0))
    final = final_gain is not None
    extra_specs = [pl.BlockSpec((1, D_MODEL), lambda i, j: (0, 0))] if final else []
    extra_args = [final_gain.reshape(1, D_MODEL)] if final else []
    return pl.pallas_call(
        functools.partial(_ffn_kernel, first_sub=row0 // MOD_ROWS, final=final),
        grid=(m // TM_FFN, nj),
        in_specs=[pl.BlockSpec((TM_FFN, D_MODEL), lambda i, j: (tile0 + i, 0)), mod_spec(3), mod_spec(4), mod_spec(5),
                  pl.BlockSpec((None, D_MODEL, FFN_TN), lambda i, j: (layer, 0, j)),
                  pl.BlockSpec((None, D_MODEL, FFN_TN), lambda i, j: (layer, 0, j + nj)),
                  pl.BlockSpec((None, FFN_TN, D_MODEL), lambda i, j: (layer, j, 0))] + extra_specs,
        out_specs=pl.BlockSpec((TM_FFN, D_MODEL), lambda i, j: (i, 0)),
        out_shape=jax.ShapeDtypeStruct((m, D_MODEL), F32),
        scratch_shapes=[pltpu.VMEM((TM_FFN, D_MODEL), BF16)],
        compiler_params=_params(2),
        name="ffn_residual",
    )(x, mod, mod, mod, w_in, w_in, w_out, *extra_args)


def _proj_res_kernel(ac0_ref, ac1_ref, as0_ref, as1_ref, w0_ref, w1_ref, xc_ref, xs_ref, gate_ref, o_ref, *, n_c):
    i = pl.program_id(0)

    def mix(a0_ref, a1_ref, x_ref, g):
        acc = _mm(a0_ref[...], w0_ref[...].astype(BF16)) + _mm(a1_ref[...], w1_ref[...].astype(BF16))
        return x_ref[...] + gate_ref[pl.ds(g, 1), :] * acc

    @pl.when(i < n_c)
    def _():
        o_ref[...] = mix(ac0_ref, ac1_ref, xc_ref, 0)

    @pl.when(i >= n_c)
    def _():
        o_ref[...] = mix(as0_ref, as1_ref, xs_ref, 1 + (i - n_c))


def _out_proj(acts_c, acts_s, w, w_layer, xc, xs, s_row0, mod, layer, k_gate):
    n_c, n_s = N_PROMPT // TM, N_SAMPLE // TM
    kw = acts_c[0].shape[1]
    xc_idx, xs_idx = _stream_index_maps(TM, s_row0)
    c_idx, s_idx = _stream_index_maps(TM, 0)
    w_specs = [pl.BlockSpec((None, kw, D_MODEL), functools.partial(lambda i, p: (w_layer, p, 0), p=p),
                            pipeline_mode=pl.Buffered(1)) for p in range(2)]
    return pl.pallas_call(
        functools.partial(_proj_res_kernel, n_c=n_c),
        grid=(n_c + n_s,),
        in_specs=[pl.BlockSpec((TM, kw), c_idx)] * 2 + [pl.BlockSpec((TM, kw), s_idx)] * 2 + w_specs + [
            pl.BlockSpec((TM, D_MODEL), xc_idx), pl.BlockSpec((TM, D_MODEL), xs_idx),
            pl.BlockSpec((None, None, 8, D_MODEL), lambda i: (layer, k_gate, 0, 0)),
        ],
        out_specs=pl.BlockSpec((TM, D_MODEL), lambda i: (i, 0)),
        out_shape=jax.ShapeDtypeStruct((N_PROMPT + N_SAMPLE, D_MODEL), F32),
        compiler_params=_params(1),
        name="out_proj_residual",
    )(*acts_c, *acts_s, w, w, xc, xs, mod)


def _gqa_kernel(*refs, sample, has_prev=False, slot=0):
    if sample:
        q_ref, k_ref, v_ref, gq_ref, gk_ref, ck_ref, cv_ref, cos_ref, sin_ref, o_ref, kb_ref, vb_ref = refs
    else:
        n_in = 8 if has_prev else 6
        q_ref, k_ref, v_ref, gq_ref, gk_ref, vf_ref = refs[:6]
        o_ref, kc_ref, vc_ref, kb_ref, vb_ref = refs[n_in:]
        if not has_prev:
            stacks = (kc_ref, vc_ref)
            kc_ref, vc_ref = kc_ref.at[slot], vc_ref.at[slot]
    qi = pl.program_id(1)
    n_new = k_ref.shape[0]
    past = PAST_LEN if sample else 0
    rep = H_B // KV_B

    @pl.when(qi == 0)
    def _():
        if not (sample or has_prev):
            for ref in stacks:
                for other in range(ref.shape[0]):
                    if other != slot:
                        ref[other] = jnp.zeros(ref.shape[1:], ref.dtype)
        for g in range(KV_B):
            sl = slice(HD_B * g, HD_B * (g + 1))
            kn = _rms(k_ref[:, sl].astype(F32)) * gk_ref[...]
            if sample:
                kb_ref[0:past, sl] = ck_ref[:, g, :].astype(BF16)
                vb_ref[g, 0:past, 0:HD_B] = cv_ref[:, g, :].astype(BF16)
                kn = kn * cos_ref[...] + pltpu.roll(kn, HD_B // 2, 1) * sin_ref[...]
            else:
                kc_ref[:, g, :] = kn
                vc_ref[:, g, :] = vf_ref[:, sl]
            kb_ref[past:past + n_new, sl] = kn.astype(BF16)
            vb_ref[g, past:past + n_new, 0:HD_B] = v_ref[:, sl].astype(BF16)
            vb_ref[g, :, HD_B:] = jnp.ones((past + n_new, HD_B), BF16)

    r0 = pl.multiple_of(qi * QB, QB)
    qs = []
    for h in range(H_B):
        qn = _rms(q_ref[:, HD_B * h:HD_B * (h + 1)].astype(F32)) * gq_ref[...]
        if sample:
            qn = qn * cos_ref[pl.ds(r0, QB), :] + pltpu.roll(qn, HD_B // 2, 1) * sin_ref[pl.ds(r0, QB), :]
        qs.append((qn * (HD_B ** -0.5)).astype(BF16))
    scores = [_nt(qs[h], kb_ref[:, HD_B * (h // rep):HD_B * (h // rep + 1)]) for h in range(H_B)]
    weights = [jnp.exp(s - jnp.max(s, axis=-1, keepdims=True)).astype(BF16) for s in scores]
    sums = [_mm(weights[h], vb_ref[h // rep]) for h in range(H_B)]
    for h in range(H_B):
        o_ref[:, HD_B * h:HD_B * (h + 1)] = (sums[h][:, :HD_B] / sums[h][:, HD_B:]).astype(o_ref.dtype)


def _gqa(proj, g_q, g_k, n_batch, seq, ctx=None, rope=None, v_f32=None, slot=0, prev=None):
    sample = ctx is not None
    m = n_batch * seq
    nq = seq // QB
    n_even = (DEPTH + 1) // 2
    in_specs = [
        pl.BlockSpec((QB, 512), lambda b, i: (b * nq + i, 3)),
        pl.BlockSpec((seq, 256), lambda b, i: (b, 8)),
        pl.BlockSpec((seq, 256), lambda b, i: (b, 9)),
        pl.BlockSpec((1, HD_B), lambda b, i: (0, 0)),
        pl.BlockSpec((1, HD_B), lambda b, i: (0, 0)),
    ]
    args = [proj, proj, proj, g_q.reshape(1, HD_B), g_k.reshape(1, HD_B)]
    o_spec = pl.BlockSpec((QB, 512), lambda b, i: (b * nq + i, 0))
    o_shape = jax.ShapeDtypeStruct((m, 512), BF16)
    aliases = {}
    if sample:
        cache_spec = pl.BlockSpec((None, None, PAST_LEN, KV_B, HD_B), lambda b, i: (b, slot, 0, 0, 0))
        in_specs += [
            cache_spec, cache_spec,
            pl.BlockSpec((seq, HD_B), lambda b, i: (0, 0)),
            pl.BlockSpec((seq, HD_B), lambda b, i: (0, 0)),
        ]
        args += [ctx[0], ctx[1], rope[0], rope[1]]
        out_specs, out_shape = o_spec, o_shape
    else:
        in_specs.append(pl.BlockSpec((seq, KV_B * HD_B), lambda b, i: (b, 0)))
        args.append(v_f32)
        if prev is not None:
            in_specs += [pl.BlockSpec(memory_space=pl.ANY)] * 2
            aliases = {len(args): 1, len(args) + 1: 2}
            args += list(prev)
        if prev is None:
            new_spec = pl.BlockSpec((None, n_even, seq, KV_B, HD_B), lambda b, i: (b, 0, 0, 0, 0))
        else:
            new_spec = pl.BlockSpec((None, None, seq, KV_B, HD_B), lambda b, i: (b, slot, 0, 0, 0))
        new_shape = jax.ShapeDtypeStruct((n_batch, n_even, seq, KV_B, HD_B), F32)
        out_specs, out_shape = [o_spec, new_spec, new_spec], [o_shape, new_shape, new_shape]
    n_keys = seq + (PAST_LEN if sample else 0)
    return pl.pallas_call(
        functools.partial(_gqa_kernel, sample=sample, has_prev=prev is not None, slot=slot),
        grid=(n_batch, nq),
        in_specs=in_specs,
        out_specs=out_specs,
        out_shape=out_shape,
        input_output_aliases=aliases,
        scratch_shapes=[pltpu.VMEM((n_keys, KV_B * HD_B), BF16), pltpu.VMEM((KV_B, n_keys, 2 * HD_B), BF16)],
        compiler_params=_params(2),
        name="gqa_sample" if sample else "gqa_prompt",
    )(*args)


MLA_QW = 2 * LANES
MLA_HW = 4 * LANES


def _rotate_pairs(x, cos_t, sin_lo, sin_hi):
    w = x.shape[1]
    return x * cos_t + pltpu.roll(x, ROPE_D // 2, 1) * sin_hi + pltpu.roll(x, w - ROPE_D // 2, 1) * sin_lo


def _mla_kernel(*refs, sample):
    if sample:
        (cq_ref, ckv_ref, kpe_ref, gq_ref, wqb_ref, gkv_ref, wkvb_ref, cckv_ref, ckpe_ref,
         qc_ref, ql_ref, qh_ref, kc_ref, kl_ref, kh_ref, o_ref, kv_s) = refs
    else:
        cq_ref, ckv_ref, kpe_ref, gq_ref, wqb_ref, gkv_ref, wkvb_ref, o_ref, ckvn_ref, kv_s = refs
    qi = pl.program_id(1)
    n_new = ckv_ref.shape[0]
    past = PAST_LEN if sample else 0

    def stage_kv(rows, kv, kpe_block):
        for h in range(H_D):
            kv_s[rows, MLA_HW * h:MLA_HW * h + NOPE_D] = kv[:, 256 * h:256 * h + NOPE_D].astype(BF16)
            kv_s[rows, MLA_HW * h + NOPE_D:MLA_HW * h + MLA_QW] = kpe_block
            kv_s[rows, MLA_HW * h + MLA_QW:MLA_HW * h + MLA_QW + V_D] = kv[:, 256 * h + NOPE_D:256 * (h + 1)].astype(BF16)

    @pl.when(qi == 0)
    def _():
        wkvb = wkvb_ref[...].astype(BF16)
        ckvn = _rms(ckv_ref[...].astype(F32)) * gkv_ref[...]
        if not sample:
            ckvn_ref[...] = ckvn
        kpe = kpe_ref[...]
        if sample:
            ctx_kpe = jnp.concatenate([ckpe_ref[...], jnp.zeros((past, LANES - ROPE_D), F32)], axis=1)
            stage_kv(slice(0, past), _mm(cckv_ref[...].astype(BF16), wkvb), ctx_kpe.astype(BF16))
            kpe = _rotate_pairs(kpe.astype(F32), kc_ref[...], kl_ref[...], kh_ref[...]).astype(BF16)
        stage_kv(slice(past, past + n_new), _mm(ckvn.astype(BF16), wkvb), kpe)
        for h in range(H_D):
            kv_s[:, MLA_HW * h + MLA_QW + V_D:MLA_HW * (h + 1)] = jnp.ones((past + n_new, V_D), BF16)

    q = _mm((_rms(cq_ref[...].astype(F32)) * gq_ref[...]).astype(BF16), wqb_ref[...].astype(BF16))
    q = q * ((NOPE_D + ROPE_D) ** -0.5)
    qs = []
    for h in range(H_D):
        q_h = q[:, MLA_QW * h:MLA_QW * (h + 1)]
        if sample:
            rows = pl.ds(pl.multiple_of(qi * QB, QB), QB)
            q_h = _rotate_pairs(q_h, qc_ref[rows, :], ql_ref[rows, :], qh_ref[rows, :])
        qs.append(q_h.astype(BF16))
    scores = [_nt(qs[h], kv_s[:, MLA_HW * h:MLA_HW * h + MLA_QW]) for h in range(H_D)]
    weights = [jnp.exp(s - jnp.max(s, axis=-1, keepdims=True)).astype(BF16) for s in scores]
    sums = [_mm(weights[h], kv_s[:, MLA_HW * h + MLA_QW:MLA_HW * (h + 1)]) for h in range(H_D)]
    for h in range(H_D):
        o_ref[:, V_D * h:V_D * (h + 1)] = (sums[h][:, :V_D] / sums[h][:, V_D:]).astype(o_ref.dtype)


def _mla(proj, g_q, w_qb, g_kv, w_kvb, n_batch, seq, ctx=None, rope=None):
    sample = ctx is not None
    m = n_batch * seq
    nq = seq // QB
    in_specs = [
        pl.BlockSpec((QB, Q_RANK), lambda b, i: (b * nq + i, 6)),
        pl.BlockSpec((seq, KV_RANK), lambda b, i: (b, 14)),
        pl.BlockSpec((seq, LANES), lambda b, i: (b, 15)),
        pl.BlockSpec((1, Q_RANK), lambda b, i: (0, 0)),
        pl.BlockSpec((Q_RANK, H_D * MLA_QW), lambda b, i: (0, 0)),
        pl.BlockSpec((1, KV_RANK), lambda b, i: (0, 0)),
        pl.BlockSpec((KV_RANK, 1024), lambda b, i: (0, 0)),
    ]
    args = [proj, proj, proj, g_q.reshape(1, Q_RANK), w_qb, g_kv.reshape(1, KV_RANK), w_kvb]
    o_spec = pl.BlockSpec((QB, 512), lambda b, i: (b * nq + i, 0))
    o_shape = jax.ShapeDtypeStruct((m, 512), BF16)
    if sample:
        in_specs += [
            pl.BlockSpec((None, PAST_LEN, KV_RANK), lambda b, i: (b, 0, 0)),
            pl.BlockSpec((None, PAST_LEN, ROPE_D), lambda b, i: (b, 0, 0)),
        ] + [pl.BlockSpec((seq, MLA_QW), lambda b, i: (0, 0))] * 3 + [pl.BlockSpec((seq, LANES), lambda b, i: (0, 0))] * 3
        args += [ctx[0], ctx[1], *rope]
        out_specs, out_shape = o_spec, o_shape
    else:
        out_specs = [o_spec, pl.BlockSpec((seq, KV_RANK), lambda b, i: (b, 0))]
        out_shape = [o_shape, jax.ShapeDtypeStruct((m, KV_RANK), F32)]
    n_keys = seq + (PAST_LEN if sample else 0)
    return pl.pallas_call(
        functools.partial(_mla_kernel, sample=sample),
        grid=(n_batch, nq),
        in_specs=in_specs,
        out_specs=out_specs,
        out_shape=out_shape,
        scratch_shapes=[pltpu.VMEM((n_keys, H_D * MLA_HW), BF16)],
        compiler_params=_params(2),
        name="mla_sample" if sample else "mla_prompt",
    )(*args)


def _dft(table, x):
    return _mm(table.astype(BF16), x.astype(BF16))


def _filter_kernel(z_ref, wf1_ref, bf1_ref, fr_ref, wf2_ref, bf2_ref, wf3_ref, t_ref, dl_ref,
                   c_ref, s_ref, gre_ref, gim_ref):
    n_tok = z_ref.shape[0]
    fr = fr_ref[...]
    hid = jnp.sin(fr * (_mm(z_ref[...].astype(BF16), wf1_ref[...].astype(BF16)) + bf1_ref[...]))
    hid = jnp.sin(fr * (_mm(hid.astype(BF16), wf2_ref[...].astype(BF16)) + bf2_ref[...]))
    filt = _mm(hid.astype(BF16), wf3_ref[...].astype(BF16))
    decay = jnp.exp(-t_ref[...] * dl_ref[...])
    row = lax.broadcasted_iota(jnp.int32, (n_tok, 1), 0)
    h_f = filt[:, :HY_W] * decay
    h_b = jnp.where(row == 0, 0.0, filt[:, HY_W:] * decay)
    p, m = h_f + h_b, h_f - h_b
    g_re = _dft(c_ref[...], p)
    g_im = _dft(s_ref[...], m)
    sign = jnp.where(row % 2 == 0, 1.0, -1.0)
    nyquist = jnp.sum(p * sign, axis=0, keepdims=True)
    g_im = jnp.where(row == 0, nyquist, g_im)
    wk = jnp.where(row == 0, 0.5 / n_tok, 1.0 / n_tok)
    gre_ref[...] = g_re * wk
    gim_ref[...] = g_im * wk


def _filter_spectrum(z, wf1, bf1, freq, wf2, bf2, wf3, t_col, deltas, tabs):
    n_tok = z.shape[0]
    out = jax.ShapeDtypeStruct((n_tok, HY_W), F32)
    return pl.pallas_call(
        _filter_kernel,
        out_shape=[out, out],
        compiler_params=pltpu.CompilerParams(vmem_limit_bytes=VMEM_LIMIT),
        name="hyena_filter",
    )(z, wf1, bf1.reshape(1, FILT_HID), freq.reshape(1, FILT_HID), wf2, bf2.reshape(1, FILT_HID), wf3,
      t_col, deltas, tabs[0], tabs[1])


HY_ROWS = 1024


def _hyena_channels(seq):
    return HY_W if seq <= 256 else HY_W // 2


def _hyena_kernel(u0_ref, u1_ref, u2_ref, w0_ref, w1_ref, w2_ref, b0_ref, b1_ref, b2_ref, skip_ref,
                  gre_ref, gim_ref, cf_ref, sf_ref, stf_ref, o_ref, c_ref, s_ref, st_ref):
    seq = c_ref.shape[0]
    n_rows = u0_ref.shape[0]
    n_seq = n_rows // seq
    pos = lax.broadcasted_iota(jnp.int32, (n_rows, 1), 0) % seq

    @pl.when((pl.program_id(0) == 0) & (pl.program_id(1) == 0))
    def _():
        c_ref[...] = cf_ref[...].astype(BF16)
        s_ref[...] = sf_ref[...].astype(BF16)
        st_ref[...] = stf_ref[...].astype(BF16)

    def short_conv(u_ref, w_ref, b_ref):
        x, w = u_ref[...].astype(F32), w_ref[...]
        prev = jnp.where(pos == 0, 0.0, pltpu.roll(x, 1, 0))
        nxt = jnp.where(pos == seq - 1, 0.0, pltpu.roll(x, n_rows - 1, 0))
        return prev * w[0:1] + x * w[1:2] + nxt * w[2:3] + b_ref[...]

    def side_by_side(a):
        return a if n_seq == 1 else jnp.concatenate([a[s * seq:(s + 1) * seq] for s in range(n_seq)], axis=1)

    def stacked(a):
        ct = a.shape[1] // n_seq
        return a if n_seq == 1 else jnp.concatenate([a[:, s * ct:(s + 1) * ct] for s in range(n_seq)], axis=0)

    x0 = short_conv(u0_ref, w0_ref, b0_ref)
    gv = short_conv(u1_ref, w1_ref, b1_ref) * short_conv(u2_ref, w2_ref, b2_ref)
    sig = side_by_side(gv).astype(BF16)
    u_re = _mm(c_ref[...], sig)
    u_im = _mm(s_ref[...], sig)
    g_re = jnp.concatenate([gre_ref[...]] * n_seq, axis=1)
    g_im = jnp.concatenate([gim_ref[...]] * n_seq, axis=1)
    bin0 = lax.broadcasted_iota(jnp.int32, (seq, 1), 0) == 0
    p_im = u_im * g_im
    y_re = u_re * g_re - jnp.where(bin0, 0.0, p_im)
    y_im = jnp.where(bin0, p_im, u_re * g_im + u_im * g_re)
    y = stacked(_mm(c_ref[...], y_re.astype(BF16)) + _mm(st_ref[...], y_im.astype(BF16)))
    o_ref[...] = (x0 * (y + gv * skip_ref[...])).astype(o_ref.dtype)


def _hyena(proj, w_conv, b_conv, skip, g_re, g_im, tabs, n_batch, seq):
    ct = _hyena_channels(seq)
    nct = HY_W // ct
    u_specs = [pl.BlockSpec((HY_ROWS, ct), functools.partial(lambda b, c, g: (b, g * nct + c), g=g)) for g in range(3)]
    w_specs = [pl.BlockSpec((3, ct), functools.partial(lambda b, c, g: (0, g * nct + c), g=g)) for g in range(3)]
    b_specs = [pl.BlockSpec((1, ct), functools.partial(lambda b, c, g: (0, g * nct + c), g=g)) for g in range(3)]
    tab_spec = pl.BlockSpec((seq, seq), lambda b, c: (0, 0))
    return pl.pallas_call(
        _hyena_kernel,
        grid=(n_batch * seq // HY_ROWS, nct),
        in_specs=u_specs + w_specs + b_specs + [
            pl.BlockSpec((1, ct), lambda b, c: (0, c)),
            pl.BlockSpec((seq, ct), lambda b, c: (0, c)),
            pl.BlockSpec((seq, ct), lambda b, c: (0, c)),
        ] + [tab_spec] * 3,
        out_specs=pl.BlockSpec((HY_ROWS, ct), lambda b, c: (b, c)),
        out_shape=jax.ShapeDtypeStruct((n_batch * seq, HY_W), BF16),
        scratch_shapes=[pltpu.VMEM((seq, seq), BF16)] * 3,
        compiler_params=_params(2),
        name="hyena_conv",
    )(proj, proj, proj, w_conv, w_conv, w_conv, b_conv, b_conv, b_conv, skip.reshape(1, HY_W), g_re, g_im, *tabs)


def _dft_tables(n_tok):
    k = np.arange(n_tok)[:, None]
    s = np.arange(n_tok)[None, :]
    ang = ((k * s) % (2 * n_tok)) * (np.pi / n_tok)
    cos_t = np.cos(ang)
    sin_f = np.where(k == 0, np.where(s % 2 == 0, 1.0, -1.0), -np.sin(ang))
    return [jnp.asarray(t, F32) for t in (cos_t, sin_f, sin_f.T)]


GLA_LEVELS = (32, 16, 8, 4, 2, 1)
GLA_SAFE_DECAY = 60.0
GLA_GROUP = 2


def _gla_constants():
    c = GLA_CHUNK
    idx = np.arange(c)
    i, t = idx[:, None], idx[None, :]
    masks = []
    for s in GLA_LEVELS:
        upper = (idx % (2 * s)) >= s
        masks.append(((i // (2 * s)) == (t // (2 * s))) & upper[:, None] & (~upper)[None, :])
    masks.append(i == t)
    tri = t <= i
    fwd_m = np.stack([np.tile(m, (H_A, 1)) for m in masks]).astype(np.float32)
    bwd_m = np.stack([np.tile(m[::-1, ::-1], (H_A, 1)) for m in masks]).astype(np.float32)
    head_of_row = np.repeat(np.arange(H_A), c)[:, None]
    head_of_lane = np.repeat(np.arange(H_A), DK_A)[None, :]
    head_mask = head_of_row == head_of_lane
    return (jnp.asarray(tri, BF16), jnp.asarray(tri[::-1, ::-1], BF16), jnp.asarray(fwd_m), jnp.asarray(bwd_m),
            jnp.asarray(head_mask, BF16))


def _pair_reference(b, s, backward, row):
    c = GLA_CHUNK
    ref = s if backward else s - 1
    if 2 * s >= 8:
        pieces = [jnp.broadcast_to(b[p * 2 * s + ref:p * 2 * s + ref + 1, :], (2 * s, b.shape[1]))
                  for p in range(c // (2 * s))]
        return pieces[0] if len(pieces) == 1 else jnp.concatenate(pieces, axis=0)
    pos = row % (2 * s)
    out = None
    for o in range(2 * s):
        d = ref - o
        shifted = b if d == 0 else pltpu.roll(b, (-d) % c, 0)
        out = shifted if out is None else jnp.where(pos == o, shifted, out)
    return out


def _chunk_log_decay(la, t_ref):
    l1 = la.astype(BF16)
    r1 = la - l1.astype(F32)
    l2 = r1.astype(BF16)
    l3 = (r1 - l2.astype(F32)).astype(BF16)
    tmat = t_ref[...]
    return _mm(tmat, l1) + _mm(tmat, l2) + _mm(tmat, l3)


def _stack_heads(a, hm):
    ab = a.astype(BF16)
    return jnp.concatenate([ab] * H_A, axis=0) * hm


def _state_terms(k, v, b, b_last):
    c = GLA_CHUNK
    k_rest = (k * jnp.exp(b_last - b)).T
    carry = jnp.broadcast_to(jnp.exp(b_last), (2 * c, b.shape[1])).T
    return k_rest.astype(BF16), carry


def _gla_chunk(q, k, v, la, t_ref, m_ref, hm, s_ref, backward):
    c = GLA_CHUNK
    b = _chunk_log_decay(la, t_ref)
    row = lax.broadcasted_iota(jnp.int32, (c, 1), 0)
    last = 0 if backward else c - 1
    b_last = b[last:last + 1, :]
    scores = _nt(_stack_heads(q, hm), k.astype(BF16)) * m_ref[len(GLA_LEVELS)]
    for lvl, s in enumerate(GLA_LEVELS):
        is_query = (row % (2 * s) < s) if backward else (row % (2 * s) >= s)
        delta = b - _pair_reference(b, s, backward, row)
        x = jnp.exp(jnp.where(is_query, delta, -delta))
        scores = scores + _nt(_stack_heads(q * x, hm), (k * x).astype(BF16)) * m_ref[lvl]
    scores = scores.astype(BF16)
    state = s_ref[...]
    inter = _mm(_stack_heads(q * jnp.exp(b), hm), state.astype(BF16))
    k_rest, carry = _state_terms(k, v, b, b_last)
    outs = []
    for h in range(H_A):
        rows = slice(c * h, c * (h + 1))
        v_h = v[:, DV_A * h:DV_A * (h + 1)]
        outs.append(_mm(scores[rows], v_h) + inter[rows])
        s_ref[rows, :] = state[rows] * carry[rows] + _mm(k_rest[rows], v_h)
    return jnp.concatenate(outs, axis=1)


def _gla_local(items, hm):
    c = GLA_CHUNK
    bs = [_chunk_log_decay(la, t_ref) for _, _, _, la, t_ref, _, _ in items]
    b_lasts = [b[(0 if it[6] else c - 1):(0 if it[6] else c - 1) + 1, :] for b, it in zip(bs, items)]
    q_decayed = [_stack_heads(it[0] * jnp.exp(b), hm) for it, b in zip(items, bs)]
    k_grown = [(it[1] * jnp.exp(-b)).astype(BF16) for it, b in zip(items, bs)]
    raw = [_nt(qd, kg) for qd, kg in zip(q_decayed, k_grown)]
    masked = [r * it[5] for r, it in zip(raw, items)]
    terms = [_state_terms(it[1], it[2], b, bl) for it, b, bl in zip(items, bs, b_lasts)]
    out = []
    for n in range(0, len(items), 2):
        v = items[n][2]
        heads = [(slice(c * h, c * (h + 1)), v[:, DV_A * h:DV_A * (h + 1)]) for h in range(H_A)]
        both = (masked[n] + masked[n + 1]).astype(BF16)
        intra = jnp.concatenate([_mm(both[rows], v_h) for rows, v_h in heads], axis=1)
        for m in (n, n + 1):
            k_rest, carry = terms[m]
            incr = jnp.concatenate([_mm(k_rest[rows], v_h) for rows, v_h in heads], axis=0)
            out.append((intra if m == n else None, q_decayed[m], incr, carry))
    return out


def _gla_kernel(*refs, sample):
    if sample:
        (x_ref, z_ref, wf_ref, bf_ref, wb_ref, bb_ref, tf_ref, tb_ref, mf_ref, mb_ref, hm_ref, gn_ref, sf0_ref, sb0_ref,
         o_ref, la_f, la_b, o_f, o_b, s_f, s_b, qd_f, qd_b, ds_f, ds_b, cr_f, cr_b) = refs
    else:
        (x_ref, z_ref, wf_ref, bf_ref, wb_ref, bb_ref, tf_ref, tb_ref, mf_ref, mb_ref, hm_ref, gn_ref,
         o_ref, sf_out, sb_out, la_f, la_b, o_f, o_b, s_f, s_b, qd_f, qd_b, ds_f, ds_b, cr_f, cr_b) = refs
    n_tok = x_ref.shape[0]
    n_chunks = n_tok // GLA_CHUNK
    hk, hv = H_A * DK_A, H_A * DV_A
    zb = z_ref[...].astype(BF16)

    def log_sigmoid(t):
        return jnp.minimum(t, 0.0) - jnp.log(1.0 + jnp.exp(-jnp.abs(t)))

    la_f[...] = log_sigmoid(_mm(zb, wf_ref[...].astype(BF16)) + bf_ref[...]) / GLA_TAU
    la_b[...] = log_sigmoid(_mm(zb, wb_ref[...].astype(BF16)) + bb_ref[...]) / GLA_TAU
    if sample:
        s_f[...] = sf0_ref[...]
        s_b[...] = sb0_ref[...]
    else:
        s_f[...] = jnp.zeros_like(s_f)
        s_b[...] = jnp.zeros_like(s_b)
    hm = hm_ref[...]

    fwd = (la_f, tf_ref, mf_ref, s_f, o_f, qd_f, ds_f, cr_f, False)
    bwd = (la_b, tb_ref, mb_ref, s_b, o_b, qd_b, ds_b, cr_b, True)
    tri_f = jnp.sum(mf_ref[...], axis=0)
    tri_b = jnp.sum(mb_ref[...], axis=0)

    def chunk_rows(ci, backward):
        cidx = n_chunks - 1 - ci if backward else ci
        return cidx, pl.ds(pl.multiple_of(cidx * GLA_CHUNK, GLA_CHUNK), GLA_CHUNK)

    def load_qkv(rows):
        q = x_ref[rows, 0:hk].astype(F32) * (DK_A ** -0.5)
        return q, x_ref[rows, hk:2 * hk].astype(F32), x_ref[rows, 2 * hk:2 * hk + hv]

    def safe_step(ci, carry):
        for la_ref, t_ref, m_ref, s_ref, out_ref, _, _, _, backward in (fwd, bwd):
            _, rows = chunk_rows(ci, backward)
            out_ref[rows, :] = _gla_chunk(*load_qkv(rows), la_ref[rows, :], t_ref, m_ref, hm, s_ref, backward)
        return carry

    def local_step(gi, carry):
        items, dests = [], []
        for u in range(GLA_GROUP):
            cidx = gi * GLA_GROUP + u
            rows = pl.ds(pl.multiple_of(cidx * GLA_CHUNK, GLA_CHUNK), GLA_CHUNK)
            qkv = load_qkv(rows)
            for (la_ref, t_ref, _, _, out_ref, qd_ref, ds_ref, cr_ref, backward), tri in ((fwd, tri_f), (bwd, tri_b)):
                items.append((*qkv, la_ref[rows, :], t_ref, tri, backward))
                dests.append((out_ref, rows, qd_ref, ds_ref, cr_ref, cidx))
        for (out_ref, rows, qd_ref, ds_ref, cr_ref, cidx), (intra, qd, incr, factor) in zip(dests, _gla_local(items, hm)):
            out_ref[rows, :] = jnp.zeros((GLA_CHUNK, hv), F32) if intra is None else intra
            qd_ref[cidx] = qd
            ds_ref[cidx] = incr
            cr_ref[cidx] = factor
        return carry

    def scan_step(ci, carry):
        for _, _, _, s_ref, out_ref, qd_ref, ds_ref, cr_ref, backward in (fwd, bwd):
            cidx, rows = chunk_rows(ci, backward)
            state = s_ref[...]
            inter = _mm(qd_ref[cidx], state.astype(BF16))
            out_ref[rows, :] += jnp.concatenate(
                [inter[GLA_CHUNK * h:GLA_CHUNK * (h + 1)] for h in range(H_A)], axis=1)
            s_ref[...] = state * cr_ref[cidx] + ds_ref[cidx]
        return carry

    chunk_sums = [jnp.sum(ref[...].reshape(n_chunks, GLA_CHUNK, hk), axis=1) for ref in (la_f, la_b)]
    mild = jnp.minimum(jnp.min(chunk_sums[0]), jnp.min(chunk_sums[1])) > -GLA_SAFE_DECAY

    @pl.when(mild)
    def _():
        lax.fori_loop(0, n_chunks // GLA_GROUP, local_step, 0, unroll=2)
        lax.fori_loop(0, n_chunks, scan_step, 0, unroll=2)

    @pl.when(jnp.logical_not(mild))
    def _():
        lax.fori_loop(0, n_chunks, safe_step, 0)
    if not sample:
        sf_out[...] = s_f[...]
        sb_out[...] = s_b[...]
    gain = gn_ref[...]
    for h in range(H_A):
        cols = slice(DV_A * h, DV_A * (h + 1))
        r = x_ref[:, 2 * hk + hv + DV_A * h:2 * hk + hv + DV_A * (h + 1)].astype(F32)
        o_ref[:, cols] = (_rms(o_f[:, cols] + o_b[:, cols]) * gain * (r * jax.nn.sigmoid(r))).astype(o_ref.dtype)


def _gla(proj, w_gf, b_gf, w_gb, b_gb, g_norm, consts, n_batch, seq, ctx=None):
    sample = ctx is not None
    hk, hv = H_A * DK_A, H_A * DV_A
    n_ch = seq // GLA_CHUNK
    full = lambda shape: pl.BlockSpec(shape, lambda b: (0,) * len(shape))
    in_specs = [
        pl.BlockSpec((seq, 2 * hk + 2 * hv), lambda b: (b, 0)),
        pl.BlockSpec((seq, LANES), lambda b: (b, EVEN_W // LANES - 1)),
        full((LANES, hk)), full((1, hk)), full((LANES, hk)), full((1, hk)),
        full(consts[0].shape), full(consts[1].shape), full(consts[2].shape), full(consts[3].shape), full(consts[4].shape),
        full((1, DV_A)),
    ]
    args = [proj, proj, w_gf, b_gf.reshape(1, hk), w_gb, b_gb.reshape(1, hk), *consts, g_norm.reshape(1, DV_A)]
    o_spec = pl.BlockSpec((seq, hv), lambda b: (b, 0))
    o_shape = jax.ShapeDtypeStruct((n_batch * seq, hv), BF16)
    st_spec = pl.BlockSpec((None, hk, DV_A), lambda b: (b, 0, 0))
    if sample:
        in_specs += [st_spec, st_spec]
        args += [ctx[0], ctx[1]]
        out_specs, out_shape = o_spec, o_shape
    else:
        st_shape = jax.ShapeDtypeStruct((n_batch, hk, DV_A), F32)
        out_specs, out_shape = [o_spec, st_spec, st_spec], [o_shape, st_shape, st_shape]
    return pl.pallas_call(
        functools.partial(_gla_kernel, sample=sample),
        grid=(n_batch,),
        in_specs=in_specs,
        out_specs=out_specs,
        out_shape=out_shape,
        scratch_shapes=[pltpu.VMEM((seq, hk), F32), pltpu.VMEM((seq, hk), F32),
                        pltpu.VMEM((seq, hv), F32), pltpu.VMEM((seq, hv), F32),
                        pltpu.VMEM((hk, DV_A), F32), pltpu.VMEM((hk, DV_A), F32),
                        pltpu.VMEM((n_ch, H_A * GLA_CHUNK, hk), BF16), pltpu.VMEM((n_ch, H_A * GLA_CHUNK, hk), BF16),
                        pltpu.VMEM((n_ch, hk, DV_A), F32), pltpu.VMEM((n_ch, hk, DV_A), F32),
                        pltpu.VMEM((n_ch, hk, DV_A), F32), pltpu.VMEM((n_ch, hk, DV_A), F32)],
        compiler_params=_params(1),
        name="gla_sample" if sample else "gla_prompt",
    )(*args)


def _axial_rope(n_tokens, dim):
    rows = n_tokens // GRID_W
    row = np.repeat(np.arange(rows), GRID_W).astype(np.float64)
    col = np.tile(np.arange(GRID_W), rows).astype(np.float64)
    n_freq = dim // 4
    inv = ROPE_THETA ** (-np.arange(n_freq) / n_freq)
    ang = np.concatenate([row[:, None] * inv, col[:, None] * inv], axis=-1)
    return np.cos(ang).astype(np.float32), np.sin(ang).astype(np.float32)


def _filter_features(n_tokens):
    t = np.linspace(0.0, 1.0, n_tokens)[:, None]
    w = 2.0 * np.pi * np.arange(n_tokens)[:, None] / n_tokens
    f = np.linspace(1e-4, FILT_BANDS - 1, FILT_BANDS)[None, :]
    z = np.concatenate([t, np.cos(f * w), -np.sin(f * w)], axis=-1)
    z = np.pad(z, ((0, 0), (0, LANES - FILT_EMB)))
    return jnp.asarray(z, F32), jnp.asarray(t, F32)


_QB_ZERO = H_D * (NOPE_D + ROPE_D)
_QB_PERM = np.array([(NOPE_D + ROPE_D) * (p // MLA_QW) + p % MLA_QW if p % MLA_QW < NOPE_D + ROPE_D else _QB_ZERO
                     for p in range(H_D * MLA_QW)])


def _mla_rope_tables(cos_d, sin_d):
    n, half = cos_d.shape
    zeros = np.zeros((n, half), np.float32)

    def lanes(pre, width):
        pad = np.zeros((n, width - pre.shape[1] - 2 * half), np.float32)
        build = lambda first, second, lead: np.concatenate([lead, first, second, pad], axis=1)
        return (build(cos_d, cos_d, pre), build(-sin_d, zeros, 0 * pre), build(zeros, sin_d, 0 * pre))

    q_tabs = lanes(np.ones((n, NOPE_D), np.float32), MLA_QW)
    k_tabs = lanes(np.zeros((n, 0), np.float32), LANES)
    return tuple(jnp.asarray(t) for t in q_tabs + k_tabs)

EVEN_ROW_GROUPS = ((0, 0, 1536), (1568, 1536, 1024), (1536, EVEN_W - 2 * GATE_RANK, 2 * GATE_RANK))
ODD_ROW_GROUPS = ((0, 0, 1984),)
EVEN_KEEP = (2304, 256)
ODD_KEEP = (1920, LANES)


def kernel(x_prompt, x_sample, state_gla_fwd, state_gla_bwd, cache_gqa_k, cache_gqa_v, cache_mla_ckv, cache_mla_kpe, c, c_ctx, w_mod, b_mod, w_in_even, w_gla_gate_f, b_gla_gate_f, w_gla_gate_b, b_gla_gate_b, g_gla_norm, g_gqa_q, g_gqa_k, w_out_even, w_in_odd, w_hy_conv, b_hy_conv, hy_skip, w_filt1, b_filt1, filt_freq, w_filt2, b_filt2, w_filt3, g_mla_q, w_mla_qb, g_mla_kv, w_mla_kvb, w_out_odd, w_ffn_in, w_ffn_out, g_final):
    cvec = jnp.concatenate([c_ctx[None, :], c, jnp.zeros((8 - 1 - DEC_BATCH, D_MODEL), F32)], axis=0)
    mod = _modulation(cvec, w_mod, b_mod)
    xc, xs, s_row0 = x_prompt.reshape(N_PROMPT, D_MODEL), x_sample.reshape(N_SAMPLE, D_MODEL), 0

    gla_consts = _gla_constants()
    cos_b, sin_b = _axial_rope(DEC_SEQ, HD_B)
    rope_b = (jnp.asarray(np.concatenate([cos_b, cos_b], axis=1)), jnp.asarray(np.concatenate([-sin_b, sin_b], axis=1)))
    cos_d, sin_d = _axial_rope(DEC_SEQ, ROPE_D)
    rope_d = _mla_rope_tables(cos_d, sin_d)
    w_qb_all = jnp.pad(w_mla_qb, ((0, 0), (0, 0), (0, 1)))[:, :, _QB_PERM]
    tabs_c, tabs_s = _dft_tables(SEQ), _dft_tables(DEC_SEQ)
    z_c, t_c = _filter_features(SEQ)
    z_s, t_s = _filter_features(DEC_SEQ)
    deltas = jnp.asarray(np.abs(np.linspace(HY_MIN_DECAY, HY_MAX_DECAY, HY_W))[None, :], F32)

    wt_even = jnp.swapaxes(w_in_even, 1, 2)
    wt_odd = jnp.swapaxes(w_in_odd, 1, 2)

    st_gf, st_gb, st_ckv, st_kpe = [], [], [], []
    new_kv = None
    for i in range(DEPTH):
        j = i // 2
        if i % 2 == 0:
            z0 = LANES - 2 * GATE_RANK
            pad_f = jnp.zeros((LANES, H_A * DK_A), F32).at[z0:z0 + GATE_RANK].set(w_gla_gate_f[j])
            pad_b = jnp.zeros((LANES, H_A * DK_A), F32).at[z0 + GATE_RANK:LANES].set(w_gla_gate_b[j])
            pc, v_new, ps = _in_proj(xc, xs, s_row0, mod, i, wt_even, j, EVEN_ROW_GROUPS, EVEN_W, EVEN_KEEP)
            gate_args = (pad_f, b_gla_gate_f[j], pad_b, b_gla_gate_b[j], g_gla_norm[j], gla_consts)
            a_c, s_f, s_b = _gla(pc, *gate_args, BATCH, SEQ)
            ctx_a = (state_gla_fwd[:, j].reshape(DEC_BATCH, H_A * DK_A, DV_A),
                     state_gla_bwd[:, j].reshape(DEC_BATCH, H_A * DK_A, DV_A))
            a_s = _gla(ps, *gate_args, DEC_BATCH, DEC_SEQ, ctx=ctx_a)
            b_c, *new_kv = _gqa(pc, g_gqa_q[j], g_gqa_k[j], BATCH, SEQ, v_f32=v_new, slot=j, prev=new_kv)
            b_s = _gqa(ps, g_gqa_q[j], g_gqa_k[j], DEC_BATCH, DEC_SEQ, ctx=(cache_gqa_k, cache_gqa_v), rope=rope_b, slot=j)
            w_out = w_out_even
            st_gf.append(s_f.reshape(BATCH, H_A, DK_A, DV_A))
            st_gb.append(s_b.reshape(BATCH, H_A, DK_A, DV_A))
        else:
            pc, kpe_new, ps = _in_proj(xc, xs, s_row0, mod, i, wt_odd, j, ODD_ROW_GROUPS, ODD_W, ODD_KEEP)
            wf1 = jnp.pad(w_filt1[j], ((0, LANES - FILT_EMB), (0, 0)))
            filt_args = (wf1, b_filt1[j], filt_freq[j], w_filt2[j], b_filt2[j], w_filt3[j])
            g_c = _filter_spectrum(z_c, *filt_args, t_c, deltas, tabs_c)
            g_s = _filter_spectrum(z_s, *filt_args, t_s, deltas, tabs_s)
            b_conv = b_hy_conv[j].reshape(1, 3 * HY_W)
            a_c = _hyena(pc, w_hy_conv[j], b_conv, hy_skip[j], g_c[0], g_c[1], tabs_c, BATCH, SEQ)
            a_s = _hyena(ps, w_hy_conv[j], b_conv, hy_skip[j], g_s[0], g_s[1], tabs_s, DEC_BATCH, DEC_SEQ)
            w_qb = w_qb_all[j]
            b_c, ckv_norm = _mla(pc, g_mla_q[j], w_qb, g_mla_kv[j], w_mla_kvb[j], BATCH, SEQ)
            b_s = _mla(ps, g_mla_q[j], w_qb, g_mla_kv[j], w_mla_kvb[j], DEC_BATCH, DEC_SEQ,
                       ctx=(cache_mla_ckv[:, j], cache_mla_kpe[:, j]), rope=rope_d)
            w_out = w_out_odd
            st_ckv.append(ckv_norm.reshape(BATCH, SEQ, KV_RANK))
            st_kpe.append(kpe_new[:, :ROPE_D].reshape(BATCH, SEQ, ROPE_D))
        x_mid = _out_proj([a_c, b_c], [a_s, b_s], w_out, j, xc, xs, s_row0, mod, i, 2)
        if i < DEPTH - 1:
            x_all = _ffn(x_mid, 0, N_PROMPT + N_SAMPLE, mod, i, w_ffn_in, w_ffn_out)
            xc, xs, s_row0 = x_all, x_all, N_PROMPT
        else:
            xc = _ffn(x_mid, 0, N_PROMPT, mod, i, w_ffn_in, w_ffn_out, final_gain=g_final)
            xs = _ffn(x_mid, N_PROMPT, N_SAMPLE, mod, i, w_ffn_in, w_ffn_out, final_gain=g_final)
    y_prompt = xc.reshape(BATCH, SEQ, D_MODEL)
    y_sample = xs.reshape(DEC_BATCH, DEC_SEQ, D_MODEL)
    return (y_prompt, y_sample, jnp.stack(st_gf, axis=1), jnp.stack(st_gb, axis=1), new_kv[0], new_kv[1],
            jnp.stack(st_ckv, axis=1), jnp.stack(st_kpe, axis=1))
```

```python
import functools
import math

import numpy as np
import jax
import jax.numpy as jnp
from jax import lax
from jax.experimental import pallas as pl
from jax.experimental.pallas import tpu as pltpu

F32 = jnp.float32
BF16 = jnp.bfloat16

D_MODEL = 1024
BATCH, SEQ = 16, 256
DEC_BATCH, DEC_SEQ = 2, 1024
DEPTH = 4
PAST_LEN = 512
GRID_W = 64
HALF_W = D_MODEL // 2
H_A, DV_A, DK_A = 4, 128, 64
GATE_RANK = 16
GLA_TAU = 16.0
GLA_CHUNK = 64
HD_B, H_B, KV_B = 128, 4, 2
HY_W = HALF_W
FILT_EMB, FILT_HID = 33, 64
FILT_BANDS = (FILT_EMB - 1) // 2
HY_MIN_DECAY = math.log(1e-2) / 1.5
HY_MAX_DECAY = math.log(1e-2) / 0.3
H_D, V_D, NOPE_D, ROPE_D = 4, 128, 128, 64
Q_RANK, KV_RANK = 256, 128
FFN_H = 2816
ROPE_THETA = 10000.0
EPS = 1e-6

LANES = 128
VMEM_LIMIT = 56 * 1024 * 1024

MOD_ROWS = 1024
TM = 1024
TM_IN = 512
TM_FFN = 2048
EVEN_W = 2688
ODD_W = 2048
FFN_TN = 256
QB = 256


def _params(n_grid):
    return pltpu.CompilerParams(dimension_semantics=("arbitrary",) * n_grid, vmem_limit_bytes=VMEM_LIMIT)


def _nt(a, b):
    return lax.dot_general(a, b, (((1,), (1,)), ((), ())), preferred_element_type=F32)


def _mm(a, b):
    return jnp.dot(a, b, preferred_element_type=F32)


def _rms(x):
    return x * lax.rsqrt(jnp.mean(x * x, axis=-1, keepdims=True) + EPS)


def _mod_kernel(c_ref, w_ref, b_ref, o_ref):
    cv = c_ref[...]
    s = cv * jax.nn.sigmoid(cv)
    o_ref[...] = _mm(s.astype(BF16), w_ref[...].astype(BF16)) + b_ref[...]


def _modulation(cvec, w_mod, b_mod):
    return pl.pallas_call(
        _mod_kernel,
        grid=(DEPTH, 6),
        in_specs=[
            pl.BlockSpec((8, D_MODEL), lambda l, n: (0, 0)),
            pl.BlockSpec((None, D_MODEL, D_MODEL), lambda l, n: (l, 0, n)),
            pl.BlockSpec((None, 1, D_MODEL), lambda l, n: (l, 0, n)),
        ],
        out_specs=pl.BlockSpec((None, None, 8, D_MODEL), lambda l, n: (l, n, 0, 0)),
        out_shape=jax.ShapeDtypeStruct((DEPTH, 6, 8, D_MODEL), F32),
        compiler_params=_params(2),
        name="adaln_mod",
    )(cvec, w_mod, b_mod.reshape(DEPTH, 1, 6 * D_MODEL))


N_PROMPT = BATCH * SEQ
N_SAMPLE = DEC_BATCH * DEC_SEQ


def _stack_out(n_batch, n_slots, tail, slot, first):
    zeros = (0,) * len(tail)
    if first:
        spec = pl.BlockSpec((None, n_slots) + tail, lambda b, *_: (b, 0) + zeros)
    else:
        spec = pl.BlockSpec((None, None) + tail, lambda b, *_: (b, slot) + zeros)
    return spec, jax.ShapeDtypeStruct((n_batch, n_slots) + tail, F32)


def _store_slot(ref, slot, owns_stack, value):
    if not owns_stack:
        ref[...] = value
        return
    for s in range(ref.shape[0]):
        ref[s] = value if s == slot else jnp.zeros_like(value)


def _stream_index_maps(tile_rows, s_row0):
    n_c = N_PROMPT // tile_rows
    return (lambda i: (jnp.minimum(i, n_c - 1), 0)), (lambda i: (s_row0 // tile_rows + jnp.maximum(i - n_c, 0), 0))


def _in_proj_kernel(xc_ref, xs_ref, sh_ref, sc_ref, wt_ref, oc_ref, keep_ref, os_ref, wb_ref, *, row_groups, keep, n_c):
    i = pl.program_id(0)

    @pl.when(i == 0)
    def _():
        wb_ref[...] = jnp.zeros_like(wb_ref)
        for src, dst, size in row_groups:
            wb_ref[dst:dst + size, :] = wt_ref[src:src + size, :].astype(BF16)

    def project(x_ref, g):
        h = (_rms(x_ref[...]) * (1.0 + sc_ref[pl.ds(g, 1), :]) + sh_ref[pl.ds(g, 1), :]).astype(BF16)
        return _nt(h, wb_ref[...])

    @pl.when(i < n_c)
    def _():
        y = project(xc_ref, 0)
        oc_ref[...] = y.astype(oc_ref.dtype)
        keep_ref[...] = y[:, keep[0]:keep[0] + keep[1]]

    @pl.when(i >= n_c)
    def _():
        os_ref[...] = project(xs_ref, 1 + (i - n_c) // (MOD_ROWS // TM_IN)).astype(os_ref.dtype)


def _in_proj(xc, xs, s_row0, mod, layer, wt, w_layer, row_groups, n, keep):
    n_c, n_s = N_PROMPT // TM_IN, N_SAMPLE // TM_IN
    xc_idx, xs_idx = _stream_index_maps(TM_IN, s_row0)
    c_idx, s_idx = _stream_index_maps(TM_IN, 0)
    return pl.pallas_call(
        functools.partial(_in_proj_kernel, row_groups=row_groups, keep=keep, n_c=n_c),
        grid=(n_c + n_s,),
        in_specs=[pl.BlockSpec((TM_IN, D_MODEL), xc_idx), pl.BlockSpec((TM_IN, D_MODEL), xs_idx),
                  pl.BlockSpec((None, None, 8, D_MODEL), lambda i: (layer, 0, 0, 0)),
                  pl.BlockSpec((None, None, 8, D_MODEL), lambda i: (layer, 1, 0, 0)),
                  pl.BlockSpec((None, wt.shape[1], D_MODEL), lambda i: (w_layer, 0, 0), pipeline_mode=pl.Buffered(1))],
        out_specs=[pl.BlockSpec((TM_IN, n), c_idx), pl.BlockSpec((TM_IN, keep[1]), c_idx), pl.BlockSpec((TM_IN, n), s_idx)],
        out_shape=[jax.ShapeDtypeStruct((N_PROMPT, n), BF16), jax.ShapeDtypeStruct((N_PROMPT, keep[1]), F32),
                   jax.ShapeDtypeStruct((N_SAMPLE, n), BF16)],
        scratch_shapes=[pltpu.VMEM((n, D_MODEL), BF16)],
        compiler_params=_params(1),
        name="norm_mod_proj",
    )(xc, xs, mod, mod, wt)


def _ffn_kernel(x_ref, sh_ref, sc_ref, gate_ref, wg_ref, wu_ref, wd_ref, *refs, first_sub, final):
    (gf_ref, o_ref, h_ref) = refs if final else (None,) + refs
    n_sub = x_ref.shape[0] // MOD_ROWS
    subs = [(slice(s * MOD_ROWS, (s + 1) * MOD_ROWS),
             jnp.maximum(first_sub + pl.program_id(0) * n_sub + s - (N_PROMPT // MOD_ROWS - 1), 0)) for s in range(n_sub)]

    @pl.when(pl.program_id(1) == 0)
    def _():
        for rows, g in subs:
            x = x_ref[rows, :]
            o_ref[rows, :] = x
            h_ref[rows, :] = (_rms(x) * (1.0 + sc_ref[pl.ds(g, 1), :]) + sh_ref[pl.ds(g, 1), :]).astype(BF16)

    wg = wg_ref[...].astype(BF16)
    wu = wu_ref[...].astype(BF16)
    wd = wd_ref[...].astype(BF16)
    for rows, g in subs:
        h = h_ref[rows, :]
        a = _mm(h, wg)
        act = (a * jax.nn.sigmoid(a) * _mm(h, wu)).astype(BF16)
        o_ref[rows, :] += gate_ref[pl.ds(g, 1), :] * _mm(act, wd)

    if final:
        @pl.when(pl.program_id(1) == pl.num_programs(1) - 1)
        def _():
            for rows, _ in subs:
                o_ref[rows, :] = _rms(o_ref[rows, :]) * gf_ref[...]


def _ffn(x, row0, m, mod, layer, w_in, w_out, final_gain=None):
    nj = FFN_H // FFN_TN
    tile0 = row0 // TM_FFN
    mod_spec = lambda k: pl.BlockSpec((None, None, 8, D_MODEL), lambda i, j: (layer, k, 0, 0))
    final = final_gain is not None
    extra_specs = [pl.BlockSpec((1, D_MODEL), lambda i, j: (0, 0))] if final else []
    extra_args = [final_gain.reshape(1, D_MODEL)] if final else []
    return pl.pallas_call(
        functools.partial(_ffn_kernel, first_sub=row0 // MOD_ROWS, final=final),
        grid=(m // TM_FFN, nj),
        in_specs=[pl.BlockSpec((TM_FFN, D_MODEL), lambda i, j: (tile0 + i, 0)), mod_spec(3), mod_spec(4), mod_spec(5),
                  pl.BlockSpec((None, D_MODEL, FFN_TN), lambda i, j: (layer, 0, j)),
                  pl.BlockSpec((None, D_MODEL, FFN_TN), lambda i, j: (layer, 0, j + nj)),
                  pl.BlockSpec((None, FFN_TN, D_MODEL), lambda i, j: (layer, j, 0))] + extra_specs,
        out_specs=pl.BlockSpec((TM_FFN, D_MODEL), lambda i, j: (i, 0)),
        out_shape=jax.ShapeDtypeStruct((m, D_MODEL), F32),
        scratch_shapes=[pltpu.VMEM((TM_FFN, D_MODEL), BF16)],
        compiler_params=_params(2),
        name="ffn_residual",
    )(x, mod, mod, mod, w_in, w_in, w_out, *extra_args)


def _proj_res_kernel(ac0_ref, ac1_ref, as0_ref, as1_ref, w0_ref, w1_ref, xc_ref, xs_ref, gate_ref, o_ref, *, n_c):
    i = pl.program_id(0)

    def mix(a0_ref, a1_ref, x_ref, g):
        acc = _mm(a0_ref[...], w0_ref[...].astype(BF16)) + _mm(a1_ref[...], w1_ref[...].astype(BF16))
        return x_ref[...] + gate_ref[pl.ds(g, 1), :] * acc

    @pl.when(i < n_c)
    def _():
        o_ref[...] = mix(ac0_ref, ac1_ref, xc_ref, 0)

    @pl.when(i >= n_c)
    def _():
        o_ref[...] = mix(as0_ref, as1_ref, xs_ref, 1 + (i - n_c))


def _out_proj(acts_c, acts_s, w, w_layer, xc, xs, s_row0, mod, layer, k_gate):
    n_c, n_s = N_PROMPT // TM, N_SAMPLE // TM
    kw = acts_c[0].shape[1]
    xc_idx, xs_idx = _stream_index_maps(TM, s_row0)
    c_idx, s_idx = _stream_index_maps(TM, 0)
    w_specs = [pl.BlockSpec((None, kw, D_MODEL), functools.partial(lambda i, p: (w_layer, p, 0), p=p),
                            pipeline_mode=pl.Buffered(1)) for p in range(2)]
    return pl.pallas_call(
        functools.partial(_proj_res_kernel, n_c=n_c),
        grid=(n_c + n_s,),
        in_specs=[pl.BlockSpec((TM, kw), c_idx)] * 2 + [pl.BlockSpec((TM, kw), s_idx)] * 2 + w_specs + [
            pl.BlockSpec((TM, D_MODEL), xc_idx), pl.BlockSpec((TM, D_MODEL), xs_idx),
            pl.BlockSpec((None, None, 8, D_MODEL), lambda i: (layer, k_gate, 0, 0)),
        ],
        out_specs=pl.BlockSpec((TM, D_MODEL), lambda i: (i, 0)),
        out_shape=jax.ShapeDtypeStruct((N_PROMPT + N_SAMPLE, D_MODEL), F32),
        compiler_params=_params(1),
        name="out_proj_residual",
    )(*acts_c, *acts_s, w, w, xc, xs, mod)


def _gqa_kernel(*refs, sample, has_prev=False, slot=0):
    if sample:
        q_ref, k_ref, v_ref, gq_ref, gk_ref, ck_ref, cv_ref, cos_ref, sin_ref, o_ref, kb_ref, vb_ref = refs
    else:
        n_in = 8 if has_prev else 6
        q_ref, k_ref, v_ref, gq_ref, gk_ref, vf_ref = refs[:6]
        o_ref, kc_ref, vc_ref, kb_ref, vb_ref = refs[n_in:]
        if not has_prev:
            stacks = (kc_ref, vc_ref)
            kc_ref, vc_ref = kc_ref.at[slot], vc_ref.at[slot]
    qi = pl.program_id(1)
    n_new = k_ref.shape[0]
    past = PAST_LEN if sample else 0
    rep = H_B // KV_B

    @pl.when(qi == 0)
    def _():
        if not (sample or has_prev):
            for ref in stacks:
                for other in range(ref.shape[0]):
                    if other != slot:
                        ref[other] = jnp.zeros(ref.shape[1:], ref.dtype)
        for g in range(KV_B):
            sl = slice(HD_B * g, HD_B * (g + 1))
            kn = _rms(k_ref[:, sl].astype(F32)) * gk_ref[...]
            if sample:
                kb_ref[0:past, sl] = ck_ref[:, g, :].astype(BF16)
                vb_ref[g, 0:past, 0:HD_B] = cv_ref[:, g, :].astype(BF16)
                kn = kn * cos_ref[...] + pltpu.roll(kn, HD_B // 2, 1) * sin_ref[...]
            else:
                kc_ref[:, g, :] = kn
                vc_ref[:, g, :] = vf_ref[:, sl]
            kb_ref[past:past + n_new, sl] = kn.astype(BF16)
            vb_ref[g, past:past + n_new, 0:HD_B] = v_ref[:, sl].astype(BF16)
            vb_ref[g, :, HD_B:] = jnp.ones((past + n_new, HD_B), BF16)

    r0 = pl.multiple_of(qi * QB, QB)
    qs = []
    for h in range(H_B):
        qn = _rms(q_ref[:, HD_B * h:HD_B * (h + 1)].astype(F32)) * gq_ref[...]
        if sample:
            qn = qn * cos_ref[pl.ds(r0, QB), :] + pltpu.roll(qn, HD_B // 2, 1) * sin_ref[pl.ds(r0, QB), :]
        qs.append((qn * (HD_B ** -0.5)).astype(BF16))
    scores = [_nt(qs[h], kb_ref[:, HD_B * (h // rep):HD_B * (h // rep + 1)]) for h in range(H_B)]
    weights = [jnp.exp(s - jnp.max(s, axis=-1, keepdims=True)).astype(BF16) for s in scores]
    sums = [_mm(weights[h], vb_ref[h // rep]) for h in range(H_B)]
    for h in range(H_B):
        o_ref[:, HD_B * h:HD_B * (h + 1)] = (sums[h][:, :HD_B] / sums[h][:, HD_B:]).astype(o_ref.dtype)


def _gqa(proj, g_q, g_k, n_batch, seq, ctx=None, rope=None, v_f32=None, slot=0, prev=None):
    sample = ctx is not None
    m = n_batch * seq
    nq = seq // QB
    n_even = (DEPTH + 1) // 2
    in_specs = [
        pl.BlockSpec((QB, 512), lambda b, i: (b * nq + i, 3)),
        pl.BlockSpec((seq, 256), lambda b, i: (b, 8)),
        pl.BlockSpec((seq, 256), lambda b, i: (b, 9)),
        pl.BlockSpec((1, HD_B), lambda b, i: (0, 0)),
        pl.BlockSpec((1, HD_B), lambda b, i: (0, 0)),
    ]
    args = [proj, proj, proj, g_q.reshape(1, HD_B), g_k.reshape(1, HD_B)]
    o_spec = pl.BlockSpec((QB, 512), lambda b, i: (b * nq + i, 0))
    o_shape = jax.ShapeDtypeStruct((m, 512), BF16)
    aliases = {}
    if sample:
        cache_spec = pl.BlockSpec((None, None, PAST_LEN, KV_B, HD_B), lambda b, i: (b, slot, 0, 0, 0))
        in_specs += [
            cache_spec, cache_spec,
            pl.BlockSpec((seq, HD_B), lambda b, i: (0, 0)),
            pl.BlockSpec((seq, HD_B), lambda b, i: (0, 0)),
        ]
        args += [ctx[0], ctx[1], rope[0], rope[1]]
        out_specs, out_shape = o_spec, o_shape
    else:
        in_specs.append(pl.BlockSpec((seq, KV_B * HD_B), lambda b, i: (b, 0)))
        args.append(v_f32)
        if prev is not None:
            in_specs += [pl.BlockSpec(memory_space=pl.ANY)] * 2
            aliases = {len(args): 1, len(args) + 1: 2}
            args += list(prev)
        if prev is None:
            new_spec = pl.BlockSpec((None, n_even, seq, KV_B, HD_B), lambda b, i: (b, 0, 0, 0, 0))
        else:
            new_spec = pl.BlockSpec((None, None, seq, KV_B, HD_B), lambda b, i: (b, slot, 0, 0, 0))
        new_shape = jax.ShapeDtypeStruct((n_batch, n_even, seq, KV_B, HD_B), F32)
        out_specs, out_shape = [o_spec, new_spec, new_spec], [o_shape, new_shape, new_shape]
    n_keys = seq + (PAST_LEN if sample else 0)
    return pl.pallas_call(
        functools.partial(_gqa_kernel, sample=sample, has_prev=prev is not None, slot=slot),
        grid=(n_batch, nq),
        in_specs=in_specs,
        out_specs=out_specs,
        out_shape=out_shape,
        input_output_aliases=aliases,
        scratch_shapes=[pltpu.VMEM((n_keys, KV_B * HD_B), BF16), pltpu.VMEM((KV_B, n_keys, 2 * HD_B), BF16)],
        compiler_params=_params(2),
        name="gqa_sample" if sample else "gqa_prompt",
    )(*args)


MLA_QW = 2 * LANES
MLA_HW = 4 * LANES


def _rotate_pairs(x, cos_t, sin_lo, sin_hi):
    w = x.shape[1]
    return x * cos_t + pltpu.roll(x, ROPE_D // 2, 1) * sin_hi + pltpu.roll(x, w - ROPE_D // 2, 1) * sin_lo


def _mla_kernel(*refs, sample, has_prev=False, slot=0):
    if sample:
        (cq_ref, ckv_ref, kpe_ref, gq_ref, wqb_ref, gkv_ref, wkvb_ref, cckv_ref, ckpe_ref,
         qc_ref, ql_ref, qh_ref, kc_ref, kl_ref, kh_ref, o_ref, kv_s) = refs
    else:
        cq_ref, ckv_ref, kpe_ref, gq_ref, wqb_ref, gkv_ref, wkvb_ref, kf_ref = refs[:8]
        o_ref, ckvn_ref, kpeo_ref, kv_s = refs[8 + (2 if has_prev else 0):]
    qi = pl.program_id(1)
    n_new = ckv_ref.shape[0]
    past = PAST_LEN if sample else 0

    def stage_kv(rows, kv, kpe_block):
        for h in range(H_D):
            kv_s[rows, MLA_HW * h:MLA_HW * h + NOPE_D] = kv[:, 256 * h:256 * h + NOPE_D].astype(BF16)
            kv_s[rows, MLA_HW * h + NOPE_D:MLA_HW * h + MLA_QW] = kpe_block
            kv_s[rows, MLA_HW * h + MLA_QW:MLA_HW * h + MLA_QW + V_D] = kv[:, 256 * h + NOPE_D:256 * (h + 1)].astype(BF16)

    @pl.when(qi == 0)
    def _():
        wkvb = wkvb_ref[...].astype(BF16)
        ckvn = _rms(ckv_ref[...].astype(F32)) * gkv_ref[...]
        if not sample:
            _store_slot(ckvn_ref, slot, not has_prev, ckvn)
            _store_slot(kpeo_ref, slot, not has_prev, kf_ref[:, 0:ROPE_D])
        kpe = kpe_ref[...]
        if sample:
            ctx_kpe = jnp.concatenate([ckpe_ref[...], jnp.zeros((past, LANES - ROPE_D), F32)], axis=1)
            stage_kv(slice(0, past), _mm(cckv_ref[...].astype(BF16), wkvb), ctx_kpe.astype(BF16))
            kpe = _rotate_pairs(kpe.astype(F32), kc_ref[...], kl_ref[...], kh_ref[...]).astype(BF16)
        stage_kv(slice(past, past + n_new), _mm(ckvn.astype(BF16), wkvb), kpe)
        for h in range(H_D):
            kv_s[:, MLA_HW * h + MLA_QW + V_D:MLA_HW * (h + 1)] = jnp.ones((past + n_new, V_D), BF16)

    q = _mm((_rms(cq_ref[...].astype(F32)) * gq_ref[...]).astype(BF16), wqb_ref[...].astype(BF16))
    q = q * ((NOPE_D + ROPE_D) ** -0.5)
    qs = []
    for h in range(H_D):
        q_h = q[:, MLA_QW * h:MLA_QW * (h + 1)]
        if sample:
            rows = pl.ds(pl.multiple_of(qi * QB, QB), QB)
            q_h = _rotate_pairs(q_h, qc_ref[rows, :], ql_ref[rows, :], qh_ref[rows, :])
        qs.append(q_h.astype(BF16))
    scores = [_nt(qs[h], kv_s[:, MLA_HW * h:MLA_HW * h + MLA_QW]) for h in range(H_D)]
    weights = [jnp.exp(s - jnp.max(s, axis=-1, keepdims=True)).astype(BF16) for s in scores]
    sums = [_mm(weights[h], kv_s[:, MLA_HW * h + MLA_QW:MLA_HW * (h + 1)]) for h in range(H_D)]
    for h in range(H_D):
        o_ref[:, V_D * h:V_D * (h + 1)] = (sums[h][:, :V_D] / sums[h][:, V_D:]).astype(o_ref.dtype)


def _mla(proj, g_q, w_qb, g_kv, w_kvb, n_batch, seq, ctx=None, rope=None, kpe_f32=None, slot=0, prev=None):
    sample = ctx is not None
    m = n_batch * seq
    nq = seq // QB
    in_specs = [
        pl.BlockSpec((QB, Q_RANK), lambda b, i: (b * nq + i, 6)),
        pl.BlockSpec((seq, KV_RANK), lambda b, i: (b, 14)),
        pl.BlockSpec((seq, LANES), lambda b, i: (b, 15)),
        pl.BlockSpec((1, Q_RANK), lambda b, i: (0, 0)),
        pl.BlockSpec((Q_RANK, H_D * MLA_QW), lambda b, i: (0, 0)),
        pl.BlockSpec((1, KV_RANK), lambda b, i: (0, 0)),
        pl.BlockSpec((KV_RANK, 1024), lambda b, i: (0, 0)),
    ]
    args = [proj, proj, proj, g_q.reshape(1, Q_RANK), w_qb, g_kv.reshape(1, KV_RANK), w_kvb]
    o_spec = pl.BlockSpec((QB, 512), lambda b, i: (b * nq + i, 0))
    o_shape = jax.ShapeDtypeStruct((m, 512), BF16)
    aliases = {}
    if sample:
        in_specs += [
            pl.BlockSpec((None, PAST_LEN, KV_RANK), lambda b, i: (b, 0, 0)),
            pl.BlockSpec((None, PAST_LEN, ROPE_D), lambda b, i: (b, 0, 0)),
        ] + [pl.BlockSpec((seq, MLA_QW), lambda b, i: (0, 0))] * 3 + [pl.BlockSpec((seq, LANES), lambda b, i: (0, 0))] * 3
        args += [ctx[0], ctx[1], *rope]
        out_specs, out_shape = o_spec, o_shape
    else:
        in_specs.append(pl.BlockSpec((seq, LANES), lambda b, i: (b, 0)))
        args.append(kpe_f32)
        if prev is not None:
            in_specs += [pl.BlockSpec(memory_space=pl.ANY)] * 2
            aliases = {len(args): 1, len(args) + 1: 2}
            args += list(prev)
        ckv_spec, ckv_shape = _stack_out(n_batch, DEPTH // 2, (seq, KV_RANK), slot, prev is None)
        kpe_spec, kpe_shape = _stack_out(n_batch, DEPTH // 2, (seq, ROPE_D), slot, prev is None)
        out_specs, out_shape = [o_spec, ckv_spec, kpe_spec], [o_shape, ckv_shape, kpe_shape]
    n_keys = seq + (PAST_LEN if sample else 0)
    return pl.pallas_call(
        functools.partial(_mla_kernel, sample=sample, has_prev=prev is not None, slot=slot),
        grid=(n_batch, nq),
        in_specs=in_specs,
        out_specs=out_specs,
        out_shape=out_shape,
        input_output_aliases=aliases,
        scratch_shapes=[pltpu.VMEM((n_keys, H_D * MLA_HW), BF16)],
        compiler_params=_params(2),
        name="mla_sample" if sample else "mla_prompt",
    )(*args)


def _dft(table, x):
    return _mm(table.astype(BF16), x.astype(BF16))


def _filter_kernel(z_ref, wf1_ref, bf1_ref, fr_ref, wf2_ref, bf2_ref, wf3_ref, t_ref, dl_ref,
                   c_ref, s_ref, gre_ref, gim_ref):
    n_tok = z_ref.shape[0]
    fr = fr_ref[...]
    hid = jnp.sin(fr * (_mm(z_ref[...].astype(BF16), wf1_ref[...].astype(BF16)) + bf1_ref[...]))
    hid = jnp.sin(fr * (_mm(hid.astype(BF16), wf2_ref[...].astype(BF16)) + bf2_ref[...]))
    filt = _mm(hid.astype(BF16), wf3_ref[...].astype(BF16))
    decay = jnp.exp(-t_ref[...] * dl_ref[...])
    row = lax.broadcasted_iota(jnp.int32, (n_tok, 1), 0)
    h_f = filt[:, :HY_W] * decay
    h_b = jnp.where(row == 0, 0.0, filt[:, HY_W:] * decay)
    p, m = h_f + h_b, h_f - h_b
    g_re = _dft(c_ref[...], p)
    g_im = _dft(s_ref[...], m)
    sign = jnp.where(row % 2 == 0, 1.0, -1.0)
    nyquist = jnp.sum(p * sign, axis=0, keepdims=True)
    g_im = jnp.where(row == 0, nyquist, g_im)
    wk = jnp.where(row == 0, 0.5 / n_tok, 1.0 / n_tok)
    gre_ref[...] = g_re * wk
    gim_ref[...] = g_im * wk


def _filter_spectrum(z, wf1, bf1, freq, wf2, bf2, wf3, t_col, deltas, tabs):
    n_tok = z.shape[0]
    out = jax.ShapeDtypeStruct((n_tok, HY_W), F32)
    return pl.pallas_call(
        _filter_kernel,
        out_shape=[out, out],
        compiler_params=pltpu.CompilerParams(vmem_limit_bytes=VMEM_LIMIT),
        name="hyena_filter",
    )(z, wf1, bf1.reshape(1, FILT_HID), freq.reshape(1, FILT_HID), wf2, bf2.reshape(1, FILT_HID), wf3,
      t_col, deltas, tabs[0], tabs[1])


HY_ROWS = 1024


def _hyena_channels(seq):
    return HY_W if seq <= 256 else HY_W // 2


def _hyena_kernel(u0_ref, u1_ref, u2_ref, w0_ref, w1_ref, w2_ref, b0_ref, b1_ref, b2_ref, skip_ref,
                  gre_ref, gim_ref, cf_ref, sf_ref, stf_ref, o_ref, c_ref, s_ref, st_ref):
    seq = c_ref.shape[0]
    n_rows = u0_ref.shape[0]
    n_seq = n_rows // seq
    pos = lax.broadcasted_iota(jnp.int32, (n_rows, 1), 0) % seq

    @pl.when((pl.program_id(0) == 0) & (pl.program_id(1) == 0))
    def _():
        c_ref[...] = cf_ref[...].astype(BF16)
        s_ref[...] = sf_ref[...].astype(BF16)
        st_ref[...] = stf_ref[...].astype(BF16)

    def short_conv(u_ref, w_ref, b_ref):
        x, w = u_ref[...].astype(F32), w_ref[...]
        prev = jnp.where(pos == 0, 0.0, pltpu.roll(x, 1, 0))
        nxt = jnp.where(pos == seq - 1, 0.0, pltpu.roll(x, n_rows - 1, 0))
        return prev * w[0:1] + x * w[1:2] + nxt * w[2:3] + b_ref[...]

    def side_by_side(a):
        return a if n_seq == 1 else jnp.concatenate([a[s * seq:(s + 1) * seq] for s in range(n_seq)], axis=1)

    def stacked(a):
        ct = a.shape[1] // n_seq
        return a if n_seq == 1 else jnp.concatenate([a[:, s * ct:(s + 1) * ct] for s in range(n_seq)], axis=0)

    x0 = short_conv(u0_ref, w0_ref, b0_ref)
    gv = short_conv(u1_ref, w1_ref, b1_ref) * short_conv(u2_ref, w2_ref, b2_ref)
    sig = side_by_side(gv).astype(BF16)
    u_re = _mm(c_ref[...], sig)
    u_im = _mm(s_ref[...], sig)
    g_re = jnp.concatenate([gre_ref[...]] * n_seq, axis=1)
    g_im = jnp.concatenate([gim_ref[...]] * n_seq, axis=1)
    bin0 = lax.broadcasted_iota(jnp.int32, (seq, 1), 0) == 0
    p_im = u_im * g_im
    y_re = u_re * g_re - jnp.where(bin0, 0.0, p_im)
    y_im = jnp.where(bin0, p_im, u_re * g_im + u_im * g_re)
    y = stacked(_mm(c_ref[...], y_re.astype(BF16)) + _mm(st_ref[...], y_im.astype(BF16)))
    o_ref[...] = (x0 * (y + gv * skip_ref[...])).astype(o_ref.dtype)


def _hyena(proj, w_conv, b_conv, skip, g_re, g_im, tabs, n_batch, seq):
    ct = _hyena_channels(seq)
    nct = HY_W // ct
    u_specs = [pl.BlockSpec((HY_ROWS, ct), functools.partial(lambda b, c, g: (b, g * nct + c), g=g)) for g in range(3)]
    w_specs = [pl.BlockSpec((3, ct), functools.partial(lambda b, c, g: (0, g * nct + c), g=g)) for g in range(3)]
    b_specs = [pl.BlockSpec((1, ct), functools.partial(lambda b, c, g: (0, g * nct + c), g=g)) for g in range(3)]
    tab_spec = pl.BlockSpec((seq, seq), lambda b, c: (0, 0))
    return pl.pallas_call(
        _hyena_kernel,
        grid=(n_batch * seq // HY_ROWS, nct),
        in_specs=u_specs + w_specs + b_specs + [
            pl.BlockSpec((1, ct), lambda b, c: (0, c)),
            pl.BlockSpec((seq, ct), lambda b, c: (0, c)),
            pl.BlockSpec((seq, ct), lambda b, c: (0, c)),
        ] + [tab_spec] * 3,
        out_specs=pl.BlockSpec((HY_ROWS, ct), lambda b, c: (b, c)),
        out_shape=jax.ShapeDtypeStruct((n_batch * seq, HY_W), BF16),
        scratch_shapes=[pltpu.VMEM((seq, seq), BF16)] * 3,
        compiler_params=_params(2),
        name="hyena_conv",
    )(proj, proj, proj, w_conv, w_conv, w_conv, b_conv, b_conv, b_conv, skip.reshape(1, HY_W), g_re, g_im, *tabs)


def _dft_tables(n_tok):
    k = np.arange(n_tok)[:, None]
    s = np.arange(n_tok)[None, :]
    ang = ((k * s) % (2 * n_tok)) * (np.pi / n_tok)
    cos_t = np.cos(ang)
    sin_f = np.where(k == 0, np.where(s % 2 == 0, 1.0, -1.0), -np.sin(ang))
    return [jnp.asarray(t, F32) for t in (cos_t, sin_f, sin_f.T)]


GLA_LEVELS = (32, 16, 8, 4, 2, 1)
GLA_SAFE_DECAY = 60.0
GLA_GROUP = 2


def _gla_constants():
    c = GLA_CHUNK
    idx = np.arange(c)
    i, t = idx[:, None], idx[None, :]
    masks = []
    for s in GLA_LEVELS:
        upper = (idx % (2 * s)) >= s
        masks.append(((i // (2 * s)) == (t // (2 * s))) & upper[:, None] & (~upper)[None, :])
    masks.append(i == t)
    tri = t <= i
    fwd_m = np.stack([np.tile(m, (H_A, 1)) for m in masks]).astype(np.float32)
    bwd_m = np.stack([np.tile(m[::-1, ::-1], (H_A, 1)) for m in masks]).astype(np.float32)
    head_of_row = np.repeat(np.arange(H_A), c)[:, None]
    head_of_lane = np.repeat(np.arange(H_A), DK_A)[None, :]
    head_mask = head_of_row == head_of_lane
    return (jnp.asarray(tri, BF16), jnp.asarray(tri[::-1, ::-1], BF16), jnp.asarray(fwd_m), jnp.asarray(bwd_m),
            jnp.asarray(head_mask, BF16))


def _pair_reference(b, s, backward, row):
    c = GLA_CHUNK
    ref = s if backward else s - 1
    if 2 * s >= 8:
        pieces = [jnp.broadcast_to(b[p * 2 * s + ref:p * 2 * s + ref + 1, :], (2 * s, b.shape[1]))
                  for p in range(c // (2 * s))]
        return pieces[0] if len(pieces) == 1 else jnp.concatenate(pieces, axis=0)
    pos = row % (2 * s)
    out = None
    for o in range(2 * s):
        d = ref - o
        shifted = b if d == 0 else pltpu.roll(b, (-d) % c, 0)
        out = shifted if out is None else jnp.where(pos == o, shifted, out)
    return out


def _chunk_log_decay(la, t_ref):
    l1 = la.astype(BF16)
    r1 = la - l1.astype(F32)
    l2 = r1.astype(BF16)
    l3 = (r1 - l2.astype(F32)).astype(BF16)
    tmat = t_ref[...]
    return _mm(tmat, l1) + _mm(tmat, l2) + _mm(tmat, l3)


def _stack_heads(a, hm):
    ab = a.astype(BF16)
    return jnp.concatenate([ab] * H_A, axis=0) * hm


def _state_terms(k, v, b, b_last):
    c = GLA_CHUNK
    k_rest = (k * jnp.exp(b_last - b)).T
    carry = jnp.broadcast_to(jnp.exp(b_last), (2 * c, b.shape[1])).T
    return k_rest.astype(BF16), carry


def _gla_chunk(q, k, v, la, t_ref, m_ref, hm, s_ref, backward):
    c = GLA_CHUNK
    b = _chunk_log_decay(la, t_ref)
    row = lax.broadcasted_iota(jnp.int32, (c, 1), 0)
    last = 0 if backward else c - 1
    b_last = b[last:last + 1, :]
    scores = _nt(_stack_heads(q, hm), k.astype(BF16)) * m_ref[len(GLA_LEVELS)]
    for lvl, s in enumerate(GLA_LEVELS):
        is_query = (row % (2 * s) < s) if backward else (row % (2 * s) >= s)
        delta = b - _pair_reference(b, s, backward, row)
        x = jnp.exp(jnp.where(is_query, delta, -delta))
        scores = scores + _nt(_stack_heads(q * x, hm), (k * x).astype(BF16)) * m_ref[lvl]
    scores = scores.astype(BF16)
    state = s_ref[...]
    inter = _mm(_stack_heads(q * jnp.exp(b), hm), state.astype(BF16))
    k_rest, carry = _state_terms(k, v, b, b_last)
    outs = []
    for h in range(H_A):
        rows = slice(c * h, c * (h + 1))
        v_h = v[:, DV_A * h:DV_A * (h + 1)]
        outs.append(_mm(scores[rows], v_h) + inter[rows])
        s_ref[rows, :] = state[rows] * carry[rows] + _mm(k_rest[rows], v_h)
    return jnp.concatenate(outs, axis=1)


def _gla_local(items, hm):
    c = GLA_CHUNK
    bs = [_chunk_log_decay(la, t_ref) for _, _, _, la, t_ref, _, _ in items]
    b_lasts = [b[(0 if it[6] else c - 1):(0 if it[6] else c - 1) + 1, :] for b, it in zip(bs, items)]
    q_decayed = [_stack_heads(it[0] * jnp.exp(b), hm) for it, b in zip(items, bs)]
    k_grown = [(it[1] * jnp.exp(-b)).astype(BF16) for it, b in zip(items, bs)]
    raw = [_nt(qd, kg) for qd, kg in zip(q_decayed, k_grown)]
    masked = [r * it[5] for r, it in zip(raw, items)]
    terms = [_state_terms(it[1], it[2], b, bl) for it, b, bl in zip(items, bs, b_lasts)]
    out = []
    for n in range(0, len(items), 2):
        v = items[n][2]
        heads = [(slice(c * h, c * (h + 1)), v[:, DV_A * h:DV_A * (h + 1)]) for h in range(H_A)]
        both = (masked[n] + masked[n + 1]).astype(BF16)
        intra = jnp.concatenate([_mm(both[rows], v_h) for rows, v_h in heads], axis=1)
        for m in (n, n + 1):
            k_rest, carry = terms[m]
            incr = jnp.concatenate([_mm(k_rest[rows], v_h) for rows, v_h in heads], axis=0)
            out.append((intra if m == n else None, q_decayed[m], incr, carry))
    return out


def _gla_kernel(*refs, sample, has_prev=False, slot=0):
    if sample:
        (x_ref, z_ref, wf_ref, bf_ref, wb_ref, bb_ref, tf_ref, tb_ref, mf_ref, mb_ref, hm_ref, gn_ref, sf0_ref, sb0_ref,
         o_ref, la_f, la_b, o_f, o_b, s_f, s_b, qd_f, qd_b, ds_f, ds_b, cr_f, cr_b) = refs
    else:
        x_ref, z_ref, wf_ref, bf_ref, wb_ref, bb_ref, tf_ref, tb_ref, mf_ref, mb_ref, hm_ref, gn_ref = refs[:12]
        (o_ref, sf_out, sb_out, la_f, la_b, o_f, o_b, s_f, s_b,
         qd_f, qd_b, ds_f, ds_b, cr_f, cr_b) = refs[12 + (2 if has_prev else 0):]
    n_tok = x_ref.shape[0]
    n_chunks = n_tok // GLA_CHUNK
    hk, hv = H_A * DK_A, H_A * DV_A
    zb = z_ref[...].astype(BF16)

    def log_sigmoid(t):
        return jnp.minimum(t, 0.0) - jnp.log(1.0 + jnp.exp(-jnp.abs(t)))

    la_f[...] = log_sigmoid(_mm(zb, wf_ref[...].astype(BF16)) + bf_ref[...]) / GLA_TAU
    la_b[...] = log_sigmoid(_mm(zb, wb_ref[...].astype(BF16)) + bb_ref[...]) / GLA_TAU
    if sample:
        s_f[...] = sf0_ref[...]
        s_b[...] = sb0_ref[...]
    else:
        s_f[...] = jnp.zeros_like(s_f)
        s_b[...] = jnp.zeros_like(s_b)
    hm = hm_ref[...]

    fwd = (la_f, tf_ref, mf_ref, s_f, o_f, qd_f, ds_f, cr_f, False)
    bwd = (la_b, tb_ref, mb_ref, s_b, o_b, qd_b, ds_b, cr_b, True)
    tri_f = jnp.sum(mf_ref[...], axis=0)
    tri_b = jnp.sum(mb_ref[...], axis=0)

    def chunk_rows(ci, backward):
        cidx = n_chunks - 1 - ci if backward else ci
        return cidx, pl.ds(pl.multiple_of(cidx * GLA_CHUNK, GLA_CHUNK), GLA_CHUNK)

    def load_qkv(rows):
        q = x_ref[rows, 0:hk].astype(F32) * (DK_A ** -0.5)
        return q, x_ref[rows, hk:2 * hk].astype(F32), x_ref[rows, 2 * hk:2 * hk + hv]

    def safe_step(ci, carry):
        for la_ref, t_ref, m_ref, s_ref, out_ref, _, _, _, backward in (fwd, bwd):
            _, rows = chunk_rows(ci, backward)
            out_ref[rows, :] = _gla_chunk(*load_qkv(rows), la_ref[rows, :], t_ref, m_ref, hm, s_ref, backward)
        return carry

    def local_step(gi, carry):
        items, dests = [], []
        for u in range(GLA_GROUP):
            cidx = gi * GLA_GROUP + u
            rows = pl.ds(pl.multiple_of(cidx * GLA_CHUNK, GLA_CHUNK), GLA_CHUNK)
            qkv = load_qkv(rows)
            for (la_ref, t_ref, _, _, out_ref, qd_ref, ds_ref, cr_ref, backward), tri in ((fwd, tri_f), (bwd, tri_b)):
                items.append((*qkv, la_ref[rows, :], t_ref, tri, backward))
                dests.append((out_ref, rows, qd_ref, ds_ref, cr_ref, cidx))
        for (out_ref, rows, qd_ref, ds_ref, cr_ref, cidx), (intra, qd, incr, factor) in zip(dests, _gla_local(items, hm)):
            out_ref[rows, :] = jnp.zeros((GLA_CHUNK, hv), F32) if intra is None else intra
            qd_ref[cidx] = qd
            ds_ref[cidx] = incr
            cr_ref[cidx] = factor
        return carry

    def scan_step(ci, carry):
        for _, _, _, s_ref, out_ref, qd_ref, ds_ref, cr_ref, backward in (fwd, bwd):
            cidx, rows = chunk_rows(ci, backward)
            state = s_ref[...]
            inter = _mm(qd_ref[cidx], state.astype(BF16))
            out_ref[rows, :] += jnp.concatenate(
                [inter[GLA_CHUNK * h:GLA_CHUNK * (h + 1)] for h in range(H_A)], axis=1)
            s_ref[...] = state * cr_ref[cidx] + ds_ref[cidx]
        return carry

    chunk_sums = [jnp.sum(ref[...].reshape(n_chunks, GLA_CHUNK, hk), axis=1) for ref in (la_f, la_b)]
    mild = jnp.minimum(jnp.min(chunk_sums[0]), jnp.min(chunk_sums[1])) > -GLA_SAFE_DECAY

    @pl.when(mild)
    def _():
        lax.fori_loop(0, n_chunks // GLA_GROUP, local_step, 0, unroll=2)
        lax.fori_loop(0, n_chunks, scan_step, 0, unroll=2)

    @pl.when(jnp.logical_not(mild))
    def _():
        lax.fori_loop(0, n_chunks, safe_step, 0)
    if not sample:
        _store_slot(sf_out, slot, not has_prev, s_f[...])
        _store_slot(sb_out, slot, not has_prev, s_b[...])
    gain = gn_ref[...]
    for h in range(H_A):
        cols = slice(DV_A * h, DV_A * (h + 1))
        r = x_ref[:, 2 * hk + hv + DV_A * h:2 * hk + hv + DV_A * (h + 1)].astype(F32)
        o_ref[:, cols] = (_rms(o_f[:, cols] + o_b[:, cols]) * gain * (r * jax.nn.sigmoid(r))).astype(o_ref.dtype)


def _gla(proj, w_gf, b_gf, w_gb, b_gb, g_norm, consts, n_batch, seq, ctx=None, slot=0, prev=None):
    sample = ctx is not None
    hk, hv = H_A * DK_A, H_A * DV_A
    n_ch = seq // GLA_CHUNK
    full = lambda shape: pl.BlockSpec(shape, lambda b: (0,) * len(shape))
    in_specs = [
        pl.BlockSpec((seq, 2 * hk + 2 * hv), lambda b: (b, 0)),
        pl.BlockSpec((seq, LANES), lambda b: (b, EVEN_W // LANES - 1)),
        full((LANES, hk)), full((1, hk)), full((LANES, hk)), full((1, hk)),
        full(consts[0].shape), full(consts[1].shape), full(consts[2].shape), full(consts[3].shape), full(consts[4].shape),
        full((1, DV_A)),
    ]
    args = [proj, proj, w_gf, b_gf.reshape(1, hk), w_gb, b_gb.reshape(1, hk), *consts, g_norm.reshape(1, DV_A)]
    o_spec = pl.BlockSpec((seq, hv), lambda b: (b, 0))
    o_shape = jax.ShapeDtypeStruct((n_batch * seq, hv), BF16)
    aliases = {}
    if sample:
        st_spec = pl.BlockSpec((None, hk, DV_A), lambda b: (b, 0, 0))
        in_specs += [st_spec, st_spec]
        args += [ctx[0], ctx[1]]
        out_specs, out_shape = o_spec, o_shape
    else:
        if prev is not None:
            in_specs += [pl.BlockSpec(memory_space=pl.ANY)] * 2
            aliases = {len(args): 1, len(args) + 1: 2}
            args += list(prev)
        st_spec, st_shape = _stack_out(n_batch, (DEPTH + 1) // 2, (hk, DV_A), slot, prev is None)
        out_specs, out_shape = [o_spec, st_spec, st_spec], [o_shape, st_shape, st_shape]
    return pl.pallas_call(
        functools.partial(_gla_kernel, sample=sample, has_prev=prev is not None, slot=slot),
        grid=(n_batch,),
        in_specs=in_specs,
        out_specs=out_specs,
        out_shape=out_shape,
        input_output_aliases=aliases,
        scratch_shapes=[pltpu.VMEM((seq, hk), F32), pltpu.VMEM((seq, hk), F32),
                        pltpu.VMEM((seq, hv), F32), pltpu.VMEM((seq, hv), F32),
                        pltpu.VMEM((hk, DV_A), F32), pltpu.VMEM((hk, DV_A), F32),
                        pltpu.VMEM((n_ch, H_A * GLA_CHUNK, hk), BF16), pltpu.VMEM((n_ch, H_A * GLA_CHUNK, hk), BF16),
                        pltpu.VMEM((n_ch, hk, DV_A), F32), pltpu.VMEM((n_ch, hk, DV_A), F32),
                        pltpu.VMEM((n_ch, hk, DV_A), F32), pltpu.VMEM((n_ch, hk, DV_A), F32)],
        compiler_params=_params(1),
        name="gla_sample" if sample else "gla_prompt",
    )(*args)


def _axial_rope(n_tokens, dim):
    rows = n_tokens // GRID_W
    row = np.repeat(np.arange(rows), GRID_W).astype(np.float64)
    col = np.tile(np.arange(GRID_W), rows).astype(np.float64)
    n_freq = dim // 4
    inv = ROPE_THETA ** (-np.arange(n_freq) / n_freq)
    ang = np.concatenate([row[:, None] * inv, col[:, None] * inv], axis=-1)
    return np.cos(ang).astype(np.float32), np.sin(ang).astype(np.float32)


def _filter_features(n_tokens):
    t = np.linspace(0.0, 1.0, n_tokens)[:, None]
    w = 2.0 * np.pi * np.arange(n_tokens)[:, None] / n_tokens
    f = np.linspace(1e-4, FILT_BANDS - 1, FILT_BANDS)[None, :]
    z = np.concatenate([t, np.cos(f * w), -np.sin(f * w)], axis=-1)
    z = np.pad(z, ((0, 0), (0, LANES - FILT_EMB)))
    return jnp.asarray(z, F32), jnp.asarray(t, F32)


_QB_ZERO = H_D * (NOPE_D + ROPE_D)
_QB_PERM = np.array([(NOPE_D + ROPE_D) * (p // MLA_QW) + p % MLA_QW if p % MLA_QW < NOPE_D + ROPE_D else _QB_ZERO
                     for p in range(H_D * MLA_QW)])


def _mla_rope_tables(cos_d, sin_d):
    n, half = cos_d.shape
    zeros = np.zeros((n, half), np.float32)

    def lanes(pre, width):
        pad = np.zeros((n, width - pre.shape[1] - 2 * half), np.float32)
        build = lambda first, second, lead: np.concatenate([lead, first, second, pad], axis=1)
        return (build(cos_d, cos_d, pre), build(-sin_d, zeros, 0 * pre), build(zeros, sin_d, 0 * pre))

    q_tabs = lanes(np.ones((n, NOPE_D), np.float32), MLA_QW)
    k_tabs = lanes(np.zeros((n, 0), np.float32), LANES)
    return tuple(jnp.asarray(t) for t in q_tabs + k_tabs)

EVEN_ROW_GROUPS = ((0, 0, 1536), (1568, 1536, 1024), (1536, EVEN_W - 2 * GATE_RANK, 2 * GATE_RANK))
ODD_ROW_GROUPS = ((0, 0, 1984),)
EVEN_KEEP = (2304, 256)
ODD_KEEP = (1920, LANES)


def kernel(x_prompt, x_sample, state_gla_fwd, state_gla_bwd, cache_gqa_k, cache_gqa_v, cache_mla_ckv, cache_mla_kpe, c, c_ctx, w_mod, b_mod, w_in_even, w_gla_gate_f, b_gla_gate_f, w_gla_gate_b, b_gla_gate_b, g_gla_norm, g_gqa_q, g_gqa_k, w_out_even, w_in_odd, w_hy_conv, b_hy_conv, hy_skip, w_filt1, b_filt1, filt_freq, w_filt2, b_filt2, w_filt3, g_mla_q, w_mla_qb, g_mla_kv, w_mla_kvb, w_out_odd, w_ffn_in, w_ffn_out, g_final):
    cvec = jnp.concatenate([c_ctx[None, :], c, jnp.zeros((8 - 1 - DEC_BATCH, D_MODEL), F32)], axis=0)
    mod = _modulation(cvec, w_mod, b_mod)
    xc, xs, s_row0 = x_prompt.reshape(N_PROMPT, D_MODEL), x_sample.reshape(N_SAMPLE, D_MODEL), 0

    gla_consts = _gla_constants()
    cos_b, sin_b = _axial_rope(DEC_SEQ, HD_B)
    rope_b = (jnp.asarray(np.concatenate([cos_b, cos_b], axis=1)), jnp.asarray(np.concatenate([-sin_b, sin_b], axis=1)))
    cos_d, sin_d = _axial_rope(DEC_SEQ, ROPE_D)
    rope_d = _mla_rope_tables(cos_d, sin_d)
    w_qb_all = jnp.pad(w_mla_qb, ((0, 0), (0, 0), (0, 1)))[:, :, _QB_PERM]
    tabs_c, tabs_s = _dft_tables(SEQ), _dft_tables(DEC_SEQ)
    z_c, t_c = _filter_features(SEQ)
    z_s, t_s = _filter_features(DEC_SEQ)
    deltas = jnp.asarray(np.abs(np.linspace(HY_MIN_DECAY, HY_MAX_DECAY, HY_W))[None, :], F32)

    wt_even = jnp.swapaxes(w_in_even, 1, 2)
    wt_odd = jnp.swapaxes(w_in_odd, 1, 2)

    new_states = new_kv = new_latent = None
    for i in range(DEPTH):
        j = i // 2
        if i % 2 == 0:
            z0 = LANES - 2 * GATE_RANK
            pad_f = jnp.zeros((LANES, H_A * DK_A), F32).at[z0:z0 + GATE_RANK].set(w_gla_gate_f[j])
            pad_b = jnp.zeros((LANES, H_A * DK_A), F32).at[z0 + GATE_RANK:LANES].set(w_gla_gate_b[j])
            pc, v_new, ps = _in_proj(xc, xs, s_row0, mod, i, wt_even, j, EVEN_ROW_GROUPS, EVEN_W, EVEN_KEEP)
            gate_args = (pad_f, b_gla_gate_f[j], pad_b, b_gla_gate_b[j], g_gla_norm[j], gla_consts)
            a_c, *new_states = _gla(pc, *gate_args, BATCH, SEQ, slot=j, prev=new_states)
            ctx_a = (state_gla_fwd[:, j].reshape(DEC_BATCH, H_A * DK_A, DV_A),
                     state_gla_bwd[:, j].reshape(DEC_BATCH, H_A * DK_A, DV_A))
            a_s = _gla(ps, *gate_args, DEC_BATCH, DEC_SEQ, ctx=ctx_a)
            b_c, *new_kv = _gqa(pc, g_gqa_q[j], g_gqa_k[j], BATCH, SEQ, v_f32=v_new, slot=j, prev=new_kv)
            b_s = _gqa(ps, g_gqa_q[j], g_gqa_k[j], DEC_BATCH, DEC_SEQ, ctx=(cache_gqa_k, cache_gqa_v), rope=rope_b, slot=j)
            w_out = w_out_even
        else:
            pc, kpe_new, ps = _in_proj(xc, xs, s_row0, mod, i, wt_odd, j, ODD_ROW_GROUPS, ODD_W, ODD_KEEP)
            wf1 = jnp.pad(w_filt1[j], ((0, LANES - FILT_EMB), (0, 0)))
            filt_args = (wf1, b_filt1[j], filt_freq[j], w_filt2[j], b_filt2[j], w_filt3[j])
            g_c = _filter_spectrum(z_c, *filt_args, t_c, deltas, tabs_c)
            g_s = _filter_spectrum(z_s, *filt_args, t_s, deltas, tabs_s)
            b_conv = b_hy_conv[j].reshape(1, 3 * HY_W)
            a_c = _hyena(pc, w_hy_conv[j], b_conv, hy_skip[j], g_c[0], g_c[1], tabs_c, BATCH, SEQ)
            a_s = _hyena(ps, w_hy_conv[j], b_conv, hy_skip[j], g_s[0], g_s[1], tabs_s, DEC_BATCH, DEC_SEQ)
            w_qb = w_qb_all[j]
            b_c, *new_latent = _mla(pc, g_mla_q[j], w_qb, g_mla_kv[j], w_mla_kvb[j], BATCH, SEQ,
                                    kpe_f32=kpe_new, slot=j, prev=new_latent)
            b_s = _mla(ps, g_mla_q[j], w_qb, g_mla_kv[j], w_mla_kvb[j], DEC_BATCH, DEC_SEQ,
                       ctx=(cache_mla_ckv[:, j], cache_mla_kpe[:, j]), rope=rope_d)
            w_out = w_out_odd
        x_mid = _out_proj([a_c, b_c], [a_s, b_s], w_out, j, xc, xs, s_row0, mod, i, 2)
        if i < DEPTH - 1:
            x_all = _ffn(x_mid, 0, N_PROMPT + N_SAMPLE, mod, i, w_ffn_in, w_ffn_out)
            xc, xs, s_row0 = x_all, x_all, N_PROMPT
        else:
            xc = _ffn(x_mid, 0, N_PROMPT, mod, i, w_ffn_in, w_ffn_out, final_gain=g_final)
            xs = _ffn(x_mid, N_PROMPT, N_SAMPLE, mod, i, w_ffn_in, w_ffn_out, final_gain=g_final)
    y_prompt = xc.reshape(BATCH, SEQ, D_MODEL)
    y_sample = xs.reshape(DEC_BATCH, DEC_SEQ, D_MODEL)
    state_shape = (BATCH, (DEPTH + 1) // 2, H_A, DK_A, DV_A)
    return (y_prompt, y_sample, new_states[0].reshape(state_shape), new_states[1].reshape(state_shape),
            new_kv[0], new_kv[1], new_latent[0], new_latent[1])
```

```python
import functools
import math

import numpy as np
import jax
import jax.numpy as jnp
from jax import lax
from jax.experimental import pallas as pl
from jax.experimental.pallas import tpu as pltpu

F32 = jnp.float32
BF16 = jnp.bfloat16

D_MODEL = 1024
BATCH, SEQ = 16, 256
DEC_BATCH, DEC_SEQ = 2, 1024
DEPTH = 4
PAST_LEN = 512
GRID_W = 64
HALF_W = D_MODEL // 2
H_A, DV_A, DK_A = 4, 128, 64
GATE_RANK = 16
GLA_TAU = 16.0
GLA_CHUNK = 64
HD_B, H_B, KV_B = 128, 4, 2
HY_W = HALF_W
FILT_EMB, FILT_HID = 33, 64
FILT_BANDS = (FILT_EMB - 1) // 2
HY_MIN_DECAY = math.log(1e-2) / 1.5
HY_MAX_DECAY = math.log(1e-2) / 0.3
H_D, V_D, NOPE_D, ROPE_D = 4, 128, 128, 64
Q_RANK, KV_RANK = 256, 128
FFN_H = 2816
ROPE_THETA = 10000.0
EPS = 1e-6

LANES = 128
VMEM_LIMIT = 56 * 1024 * 1024

MOD_ROWS = 1024
TM = 1024
TM_IN = 512
TM_FFN = 2048
EVEN_W = 2688
ODD_W = 2048
FFN_TN = 256
QB = 256


def _params(n_grid):
    return pltpu.CompilerParams(dimension_semantics=("arbitrary",) * n_grid, vmem_limit_bytes=VMEM_LIMIT)


def _nt(a, b):
    return lax.dot_general(a, b, (((1,), (1,)), ((), ())), preferred_element_type=F32)


def _mm(a, b):
    return jnp.dot(a, b, preferred_element_type=F32)


def _rms(x):
    return x * lax.rsqrt(jnp.mean(x * x, axis=-1, keepdims=True) + EPS)


def _mod_kernel(c_ref, w_ref, b_ref, o_ref):
    cv = c_ref[...]
    s = cv * jax.nn.sigmoid(cv)
    o_ref[...] = _mm(s.astype(BF16), w_ref[...].astype(BF16)) + b_ref[...]


def _modulation(cvec, w_mod, b_mod):
    return pl.pallas_call(
        _mod_kernel,
        grid=(DEPTH, 6),
        in_specs=[
            pl.BlockSpec((8, D_MODEL), lambda l, n: (0, 0)),
            pl.BlockSpec((None, D_MODEL, D_MODEL), lambda l, n: (l, 0, n)),
            pl.BlockSpec((None, 1, D_MODEL), lambda l, n: (l, 0, n)),
        ],
        out_specs=pl.BlockSpec((None, None, 8, D_MODEL), lambda l, n: (l, n, 0, 0)),
        out_shape=jax.ShapeDtypeStruct((DEPTH, 6, 8, D_MODEL), F32),
        compiler_params=_params(2),
        name="adaln_mod",
    )(cvec, w_mod, b_mod.reshape(DEPTH, 1, 6 * D_MODEL))


N_PROMPT = BATCH * SEQ
N_SAMPLE = DEC_BATCH * DEC_SEQ


def _stack_out(n_batch, n_slots, tail, slot, first):
    zeros = (0,) * len(tail)
    if first:
        spec = pl.BlockSpec((None, n_slots) + tail, lambda b, *_: (b, 0) + zeros)
    else:
        spec = pl.BlockSpec((None, None) + tail, lambda b, *_: (b, slot) + zeros)
    return spec, jax.ShapeDtypeStruct((n_batch, n_slots) + tail, F32)


def _store_slot(ref, slot, owns_stack, value):
    if not owns_stack:
        ref[...] = value
        return
    for s in range(ref.shape[0]):
        ref[s] = value if s == slot else jnp.zeros_like(value)


def _stream_index_maps(tile_rows, s_row0):
    n_c = N_PROMPT // tile_rows
    return (lambda i: (jnp.minimum(i, n_c - 1), 0)), (lambda i: (s_row0 // tile_rows + jnp.maximum(i - n_c, 0), 0))


def _in_proj_kernel(xc_ref, xs_ref, sh_ref, sc_ref, wt_ref, oc_ref, keep_ref, os_ref, wb_ref, *, row_groups, keep, n_c):
    i = pl.program_id(0)

    @pl.when(i == 0)
    def _():
        wb_ref[...] = jnp.zeros_like(wb_ref)
        for src, dst, size in row_groups:
            wb_ref[dst:dst + size, :] = wt_ref[src:src + size, :].astype(BF16)

    def project(x_ref, g):
        h = (_rms(x_ref[...]) * (1.0 + sc_ref[pl.ds(g, 1), :]) + sh_ref[pl.ds(g, 1), :]).astype(BF16)
        return _nt(h, wb_ref[...])

    @pl.when(i < n_c)
    def _():
        y = project(xc_ref, 0)
        oc_ref[...] = y.astype(oc_ref.dtype)
        keep_ref[...] = y[:, keep[0]:keep[0] + keep[1]]

    @pl.when(i >= n_c)
    def _():
        os_ref[...] = project(xs_ref, 1 + (i - n_c) // (MOD_ROWS // TM_IN)).astype(os_ref.dtype)


def _in_proj(xc, xs, s_row0, mod, layer, wt, w_layer, row_groups, n, keep):
    n_c, n_s = N_PROMPT // TM_IN, N_SAMPLE // TM_IN
    xc_idx, xs_idx = _stream_index_maps(TM_IN, s_row0)
    c_idx, s_idx = _stream_index_maps(TM_IN, 0)
    return pl.pallas_call(
        functools.partial(_in_proj_kernel, row_groups=row_groups, keep=keep, n_c=n_c),
        grid=(n_c + n_s,),
        in_specs=[pl.BlockSpec((TM_IN, D_MODEL), xc_idx), pl.BlockSpec((TM_IN, D_MODEL), xs_idx),
                  pl.BlockSpec((None, None, 8, D_MODEL), lambda i: (layer, 0, 0, 0)),
                  pl.BlockSpec((None, None, 8, D_MODEL), lambda i: (layer, 1, 0, 0)),
                  pl.BlockSpec((None, wt.shape[1], D_MODEL), lambda i: (w_layer, 0, 0), pipeline_mode=pl.Buffered(1))],
        out_specs=[pl.BlockSpec((TM_IN, n), c_idx), pl.BlockSpec((TM_IN, keep[1]), c_idx), pl.BlockSpec((TM_IN, n), s_idx)],
        out_shape=[jax.ShapeDtypeStruct((N_PROMPT, n), BF16), jax.ShapeDtypeStruct((N_PROMPT, keep[1]), F32),
                   jax.ShapeDtypeStruct((N_SAMPLE, n), BF16)],
        scratch_shapes=[pltpu.VMEM((n, D_MODEL), BF16)],
        compiler_params=_params(1),
        name="norm_mod_proj",
    )(xc, xs, mod, mod, wt)


def _ffn_kernel(x_ref, sh_ref, sc_ref, gate_ref, wg_ref, wu_ref, wd_ref, *refs, first_sub, final):
    (gf_ref, o_ref, h_ref) = refs if final else (None,) + refs
    n_sub = x_ref.shape[0] // MOD_ROWS
    subs = [(slice(s * MOD_ROWS, (s + 1) * MOD_ROWS),
             jnp.maximum(first_sub + pl.program_id(0) * n_sub + s - (N_PROMPT // MOD_ROWS - 1), 0)) for s in range(n_sub)]

    @pl.when(pl.program_id(1) == 0)
    def _():
        for rows, g in subs:
            x = x_ref[rows, :]
            o_ref[rows, :] = x
            h_ref[rows, :] = (_rms(x) * (1.0 + sc_ref[pl.ds(g, 1), :]) + sh_ref[pl.ds(g, 1), :]).astype(BF16)

    wg = wg_ref[...].astype(BF16)
    wu = wu_ref[...].astype(BF16)
    wd = wd_ref[...].astype(BF16)
    for rows, g in subs:
        h = h_ref[rows, :]
        a = _mm(h, wg)
        act = (a * jax.nn.sigmoid(a) * _mm(h, wu)).astype(BF16)
        o_ref[rows, :] += gate_ref[pl.ds(g, 1), :] * _mm(act, wd)

    if final:
        @pl.when(pl.program_id(1) == pl.num_programs(1) - 1)
        def _():
            for rows, _ in subs:
                o_ref[rows, :] = _rms(o_ref[rows, :]) * gf_ref[...]


def _ffn(x, row0, m, mod, layer, w_in, w_out, final_gain=None):
    nj = FFN_H // FFN_TN
    tile0 = row0 // TM_FFN
    mod_spec = lambda k: pl.BlockSpec((None, None, 8, D_MODEL), lambda i, j: (layer, k, 0, 0))
    final = final_gain is not None
    extra_specs = [pl.BlockSpec((1, D_MODEL), lambda i, j: (0, 0))] if final else []
    extra_args = [final_gain.reshape(1, D_MODEL)] if final else []
    return pl.pallas_call(
        functools.partial(_ffn_kernel, first_sub=row0 // MOD_ROWS, final=final),
        grid=(m // TM_FFN, nj),
        in_specs=[pl.BlockSpec((TM_FFN, D_MODEL), lambda i, j: (tile0 + i, 0)), mod_spec(3), mod_spec(4), mod_spec(5),
                  pl.BlockSpec((None, D_MODEL, FFN_TN), lambda i, j: (layer, 0, j)),
                  pl.BlockSpec((None, D_MODEL, FFN_TN), lambda i, j: (layer, 0, j + nj)),
                  pl.BlockSpec((None, FFN_TN, D_MODEL), lambda i, j: (layer, j, 0))] + extra_specs,
        out_specs=pl.BlockSpec((TM_FFN, D_MODEL), lambda i, j: (i, 0)),
        out_shape=jax.ShapeDtypeStruct((m, D_MODEL), F32),
        scratch_shapes=[pltpu.VMEM((TM_FFN, D_MODEL), BF16)],
        compiler_params=_params(2),
        name="ffn_residual",
    )(x, mod, mod, mod, w_in, w_in, w_out, *extra_args)


def _proj_res_kernel(ac0_ref, ac1_ref, as0_ref, as1_ref, w0_ref, w1_ref, xc_ref, xs_ref, gate_ref, o_ref, *, n_c):
    i = pl.program_id(0)

    def mix(a0_ref, a1_ref, x_ref, g):
        acc = _mm(a0_ref[...], w0_ref[...].astype(BF16)) + _mm(a1_ref[...], w1_ref[...].astype(BF16))
        return x_ref[...] + gate_ref[pl.ds(g, 1), :] * acc

    @pl.when(i < n_c)
    def _():
        o_ref[...] = mix(ac0_ref, ac1_ref, xc_ref, 0)

    @pl.when(i >= n_c)
    def _():
        o_ref[...] = mix(as0_ref, as1_ref, xs_ref, 1 + (i - n_c))


def _out_proj(acts_c, acts_s, w, w_layer, xc, xs, s_row0, mod, layer, k_gate):
    n_c, n_s = N_PROMPT // TM, N_SAMPLE // TM
    kw = acts_c[0].shape[1]
    xc_idx, xs_idx = _stream_index_maps(TM, s_row0)
    c_idx, s_idx = _stream_index_maps(TM, 0)
    w_specs = [pl.BlockSpec((None, kw, D_MODEL), functools.partial(lambda i, p: (w_layer, p, 0), p=p),
                            pipeline_mode=pl.Buffered(1)) for p in range(2)]
    return pl.pallas_call(
        functools.partial(_proj_res_kernel, n_c=n_c),
        grid=(n_c + n_s,),
        in_specs=[pl.BlockSpec((TM, kw), c_idx)] * 2 + [pl.BlockSpec((TM, kw), s_idx)] * 2 + w_specs + [
            pl.BlockSpec((TM, D_MODEL), xc_idx), pl.BlockSpec((TM, D_MODEL), xs_idx),
            pl.BlockSpec((None, None, 8, D_MODEL), lambda i: (layer, k_gate, 0, 0)),
        ],
        out_specs=pl.BlockSpec((TM, D_MODEL), lambda i: (i, 0)),
        out_shape=jax.ShapeDtypeStruct((N_PROMPT + N_SAMPLE, D_MODEL), F32),
        compiler_params=_params(1),
        name="out_proj_residual",
    )(*acts_c, *acts_s, w, w, xc, xs, mod)


def _gqa_kernel(*refs, sample, has_prev=False, slot=0):
    if sample:
        q_ref, k_ref, v_ref, gq_ref, gk_ref, ck_ref, cv_ref, cos_ref, sin_ref, o_ref, kb_ref, vb_ref = refs
    else:
        n_in = 8 if has_prev else 6
        q_ref, k_ref, v_ref, gq_ref, gk_ref, vf_ref = refs[:6]
        o_ref, kc_ref, vc_ref, kb_ref, vb_ref = refs[n_in:]
        if not has_prev:
            stacks = (kc_ref, vc_ref)
            kc_ref, vc_ref = kc_ref.at[slot], vc_ref.at[slot]
    qi = pl.program_id(1)
    n_new = k_ref.shape[0]
    past = PAST_LEN if sample else 0
    rep = H_B // KV_B

    @pl.when(qi == 0)
    def _():
        if not (sample or has_prev):
            for ref in stacks:
                for other in range(ref.shape[0]):
                    if other != slot:
                        ref[other] = jnp.zeros(ref.shape[1:], ref.dtype)
        for g in range(KV_B):
            sl = slice(HD_B * g, HD_B * (g + 1))
            kn = _rms(k_ref[:, sl].astype(F32)) * gk_ref[...]
            if sample:
                kb_ref[0:past, sl] = ck_ref[:, g, :].astype(BF16)
                vb_ref[g, 0:past, 0:HD_B] = cv_ref[:, g, :].astype(BF16)
                kn = kn * cos_ref[...] + pltpu.roll(kn, HD_B // 2, 1) * sin_ref[...]
            else:
                kc_ref[:, g, :] = kn
                vc_ref[:, g, :] = vf_ref[:, sl]
            kb_ref[past:past + n_new, sl] = kn.astype(BF16)
            vb_ref[g, past:past + n_new, 0:HD_B] = v_ref[:, sl].astype(BF16)
            vb_ref[g, :, HD_B:] = jnp.ones((past + n_new, HD_B), BF16)

    r0 = pl.multiple_of(qi * QB, QB)
    qs = []
    for h in range(H_B):
        qn = _rms(q_ref[:, HD_B * h:HD_B * (h + 1)].astype(F32)) * gq_ref[...]
        if sample:
            qn = qn * cos_ref[pl.ds(r0, QB), :] + pltpu.roll(qn, HD_B // 2, 1) * sin_ref[pl.ds(r0, QB), :]
        qs.append((qn * (HD_B ** -0.5)).astype(BF16))
    scores = [_nt(qs[h], kb_ref[:, HD_B * (h // rep):HD_B * (h // rep + 1)]) for h in range(H_B)]
    weights = [jnp.exp(s - jnp.max(s, axis=-1, keepdims=True)).astype(BF16) for s in scores]
    sums = [_mm(weights[h], vb_ref[h // rep]) for h in range(H_B)]
    for h in range(H_B):
        o_ref[:, HD_B * h:HD_B * (h + 1)] = (sums[h][:, :HD_B] / sums[h][:, HD_B:]).astype(o_ref.dtype)


def _gqa(proj, g_q, g_k, n_batch, seq, ctx=None, rope=None, v_f32=None, slot=0, prev=None):
    sample = ctx is not None
    m = n_batch * seq
    nq = seq // QB
    n_even = (DEPTH + 1) // 2
    in_specs = [
        pl.BlockSpec((QB, 512), lambda b, i: (b * nq + i, 3)),
        pl.BlockSpec((seq, 256), lambda b, i: (b, 8)),
        pl.BlockSpec((seq, 256), lambda b, i: (b, 9)),
        pl.BlockSpec((1, HD_B), lambda b, i: (0, 0)),
        pl.BlockSpec((1, HD_B), lambda b, i: (0, 0)),
    ]
    args = [proj, proj, proj, g_q.reshape(1, HD_B), g_k.reshape(1, HD_B)]
    o_spec = pl.BlockSpec((QB, 512), lambda b, i: (b * nq + i, 0))
    o_shape = jax.ShapeDtypeStruct((m, 512), BF16)
    aliases = {}
    if sample:
        cache_spec = pl.BlockSpec((None, None, PAST_LEN, KV_B, HD_B), lambda b, i: (b, slot, 0, 0, 0))
        in_specs += [
            cache_spec, cache_spec,
            pl.BlockSpec((seq, HD_B), lambda b, i: (0, 0)),
            pl.BlockSpec((seq, HD_B), lambda b, i: (0, 0)),
        ]
        args += [ctx[0], ctx[1], rope[0], rope[1]]
        out_specs, out_shape = o_spec, o_shape
    else:
        in_specs.append(pl.BlockSpec((seq, KV_B * HD_B), lambda b, i: (b, 0)))
        args.append(v_f32)
        if prev is not None:
            in_specs += [pl.BlockSpec(memory_space=pl.ANY)] * 2
            aliases = {len(args): 1, len(args) + 1: 2}
            args += list(prev)
        if prev is None:
            new_spec = pl.BlockSpec((None, n_even, seq, KV_B, HD_B), lambda b, i: (b, 0, 0, 0, 0))
        else:
            new_spec = pl.BlockSpec((None, None, seq, KV_B, HD_B), lambda b, i: (b, slot, 0, 0, 0))
        new_shape = jax.ShapeDtypeStruct((n_batch, n_even, seq, KV_B, HD_B), F32)
        out_specs, out_shape = [o_spec, new_spec, new_spec], [o_shape, new_shape, new_shape]
    n_keys = seq + (PAST_LEN if sample else 0)
    return pl.pallas_call(
        functools.partial(_gqa_kernel, sample=sample, has_prev=prev is not None, slot=slot),
        grid=(n_batch, nq),
        in_specs=in_specs,
        out_specs=out_specs,
        out_shape=out_shape,
        input_output_aliases=aliases,
        scratch_shapes=[pltpu.VMEM((n_keys, KV_B * HD_B), BF16), pltpu.VMEM((KV_B, n_keys, 2 * HD_B), BF16)],
        compiler_params=_params(2),
        name="gqa_sample" if sample else "gqa_prompt",
    )(*args)


MLA_QW = 2 * LANES
MLA_HW = 4 * LANES


def _rotate_pairs(x, cos_t, sin_lo, sin_hi):
    w = x.shape[1]
    return x * cos_t + pltpu.roll(x, ROPE_D // 2, 1) * sin_hi + pltpu.roll(x, w - ROPE_D // 2, 1) * sin_lo


def _mla_kernel(*refs, sample, has_prev=False, slot=0):
    if sample:
        (cq_ref, ckv_ref, kpe_ref, gq_ref, wqb_ref, gkv_ref, wkvb_ref, cckv_ref, ckpe_ref,
         qc_ref, ql_ref, qh_ref, kc_ref, kl_ref, kh_ref, o_ref, kv_s) = refs
    else:
        cq_ref, ckv_ref, kpe_ref, gq_ref, wqb_ref, gkv_ref, wkvb_ref, kf_ref = refs[:8]
        o_ref, ckvn_ref, kpeo_ref, kv_s = refs[8 + (2 if has_prev else 0):]
    qi = pl.program_id(1)
    n_new = ckv_ref.shape[0]
    past = PAST_LEN if sample else 0

    def stage_kv(rows, kv, kpe_block):
        for h in range(H_D):
            kv_s[rows, MLA_HW * h:MLA_HW * h + NOPE_D] = kv[:, 256 * h:256 * h + NOPE_D].astype(BF16)
            kv_s[rows, MLA_HW * h + NOPE_D:MLA_HW * h + MLA_QW] = kpe_block
            kv_s[rows, MLA_HW * h + MLA_QW:MLA_HW * h + MLA_QW + V_D] = kv[:, 256 * h + NOPE_D:256 * (h + 1)].astype(BF16)

    @pl.when(qi == 0)
    def _():
        wkvb = wkvb_ref[...].astype(BF16)
        ckvn = _rms(ckv_ref[...].astype(F32)) * gkv_ref[...]
        if not sample:
            _store_slot(ckvn_ref, slot, not has_prev, ckvn)
            _store_slot(kpeo_ref, slot, not has_prev, kf_ref[:, 0:ROPE_D])
        kpe = kpe_ref[...]
        if sample:
            ctx_kpe = jnp.concatenate([ckpe_ref[...], jnp.zeros((past, LANES - ROPE_D), F32)], axis=1)
            stage_kv(slice(0, past), _mm(cckv_ref[...].astype(BF16), wkvb), ctx_kpe.astype(BF16))
            kpe = _rotate_pairs(kpe.astype(F32), kc_ref[...], kl_ref[...], kh_ref[...]).astype(BF16)
        stage_kv(slice(past, past + n_new), _mm(ckvn.astype(BF16), wkvb), kpe)
        for h in range(H_D):
            kv_s[:, MLA_HW * h + MLA_QW + V_D:MLA_HW * (h + 1)] = jnp.ones((past + n_new, V_D), BF16)

    q = _mm((_rms(cq_ref[...].astype(F32)) * gq_ref[...]).astype(BF16), wqb_ref[...].astype(BF16))
    q = q * ((NOPE_D + ROPE_D) ** -0.5)
    qs = []
    for h in range(H_D):
        q_h = q[:, MLA_QW * h:MLA_QW * (h + 1)]
        if sample:
            rows = pl.ds(pl.multiple_of(qi * QB, QB), QB)
            q_h = _rotate_pairs(q_h, qc_ref[rows, :], ql_ref[rows, :], qh_ref[rows, :])
        qs.append(q_h.astype(BF16))
    scores = [_nt(qs[h], kv_s[:, MLA_HW * h:MLA_HW * h + MLA_QW]) for h in range(H_D)]
    weights = [jnp.exp(s - jnp.max(s, axis=-1, keepdims=True)).astype(BF16) for s in scores]
    sums = [_mm(weights[h], kv_s[:, MLA_HW * h + MLA_QW:MLA_HW * (h + 1)]) for h in range(H_D)]
    for h in range(H_D):
        o_ref[:, V_D * h:V_D * (h + 1)] = (sums[h][:, :V_D] / sums[h][:, V_D:]).astype(o_ref.dtype)


def _mla(proj, g_q, w_qb, g_kv, w_kvb, n_batch, seq, ctx=None, rope=None, kpe_f32=None, slot=0, prev=None):
    sample = ctx is not None
    m = n_batch * seq
    nq = seq // QB
    in_specs = [
        pl.BlockSpec((QB, Q_RANK), lambda b, i: (b * nq + i, 6)),
        pl.BlockSpec((seq, KV_RANK), lambda b, i: (b, 14)),
        pl.BlockSpec((seq, LANES), lambda b, i: (b, 15)),
        pl.BlockSpec((1, Q_RANK), lambda b, i: (0, 0)),
        pl.BlockSpec((Q_RANK, H_D * MLA_QW), lambda b, i: (0, 0)),
        pl.BlockSpec((1, KV_RANK), lambda b, i: (0, 0)),
        pl.BlockSpec((KV_RANK, 1024), lambda b, i: (0, 0)),
    ]
    args = [proj, proj, proj, g_q.reshape(1, Q_RANK), w_qb, g_kv.reshape(1, KV_RANK), w_kvb]
    o_spec = pl.BlockSpec((QB, 512), lambda b, i: (b * nq + i, 0))
    o_shape = jax.ShapeDtypeStruct((m, 512), BF16)
    aliases = {}
    if sample:
        in_specs += [
            pl.BlockSpec((None, PAST_LEN, KV_RANK), lambda b, i: (b, 0, 0)),
            pl.BlockSpec((None, PAST_LEN, ROPE_D), lambda b, i: (b, 0, 0)),
        ] + [pl.BlockSpec((seq, MLA_QW), lambda b, i: (0, 0))] * 3 + [pl.BlockSpec((seq, LANES), lambda b, i: (0, 0))] * 3
        args += [ctx[0], ctx[1], *rope]
        out_specs, out_shape = o_spec, o_shape
    else:
        in_specs.append(pl.BlockSpec((seq, LANES), lambda b, i: (b, 0)))
        args.append(kpe_f32)
        if prev is not None:
            in_specs += [pl.BlockSpec(memory_space=pl.ANY)] * 2
            aliases = {len(args): 1, len(args) + 1: 2}
            args += list(prev)
        ckv_spec, ckv_shape = _stack_out(n_batch, DEPTH // 2, (seq, KV_RANK), slot, prev is None)
        kpe_spec, kpe_shape = _stack_out(n_batch, DEPTH // 2, (seq, ROPE_D), slot, prev is None)
        out_specs, out_shape = [o_spec, ckv_spec, kpe_spec], [o_shape, ckv_shape, kpe_shape]
    n_keys = seq + (PAST_LEN if sample else 0)
    return pl.pallas_call(
        functools.partial(_mla_kernel, sample=sample, has_prev=prev is not None, slot=slot),
        grid=(n_batch, nq),
        in_specs=in_specs,
        out_specs=out_specs,
        out_shape=out_shape,
        input_output_aliases=aliases,
        scratch_shapes=[pltpu.VMEM((n_keys, H_D * MLA_HW), BF16)],
        compiler_params=_params(2),
        name="mla_sample" if sample else "mla_prompt",
    )(*args)


def _dft(table, x):
    return _mm(table.astype(BF16), x.astype(BF16))


def _filter_kernel(z_ref, wf1_ref, bf1_ref, fr_ref, wf2_ref, bf2_ref, wf3_ref, t_ref, dl_ref,
                   c_ref, s_ref, gre_ref, gim_ref):
    n_tok = z_ref.shape[0]
    fr = fr_ref[...]
    hid = jnp.sin(fr * (_mm(z_ref[...].astype(BF16), wf1_ref[...].astype(BF16)) + bf1_ref[...]))
    hid = jnp.sin(fr * (_mm(hid.astype(BF16), wf2_ref[...].astype(BF16)) + bf2_ref[...]))
    filt = _mm(hid.astype(BF16), wf3_ref[...].astype(BF16))
    decay = jnp.exp(-t_ref[...] * dl_ref[...])
    row = lax.broadcasted_iota(jnp.int32, (n_tok, 1), 0)
    h_f = filt[:, :HY_W] * decay
    h_b = jnp.where(row == 0, 0.0, filt[:, HY_W:] * decay)
    p, m = h_f + h_b, h_f - h_b
    g_re = _dft(c_ref[...], p)
    g_im = _dft(s_ref[...], m)
    sign = jnp.where(row % 2 == 0, 1.0, -1.0)
    nyquist = jnp.sum(p * sign, axis=0, keepdims=True)
    g_im = jnp.where(row == 0, nyquist, g_im)
    wk = jnp.where(row == 0, 0.5 / n_tok, 1.0 / n_tok)
    gre_ref[...] = g_re * wk
    gim_ref[...] = g_im * wk


def _filter_spectrum(z, wf1, bf1, freq, wf2, bf2, wf3, t_col, deltas, tabs):
    n_layers, n_tok = wf1.shape[0], z.shape[0]
    shared = lambda a: pl.BlockSpec(a.shape, lambda l: (0,) * a.ndim)
    per_layer = lambda a: pl.BlockSpec((None,) + a.shape[1:], lambda l: (l,) + (0,) * (a.ndim - 1))
    row = lambda a: a.reshape(n_layers, 1, FILT_HID)
    args = [z, wf1, row(bf1), row(freq), wf2, row(bf2), wf3, t_col, deltas, tabs[0], tabs[1]]
    layered = [False, True, True, True, True, True, True, False, False, False, False]
    out = jax.ShapeDtypeStruct((n_layers, n_tok, HY_W), F32)
    out_spec = pl.BlockSpec((None, n_tok, HY_W), lambda l: (l, 0, 0))
    return pl.pallas_call(
        _filter_kernel,
        grid=(n_layers,),
        in_specs=[per_layer(a) if lay else shared(a) for a, lay in zip(args, layered)],
        out_specs=[out_spec, out_spec],
        out_shape=[out, out],
        compiler_params=_params(1),
        name="hyena_filter",
    )(*args)


HY_ROWS = 1024


def _hyena_channels(seq):
    return HY_W if seq <= 256 else HY_W // 2


def _hyena_kernel(u0_ref, u1_ref, u2_ref, w0_ref, w1_ref, w2_ref, b0_ref, b1_ref, b2_ref, skip_ref,
                  gre_ref, gim_ref, cf_ref, sf_ref, stf_ref, o_ref, c_ref, s_ref, st_ref):
    seq = c_ref.shape[0]
    n_rows = u0_ref.shape[0]
    n_seq = n_rows // seq
    pos = lax.broadcasted_iota(jnp.int32, (n_rows, 1), 0) % seq

    @pl.when((pl.program_id(0) == 0) & (pl.program_id(1) == 0))
    def _():
        c_ref[...] = cf_ref[...].astype(BF16)
        s_ref[...] = sf_ref[...].astype(BF16)
        st_ref[...] = stf_ref[...].astype(BF16)

    def short_conv(u_ref, w_ref, b_ref):
        x, w = u_ref[...].astype(F32), w_ref[...]
        prev = jnp.where(pos == 0, 0.0, pltpu.roll(x, 1, 0))
        nxt = jnp.where(pos == seq - 1, 0.0, pltpu.roll(x, n_rows - 1, 0))
        return prev * w[0:1] + x * w[1:2] + nxt * w[2:3] + b_ref[...]

    def side_by_side(a):
        return a if n_seq == 1 else jnp.concatenate([a[s * seq:(s + 1) * seq] for s in range(n_seq)], axis=1)

    def stacked(a):
        ct = a.shape[1] // n_seq
        return a if n_seq == 1 else jnp.concatenate([a[:, s * ct:(s + 1) * ct] for s in range(n_seq)], axis=0)

    x0 = short_conv(u0_ref, w0_ref, b0_ref)
    gv = short_conv(u1_ref, w1_ref, b1_ref) * short_conv(u2_ref, w2_ref, b2_ref)
    sig = side_by_side(gv).astype(BF16)
    u_re = _mm(c_ref[...], sig)
    u_im = _mm(s_ref[...], sig)
    g_re = jnp.concatenate([gre_ref[...]] * n_seq, axis=1)
    g_im = jnp.concatenate([gim_ref[...]] * n_seq, axis=1)
    bin0 = lax.broadcasted_iota(jnp.int32, (seq, 1), 0) == 0
    p_im = u_im * g_im
    y_re = u_re * g_re - jnp.where(bin0, 0.0, p_im)
    y_im = jnp.where(bin0, p_im, u_re * g_im + u_im * g_re)
    y = stacked(_mm(c_ref[...], y_re.astype(BF16)) + _mm(st_ref[...], y_im.astype(BF16)))
    o_ref[...] = (x0 * (y + gv * skip_ref[...])).astype(o_ref.dtype)


def _hyena(proj, layer, w_conv, b_conv, skip, g_re, g_im, tabs, n_batch, seq):
    ct = _hyena_channels(seq)
    nct = HY_W // ct
    u_specs = [pl.BlockSpec((HY_ROWS, ct), functools.partial(lambda b, c, g: (b, g * nct + c), g=g)) for g in range(3)]
    w_specs = [pl.BlockSpec((None, 3, ct), functools.partial(lambda b, c, g: (layer, 0, g * nct + c), g=g)) for g in range(3)]
    b_specs = [pl.BlockSpec((None, 1, ct), functools.partial(lambda b, c, g: (layer, 0, g * nct + c), g=g)) for g in range(3)]
    tab_spec = pl.BlockSpec((seq, seq), lambda b, c: (0, 0))
    return pl.pallas_call(
        _hyena_kernel,
        grid=(n_batch * seq // HY_ROWS, nct),
        in_specs=u_specs + w_specs + b_specs + [
            pl.BlockSpec((None, 1, ct), lambda b, c: (layer, 0, c)),
            pl.BlockSpec((None, seq, ct), lambda b, c: (layer, 0, c)),
            pl.BlockSpec((None, seq, ct), lambda b, c: (layer, 0, c)),
        ] + [tab_spec] * 3,
        out_specs=pl.BlockSpec((HY_ROWS, ct), lambda b, c: (b, c)),
        out_shape=jax.ShapeDtypeStruct((n_batch * seq, HY_W), BF16),
        scratch_shapes=[pltpu.VMEM((seq, seq), BF16)] * 3,
        compiler_params=_params(2),
        name="hyena_conv",
    )(proj, proj, proj, w_conv, w_conv, w_conv, b_conv, b_conv, b_conv, skip, g_re, g_im, *tabs)


def _dft_tables(n_tok):
    k = np.arange(n_tok)[:, None]
    s = np.arange(n_tok)[None, :]
    ang = ((k * s) % (2 * n_tok)) * (np.pi / n_tok)
    cos_t = np.cos(ang)
    sin_f = np.where(k == 0, np.where(s % 2 == 0, 1.0, -1.0), -np.sin(ang))
    return [jnp.asarray(t, F32) for t in (cos_t, sin_f, sin_f.T)]


GLA_LEVELS = (32, 16, 8, 4, 2, 1)
GLA_SAFE_DECAY = 60.0
GLA_GROUP = 2


def _gla_constants():
    c = GLA_CHUNK
    idx = np.arange(c)
    i, t = idx[:, None], idx[None, :]
    masks = []
    for s in GLA_LEVELS:
        upper = (idx % (2 * s)) >= s
        masks.append(((i // (2 * s)) == (t // (2 * s))) & upper[:, None] & (~upper)[None, :])
    masks.append(i == t)
    tri = t <= i
    fwd_m = np.stack([np.tile(m, (H_A, 1)) for m in masks]).astype(np.float32)
    bwd_m = np.stack([np.tile(m[::-1, ::-1], (H_A, 1)) for m in masks]).astype(np.float32)
    head_of_row = np.repeat(np.arange(H_A), c)[:, None]
    head_of_lane = np.repeat(np.arange(H_A), DK_A)[None, :]
    head_mask = head_of_row == head_of_lane
    return (jnp.asarray(tri, BF16), jnp.asarray(tri[::-1, ::-1], BF16), jnp.asarray(fwd_m), jnp.asarray(bwd_m),
            jnp.asarray(head_mask, BF16))


def _pair_reference(b, s, backward, row):
    c = GLA_CHUNK
    ref = s if backward else s - 1
    if 2 * s >= 8:
        pieces = [jnp.broadcast_to(b[p * 2 * s + ref:p * 2 * s + ref + 1, :], (2 * s, b.shape[1]))
                  for p in range(c // (2 * s))]
        return pieces[0] if len(pieces) == 1 else jnp.concatenate(pieces, axis=0)
    pos = row % (2 * s)
    out = None
    for o in range(2 * s):
        d = ref - o
        shifted = b if d == 0 else pltpu.roll(b, (-d) % c, 0)
        out = shifted if out is None else jnp.where(pos == o, shifted, out)
    return out


def _chunk_log_decay(la, t_ref):
    l1 = la.astype(BF16)
    r1 = la - l1.astype(F32)
    l2 = r1.astype(BF16)
    l3 = (r1 - l2.astype(F32)).astype(BF16)
    tmat = t_ref[...]
    return _mm(tmat, l1) + _mm(tmat, l2) + _mm(tmat, l3)


def _stack_heads(a, hm):
    ab = a.astype(BF16)
    return jnp.concatenate([ab] * H_A, axis=0) * hm


def _state_terms(k, v, b, b_last):
    c = GLA_CHUNK
    k_rest = (k * jnp.exp(b_last - b)).T
    carry = jnp.broadcast_to(jnp.exp(b_last), (2 * c, b.shape[1])).T
    return k_rest.astype(BF16), carry


def _gla_chunk(q, k, v, la, t_ref, m_ref, hm, s_ref, backward):
    c = GLA_CHUNK
    b = _chunk_log_decay(la, t_ref)
    row = lax.broadcasted_iota(jnp.int32, (c, 1), 0)
    last = 0 if backward else c - 1
    b_last = b[last:last + 1, :]
    scores = _nt(_stack_heads(q, hm), k.astype(BF16)) * m_ref[len(GLA_LEVELS)]
    for lvl, s in enumerate(GLA_LEVELS):
        is_query = (row % (2 * s) < s) if backward else (row % (2 * s) >= s)
        delta = b - _pair_reference(b, s, backward, row)
        x = jnp.exp(jnp.where(is_query, delta, -delta))
        scores = scores + _nt(_stack_heads(q * x, hm), (k * x).astype(BF16)) * m_ref[lvl]
    scores = scores.astype(BF16)
    state = s_ref[...]
    inter = _mm(_stack_heads(q * jnp.exp(b), hm), state.astype(BF16))
    k_rest, carry = _state_terms(k, v, b, b_last)
    outs = []
    for h in range(H_A):
        rows = slice(c * h, c * (h + 1))
        v_h = v[:, DV_A * h:DV_A * (h + 1)]
        outs.append(_mm(scores[rows], v_h) + inter[rows])
        s_ref[rows, :] = state[rows] * carry[rows] + _mm(k_rest[rows], v_h)
    return jnp.concatenate(outs, axis=1)


def _gla_local(items, hm):
    c = GLA_CHUNK
    bs = [_chunk_log_decay(la, t_ref) for _, _, _, la, t_ref, _, _ in items]
    b_lasts = [b[(0 if it[6] else c - 1):(0 if it[6] else c - 1) + 1, :] for b, it in zip(bs, items)]
    q_decayed = [_stack_heads(it[0] * jnp.exp(b), hm) for it, b in zip(items, bs)]
    k_grown = [(it[1] * jnp.exp(-b)).astype(BF16) for it, b in zip(items, bs)]
    raw = [_nt(qd, kg) for qd, kg in zip(q_decayed, k_grown)]
    masked = [r * it[5] for r, it in zip(raw, items)]
    terms = [_state_terms(it[1], it[2], b, bl) for it, b, bl in zip(items, bs, b_lasts)]
    out = []
    for n in range(0, len(items), 2):
        v = items[n][2]
        heads = [(slice(c * h, c * (h + 1)), v[:, DV_A * h:DV_A * (h + 1)]) for h in range(H_A)]
        both = (masked[n] + masked[n + 1]).astype(BF16)
        intra = jnp.concatenate([_mm(both[rows], v_h) for rows, v_h in heads], axis=1)
        for m in (n, n + 1):
            k_rest, carry = terms[m]
            incr = jnp.concatenate([_mm(k_rest[rows], v_h) for rows, v_h in heads], axis=0)
            out.append((intra if m == n else None, q_decayed[m], incr, carry))
    return out


def _gla_kernel(*refs, sample, has_prev=False, slot=0):
    if sample:
        (x_ref, z_ref, wf_ref, bf_ref, wb_ref, bb_ref, tf_ref, tb_ref, mf_ref, mb_ref, hm_ref, gn_ref, sf0_ref, sb0_ref,
         o_ref, la_f, la_b, o_f, o_b, s_f, s_b, qd_f, qd_b, ds_f, ds_b, cr_f, cr_b) = refs
    else:
        x_ref, z_ref, wf_ref, bf_ref, wb_ref, bb_ref, tf_ref, tb_ref, mf_ref, mb_ref, hm_ref, gn_ref = refs[:12]
        (o_ref, sf_out, sb_out, la_f, la_b, o_f, o_b, s_f, s_b,
         qd_f, qd_b, ds_f, ds_b, cr_f, cr_b) = refs[12 + (2 if has_prev else 0):]
    n_tok = x_ref.shape[0]
    n_chunks = n_tok // GLA_CHUNK
    hk, hv = H_A * DK_A, H_A * DV_A
    zb = z_ref[...].astype(BF16)

    def log_sigmoid(t):
        return jnp.minimum(t, 0.0) - jnp.log(1.0 + jnp.exp(-jnp.abs(t)))

    la_f[...] = log_sigmoid(_mm(zb, wf_ref[...].astype(BF16)) + bf_ref[...]) / GLA_TAU
    la_b[...] = log_sigmoid(_mm(zb, wb_ref[...].astype(BF16)) + bb_ref[...]) / GLA_TAU
    if sample:
        s_f[...] = sf0_ref[...]
        s_b[...] = sb0_ref[...]
    else:
        s_f[...] = jnp.zeros_like(s_f)
        s_b[...] = jnp.zeros_like(s_b)
    hm = hm_ref[...]

    fwd = (la_f, tf_ref, mf_ref, s_f, o_f, qd_f, ds_f, cr_f, False)
    bwd = (la_b, tb_ref, mb_ref, s_b, o_b, qd_b, ds_b, cr_b, True)
    tri_f = jnp.sum(mf_ref[...], axis=0)
    tri_b = jnp.sum(mb_ref[...], axis=0)

    def chunk_rows(ci, backward):
        cidx = n_chunks - 1 - ci if backward else ci
        return cidx, pl.ds(pl.multiple_of(cidx * GLA_CHUNK, GLA_CHUNK), GLA_CHUNK)

    def load_qkv(rows):
        q = x_ref[rows, 0:hk].astype(F32) * (DK_A ** -0.5)
        return q, x_ref[rows, hk:2 * hk].astype(F32), x_ref[rows, 2 * hk:2 * hk + hv]

    def safe_step(ci, carry):
        for la_ref, t_ref, m_ref, s_ref, out_ref, _, _, _, backward in (fwd, bwd):
            _, rows = chunk_rows(ci, backward)
            out_ref[rows, :] = _gla_chunk(*load_qkv(rows), la_ref[rows, :], t_ref, m_ref, hm, s_ref, backward)
        return carry

    def local_step(gi, carry):
        items, dests = [], []
        for u in range(GLA_GROUP):
            cidx = gi * GLA_GROUP + u
            rows = pl.ds(pl.multiple_of(cidx * GLA_CHUNK, GLA_CHUNK), GLA_CHUNK)
            qkv = load_qkv(rows)
            for (la_ref, t_ref, _, _, out_ref, qd_ref, ds_ref, cr_ref, backward), tri in ((fwd, tri_f), (bwd, tri_b)):
                items.append((*qkv, la_ref[rows, :], t_ref, tri, backward))
                dests.append((out_ref, rows, qd_ref, ds_ref, cr_ref, cidx))
        for (out_ref, rows, qd_ref, ds_ref, cr_ref, cidx), (intra, qd, incr, factor) in zip(dests, _gla_local(items, hm)):
            out_ref[rows, :] = jnp.zeros((GLA_CHUNK, hv), F32) if intra is None else intra
            qd_ref[cidx] = qd
            ds_ref[cidx] = incr
            cr_ref[cidx] = factor
        return carry

    def scan_step(ci, carry):
        for _, _, _, s_ref, out_ref, qd_ref, ds_ref, cr_ref, backward in (fwd, bwd):
            cidx, rows = chunk_rows(ci, backward)
            state = s_ref[...]
            inter = _mm(qd_ref[cidx], state.astype(BF16))
            out_ref[rows, :] += jnp.concatenate(
                [inter[GLA_CHUNK * h:GLA_CHUNK * (h + 1)] for h in range(H_A)], axis=1)
            s_ref[...] = state * cr_ref[cidx] + ds_ref[cidx]
        return carry

    chunk_sums = [jnp.sum(ref[...].reshape(n_chunks, GLA_CHUNK, hk), axis=1) for ref in (la_f, la_b)]
    mild = jnp.minimum(jnp.min(chunk_sums[0]), jnp.min(chunk_sums[1])) > -GLA_SAFE_DECAY

    @pl.when(mild)
    def _():
        lax.fori_loop(0, n_chunks // GLA_GROUP, local_step, 0, unroll=2)
        lax.fori_loop(0, n_chunks, scan_step, 0, unroll=2)

    @pl.when(jnp.logical_not(mild))
    def _():
        lax.fori_loop(0, n_chunks, safe_step, 0)
    if not sample:
        _store_slot(sf_out, slot, not has_prev, s_f[...])
        _store_slot(sb_out, slot, not has_prev, s_b[...])
    gain = gn_ref[...]
    for h in range(H_A):
        cols = slice(DV_A * h, DV_A * (h + 1))
        r = x_ref[:, 2 * hk + hv + DV_A * h:2 * hk + hv + DV_A * (h + 1)].astype(F32)
        o_ref[:, cols] = (_rms(o_f[:, cols] + o_b[:, cols]) * gain * (r * jax.nn.sigmoid(r))).astype(o_ref.dtype)


def _gla(proj, w_gf, b_gf, w_gb, b_gb, g_norm, consts, n_batch, seq, ctx=None, slot=0, prev=None):
    sample = ctx is not None
    hk, hv = H_A * DK_A, H_A * DV_A
    n_ch = seq // GLA_CHUNK
    full = lambda shape: pl.BlockSpec(shape, lambda b: (0,) * len(shape))
    in_specs = [
        pl.BlockSpec((seq, 2 * hk + 2 * hv), lambda b: (b, 0)),
        pl.BlockSpec((seq, LANES), lambda b: (b, EVEN_W // LANES - 1)),
        full((LANES, hk)), full((1, hk)), full((LANES, hk)), full((1, hk)),
        full(consts[0].shape), full(consts[1].shape), full(consts[2].shape), full(consts[3].shape), full(consts[4].shape),
        full((1, DV_A)),
    ]
    args = [proj, proj, w_gf, b_gf.reshape(1, hk), w_gb, b_gb.reshape(1, hk), *consts, g_norm.reshape(1, DV_A)]
    o_spec = pl.BlockSpec((seq, hv), lambda b: (b, 0))
    o_shape = jax.ShapeDtypeStruct((n_batch * seq, hv), BF16)
    aliases = {}
    if sample:
        st_spec = pl.BlockSpec((None, hk, DV_A), lambda b: (b, 0, 0))
        in_specs += [st_spec, st_spec]
        args += [ctx[0], ctx[1]]
        out_specs, out_shape = o_spec, o_shape
    else:
        if prev is not None:
            in_specs += [pl.BlockSpec(memory_space=pl.ANY)] * 2
            aliases = {len(args): 1, len(args) + 1: 2}
            args += list(prev)
        st_spec, st_shape = _stack_out(n_batch, (DEPTH + 1) // 2, (hk, DV_A), slot, prev is None)
        out_specs, out_shape = [o_spec, st_spec, st_spec], [o_shape, st_shape, st_shape]
    return pl.pallas_call(
        functools.partial(_gla_kernel, sample=sample, has_prev=prev is not None, slot=slot),
        grid=(n_batch,),
        in_specs=in_specs,
        out_specs=out_specs,
        out_shape=out_shape,
        input_output_aliases=aliases,
        scratch_shapes=[pltpu.VMEM((seq, hk), F32), pltpu.VMEM((seq, hk), F32),
                        pltpu.VMEM((seq, hv), F32), pltpu.VMEM((seq, hv), F32),
                        pltpu.VMEM((hk, DV_A), F32), pltpu.VMEM((hk, DV_A), F32),
                        pltpu.VMEM((n_ch, H_A * GLA_CHUNK, hk), BF16), pltpu.VMEM((n_ch, H_A * GLA_CHUNK, hk), BF16),
                        pltpu.VMEM((n_ch, hk, DV_A), F32), pltpu.VMEM((n_ch, hk, DV_A), F32),
                        pltpu.VMEM((n_ch, hk, DV_A), F32), pltpu.VMEM((n_ch, hk, DV_A), F32)],
        compiler_params=_params(1),
        name="gla_sample" if sample else "gla_prompt",
    )(*args)


def _axial_rope(n_tokens, dim):
    rows = n_tokens // GRID_W
    row = np.repeat(np.arange(rows), GRID_W).astype(np.float64)
    col = np.tile(np.arange(GRID_W), rows).astype(np.float64)
    n_freq = dim // 4
    inv = ROPE_THETA ** (-np.arange(n_freq) / n_freq)
    ang = np.concatenate([row[:, None] * inv, col[:, None] * inv], axis=-1)
    return np.cos(ang).astype(np.float32), np.sin(ang).astype(np.float32)


def _filter_features(n_tokens):
    t = np.linspace(0.0, 1.0, n_tokens)[:, None]
    w = 2.0 * np.pi * np.arange(n_tokens)[:, None] / n_tokens
    f = np.linspace(1e-4, FILT_BANDS - 1, FILT_BANDS)[None, :]
    z = np.concatenate([t, np.cos(f * w), -np.sin(f * w)], axis=-1)
    z = np.pad(z, ((0, 0), (0, LANES - FILT_EMB)))
    return jnp.asarray(z, F32), jnp.asarray(t, F32)


_QB_ZERO = H_D * (NOPE_D + ROPE_D)
_QB_PERM = np.array([(NOPE_D + ROPE_D) * (p // MLA_QW) + p % MLA_QW if p % MLA_QW < NOPE_D + ROPE_D else _QB_ZERO
                     for p in range(H_D * MLA_QW)])


def _mla_rope_tables(cos_d, sin_d):
    n, half = cos_d.shape
    zeros = np.zeros((n, half), np.float32)

    def lanes(pre, width):
        pad = np.zeros((n, width - pre.shape[1] - 2 * half), np.float32)
        build = lambda first, second, lead: np.concatenate([lead, first, second, pad], axis=1)
        return (build(cos_d, cos_d, pre), build(-sin_d, zeros, 0 * pre), build(zeros, sin_d, 0 * pre))

    q_tabs = lanes(np.ones((n, NOPE_D), np.float32), MLA_QW)
    k_tabs = lanes(np.zeros((n, 0), np.float32), LANES)
    return tuple(jnp.asarray(t) for t in q_tabs + k_tabs)

EVEN_ROW_GROUPS = ((0, 0, 1536), (1568, 1536, 1024), (1536, EVEN_W - 2 * GATE_RANK, 2 * GATE_RANK))
ODD_ROW_GROUPS = ((0, 0, 1984),)
EVEN_KEEP = (2304, 256)
ODD_KEEP = (1920, LANES)


def kernel(x_prompt, x_sample, state_gla_fwd, state_gla_bwd, cache_gqa_k, cache_gqa_v, cache_mla_ckv, cache_mla_kpe, c, c_ctx, w_mod, b_mod, w_in_even, w_gla_gate_f, b_gla_gate_f, w_gla_gate_b, b_gla_gate_b, g_gla_norm, g_gqa_q, g_gqa_k, w_out_even, w_in_odd, w_hy_conv, b_hy_conv, hy_skip, w_filt1, b_filt1, filt_freq, w_filt2, b_filt2, w_filt3, g_mla_q, w_mla_qb, g_mla_kv, w_mla_kvb, w_out_odd, w_ffn_in, w_ffn_out, g_final):
    cvec = jnp.concatenate([c_ctx[None, :], c, jnp.zeros((8 - 1 - DEC_BATCH, D_MODEL), F32)], axis=0)
    mod = _modulation(cvec, w_mod, b_mod)
    xc, xs, s_row0 = x_prompt.reshape(N_PROMPT, D_MODEL), x_sample.reshape(N_SAMPLE, D_MODEL), 0

    gla_consts = _gla_constants()
    cos_b, sin_b = _axial_rope(DEC_SEQ, HD_B)
    rope_b = (jnp.asarray(np.concatenate([cos_b, cos_b], axis=1)), jnp.asarray(np.concatenate([-sin_b, sin_b], axis=1)))
    cos_d, sin_d = _axial_rope(DEC_SEQ, ROPE_D)
    rope_d = _mla_rope_tables(cos_d, sin_d)
    w_qb_all = jnp.pad(w_mla_qb, ((0, 0), (0, 0), (0, 1)))[:, :, _QB_PERM]
    tabs_c, tabs_s = _dft_tables(SEQ), _dft_tables(DEC_SEQ)
    z_c, t_c = _filter_features(SEQ)
    z_s, t_s = _filter_features(DEC_SEQ)
    deltas = jnp.asarray(np.abs(np.linspace(HY_MIN_DECAY, HY_MAX_DECAY, HY_W))[None, :], F32)

    wt_even = jnp.swapaxes(w_in_even, 1, 2)
    wt_odd = jnp.swapaxes(w_in_odd, 1, 2)

    filt_args = (jnp.pad(w_filt1, ((0, 0), (0, LANES - FILT_EMB), (0, 0))), b_filt1, filt_freq, w_filt2, b_filt2, w_filt3)
    g_c = _filter_spectrum(z_c, *filt_args, t_c, deltas, tabs_c)
    g_s = _filter_spectrum(z_s, *filt_args, t_s, deltas, tabs_s)
    b_conv = b_hy_conv.reshape(DEPTH // 2, 1, 3 * HY_W)
    skip = hy_skip.reshape(DEPTH // 2, 1, HY_W)

    new_states = new_kv = new_latent = None
    for i in range(DEPTH):
        j = i // 2
        if i % 2 == 0:
            z0 = LANES - 2 * GATE_RANK
            pad_f = jnp.zeros((LANES, H_A * DK_A), F32).at[z0:z0 + GATE_RANK].set(w_gla_gate_f[j])
            pad_b = jnp.zeros((LANES, H_A * DK_A), F32).at[z0 + GATE_RANK:LANES].set(w_gla_gate_b[j])
            pc, v_new, ps = _in_proj(xc, xs, s_row0, mod, i, wt_even, j, EVEN_ROW_GROUPS, EVEN_W, EVEN_KEEP)
            gate_args = (pad_f, b_gla_gate_f[j], pad_b, b_gla_gate_b[j], g_gla_norm[j], gla_consts)
            a_c, *new_states = _gla(pc, *gate_args, BATCH, SEQ, slot=j, prev=new_states)
            ctx_a = (state_gla_fwd[:, j].reshape(DEC_BATCH, H_A * DK_A, DV_A),
                     state_gla_bwd[:, j].reshape(DEC_BATCH, H_A * DK_A, DV_A))
            a_s = _gla(ps, *gate_args, DEC_BATCH, DEC_SEQ, ctx=ctx_a)
            b_c, *new_kv = _gqa(pc, g_gqa_q[j], g_gqa_k[j], BATCH, SEQ, v_f32=v_new, slot=j, prev=new_kv)
            b_s = _gqa(ps, g_gqa_q[j], g_gqa_k[j], DEC_BATCH, DEC_SEQ, ctx=(cache_gqa_k, cache_gqa_v), rope=rope_b, slot=j)
            w_out = w_out_even
        else:
            pc, kpe_new, ps = _in_proj(xc, xs, s_row0, mod, i, wt_odd, j, ODD_ROW_GROUPS, ODD_W, ODD_KEEP)
            a_c = _hyena(pc, j, w_hy_conv, b_conv, skip, g_c[0], g_c[1], tabs_c, BATCH, SEQ)
            a_s = _hyena(ps, j, w_hy_conv, b_conv, skip, g_s[0], g_s[1], tabs_s, DEC_BATCH, DEC_SEQ)
            w_qb = w_qb_all[j]
            b_c, *new_latent = _mla(pc, g_mla_q[j], w_qb, g_mla_kv[j], w_mla_kvb[j], BATCH, SEQ,
                                    kpe_f32=kpe_new, slot=j, prev=new_latent)
            b_s = _mla(ps, g_mla_q[j], w_qb, g_mla_kv[j], w_mla_kvb[j], DEC_BATCH, DEC_SEQ,
                       ctx=(cache_mla_ckv[:, j], cache_mla_kpe[:, j]), rope=rope_d)
            w_out = w_out_odd
        x_mid = _out_proj([a_c, b_c], [a_s, b_s], w_out, j, xc, xs, s_row0, mod, i, 2)
        if i < DEPTH - 1:
            x_all = _ffn(x_mid, 0, N_PROMPT + N_SAMPLE, mod, i, w_ffn_in, w_ffn_out)
            xc, xs, s_row0 = x_all, x_all, N_PROMPT
        else:
            xc = _ffn(x_mid, 0, N_PROMPT, mod, i, w_ffn_in, w_ffn_out, final_gain=g_final)
            xs = _ffn(x_mid, N_PROMPT, N_SAMPLE, mod, i, w_ffn_in, w_ffn_out, final_gain=g_final)
    y_prompt = xc.reshape(BATCH, SEQ, D_MODEL)
    y_sample = xs.reshape(DEC_BATCH, DEC_SEQ, D_MODEL)
    state_shape = (BATCH, (DEPTH + 1) // 2, H_A, DK_A, DV_A)
    return (y_prompt, y_sample, new_states[0].reshape(state_shape), new_states[1].reshape(state_shape),
            new_kv[0], new_kv[1], new_latent[0], new_latent[1])
```

```python
import functools
import math

import numpy as np
import jax
import jax.numpy as jnp
from jax import lax
from jax.experimental import pallas as pl
from jax.experimental.pallas import tpu as pltpu

F32 = jnp.float32
BF16 = jnp.bfloat16

D_MODEL = 1024
BATCH, SEQ = 16, 256
DEC_BATCH, DEC_SEQ = 2, 1024
DEPTH = 4
PAST_LEN = 512
GRID_W = 64
HALF_W = D_MODEL // 2
H_A, DV_A, DK_A = 4, 128, 64
GATE_RANK = 16
GLA_TAU = 16.0
GLA_CHUNK = 64
HD_B, H_B, KV_B = 128, 4, 2
HY_W = HALF_W
FILT_EMB, FILT_HID = 33, 64
FILT_BANDS = (FILT_EMB - 1) // 2
HY_MIN_DECAY = math.log(1e-2) / 1.5
HY_MAX_DECAY = math.log(1e-2) / 0.3
H_D, V_D, NOPE_D, ROPE_D = 4, 128, 128, 64
Q_RANK, KV_RANK = 256, 128
FFN_H = 2816
ROPE_THETA = 10000.0
EPS = 1e-6

LANES = 128
VMEM_LIMIT = 56 * 1024 * 1024

MOD_ROWS = 1024
TM = 1024
TM_IN = 512
TM_FFN = 2048
EVEN_W = 2688
ODD_W = 2048
FFN_TN = 256
QB = 256


def _params(n_grid):
    return pltpu.CompilerParams(dimension_semantics=("arbitrary",) * n_grid, vmem_limit_bytes=VMEM_LIMIT)


def _nt(a, b):
    return lax.dot_general(a, b, (((1,), (1,)), ((), ())), preferred_element_type=F32)


def _mm(a, b):
    return jnp.dot(a, b, preferred_element_type=F32)


def _rms(x):
    return x * lax.rsqrt(jnp.mean(x * x, axis=-1, keepdims=True) + EPS)


def _mod_kernel(c_ref, w_ref, b_ref, o_ref):
    cv = c_ref[...]
    s = cv * jax.nn.sigmoid(cv)
    o_ref[...] = _mm(s.astype(BF16), w_ref[...].astype(BF16)) + b_ref[...]


def _modulation(cvec, w_mod, b_mod):
    return pl.pallas_call(
        _mod_kernel,
        grid=(DEPTH, 6),
        in_specs=[
            pl.BlockSpec((8, D_MODEL), lambda l, n: (0, 0)),
            pl.BlockSpec((None, D_MODEL, D_MODEL), lambda l, n: (l, 0, n)),
            pl.BlockSpec((None, 1, D_MODEL), lambda l, n: (l, 0, n)),
        ],
        out_specs=pl.BlockSpec((None, None, 8, D_MODEL), lambda l, n: (l, n, 0, 0)),
        out_shape=jax.ShapeDtypeStruct((DEPTH, 6, 8, D_MODEL), F32),
        compiler_params=_params(2),
        name="adaln_mod",
    )(cvec, w_mod, b_mod.reshape(DEPTH, 1, 6 * D_MODEL))


N_PROMPT = BATCH * SEQ
N_SAMPLE = DEC_BATCH * DEC_SEQ


def _stack_out(n_batch, n_slots, tail, slot, first):
    zeros = (0,) * len(tail)
    if first:
        spec = pl.BlockSpec((None, n_slots) + tail, lambda b, *_: (b, 0) + zeros)
    else:
        spec = pl.BlockSpec((None, None) + tail, lambda b, *_: (b, slot) + zeros)
    return spec, jax.ShapeDtypeStruct((n_batch, n_slots) + tail, F32)


def _store_slot(ref, slot, owns_stack, value):
    if not owns_stack:
        ref[...] = value
        return
    for s in range(ref.shape[0]):
        ref[s] = value if s == slot else jnp.zeros_like(value)


def _stream_index_maps(tile_rows, s_row0):
    n_c = N_PROMPT // tile_rows
    return (lambda i: (jnp.minimum(i, n_c - 1), 0)), (lambda i: (s_row0 // tile_rows + jnp.maximum(i - n_c, 0), 0))


def _in_proj_kernel(xc_ref, xs_ref, sh_ref, sc_ref, wt_ref, oc_ref, keep_ref, os_ref, wb_ref, *, row_groups, keep, n_c):
    i = pl.program_id(0)

    @pl.when(i == 0)
    def _():
        wb_ref[...] = jnp.zeros_like(wb_ref)
        for src, dst, size in row_groups:
            wb_ref[dst:dst + size, :] = wt_ref[src:src + size, :].astype(BF16)

    def project(x_ref, g):
        h = (_rms(x_ref[...]) * (1.0 + sc_ref[pl.ds(g, 1), :]) + sh_ref[pl.ds(g, 1), :]).astype(BF16)
        return _nt(h, wb_ref[...])

    @pl.when(i < n_c)
    def _():
        y = project(xc_ref, 0)
        oc_ref[...] = y.astype(oc_ref.dtype)
        keep_ref[...] = y[:, keep[0]:keep[0] + keep[1]]

    @pl.when(i >= n_c)
    def _():
        os_ref[...] = project(xs_ref, 1 + (i - n_c) // (MOD_ROWS // TM_IN)).astype(os_ref.dtype)


def _in_proj(xc, xs, s_row0, mod, layer, wt, w_layer, row_groups, n, keep):
    n_c, n_s = N_PROMPT // TM_IN, N_SAMPLE // TM_IN
    xc_idx, xs_idx = _stream_index_maps(TM_IN, s_row0)
    c_idx, s_idx = _stream_index_maps(TM_IN, 0)
    return pl.pallas_call(
        functools.partial(_in_proj_kernel, row_groups=row_groups, keep=keep, n_c=n_c),
        grid=(n_c + n_s,),
        in_specs=[pl.BlockSpec((TM_IN, D_MODEL), xc_idx), pl.BlockSpec((TM_IN, D_MODEL), xs_idx),
                  pl.BlockSpec((None, None, 8, D_MODEL), lambda i: (layer, 0, 0, 0)),
                  pl.BlockSpec((None, None, 8, D_MODEL), lambda i: (layer, 1, 0, 0)),
                  pl.BlockSpec((None, wt.shape[1], D_MODEL), lambda i: (w_layer, 0, 0), pipeline_mode=pl.Buffered(1))],
        out_specs=[pl.BlockSpec((TM_IN, n), c_idx), pl.BlockSpec((TM_IN, keep[1]), c_idx), pl.BlockSpec((TM_IN, n), s_idx)],
        out_shape=[jax.ShapeDtypeStruct((N_PROMPT, n), BF16), jax.ShapeDtypeStruct((N_PROMPT, keep[1]), F32),
                   jax.ShapeDtypeStruct((N_SAMPLE, n), BF16)],
        scratch_shapes=[pltpu.VMEM((n, D_MODEL), BF16)],
        compiler_params=_params(1),
        name="norm_mod_proj",
    )(xc, xs, mod, mod, wt)


def _ffn_kernel(x_ref, sh_ref, sc_ref, gate_ref, wg_ref, wu_ref, wd_ref, *refs, first_sub, final):
    (gf_ref, o_ref, h_ref) = refs if final else (None,) + refs
    n_sub = x_ref.shape[0] // MOD_ROWS
    subs = [(slice(s * MOD_ROWS, (s + 1) * MOD_ROWS),
             jnp.maximum(first_sub + pl.program_id(0) * n_sub + s - (N_PROMPT // MOD_ROWS - 1), 0)) for s in range(n_sub)]

    @pl.when(pl.program_id(1) == 0)
    def _():
        for rows, g in subs:
            x = x_ref[rows, :]
            o_ref[rows, :] = x
            h_ref[rows, :] = (_rms(x) * (1.0 + sc_ref[pl.ds(g, 1), :]) + sh_ref[pl.ds(g, 1), :]).astype(BF16)

    wg = wg_ref[...].astype(BF16)
    wu = wu_ref[...].astype(BF16)
    wd = wd_ref[...].astype(BF16)
    for rows, g in subs:
        h = h_ref[rows, :]
        a = _mm(h, wg)
        act = (a * jax.nn.sigmoid(a) * _mm(h, wu)).astype(BF16)
        o_ref[rows, :] += gate_ref[pl.ds(g, 1), :] * _mm(act, wd)

    if final:
        @pl.when(pl.program_id(1) == pl.num_programs(1) - 1)
        def _():
            for rows, _ in subs:
                o_ref[rows, :] = _rms(o_ref[rows, :]) * gf_ref[...]


def _ffn(x, row0, m, mod, layer, w_in, w_out, final_gain=None):
    nj = FFN_H // FFN_TN
    tile0 = row0 // TM_FFN
    mod_spec = lambda k: pl.BlockSpec((None, None, 8, D_MODEL), lambda i, j: (layer, k, 0, 0))
    final = final_gain is not None
    extra_specs = [pl.BlockSpec((1, D_MODEL), lambda i, j: (0, 0))] if final else []
    extra_args = [final_gain.reshape(1, D_MODEL)] if final else []
    return pl.pallas_call(
        functools.partial(_ffn_kernel, first_sub=row0 // MOD_ROWS, final=final),
        grid=(m // TM_FFN, nj),
        in_specs=[pl.BlockSpec((TM_FFN, D_MODEL), lambda i, j: (tile0 + i, 0)), mod_spec(3), mod_spec(4), mod_spec(5),
                  pl.BlockSpec((None, D_MODEL, FFN_TN), lambda i, j: (layer, 0, j)),
                  pl.BlockSpec((None, D_MODEL, FFN_TN), lambda i, j: (layer, 0, j + nj)),
                  pl.BlockSpec((None, FFN_TN, D_MODEL), lambda i, j: (layer, j, 0))] + extra_specs,
        out_specs=pl.BlockSpec((TM_FFN, D_MODEL), lambda i, j: (i, 0)),
        out_shape=jax.ShapeDtypeStruct((m, D_MODEL), F32),
        scratch_shapes=[pltpu.VMEM((TM_FFN, D_MODEL), BF16)],
        compiler_params=_params(2),
        name="ffn_residual",
    )(x, mod, mod, mod, w_in, w_in, w_out, *extra_args)


def _proj_res_kernel(ac0_ref, ac1_ref, as0_ref, as1_ref, w0_ref, w1_ref, xc_ref, xs_ref, gate_ref, o_ref, *, n_c):
    i = pl.program_id(0)

    def mix(a0_ref, a1_ref, x_ref, g):
        acc = _mm(a0_ref[...], w0_ref[...].astype(BF16)) + _mm(a1_ref[...], w1_ref[...].astype(BF16))
        return x_ref[...] + gate_ref[pl.ds(g, 1), :] * acc

    @pl.when(i < n_c)
    def _():
        o_ref[...] = mix(ac0_ref, ac1_ref, xc_ref, 0)

    @pl.when(i >= n_c)
    def _():
        o_ref[...] = mix(as0_ref, as1_ref, xs_ref, 1 + (i - n_c))


def _out_proj(acts_c, acts_s, w, w_layer, xc, xs, s_row0, mod, layer, k_gate):
    n_c, n_s = N_PROMPT // TM, N_SAMPLE // TM
    kw = acts_c[0].shape[1]
    xc_idx, xs_idx = _stream_index_maps(TM, s_row0)
    c_idx, s_idx = _stream_index_maps(TM, 0)
    w_specs = [pl.BlockSpec((None, kw, D_MODEL), functools.partial(lambda i, p: (w_layer, p, 0), p=p),
                            pipeline_mode=pl.Buffered(1)) for p in range(2)]
    return pl.pallas_call(
        functools.partial(_proj_res_kernel, n_c=n_c),
        grid=(n_c + n_s,),
        in_specs=[pl.BlockSpec((TM, kw), c_idx)] * 2 + [pl.BlockSpec((TM, kw), s_idx)] * 2 + w_specs + [
            pl.BlockSpec((TM, D_MODEL), xc_idx), pl.BlockSpec((TM, D_MODEL), xs_idx),
            pl.BlockSpec((None, None, 8, D_MODEL), lambda i: (layer, k_gate, 0, 0)),
        ],
        out_specs=pl.BlockSpec((TM, D_MODEL), lambda i: (i, 0)),
        out_shape=jax.ShapeDtypeStruct((N_PROMPT + N_SAMPLE, D_MODEL), F32),
        compiler_params=_params(1),
        name="out_proj_residual",
    )(*acts_c, *acts_s, w, w, xc, xs, mod)


def _gqa_kernel(*refs, sample, has_prev=False, slot=0):
    if sample:
        q_ref, k_ref, v_ref, gq_ref, gk_ref, ck_ref, cv_ref, cos_ref, sin_ref, o_ref, kb_ref, vb_ref = refs
    else:
        n_in = 8 if has_prev else 6
        q_ref, k_ref, v_ref, gq_ref, gk_ref, vf_ref = refs[:6]
        o_ref, kc_ref, vc_ref, kb_ref, vb_ref = refs[n_in:]
        if not has_prev:
            stacks = (kc_ref, vc_ref)
            kc_ref, vc_ref = kc_ref.at[slot], vc_ref.at[slot]
    qi = pl.program_id(1)
    n_new = k_ref.shape[0]
    past = PAST_LEN if sample else 0
    rep = H_B // KV_B

    @pl.when(qi == 0)
    def _():
        if not (sample or has_prev):
            for ref in stacks:
                for other in range(ref.shape[0]):
                    if other != slot:
                        ref[other] = jnp.zeros(ref.shape[1:], ref.dtype)
        for g in range(KV_B):
            sl = slice(HD_B * g, HD_B * (g + 1))
            kn = _rms(k_ref[:, sl].astype(F32)) * gk_ref[...]
            if sample:
                kb_ref[0:past, sl] = ck_ref[:, g, :].astype(BF16)
                vb_ref[g, 0:past, 0:HD_B] = cv_ref[:, g, :].astype(BF16)
                kn = kn * cos_ref[...] + pltpu.roll(kn, HD_B // 2, 1) * sin_ref[...]
            else:
                kc_ref[:, g, :] = kn
                vc_ref[:, g, :] = vf_ref[:, sl]
            kb_ref[past:past + n_new, sl] = kn.astype(BF16)
            vb_ref[g, past:past + n_new, 0:HD_B] = v_ref[:, sl].astype(BF16)
            vb_ref[g, :, HD_B:] = jnp.ones((past + n_new, HD_B), BF16)

    r0 = pl.multiple_of(qi * QB, QB)
    qs = []
    for h in range(H_B):
        qn = _rms(q_ref[:, HD_B * h:HD_B * (h + 1)].astype(F32)) * gq_ref[...]
        if sample:
            qn = qn * cos_ref[pl.ds(r0, QB), :] + pltpu.roll(qn, HD_B // 2, 1) * sin_ref[pl.ds(r0, QB), :]
        qs.append((qn * (HD_B ** -0.5)).astype(BF16))
    scores = [_nt(qs[h], kb_ref[:, HD_B * (h // rep):HD_B * (h // rep + 1)]) for h in range(H_B)]
    weights = [jnp.exp(s - jnp.max(s, axis=-1, keepdims=True)).astype(BF16) for s in scores]
    sums = [_mm(weights[h], vb_ref[h // rep]) for h in range(H_B)]
    for h in range(H_B):
        o_ref[:, HD_B * h:HD_B * (h + 1)] = (sums[h][:, :HD_B] / sums[h][:, HD_B:]).astype(o_ref.dtype)


def _gqa(proj, g_q, g_k, n_batch, seq, ctx=None, rope=None, v_f32=None, slot=0, prev=None):
    sample = ctx is not None
    m = n_batch * seq
    nq = seq // QB
    n_even = (DEPTH + 1) // 2
    in_specs = [
        pl.BlockSpec((QB, 512), lambda b, i: (b * nq + i, 3)),
        pl.BlockSpec((seq, 256), lambda b, i: (b, 8)),
        pl.BlockSpec((seq, 256), lambda b, i: (b, 9)),
        pl.BlockSpec((None, 1, HD_B), lambda b, i: (slot, 0, 0)),
        pl.BlockSpec((None, 1, HD_B), lambda b, i: (slot, 0, 0)),
    ]
    args = [proj, proj, proj, g_q, g_k]
    o_spec = pl.BlockSpec((QB, 512), lambda b, i: (b * nq + i, 0))
    o_shape = jax.ShapeDtypeStruct((m, 512), BF16)
    aliases = {}
    if sample:
        cache_spec = pl.BlockSpec((None, None, PAST_LEN, KV_B, HD_B), lambda b, i: (b, slot, 0, 0, 0))
        in_specs += [
            cache_spec, cache_spec,
            pl.BlockSpec((seq, HD_B), lambda b, i: (0, 0)),
            pl.BlockSpec((seq, HD_B), lambda b, i: (0, 0)),
        ]
        args += [ctx[0], ctx[1], rope[0], rope[1]]
        out_specs, out_shape = o_spec, o_shape
    else:
        in_specs.append(pl.BlockSpec((seq, KV_B * HD_B), lambda b, i: (b, 0)))
        args.append(v_f32)
        if prev is not None:
            in_specs += [pl.BlockSpec(memory_space=pl.ANY)] * 2
            aliases = {len(args): 1, len(args) + 1: 2}
            args += list(prev)
        if prev is None:
            new_spec = pl.BlockSpec((None, n_even, seq, KV_B, HD_B), lambda b, i: (b, 0, 0, 0, 0))
        else:
            new_spec = pl.BlockSpec((None, None, seq, KV_B, HD_B), lambda b, i: (b, slot, 0, 0, 0))
        new_shape = jax.ShapeDtypeStruct((n_batch, n_even, seq, KV_B, HD_B), F32)
        out_specs, out_shape = [o_spec, new_spec, new_spec], [o_shape, new_shape, new_shape]
    n_keys = seq + (PAST_LEN if sample else 0)
    return pl.pallas_call(
        functools.partial(_gqa_kernel, sample=sample, has_prev=prev is not None, slot=slot),
        grid=(n_batch, nq),
        in_specs=in_specs,
        out_specs=out_specs,
        out_shape=out_shape,
        input_output_aliases=aliases,
        scratch_shapes=[pltpu.VMEM((n_keys, KV_B * HD_B), BF16), pltpu.VMEM((KV_B, n_keys, 2 * HD_B), BF16)],
        compiler_params=_params(2),
        name="gqa_sample" if sample else "gqa_prompt",
    )(*args)


MLA_QW = 2 * LANES
MLA_HW = 4 * LANES


def _rotate_pairs(x, cos_t, sin_lo, sin_hi):
    w = x.shape[1]
    return x * cos_t + pltpu.roll(x, ROPE_D // 2, 1) * sin_hi + pltpu.roll(x, w - ROPE_D // 2, 1) * sin_lo


def _mla_kernel(*refs, sample, has_prev=False, slot=0):
    if sample:
        (cq_ref, ckv_ref, kpe_ref, gq_ref, wqb_ref, gkv_ref, wkvb_ref, cckv_ref, ckpe_ref,
         qc_ref, ql_ref, qh_ref, kc_ref, kl_ref, kh_ref, o_ref, kv_s) = refs
    else:
        cq_ref, ckv_ref, kpe_ref, gq_ref, wqb_ref, gkv_ref, wkvb_ref, kf_ref = refs[:8]
        o_ref, ckvn_ref, kpeo_ref, kv_s = refs[8 + (2 if has_prev else 0):]
    qi = pl.program_id(1)
    n_new = ckv_ref.shape[0]
    past = PAST_LEN if sample else 0

    def stage_kv(rows, kv, kpe_block):
        for h in range(H_D):
            kv_s[rows, MLA_HW * h:MLA_HW * h + NOPE_D] = kv[:, 256 * h:256 * h + NOPE_D].astype(BF16)
            kv_s[rows, MLA_HW * h + NOPE_D:MLA_HW * h + MLA_QW] = kpe_block
            kv_s[rows, MLA_HW * h + MLA_QW:MLA_HW * h + MLA_QW + V_D] = kv[:, 256 * h + NOPE_D:256 * (h + 1)].astype(BF16)

    @pl.when(qi == 0)
    def _():
        wkvb = wkvb_ref[...].astype(BF16)
        ckvn = _rms(ckv_ref[...].astype(F32)) * gkv_ref[...]
        if not sample:
            _store_slot(ckvn_ref, slot, not has_prev, ckvn)
            _store_slot(kpeo_ref, slot, not has_prev, kf_ref[:, 0:ROPE_D])
        kpe = kpe_ref[...]
        if sample:
            ctx_kpe = jnp.concatenate([ckpe_ref[...], jnp.zeros((past, LANES - ROPE_D), F32)], axis=1)
            stage_kv(slice(0, past), _mm(cckv_ref[...].astype(BF16), wkvb), ctx_kpe.astype(BF16))
            kpe = _rotate_pairs(kpe.astype(F32), kc_ref[...], kl_ref[...], kh_ref[...]).astype(BF16)
        stage_kv(slice(past, past + n_new), _mm(ckvn.astype(BF16), wkvb), kpe)
        for h in range(H_D):
            kv_s[:, MLA_HW * h + MLA_QW + V_D:MLA_HW * (h + 1)] = jnp.ones((past + n_new, V_D), BF16)

    q = _mm((_rms(cq_ref[...].astype(F32)) * gq_ref[...]).astype(BF16), wqb_ref[...].astype(BF16))
    q = q * ((NOPE_D + ROPE_D) ** -0.5)
    qs = []
    for h in range(H_D):
        q_h = q[:, MLA_QW * h:MLA_QW * (h + 1)]
        if sample:
            rows = pl.ds(pl.multiple_of(qi * QB, QB), QB)
            q_h = _rotate_pairs(q_h, qc_ref[rows, :], ql_ref[rows, :], qh_ref[rows, :])
        qs.append(q_h.astype(BF16))
    scores = [_nt(qs[h], kv_s[:, MLA_HW * h:MLA_HW * h + MLA_QW]) for h in range(H_D)]
    weights = [jnp.exp(s - jnp.max(s, axis=-1, keepdims=True)).astype(BF16) for s in scores]
    sums = [_mm(weights[h], kv_s[:, MLA_HW * h + MLA_QW:MLA_HW * (h + 1)]) for h in range(H_D)]
    for h in range(H_D):
        o_ref[:, V_D * h:V_D * (h + 1)] = (sums[h][:, :V_D] / sums[h][:, V_D:]).astype(o_ref.dtype)


def _mla(proj, g_q, w_qb, g_kv, w_kvb, n_batch, seq, ctx=None, rope=None, kpe_f32=None, slot=0, prev=None):
    sample = ctx is not None
    m = n_batch * seq
    nq = seq // QB
    in_specs = [
        pl.BlockSpec((QB, Q_RANK), lambda b, i: (b * nq + i, 6)),
        pl.BlockSpec((seq, KV_RANK), lambda b, i: (b, 14)),
        pl.BlockSpec((seq, LANES), lambda b, i: (b, 15)),
        pl.BlockSpec((None, 1, Q_RANK), lambda b, i: (slot, 0, 0)),
        pl.BlockSpec((None, Q_RANK, H_D * MLA_QW), lambda b, i: (slot, 0, 0)),
        pl.BlockSpec((None, 1, KV_RANK), lambda b, i: (slot, 0, 0)),
        pl.BlockSpec((None, KV_RANK, 1024), lambda b, i: (slot, 0, 0)),
    ]
    args = [proj, proj, proj, g_q, w_qb, g_kv, w_kvb]
    o_spec = pl.BlockSpec((QB, 512), lambda b, i: (b * nq + i, 0))
    o_shape = jax.ShapeDtypeStruct((m, 512), BF16)
    aliases = {}
    if sample:
        in_specs += [
            pl.BlockSpec((None, None, PAST_LEN, KV_RANK), lambda b, i: (b, slot, 0, 0)),
            pl.BlockSpec((None, None, PAST_LEN, ROPE_D), lambda b, i: (b, slot, 0, 0)),
        ] + [pl.BlockSpec((seq, MLA_QW), lambda b, i: (0, 0))] * 3 + [pl.BlockSpec((seq, LANES), lambda b, i: (0, 0))] * 3
        args += [ctx[0], ctx[1], *rope]
        out_specs, out_shape = o_spec, o_shape
    else:
        in_specs.append(pl.BlockSpec((seq, LANES), lambda b, i: (b, 0)))
        args.append(kpe_f32)
        if prev is not None:
            in_specs += [pl.BlockSpec(memory_space=pl.ANY)] * 2
            aliases = {len(args): 1, len(args) + 1: 2}
            args += list(prev)
        ckv_spec, ckv_shape = _stack_out(n_batch, DEPTH // 2, (seq, KV_RANK), slot, prev is None)
        kpe_spec, kpe_shape = _stack_out(n_batch, DEPTH // 2, (seq, ROPE_D), slot, prev is None)
        out_specs, out_shape = [o_spec, ckv_spec, kpe_spec], [o_shape, ckv_shape, kpe_shape]
    n_keys = seq + (PAST_LEN if sample else 0)
    return pl.pallas_call(
        functools.partial(_mla_kernel, sample=sample, has_prev=prev is not None, slot=slot),
        grid=(n_batch, nq),
        in_specs=in_specs,
        out_specs=out_specs,
        out_shape=out_shape,
        input_output_aliases=aliases,
        scratch_shapes=[pltpu.VMEM((n_keys, H_D * MLA_HW), BF16)],
        compiler_params=_params(2),
        name="mla_sample" if sample else "mla_prompt",
    )(*args)


def _dft(table, x):
    return _mm(table.astype(BF16), x.astype(BF16))


def _filter_kernel(z_ref, wf1_ref, bf1_ref, fr_ref, wf2_ref, bf2_ref, wf3_ref, t_ref, dl_ref,
                   c_ref, s_ref, gre_ref, gim_ref):
    n_tok = z_ref.shape[0]
    fr = fr_ref[...]
    hid = jnp.sin(fr * (_mm(z_ref[...].astype(BF16), wf1_ref[...].astype(BF16)) + bf1_ref[...]))
    hid = jnp.sin(fr * (_mm(hid.astype(BF16), wf2_ref[...].astype(BF16)) + bf2_ref[...]))
    filt = _mm(hid.astype(BF16), wf3_ref[...].astype(BF16))
    decay = jnp.exp(-t_ref[...] * dl_ref[...])
    row = lax.broadcasted_iota(jnp.int32, (n_tok, 1), 0)
    h_f = filt[:, :HY_W] * decay
    h_b = jnp.where(row == 0, 0.0, filt[:, HY_W:] * decay)
    p, m = h_f + h_b, h_f - h_b
    g_re = _dft(c_ref[...], p)
    g_im = _dft(s_ref[...], m)
    sign = jnp.where(row % 2 == 0, 1.0, -1.0)
    nyquist = jnp.sum(p * sign, axis=0, keepdims=True)
    g_im = jnp.where(row == 0, nyquist, g_im)
    wk = jnp.where(row == 0, 0.5 / n_tok, 1.0 / n_tok)
    gre_ref[...] = g_re * wk
    gim_ref[...] = g_im * wk


def _filter_spectrum(z, wf1, bf1, freq, wf2, bf2, wf3, t_col, deltas, tabs):
    n_layers, n_tok = wf1.shape[0], z.shape[0]
    shared = lambda a: pl.BlockSpec(a.shape, lambda l: (0,) * a.ndim)
    per_layer = lambda a: pl.BlockSpec((None,) + a.shape[1:], lambda l: (l,) + (0,) * (a.ndim - 1))
    row = lambda a: a.reshape(n_layers, 1, FILT_HID)
    args = [z, wf1, row(bf1), row(freq), wf2, row(bf2), wf3, t_col, deltas, tabs[0], tabs[1]]
    layered = [False, True, True, True, True, True, True, False, False, False, False]
    out = jax.ShapeDtypeStruct((n_layers, n_tok, HY_W), F32)
    out_spec = pl.BlockSpec((None, n_tok, HY_W), lambda l: (l, 0, 0))
    return pl.pallas_call(
        _filter_kernel,
        grid=(n_layers,),
        in_specs=[per_layer(a) if lay else shared(a) for a, lay in zip(args, layered)],
        out_specs=[out_spec, out_spec],
        out_shape=[out, out],
        compiler_params=_params(1),
        name="hyena_filter",
    )(*args)


HY_ROWS = 1024


def _hyena_channels(seq):
    return HY_W if seq <= 256 else HY_W // 2


def _hyena_kernel(u0_ref, u1_ref, u2_ref, w0_ref, w1_ref, w2_ref, b0_ref, b1_ref, b2_ref, skip_ref,
                  gre_ref, gim_ref, cf_ref, sf_ref, stf_ref, o_ref, c_ref, s_ref, st_ref):
    seq = c_ref.shape[0]
    n_rows = u0_ref.shape[0]
    n_seq = n_rows // seq
    pos = lax.broadcasted_iota(jnp.int32, (n_rows, 1), 0) % seq

    @pl.when((pl.program_id(0) == 0) & (pl.program_id(1) == 0))
    def _():
        c_ref[...] = cf_ref[...].astype(BF16)
        s_ref[...] = sf_ref[...].astype(BF16)
        st_ref[...] = stf_ref[...].astype(BF16)

    def short_conv(u_ref, w_ref, b_ref):
        x, w = u_ref[...].astype(F32), w_ref[...]
        prev = jnp.where(pos == 0, 0.0, pltpu.roll(x, 1, 0))
        nxt = jnp.where(pos == seq - 1, 0.0, pltpu.roll(x, n_rows - 1, 0))
        return prev * w[0:1] + x * w[1:2] + nxt * w[2:3] + b_ref[...]

    def side_by_side(a):
        return a if n_seq == 1 else jnp.concatenate([a[s * seq:(s + 1) * seq] for s in range(n_seq)], axis=1)

    def stacked(a):
        ct = a.shape[1] // n_seq
        return a if n_seq == 1 else jnp.concatenate([a[:, s * ct:(s + 1) * ct] for s in range(n_seq)], axis=0)

    x0 = short_conv(u0_ref, w0_ref, b0_ref)
    gv = short_conv(u1_ref, w1_ref, b1_ref) * short_conv(u2_ref, w2_ref, b2_ref)
    sig = side_by_side(gv).astype(BF16)
    u_re = _mm(c_ref[...], sig)
    u_im = _mm(s_ref[...], sig)
    g_re = jnp.concatenate([gre_ref[...]] * n_seq, axis=1)
    g_im = jnp.concatenate([gim_ref[...]] * n_seq, axis=1)
    bin0 = lax.broadcasted_iota(jnp.int32, (seq, 1), 0) == 0
    p_im = u_im * g_im
    y_re = u_re * g_re - jnp.where(bin0, 0.0, p_im)
    y_im = jnp.where(bin0, p_im, u_re * g_im + u_im * g_re)
    y = stacked(_mm(c_ref[...], y_re.astype(BF16)) + _mm(st_ref[...], y_im.astype(BF16)))
    o_ref[...] = (x0 * (y + gv * skip_ref[...])).astype(o_ref.dtype)


def _hyena(proj, layer, w_conv, b_conv, skip, g_re, g_im, tabs, n_batch, seq):
    ct = _hyena_channels(seq)
    nct = HY_W // ct
    u_specs = [pl.BlockSpec((HY_ROWS, ct), functools.partial(lambda b, c, g: (b, g * nct + c), g=g)) for g in range(3)]
    w_specs = [pl.BlockSpec((None, 3, ct), functools.partial(lambda b, c, g: (layer, 0, g * nct + c), g=g)) for g in range(3)]
    b_specs = [pl.BlockSpec((None, 1, ct), functools.partial(lambda b, c, g: (layer, 0, g * nct + c), g=g)) for g in range(3)]
    tab_spec = pl.BlockSpec((seq, seq), lambda b, c: (0, 0))
    return pl.pallas_call(
        _hyena_kernel,
        grid=(n_batch * seq // HY_ROWS, nct),
        in_specs=u_specs + w_specs + b_specs + [
            pl.BlockSpec((None, 1, ct), lambda b, c: (layer, 0, c)),
            pl.BlockSpec((None, seq, ct), lambda b, c: (layer, 0, c)),
            pl.BlockSpec((None, seq, ct), lambda b, c: (layer, 0, c)),
        ] + [tab_spec] * 3,
        out_specs=pl.BlockSpec((HY_ROWS, ct), lambda b, c: (b, c)),
        out_shape=jax.ShapeDtypeStruct((n_batch * seq, HY_W), BF16),
        scratch_shapes=[pltpu.VMEM((seq, seq), BF16)] * 3,
        compiler_params=_params(2),
        name="hyena_conv",
    )(proj, proj, proj, w_conv, w_conv, w_conv, b_conv, b_conv, b_conv, skip, g_re, g_im, *tabs)


def _dft_tables(n_tok):
    k = np.arange(n_tok)[:, None]
    s = np.arange(n_tok)[None, :]
    ang = ((k * s) % (2 * n_tok)) * (np.pi / n_tok)
    cos_t = np.cos(ang)
    sin_f = np.where(k == 0, np.where(s % 2 == 0, 1.0, -1.0), -np.sin(ang))
    return [jnp.asarray(t, F32) for t in (cos_t, sin_f, sin_f.T)]


GLA_LEVELS = (32, 16, 8, 4, 2, 1)
GLA_SAFE_DECAY = 60.0
GLA_GROUP = 2


def _gla_constants():
    c = GLA_CHUNK
    idx = np.arange(c)
    i, t = idx[:, None], idx[None, :]
    masks = []
    for s in GLA_LEVELS:
        upper = (idx % (2 * s)) >= s
        masks.append(((i // (2 * s)) == (t // (2 * s))) & upper[:, None] & (~upper)[None, :])
    masks.append(i == t)
    tri = t <= i
    fwd_m = np.stack([np.tile(m, (H_A, 1)) for m in masks]).astype(np.float32)
    bwd_m = np.stack([np.tile(m[::-1, ::-1], (H_A, 1)) for m in masks]).astype(np.float32)
    head_of_row = np.repeat(np.arange(H_A), c)[:, None]
    head_of_lane = np.repeat(np.arange(H_A), DK_A)[None, :]
    head_mask = head_of_row == head_of_lane
    return (jnp.asarray(tri, BF16), jnp.asarray(tri[::-1, ::-1], BF16), jnp.asarray(fwd_m), jnp.asarray(bwd_m),
            jnp.asarray(head_mask, BF16))


def _pair_reference(b, s, backward, row):
    c = GLA_CHUNK
    ref = s if backward else s - 1
    if 2 * s >= 8:
        pieces = [jnp.broadcast_to(b[p * 2 * s + ref:p * 2 * s + ref + 1, :], (2 * s, b.shape[1]))
                  for p in range(c // (2 * s))]
        return pieces[0] if len(pieces) == 1 else jnp.concatenate(pieces, axis=0)
    pos = row % (2 * s)
    out = None
    for o in range(2 * s):
        d = ref - o
        shifted = b if d == 0 else pltpu.roll(b, (-d) % c, 0)
        out = shifted if out is None else jnp.where(pos == o, shifted, out)
    return out


def _chunk_log_decay(la, t_ref):
    l1 = la.astype(BF16)
    r1 = la - l1.astype(F32)
    l2 = r1.astype(BF16)
    l3 = (r1 - l2.astype(F32)).astype(BF16)
    tmat = t_ref[...]
    return _mm(tmat, l1) + _mm(tmat, l2) + _mm(tmat, l3)


def _stack_heads(a, hm):
    ab = a.astype(BF16)
    return jnp.concatenate([ab] * H_A, axis=0) * hm


def _state_terms(k, v, b, b_last):
    c = GLA_CHUNK
    k_rest = (k * jnp.exp(b_last - b)).T
    carry = jnp.broadcast_to(jnp.exp(b_last), (2 * c, b.shape[1])).T
    return k_rest.astype(BF16), carry


def _gla_chunk(q, k, v, la, t_ref, m_ref, hm, s_ref, backward):
    c = GLA_CHUNK
    b = _chunk_log_decay(la, t_ref)
    row = lax.broadcasted_iota(jnp.int32, (c, 1), 0)
    last = 0 if backward else c - 1
    b_last = b[last:last + 1, :]
    scores = _nt(_stack_heads(q, hm), k.astype(BF16)) * m_ref[len(GLA_LEVELS)]
    for lvl, s in enumerate(GLA_LEVELS):
        is_query = (row % (2 * s) < s) if backward else (row % (2 * s) >= s)
        delta = b - _pair_reference(b, s, backward, row)
        x = jnp.exp(jnp.where(is_query, delta, -delta))
        scores = scores + _nt(_stack_heads(q * x, hm), (k * x).astype(BF16)) * m_ref[lvl]
    scores = scores.astype(BF16)
    state = s_ref[...]
    inter = _mm(_stack_heads(q * jnp.exp(b), hm), state.astype(BF16))
    k_rest, carry = _state_terms(k, v, b, b_last)
    outs = []
    for h in range(H_A):
        rows = slice(c * h, c * (h + 1))
        v_h = v[:, DV_A * h:DV_A * (h + 1)]
        outs.append(_mm(scores[rows], v_h) + inter[rows])
        s_ref[rows, :] = state[rows] * carry[rows] + _mm(k_rest[rows], v_h)
    return jnp.concatenate(outs, axis=1)


def _gla_local(items, hm):
    c = GLA_CHUNK
    bs = [_chunk_log_decay(la, t_ref) for _, _, _, la, t_ref, _, _ in items]
    b_lasts = [b[(0 if it[6] else c - 1):(0 if it[6] else c - 1) + 1, :] for b, it in zip(bs, items)]
    q_decayed = [_stack_heads(it[0] * jnp.exp(b), hm) for it, b in zip(items, bs)]
    k_grown = [(it[1] * jnp.exp(-b)).astype(BF16) for it, b in zip(items, bs)]
    raw = [_nt(qd, kg) for qd, kg in zip(q_decayed, k_grown)]
    masked = [r * it[5] for r, it in zip(raw, items)]
    terms = [_state_terms(it[1], it[2], b, bl) for it, b, bl in zip(items, bs, b_lasts)]
    out = []
    for n in range(0, len(items), 2):
        v = items[n][2]
        heads = [(slice(c * h, c * (h + 1)), v[:, DV_A * h:DV_A * (h + 1)]) for h in range(H_A)]
        both = (masked[n] + masked[n + 1]).astype(BF16)
        intra = jnp.concatenate([_mm(both[rows], v_h) for rows, v_h in heads], axis=1)
        for m in (n, n + 1):
            k_rest, carry = terms[m]
            incr = jnp.concatenate([_mm(k_rest[rows], v_h) for rows, v_h in heads], axis=0)
            out.append((intra if m == n else None, q_decayed[m], incr, carry))
    return out


def _gla_kernel(*refs, sample, has_prev=False, slot=0):
    if sample:
        (x_ref, z_ref, wf_ref, bf_ref, wb_ref, bb_ref, tf_ref, tb_ref, mf_ref, mb_ref, hm_ref, gn_ref, sf0_ref, sb0_ref,
         o_ref, la_f, la_b, o_f, o_b, s_f, s_b, qd_f, qd_b, ds_f, ds_b, cr_f, cr_b) = refs
    else:
        x_ref, z_ref, wf_ref, bf_ref, wb_ref, bb_ref, tf_ref, tb_ref, mf_ref, mb_ref, hm_ref, gn_ref = refs[:12]
        (o_ref, sf_out, sb_out, la_f, la_b, o_f, o_b, s_f, s_b,
         qd_f, qd_b, ds_f, ds_b, cr_f, cr_b) = refs[12 + (2 if has_prev else 0):]
    n_tok = x_ref.shape[0]
    n_chunks = n_tok // GLA_CHUNK
    hk, hv = H_A * DK_A, H_A * DV_A
    zb = z_ref[...].astype(BF16)

    def log_sigmoid(t):
        return jnp.minimum(t, 0.0) - jnp.log(1.0 + jnp.exp(-jnp.abs(t)))

    la_f[...] = log_sigmoid(_mm(zb, wf_ref[...].astype(BF16)) + bf_ref[...]) / GLA_TAU
    la_b[...] = log_sigmoid(_mm(zb, wb_ref[...].astype(BF16)) + bb_ref[...]) / GLA_TAU
    if sample:
        s_f[...] = sf0_ref[...]
        s_b[...] = sb0_ref[...]
    else:
        s_f[...] = jnp.zeros_like(s_f)
        s_b[...] = jnp.zeros_like(s_b)
    hm = hm_ref[...]

    fwd = (la_f, tf_ref, mf_ref, s_f, o_f, qd_f, ds_f, cr_f, False)
    bwd = (la_b, tb_ref, mb_ref, s_b, o_b, qd_b, ds_b, cr_b, True)
    tri_f = jnp.sum(mf_ref[...], axis=0)
    tri_b = jnp.sum(mb_ref[...], axis=0)

    def chunk_rows(ci, backward):
        cidx = n_chunks - 1 - ci if backward else ci
        return cidx, pl.ds(pl.multiple_of(cidx * GLA_CHUNK, GLA_CHUNK), GLA_CHUNK)

    def load_qkv(rows):
        q = x_ref[rows, 0:hk].astype(F32) * (DK_A ** -0.5)
        return q, x_ref[rows, hk:2 * hk].astype(F32), x_ref[rows, 2 * hk:2 * hk + hv]

    def safe_step(ci, carry):
        for la_ref, t_ref, m_ref, s_ref, out_ref, _, _, _, backward in (fwd, bwd):
            _, rows = chunk_rows(ci, backward)
            out_ref[rows, :] = _gla_chunk(*load_qkv(rows), la_ref[rows, :], t_ref, m_ref, hm, s_ref, backward)
        return carry

    def local_step(gi, carry):
        items, dests = [], []
        for u in range(GLA_GROUP):
            cidx = gi * GLA_GROUP + u
            rows = pl.ds(pl.multiple_of(cidx * GLA_CHUNK, GLA_CHUNK), GLA_CHUNK)
            qkv = load_qkv(rows)
            for (la_ref, t_ref, _, _, out_ref, qd_ref, ds_ref, cr_ref, backward), tri in ((fwd, tri_f), (bwd, tri_b)):
                items.append((*qkv, la_ref[rows, :], t_ref, tri, backward))
                dests.append((out_ref, rows, qd_ref, ds_ref, cr_ref, cidx))
        for (out_ref, rows, qd_ref, ds_ref, cr_ref, cidx), (intra, qd, incr, factor) in zip(dests, _gla_local(items, hm)):
            out_ref[rows, :] = jnp.zeros((GLA_CHUNK, hv), F32) if intra is None else intra
            qd_ref[cidx] = qd
            ds_ref[cidx] = incr
            cr_ref[cidx] = factor
        return carry

    def scan_step(ci, carry):
        for _, _, _, s_ref, out_ref, qd_ref, ds_ref, cr_ref, backward in (fwd, bwd):
            cidx, rows = chunk_rows(ci, backward)
            state = s_ref[...]
            inter = _mm(qd_ref[cidx], state.astype(BF16))
            out_ref[rows, :] += jnp.concatenate(
                [inter[GLA_CHUNK * h:GLA_CHUNK * (h + 1)] for h in range(H_A)], axis=1)
            s_ref[...] = state * cr_ref[cidx] + ds_ref[cidx]
        return carry

    chunk_sums = [jnp.sum(ref[...].reshape(n_chunks, GLA_CHUNK, hk), axis=1) for ref in (la_f, la_b)]
    mild = jnp.minimum(jnp.min(chunk_sums[0]), jnp.min(chunk_sums[1])) > -GLA_SAFE_DECAY

    @pl.when(mild)
    def _():
        lax.fori_loop(0, n_chunks // GLA_GROUP, local_step, 0, unroll=2)
        lax.fori_loop(0, n_chunks, scan_step, 0, unroll=2)

    @pl.when(jnp.logical_not(mild))
    def _():
        lax.fori_loop(0, n_chunks, safe_step, 0)
    if not sample:
        _store_slot(sf_out, slot, not has_prev, s_f[...])
        _store_slot(sb_out, slot, not has_prev, s_b[...])
    gain = gn_ref[...]
    for h in range(H_A):
        cols = slice(DV_A * h, DV_A * (h + 1))
        r = x_ref[:, 2 * hk + hv + DV_A * h:2 * hk + hv + DV_A * (h + 1)].astype(F32)
        o_ref[:, cols] = (_rms(o_f[:, cols] + o_b[:, cols]) * gain * (r * jax.nn.sigmoid(r))).astype(o_ref.dtype)


def _gla(proj, w_gf, b_gf, w_gb, b_gb, g_norm, consts, n_batch, seq, ctx=None, slot=0, prev=None):
    sample = ctx is not None
    hk, hv = H_A * DK_A, H_A * DV_A
    n_ch = seq // GLA_CHUNK
    full = lambda shape: pl.BlockSpec(shape, lambda b: (0,) * len(shape))
    layered = lambda shape: pl.BlockSpec((None,) + shape, lambda b: (slot,) + (0,) * len(shape))
    in_specs = [
        pl.BlockSpec((seq, 2 * hk + 2 * hv), lambda b: (b, 0)),
        pl.BlockSpec((seq, LANES), lambda b: (b, EVEN_W // LANES - 1)),
        layered((LANES, hk)), layered((1, hk)), layered((LANES, hk)), layered((1, hk)),
        full(consts[0].shape), full(consts[1].shape), full(consts[2].shape), full(consts[3].shape), full(consts[4].shape),
        layered((1, DV_A)),
    ]
    args = [proj, proj, w_gf, b_gf, w_gb, b_gb, *consts, g_norm]
    o_spec = pl.BlockSpec((seq, hv), lambda b: (b, 0))
    o_shape = jax.ShapeDtypeStruct((n_batch * seq, hv), BF16)
    aliases = {}
    if sample:
        st_spec = pl.BlockSpec((None, None, hk, DV_A), lambda b: (b, slot, 0, 0))
        in_specs += [st_spec, st_spec]
        args += [ctx[0], ctx[1]]
        out_specs, out_shape = o_spec, o_shape
    else:
        if prev is not None:
            in_specs += [pl.BlockSpec(memory_space=pl.ANY)] * 2
            aliases = {len(args): 1, len(args) + 1: 2}
            args += list(prev)
        st_spec, st_shape = _stack_out(n_batch, (DEPTH + 1) // 2, (hk, DV_A), slot, prev is None)
        out_specs, out_shape = [o_spec, st_spec, st_spec], [o_shape, st_shape, st_shape]
    return pl.pallas_call(
        functools.partial(_gla_kernel, sample=sample, has_prev=prev is not None, slot=slot),
        grid=(n_batch,),
        in_specs=in_specs,
        out_specs=out_specs,
        out_shape=out_shape,
        input_output_aliases=aliases,
        scratch_shapes=[pltpu.VMEM((seq, hk), F32), pltpu.VMEM((seq, hk), F32),
                        pltpu.VMEM((seq, hv), F32), pltpu.VMEM((seq, hv), F32),
                        pltpu.VMEM((hk, DV_A), F32), pltpu.VMEM((hk, DV_A), F32),
                        pltpu.VMEM((n_ch, H_A * GLA_CHUNK, hk), BF16), pltpu.VMEM((n_ch, H_A * GLA_CHUNK, hk), BF16),
                        pltpu.VMEM((n_ch, hk, DV_A), F32), pltpu.VMEM((n_ch, hk, DV_A), F32),
                        pltpu.VMEM((n_ch, hk, DV_A), F32), pltpu.VMEM((n_ch, hk, DV_A), F32)],
        compiler_params=_params(1),
        name="gla_sample" if sample else "gla_prompt",
    )(*args)


def _axial_rope(n_tokens, dim):
    rows = n_tokens // GRID_W
    row = np.repeat(np.arange(rows), GRID_W).astype(np.float64)
    col = np.tile(np.arange(GRID_W), rows).astype(np.float64)
    n_freq = dim // 4
    inv = ROPE_THETA ** (-np.arange(n_freq) / n_freq)
    ang = np.concatenate([row[:, None] * inv, col[:, None] * inv], axis=-1)
    return np.cos(ang).astype(np.float32), np.sin(ang).astype(np.float32)


def _filter_features(n_tokens):
    t = np.linspace(0.0, 1.0, n_tokens)[:, None]
    w = 2.0 * np.pi * np.arange(n_tokens)[:, None] / n_tokens
    f = np.linspace(1e-4, FILT_BANDS - 1, FILT_BANDS)[None, :]
    z = np.concatenate([t, np.cos(f * w), -np.sin(f * w)], axis=-1)
    z = np.pad(z, ((0, 0), (0, LANES - FILT_EMB)))
    return jnp.asarray(z, F32), jnp.asarray(t, F32)


_QB_ZERO = H_D * (NOPE_D + ROPE_D)
_QB_PERM = np.array([(NOPE_D + ROPE_D) * (p // MLA_QW) + p % MLA_QW if p % MLA_QW < NOPE_D + ROPE_D else _QB_ZERO
                     for p in range(H_D * MLA_QW)])


def _mla_rope_tables(cos_d, sin_d):
    n, half = cos_d.shape
    zeros = np.zeros((n, half), np.float32)

    def lanes(pre, width):
        pad = np.zeros((n, width - pre.shape[1] - 2 * half), np.float32)
        build = lambda first, second, lead: np.concatenate([lead, first, second, pad], axis=1)
        return (build(cos_d, cos_d, pre), build(-sin_d, zeros, 0 * pre), build(zeros, sin_d, 0 * pre))

    q_tabs = lanes(np.ones((n, NOPE_D), np.float32), MLA_QW)
    k_tabs = lanes(np.zeros((n, 0), np.float32), LANES)
    return tuple(jnp.asarray(t) for t in q_tabs + k_tabs)

EVEN_ROW_GROUPS = ((0, 0, 1536), (1568, 1536, 1024), (1536, EVEN_W - 2 * GATE_RANK, 2 * GATE_RANK))
ODD_ROW_GROUPS = ((0, 0, 1984),)
EVEN_KEEP = (2304, 256)
ODD_KEEP = (1920, LANES)


def kernel(x_prompt, x_sample, state_gla_fwd, state_gla_bwd, cache_gqa_k, cache_gqa_v, cache_mla_ckv, cache_mla_kpe, c, c_ctx, w_mod, b_mod, w_in_even, w_gla_gate_f, b_gla_gate_f, w_gla_gate_b, b_gla_gate_b, g_gla_norm, g_gqa_q, g_gqa_k, w_out_even, w_in_odd, w_hy_conv, b_hy_conv, hy_skip, w_filt1, b_filt1, filt_freq, w_filt2, b_filt2, w_filt3, g_mla_q, w_mla_qb, g_mla_kv, w_mla_kvb, w_out_odd, w_ffn_in, w_ffn_out, g_final):
    cvec = jnp.concatenate([c_ctx[None, :], c, jnp.zeros((8 - 1 - DEC_BATCH, D_MODEL), F32)], axis=0)
    mod = _modulation(cvec, w_mod, b_mod)
    xc, xs, s_row0 = x_prompt.reshape(N_PROMPT, D_MODEL), x_sample.reshape(N_SAMPLE, D_MODEL), 0

    gla_consts = _gla_constants()
    cos_b, sin_b = _axial_rope(DEC_SEQ, HD_B)
    rope_b = (jnp.asarray(np.concatenate([cos_b, cos_b], axis=1)), jnp.asarray(np.concatenate([-sin_b, sin_b], axis=1)))
    cos_d, sin_d = _axial_rope(DEC_SEQ, ROPE_D)
    rope_d = _mla_rope_tables(cos_d, sin_d)
    w_qb_all = jnp.pad(w_mla_qb, ((0, 0), (0, 0), (0, 1)))[:, :, _QB_PERM]
    tabs_c, tabs_s = _dft_tables(SEQ), _dft_tables(DEC_SEQ)
    z_c, t_c = _filter_features(SEQ)
    z_s, t_s = _filter_features(DEC_SEQ)
    deltas = jnp.asarray(np.abs(np.linspace(HY_MIN_DECAY, HY_MAX_DECAY, HY_W))[None, :], F32)

    wt_even = jnp.swapaxes(w_in_even, 1, 2)
    wt_odd = jnp.swapaxes(w_in_odd, 1, 2)

    filt_args = (jnp.pad(w_filt1, ((0, 0), (0, LANES - FILT_EMB), (0, 0))), b_filt1, filt_freq, w_filt2, b_filt2, w_filt3)
    g_c = _filter_spectrum(z_c, *filt_args, t_c, deltas, tabs_c)
    g_s = _filter_spectrum(z_s, *filt_args, t_s, deltas, tabs_s)
    b_conv = b_hy_conv.reshape(DEPTH // 2, 1, 3 * HY_W)
    skip = hy_skip.reshape(DEPTH // 2, 1, HY_W)

    n_even, n_odd = (DEPTH + 1) // 2, DEPTH // 2
    hk = H_A * DK_A
    z0 = LANES - 2 * GATE_RANK
    pad_f = jnp.zeros((n_even, LANES, hk), F32).at[:, z0:z0 + GATE_RANK].set(w_gla_gate_f)
    pad_b = jnp.zeros((n_even, LANES, hk), F32).at[:, z0 + GATE_RANK:].set(w_gla_gate_b)
    gate_args = (pad_f, b_gla_gate_f.reshape(n_even, 1, hk), pad_b, b_gla_gate_b.reshape(n_even, 1, hk),
                 g_gla_norm.reshape(n_even, 1, DV_A), gla_consts)
    ctx_a = (state_gla_fwd.reshape(DEC_BATCH, n_even, hk, DV_A), state_gla_bwd.reshape(DEC_BATCH, n_even, hk, DV_A))
    gq_b, gk_b = g_gqa_q.reshape(n_even, 1, HD_B), g_gqa_k.reshape(n_even, 1, HD_B)
    mla_args = (g_mla_q.reshape(n_odd, 1, Q_RANK), w_qb_all, g_mla_kv.reshape(n_odd, 1, KV_RANK), w_mla_kvb)

    new_states = new_kv = new_latent = None
    for i in range(DEPTH):
        j = i // 2
        if i % 2 == 0:
            pc, v_new, ps = _in_proj(xc, xs, s_row0, mod, i, wt_even, j, EVEN_ROW_GROUPS, EVEN_W, EVEN_KEEP)
            a_c, *new_states = _gla(pc, *gate_args, BATCH, SEQ, slot=j, prev=new_states)
            a_s = _gla(ps, *gate_args, DEC_BATCH, DEC_SEQ, ctx=ctx_a, slot=j)
            b_c, *new_kv = _gqa(pc, gq_b, gk_b, BATCH, SEQ, v_f32=v_new, slot=j, prev=new_kv)
            b_s = _gqa(ps, gq_b, gk_b, DEC_BATCH, DEC_SEQ, ctx=(cache_gqa_k, cache_gqa_v), rope=rope_b, slot=j)
            w_out = w_out_even
        else:
            pc, kpe_new, ps = _in_proj(xc, xs, s_row0, mod, i, wt_odd, j, ODD_ROW_GROUPS, ODD_W, ODD_KEEP)
            a_c = _hyena(pc, j, w_hy_conv, b_conv, skip, g_c[0], g_c[1], tabs_c, BATCH, SEQ)
            a_s = _hyena(ps, j, w_hy_conv, b_conv, skip, g_s[0], g_s[1], tabs_s, DEC_BATCH, DEC_SEQ)
            b_c, *new_latent = _mla(pc, *mla_args, BATCH, SEQ, kpe_f32=kpe_new, slot=j, prev=new_latent)
            b_s = _mla(ps, *mla_args, DEC_BATCH, DEC_SEQ, ctx=(cache_mla_ckv, cache_mla_kpe), rope=rope_d, slot=j)
            w_out = w_out_odd
        x_mid = _out_proj([a_c, b_c], [a_s, b_s], w_out, j, xc, xs, s_row0, mod, i, 2)
        if i < DEPTH - 1:
            x_all = _ffn(x_mid, 0, N_PROMPT + N_SAMPLE, mod, i, w_ffn_in, w_ffn_out)
            xc, xs, s_row0 = x_all, x_all, N_PROMPT
        else:
            xc = _ffn(x_mid, 0, N_PROMPT, mod, i, w_ffn_in, w_ffn_out, final_gain=g_final)
            xs = _ffn(x_mid, N_PROMPT, N_SAMPLE, mod, i, w_ffn_in, w_ffn_out, final_gain=g_final)
    y_prompt = xc.reshape(BATCH, SEQ, D_MODEL)
    y_sample = xs.reshape(DEC_BATCH, DEC_SEQ, D_MODEL)
    state_shape = (BATCH, (DEPTH + 1) // 2, H_A, DK_A, DV_A)
    return (y_prompt, y_sample, new_states[0].reshape(state_shape), new_states[1].reshape(state_shape),
            new_kv[0], new_kv[1], new_latent[0], new_latent[1])
```

```python
import functools
import math

import numpy as np
import jax
import jax.numpy as jnp
from jax import lax
from jax.experimental import pallas as pl
from jax.experimental.pallas import tpu as pltpu

F32 = jnp.float32
BF16 = jnp.bfloat16

D_MODEL = 1024
BATCH, SEQ = 16, 256
DEC_BATCH, DEC_SEQ = 2, 1024
DEPTH = 4
PAST_LEN = 512
GRID_W = 64
HALF_W = D_MODEL // 2
H_A, DV_A, DK_A = 4, 128, 64
GATE_RANK = 16
GLA_TAU = 16.0
GLA_CHUNK = 64
HD_B, H_B, KV_B = 128, 4, 2
HY_W = HALF_W
FILT_EMB, FILT_HID = 33, 64
FILT_BANDS = (FILT_EMB - 1) // 2
HY_MIN_DECAY = math.log(1e-2) / 1.5
HY_MAX_DECAY = math.log(1e-2) / 0.3
H_D, V_D, NOPE_D, ROPE_D = 4, 128, 128, 64
Q_RANK, KV_RANK = 256, 128
FFN_H = 2816
ROPE_THETA = 10000.0
EPS = 1e-6

LANES = 128
VMEM_LIMIT = 56 * 1024 * 1024

MOD_ROWS = 1024
TM = 1024
TM_IN = 512
TM_FFN = 2048
EVEN_W = 2688
ODD_W = 2048
FFN_TN = 256
QB = 256


def _params(n_grid):
    return pltpu.CompilerParams(dimension_semantics=("arbitrary",) * n_grid, vmem_limit_bytes=VMEM_LIMIT)


def _nt(a, b):
    return lax.dot_general(a, b, (((1,), (1,)), ((), ())), preferred_element_type=F32)


def _mm(a, b):
    return jnp.dot(a, b, preferred_element_type=F32)


def _rms(x):
    return x * lax.rsqrt(jnp.mean(x * x, axis=-1, keepdims=True) + EPS)


MOD_GROUP = 3


def _mod_kernel(c_ref, w_ref, b_ref, o_ref):
    cv = c_ref[...]
    s = cv * jax.nn.sigmoid(cv)
    m = _mm(s.astype(BF16), w_ref[...].astype(BF16)) + b_ref[...]
    for k in range(MOD_GROUP):
        o_ref[k] = m[:, D_MODEL * k:D_MODEL * (k + 1)]


def _modulation(cvec, w_mod, b_mod):
    return pl.pallas_call(
        _mod_kernel,
        grid=(DEPTH, 6 // MOD_GROUP),
        in_specs=[
            pl.BlockSpec((8, D_MODEL), lambda l, n: (0, 0)),
            pl.BlockSpec((None, D_MODEL, MOD_GROUP * D_MODEL), lambda l, n: (l, 0, n)),
            pl.BlockSpec((None, 1, MOD_GROUP * D_MODEL), lambda l, n: (l, 0, n)),
        ],
        out_specs=pl.BlockSpec((None, MOD_GROUP, 8, D_MODEL), lambda l, n: (l, n, 0, 0)),
        out_shape=jax.ShapeDtypeStruct((DEPTH, 6, 8, D_MODEL), F32),
        compiler_params=_params(2),
        name="adaln_mod",
    )(cvec, w_mod, b_mod.reshape(DEPTH, 1, 6 * D_MODEL))


N_PROMPT = BATCH * SEQ
N_SAMPLE = DEC_BATCH * DEC_SEQ


def _stack_out(n_batch, n_slots, tail, slot, first):
    zeros = (0,) * len(tail)
    if first:
        spec = pl.BlockSpec((None, n_slots) + tail, lambda b, *_: (b, 0) + zeros)
    else:
        spec = pl.BlockSpec((None, None) + tail, lambda b, *_: (b, slot) + zeros)
    return spec, jax.ShapeDtypeStruct((n_batch, n_slots) + tail, F32)


def _store_slot(ref, slot, owns_stack, value):
    if not owns_stack:
        ref[...] = value
        return
    for s in range(ref.shape[0]):
        ref[s] = value if s == slot else jnp.zeros_like(value)


def _stream_index_maps(tile_rows, s_row0):
    n_c = N_PROMPT // tile_rows
    return (lambda i: (jnp.minimum(i, n_c - 1), 0)), (lambda i: (s_row0 // tile_rows + jnp.maximum(i - n_c, 0), 0))


def _in_proj_kernel(xc_ref, xs_ref, sh_ref, sc_ref, wt_ref, oc_ref, keep_ref, os_ref, wb_ref, *, row_groups, keep, n_c):
    i = pl.program_id(0)

    @pl.when(i == 0)
    def _():
        wb_ref[...] = jnp.zeros_like(wb_ref)
        for src, dst, size in row_groups:
            wb_ref[dst:dst + size, :] = wt_ref[src:src + size, :].astype(BF16)

    def project(x_ref, g):
        h = (_rms(x_ref[...]) * (1.0 + sc_ref[pl.ds(g, 1), :]) + sh_ref[pl.ds(g, 1), :]).astype(BF16)
        return _nt(h, wb_ref[...])

    @pl.when(i < n_c)
    def _():
        y = project(xc_ref, 0)
        oc_ref[...] = y.astype(oc_ref.dtype)
        keep_ref[...] = y[:, keep[0]:keep[0] + keep[1]]

    @pl.when(i >= n_c)
    def _():
        os_ref[...] = project(xs_ref, 1 + (i - n_c) // (MOD_ROWS // TM_IN)).astype(os_ref.dtype)


def _in_proj(xc, xs, s_row0, mod, layer, wt, w_layer, row_groups, n, keep):
    n_c, n_s = N_PROMPT // TM_IN, N_SAMPLE // TM_IN
    xc_idx, xs_idx = _stream_index_maps(TM_IN, s_row0)
    c_idx, s_idx = _stream_index_maps(TM_IN, 0)
    return pl.pallas_call(
        functools.partial(_in_proj_kernel, row_groups=row_groups, keep=keep, n_c=n_c),
        grid=(n_c + n_s,),
        in_specs=[pl.BlockSpec((TM_IN, D_MODEL), xc_idx), pl.BlockSpec((TM_IN, D_MODEL), xs_idx),
                  pl.BlockSpec((None, None, 8, D_MODEL), lambda i: (layer, 0, 0, 0)),
                  pl.BlockSpec((None, None, 8, D_MODEL), lambda i: (layer, 1, 0, 0)),
                  pl.BlockSpec((None, wt.shape[1], D_MODEL), lambda i: (w_layer, 0, 0), pipeline_mode=pl.Buffered(1))],
        out_specs=[pl.BlockSpec((TM_IN, n), c_idx), pl.BlockSpec((TM_IN, keep[1]), c_idx), pl.BlockSpec((TM_IN, n), s_idx)],
        out_shape=[jax.ShapeDtypeStruct((N_PROMPT, n), BF16), jax.ShapeDtypeStruct((N_PROMPT, keep[1]), F32),
                   jax.ShapeDtypeStruct((N_SAMPLE, n), BF16)],
        scratch_shapes=[pltpu.VMEM((n, D_MODEL), BF16)],
        compiler_params=_params(1),
        name="norm_mod_proj",
    )(xc, xs, mod, mod, wt)


def _ffn_kernel(x_ref, sh_ref, sc_ref, gate_ref, wg_ref, wu_ref, wd_ref, *refs, first_sub, final):
    (gf_ref, o_ref, h_ref) = refs if final else (None,) + refs
    n_sub = x_ref.shape[0] // MOD_ROWS
    subs = [(slice(s * MOD_ROWS, (s + 1) * MOD_ROWS),
             jnp.maximum(first_sub + pl.program_id(0) * n_sub + s - (N_PROMPT // MOD_ROWS - 1), 0)) for s in range(n_sub)]

    @pl.when(pl.program_id(1) == 0)
    def _():
        for rows, g in subs:
            x = x_ref[rows, :]
            o_ref[rows, :] = x
            h_ref[rows, :] = (_rms(x) * (1.0 + sc_ref[pl.ds(g, 1), :]) + sh_ref[pl.ds(g, 1), :]).astype(BF16)

    wg = wg_ref[...].astype(BF16)
    wu = wu_ref[...].astype(BF16)
    wd = wd_ref[...].astype(BF16)
    for rows, g in subs:
        h = h_ref[rows, :]
        a = _mm(h, wg)
        act = (a * jax.nn.sigmoid(a) * _mm(h, wu)).astype(BF16)
        o_ref[rows, :] += gate_ref[pl.ds(g, 1), :] * _mm(act, wd)

    if final:
        @pl.when(pl.program_id(1) == pl.num_programs(1) - 1)
        def _():
            for rows, _ in subs:
                o_ref[rows, :] = _rms(o_ref[rows, :]) * gf_ref[...]


def _ffn(x, row0, m, mod, layer, w_in, w_out, final_gain=None):
    nj = FFN_H // FFN_TN
    tile0 = row0 // TM_FFN
    mod_spec = lambda k: pl.BlockSpec((None, None, 8, D_MODEL), lambda i, j: (layer, k, 0, 0))
    final = final_gain is not None
    extra_specs = [pl.BlockSpec((1, D_MODEL), lambda i, j: (0, 0))] if final else []
    extra_args = [final_gain.reshape(1, D_MODEL)] if final else []
    return pl.pallas_call(
        functools.partial(_ffn_kernel, first_sub=row0 // MOD_ROWS, final=final),
        grid=(m // TM_FFN, nj),
        in_specs=[pl.BlockSpec((TM_FFN, D_MODEL), lambda i, j: (tile0 + i, 0)), mod_spec(3), mod_spec(4), mod_spec(5),
                  pl.BlockSpec((None, D_MODEL, FFN_TN), lambda i, j: (layer, 0, j)),
                  pl.BlockSpec((None, D_MODEL, FFN_TN), lambda i, j: (layer, 0, j + nj)),
                  pl.BlockSpec((None, FFN_TN, D_MODEL), lambda i, j: (layer, j, 0))] + extra_specs,
        out_specs=pl.BlockSpec((TM_FFN, D_MODEL), lambda i, j: (i, 0)),
        out_shape=jax.ShapeDtypeStruct((m, D_MODEL), F32),
        scratch_shapes=[pltpu.VMEM((TM_FFN, D_MODEL), BF16)],
        compiler_params=_params(2),
        name="ffn_residual",
    )(x, mod, mod, mod, w_in, w_in, w_out, *extra_args)


def _proj_res_kernel(ac0_ref, ac1_ref, as0_ref, as1_ref, w0_ref, w1_ref, xc_ref, xs_ref, gate_ref, o_ref, *, n_c):
    i = pl.program_id(0)

    def mix(a0_ref, a1_ref, x_ref, g):
        acc = _mm(a0_ref[...], w0_ref[...].astype(BF16)) + _mm(a1_ref[...], w1_ref[...].astype(BF16))
        return x_ref[...] + gate_ref[pl.ds(g, 1), :] * acc

    @pl.when(i < n_c)
    def _():
        o_ref[...] = mix(ac0_ref, ac1_ref, xc_ref, 0)

    @pl.when(i >= n_c)
    def _():
        o_ref[...] = mix(as0_ref, as1_ref, xs_ref, 1 + (i - n_c))


def _out_proj(acts_c, acts_s, w, w_layer, xc, xs, s_row0, mod, layer, k_gate):
    n_c, n_s = N_PROMPT // TM, N_SAMPLE // TM
    kw = acts_c[0].shape[1]
    xc_idx, xs_idx = _stream_index_maps(TM, s_row0)
    c_idx, s_idx = _stream_index_maps(TM, 0)
    w_specs = [pl.BlockSpec((None, kw, D_MODEL), functools.partial(lambda i, p: (w_layer, p, 0), p=p),
                            pipeline_mode=pl.Buffered(1)) for p in range(2)]
    return pl.pallas_call(
        functools.partial(_proj_res_kernel, n_c=n_c),
        grid=(n_c + n_s,),
        in_specs=[pl.BlockSpec((TM, kw), c_idx)] * 2 + [pl.BlockSpec((TM, kw), s_idx)] * 2 + w_specs + [
            pl.BlockSpec((TM, D_MODEL), xc_idx), pl.BlockSpec((TM, D_MODEL), xs_idx),
            pl.BlockSpec((None, None, 8, D_MODEL), lambda i: (layer, k_gate, 0, 0)),
        ],
        out_specs=pl.BlockSpec((TM, D_MODEL), lambda i: (i, 0)),
        out_shape=jax.ShapeDtypeStruct((N_PROMPT + N_SAMPLE, D_MODEL), F32),
        compiler_params=_params(1),
        name="out_proj_residual",
    )(*acts_c, *acts_s, w, w, xc, xs, mod)


def _gqa_kernel(*refs, sample, has_prev=False, slot=0):
    if sample:
        q_ref, k_ref, v_ref, gq_ref, gk_ref, ck_ref, cv_ref, cos_ref, sin_ref, o_ref, kb_ref, vb_ref = refs
    else:
        n_in = 8 if has_prev else 6
        q_ref, k_ref, v_ref, gq_ref, gk_ref, vf_ref = refs[:6]
        o_ref, kc_ref, vc_ref, kb_ref, vb_ref = refs[n_in:]
        if not has_prev:
            stacks = (kc_ref, vc_ref)
            kc_ref, vc_ref = kc_ref.at[slot], vc_ref.at[slot]
    qi = pl.program_id(1)
    n_new = k_ref.shape[0]
    past = PAST_LEN if sample else 0
    rep = H_B // KV_B

    @pl.when(qi == 0)
    def _():
        if not (sample or has_prev):
            for ref in stacks:
                for other in range(ref.shape[0]):
                    if other != slot:
                        ref[other] = jnp.zeros(ref.shape[1:], ref.dtype)
        for g in range(KV_B):
            sl = slice(HD_B * g, HD_B * (g + 1))
            kn = _rms(k_ref[:, sl].astype(F32)) * gk_ref[...]
            if sample:
                kb_ref[0:past, sl] = ck_ref[:, g, :].astype(BF16)
                vb_ref[g, 0:past, 0:HD_B] = cv_ref[:, g, :].astype(BF16)
                kn = kn * cos_ref[...] + pltpu.roll(kn, HD_B // 2, 1) * sin_ref[...]
            else:
                kc_ref[:, g, :] = kn
                vc_ref[:, g, :] = vf_ref[:, sl]
            kb_ref[past:past + n_new, sl] = kn.astype(BF16)
            vb_ref[g, past:past + n_new, 0:HD_B] = v_ref[:, sl].astype(BF16)
            vb_ref[g, :, HD_B:] = jnp.ones((past + n_new, HD_B), BF16)

    r0 = pl.multiple_of(qi * QB, QB)
    qs = []
    for h in range(H_B):
        qn = _rms(q_ref[:, HD_B * h:HD_B * (h + 1)].astype(F32)) * gq_ref[...]
        if sample:
            qn = qn * cos_ref[pl.ds(r0, QB), :] + pltpu.roll(qn, HD_B // 2, 1) * sin_ref[pl.ds(r0, QB), :]
        qs.append((qn * (HD_B ** -0.5)).astype(BF16))
    scores = [_nt(qs[h], kb_ref[:, HD_B * (h // rep):HD_B * (h // rep + 1)]) for h in range(H_B)]
    weights = [jnp.exp(s - jnp.max(s, axis=-1, keepdims=True)).astype(BF16) for s in scores]
    sums = [_mm(weights[h], vb_ref[h // rep]) for h in range(H_B)]
    for h in range(H_B):
        o_ref[:, HD_B * h:HD_B * (h + 1)] = (sums[h][:, :HD_B] / sums[h][:, HD_B:]).astype(o_ref.dtype)


def _gqa(proj, g_q, g_k, n_batch, seq, ctx=None, rope=None, v_f32=None, slot=0, prev=None):
    sample = ctx is not None
    m = n_batch * seq
    nq = seq // QB
    n_even = (DEPTH + 1) // 2
    in_specs = [
        pl.BlockSpec((QB, 512), lambda b, i: (b * nq + i, 3)),
        pl.BlockSpec((seq, 256), lambda b, i: (b, 8)),
        pl.BlockSpec((seq, 256), lambda b, i: (b, 9)),
        pl.BlockSpec((None, 1, HD_B), lambda b, i: (slot, 0, 0)),
        pl.BlockSpec((None, 1, HD_B), lambda b, i: (slot, 0, 0)),
    ]
    args = [proj, proj, proj, g_q, g_k]
    o_spec = pl.BlockSpec((QB, 512), lambda b, i: (b * nq + i, 0))
    o_shape = jax.ShapeDtypeStruct((m, 512), BF16)
    aliases = {}
    if sample:
        cache_spec = pl.BlockSpec((None, None, PAST_LEN, KV_B, HD_B), lambda b, i: (b, slot, 0, 0, 0))
        in_specs += [
            cache_spec, cache_spec,
            pl.BlockSpec((seq, HD_B), lambda b, i: (0, 0)),
            pl.BlockSpec((seq, HD_B), lambda b, i: (0, 0)),
        ]
        args += [ctx[0], ctx[1], rope[0], rope[1]]
        out_specs, out_shape = o_spec, o_shape
    else:
        in_specs.append(pl.BlockSpec((seq, KV_B * HD_B), lambda b, i: (b, 0)))
        args.append(v_f32)
        if prev is not None:
            in_specs += [pl.BlockSpec(memory_space=pl.ANY)] * 2
            aliases = {len(args): 1, len(args) + 1: 2}
            args += list(prev)
        if prev is None:
            new_spec = pl.BlockSpec((None, n_even, seq, KV_B, HD_B), lambda b, i: (b, 0, 0, 0, 0))
        else:
            new_spec = pl.BlockSpec((None, None, seq, KV_B, HD_B), lambda b, i: (b, slot, 0, 0, 0))
        new_shape = jax.ShapeDtypeStruct((n_batch, n_even, seq, KV_B, HD_B), F32)
        out_specs, out_shape = [o_spec, new_spec, new_spec], [o_shape, new_shape, new_shape]
    n_keys = seq + (PAST_LEN if sample else 0)
    return pl.pallas_call(
        functools.partial(_gqa_kernel, sample=sample, has_prev=prev is not None, slot=slot),
        grid=(n_batch, nq),
        in_specs=in_specs,
        out_specs=out_specs,
        out_shape=out_shape,
        input_output_aliases=aliases,
        scratch_shapes=[pltpu.VMEM((n_keys, KV_B * HD_B), BF16), pltpu.VMEM((KV_B, n_keys, 2 * HD_B), BF16)],
        compiler_params=_params(2),
        name="gqa_sample" if sample else "gqa_prompt",
    )(*args)


MLA_QW = 2 * LANES
MLA_HW = 4 * LANES


def _rotate_pairs(x, cos_t, sin_lo, sin_hi):
    w = x.shape[1]
    return x * cos_t + pltpu.roll(x, ROPE_D // 2, 1) * sin_hi + pltpu.roll(x, w - ROPE_D // 2, 1) * sin_lo


def _mla_kernel(*refs, sample, has_prev=False, slot=0):
    if sample:
        (cq_ref, ckv_ref, kpe_ref, gq_ref, wqb_ref, gkv_ref, wkvb_ref, cckv_ref, ckpe_ref,
         qc_ref, ql_ref, qh_ref, kc_ref, kl_ref, kh_ref, o_ref, kv_s) = refs
    else:
        cq_ref, ckv_ref, kpe_ref, gq_ref, wqb_ref, gkv_ref, wkvb_ref, kf_ref = refs[:8]
        o_ref, ckvn_ref, kpeo_ref, kv_s = refs[8 + (2 if has_prev else 0):]
    qi = pl.program_id(1)
    n_new = ckv_ref.shape[0]
    past = PAST_LEN if sample else 0

    def stage_kv(rows, kv, kpe_block):
        for h in range(H_D):
            kv_s[rows, MLA_HW * h:MLA_HW * h + NOPE_D] = kv[:, 256 * h:256 * h + NOPE_D].astype(BF16)
            kv_s[rows, MLA_HW * h + NOPE_D:MLA_HW * h + MLA_QW] = kpe_block
            kv_s[rows, MLA_HW * h + MLA_QW:MLA_HW * h + MLA_QW + V_D] = kv[:, 256 * h + NOPE_D:256 * (h + 1)].astype(BF16)

    @pl.when(qi == 0)
    def _():
        wkvb = wkvb_ref[...].astype(BF16)
        ckvn = _rms(ckv_ref[...].astype(F32)) * gkv_ref[...]
        if not sample:
            _store_slot(ckvn_ref, slot, not has_prev, ckvn)
            _store_slot(kpeo_ref, slot, not has_prev, kf_ref[:, 0:ROPE_D])
        kpe = kpe_ref[...]
        if sample:
            ctx_kpe = jnp.concatenate([ckpe_ref[...], jnp.zeros((past, LANES - ROPE_D), F32)], axis=1)
            stage_kv(slice(0, past), _mm(cckv_ref[...].astype(BF16), wkvb), ctx_kpe.astype(BF16))
            kpe = _rotate_pairs(kpe.astype(F32), kc_ref[...], kl_ref[...], kh_ref[...]).astype(BF16)
        stage_kv(slice(past, past + n_new), _mm(ckvn.astype(BF16), wkvb), kpe)
        for h in range(H_D):
            kv_s[:, MLA_HW * h + MLA_QW + V_D:MLA_HW * (h + 1)] = jnp.ones((past + n_new, V_D), BF16)

    q = _mm((_rms(cq_ref[...].astype(F32)) * gq_ref[...]).astype(BF16), wqb_ref[...].astype(BF16))
    q = q * ((NOPE_D + ROPE_D) ** -0.5)
    qs = []
    for h in range(H_D):
        q_h = q[:, MLA_QW * h:MLA_QW * (h + 1)]
        if sample:
            rows = pl.ds(pl.multiple_of(qi * QB, QB), QB)
            q_h = _rotate_pairs(q_h, qc_ref[rows, :], ql_ref[rows, :], qh_ref[rows, :])
        qs.append(q_h.astype(BF16))
    scores = [_nt(qs[h], kv_s[:, MLA_HW * h:MLA_HW * h + MLA_QW]) for h in range(H_D)]
    weights = [jnp.exp(s - jnp.max(s, axis=-1, keepdims=True)).astype(BF16) for s in scores]
    sums = [_mm(weights[h], kv_s[:, MLA_HW * h + MLA_QW:MLA_HW * (h + 1)]) for h in range(H_D)]
    for h in range(H_D):
        o_ref[:, V_D * h:V_D * (h + 1)] = (sums[h][:, :V_D] / sums[h][:, V_D:]).astype(o_ref.dtype)


def _mla(proj, g_q, w_qb, g_kv, w_kvb, n_batch, seq, ctx=None, rope=None, kpe_f32=None, slot=0, prev=None):
    sample = ctx is not None
    m = n_batch * seq
    nq = seq // QB
    in_specs = [
        pl.BlockSpec((QB, Q_RANK), lambda b, i: (b * nq + i, 6)),
        pl.BlockSpec((seq, KV_RANK), lambda b, i: (b, 14)),
        pl.BlockSpec((seq, LANES), lambda b, i: (b, 15)),
        pl.BlockSpec((None, 1, Q_RANK), lambda b, i: (slot, 0, 0)),
        pl.BlockSpec((None, Q_RANK, H_D * MLA_QW), lambda b, i: (slot, 0, 0)),
        pl.BlockSpec((None, 1, KV_RANK), lambda b, i: (slot, 0, 0)),
        pl.BlockSpec((None, KV_RANK, 1024), lambda b, i: (slot, 0, 0)),
    ]
    args = [proj, proj, proj, g_q, w_qb, g_kv, w_kvb]
    o_spec = pl.BlockSpec((QB, 512), lambda b, i: (b * nq + i, 0))
    o_shape = jax.ShapeDtypeStruct((m, 512), BF16)
    aliases = {}
    if sample:
        in_specs += [
            pl.BlockSpec((None, None, PAST_LEN, KV_RANK), lambda b, i: (b, slot, 0, 0)),
            pl.BlockSpec((None, None, PAST_LEN, ROPE_D), lambda b, i: (b, slot, 0, 0)),
        ] + [pl.BlockSpec((seq, MLA_QW), lambda b, i: (0, 0))] * 3 + [pl.BlockSpec((seq, LANES), lambda b, i: (0, 0))] * 3
        args += [ctx[0], ctx[1], *rope]
        out_specs, out_shape = o_spec, o_shape
    else:
        in_specs.append(pl.BlockSpec((seq, LANES), lambda b, i: (b, 0)))
        args.append(kpe_f32)
        if prev is not None:
            in_specs += [pl.BlockSpec(memory_space=pl.ANY)] * 2
            aliases = {len(args): 1, len(args) + 1: 2}
            args += list(prev)
        ckv_spec, ckv_shape = _stack_out(n_batch, DEPTH // 2, (seq, KV_RANK), slot, prev is None)
        kpe_spec, kpe_shape = _stack_out(n_batch, DEPTH // 2, (seq, ROPE_D), slot, prev is None)
        out_specs, out_shape = [o_spec, ckv_spec, kpe_spec], [o_shape, ckv_shape, kpe_shape]
    n_keys = seq + (PAST_LEN if sample else 0)
    return pl.pallas_call(
        functools.partial(_mla_kernel, sample=sample, has_prev=prev is not None, slot=slot),
        grid=(n_batch, nq),
        in_specs=in_specs,
        out_specs=out_specs,
        out_shape=out_shape,
        input_output_aliases=aliases,
        scratch_shapes=[pltpu.VMEM((n_keys, H_D * MLA_HW), BF16)],
        compiler_params=_params(2),
        name="mla_sample" if sample else "mla_prompt",
    )(*args)


def _dft(table, x):
    return _mm(table.astype(BF16), x.astype(BF16))


def _filter_kernel(z_ref, wf1_ref, bf1_ref, fr_ref, wf2_ref, bf2_ref, wf3_ref, t_ref, dl_ref,
                   c_ref, s_ref, gre_ref, gim_ref):
    n_tok = z_ref.shape[0]
    fr = fr_ref[...]
    hid = jnp.sin(fr * (_mm(z_ref[...].astype(BF16), wf1_ref[...].astype(BF16)) + bf1_ref[...]))
    hid = jnp.sin(fr * (_mm(hid.astype(BF16), wf2_ref[...].astype(BF16)) + bf2_ref[...]))
    filt = _mm(hid.astype(BF16), wf3_ref[...].astype(BF16))
    decay = jnp.exp(-t_ref[...] * dl_ref[...])
    row = lax.broadcasted_iota(jnp.int32, (n_tok, 1), 0)
    h_f = filt[:, :HY_W] * decay
    h_b = jnp.where(row == 0, 0.0, filt[:, HY_W:] * decay)
    p, m = h_f + h_b, h_f - h_b
    g_re = _dft(c_ref[...], p)
    g_im = _dft(s_ref[...], m)
    sign = jnp.where(row % 2 == 0, 1.0, -1.0)
    nyquist = jnp.sum(p * sign, axis=0, keepdims=True)
    g_im = jnp.where(row == 0, nyquist, g_im)
    wk = jnp.where(row == 0, 0.5 / n_tok, 1.0 / n_tok)
    gre_ref[...] = g_re * wk
    gim_ref[...] = g_im * wk


def _filter_spectrum(z, wf1, bf1, freq, wf2, bf2, wf3, t_col, deltas, tabs):
    n_layers, n_tok = wf1.shape[0], z.shape[0]
    shared = lambda a: pl.BlockSpec(a.shape, lambda l: (0,) * a.ndim)
    per_layer = lambda a: pl.BlockSpec((None,) + a.shape[1:], lambda l: (l,) + (0,) * (a.ndim - 1))
    row = lambda a: a.reshape(n_layers, 1, FILT_HID)
    args = [z, wf1, row(bf1), row(freq), wf2, row(bf2), wf3, t_col, deltas, tabs[0], tabs[1]]
    layered = [False, True, True, True, True, True, True, False, False, False, False]
    out = jax.ShapeDtypeStruct((n_layers, n_tok, HY_W), F32)
    out_spec = pl.BlockSpec((None, n_tok, HY_W), lambda l: (l, 0, 0))
    return pl.pallas_call(
        _filter_kernel,
        grid=(n_layers,),
        in_specs=[per_layer(a) if lay else shared(a) for a, lay in zip(args, layered)],
        out_specs=[out_spec, out_spec],
        out_shape=[out, out],
        compiler_params=_params(1),
        name="hyena_filter",
    )(*args)


HY_ROWS = 1024


def _hyena_channels(seq):
    return HY_W if seq <= 256 else HY_W // 2


def _hyena_kernel(u0_ref, u1_ref, u2_ref, w0_ref, w1_ref, w2_ref, b0_ref, b1_ref, b2_ref, skip_ref,
                  gre_ref, gim_ref, cf_ref, sf_ref, stf_ref, o_ref, c_ref, s_ref, st_ref):
    seq = c_ref.shape[0]
    n_rows = u0_ref.shape[0]
    n_seq = n_rows // seq
    pos = lax.broadcasted_iota(jnp.int32, (n_rows, 1), 0) % seq

    @pl.when((pl.program_id(0) == 0) & (pl.program_id(1) == 0))
    def _():
        c_ref[...] = cf_ref[...].astype(BF16)
        s_ref[...] = sf_ref[...].astype(BF16)
        st_ref[...] = stf_ref[...].astype(BF16)

    def short_conv(u_ref, w_ref, b_ref):
        x, w = u_ref[...].astype(F32), w_ref[...]
        prev = jnp.where(pos == 0, 0.0, pltpu.roll(x, 1, 0))
        nxt = jnp.where(pos == seq - 1, 0.0, pltpu.roll(x, n_rows - 1, 0))
        return prev * w[0:1] + x * w[1:2] + nxt * w[2:3] + b_ref[...]

    def side_by_side(a):
        return a if n_seq == 1 else jnp.concatenate([a[s * seq:(s + 1) * seq] for s in range(n_seq)], axis=1)

    def stacked(a):
        ct = a.shape[1] // n_seq
        return a if n_seq == 1 else jnp.concatenate([a[:, s * ct:(s + 1) * ct] for s in range(n_seq)], axis=0)

    x0 = short_conv(u0_ref, w0_ref, b0_ref)
    gv = short_conv(u1_ref, w1_ref, b1_ref) * short_conv(u2_ref, w2_ref, b2_ref)
    sig = side_by_side(gv).astype(BF16)
    u_re = _mm(c_ref[...], sig)
    u_im = _mm(s_ref[...], sig)
    g_re = jnp.concatenate([gre_ref[...]] * n_seq, axis=1)
    g_im = jnp.concatenate([gim_ref[...]] * n_seq, axis=1)
    bin0 = lax.broadcasted_iota(jnp.int32, (seq, 1), 0) == 0
    p_im = u_im * g_im
    y_re = u_re * g_re - jnp.where(bin0, 0.0, p_im)
    y_im = jnp.where(bin0, p_im, u_re * g_im + u_im * g_re)
    y = stacked(_mm(c_ref[...], y_re.astype(BF16)) + _mm(st_ref[...], y_im.astype(BF16)))
    o_ref[...] = (x0 * (y + gv * skip_ref[...])).astype(o_ref.dtype)


def _hyena(proj, layer, w_conv, b_conv, skip, g_re, g_im, tabs, n_batch, seq):
    ct = _hyena_channels(seq)
    nct = HY_W // ct
    u_specs = [pl.BlockSpec((HY_ROWS, ct), functools.partial(lambda b, c, g: (b, g * nct + c), g=g)) for g in range(3)]
    w_specs = [pl.BlockSpec((None, 3, ct), functools.partial(lambda b, c, g: (layer, 0, g * nct + c), g=g)) for g in range(3)]
    b_specs = [pl.BlockSpec((None, 1, ct), functools.partial(lambda b, c, g: (layer, 0, g * nct + c), g=g)) for g in range(3)]
    tab_spec = pl.BlockSpec((seq, seq), lambda b, c: (0, 0))
    return pl.pallas_call(
        _hyena_kernel,
        grid=(n_batch * seq // HY_ROWS, nct),
        in_specs=u_specs + w_specs + b_specs + [
            pl.BlockSpec((None, 1, ct), lambda b, c: (layer, 0, c)),
            pl.BlockSpec((None, seq, ct), lambda b, c: (layer, 0, c)),
            pl.BlockSpec((None, seq, ct), lambda b, c: (layer, 0, c)),
        ] + [tab_spec] * 3,
        out_specs=pl.BlockSpec((HY_ROWS, ct), lambda b, c: (b, c)),
        out_shape=jax.ShapeDtypeStruct((n_batch * seq, HY_W), BF16),
        scratch_shapes=[pltpu.VMEM((seq, seq), BF16)] * 3,
        compiler_params=_params(2),
        name="hyena_conv",
    )(proj, proj, proj, w_conv, w_conv, w_conv, b_conv, b_conv, b_conv, skip, g_re, g_im, *tabs)


def _dft_tables(n_tok):
    k = np.arange(n_tok)[:, None]
    s = np.arange(n_tok)[None, :]
    ang = ((k * s) % (2 * n_tok)) * (np.pi / n_tok)
    cos_t = np.cos(ang)
    sin_f = np.where(k == 0, np.where(s % 2 == 0, 1.0, -1.0), -np.sin(ang))
    return [jnp.asarray(t, F32) for t in (cos_t, sin_f, sin_f.T)]


GLA_LEVELS = (32, 16, 8, 4, 2, 1)
GLA_SAFE_DECAY = 60.0
GLA_GROUP = 2


def _gla_constants():
    c = GLA_CHUNK
    idx = np.arange(c)
    i, t = idx[:, None], idx[None, :]
    masks = []
    for s in GLA_LEVELS:
        upper = (idx % (2 * s)) >= s
        masks.append(((i // (2 * s)) == (t // (2 * s))) & upper[:, None] & (~upper)[None, :])
    masks.append(i == t)
    tri = t <= i
    fwd_m = np.stack([np.tile(m, (H_A, 1)) for m in masks]).astype(np.float32)
    bwd_m = np.stack([np.tile(m[::-1, ::-1], (H_A, 1)) for m in masks]).astype(np.float32)
    head_of_row = np.repeat(np.arange(H_A), c)[:, None]
    head_of_lane = np.repeat(np.arange(H_A), DK_A)[None, :]
    head_mask = head_of_row == head_of_lane
    return (jnp.asarray(tri, BF16), jnp.asarray(tri[::-1, ::-1], BF16), jnp.asarray(fwd_m), jnp.asarray(bwd_m),
            jnp.asarray(head_mask, BF16))


def _pair_reference(b, s, backward, row):
    c = GLA_CHUNK
    ref = s if backward else s - 1
    if 2 * s >= 8:
        pieces = [jnp.broadcast_to(b[p * 2 * s + ref:p * 2 * s + ref + 1, :], (2 * s, b.shape[1]))
                  for p in range(c // (2 * s))]
        return pieces[0] if len(pieces) == 1 else jnp.concatenate(pieces, axis=0)
    pos = row % (2 * s)
    out = None
    for o in range(2 * s):
        d = ref - o
        shifted = b if d == 0 else pltpu.roll(b, (-d) % c, 0)
        out = shifted if out is None else jnp.where(pos == o, shifted, out)
    return out


def _chunk_log_decay(la, t_ref):
    l1 = la.astype(BF16)
    r1 = la - l1.astype(F32)
    l2 = r1.astype(BF16)
    l3 = (r1 - l2.astype(F32)).astype(BF16)
    tmat = t_ref[...]
    return _mm(tmat, l1) + _mm(tmat, l2) + _mm(tmat, l3)


def _stack_heads(a, hm):
    ab = a.astype(BF16)
    return jnp.concatenate([ab] * H_A, axis=0) * hm


def _state_terms(k, v, b, b_last):
    c = GLA_CHUNK
    k_rest = (k * jnp.exp(b_last - b)).T
    carry = jnp.broadcast_to(jnp.exp(b_last), (2 * c, b.shape[1])).T
    return k_rest.astype(BF16), carry


def _gla_chunk(q, k, v, la, t_ref, m_ref, hm, s_ref, backward):
    c = GLA_CHUNK
    b = _chunk_log_decay(la, t_ref)
    row = lax.broadcasted_iota(jnp.int32, (c, 1), 0)
    last = 0 if backward else c - 1
    b_last = b[last:last + 1, :]
    scores = _nt(_stack_heads(q, hm), k.astype(BF16)) * m_ref[len(GLA_LEVELS)]
    for lvl, s in enumerate(GLA_LEVELS):
        is_query = (row % (2 * s) < s) if backward else (row % (2 * s) >= s)
        delta = b - _pair_reference(b, s, backward, row)
        x = jnp.exp(jnp.where(is_query, delta, -delta))
        scores = scores + _nt(_stack_heads(q * x, hm), (k * x).astype(BF16)) * m_ref[lvl]
    scores = scores.astype(BF16)
    state = s_ref[...]
    inter = _mm(_stack_heads(q * jnp.exp(b), hm), state.astype(BF16))
    k_rest, carry = _state_terms(k, v, b, b_last)
    outs = []
    for h in range(H_A):
        rows = slice(c * h, c * (h + 1))
        v_h = v[:, DV_A * h:DV_A * (h + 1)]
        outs.append(_mm(scores[rows], v_h) + inter[rows])
        s_ref[rows, :] = state[rows] * carry[rows] + _mm(k_rest[rows], v_h)
    return jnp.concatenate(outs, axis=1)


def _gla_local(items, hm):
    c = GLA_CHUNK
    bs = [_chunk_log_decay(la, t_ref) for _, _, _, la, t_ref, _, _ in items]
    b_lasts = [b[(0 if it[6] else c - 1):(0 if it[6] else c - 1) + 1, :] for b, it in zip(bs, items)]
    q_decayed = [_stack_heads(it[0] * jnp.exp(b), hm) for it, b in zip(items, bs)]
    k_grown = [(it[1] * jnp.exp(-b)).astype(BF16) for it, b in zip(items, bs)]
    raw = [_nt(qd, kg) for qd, kg in zip(q_decayed, k_grown)]
    masked = [r * it[5] for r, it in zip(raw, items)]
    terms = [_state_terms(it[1], it[2], b, bl) for it, b, bl in zip(items, bs, b_lasts)]
    out = []
    for n in range(0, len(items), 2):
        v = items[n][2]
        heads = [(slice(c * h, c * (h + 1)), v[:, DV_A * h:DV_A * (h + 1)]) for h in range(H_A)]
        both = (masked[n] + masked[n + 1]).astype(BF16)
        intra = jnp.concatenate([_mm(both[rows], v_h) for rows, v_h in heads], axis=1)
        for m in (n, n + 1):
            k_rest, carry = terms[m]
            incr = jnp.concatenate([_mm(k_rest[rows], v_h) for rows, v_h in heads], axis=0)
            out.append((intra if m == n else None, q_decayed[m], incr, carry))
    return out


def _gla_kernel(*refs, sample, has_prev=False, slot=0):
    if sample:
        (x_ref, z_ref, wf_ref, bf_ref, wb_ref, bb_ref, tf_ref, tb_ref, mf_ref, mb_ref, hm_ref, gn_ref, sf0_ref, sb0_ref,
         o_ref, la_f, la_b, o_f, o_b, s_f, s_b, qd_f, qd_b, ds_f, ds_b, cr_f, cr_b) = refs
    else:
        x_ref, z_ref, wf_ref, bf_ref, wb_ref, bb_ref, tf_ref, tb_ref, mf_ref, mb_ref, hm_ref, gn_ref = refs[:12]
        (o_ref, sf_out, sb_out, la_f, la_b, o_f, o_b, s_f, s_b,
         qd_f, qd_b, ds_f, ds_b, cr_f, cr_b) = refs[12 + (2 if has_prev else 0):]
    n_tok = x_ref.shape[0]
    n_chunks = n_tok // GLA_CHUNK
    hk, hv = H_A * DK_A, H_A * DV_A
    zb = z_ref[...].astype(BF16)

    def log_sigmoid(t):
        return jnp.minimum(t, 0.0) - jnp.log(1.0 + jnp.exp(-jnp.abs(t)))

    la_f[...] = log_sigmoid(_mm(zb, wf_ref[...].astype(BF16)) + bf_ref[...]) / GLA_TAU
    la_b[...] = log_sigmoid(_mm(zb, wb_ref[...].astype(BF16)) + bb_ref[...]) / GLA_TAU
    if sample:
        s_f[...] = sf0_ref[...]
        s_b[...] = sb0_ref[...]
    else:
        s_f[...] = jnp.zeros_like(s_f)
        s_b[...] = jnp.zeros_like(s_b)
    hm = hm_ref[...]

    fwd = (la_f, tf_ref, mf_ref, s_f, o_f, qd_f, ds_f, cr_f, False)
    bwd = (la_b, tb_ref, mb_ref, s_b, o_b, qd_b, ds_b, cr_b, True)
    tri_f = jnp.sum(mf_ref[...], axis=0)
    tri_b = jnp.sum(mb_ref[...], axis=0)

    def chunk_rows(ci, backward):
        cidx = n_chunks - 1 - ci if backward else ci
        return cidx, pl.ds(pl.multiple_of(cidx * GLA_CHUNK, GLA_CHUNK), GLA_CHUNK)

    def load_qkv(rows):
        q = x_ref[rows, 0:hk].astype(F32) * (DK_A ** -0.5)
        return q, x_ref[rows, hk:2 * hk].astype(F32), x_ref[rows, 2 * hk:2 * hk + hv]

    def safe_step(ci, carry):
        for la_ref, t_ref, m_ref, s_ref, out_ref, _, _, _, backward in (fwd, bwd):
            _, rows = chunk_rows(ci, backward)
            out_ref[rows, :] = _gla_chunk(*load_qkv(rows), la_ref[rows, :], t_ref, m_ref, hm, s_ref, backward)
        return carry

    def local_step(gi, carry):
        items, dests = [], []
        for u in range(GLA_GROUP):
            cidx = gi * GLA_GROUP + u
            rows = pl.ds(pl.multiple_of(cidx * GLA_CHUNK, GLA_CHUNK), GLA_CHUNK)
            qkv = load_qkv(rows)
            for (la_ref, t_ref, _, _, out_ref, qd_ref, ds_ref, cr_ref, backward), tri in ((fwd, tri_f), (bwd, tri_b)):
                items.append((*qkv, la_ref[rows, :], t_ref, tri, backward))
                dests.append((out_ref, rows, qd_ref, ds_ref, cr_ref, cidx))
        for (out_ref, rows, qd_ref, ds_ref, cr_ref, cidx), (intra, qd, incr, factor) in zip(dests, _gla_local(items, hm)):
            out_ref[rows, :] = jnp.zeros((GLA_CHUNK, hv), F32) if intra is None else intra
            qd_ref[cidx] = qd
            ds_ref[cidx] = incr
            cr_ref[cidx] = factor
        return carry

    def scan_step(ci, carry):
        for _, _, _, s_ref, out_ref, qd_ref, ds_ref, cr_ref, backward in (fwd, bwd):
            cidx, rows = chunk_rows(ci, backward)
            state = s_ref[...]
            inter = _mm(qd_ref[cidx], state.astype(BF16))
            out_ref[rows, :] += jnp.concatenate(
                [inter[GLA_CHUNK * h:GLA_CHUNK * (h + 1)] for h in range(H_A)], axis=1)
            s_ref[...] = state * cr_ref[cidx] + ds_ref[cidx]
        return carry

    chunk_sums = [jnp.sum(ref[...].reshape(n_chunks, GLA_CHUNK, hk), axis=1) for ref in (la_f, la_b)]
    mild = jnp.minimum(jnp.min(chunk_sums[0]), jnp.min(chunk_sums[1])) > -GLA_SAFE_DECAY

    @pl.when(mild)
    def _():
        lax.fori_loop(0, n_chunks // GLA_GROUP, local_step, 0, unroll=2)
        lax.fori_loop(0, n_chunks, scan_step, 0, unroll=2)

    @pl.when(jnp.logical_not(mild))
    def _():
        lax.fori_loop(0, n_chunks, safe_step, 0)
    if not sample:
        _store_slot(sf_out, slot, not has_prev, s_f[...])
        _store_slot(sb_out, slot, not has_prev, s_b[...])
    gain = gn_ref[...]
    for h in range(H_A):
        cols = slice(DV_A * h, DV_A * (h + 1))
        r = x_ref[:, 2 * hk + hv + DV_A * h:2 * hk + hv + DV_A * (h + 1)].astype(F32)
        o_ref[:, cols] = (_rms(o_f[:, cols] + o_b[:, cols]) * gain * (r * jax.nn.sigmoid(r))).astype(o_ref.dtype)


def _gla(proj, w_gf, b_gf, w_gb, b_gb, g_norm, consts, n_batch, seq, ctx=None, slot=0, prev=None):
    sample = ctx is not None
    hk, hv = H_A * DK_A, H_A * DV_A
    n_ch = seq // GLA_CHUNK
    full = lambda shape: pl.BlockSpec(shape, lambda b: (0,) * len(shape))
    layered = lambda shape: pl.BlockSpec((None,) + shape, lambda b: (slot,) + (0,) * len(shape))
    in_specs = [
        pl.BlockSpec((seq, 2 * hk + 2 * hv), lambda b: (b, 0)),
        pl.BlockSpec((seq, LANES), lambda b: (b, EVEN_W // LANES - 1)),
        layered((LANES, hk)), layered((1, hk)), layered((LANES, hk)), layered((1, hk)),
        full(consts[0].shape), full(consts[1].shape), full(consts[2].shape), full(consts[3].shape), full(consts[4].shape),
        layered((1, DV_A)),
    ]
    args = [proj, proj, w_gf, b_gf, w_gb, b_gb, *consts, g_norm]
    o_spec = pl.BlockSpec((seq, hv), lambda b: (b, 0))
    o_shape = jax.ShapeDtypeStruct((n_batch * seq, hv), BF16)
    aliases = {}
    if sample:
        st_spec = pl.BlockSpec((None, None, hk, DV_A), lambda b: (b, slot, 0, 0))
        in_specs += [st_spec, st_spec]
        args += [ctx[0], ctx[1]]
        out_specs, out_shape = o_spec, o_shape
    else:
        if prev is not None:
            in_specs += [pl.BlockSpec(memory_space=pl.ANY)] * 2
            aliases = {len(args): 1, len(args) + 1: 2}
            args += list(prev)
        st_spec, st_shape = _stack_out(n_batch, (DEPTH + 1) // 2, (hk, DV_A), slot, prev is None)
        out_specs, out_shape = [o_spec, st_spec, st_spec], [o_shape, st_shape, st_shape]
    return pl.pallas_call(
        functools.partial(_gla_kernel, sample=sample, has_prev=prev is not None, slot=slot),
        grid=(n_batch,),
        in_specs=in_specs,
        out_specs=out_specs,
        out_shape=out_shape,
        input_output_aliases=aliases,
        scratch_shapes=[pltpu.VMEM((seq, hk), F32), pltpu.VMEM((seq, hk), F32),
                        pltpu.VMEM((seq, hv), F32), pltpu.VMEM((seq, hv), F32),
                        pltpu.VMEM((hk, DV_A), F32), pltpu.VMEM((hk, DV_A), F32),
                        pltpu.VMEM((n_ch, H_A * GLA_CHUNK, hk), BF16), pltpu.VMEM((n_ch, H_A * GLA_CHUNK, hk), BF16),
                        pltpu.VMEM((n_ch, hk, DV_A), F32), pltpu.VMEM((n_ch, hk, DV_A), F32),
                        pltpu.VMEM((n_ch, hk, DV_A), F32), pltpu.VMEM((n_ch, hk, DV_A), F32)],
        compiler_params=_params(1),
        name="gla_sample" if sample else "gla_prompt",
    )(*args)


def _axial_rope(n_tokens, dim):
    rows = n_tokens // GRID_W
    row = np.repeat(np.arange(rows), GRID_W).astype(np.float64)
    col = np.tile(np.arange(GRID_W), rows).astype(np.float64)
    n_freq = dim // 4
    inv = ROPE_THETA ** (-np.arange(n_freq) / n_freq)
    ang = np.concatenate([row[:, None] * inv, col[:, None] * inv], axis=-1)
    return np.cos(ang).astype(np.float32), np.sin(ang).astype(np.float32)


def _filter_features(n_tokens):
    t = np.linspace(0.0, 1.0, n_tokens)[:, None]
    w = 2.0 * np.pi * np.arange(n_tokens)[:, None] / n_tokens
    f = np.linspace(1e-4, FILT_BANDS - 1, FILT_BANDS)[None, :]
    z = np.concatenate([t, np.cos(f * w), -np.sin(f * w)], axis=-1)
    z = np.pad(z, ((0, 0), (0, LANES - FILT_EMB)))
    return jnp.asarray(z, F32), jnp.asarray(t, F32)


_QB_ZERO = H_D * (NOPE_D + ROPE_D)
_QB_PERM = np.array([(NOPE_D + ROPE_D) * (p // MLA_QW) + p % MLA_QW if p % MLA_QW < NOPE_D + ROPE_D else _QB_ZERO
                     for p in range(H_D * MLA_QW)])


def _mla_rope_tables(cos_d, sin_d):
    n, half = cos_d.shape
    zeros = np.zeros((n, half), np.float32)

    def lanes(pre, width):
        pad = np.zeros((n, width - pre.shape[1] - 2 * half), np.float32)
        build = lambda first, second, lead: np.concatenate([lead, first, second, pad], axis=1)
        return (build(cos_d, cos_d, pre), build(-sin_d, zeros, 0 * pre), build(zeros, sin_d, 0 * pre))

    q_tabs = lanes(np.ones((n, NOPE_D), np.float32), MLA_QW)
    k_tabs = lanes(np.zeros((n, 0), np.float32), LANES)
    return tuple(jnp.asarray(t) for t in q_tabs + k_tabs)

EVEN_ROW_GROUPS = ((0, 0, 1536), (1568, 1536, 1024), (1536, EVEN_W - 2 * GATE_RANK, 2 * GATE_RANK))
ODD_ROW_GROUPS = ((0, 0, 1984),)
EVEN_KEEP = (2304, 256)
ODD_KEEP = (1920, LANES)


def kernel(x_prompt, x_sample, state_gla_fwd, state_gla_bwd, cache_gqa_k, cache_gqa_v, cache_mla_ckv, cache_mla_kpe, c, c_ctx, w_mod, b_mod, w_in_even, w_gla_gate_f, b_gla_gate_f, w_gla_gate_b, b_gla_gate_b, g_gla_norm, g_gqa_q, g_gqa_k, w_out_even, w_in_odd, w_hy_conv, b_hy_conv, hy_skip, w_filt1, b_filt1, filt_freq, w_filt2, b_filt2, w_filt3, g_mla_q, w_mla_qb, g_mla_kv, w_mla_kvb, w_out_odd, w_ffn_in, w_ffn_out, g_final):
    cvec = jnp.concatenate([c_ctx[None, :], c, jnp.zeros((8 - 1 - DEC_BATCH, D_MODEL), F32)], axis=0)
    mod = _modulation(cvec, w_mod, b_mod)
    xc, xs, s_row0 = x_prompt.reshape(N_PROMPT, D_MODEL), x_sample.reshape(N_SAMPLE, D_MODEL), 0

    gla_consts = _gla_constants()
    cos_b, sin_b = _axial_rope(DEC_SEQ, HD_B)
    rope_b = (jnp.asarray(np.concatenate([cos_b, cos_b], axis=1)), jnp.asarray(np.concatenate([-sin_b, sin_b], axis=1)))
    cos_d, sin_d = _axial_rope(DEC_SEQ, ROPE_D)
    rope_d = _mla_rope_tables(cos_d, sin_d)
    w_qb_all = jnp.pad(w_mla_qb, ((0, 0), (0, 0), (0, 1)))[:, :, _QB_PERM]
    tabs_c, tabs_s = _dft_tables(SEQ), _dft_tables(DEC_SEQ)
    z_c, t_c = _filter_features(SEQ)
    z_s, t_s = _filter_features(DEC_SEQ)
    deltas = jnp.asarray(np.abs(np.linspace(HY_MIN_DECAY, HY_MAX_DECAY, HY_W))[None, :], F32)

    wt_even = jnp.swapaxes(w_in_even, 1, 2)
    wt_odd = jnp.swapaxes(w_in_odd, 1, 2)

    filt_args = (jnp.pad(w_filt1, ((0, 0), (0, LANES - FILT_EMB), (0, 0))), b_filt1, filt_freq, w_filt2, b_filt2, w_filt3)
    g_c = _filter_spectrum(z_c, *filt_args, t_c, deltas, tabs_c)
    g_s = _filter_spectrum(z_s, *filt_args, t_s, deltas, tabs_s)
    b_conv = b_hy_conv.reshape(DEPTH // 2, 1, 3 * HY_W)
    skip = hy_skip.reshape(DEPTH // 2, 1, HY_W)

    n_even, n_odd = (DEPTH + 1) // 2, DEPTH // 2
    hk = H_A * DK_A
    z0 = LANES - 2 * GATE_RANK
    pad_f = jnp.zeros((n_even, LANES, hk), F32).at[:, z0:z0 + GATE_RANK].set(w_gla_gate_f)
    pad_b = jnp.zeros((n_even, LANES, hk), F32).at[:, z0 + GATE_RANK:].set(w_gla_gate_b)
    gate_args = (pad_f, b_gla_gate_f.reshape(n_even, 1, hk), pad_b, b_gla_gate_b.reshape(n_even, 1, hk),
                 g_gla_norm.reshape(n_even, 1, DV_A), gla_consts)
    ctx_a = (state_gla_fwd.reshape(DEC_BATCH, n_even, hk, DV_A), state_gla_bwd.reshape(DEC_BATCH, n_even, hk, DV_A))
    gq_b, gk_b = g_gqa_q.reshape(n_even, 1, HD_B), g_gqa_k.reshape(n_even, 1, HD_B)
    mla_args = (g_mla_q.reshape(n_odd, 1, Q_RANK), w_qb_all, g_mla_kv.reshape(n_odd, 1, KV_RANK), w_mla_kvb)

    new_states = new_kv = new_latent = None
    for i in range(DEPTH):
        j = i // 2
        if i % 2 == 0:
            pc, v_new, ps = _in_proj(xc, xs, s_row0, mod, i, wt_even, j, EVEN_ROW_GROUPS, EVEN_W, EVEN_KEEP)
            a_c, *new_states = _gla(pc, *gate_args, BATCH, SEQ, slot=j, prev=new_states)
            a_s = _gla(ps, *gate_args, DEC_BATCH, DEC_SEQ, ctx=ctx_a, slot=j)
            b_c, *new_kv = _gqa(pc, gq_b, gk_b, BATCH, SEQ, v_f32=v_new, slot=j, prev=new_kv)
            b_s = _gqa(ps, gq_b, gk_b, DEC_BATCH, DEC_SEQ, ctx=(cache_gqa_k, cache_gqa_v), rope=rope_b, slot=j)
            w_out = w_out_even
        else:
            pc, kpe_new, ps = _in_proj(xc, xs, s_row0, mod, i, wt_odd, j, ODD_ROW_GROUPS, ODD_W, ODD_KEEP)
            a_c = _hyena(pc, j, w_hy_conv, b_conv, skip, g_c[0], g_c[1], tabs_c, BATCH, SEQ)
            a_s = _hyena(ps, j, w_hy_conv, b_conv, skip, g_s[0], g_s[1], tabs_s, DEC_BATCH, DEC_SEQ)
            b_c, *new_latent = _mla(pc, *mla_args, BATCH, SEQ, kpe_f32=kpe_new, slot=j, prev=new_latent)
            b_s = _mla(ps, *mla_args, DEC_BATCH, DEC_SEQ, ctx=(cache_mla_ckv, cache_mla_kpe), rope=rope_d, slot=j)
            w_out = w_out_odd
        x_mid = _out_proj([a_c, b_c], [a_s, b_s], w_out, j, xc, xs, s_row0, mod, i, 2)
        if i < DEPTH - 1:
            x_all = _ffn(x_mid, 0, N_PROMPT + N_SAMPLE, mod, i, w_ffn_in, w_ffn_out)
            xc, xs, s_row0 = x_all, x_all, N_PROMPT
        else:
            xc = _ffn(x_mid, 0, N_PROMPT, mod, i, w_ffn_in, w_ffn_out, final_gain=g_final)
            xs = _ffn(x_mid, N_PROMPT, N_SAMPLE, mod, i, w_ffn_in, w_ffn_out, final_gain=g_final)
    y_prompt = xc.reshape(BATCH, SEQ, D_MODEL)
    y_sample = xs.reshape(DEC_BATCH, DEC_SEQ, D_MODEL)
    state_shape = (BATCH, (DEPTH + 1) // 2, H_A, DK_A, DV_A)
    return (y_prompt, y_sample, new_states[0].reshape(state_shape), new_states[1].reshape(state_shape),
            new_kv[0], new_kv[1], new_latent[0], new_latent[1])
```

```python
import functools
import math

import numpy as np
import jax
import jax.numpy as jnp
from jax import lax
from jax.experimental import pallas as pl
from jax.experimental.pallas import tpu as pltpu

F32 = jnp.float32
BF16 = jnp.bfloat16

D_MODEL = 1024
BATCH, SEQ = 16, 256
DEC_BATCH, DEC_SEQ = 2, 1024
DEPTH = 4
PAST_LEN = 512
GRID_W = 64
HALF_W = D_MODEL // 2
H_A, DV_A, DK_A = 4, 128, 64
GATE_RANK = 16
GLA_TAU = 16.0
GLA_CHUNK = 64
HD_B, H_B, KV_B = 128, 4, 2
HY_W = HALF_W
FILT_EMB, FILT_HID = 33, 64
FILT_BANDS = (FILT_EMB - 1) // 2
HY_MIN_DECAY = math.log(1e-2) / 1.5
HY_MAX_DECAY = math.log(1e-2) / 0.3
H_D, V_D, NOPE_D, ROPE_D = 4, 128, 128, 64
Q_RANK, KV_RANK = 256, 128
FFN_H = 2816
ROPE_THETA = 10000.0
EPS = 1e-6

LANES = 128
VMEM_LIMIT = 56 * 1024 * 1024

MOD_ROWS = 1024
TM = 1024
TM_IN = 512
TM_FFN = 2048
EVEN_W = 2688
ODD_W = 2048
FFN_TN = 256
QB = 256


def _params(n_grid):
    return pltpu.CompilerParams(dimension_semantics=("arbitrary",) * n_grid, vmem_limit_bytes=VMEM_LIMIT)


def _nt(a, b):
    return lax.dot_general(a, b, (((1,), (1,)), ((), ())), preferred_element_type=F32)


def _mm(a, b):
    return jnp.dot(a, b, preferred_element_type=F32)


def _rms(x):
    return x * lax.rsqrt(jnp.mean(x * x, axis=-1, keepdims=True) + EPS)


MOD_GROUP = 3


def _mod_kernel(c_ref, w_ref, b_ref, o_ref):
    cv = c_ref[...]
    s = cv * jax.nn.sigmoid(cv)
    m = _mm(s.astype(BF16), w_ref[...].astype(BF16)) + b_ref[...]
    for k in range(MOD_GROUP):
        o_ref[k] = m[:, D_MODEL * k:D_MODEL * (k + 1)]


def _modulation(cvec, w_mod, b_mod):
    return pl.pallas_call(
        _mod_kernel,
        grid=(DEPTH, 6 // MOD_GROUP),
        in_specs=[
            pl.BlockSpec((8, D_MODEL), lambda l, n: (0, 0)),
            pl.BlockSpec((None, D_MODEL, MOD_GROUP * D_MODEL), lambda l, n: (l, 0, n)),
            pl.BlockSpec((None, 1, MOD_GROUP * D_MODEL), lambda l, n: (l, 0, n)),
        ],
        out_specs=pl.BlockSpec((None, MOD_GROUP, 8, D_MODEL), lambda l, n: (l, n, 0, 0)),
        out_shape=jax.ShapeDtypeStruct((DEPTH, 6, 8, D_MODEL), F32),
        compiler_params=_params(2),
        name="adaln_mod",
    )(cvec, w_mod, b_mod.reshape(DEPTH, 1, 6 * D_MODEL))


N_PROMPT = BATCH * SEQ
N_SAMPLE = DEC_BATCH * DEC_SEQ


def _stack_out(n_batch, n_slots, tail, slot, first):
    zeros = (0,) * len(tail)
    if first:
        spec = pl.BlockSpec((None, n_slots) + tail, lambda b, *_: (b, 0) + zeros)
    else:
        spec = pl.BlockSpec((None, None) + tail, lambda b, *_: (b, slot) + zeros)
    return spec, jax.ShapeDtypeStruct((n_batch, n_slots) + tail, F32)


def _store_slot(ref, slot, owns_stack, value):
    if not owns_stack:
        ref[...] = value
        return
    for s in range(ref.shape[0]):
        ref[s] = value if s == slot else jnp.zeros_like(value)


def _stream_index_maps(tile_rows, s_row0):
    n_c = N_PROMPT // tile_rows
    return (lambda i: (jnp.minimum(i, n_c - 1), 0)), (lambda i: (s_row0 // tile_rows + jnp.maximum(i - n_c, 0), 0))


def _in_proj_kernel(xc_ref, xs_ref, sh_ref, sc_ref, wt_ref, oc_ref, keep_ref, os_ref, wb_ref, *, row_groups, keep, n_c):
    i = pl.program_id(0)

    @pl.when(i == 0)
    def _():
        wb_ref[...] = jnp.zeros_like(wb_ref)
        for src, dst, size in row_groups:
            wb_ref[dst:dst + size, :] = wt_ref[src:src + size, :].astype(BF16)

    def project(x_ref, g):
        h = (_rms(x_ref[...]) * (1.0 + sc_ref[pl.ds(g, 1), :]) + sh_ref[pl.ds(g, 1), :]).astype(BF16)
        return _nt(h, wb_ref[...])

    @pl.when(i < n_c)
    def _():
        y = project(xc_ref, 0)
        oc_ref[...] = y.astype(oc_ref.dtype)
        keep_ref[...] = y[:, keep[0]:keep[0] + keep[1]]

    @pl.when(i >= n_c)
    def _():
        os_ref[...] = project(xs_ref, 1 + (i - n_c) // (MOD_ROWS // TM_IN)).astype(os_ref.dtype)


def _in_proj(xc, xs, s_row0, mod, layer, wt, w_layer, row_groups, n, keep):
    n_c, n_s = N_PROMPT // TM_IN, N_SAMPLE // TM_IN
    xc_idx, xs_idx = _stream_index_maps(TM_IN, s_row0)
    c_idx, s_idx = _stream_index_maps(TM_IN, 0)
    return pl.pallas_call(
        functools.partial(_in_proj_kernel, row_groups=row_groups, keep=keep, n_c=n_c),
        grid=(n_c + n_s,),
        in_specs=[pl.BlockSpec((TM_IN, D_MODEL), xc_idx), pl.BlockSpec((TM_IN, D_MODEL), xs_idx),
                  pl.BlockSpec((None, None, 8, D_MODEL), lambda i: (layer, 0, 0, 0)),
                  pl.BlockSpec((None, None, 8, D_MODEL), lambda i: (layer, 1, 0, 0)),
                  pl.BlockSpec((None, wt.shape[1], D_MODEL), lambda i: (w_layer, 0, 0), pipeline_mode=pl.Buffered(1))],
        out_specs=[pl.BlockSpec((TM_IN, n), c_idx), pl.BlockSpec((TM_IN, keep[1]), c_idx), pl.BlockSpec((TM_IN, n), s_idx)],
        out_shape=[jax.ShapeDtypeStruct((N_PROMPT, n), BF16), jax.ShapeDtypeStruct((N_PROMPT, keep[1]), F32),
                   jax.ShapeDtypeStruct((N_SAMPLE, n), BF16)],
        scratch_shapes=[pltpu.VMEM((n, D_MODEL), BF16)],
        compiler_params=_params(1),
        name="norm_mod_proj",
    )(xc, xs, mod, mod, wt)


def _ffn_kernel(x_ref, sh_ref, sc_ref, gate_ref, wg_ref, wu_ref, wd_ref, *refs, first_sub, final):
    (gf_ref, o_ref, h_ref) = refs if final else (None,) + refs
    n_sub = x_ref.shape[0] // MOD_ROWS
    subs = [(slice(s * MOD_ROWS, (s + 1) * MOD_ROWS),
             jnp.maximum(first_sub + pl.program_id(0) * n_sub + s - (N_PROMPT // MOD_ROWS - 1), 0)) for s in range(n_sub)]

    @pl.when(pl.program_id(1) == 0)
    def _():
        for rows, g in subs:
            x = x_ref[rows, :]
            o_ref[rows, :] = x
            h_ref[rows, :] = (_rms(x) * (1.0 + sc_ref[pl.ds(g, 1), :]) + sh_ref[pl.ds(g, 1), :]).astype(BF16)

    wg = wg_ref[...].astype(BF16)
    wu = wu_ref[...].astype(BF16)
    wd = wd_ref[...].astype(BF16)
    for rows, g in subs:
        h = h_ref[rows, :]
        a = _mm(h, wg)
        act = (a * jax.nn.sigmoid(a) * _mm(h, wu)).astype(BF16)
        o_ref[rows, :] += gate_ref[pl.ds(g, 1), :] * _mm(act, wd)

    if final:
        @pl.when(pl.program_id(1) == pl.num_programs(1) - 1)
        def _():
            for rows, _ in subs:
                o_ref[rows, :] = _rms(o_ref[rows, :]) * gf_ref[...]


def _ffn(x, row0, m, mod, layer, w_in, w_out, final_gain=None):
    nj = FFN_H // FFN_TN
    tile0 = row0 // TM_FFN
    mod_spec = lambda k: pl.BlockSpec((None, None, 8, D_MODEL), lambda i, j: (layer, k, 0, 0))
    final = final_gain is not None
    extra_specs = [pl.BlockSpec((1, D_MODEL), lambda i, j: (0, 0))] if final else []
    extra_args = [final_gain.reshape(1, D_MODEL)] if final else []
    return pl.pallas_call(
        functools.partial(_ffn_kernel, first_sub=row0 // MOD_ROWS, final=final),
        grid=(m // TM_FFN, nj),
        in_specs=[pl.BlockSpec((TM_FFN, D_MODEL), lambda i, j: (tile0 + i, 0)), mod_spec(3), mod_spec(4), mod_spec(5),
                  pl.BlockSpec((None, D_MODEL, FFN_TN), lambda i, j: (layer, 0, j)),
                  pl.BlockSpec((None, D_MODEL, FFN_TN), lambda i, j: (layer, 0, j + nj)),
                  pl.BlockSpec((None, FFN_TN, D_MODEL), lambda i, j: (layer, j, 0))] + extra_specs,
        out_specs=pl.BlockSpec((TM_FFN, D_MODEL), lambda i, j: (i, 0)),
        out_shape=jax.ShapeDtypeStruct((m, D_MODEL), F32),
        scratch_shapes=[pltpu.VMEM((TM_FFN, D_MODEL), BF16)],
        compiler_params=_params(2),
        name="ffn_residual",
    )(x, mod, mod, mod, w_in, w_in, w_out, *extra_args)


def _proj_res_kernel(ac0_ref, ac1_ref, as0_ref, as1_ref, w0_ref, w1_ref, xc_ref, xs_ref, gate_ref, o_ref, *, n_c):
    i = pl.program_id(0)

    def mix(a0_ref, a1_ref, x_ref, g):
        acc = _mm(a0_ref[...], w0_ref[...].astype(BF16)) + _mm(a1_ref[...], w1_ref[...].astype(BF16))
        return x_ref[...] + gate_ref[pl.ds(g, 1), :] * acc

    @pl.when(i < n_c)
    def _():
        o_ref[...] = mix(ac0_ref, ac1_ref, xc_ref, 0)

    @pl.when(i >= n_c)
    def _():
        o_ref[...] = mix(as0_ref, as1_ref, xs_ref, 1 + (i - n_c))


def _out_proj(acts_c, acts_s, w, w_layer, xc, xs, s_row0, mod, layer, k_gate):
    n_c, n_s = N_PROMPT // TM, N_SAMPLE // TM
    kw = acts_c[0].shape[1]
    xc_idx, xs_idx = _stream_index_maps(TM, s_row0)
    c_idx, s_idx = _stream_index_maps(TM, 0)
    w_specs = [pl.BlockSpec((None, kw, D_MODEL), functools.partial(lambda i, p: (w_layer, p, 0), p=p),
                            pipeline_mode=pl.Buffered(1)) for p in range(2)]
    return pl.pallas_call(
        functools.partial(_proj_res_kernel, n_c=n_c),
        grid=(n_c + n_s,),
        in_specs=[pl.BlockSpec((TM, kw), c_idx)] * 2 + [pl.BlockSpec((TM, kw), s_idx)] * 2 + w_specs + [
            pl.BlockSpec((TM, D_MODEL), xc_idx), pl.BlockSpec((TM, D_MODEL), xs_idx),
            pl.BlockSpec((None, None, 8, D_MODEL), lambda i: (layer, k_gate, 0, 0)),
        ],
        out_specs=pl.BlockSpec((TM, D_MODEL), lambda i: (i, 0)),
        out_shape=jax.ShapeDtypeStruct((N_PROMPT + N_SAMPLE, D_MODEL), F32),
        compiler_params=_params(1),
        name="out_proj_residual",
    )(*acts_c, *acts_s, w, w, xc, xs, mod)


def _gqa_kernel(*refs, sample, has_prev=False, slot=0):
    if sample:
        q_ref, k_ref, v_ref, gq_ref, gk_ref, ck_ref, cv_ref, cos_ref, sin_ref, o_ref, kb_ref, vb_ref = refs
    else:
        n_in = 8 if has_prev else 6
        q_ref, k_ref, v_ref, gq_ref, gk_ref, vf_ref = refs[:6]
        o_ref, kc_ref, vc_ref, kb_ref, vb_ref = refs[n_in:]
        if not has_prev:
            stacks = (kc_ref, vc_ref)
            kc_ref, vc_ref = kc_ref.at[slot], vc_ref.at[slot]
    qi = pl.program_id(1)
    n_new = k_ref.shape[0]
    past = PAST_LEN if sample else 0
    rep = H_B // KV_B

    @pl.when(qi == 0)
    def _():
        if not (sample or has_prev):
            for ref in stacks:
                for other in range(ref.shape[0]):
                    if other != slot:
                        ref[other] = jnp.zeros(ref.shape[1:], ref.dtype)
        for g in range(KV_B):
            sl = slice(HD_B * g, HD_B * (g + 1))
            kn = _rms(k_ref[:, sl].astype(F32)) * gk_ref[...]
            if sample:
                kb_ref[0:past, sl] = ck_ref[:, g, :].astype(BF16)
                vb_ref[g, 0:past, 0:HD_B] = cv_ref[:, g, :].astype(BF16)
                kn = kn * cos_ref[...] + pltpu.roll(kn, HD_B // 2, 1) * sin_ref[...]
            else:
                kc_ref[:, g, :] = kn
                vc_ref[:, g, :] = vf_ref[:, sl]
            kb_ref[past:past + n_new, sl] = kn.astype(BF16)
            vb_ref[g, past:past + n_new, 0:HD_B] = v_ref[:, sl].astype(BF16)
            vb_ref[g, :, HD_B:] = jnp.ones((past + n_new, HD_B), BF16)

    r0 = pl.multiple_of(qi * QB, QB)
    qs = []
    for h in range(H_B):
        qn = _rms(q_ref[:, HD_B * h:HD_B * (h + 1)].astype(F32)) * gq_ref[...]
        if sample:
            qn = qn * cos_ref[pl.ds(r0, QB), :] + pltpu.roll(qn, HD_B // 2, 1) * sin_ref[pl.ds(r0, QB), :]
        qs.append((qn * (HD_B ** -0.5)).astype(BF16))
    scores = [_nt(qs[h], kb_ref[:, HD_B * (h // rep):HD_B * (h // rep + 1)]) for h in range(H_B)]
    weights = [jnp.exp(s - jnp.max(s, axis=-1, keepdims=True)).astype(BF16) for s in scores]
    sums = [_mm(weights[h], vb_ref[h // rep]) for h in range(H_B)]
    for h in range(H_B):
        o_ref[:, HD_B * h:HD_B * (h + 1)] = (sums[h][:, :HD_B] / sums[h][:, HD_B:]).astype(o_ref.dtype)


def _gqa(proj, g_q, g_k, n_batch, seq, ctx=None, rope=None, v_f32=None, slot=0, prev=None):
    sample = ctx is not None
    m = n_batch * seq
    nq = seq // QB
    n_even = (DEPTH + 1) // 2
    in_specs = [
        pl.BlockSpec((QB, 512), lambda b, i: (b * nq + i, 3)),
        pl.BlockSpec((seq, 256), lambda b, i: (b, 8)),
        pl.BlockSpec((seq, 256), lambda b, i: (b, 9)),
        pl.BlockSpec((None, 1, HD_B), lambda b, i: (slot, 0, 0)),
        pl.BlockSpec((None, 1, HD_B), lambda b, i: (slot, 0, 0)),
    ]
    args = [proj, proj, proj, g_q, g_k]
    o_spec = pl.BlockSpec((QB, 512), lambda b, i: (b * nq + i, 0))
    o_shape = jax.ShapeDtypeStruct((m, 512), BF16)
    aliases = {}
    if sample:
        cache_spec = pl.BlockSpec((None, None, PAST_LEN, KV_B, HD_B), lambda b, i: (b, slot, 0, 0, 0))
        in_specs += [
            cache_spec, cache_spec,
            pl.BlockSpec((seq, HD_B), lambda b, i: (0, 0)),
            pl.BlockSpec((seq, HD_B), lambda b, i: (0, 0)),
        ]
        args += [ctx[0], ctx[1], rope[0], rope[1]]
        out_specs, out_shape = o_spec, o_shape
    else:
        in_specs.append(pl.BlockSpec((seq, KV_B * HD_B), lambda b, i: (b, 0)))
        args.append(v_f32)
        if prev is not None:
            in_specs += [pl.BlockSpec(memory_space=pl.ANY)] * 2
            aliases = {len(args): 1, len(args) + 1: 2}
            args += list(prev)
        if prev is None:
            new_spec = pl.BlockSpec((None, n_even, seq, KV_B, HD_B), lambda b, i: (b, 0, 0, 0, 0))
        else:
            new_spec = pl.BlockSpec((None, None, seq, KV_B, HD_B), lambda b, i: (b, slot, 0, 0, 0))
        new_shape = jax.ShapeDtypeStruct((n_batch, n_even, seq, KV_B, HD_B), F32)
        out_specs, out_shape = [o_spec, new_spec, new_spec], [o_shape, new_shape, new_shape]
    n_keys = seq + (PAST_LEN if sample else 0)
    return pl.pallas_call(
        functools.partial(_gqa_kernel, sample=sample, has_prev=prev is not None, slot=slot),
        grid=(n_batch, nq),
        in_specs=in_specs,
        out_specs=out_specs,
        out_shape=out_shape,
        input_output_aliases=aliases,
        scratch_shapes=[pltpu.VMEM((n_keys, KV_B * HD_B), BF16), pltpu.VMEM((KV_B, n_keys, 2 * HD_B), BF16)],
        compiler_params=_params(2),
        name="gqa_sample" if sample else "gqa_prompt",
    )(*args)


MLA_QW = 2 * LANES
MLA_HW = 4 * LANES


def _rotate_pairs(x, cos_t, sin_lo, sin_hi):
    w = x.shape[1]
    return x * cos_t + pltpu.roll(x, ROPE_D // 2, 1) * sin_hi + pltpu.roll(x, w - ROPE_D // 2, 1) * sin_lo


def _mla_kernel(*refs, sample, has_prev=False, slot=0):
    if sample:
        (cq_ref, ckv_ref, kpe_ref, gq_ref, wqb_ref, gkv_ref, wkvb_ref, cckv_ref, ckpe_ref,
         qc_ref, ql_ref, qh_ref, kc_ref, kl_ref, kh_ref, o_ref, kv_s) = refs
    else:
        cq_ref, ckv_ref, kpe_ref, gq_ref, wqb_ref, gkv_ref, wkvb_ref, kf_ref = refs[:8]
        o_ref, ckvn_ref, kpeo_ref, kv_s = refs[8 + (2 if has_prev else 0):]
    qi = pl.program_id(1)
    n_new = ckv_ref.shape[0]
    past = PAST_LEN if sample else 0

    def stage_kv(rows, kv, kpe_block):
        for h in range(H_D):
            kv_s[rows, MLA_HW * h:MLA_HW * h + NOPE_D] = kv[:, 256 * h:256 * h + NOPE_D].astype(BF16)
            kv_s[rows, MLA_HW * h + NOPE_D:MLA_HW * h + MLA_QW] = kpe_block
            kv_s[rows, MLA_HW * h + MLA_QW:MLA_HW * h + MLA_QW + V_D] = kv[:, 256 * h + NOPE_D:256 * (h + 1)].astype(BF16)

    @pl.when(qi == 0)
    def _():
        wkvb = wkvb_ref[...].astype(BF16)
        ckvn = _rms(ckv_ref[...].astype(F32)) * gkv_ref[...]
        if not sample:
            _store_slot(ckvn_ref, slot, not has_prev, ckvn)
            _store_slot(kpeo_ref, slot, not has_prev, kf_ref[:, 0:ROPE_D])
        kpe = kpe_ref[...]
        if sample:
            ctx_kpe = jnp.concatenate([ckpe_ref[...], jnp.zeros((past, LANES - ROPE_D), F32)], axis=1)
            stage_kv(slice(0, past), _mm(cckv_ref[...].astype(BF16), wkvb), ctx_kpe.astype(BF16))
            kpe = _rotate_pairs(kpe.astype(F32), kc_ref[...], kl_ref[...], kh_ref[...]).astype(BF16)
        stage_kv(slice(past, past + n_new), _mm(ckvn.astype(BF16), wkvb), kpe)
        for h in range(H_D):
            kv_s[:, MLA_HW * h + MLA_QW + V_D:MLA_HW * (h + 1)] = jnp.ones((past + n_new, V_D), BF16)

    q = _mm((_rms(cq_ref[...].astype(F32)) * gq_ref[...]).astype(BF16), wqb_ref[...].astype(BF16))
    q = q * ((NOPE_D + ROPE_D) ** -0.5)
    qs = []
    for h in range(H_D):
        q_h = q[:, MLA_QW * h:MLA_QW * (h + 1)]
        if sample:
            rows = pl.ds(pl.multiple_of(qi * QB, QB), QB)
            q_h = _rotate_pairs(q_h, qc_ref[rows, :], ql_ref[rows, :], qh_ref[rows, :])
        qs.append(q_h.astype(BF16))
    scores = [_nt(qs[h], kv_s[:, MLA_HW * h:MLA_HW * h + MLA_QW]) for h in range(H_D)]
    weights = [jnp.exp(s - jnp.max(s, axis=-1, keepdims=True)).astype(BF16) for s in scores]
    sums = [_mm(weights[h], kv_s[:, MLA_HW * h + MLA_QW:MLA_HW * (h + 1)]) for h in range(H_D)]
    for h in range(H_D):
        o_ref[:, V_D * h:V_D * (h + 1)] = (sums[h][:, :V_D] / sums[h][:, V_D:]).astype(o_ref.dtype)


def _mla(proj, g_q, w_qb, g_kv, w_kvb, n_batch, seq, ctx=None, rope=None, kpe_f32=None, slot=0, prev=None):
    sample = ctx is not None
    m = n_batch * seq
    nq = seq // QB
    in_specs = [
        pl.BlockSpec((QB, Q_RANK), lambda b, i: (b * nq + i, 6)),
        pl.BlockSpec((seq, KV_RANK), lambda b, i: (b, 14)),
        pl.BlockSpec((seq, LANES), lambda b, i: (b, 15)),
        pl.BlockSpec((None, 1, Q_RANK), lambda b, i: (slot, 0, 0)),
        pl.BlockSpec((None, Q_RANK, H_D * MLA_QW), lambda b, i: (slot, 0, 0)),
        pl.BlockSpec((None, 1, KV_RANK), lambda b, i: (slot, 0, 0)),
        pl.BlockSpec((None, KV_RANK, 1024), lambda b, i: (slot, 0, 0)),
    ]
    args = [proj, proj, proj, g_q, w_qb, g_kv, w_kvb]
    o_spec = pl.BlockSpec((QB, 512), lambda b, i: (b * nq + i, 0))
    o_shape = jax.ShapeDtypeStruct((m, 512), BF16)
    aliases = {}
    if sample:
        in_specs += [
            pl.BlockSpec((None, None, PAST_LEN, KV_RANK), lambda b, i: (b, slot, 0, 0)),
            pl.BlockSpec((None, None, PAST_LEN, ROPE_D), lambda b, i: (b, slot, 0, 0)),
        ] + [pl.BlockSpec((seq, MLA_QW), lambda b, i: (0, 0))] * 3 + [pl.BlockSpec((seq, LANES), lambda b, i: (0, 0))] * 3
        args += [ctx[0], ctx[1], *rope]
        out_specs, out_shape = o_spec, o_shape
    else:
        in_specs.append(pl.BlockSpec((seq, LANES), lambda b, i: (b, 0)))
        args.append(kpe_f32)
        if prev is not None:
            in_specs += [pl.BlockSpec(memory_space=pl.ANY)] * 2
            aliases = {len(args): 1, len(args) + 1: 2}
            args += list(prev)
        ckv_spec, ckv_shape = _stack_out(n_batch, DEPTH // 2, (seq, KV_RANK), slot, prev is None)
        kpe_spec, kpe_shape = _stack_out(n_batch, DEPTH // 2, (seq, ROPE_D), slot, prev is None)
        out_specs, out_shape = [o_spec, ckv_spec, kpe_spec], [o_shape, ckv_shape, kpe_shape]
    n_keys = seq + (PAST_LEN if sample else 0)
    return pl.pallas_call(
        functools.partial(_mla_kernel, sample=sample, has_prev=prev is not None, slot=slot),
        grid=(n_batch, nq),
        in_specs=in_specs,
        out_specs=out_specs,
        out_shape=out_shape,
        input_output_aliases=aliases,
        scratch_shapes=[pltpu.VMEM((n_keys, H_D * MLA_HW), BF16)],
        compiler_params=_params(2),
        name="mla_sample" if sample else "mla_prompt",
    )(*args)


def _dft(table, x):
    return _mm(table.astype(BF16), x.astype(BF16))


def _filter_kernel(z_ref, wf1_ref, bf1_ref, fr_ref, wf2_ref, bf2_ref, wf3_ref, t_ref, dl_ref,
                   c_ref, s_ref, gre_ref, gim_ref):
    n_tok = z_ref.shape[0]
    fr = fr_ref[...]
    hid = jnp.sin(fr * (_mm(z_ref[...].astype(BF16), wf1_ref[...].astype(BF16)) + bf1_ref[...]))
    hid = jnp.sin(fr * (_mm(hid.astype(BF16), wf2_ref[...].astype(BF16)) + bf2_ref[...]))
    filt = _mm(hid.astype(BF16), wf3_ref[...].astype(BF16))
    decay = jnp.exp(-t_ref[...] * dl_ref[...])
    row = lax.broadcasted_iota(jnp.int32, (n_tok, 1), 0)
    h_f = filt[:, :HY_W] * decay
    h_b = jnp.where(row == 0, 0.0, filt[:, HY_W:] * decay)
    p, m = h_f + h_b, h_f - h_b
    g_re = _dft(c_ref[...], p)
    g_im = _dft(s_ref[...], m)
    sign = jnp.where(row % 2 == 0, 1.0, -1.0)
    nyquist = jnp.sum(p * sign, axis=0, keepdims=True)
    g_im = jnp.where(row == 0, nyquist, g_im)
    wk = jnp.where(row == 0, 0.5 / n_tok, 1.0 / n_tok)
    gre_ref[...] = g_re * wk
    gim_ref[...] = g_im * wk


def _filter_spectrum(z, wf1, bf1, freq, wf2, bf2, wf3, t_col, deltas, tabs):
    n_layers, n_tok = wf1.shape[0], z.shape[0]
    shared = lambda a: pl.BlockSpec(a.shape, lambda l: (0,) * a.ndim)
    per_layer = lambda a: pl.BlockSpec((None,) + a.shape[1:], lambda l: (l,) + (0,) * (a.ndim - 1))
    row = lambda a: a.reshape(n_layers, 1, FILT_HID)
    args = [z, wf1, row(bf1), row(freq), wf2, row(bf2), wf3, t_col, deltas, tabs[0], tabs[1]]
    layered = [False, True, True, True, True, True, True, False, False, False, False]
    out = jax.ShapeDtypeStruct((n_layers, n_tok, HY_W), F32)
    out_spec = pl.BlockSpec((None, n_tok, HY_W), lambda l: (l, 0, 0))
    return pl.pallas_call(
        _filter_kernel,
        grid=(n_layers,),
        in_specs=[per_layer(a) if lay else shared(a) for a, lay in zip(args, layered)],
        out_specs=[out_spec, out_spec],
        out_shape=[out, out],
        compiler_params=_params(1),
        name="hyena_filter",
    )(*args)


HY_ROWS = 1024


def _hyena_channels(seq):
    return HY_W


def _hyena_kernel(u0_ref, u1_ref, u2_ref, w0_ref, w1_ref, w2_ref, b0_ref, b1_ref, b2_ref, skip_ref,
                  gre_ref, gim_ref, cf_ref, sf_ref, stf_ref, o_ref, c_ref, s_ref, st_ref):
    seq = c_ref.shape[0]
    n_rows = u0_ref.shape[0]
    n_seq = n_rows // seq
    pos = lax.broadcasted_iota(jnp.int32, (n_rows, 1), 0) % seq

    @pl.when((pl.program_id(0) == 0) & (pl.program_id(1) == 0))
    def _():
        c_ref[...] = cf_ref[...].astype(BF16)
        s_ref[...] = sf_ref[...].astype(BF16)
        st_ref[...] = stf_ref[...].astype(BF16)

    def short_conv(u_ref, w_ref, b_ref):
        x, w = u_ref[...].astype(F32), w_ref[...]
        prev = jnp.where(pos == 0, 0.0, pltpu.roll(x, 1, 0))
        nxt = jnp.where(pos == seq - 1, 0.0, pltpu.roll(x, n_rows - 1, 0))
        return prev * w[0:1] + x * w[1:2] + nxt * w[2:3] + b_ref[...]

    def side_by_side(a):
        return a if n_seq == 1 else jnp.concatenate([a[s * seq:(s + 1) * seq] for s in range(n_seq)], axis=1)

    def stacked(a):
        ct = a.shape[1] // n_seq
        return a if n_seq == 1 else jnp.concatenate([a[:, s * ct:(s + 1) * ct] for s in range(n_seq)], axis=0)

    x0 = short_conv(u0_ref, w0_ref, b0_ref)
    gv = short_conv(u1_ref, w1_ref, b1_ref) * short_conv(u2_ref, w2_ref, b2_ref)
    sig = side_by_side(gv).astype(BF16)
    u_re = _mm(c_ref[...], sig)
    u_im = _mm(s_ref[...], sig)
    g_re = jnp.concatenate([gre_ref[...]] * n_seq, axis=1)
    g_im = jnp.concatenate([gim_ref[...]] * n_seq, axis=1)
    bin0 = lax.broadcasted_iota(jnp.int32, (seq, 1), 0) == 0
    p_im = u_im * g_im
    y_re = u_re * g_re - jnp.where(bin0, 0.0, p_im)
    y_im = jnp.where(bin0, p_im, u_re * g_im + u_im * g_re)
    y = stacked(_mm(c_ref[...], y_re.astype(BF16)) + _mm(st_ref[...], y_im.astype(BF16)))
    o_ref[...] = (x0 * (y + gv * skip_ref[...])).astype(o_ref.dtype)


def _hyena(proj, layer, w_conv, b_conv, skip, g_re, g_im, tabs, n_batch, seq):
    ct = _hyena_channels(seq)
    nct = HY_W // ct
    u_specs = [pl.BlockSpec((HY_ROWS, ct), functools.partial(lambda b, c, g: (b, g * nct + c), g=g)) for g in range(3)]
    w_specs = [pl.BlockSpec((None, 3, ct), functools.partial(lambda b, c, g: (layer, 0, g * nct + c), g=g)) for g in range(3)]
    b_specs = [pl.BlockSpec((None, 1, ct), functools.partial(lambda b, c, g: (layer, 0, g * nct + c), g=g)) for g in range(3)]
    tab_spec = pl.BlockSpec((seq, seq), lambda b, c: (0, 0))
    return pl.pallas_call(
        _hyena_kernel,
        grid=(n_batch * seq // HY_ROWS, nct),
        in_specs=u_specs + w_specs + b_specs + [
            pl.BlockSpec((None, 1, ct), lambda b, c: (layer, 0, c)),
            pl.BlockSpec((None, seq, ct), lambda b, c: (layer, 0, c)),
            pl.BlockSpec((None, seq, ct), lambda b, c: (layer, 0, c)),
        ] + [tab_spec] * 3,
        out_specs=pl.BlockSpec((HY_ROWS, ct), lambda b, c: (b, c)),
        out_shape=jax.ShapeDtypeStruct((n_batch * seq, HY_W), BF16),
        scratch_shapes=[pltpu.VMEM((seq, seq), BF16)] * 3,
        compiler_params=_params(2),
        name="hyena_conv",
    )(proj, proj, proj, w_conv, w_conv, w_conv, b_conv, b_conv, b_conv, skip, g_re, g_im, *tabs)


def _dft_tables(n_tok):
    k = np.arange(n_tok)[:, None]
    s = np.arange(n_tok)[None, :]
    ang = ((k * s) % (2 * n_tok)) * (np.pi / n_tok)
    cos_t = np.cos(ang)
    sin_f = np.where(k == 0, np.where(s % 2 == 0, 1.0, -1.0), -np.sin(ang))
    return [jnp.asarray(t, F32) for t in (cos_t, sin_f, sin_f.T)]


GLA_LEVELS = (32, 16, 8, 4, 2, 1)
GLA_SAFE_DECAY = 60.0
GLA_GROUP = 2


def _gla_constants():
    c = GLA_CHUNK
    idx = np.arange(c)
    i, t = idx[:, None], idx[None, :]
    masks = []
    for s in GLA_LEVELS:
        upper = (idx % (2 * s)) >= s
        masks.append(((i // (2 * s)) == (t // (2 * s))) & upper[:, None] & (~upper)[None, :])
    masks.append(i == t)
    tri = t <= i
    fwd_m = np.stack([np.tile(m, (H_A, 1)) for m in masks]).astype(np.float32)
    bwd_m = np.stack([np.tile(m[::-1, ::-1], (H_A, 1)) for m in masks]).astype(np.float32)
    head_of_row = np.repeat(np.arange(H_A), c)[:, None]
    head_of_lane = np.repeat(np.arange(H_A), DK_A)[None, :]
    head_mask = head_of_row == head_of_lane
    return (jnp.asarray(tri, BF16), jnp.asarray(tri[::-1, ::-1], BF16), jnp.asarray(fwd_m), jnp.asarray(bwd_m),
            jnp.asarray(head_mask, BF16))


def _pair_reference(b, s, backward, row):
    c = GLA_CHUNK
    ref = s if backward else s - 1
    if 2 * s >= 8:
        pieces = [jnp.broadcast_to(b[p * 2 * s + ref:p * 2 * s + ref + 1, :], (2 * s, b.shape[1]))
                  for p in range(c // (2 * s))]
        return pieces[0] if len(pieces) == 1 else jnp.concatenate(pieces, axis=0)
    pos = row % (2 * s)
    out = None
    for o in range(2 * s):
        d = ref - o
        shifted = b if d == 0 else pltpu.roll(b, (-d) % c, 0)
        out = shifted if out is None else jnp.where(pos == o, shifted, out)
    return out


def _chunk_log_decay(la, t_ref):
    l1 = la.astype(BF16)
    r1 = la - l1.astype(F32)
    l2 = r1.astype(BF16)
    l3 = (r1 - l2.astype(F32)).astype(BF16)
    tmat = t_ref[...]
    return _mm(tmat, l1) + _mm(tmat, l2) + _mm(tmat, l3)


def _stack_heads(a, hm):
    ab = a.astype(BF16)
    return jnp.concatenate([ab] * H_A, axis=0) * hm


def _state_terms(k, v, b, b_last):
    c = GLA_CHUNK
    k_rest = (k * jnp.exp(b_last - b)).T
    carry = jnp.broadcast_to(jnp.exp(b_last), (2 * c, b.shape[1])).T
    return k_rest.astype(BF16), carry


def _gla_chunk(q, k, v, la, t_ref, m_ref, hm, s_ref, backward):
    c = GLA_CHUNK
    b = _chunk_log_decay(la, t_ref)
    row = lax.broadcasted_iota(jnp.int32, (c, 1), 0)
    last = 0 if backward else c - 1
    b_last = b[last:last + 1, :]
    scores = _nt(_stack_heads(q, hm), k.astype(BF16)) * m_ref[len(GLA_LEVELS)]
    for lvl, s in enumerate(GLA_LEVELS):
        is_query = (row % (2 * s) < s) if backward else (row % (2 * s) >= s)
        delta = b - _pair_reference(b, s, backward, row)
        x = jnp.exp(jnp.where(is_query, delta, -delta))
        scores = scores + _nt(_stack_heads(q * x, hm), (k * x).astype(BF16)) * m_ref[lvl]
    scores = scores.astype(BF16)
    state = s_ref[...]
    inter = _mm(_stack_heads(q * jnp.exp(b), hm), state.astype(BF16))
    k_rest, carry = _state_terms(k, v, b, b_last)
    outs = []
    for h in range(H_A):
        rows = slice(c * h, c * (h + 1))
        v_h = v[:, DV_A * h:DV_A * (h + 1)]
        outs.append(_mm(scores[rows], v_h) + inter[rows])
        s_ref[rows, :] = state[rows] * carry[rows] + _mm(k_rest[rows], v_h)
    return jnp.concatenate(outs, axis=1)


def _gla_local(items, hm):
    c = GLA_CHUNK
    bs = [_chunk_log_decay(la, t_ref) for _, _, _, la, t_ref, _, _ in items]
    b_lasts = [b[(0 if it[6] else c - 1):(0 if it[6] else c - 1) + 1, :] for b, it in zip(bs, items)]
    q_decayed = [_stack_heads(it[0] * jnp.exp(b), hm) for it, b in zip(items, bs)]
    k_grown = [(it[1] * jnp.exp(-b)).astype(BF16) for it, b in zip(items, bs)]
    raw = [_nt(qd, kg) for qd, kg in zip(q_decayed, k_grown)]
    masked = [r * it[5] for r, it in zip(raw, items)]
    terms = [_state_terms(it[1], it[2], b, bl) for it, b, bl in zip(items, bs, b_lasts)]
    out = []
    for n in range(0, len(items), 2):
        v = items[n][2]
        heads = [(slice(c * h, c * (h + 1)), v[:, DV_A * h:DV_A * (h + 1)]) for h in range(H_A)]
        both = (masked[n] + masked[n + 1]).astype(BF16)
        intra = jnp.concatenate([_mm(both[rows], v_h) for rows, v_h in heads], axis=1)
        for m in (n, n + 1):
            k_rest, carry = terms[m]
            incr = jnp.concatenate([_mm(k_rest[rows], v_h) for rows, v_h in heads], axis=0)
            out.append((intra if m == n else None, q_decayed[m], incr, carry))
    return out


def _gla_kernel(*refs, sample, has_prev=False, slot=0):
    if sample:
        (x_ref, z_ref, wf_ref, bf_ref, wb_ref, bb_ref, tf_ref, tb_ref, mf_ref, mb_ref, hm_ref, gn_ref, sf0_ref, sb0_ref,
         o_ref, la_f, la_b, o_f, o_b, s_f, s_b, qd_f, qd_b, ds_f, ds_b, cr_f, cr_b) = refs
    else:
        x_ref, z_ref, wf_ref, bf_ref, wb_ref, bb_ref, tf_ref, tb_ref, mf_ref, mb_ref, hm_ref, gn_ref = refs[:12]
        (o_ref, sf_out, sb_out, la_f, la_b, o_f, o_b, s_f, s_b,
         qd_f, qd_b, ds_f, ds_b, cr_f, cr_b) = refs[12 + (2 if has_prev else 0):]
    n_tok = x_ref.shape[0]
    n_chunks = n_tok // GLA_CHUNK
    hk, hv = H_A * DK_A, H_A * DV_A
    zb = z_ref[...].astype(BF16)

    def log_sigmoid(t):
        return jnp.minimum(t, 0.0) - jnp.log(1.0 + jnp.exp(-jnp.abs(t)))

    la_f[...] = log_sigmoid(_mm(zb, wf_ref[...].astype(BF16)) + bf_ref[...]) / GLA_TAU
    la_b[...] = log_sigmoid(_mm(zb, wb_ref[...].astype(BF16)) + bb_ref[...]) / GLA_TAU
    if sample:
        s_f[...] = sf0_ref[...]
        s_b[...] = sb0_ref[...]
    else:
        s_f[...] = jnp.zeros_like(s_f)
        s_b[...] = jnp.zeros_like(s_b)
    hm = hm_ref[...]

    fwd = (la_f, tf_ref, mf_ref, s_f, o_f, qd_f, ds_f, cr_f, False)
    bwd = (la_b, tb_ref, mb_ref, s_b, o_b, qd_b, ds_b, cr_b, True)
    tri_f = jnp.sum(mf_ref[...], axis=0)
    tri_b = jnp.sum(mb_ref[...], axis=0)

    def chunk_rows(ci, backward):
        cidx = n_chunks - 1 - ci if backward else ci
        return cidx, pl.ds(pl.multiple_of(cidx * GLA_CHUNK, GLA_CHUNK), GLA_CHUNK)

    def load_qkv(rows):
        q = x_ref[rows, 0:hk].astype(F32) * (DK_A ** -0.5)
        return q, x_ref[rows, hk:2 * hk].astype(F32), x_ref[rows, 2 * hk:2 * hk + hv]

    def safe_step(ci, carry):
        for la_ref, t_ref, m_ref, s_ref, out_ref, _, _, _, backward in (fwd, bwd):
            _, rows = chunk_rows(ci, backward)
            out_ref[rows, :] = _gla_chunk(*load_qkv(rows), la_ref[rows, :], t_ref, m_ref, hm, s_ref, backward)
        return carry

    def local_step(gi, carry):
        items, dests = [], []
        for u in range(GLA_GROUP):
            cidx = gi * GLA_GROUP + u
            rows = pl.ds(pl.multiple_of(cidx * GLA_CHUNK, GLA_CHUNK), GLA_CHUNK)
            qkv = load_qkv(rows)
            for (la_ref, t_ref, _, _, out_ref, qd_ref, ds_ref, cr_ref, backward), tri in ((fwd, tri_f), (bwd, tri_b)):
                items.append((*qkv, la_ref[rows, :], t_ref, tri, backward))
                dests.append((out_ref, rows, qd_ref, ds_ref, cr_ref, cidx))
        for (out_ref, rows, qd_ref, ds_ref, cr_ref, cidx), (intra, qd, incr, factor) in zip(dests, _gla_local(items, hm)):
            out_ref[rows, :] = jnp.zeros((GLA_CHUNK, hv), F32) if intra is None else intra
            qd_ref[cidx] = qd
            ds_ref[cidx] = incr
            cr_ref[cidx] = factor
        return carry

    def scan_step(ci, carry):
        for _, _, _, s_ref, out_ref, qd_ref, ds_ref, cr_ref, backward in (fwd, bwd):
            cidx, rows = chunk_rows(ci, backward)
            state = s_ref[...]
            inter = _mm(qd_ref[cidx], state.astype(BF16))
            out_ref[rows, :] += jnp.concatenate(
                [inter[GLA_CHUNK * h:GLA_CHUNK * (h + 1)] for h in range(H_A)], axis=1)
            s_ref[...] = state * cr_ref[cidx] + ds_ref[cidx]
        return carry

    chunk_sums = [jnp.sum(ref[...].reshape(n_chunks, GLA_CHUNK, hk), axis=1) for ref in (la_f, la_b)]
    mild = jnp.minimum(jnp.min(chunk_sums[0]), jnp.min(chunk_sums[1])) > -GLA_SAFE_DECAY

    @pl.when(mild)
    def _():
        lax.fori_loop(0, n_chunks // GLA_GROUP, local_step, 0, unroll=2)
        lax.fori_loop(0, n_chunks, scan_step, 0, unroll=2)

    @pl.when(jnp.logical_not(mild))
    def _():
        lax.fori_loop(0, n_chunks, safe_step, 0)
    if not sample:
        _store_slot(sf_out, slot, not has_prev, s_f[...])
        _store_slot(sb_out, slot, not has_prev, s_b[...])
    gain = gn_ref[...]
    for h in range(H_A):
        cols = slice(DV_A * h, DV_A * (h + 1))
        r = x_ref[:, 2 * hk + hv + DV_A * h:2 * hk + hv + DV_A * (h + 1)].astype(F32)
        o_ref[:, cols] = (_rms(o_f[:, cols] + o_b[:, cols]) * gain * (r * jax.nn.sigmoid(r))).astype(o_ref.dtype)


def _gla(proj, w_gf, b_gf, w_gb, b_gb, g_norm, consts, n_batch, seq, ctx=None, slot=0, prev=None):
    sample = ctx is not None
    hk, hv = H_A * DK_A, H_A * DV_A
    n_ch = seq // GLA_CHUNK
    full = lambda shape: pl.BlockSpec(shape, lambda b: (0,) * len(shape))
    layered = lambda shape: pl.BlockSpec((None,) + shape, lambda b: (slot,) + (0,) * len(shape))
    in_specs = [
        pl.BlockSpec((seq, 2 * hk + 2 * hv), lambda b: (b, 0)),
        pl.BlockSpec((seq, LANES), lambda b: (b, EVEN_W // LANES - 1)),
        layered((LANES, hk)), layered((1, hk)), layered((LANES, hk)), layered((1, hk)),
        full(consts[0].shape), full(consts[1].shape), full(consts[2].shape), full(consts[3].shape), full(consts[4].shape),
        layered((1, DV_A)),
    ]
    args = [proj, proj, w_gf, b_gf, w_gb, b_gb, *consts, g_norm]
    o_spec = pl.BlockSpec((seq, hv), lambda b: (b, 0))
    o_shape = jax.ShapeDtypeStruct((n_batch * seq, hv), BF16)
    aliases = {}
    if sample:
        st_spec = pl.BlockSpec((None, None, hk, DV_A), lambda b: (b, slot, 0, 0))
        in_specs += [st_spec, st_spec]
        args += [ctx[0], ctx[1]]
        out_specs, out_shape = o_spec, o_shape
    else:
        if prev is not None:
            in_specs += [pl.BlockSpec(memory_space=pl.ANY)] * 2
            aliases = {len(args): 1, len(args) + 1: 2}
            args += list(prev)
        st_spec, st_shape = _stack_out(n_batch, (DEPTH + 1) // 2, (hk, DV_A), slot, prev is None)
        out_specs, out_shape = [o_spec, st_spec, st_spec], [o_shape, st_shape, st_shape]
    return pl.pallas_call(
        functools.partial(_gla_kernel, sample=sample, has_prev=prev is not None, slot=slot),
        grid=(n_batch,),
        in_specs=in_specs,
        out_specs=out_specs,
        out_shape=out_shape,
        input_output_aliases=aliases,
        scratch_shapes=[pltpu.VMEM((seq, hk), F32), pltpu.VMEM((seq, hk), F32),
                        pltpu.VMEM((seq, hv), F32), pltpu.VMEM((seq, hv), F32),
                        pltpu.VMEM((hk, DV_A), F32), pltpu.VMEM((hk, DV_A), F32),
                        pltpu.VMEM((n_ch, H_A * GLA_CHUNK, hk), BF16), pltpu.VMEM((n_ch, H_A * GLA_CHUNK, hk), BF16),
                        pltpu.VMEM((n_ch, hk, DV_A), F32), pltpu.VMEM((n_ch, hk, DV_A), F32),
                        pltpu.VMEM((n_ch, hk, DV_A), F32), pltpu.VMEM((n_ch, hk, DV_A), F32)],
        compiler_params=_params(1),
        name="gla_sample" if sample else "gla_prompt",
    )(*args)


def _axial_rope(n_tokens, dim):
    rows = n_tokens // GRID_W
    row = np.repeat(np.arange(rows), GRID_W).astype(np.float64)
    col = np.tile(np.arange(GRID_W), rows).astype(np.float64)
    n_freq = dim // 4
    inv = ROPE_THETA ** (-np.arange(n_freq) / n_freq)
    ang = np.concatenate([row[:, None] * inv, col[:, None] * inv], axis=-1)
    return np.cos(ang).astype(np.float32), np.sin(ang).astype(np.float32)


def _filter_features(n_tokens):
    t = np.linspace(0.0, 1.0, n_tokens)[:, None]
    w = 2.0 * np.pi * np.arange(n_tokens)[:, None] / n_tokens
    f = np.linspace(1e-4, FILT_BANDS - 1, FILT_BANDS)[None, :]
    z = np.concatenate([t, np.cos(f * w), -np.sin(f * w)], axis=-1)
    z = np.pad(z, ((0, 0), (0, LANES - FILT_EMB)))
    return jnp.asarray(z, F32), jnp.asarray(t, F32)


_QB_ZERO = H_D * (NOPE_D + ROPE_D)
_QB_PERM = np.array([(NOPE_D + ROPE_D) * (p // MLA_QW) + p % MLA_QW if p % MLA_QW < NOPE_D + ROPE_D else _QB_ZERO
                     for p in range(H_D * MLA_QW)])


def _mla_rope_tables(cos_d, sin_d):
    n, half = cos_d.shape
    zeros = np.zeros((n, half), np.float32)

    def lanes(pre, width):
        pad = np.zeros((n, width - pre.shape[1] - 2 * half), np.float32)
        build = lambda first, second, lead: np.concatenate([lead, first, second, pad], axis=1)
        return (build(cos_d, cos_d, pre), build(-sin_d, zeros, 0 * pre), build(zeros, sin_d, 0 * pre))

    q_tabs = lanes(np.ones((n, NOPE_D), np.float32), MLA_QW)
    k_tabs = lanes(np.zeros((n, 0), np.float32), LANES)
    return tuple(jnp.asarray(t) for t in q_tabs + k_tabs)

EVEN_ROW_GROUPS = ((0, 0, 1536), (1568, 1536, 1024), (1536, EVEN_W - 2 * GATE_RANK, 2 * GATE_RANK))
ODD_ROW_GROUPS = ((0, 0, 1984),)
EVEN_KEEP = (2304, 256)
ODD_KEEP = (1920, LANES)


def kernel(x_prompt, x_sample, state_gla_fwd, state_gla_bwd, cache_gqa_k, cache_gqa_v, cache_mla_ckv, cache_mla_kpe, c, c_ctx, w_mod, b_mod, w_in_even, w_gla_gate_f, b_gla_gate_f, w_gla_gate_b, b_gla_gate_b, g_gla_norm, g_gqa_q, g_gqa_k, w_out_even, w_in_odd, w_hy_conv, b_hy_conv, hy_skip, w_filt1, b_filt1, filt_freq, w_filt2, b_filt2, w_filt3, g_mla_q, w_mla_qb, g_mla_kv, w_mla_kvb, w_out_odd, w_ffn_in, w_ffn_out, g_final):
    cvec = jnp.concatenate([c_ctx[None, :], c, jnp.zeros((8 - 1 - DEC_BATCH, D_MODEL), F32)], axis=0)
    mod = _modulation(cvec, w_mod, b_mod)
    xc, xs, s_row0 = x_prompt.reshape(N_PROMPT, D_MODEL), x_sample.reshape(N_SAMPLE, D_MODEL), 0

    gla_consts = _gla_constants()
    cos_b, sin_b = _axial_rope(DEC_SEQ, HD_B)
    rope_b = (jnp.asarray(np.concatenate([cos_b, cos_b], axis=1)), jnp.asarray(np.concatenate([-sin_b, sin_b], axis=1)))
    cos_d, sin_d = _axial_rope(DEC_SEQ, ROPE_D)
    rope_d = _mla_rope_tables(cos_d, sin_d)
    w_qb_all = jnp.pad(w_mla_qb, ((0, 0), (0, 0), (0, 1)))[:, :, _QB_PERM]
    tabs_c, tabs_s = _dft_tables(SEQ), _dft_tables(DEC_SEQ)
    z_c, t_c = _filter_features(SEQ)
    z_s, t_s = _filter_features(DEC_SEQ)
    deltas = jnp.asarray(np.abs(np.linspace(HY_MIN_DECAY, HY_MAX_DECAY, HY_W))[None, :], F32)

    wt_even = jnp.swapaxes(w_in_even, 1, 2)
    wt_odd = jnp.swapaxes(w_in_odd, 1, 2)

    filt_args = (jnp.pad(w_filt1, ((0, 0), (0, LANES - FILT_EMB), (0, 0))), b_filt1, filt_freq, w_filt2, b_filt2, w_filt3)
    g_c = _filter_spectrum(z_c, *filt_args, t_c, deltas, tabs_c)
    g_s = _filter_spectrum(z_s, *filt_args, t_s, deltas, tabs_s)
    b_conv = b_hy_conv.reshape(DEPTH // 2, 1, 3 * HY_W)
    skip = hy_skip.reshape(DEPTH // 2, 1, HY_W)

    n_even, n_odd = (DEPTH + 1) // 2, DEPTH // 2
    hk = H_A * DK_A
    z0 = LANES - 2 * GATE_RANK
    pad_f = jnp.zeros((n_even, LANES, hk), F32).at[:, z0:z0 + GATE_RANK].set(w_gla_gate_f)
    pad_b = jnp.zeros((n_even, LANES, hk), F32).at[:, z0 + GATE_RANK:].set(w_gla_gate_b)
    gate_args = (pad_f, b_gla_gate_f.reshape(n_even, 1, hk), pad_b, b_gla_gate_b.reshape(n_even, 1, hk),
                 g_gla_norm.reshape(n_even, 1, DV_A), gla_consts)
    ctx_a = (state_gla_fwd.reshape(DEC_BATCH, n_even, hk, DV_A), state_gla_bwd.reshape(DEC_BATCH, n_even, hk, DV_A))
    gq_b, gk_b = g_gqa_q.reshape(n_even, 1, HD_B), g_gqa_k.reshape(n_even, 1, HD_B)
    mla_args = (g_mla_q.reshape(n_odd, 1, Q_RANK), w_qb_all, g_mla_kv.reshape(n_odd, 1, KV_RANK), w_mla_kvb)

    new_states = new_kv = new_latent = None
    for i in range(DEPTH):
        j = i // 2
        if i % 2 == 0:
            pc, v_new, ps = _in_proj(xc, xs, s_row0, mod, i, wt_even, j, EVEN_ROW_GROUPS, EVEN_W, EVEN_KEEP)
            a_c, *new_states = _gla(pc, *gate_args, BATCH, SEQ, slot=j, prev=new_states)
            a_s = _gla(ps, *gate_args, DEC_BATCH, DEC_SEQ, ctx=ctx_a, slot=j)
            b_c, *new_kv = _gqa(pc, gq_b, gk_b, BATCH, SEQ, v_f32=v_new, slot=j, prev=new_kv)
            b_s = _gqa(ps, gq_b, gk_b, DEC_BATCH, DEC_SEQ, ctx=(cache_gqa_k, cache_gqa_v), rope=rope_b, slot=j)
            w_out = w_out_even
        else:
            pc, kpe_new, ps = _in_proj(xc, xs, s_row0, mod, i, wt_odd, j, ODD_ROW_GROUPS, ODD_W, ODD_KEEP)
            a_c = _hyena(pc, j, w_hy_conv, b_conv, skip, g_c[0], g_c[1], tabs_c, BATCH, SEQ)
            a_s = _hyena(ps, j, w_hy_conv, b_conv, skip, g_s[0], g_s[1], tabs_s, DEC_BATCH, DEC_SEQ)
            b_c, *new_latent = _mla(pc, *mla_args, BATCH, SEQ, kpe_f32=kpe_new, slot=j, prev=new_latent)
            b_s = _mla(ps, *mla_args, DEC_BATCH, DEC_SEQ, ctx=(cache_mla_ckv, cache_mla_kpe), rope=rope_d, slot=j)
            w_out = w_out_odd
        x_mid = _out_proj([a_c, b_c], [a_s, b_s], w_out, j, xc, xs, s_row0, mod, i, 2)
        if i < DEPTH - 1:
            x_all = _ffn(x_mid, 0, N_PROMPT + N_SAMPLE, mod, i, w_ffn_in, w_ffn_out)
            xc, xs, s_row0 = x_all, x_all, N_PROMPT
        else:
            xc = _ffn(x_mid, 0, N_PROMPT, mod, i, w_ffn_in, w_ffn_out, final_gain=g_final)
            xs = _ffn(x_mid, N_PROMPT, N_SAMPLE, mod, i, w_ffn_in, w_ffn_out, final_gain=g_final)
    y_prompt = xc.reshape(BATCH, SEQ, D_MODEL)
    y_sample = xs.reshape(DEC_BATCH, DEC_SEQ, D_MODEL)
    state_shape = (BATCH, (DEPTH + 1) // 2, H_A, DK_A, DV_A)
    return (y_prompt, y_sample, new_states[0].reshape(state_shape), new_states[1].reshape(state_shape),
            new_kv[0], new_kv[1], new_latent[0], new_latent[1])
```

```python
import functools
import math

import numpy as np
import jax
import jax.numpy as jnp
from jax import lax
from jax.experimental import pallas as pl
from jax.experimental.pallas import tpu as pltpu

F32 = jnp.float32
BF16 = jnp.bfloat16

D_MODEL = 1024
BATCH, SEQ = 16, 256
DEC_BATCH, DEC_SEQ = 2, 1024
DEPTH = 4
PAST_LEN = 512
GRID_W = 64
HALF_W = D_MODEL // 2
H_A, DV_A, DK_A = 4, 128, 64
GATE_RANK = 16
GLA_TAU = 16.0
GLA_CHUNK = 64
HD_B, H_B, KV_B = 128, 4, 2
HY_W = HALF_W
FILT_EMB, FILT_HID = 33, 64
FILT_BANDS = (FILT_EMB - 1) // 2
HY_MIN_DECAY = math.log(1e-2) / 1.5
HY_MAX_DECAY = math.log(1e-2) / 0.3
H_D, V_D, NOPE_D, ROPE_D = 4, 128, 128, 64
Q_RANK, KV_RANK = 256, 128
FFN_H = 2816
ROPE_THETA = 10000.0
EPS = 1e-6

LANES = 128
VMEM_LIMIT = 56 * 1024 * 1024

MOD_ROWS = 1024
TM = 1024
TM_IN = 512
TM_FFN = 2048
EVEN_W = 2688
ODD_W = 2048
FFN_TN = 256
QB_GQA = 256
QB_MLA = 512


def _params(n_grid):
    return pltpu.CompilerParams(dimension_semantics=("arbitrary",) * n_grid, vmem_limit_bytes=VMEM_LIMIT)


def _nt(a, b):
    return lax.dot_general(a, b, (((1,), (1,)), ((), ())), preferred_element_type=F32)


def _mm(a, b):
    return jnp.dot(a, b, preferred_element_type=F32)


def _rms(x):
    return x * lax.rsqrt(jnp.mean(x * x, axis=-1, keepdims=True) + EPS)


MOD_GROUP = 3


def _mod_kernel(c_ref, w_ref, b_ref, o_ref):
    cv = c_ref[...]
    s = cv * jax.nn.sigmoid(cv)
    m = _mm(s.astype(BF16), w_ref[...].astype(BF16)) + b_ref[...]
    for k in range(MOD_GROUP):
        o_ref[k] = m[:, D_MODEL * k:D_MODEL * (k + 1)]


def _modulation(cvec, w_mod, b_mod):
    return pl.pallas_call(
        _mod_kernel,
        grid=(DEPTH, 6 // MOD_GROUP),
        in_specs=[
            pl.BlockSpec((8, D_MODEL), lambda l, n: (0, 0)),
            pl.BlockSpec((None, D_MODEL, MOD_GROUP * D_MODEL), lambda l, n: (l, 0, n)),
            pl.BlockSpec((None, 1, MOD_GROUP * D_MODEL), lambda l, n: (l, 0, n)),
        ],
        out_specs=pl.BlockSpec((None, MOD_GROUP, 8, D_MODEL), lambda l, n: (l, n, 0, 0)),
        out_shape=jax.ShapeDtypeStruct((DEPTH, 6, 8, D_MODEL), F32),
        compiler_params=_params(2),
        name="adaln_mod",
    )(cvec, w_mod, b_mod.reshape(DEPTH, 1, 6 * D_MODEL))


N_PROMPT = BATCH * SEQ
N_SAMPLE = DEC_BATCH * DEC_SEQ


def _stack_out(n_batch, n_slots, tail, slot, first):
    zeros = (0,) * len(tail)
    if first:
        spec = pl.BlockSpec((None, n_slots) + tail, lambda b, *_: (b, 0) + zeros)
    else:
        spec = pl.BlockSpec((None, None) + tail, lambda b, *_: (b, slot) + zeros)
    return spec, jax.ShapeDtypeStruct((n_batch, n_slots) + tail, F32)


def _store_slot(ref, slot, owns_stack, value):
    if not owns_stack:
        ref[...] = value
        return
    for s in range(ref.shape[0]):
        ref[s] = value if s == slot else jnp.zeros_like(value)


def _stream_index_maps(tile_rows, s_row0):
    n_c = N_PROMPT // tile_rows
    return (lambda i: (jnp.minimum(i, n_c - 1), 0)), (lambda i: (s_row0 // tile_rows + jnp.maximum(i - n_c, 0), 0))


def _in_proj_kernel(xc_ref, xs_ref, sh_ref, sc_ref, wt_ref, oc_ref, keep_ref, os_ref, wb_ref, *, row_groups, keep, n_c):
    i = pl.program_id(0)

    @pl.when(i == 0)
    def _():
        wb_ref[...] = jnp.zeros_like(wb_ref)
        for src, dst, size in row_groups:
            wb_ref[dst:dst + size, :] = wt_ref[src:src + size, :].astype(BF16)

    def project(x_ref, g):
        h = (_rms(x_ref[...]) * (1.0 + sc_ref[pl.ds(g, 1), :]) + sh_ref[pl.ds(g, 1), :]).astype(BF16)
        return _nt(h, wb_ref[...])

    @pl.when(i < n_c)
    def _():
        y = project(xc_ref, 0)
        oc_ref[...] = y.astype(oc_ref.dtype)
        keep_ref[...] = y[:, keep[0]:keep[0] + keep[1]]

    @pl.when(i >= n_c)
    def _():
        os_ref[...] = project(xs_ref, 1 + (i - n_c) // (MOD_ROWS // TM_IN)).astype(os_ref.dtype)


def _in_proj(xc, xs, s_row0, mod, layer, wt, w_layer, row_groups, n, keep):
    n_c, n_s = N_PROMPT // TM_IN, N_SAMPLE // TM_IN
    xc_idx, xs_idx = _stream_index_maps(TM_IN, s_row0)
    c_idx, s_idx = _stream_index_maps(TM_IN, 0)
    return pl.pallas_call(
        functools.partial(_in_proj_kernel, row_groups=row_groups, keep=keep, n_c=n_c),
        grid=(n_c + n_s,),
        in_specs=[pl.BlockSpec((TM_IN, D_MODEL), xc_idx), pl.BlockSpec((TM_IN, D_MODEL), xs_idx),
                  pl.BlockSpec((None, None, 8, D_MODEL), lambda i: (layer, 0, 0, 0)),
                  pl.BlockSpec((None, None, 8, D_MODEL), lambda i: (layer, 1, 0, 0)),
                  pl.BlockSpec((None, wt.shape[1], D_MODEL), lambda i: (w_layer, 0, 0), pipeline_mode=pl.Buffered(1))],
        out_specs=[pl.BlockSpec((TM_IN, n), c_idx), pl.BlockSpec((TM_IN, keep[1]), c_idx), pl.BlockSpec((TM_IN, n), s_idx)],
        out_shape=[jax.ShapeDtypeStruct((N_PROMPT, n), BF16), jax.ShapeDtypeStruct((N_PROMPT, keep[1]), F32),
                   jax.ShapeDtypeStruct((N_SAMPLE, n), BF16)],
        scratch_shapes=[pltpu.VMEM((n, D_MODEL), BF16)],
        compiler_params=_params(1),
        name="norm_mod_proj",
    )(xc, xs, mod, mod, wt)


def _ffn_kernel(x_ref, sh_ref, sc_ref, gate_ref, wg_ref, wu_ref, wd_ref, *refs, first_sub, final):
    (gf_ref, o_ref, h_ref) = refs if final else (None,) + refs
    n_sub = x_ref.shape[0] // MOD_ROWS
    subs = [(slice(s * MOD_ROWS, (s + 1) * MOD_ROWS),
             jnp.maximum(first_sub + pl.program_id(0) * n_sub + s - (N_PROMPT // MOD_ROWS - 1), 0)) for s in range(n_sub)]

    @pl.when(pl.program_id(1) == 0)
    def _():
        for rows, g in subs:
            x = x_ref[rows, :]
            o_ref[rows, :] = x
            h_ref[rows, :] = (_rms(x) * (1.0 + sc_ref[pl.ds(g, 1), :]) + sh_ref[pl.ds(g, 1), :]).astype(BF16)

    wg = wg_ref[...].astype(BF16)
    wu = wu_ref[...].astype(BF16)
    wd = wd_ref[...].astype(BF16)
    for rows, g in subs:
        h = h_ref[rows, :]
        a = _mm(h, wg)
        act = (a * jax.nn.sigmoid(a) * _mm(h, wu)).astype(BF16)
        o_ref[rows, :] += gate_ref[pl.ds(g, 1), :] * _mm(act, wd)

    if final:
        @pl.when(pl.program_id(1) == pl.num_programs(1) - 1)
        def _():
            for rows, _ in subs:
                o_ref[rows, :] = _rms(o_ref[rows, :]) * gf_ref[...]


def _ffn(x, row0, m, mod, layer, w_in, w_out, final_gain=None):
    nj = FFN_H // FFN_TN
    tile0 = row0 // TM_FFN
    mod_spec = lambda k: pl.BlockSpec((None, None, 8, D_MODEL), lambda i, j: (layer, k, 0, 0))
    final = final_gain is not None
    extra_specs = [pl.BlockSpec((1, D_MODEL), lambda i, j: (0, 0))] if final else []
    extra_args = [final_gain.reshape(1, D_MODEL)] if final else []
    return pl.pallas_call(
        functools.partial(_ffn_kernel, first_sub=row0 // MOD_ROWS, final=final),
        grid=(m // TM_FFN, nj),
        in_specs=[pl.BlockSpec((TM_FFN, D_MODEL), lambda i, j: (tile0 + i, 0)), mod_spec(3), mod_spec(4), mod_spec(5),
                  pl.BlockSpec((None, D_MODEL, FFN_TN), lambda i, j: (layer, 0, j)),
                  pl.BlockSpec((None, D_MODEL, FFN_TN), lambda i, j: (layer, 0, j + nj)),
                  pl.BlockSpec((None, FFN_TN, D_MODEL), lambda i, j: (layer, j, 0))] + extra_specs,
        out_specs=pl.BlockSpec((TM_FFN, D_MODEL), lambda i, j: (i, 0)),
        out_shape=jax.ShapeDtypeStruct((m, D_MODEL), F32),
        scratch_shapes=[pltpu.VMEM((TM_FFN, D_MODEL), BF16)],
        compiler_params=_params(2),
        name="ffn_residual",
    )(x, mod, mod, mod, w_in, w_in, w_out, *extra_args)


def _proj_res_kernel(ac0_ref, ac1_ref, as0_ref, as1_ref, w0_ref, w1_ref, xc_ref, xs_ref, gate_ref, o_ref, *, n_c):
    i = pl.program_id(0)

    def mix(a0_ref, a1_ref, x_ref, g):
        acc = _mm(a0_ref[...], w0_ref[...].astype(BF16)) + _mm(a1_ref[...], w1_ref[...].astype(BF16))
        return x_ref[...] + gate_ref[pl.ds(g, 1), :] * acc

    @pl.when(i < n_c)
    def _():
        o_ref[...] = mix(ac0_ref, ac1_ref, xc_ref, 0)

    @pl.when(i >= n_c)
    def _():
        o_ref[...] = mix(as0_ref, as1_ref, xs_ref, 1 + (i - n_c))


def _out_proj(acts_c, acts_s, w, w_layer, xc, xs, s_row0, mod, layer, k_gate):
    n_c, n_s = N_PROMPT // TM, N_SAMPLE // TM
    kw = acts_c[0].shape[1]
    xc_idx, xs_idx = _stream_index_maps(TM, s_row0)
    c_idx, s_idx = _stream_index_maps(TM, 0)
    w_specs = [pl.BlockSpec((None, kw, D_MODEL), functools.partial(lambda i, p: (w_layer, p, 0), p=p),
                            pipeline_mode=pl.Buffered(1)) for p in range(2)]
    return pl.pallas_call(
        functools.partial(_proj_res_kernel, n_c=n_c),
        grid=(n_c + n_s,),
        in_specs=[pl.BlockSpec((TM, kw), c_idx)] * 2 + [pl.BlockSpec((TM, kw), s_idx)] * 2 + w_specs + [
            pl.BlockSpec((TM, D_MODEL), xc_idx), pl.BlockSpec((TM, D_MODEL), xs_idx),
            pl.BlockSpec((None, None, 8, D_MODEL), lambda i: (layer, k_gate, 0, 0)),
        ],
        out_specs=pl.BlockSpec((TM, D_MODEL), lambda i: (i, 0)),
        out_shape=jax.ShapeDtypeStruct((N_PROMPT + N_SAMPLE, D_MODEL), F32),
        compiler_params=_params(1),
        name="out_proj_residual",
    )(*acts_c, *acts_s, w, w, xc, xs, mod)


def _gqa_kernel(*refs, sample, has_prev=False, slot=0):
    if sample:
        q_ref, k_ref, v_ref, gq_ref, gk_ref, ck_ref, cv_ref, cos_ref, sin_ref, o_ref, kb_ref, vb_ref = refs
    else:
        n_in = 8 if has_prev else 6
        q_ref, k_ref, v_ref, gq_ref, gk_ref, vf_ref = refs[:6]
        o_ref, kc_ref, vc_ref, kb_ref, vb_ref = refs[n_in:]
        if not has_prev:
            stacks = (kc_ref, vc_ref)
            kc_ref, vc_ref = kc_ref.at[slot], vc_ref.at[slot]
    qi = pl.program_id(1)
    n_new = k_ref.shape[0]
    past = PAST_LEN if sample else 0
    rep = H_B // KV_B

    @pl.when(qi == 0)
    def _():
        if not (sample or has_prev):
            for ref in stacks:
                for other in range(ref.shape[0]):
                    if other != slot:
                        ref[other] = jnp.zeros(ref.shape[1:], ref.dtype)
        for g in range(KV_B):
            sl = slice(HD_B * g, HD_B * (g + 1))
            kn = _rms(k_ref[:, sl].astype(F32)) * gk_ref[...]
            if sample:
                kb_ref[0:past, sl] = ck_ref[:, g, :].astype(BF16)
                vb_ref[g, 0:past, 0:HD_B] = cv_ref[:, g, :].astype(BF16)
                kn = kn * cos_ref[...] + pltpu.roll(kn, HD_B // 2, 1) * sin_ref[...]
            else:
                kc_ref[:, g, :] = kn
                vc_ref[:, g, :] = vf_ref[:, sl]
            kb_ref[past:past + n_new, sl] = kn.astype(BF16)
            vb_ref[g, past:past + n_new, 0:HD_B] = v_ref[:, sl].astype(BF16)
            vb_ref[g, :, HD_B:] = jnp.ones((past + n_new, HD_B), BF16)

    qb = q_ref.shape[0]
    r0 = pl.multiple_of(qi * qb, qb)
    qs = []
    for h in range(H_B):
        qn = _rms(q_ref[:, HD_B * h:HD_B * (h + 1)].astype(F32)) * gq_ref[...]
        if sample:
            qn = qn * cos_ref[pl.ds(r0, qb), :] + pltpu.roll(qn, HD_B // 2, 1) * sin_ref[pl.ds(r0, qb), :]
        qs.append((qn * (HD_B ** -0.5)).astype(BF16))
    scores = [_nt(qs[h], kb_ref[:, HD_B * (h // rep):HD_B * (h // rep + 1)]) for h in range(H_B)]
    weights = [jnp.exp(s - jnp.max(s, axis=-1, keepdims=True)).astype(BF16) for s in scores]
    sums = [_mm(weights[h], vb_ref[h // rep]) for h in range(H_B)]
    for h in range(H_B):
        o_ref[:, HD_B * h:HD_B * (h + 1)] = (sums[h][:, :HD_B] / sums[h][:, HD_B:]).astype(o_ref.dtype)


def _gqa(proj, g_q, g_k, n_batch, seq, ctx=None, rope=None, v_f32=None, slot=0, prev=None):
    sample = ctx is not None
    m = n_batch * seq
    qb = min(QB_GQA, seq)
    nq = seq // qb
    n_even = (DEPTH + 1) // 2
    in_specs = [
        pl.BlockSpec((qb, 512), lambda b, i: (b * nq + i, 3)),
        pl.BlockSpec((seq, 256), lambda b, i: (b, 8)),
        pl.BlockSpec((seq, 256), lambda b, i: (b, 9)),
        pl.BlockSpec((None, 1, HD_B), lambda b, i: (slot, 0, 0)),
        pl.BlockSpec((None, 1, HD_B), lambda b, i: (slot, 0, 0)),
    ]
    args = [proj, proj, proj, g_q, g_k]
    o_spec = pl.BlockSpec((qb, 512), lambda b, i: (b * nq + i, 0))
    o_shape = jax.ShapeDtypeStruct((m, 512), BF16)
    aliases = {}
    if sample:
        cache_spec = pl.BlockSpec((None, None, PAST_LEN, KV_B, HD_B), lambda b, i: (b, slot, 0, 0, 0))
        in_specs += [
            cache_spec, cache_spec,
            pl.BlockSpec((seq, HD_B), lambda b, i: (0, 0)),
            pl.BlockSpec((seq, HD_B), lambda b, i: (0, 0)),
        ]
        args += [ctx[0], ctx[1], rope[0], rope[1]]
        out_specs, out_shape = o_spec, o_shape
    else:
        in_specs.append(pl.BlockSpec((seq, KV_B * HD_B), lambda b, i: (b, 0)))
        args.append(v_f32)
        if prev is not None:
            in_specs += [pl.BlockSpec(memory_space=pl.ANY)] * 2
            aliases = {len(args): 1, len(args) + 1: 2}
            args += list(prev)
        if prev is None:
            new_spec = pl.BlockSpec((None, n_even, seq, KV_B, HD_B), lambda b, i: (b, 0, 0, 0, 0))
        else:
            new_spec = pl.BlockSpec((None, None, seq, KV_B, HD_B), lambda b, i: (b, slot, 0, 0, 0))
        new_shape = jax.ShapeDtypeStruct((n_batch, n_even, seq, KV_B, HD_B), F32)
        out_specs, out_shape = [o_spec, new_spec, new_spec], [o_shape, new_shape, new_shape]
    n_keys = seq + (PAST_LEN if sample else 0)
    return pl.pallas_call(
        functools.partial(_gqa_kernel, sample=sample, has_prev=prev is not None, slot=slot),
        grid=(n_batch, nq),
        in_specs=in_specs,
        out_specs=out_specs,
        out_shape=out_shape,
        input_output_aliases=aliases,
        scratch_shapes=[pltpu.VMEM((n_keys, KV_B * HD_B), BF16), pltpu.VMEM((KV_B, n_keys, 2 * HD_B), BF16)],
        compiler_params=_params(2),
        name="gqa_sample" if sample else "gqa_prompt",
    )(*args)


MLA_QW = 2 * LANES
MLA_HW = 4 * LANES


def _rotate_pairs(x, cos_t, sin_lo, sin_hi):
    w = x.shape[1]
    return x * cos_t + pltpu.roll(x, ROPE_D // 2, 1) * sin_hi + pltpu.roll(x, w - ROPE_D // 2, 1) * sin_lo


def _mla_kernel(*refs, sample, has_prev=False, slot=0):
    if sample:
        (cq_ref, ckv_ref, kpe_ref, gq_ref, wqb_ref, gkv_ref, wkvb_ref, cckv_ref, ckpe_ref,
         qc_ref, ql_ref, qh_ref, kc_ref, kl_ref, kh_ref, o_ref, kv_s) = refs
    else:
        cq_ref, ckv_ref, kpe_ref, gq_ref, wqb_ref, gkv_ref, wkvb_ref, kf_ref = refs[:8]
        o_ref, ckvn_ref, kpeo_ref, kv_s = refs[8 + (2 if has_prev else 0):]
    qi = pl.program_id(1)
    n_new = ckv_ref.shape[0]
    past = PAST_LEN if sample else 0

    def stage_kv(rows, kv, kpe_block):
        for h in range(H_D):
            kv_s[rows, MLA_HW * h:MLA_HW * h + NOPE_D] = kv[:, 256 * h:256 * h + NOPE_D].astype(BF16)
            kv_s[rows, MLA_HW * h + NOPE_D:MLA_HW * h + MLA_QW] = kpe_block
            kv_s[rows, MLA_HW * h + MLA_QW:MLA_HW * h + MLA_QW + V_D] = kv[:, 256 * h + NOPE_D:256 * (h + 1)].astype(BF16)

    @pl.when(qi == 0)
    def _():
        wkvb = wkvb_ref[...].astype(BF16)
        ckvn = _rms(ckv_ref[...].astype(F32)) * gkv_ref[...]
        if not sample:
            _store_slot(ckvn_ref, slot, not has_prev, ckvn)
            _store_slot(kpeo_ref, slot, not has_prev, kf_ref[:, 0:ROPE_D])
        kpe = kpe_ref[...]
        if sample:
            ctx_kpe = jnp.concatenate([ckpe_ref[...], jnp.zeros((past, LANES - ROPE_D), F32)], axis=1)
            stage_kv(slice(0, past), _mm(cckv_ref[...].astype(BF16), wkvb), ctx_kpe.astype(BF16))
            kpe = _rotate_pairs(kpe.astype(F32), kc_ref[...], kl_ref[...], kh_ref[...]).astype(BF16)
        stage_kv(slice(past, past + n_new), _mm(ckvn.astype(BF16), wkvb), kpe)
        for h in range(H_D):
            kv_s[:, MLA_HW * h + MLA_QW + V_D:MLA_HW * (h + 1)] = jnp.ones((past + n_new, V_D), BF16)

    q = _mm((_rms(cq_ref[...].astype(F32)) * gq_ref[...]).astype(BF16), wqb_ref[...].astype(BF16))
    q = q * ((NOPE_D + ROPE_D) ** -0.5)
    qs = []
    for h in range(H_D):
        q_h = q[:, MLA_QW * h:MLA_QW * (h + 1)]
        if sample:
            qb = cq_ref.shape[0]
            rows = pl.ds(pl.multiple_of(qi * qb, qb), qb)
            q_h = _rotate_pairs(q_h, qc_ref[rows, :], ql_ref[rows, :], qh_ref[rows, :])
        qs.append(q_h.astype(BF16))
    scores = [_nt(qs[h], kv_s[:, MLA_HW * h:MLA_HW * h + MLA_QW]) for h in range(H_D)]
    weights = [jnp.exp(s - jnp.max(s, axis=-1, keepdims=True)).astype(BF16) for s in scores]
    sums = [_mm(weights[h], kv_s[:, MLA_HW * h + MLA_QW:MLA_HW * (h + 1)]) for h in range(H_D)]
    for h in range(H_D):
        o_ref[:, V_D * h:V_D * (h + 1)] = (sums[h][:, :V_D] / sums[h][:, V_D:]).astype(o_ref.dtype)


def _mla(proj, g_q, w_qb, g_kv, w_kvb, n_batch, seq, ctx=None, rope=None, kpe_f32=None, slot=0, prev=None):
    sample = ctx is not None
    m = n_batch * seq
    qb = min(QB_MLA, seq)
    nq = seq // qb
    in_specs = [
        pl.BlockSpec((qb, Q_RANK), lambda b, i: (b * nq + i, 6)),
        pl.BlockSpec((seq, KV_RANK), lambda b, i: (b, 14)),
        pl.BlockSpec((seq, LANES), lambda b, i: (b, 15)),
        pl.BlockSpec((None, 1, Q_RANK), lambda b, i: (slot, 0, 0)),
        pl.BlockSpec((None, Q_RANK, H_D * MLA_QW), lambda b, i: (slot, 0, 0)),
        pl.BlockSpec((None, 1, KV_RANK), lambda b, i: (slot, 0, 0)),
        pl.BlockSpec((None, KV_RANK, 1024), lambda b, i: (slot, 0, 0)),
    ]
    args = [proj, proj, proj, g_q, w_qb, g_kv, w_kvb]
    o_spec = pl.BlockSpec((qb, 512), lambda b, i: (b * nq + i, 0))
    o_shape = jax.ShapeDtypeStruct((m, 512), BF16)
    aliases = {}
    if sample:
        in_specs += [
            pl.BlockSpec((None, None, PAST_LEN, KV_RANK), lambda b, i: (b, slot, 0, 0)),
            pl.BlockSpec((None, None, PAST_LEN, ROPE_D), lambda b, i: (b, slot, 0, 0)),
        ] + [pl.BlockSpec((seq, MLA_QW), lambda b, i: (0, 0))] * 3 + [pl.BlockSpec((seq, LANES), lambda b, i: (0, 0))] * 3
        args += [ctx[0], ctx[1], *rope]
        out_specs, out_shape = o_spec, o_shape
    else:
        in_specs.append(pl.BlockSpec((seq, LANES), lambda b, i: (b, 0)))
        args.append(kpe_f32)
        if prev is not None:
            in_specs += [pl.BlockSpec(memory_space=pl.ANY)] * 2
            aliases = {len(args): 1, len(args) + 1: 2}
            args += list(prev)
        ckv_spec, ckv_shape = _stack_out(n_batch, DEPTH // 2, (seq, KV_RANK), slot, prev is None)
        kpe_spec, kpe_shape = _stack_out(n_batch, DEPTH // 2, (seq, ROPE_D), slot, prev is None)
        out_specs, out_shape = [o_spec, ckv_spec, kpe_spec], [o_shape, ckv_shape, kpe_shape]
    n_keys = seq + (PAST_LEN if sample else 0)
    return pl.pallas_call(
        functools.partial(_mla_kernel, sample=sample, has_prev=prev is not None, slot=slot),
        grid=(n_batch, nq),
        in_specs=in_specs,
        out_specs=out_specs,
        out_shape=out_shape,
        input_output_aliases=aliases,
        scratch_shapes=[pltpu.VMEM((n_keys, H_D * MLA_HW), BF16)],
        compiler_params=_params(2),
        name="mla_sample" if sample else "mla_prompt",
    )(*args)


def _dft(table, x):
    return _mm(table.astype(BF16), x.astype(BF16))


def _filter_kernel(z_ref, wf1_ref, bf1_ref, fr_ref, wf2_ref, bf2_ref, wf3_ref, t_ref, dl_ref,
                   c_ref, s_ref, gre_ref, gim_ref):
    n_tok = z_ref.shape[0]
    fr = fr_ref[...]
    hid = jnp.sin(fr * (_mm(z_ref[...].astype(BF16), wf1_ref[...].astype(BF16)) + bf1_ref[...]))
    hid = jnp.sin(fr * (_mm(hid.astype(BF16), wf2_ref[...].astype(BF16)) + bf2_ref[...]))
    filt = _mm(hid.astype(BF16), wf3_ref[...].astype(BF16))
    decay = jnp.exp(-t_ref[...] * dl_ref[...])
    row = lax.broadcasted_iota(jnp.int32, (n_tok, 1), 0)
    h_f = filt[:, :HY_W] * decay
    h_b = jnp.where(row == 0, 0.0, filt[:, HY_W:] * decay)
    p, m = h_f + h_b, h_f - h_b
    g_re = _dft(c_ref[...], p)
    g_im = _dft(s_ref[...], m)
    sign = jnp.where(row % 2 == 0, 1.0, -1.0)
    nyquist = jnp.sum(p * sign, axis=0, keepdims=True)
    g_im = jnp.where(row == 0, nyquist, g_im)
    wk = jnp.where(row == 0, 0.5 / n_tok, 1.0 / n_tok)
    gre_ref[...] = g_re * wk
    gim_ref[...] = g_im * wk


def _filter_spectrum(z, wf1, bf1, freq, wf2, bf2, wf3, t_col, deltas, tabs):
    n_layers, n_tok = wf1.shape[0], z.shape[0]
    shared = lambda a: pl.BlockSpec(a.shape, lambda l: (0,) * a.ndim)
    per_layer = lambda a: pl.BlockSpec((None,) + a.shape[1:], lambda l: (l,) + (0,) * (a.ndim - 1))
    row = lambda a: a.reshape(n_layers, 1, FILT_HID)
    args = [z, wf1, row(bf1), row(freq), wf2, row(bf2), wf3, t_col, deltas, tabs[0], tabs[1]]
    layered = [False, True, True, True, True, True, True, False, False, False, False]
    out = jax.ShapeDtypeStruct((n_layers, n_tok, HY_W), F32)
    out_spec = pl.BlockSpec((None, n_tok, HY_W), lambda l: (l, 0, 0))
    return pl.pallas_call(
        _filter_kernel,
        grid=(n_layers,),
        in_specs=[per_layer(a) if lay else shared(a) for a, lay in zip(args, layered)],
        out_specs=[out_spec, out_spec],
        out_shape=[out, out],
        compiler_params=_params(1),
        name="hyena_filter",
    )(*args)


HY_ROWS = 1024


HY_CT = HY_W


def _hyena_kernel(u0_ref, u1_ref, u2_ref, w0_ref, w1_ref, w2_ref, b0_ref, b1_ref, b2_ref, skip_ref,
                  gre_ref, gim_ref, cf_ref, sf_ref, stf_ref, o_ref, c_ref, s_ref, st_ref):
    seq = c_ref.shape[0]
    n_rows = u0_ref.shape[0]
    n_seq = n_rows // seq
    pos = lax.broadcasted_iota(jnp.int32, (n_rows, 1), 0) % seq

    @pl.when((pl.program_id(0) == 0) & (pl.program_id(1) == 0))
    def _():
        c_ref[...] = cf_ref[...].astype(BF16)
        s_ref[...] = sf_ref[...].astype(BF16)
        st_ref[...] = stf_ref[...].astype(BF16)

    def short_conv(u_ref, w_ref, b_ref):
        x, w = u_ref[...].astype(F32), w_ref[...]
        prev = jnp.where(pos == 0, 0.0, pltpu.roll(x, 1, 0))
        nxt = jnp.where(pos == seq - 1, 0.0, pltpu.roll(x, n_rows - 1, 0))
        return prev * w[0:1] + x * w[1:2] + nxt * w[2:3] + b_ref[...]

    def side_by_side(a):
        return a if n_seq == 1 else jnp.concatenate([a[s * seq:(s + 1) * seq] for s in range(n_seq)], axis=1)

    def stacked(a):
        ct = a.shape[1] // n_seq
        return a if n_seq == 1 else jnp.concatenate([a[:, s * ct:(s + 1) * ct] for s in range(n_seq)], axis=0)

    x0 = short_conv(u0_ref, w0_ref, b0_ref)
    gv = short_conv(u1_ref, w1_ref, b1_ref) * short_conv(u2_ref, w2_ref, b2_ref)
    sig = side_by_side(gv).astype(BF16)
    u_re = _mm(c_ref[...], sig)
    u_im = _mm(s_ref[...], sig)
    g_re = jnp.concatenate([gre_ref[...]] * n_seq, axis=1)
    g_im = jnp.concatenate([gim_ref[...]] * n_seq, axis=1)
    bin0 = lax.broadcasted_iota(jnp.int32, (seq, 1), 0) == 0
    p_im = u_im * g_im
    y_re = u_re * g_re - jnp.where(bin0, 0.0, p_im)
    y_im = jnp.where(bin0, p_im, u_re * g_im + u_im * g_re)
    y = stacked(_mm(c_ref[...], y_re.astype(BF16)) + _mm(st_ref[...], y_im.astype(BF16)))
    o_ref[...] = (x0 * (y + gv * skip_ref[...])).astype(o_ref.dtype)


def _hyena(proj, layer, w_conv, b_conv, skip, g_re, g_im, tabs, n_batch, seq):
    ct = HY_CT
    nct = HY_W // ct
    u_specs = [pl.BlockSpec((HY_ROWS, ct), functools.partial(lambda b, c, g: (b, g * nct + c), g=g)) for g in range(3)]
    w_specs = [pl.BlockSpec((None, 3, ct), functools.partial(lambda b, c, g: (layer, 0, g * nct + c), g=g)) for g in range(3)]
    b_specs = [pl.BlockSpec((None, 1, ct), functools.partial(lambda b, c, g: (layer, 0, g * nct + c), g=g)) for g in range(3)]
    tab_spec = pl.BlockSpec((seq, seq), lambda b, c: (0, 0))
    return pl.pallas_call(
        _hyena_kernel,
        grid=(n_batch * seq // HY_ROWS, nct),
        in_specs=u_specs + w_specs + b_specs + [
            pl.BlockSpec((None, 1, ct), lambda b, c: (layer, 0, c)),
            pl.BlockSpec((None, seq, ct), lambda b, c: (layer, 0, c)),
            pl.BlockSpec((None, seq, ct), lambda b, c: (layer, 0, c)),
        ] + [tab_spec] * 3,
        out_specs=pl.BlockSpec((HY_ROWS, ct), lambda b, c: (b, c)),
        out_shape=jax.ShapeDtypeStruct((n_batch * seq, HY_W), BF16),
        scratch_shapes=[pltpu.VMEM((seq, seq), BF16)] * 3,
        compiler_params=_params(2),
        name="hyena_conv",
    )(proj, proj, proj, w_conv, w_conv, w_conv, b_conv, b_conv, b_conv, skip, g_re, g_im, *tabs)


def _dft_tables(n_tok):
    k = np.arange(n_tok)[:, None]
    s = np.arange(n_tok)[None, :]
    ang = ((k * s) % (2 * n_tok)) * (np.pi / n_tok)
    cos_t = np.cos(ang)
    sin_f = np.where(k == 0, np.where(s % 2 == 0, 1.0, -1.0), -np.sin(ang))
    return [jnp.asarray(t, F32) for t in (cos_t, sin_f, sin_f.T)]


GLA_LEVELS = (32, 16, 8, 4, 2, 1)
GLA_SAFE_DECAY = 60.0
GLA_GROUP = 2


def _gla_constants():
    c = GLA_CHUNK
    idx = np.arange(c)
    i, t = idx[:, None], idx[None, :]
    masks = []
    for s in GLA_LEVELS:
        upper = (idx % (2 * s)) >= s
        masks.append(((i // (2 * s)) == (t // (2 * s))) & upper[:, None] & (~upper)[None, :])
    masks.append(i == t)
    tri = t <= i
    fwd_m = np.stack([np.tile(m, (H_A, 1)) for m in masks]).astype(np.float32)
    bwd_m = np.stack([np.tile(m[::-1, ::-1], (H_A, 1)) for m in masks]).astype(np.float32)
    head_of_row = np.repeat(np.arange(H_A), c)[:, None]
    head_of_lane = np.repeat(np.arange(H_A), DK_A)[None, :]
    head_mask = head_of_row == head_of_lane
    return (jnp.asarray(tri, BF16), jnp.asarray(tri[::-1, ::-1], BF16), jnp.asarray(fwd_m), jnp.asarray(bwd_m),
            jnp.asarray(head_mask, BF16))


def _pair_reference(b, s, backward, row):
    c = GLA_CHUNK
    ref = s if backward else s - 1
    if 2 * s >= 8:
        pieces = [jnp.broadcast_to(b[p * 2 * s + ref:p * 2 * s + ref + 1, :], (2 * s, b.shape[1]))
                  for p in range(c // (2 * s))]
        return pieces[0] if len(pieces) == 1 else jnp.concatenate(pieces, axis=0)
    pos = row % (2 * s)
    out = None
    for o in range(2 * s):
        d = ref - o
        shifted = b if d == 0 else pltpu.roll(b, (-d) % c, 0)
        out = shifted if out is None else jnp.where(pos == o, shifted, out)
    return out


def _chunk_log_decay(la, t_ref):
    l1 = la.astype(BF16)
    r1 = la - l1.astype(F32)
    l2 = r1.astype(BF16)
    l3 = (r1 - l2.astype(F32)).astype(BF16)
    tmat = t_ref[...]
    return _mm(tmat, l1) + _mm(tmat, l2) + _mm(tmat, l3)


def _stack_heads(a, hm):
    ab = a.astype(BF16)
    return jnp.concatenate([ab] * H_A, axis=0) * hm


def _state_terms(k, v, b, b_last):
    c = GLA_CHUNK
    k_rest = (k * jnp.exp(b_last - b)).T
    carry = jnp.broadcast_to(jnp.exp(b_last), (2 * c, b.shape[1])).T
    return k_rest.astype(BF16), carry


def _gla_chunk(q, k, v, la, t_ref, m_ref, hm, s_ref, backward):
    c = GLA_CHUNK
    b = _chunk_log_decay(la, t_ref)
    row = lax.broadcasted_iota(jnp.int32, (c, 1), 0)
    last = 0 if backward else c - 1
    b_last = b[last:last + 1, :]
    scores = _nt(_stack_heads(q, hm), k.astype(BF16)) * m_ref[len(GLA_LEVELS)]
    for lvl, s in enumerate(GLA_LEVELS):
        is_query = (row % (2 * s) < s) if backward else (row % (2 * s) >= s)
        delta = b - _pair_reference(b, s, backward, row)
        x = jnp.exp(jnp.where(is_query, delta, -delta))
        scores = scores + _nt(_stack_heads(q * x, hm), (k * x).astype(BF16)) * m_ref[lvl]
    scores = scores.astype(BF16)
    state = s_ref[...]
    inter = _mm(_stack_heads(q * jnp.exp(b), hm), state.astype(BF16))
    k_rest, carry = _state_terms(k, v, b, b_last)
    outs = []
    for h in range(H_A):
        rows = slice(c * h, c * (h + 1))
        v_h = v[:, DV_A * h:DV_A * (h + 1)]
        outs.append(_mm(scores[rows], v_h) + inter[rows])
        s_ref[rows, :] = state[rows] * carry[rows] + _mm(k_rest[rows], v_h)
    return jnp.concatenate(outs, axis=1)


def _gla_local(items, hm):
    c = GLA_CHUNK
    bs = [_chunk_log_decay(la, t_ref) for _, _, _, la, t_ref, _, _ in items]
    b_lasts = [b[(0 if it[6] else c - 1):(0 if it[6] else c - 1) + 1, :] for b, it in zip(bs, items)]
    q_decayed = [_stack_heads(it[0] * jnp.exp(b), hm) for it, b in zip(items, bs)]
    k_grown = [(it[1] * jnp.exp(-b)).astype(BF16) for it, b in zip(items, bs)]
    raw = [_nt(qd, kg) for qd, kg in zip(q_decayed, k_grown)]
    masked = [r * it[5] for r, it in zip(raw, items)]
    terms = [_state_terms(it[1], it[2], b, bl) for it, b, bl in zip(items, bs, b_lasts)]
    out = []
    for n in range(0, len(items), 2):
        v = items[n][2]
        heads = [(slice(c * h, c * (h + 1)), v[:, DV_A * h:DV_A * (h + 1)]) for h in range(H_A)]
        both = (masked[n] + masked[n + 1]).astype(BF16)
        intra = jnp.concatenate([_mm(both[rows], v_h) for rows, v_h in heads], axis=1)
        for m in (n, n + 1):
            k_rest, carry = terms[m]
            incr = jnp.concatenate([_mm(k_rest[rows], v_h) for rows, v_h in heads], axis=0)
            out.append((intra if m == n else None, q_decayed[m], incr, carry))
    return out


def _gla_kernel(*refs, sample, has_prev=False, slot=0):
    if sample:
        (x_ref, z_ref, wf_ref, bf_ref, wb_ref, bb_ref, tf_ref, tb_ref, mf_ref, mb_ref, hm_ref, gn_ref, sf0_ref, sb0_ref,
         o_ref, la_f, la_b, o_f, o_b, s_f, s_b, qd_f, qd_b, ds_f, ds_b, cr_f, cr_b) = refs
    else:
        x_ref, z_ref, wf_ref, bf_ref, wb_ref, bb_ref, tf_ref, tb_ref, mf_ref, mb_ref, hm_ref, gn_ref = refs[:12]
        (o_ref, sf_out, sb_out, la_f, la_b, o_f, o_b, s_f, s_b,
         qd_f, qd_b, ds_f, ds_b, cr_f, cr_b) = refs[12 + (2 if has_prev else 0):]
    n_tok = x_ref.shape[0]
    n_chunks = n_tok // GLA_CHUNK
    hk, hv = H_A * DK_A, H_A * DV_A
    zb = z_ref[...].astype(BF16)

    def log_sigmoid(t):
        return jnp.minimum(t, 0.0) - jnp.log(1.0 + jnp.exp(-jnp.abs(t)))

    la_f[...] = log_sigmoid(_mm(zb, wf_ref[...].astype(BF16)) + bf_ref[...]) / GLA_TAU
    la_b[...] = log_sigmoid(_mm(zb, wb_ref[...].astype(BF16)) + bb_ref[...]) / GLA_TAU
    if sample:
        s_f[...] = sf0_ref[...]
        s_b[...] = sb0_ref[...]
    else:
        s_f[...] = jnp.zeros_like(s_f)
        s_b[...] = jnp.zeros_like(s_b)
    hm = hm_ref[...]

    fwd = (la_f, tf_ref, mf_ref, s_f, o_f, qd_f, ds_f, cr_f, False)
    bwd = (la_b, tb_ref, mb_ref, s_b, o_b, qd_b, ds_b, cr_b, True)
    tri_f = jnp.sum(mf_ref[...], axis=0)
    tri_b = jnp.sum(mb_ref[...], axis=0)

    def chunk_rows(ci, backward):
        cidx = n_chunks - 1 - ci if backward else ci
        return cidx, pl.ds(pl.multiple_of(cidx * GLA_CHUNK, GLA_CHUNK), GLA_CHUNK)

    def load_qkv(rows):
        q = x_ref[rows, 0:hk].astype(F32) * (DK_A ** -0.5)
        return q, x_ref[rows, hk:2 * hk].astype(F32), x_ref[rows, 2 * hk:2 * hk + hv]

    def safe_step(ci, carry):
        for la_ref, t_ref, m_ref, s_ref, out_ref, _, _, _, backward in (fwd, bwd):
            _, rows = chunk_rows(ci, backward)
            out_ref[rows, :] = _gla_chunk(*load_qkv(rows), la_ref[rows, :], t_ref, m_ref, hm, s_ref, backward)
        return carry

    def local_step(gi, carry):
        items, dests = [], []
        for u in range(GLA_GROUP):
            cidx = gi * GLA_GROUP + u
            rows = pl.ds(pl.multiple_of(cidx * GLA_CHUNK, GLA_CHUNK), GLA_CHUNK)
            qkv = load_qkv(rows)
            for (la_ref, t_ref, _, _, out_ref, qd_ref, ds_ref, cr_ref, backward), tri in ((fwd, tri_f), (bwd, tri_b)):
                items.append((*qkv, la_ref[rows, :], t_ref, tri, backward))
                dests.append((out_ref, rows, qd_ref, ds_ref, cr_ref, cidx))
        for (out_ref, rows, qd_ref, ds_ref, cr_ref, cidx), (intra, qd, incr, factor) in zip(dests, _gla_local(items, hm)):
            out_ref[rows, :] = jnp.zeros((GLA_CHUNK, hv), F32) if intra is None else intra
            qd_ref[cidx] = qd
            ds_ref[cidx] = incr
            cr_ref[cidx] = factor
        return carry

    def scan_step(ci, carry):
        for _, _, _, s_ref, out_ref, qd_ref, ds_ref, cr_ref, backward in (fwd, bwd):
            cidx, rows = chunk_rows(ci, backward)
            state = s_ref[...]
            inter = _mm(qd_ref[cidx], state.astype(BF16))
            out_ref[rows, :] += jnp.concatenate(
                [inter[GLA_CHUNK * h:GLA_CHUNK * (h + 1)] for h in range(H_A)], axis=1)
            s_ref[...] = state * cr_ref[cidx] + ds_ref[cidx]
        return carry

    chunk_sums = [jnp.sum(ref[...].reshape(n_chunks, GLA_CHUNK, hk), axis=1) for ref in (la_f, la_b)]
    mild = jnp.minimum(jnp.min(chunk_sums[0]), jnp.min(chunk_sums[1])) > -GLA_SAFE_DECAY

    @pl.when(mild)
    def _():
        lax.fori_loop(0, n_chunks // GLA_GROUP, local_step, 0, unroll=2)
        lax.fori_loop(0, n_chunks, scan_step, 0, unroll=2)

    @pl.when(jnp.logical_not(mild))
    def _():
        lax.fori_loop(0, n_chunks, safe_step, 0)
    if not sample:
        _store_slot(sf_out, slot, not has_prev, s_f[...])
        _store_slot(sb_out, slot, not has_prev, s_b[...])
    gain = gn_ref[...]
    for h in range(H_A):
        cols = slice(DV_A * h, DV_A * (h + 1))
        r = x_ref[:, 2 * hk + hv + DV_A * h:2 * hk + hv + DV_A * (h + 1)].astype(F32)
        o_ref[:, cols] = (_rms(o_f[:, cols] + o_b[:, cols]) * gain * (r * jax.nn.sigmoid(r))).astype(o_ref.dtype)


def _gla(proj, w_gf, b_gf, w_gb, b_gb, g_norm, consts, n_batch, seq, ctx=None, slot=0, prev=None):
    sample = ctx is not None
    hk, hv = H_A * DK_A, H_A * DV_A
    n_ch = seq // GLA_CHUNK
    full = lambda shape: pl.BlockSpec(shape, lambda b: (0,) * len(shape))
    layered = lambda shape: pl.BlockSpec((None,) + shape, lambda b: (slot,) + (0,) * len(shape))
    in_specs = [
        pl.BlockSpec((seq, 2 * hk + 2 * hv), lambda b: (b, 0)),
        pl.BlockSpec((seq, LANES), lambda b: (b, EVEN_W // LANES - 1)),
        layered((LANES, hk)), layered((1, hk)), layered((LANES, hk)), layered((1, hk)),
        full(consts[0].shape), full(consts[1].shape), full(consts[2].shape), full(consts[3].shape), full(consts[4].shape),
        layered((1, DV_A)),
    ]
    args = [proj, proj, w_gf, b_gf, w_gb, b_gb, *consts, g_norm]
    o_spec = pl.BlockSpec((seq, hv), lambda b: (b, 0))
    o_shape = jax.ShapeDtypeStruct((n_batch * seq, hv), BF16)
    aliases = {}
    if sample:
        st_spec = pl.BlockSpec((None, None, hk, DV_A), lambda b: (b, slot, 0, 0))
        in_specs += [st_spec, st_spec]
        args += [ctx[0], ctx[1]]
        out_specs, out_shape = o_spec, o_shape
    else:
        if prev is not None:
            in_specs += [pl.BlockSpec(memory_space=pl.ANY)] * 2
            aliases = {len(args): 1, len(args) + 1: 2}
            args += list(prev)
        st_spec, st_shape = _stack_out(n_batch, (DEPTH + 1) // 2, (hk, DV_A), slot, prev is None)
        out_specs, out_shape = [o_spec, st_spec, st_spec], [o_shape, st_shape, st_shape]
    return pl.pallas_call(
        functools.partial(_gla_kernel, sample=sample, has_prev=prev is not None, slot=slot),
        grid=(n_batch,),
        in_specs=in_specs,
        out_specs=out_specs,
        out_shape=out_shape,
        input_output_aliases=aliases,
        scratch_shapes=[pltpu.VMEM((seq, hk), F32), pltpu.VMEM((seq, hk), F32),
                        pltpu.VMEM((seq, hv), F32), pltpu.VMEM((seq, hv), F32),
                        pltpu.VMEM((hk, DV_A), F32), pltpu.VMEM((hk, DV_A), F32),
                        pltpu.VMEM((n_ch, H_A * GLA_CHUNK, hk), BF16), pltpu.VMEM((n_ch, H_A * GLA_CHUNK, hk), BF16),
                        pltpu.VMEM((n_ch, hk, DV_A), F32), pltpu.VMEM((n_ch, hk, DV_A), F32),
                        pltpu.VMEM((n_ch, hk, DV_A), F32), pltpu.VMEM((n_ch, hk, DV_A), F32)],
        compiler_params=_params(1),
        name="gla_sample" if sample else "gla_prompt",
    )(*args)


def _axial_rope(n_tokens, dim):
    rows = n_tokens // GRID_W
    row = np.repeat(np.arange(rows), GRID_W).astype(np.float64)
    col = np.tile(np.arange(GRID_W), rows).astype(np.float64)
    n_freq = dim // 4
    inv = ROPE_THETA ** (-np.arange(n_freq) / n_freq)
    ang = np.concatenate([row[:, None] * inv, col[:, None] * inv], axis=-1)
    return np.cos(ang).astype(np.float32), np.sin(ang).astype(np.float32)


def _filter_features(n_tokens):
    t = np.linspace(0.0, 1.0, n_tokens)[:, None]
    w = 2.0 * np.pi * np.arange(n_tokens)[:, None] / n_tokens
    f = np.linspace(1e-4, FILT_BANDS - 1, FILT_BANDS)[None, :]
    z = np.concatenate([t, np.cos(f * w), -np.sin(f * w)], axis=-1)
    z = np.pad(z, ((0, 0), (0, LANES - FILT_EMB)))
    return jnp.asarray(z, F32), jnp.asarray(t, F32)


_QB_ZERO = H_D * (NOPE_D + ROPE_D)
_QB_PERM = np.array([(NOPE_D + ROPE_D) * (p // MLA_QW) + p % MLA_QW if p % MLA_QW < NOPE_D + ROPE_D else _QB_ZERO
                     for p in range(H_D * MLA_QW)])


def _mla_rope_tables(cos_d, sin_d):
    n, half = cos_d.shape
    zeros = np.zeros((n, half), np.float32)

    def lanes(pre, width):
        pad = np.zeros((n, width - pre.shape[1] - 2 * half), np.float32)
        build = lambda first, second, lead: np.concatenate([lead, first, second, pad], axis=1)
        return (build(cos_d, cos_d, pre), build(-sin_d, zeros, 0 * pre), build(zeros, sin_d, 0 * pre))

    q_tabs = lanes(np.ones((n, NOPE_D), np.float32), MLA_QW)
    k_tabs = lanes(np.zeros((n, 0), np.float32), LANES)
    return tuple(jnp.asarray(t) for t in q_tabs + k_tabs)

EVEN_ROW_GROUPS = ((0, 0, 1536), (1568, 1536, 1024), (1536, EVEN_W - 2 * GATE_RANK, 2 * GATE_RANK))
ODD_ROW_GROUPS = ((0, 0, 1984),)
EVEN_KEEP = (2304, 256)
ODD_KEEP = (1920, LANES)


def kernel(x_prompt, x_sample, state_gla_fwd, state_gla_bwd, cache_gqa_k, cache_gqa_v, cache_mla_ckv, cache_mla_kpe, c, c_ctx, w_mod, b_mod, w_in_even, w_gla_gate_f, b_gla_gate_f, w_gla_gate_b, b_gla_gate_b, g_gla_norm, g_gqa_q, g_gqa_k, w_out_even, w_in_odd, w_hy_conv, b_hy_conv, hy_skip, w_filt1, b_filt1, filt_freq, w_filt2, b_filt2, w_filt3, g_mla_q, w_mla_qb, g_mla_kv, w_mla_kvb, w_out_odd, w_ffn_in, w_ffn_out, g_final):
    cvec = jnp.concatenate([c_ctx[None, :], c, jnp.zeros((8 - 1 - DEC_BATCH, D_MODEL), F32)], axis=0)
    mod = _modulation(cvec, w_mod, b_mod)
    xc, xs, s_row0 = x_prompt.reshape(N_PROMPT, D_MODEL), x_sample.reshape(N_SAMPLE, D_MODEL), 0

    gla_consts = _gla_constants()
    cos_b, sin_b = _axial_rope(DEC_SEQ, HD_B)
    rope_b = (jnp.asarray(np.concatenate([cos_b, cos_b], axis=1)), jnp.asarray(np.concatenate([-sin_b, sin_b], axis=1)))
    cos_d, sin_d = _axial_rope(DEC_SEQ, ROPE_D)
    rope_d = _mla_rope_tables(cos_d, sin_d)
    w_qb_all = jnp.pad(w_mla_qb, ((0, 0), (0, 0), (0, 1)))[:, :, _QB_PERM]
    tabs_c, tabs_s = _dft_tables(SEQ), _dft_tables(DEC_SEQ)
    z_c, t_c = _filter_features(SEQ)
    z_s, t_s = _filter_features(DEC_SEQ)
    deltas = jnp.asarray(np.abs(np.linspace(HY_MIN_DECAY, HY_MAX_DECAY, HY_W))[None, :], F32)

    wt_even = jnp.swapaxes(w_in_even, 1, 2)
    wt_odd = jnp.swapaxes(w_in_odd, 1, 2)

    filt_args = (jnp.pad(w_filt1, ((0, 0), (0, LANES - FILT_EMB), (0, 0))), b_filt1, filt_freq, w_filt2, b_filt2, w_filt3)
    g_c = _filter_spectrum(z_c, *filt_args, t_c, deltas, tabs_c)
    g_s = _filter_spectrum(z_s, *filt_args, t_s, deltas, tabs_s)
    b_conv = b_hy_conv.reshape(DEPTH // 2, 1, 3 * HY_W)
    skip = hy_skip.reshape(DEPTH // 2, 1, HY_W)

    n_even, n_odd = (DEPTH + 1) // 2, DEPTH // 2
    hk = H_A * DK_A
    z0 = LANES - 2 * GATE_RANK
    pad_f = jnp.zeros((n_even, LANES, hk), F32).at[:, z0:z0 + GATE_RANK].set(w_gla_gate_f)
    pad_b = jnp.zeros((n_even, LANES, hk), F32).at[:, z0 + GATE_RANK:].set(w_gla_gate_b)
    gate_args = (pad_f, b_gla_gate_f.reshape(n_even, 1, hk), pad_b, b_gla_gate_b.reshape(n_even, 1, hk),
                 g_gla_norm.reshape(n_even, 1, DV_A), gla_consts)
    ctx_a = (state_gla_fwd.reshape(DEC_BATCH, n_even, hk, DV_A), state_gla_bwd.reshape(DEC_BATCH, n_even, hk, DV_A))
    gq_b, gk_b = g_gqa_q.reshape(n_even, 1, HD_B), g_gqa_k.reshape(n_even, 1, HD_B)
    mla_args = (g_mla_q.reshape(n_odd, 1, Q_RANK), w_qb_all, g_mla_kv.reshape(n_odd, 1, KV_RANK), w_mla_kvb)

    new_states = new_kv = new_latent = None
    for i in range(DEPTH):
        j = i // 2
        if i % 2 == 0:
            pc, v_new, ps = _in_proj(xc, xs, s_row0, mod, i, wt_even, j, EVEN_ROW_GROUPS, EVEN_W, EVEN_KEEP)
            a_c, *new_states = _gla(pc, *gate_args, BATCH, SEQ, slot=j, prev=new_states)
            a_s = _gla(ps, *gate_args, DEC_BATCH, DEC_SEQ, ctx=ctx_a, slot=j)
            b_c, *new_kv = _gqa(pc, gq_b, gk_b, BATCH, SEQ, v_f32=v_new, slot=j, prev=new_kv)
            b_s = _gqa(ps, gq_b, gk_b, DEC_BATCH, DEC_SEQ, ctx=(cache_gqa_k, cache_gqa_v), rope=rope_b, slot=j)
            w_out = w_out_even
        else:
            pc, kpe_new, ps = _in_proj(xc, xs, s_row0, mod, i, wt_odd, j, ODD_ROW_GROUPS, ODD_W, ODD_KEEP)
            a_c = _hyena(pc, j, w_hy_conv, b_conv, skip, g_c[0], g_c[1], tabs_c, BATCH, SEQ)
            a_s = _hyena(ps, j, w_hy_conv, b_conv, skip, g_s[0], g_s[1], tabs_s, DEC_BATCH, DEC_SEQ)
            b_c, *new_latent = _mla(pc, *mla_args, BATCH, SEQ, kpe_f32=kpe_new, slot=j, prev=new_latent)
            b_s = _mla(ps, *mla_args, DEC_BATCH, DEC_SEQ, ctx=(cache_mla_ckv, cache_mla_kpe), rope=rope_d, slot=j)
            w_out = w_out_odd
        x_mid = _out_proj([a_c, b_c], [a_s, b_s], w_out, j, xc, xs, s_row0, mod, i, 2)
        if i < DEPTH - 1:
            x_all = _ffn(x_mid, 0, N_PROMPT + N_SAMPLE, mod, i, w_ffn_in, w_ffn_out)
            xc, xs, s_row0 = x_all, x_all, N_PROMPT
        else:
            xc = _ffn(x_mid, 0, N_PROMPT, mod, i, w_ffn_in, w_ffn_out, final_gain=g_final)
            xs = _ffn(x_mid, N_PROMPT, N_SAMPLE, mod, i, w_ffn_in, w_ffn_out, final_gain=g_final)
    y_prompt = xc.reshape(BATCH, SEQ, D_MODEL)
    y_sample = xs.reshape(DEC_BATCH, DEC_SEQ, D_MODEL)
    state_shape = (BATCH, (DEPTH + 1) // 2, H_A, DK_A, DV_A)
    return (y_prompt, y_sample, new_states[0].reshape(state_shape), new_states[1].reshape(state_shape),
            new_kv[0], new_kv[1], new_latent[0], new_latent[1])
```

```python
import functools
import math

import numpy as np
import jax
import jax.numpy as jnp
from jax import lax
from jax.experimental import pallas as pl
from jax.experimental.pallas import tpu as pltpu

F32 = jnp.float32
BF16 = jnp.bfloat16

D_MODEL = 1024
BATCH, SEQ = 16, 256
DEC_BATCH, DEC_SEQ = 2, 1024
DEPTH = 4
PAST_LEN = 512
GRID_W = 64
HALF_W = D_MODEL // 2
H_A, DV_A, DK_A = 4, 128, 64
GATE_RANK = 16
GLA_TAU = 16.0
GLA_CHUNK = 64
HD_B, H_B, KV_B = 128, 4, 2
HY_W = HALF_W
FILT_EMB, FILT_HID = 33, 64
FILT_BANDS = (FILT_EMB - 1) // 2
HY_MIN_DECAY = math.log(1e-2) / 1.5
HY_MAX_DECAY = math.log(1e-2) / 0.3
H_D, V_D, NOPE_D, ROPE_D = 4, 128, 128, 64
Q_RANK, KV_RANK = 256, 128
FFN_H = 2816
ROPE_THETA = 10000.0
EPS = 1e-6

LANES = 128
VMEM_LIMIT = 56 * 1024 * 1024

MOD_ROWS = 1024
TM = 1024
TM_IN = 512
TM_FFN = 2048
EVEN_W = 2688
ODD_W = 2048
FFN_TN = 256
QB_GQA = 256
QB_MLA = 512


def _params(n_grid):
    return pltpu.CompilerParams(dimension_semantics=("arbitrary",) * n_grid, vmem_limit_bytes=VMEM_LIMIT)


def _nt(a, b):
    return lax.dot_general(a, b, (((1,), (1,)), ((), ())), preferred_element_type=F32)


def _mm(a, b):
    return jnp.dot(a, b, preferred_element_type=F32)


def _rms(x):
    return x * lax.rsqrt(jnp.mean(x * x, axis=-1, keepdims=True) + EPS)


MOD_GROUP = 3


def _mod_kernel(c_ref, w_ref, b_ref, o_ref):
    cv = c_ref[...]
    s = cv * jax.nn.sigmoid(cv)
    m = _mm(s.astype(BF16), w_ref[...].astype(BF16)) + b_ref[...]
    for k in range(MOD_GROUP):
        o_ref[k] = m[:, D_MODEL * k:D_MODEL * (k + 1)]


def _modulation(cvec, w_mod, b_mod):
    return pl.pallas_call(
        _mod_kernel,
        grid=(DEPTH, 6 // MOD_GROUP),
        in_specs=[
            pl.BlockSpec((8, D_MODEL), lambda l, n: (0, 0)),
            pl.BlockSpec((None, D_MODEL, MOD_GROUP * D_MODEL), lambda l, n: (l, 0, n)),
            pl.BlockSpec((None, 1, MOD_GROUP * D_MODEL), lambda l, n: (l, 0, n)),
        ],
        out_specs=pl.BlockSpec((None, MOD_GROUP, 8, D_MODEL), lambda l, n: (l, n, 0, 0)),
        out_shape=jax.ShapeDtypeStruct((DEPTH, 6, 8, D_MODEL), F32),
        compiler_params=_params(2),
        name="adaln_mod",
    )(cvec, w_mod, b_mod.reshape(DEPTH, 1, 6 * D_MODEL))


N_PROMPT = BATCH * SEQ
N_SAMPLE = DEC_BATCH * DEC_SEQ


def _stack_out(n_batch, n_slots, tail, slot, first):
    zeros = (0,) * len(tail)
    if first:
        spec = pl.BlockSpec((None, n_slots) + tail, lambda b, *_: (b, 0) + zeros)
    else:
        spec = pl.BlockSpec((None, None) + tail, lambda b, *_: (b, slot) + zeros)
    return spec, jax.ShapeDtypeStruct((n_batch, n_slots) + tail, F32)


def _store_slot(ref, slot, owns_stack, value):
    if not owns_stack:
        ref[...] = value
        return
    for s in range(ref.shape[0]):
        ref[s] = value if s == slot else jnp.zeros_like(value)


def _stream_index_maps(tile_rows, s_row0):
    n_c = N_PROMPT // tile_rows
    return (lambda i: (jnp.minimum(i, n_c - 1), 0)), (lambda i: (s_row0 // tile_rows + jnp.maximum(i - n_c, 0), 0))


def _in_proj_kernel(xc_ref, xs_ref, sh_ref, sc_ref, wt_ref, oc_ref, keep_ref, os_ref, wb_ref, *, row_groups, keep, n_c):
    i = pl.program_id(0)

    @pl.when(i == 0)
    def _():
        wb_ref[...] = jnp.zeros_like(wb_ref)
        for src, dst, size in row_groups:
            wb_ref[dst:dst + size, :] = wt_ref[src:src + size, :].astype(BF16)

    def project(x_ref, g):
        h = (_rms(x_ref[...]) * (1.0 + sc_ref[pl.ds(g, 1), :]) + sh_ref[pl.ds(g, 1), :]).astype(BF16)
        return _nt(h, wb_ref[...])

    @pl.when(i < n_c)
    def _():
        y = project(xc_ref, 0)
        oc_ref[...] = y.astype(oc_ref.dtype)
        keep_ref[...] = y[:, keep[0]:keep[0] + keep[1]]

    @pl.when(i >= n_c)
    def _():
        os_ref[...] = project(xs_ref, 1 + (i - n_c) // (MOD_ROWS // TM_IN)).astype(os_ref.dtype)


def _in_proj(xc, xs, s_row0, mod, layer, wt, w_layer, row_groups, n, keep):
    n_c, n_s = N_PROMPT // TM_IN, N_SAMPLE // TM_IN
    xc_idx, xs_idx = _stream_index_maps(TM_IN, s_row0)
    c_idx, s_idx = _stream_index_maps(TM_IN, 0)
    return pl.pallas_call(
        functools.partial(_in_proj_kernel, row_groups=row_groups, keep=keep, n_c=n_c),
        grid=(n_c + n_s,),
        in_specs=[pl.BlockSpec((TM_IN, D_MODEL), xc_idx), pl.BlockSpec((TM_IN, D_MODEL), xs_idx),
                  pl.BlockSpec((None, None, 8, D_MODEL), lambda i: (layer, 0, 0, 0)),
                  pl.BlockSpec((None, None, 8, D_MODEL), lambda i: (layer, 1, 0, 0)),
                  pl.BlockSpec((None, wt.shape[1], D_MODEL), lambda i: (w_layer, 0, 0), pipeline_mode=pl.Buffered(1))],
        out_specs=[pl.BlockSpec((TM_IN, n), c_idx), pl.BlockSpec((TM_IN, keep[1]), c_idx), pl.BlockSpec((TM_IN, n), s_idx)],
        out_shape=[jax.ShapeDtypeStruct((N_PROMPT, n), BF16), jax.ShapeDtypeStruct((N_PROMPT, keep[1]), F32),
                   jax.ShapeDtypeStruct((N_SAMPLE, n), BF16)],
        scratch_shapes=[pltpu.VMEM((n, D_MODEL), BF16)],
        compiler_params=_params(1),
        name="norm_mod_proj",
    )(xc, xs, mod, mod, wt)


def _ffn_kernel(x_ref, sh_ref, sc_ref, gate_ref, wg_ref, wu_ref, wd_ref, *refs, first_sub, final):
    (gf_ref, o_ref, h_ref) = refs if final else (None,) + refs
    n_sub = x_ref.shape[0] // MOD_ROWS
    subs = [(slice(s * MOD_ROWS, (s + 1) * MOD_ROWS),
             jnp.maximum(first_sub + pl.program_id(0) * n_sub + s - (N_PROMPT // MOD_ROWS - 1), 0)) for s in range(n_sub)]

    @pl.when(pl.program_id(1) == 0)
    def _():
        for rows, g in subs:
            x = x_ref[rows, :]
            o_ref[rows, :] = x
            h_ref[rows, :] = (_rms(x) * (1.0 + sc_ref[pl.ds(g, 1), :]) + sh_ref[pl.ds(g, 1), :]).astype(BF16)

    wg = wg_ref[...].astype(BF16)
    wu = wu_ref[...].astype(BF16)
    wd = wd_ref[...].astype(BF16)
    for rows, g in subs:
        h = h_ref[rows, :]
        a = _mm(h, wg)
        act = (a * jax.nn.sigmoid(a) * _mm(h, wu)).astype(BF16)
        o_ref[rows, :] += gate_ref[pl.ds(g, 1), :] * _mm(act, wd)

    if final:
        @pl.when(pl.program_id(1) == pl.num_programs(1) - 1)
        def _():
            for rows, _ in subs:
                o_ref[rows, :] = _rms(o_ref[rows, :]) * gf_ref[...]


def _ffn(x, row0, m, mod, layer, w_in, w_out, final_gain=None):
    nj = FFN_H // FFN_TN
    tile0 = row0 // TM_FFN
    mod_spec = lambda k: pl.BlockSpec((None, None, 8, D_MODEL), lambda i, j: (layer, k, 0, 0))
    final = final_gain is not None
    extra_specs = [pl.BlockSpec((1, D_MODEL), lambda i, j: (0, 0))] if final else []
    extra_args = [final_gain.reshape(1, D_MODEL)] if final else []
    return pl.pallas_call(
        functools.partial(_ffn_kernel, first_sub=row0 // MOD_ROWS, final=final),
        grid=(m // TM_FFN, nj),
        in_specs=[pl.BlockSpec((TM_FFN, D_MODEL), lambda i, j: (tile0 + i, 0)), mod_spec(3), mod_spec(4), mod_spec(5),
                  pl.BlockSpec((None, D_MODEL, FFN_TN), lambda i, j: (layer, 0, j)),
                  pl.BlockSpec((None, D_MODEL, FFN_TN), lambda i, j: (layer, 0, j + nj)),
                  pl.BlockSpec((None, FFN_TN, D_MODEL), lambda i, j: (layer, j, 0))] + extra_specs,
        out_specs=pl.BlockSpec((TM_FFN, D_MODEL), lambda i, j: (i, 0)),
        out_shape=jax.ShapeDtypeStruct((m, D_MODEL), F32),
        scratch_shapes=[pltpu.VMEM((TM_FFN, D_MODEL), BF16)],
        compiler_params=_params(2),
        name="ffn_residual",
    )(x, mod, mod, mod, w_in, w_in, w_out, *extra_args)


def _proj_res_kernel(ac0_ref, ac1_ref, as0_ref, as1_ref, w0_ref, w1_ref, xc_ref, xs_ref, gate_ref, o_ref, *, n_c):
    i = pl.program_id(0)

    def mix(a0_ref, a1_ref, x_ref, g):
        acc = _mm(a0_ref[...], w0_ref[...].astype(BF16)) + _mm(a1_ref[...], w1_ref[...].astype(BF16))
        return x_ref[...] + gate_ref[pl.ds(g, 1), :] * acc

    @pl.when(i < n_c)
    def _():
        o_ref[...] = mix(ac0_ref, ac1_ref, xc_ref, 0)

    @pl.when(i >= n_c)
    def _():
        o_ref[...] = mix(as0_ref, as1_ref, xs_ref, 1 + (i - n_c))


def _out_proj(acts_c, acts_s, w, w_layer, xc, xs, s_row0, mod, layer, k_gate):
    n_c, n_s = N_PROMPT // TM, N_SAMPLE // TM
    kw = acts_c[0].shape[1]
    xc_idx, xs_idx = _stream_index_maps(TM, s_row0)
    c_idx, s_idx = _stream_index_maps(TM, 0)
    w_specs = [pl.BlockSpec((None, kw, D_MODEL), functools.partial(lambda i, p: (w_layer, p, 0), p=p),
                            pipeline_mode=pl.Buffered(1)) for p in range(2)]
    return pl.pallas_call(
        functools.partial(_proj_res_kernel, n_c=n_c),
        grid=(n_c + n_s,),
        in_specs=[pl.BlockSpec((TM, kw), c_idx)] * 2 + [pl.BlockSpec((TM, kw), s_idx)] * 2 + w_specs + [
            pl.BlockSpec((TM, D_MODEL), xc_idx), pl.BlockSpec((TM, D_MODEL), xs_idx),
            pl.BlockSpec((None, None, 8, D_MODEL), lambda i: (layer, k_gate, 0, 0)),
        ],
        out_specs=pl.BlockSpec((TM, D_MODEL), lambda i: (i, 0)),
        out_shape=jax.ShapeDtypeStruct((N_PROMPT + N_SAMPLE, D_MODEL), F32),
        compiler_params=_params(1),
        name="out_proj_residual",
    )(*acts_c, *acts_s, w, w, xc, xs, mod)


def _gqa_kernel(*refs, sample, has_prev=False, slot=0):
    if sample:
        q_ref, k_ref, v_ref, gq_ref, gk_ref, ck_ref, cv_ref, cos_ref, sin_ref, o_ref, kb_ref, vb_ref = refs
    else:
        n_in = 8 if has_prev else 6
        q_ref, k_ref, v_ref, gq_ref, gk_ref, vf_ref = refs[:6]
        o_ref, kc_ref, vc_ref, kb_ref, vb_ref = refs[n_in:]
        if not has_prev:
            stacks = (kc_ref, vc_ref)
            kc_ref, vc_ref = kc_ref.at[slot], vc_ref.at[slot]
    qi = pl.program_id(1)
    n_new = k_ref.shape[0]
    past = PAST_LEN if sample else 0
    rep = H_B // KV_B

    @pl.when(qi == 0)
    def _():
        if not (sample or has_prev):
            for ref in stacks:
                for other in range(ref.shape[0]):
                    if other != slot:
                        ref[other] = jnp.zeros(ref.shape[1:], ref.dtype)
        for g in range(KV_B):
            sl = slice(HD_B * g, HD_B * (g + 1))
            kn = _rms(k_ref[:, sl].astype(F32)) * gk_ref[...]
            if sample:
                kb_ref[0:past, sl] = ck_ref[:, g, :].astype(BF16)
                vb_ref[g, 0:past, 0:HD_B] = cv_ref[:, g, :].astype(BF16)
                kn = kn * cos_ref[...] + pltpu.roll(kn, HD_B // 2, 1) * sin_ref[...]
            else:
                kc_ref[:, g, :] = kn
                vc_ref[:, g, :] = vf_ref[:, sl]
            kb_ref[past:past + n_new, sl] = kn.astype(BF16)
            vb_ref[g, past:past + n_new, 0:HD_B] = v_ref[:, sl].astype(BF16)
            vb_ref[g, :, HD_B:] = jnp.ones((past + n_new, HD_B), BF16)

    qb = q_ref.shape[0]
    r0 = pl.multiple_of(qi * qb, qb)
    qs = []
    for h in range(H_B):
        qn = _rms(q_ref[:, HD_B * h:HD_B * (h + 1)].astype(F32)) * gq_ref[...]
        if sample:
            qn = qn * cos_ref[pl.ds(r0, qb), :] + pltpu.roll(qn, HD_B // 2, 1) * sin_ref[pl.ds(r0, qb), :]
        qs.append((qn * (HD_B ** -0.5)).astype(BF16))
    scores = [_nt(qs[h], kb_ref[:, HD_B * (h // rep):HD_B * (h // rep + 1)]) for h in range(H_B)]
    weights = [jnp.exp(s - jnp.max(s, axis=-1, keepdims=True)).astype(BF16) for s in scores]
    sums = [_mm(weights[h], vb_ref[h // rep]) for h in range(H_B)]
    for h in range(H_B):
        o_ref[:, HD_B * h:HD_B * (h + 1)] = (sums[h][:, :HD_B] / sums[h][:, HD_B:]).astype(o_ref.dtype)


def _gqa(proj, g_q, g_k, n_batch, seq, ctx=None, rope=None, v_f32=None, slot=0, prev=None):
    sample = ctx is not None
    m = n_batch * seq
    qb = min(QB_GQA, seq)
    nq = seq // qb
    n_even = (DEPTH + 1) // 2
    in_specs = [
        pl.BlockSpec((qb, 512), lambda b, i: (b * nq + i, 3)),
        pl.BlockSpec((seq, 256), lambda b, i: (b, 8)),
        pl.BlockSpec((seq, 256), lambda b, i: (b, 9)),
        pl.BlockSpec((None, 1, HD_B), lambda b, i: (slot, 0, 0)),
        pl.BlockSpec((None, 1, HD_B), lambda b, i: (slot, 0, 0)),
    ]
    args = [proj, proj, proj, g_q, g_k]
    o_spec = pl.BlockSpec((qb, 512), lambda b, i: (b * nq + i, 0))
    o_shape = jax.ShapeDtypeStruct((m, 512), BF16)
    aliases = {}
    if sample:
        cache_spec = pl.BlockSpec((None, None, PAST_LEN, KV_B, HD_B), lambda b, i: (b, slot, 0, 0, 0))
        in_specs += [
            cache_spec, cache_spec,
            pl.BlockSpec((seq, HD_B), lambda b, i: (0, 0)),
            pl.BlockSpec((seq, HD_B), lambda b, i: (0, 0)),
        ]
        args += [ctx[0], ctx[1], rope[0], rope[1]]
        out_specs, out_shape = o_spec, o_shape
    else:
        in_specs.append(pl.BlockSpec((seq, KV_B * HD_B), lambda b, i: (b, 0)))
        args.append(v_f32)
        if prev is not None:
            in_specs += [pl.BlockSpec(memory_space=pl.ANY)] * 2
            aliases = {len(args): 1, len(args) + 1: 2}
            args += list(prev)
        if prev is None:
            new_spec = pl.BlockSpec((None, n_even, seq, KV_B, HD_B), lambda b, i: (b, 0, 0, 0, 0))
        else:
            new_spec = pl.BlockSpec((None, None, seq, KV_B, HD_B), lambda b, i: (b, slot, 0, 0, 0))
        new_shape = jax.ShapeDtypeStruct((n_batch, n_even, seq, KV_B, HD_B), F32)
        out_specs, out_shape = [o_spec, new_spec, new_spec], [o_shape, new_shape, new_shape]
    n_keys = seq + (PAST_LEN if sample else 0)
    return pl.pallas_call(
        functools.partial(_gqa_kernel, sample=sample, has_prev=prev is not None, slot=slot),
        grid=(n_batch, nq),
        in_specs=in_specs,
        out_specs=out_specs,
        out_shape=out_shape,
        input_output_aliases=aliases,
        scratch_shapes=[pltpu.VMEM((n_keys, KV_B * HD_B), BF16), pltpu.VMEM((KV_B, n_keys, 2 * HD_B), BF16)],
        compiler_params=_params(2),
        name="gqa_sample" if sample else "gqa_prompt",
    )(*args)


MLA_QW = 2 * LANES
MLA_HW = 4 * LANES


def _rotate_pairs(x, cos_t, sin_lo, sin_hi):
    w = x.shape[1]
    return x * cos_t + pltpu.roll(x, ROPE_D // 2, 1) * sin_hi + pltpu.roll(x, w - ROPE_D // 2, 1) * sin_lo


def _mla_kernel(*refs, sample, has_prev=False, slot=0):
    if sample:
        (cq_ref, ckv_ref, kpe_ref, gq_ref, wqb_ref, gkv_ref, wkvb_ref, cckv_ref, ckpe_ref,
         qc_ref, ql_ref, qh_ref, kc_ref, kl_ref, kh_ref, o_ref, kv_s) = refs
    else:
        cq_ref, ckv_ref, kpe_ref, gq_ref, wqb_ref, gkv_ref, wkvb_ref, kf_ref = refs[:8]
        o_ref, ckvn_ref, kpeo_ref, kv_s = refs[8 + (2 if has_prev else 0):]
    qi = pl.program_id(1)
    n_new = ckv_ref.shape[0]
    past = PAST_LEN if sample else 0

    def stage_kv(rows, kv, kpe_block):
        for h in range(H_D):
            kv_s[rows, MLA_HW * h:MLA_HW * h + NOPE_D] = kv[:, 256 * h:256 * h + NOPE_D].astype(BF16)
            kv_s[rows, MLA_HW * h + NOPE_D:MLA_HW * h + MLA_QW] = kpe_block
            kv_s[rows, MLA_HW * h + MLA_QW:MLA_HW * h + MLA_QW + V_D] = kv[:, 256 * h + NOPE_D:256 * (h + 1)].astype(BF16)

    @pl.when(qi == 0)
    def _():
        wkvb = wkvb_ref[...].astype(BF16)
        ckvn = _rms(ckv_ref[...].astype(F32)) * gkv_ref[...]
        if not sample:
            _store_slot(ckvn_ref, slot, not has_prev, ckvn)
            _store_slot(kpeo_ref, slot, not has_prev, kf_ref[:, 0:ROPE_D])
        kpe = kpe_ref[...]
        if sample:
            ctx_kpe = jnp.concatenate([ckpe_ref[...], jnp.zeros((past, LANES - ROPE_D), F32)], axis=1)
            stage_kv(slice(0, past), _mm(cckv_ref[...].astype(BF16), wkvb), ctx_kpe.astype(BF16))
            kpe = _rotate_pairs(kpe.astype(F32), kc_ref[...], kl_ref[...], kh_ref[...]).astype(BF16)
        stage_kv(slice(past, past + n_new), _mm(ckvn.astype(BF16), wkvb), kpe)
        for h in range(H_D):
            kv_s[:, MLA_HW * h + MLA_QW + V_D:MLA_HW * (h + 1)] = jnp.ones((past + n_new, V_D), BF16)

    q = _mm((_rms(cq_ref[...].astype(F32)) * gq_ref[...]).astype(BF16), wqb_ref[...].astype(BF16))
    q = q * ((NOPE_D + ROPE_D) ** -0.5)
    qs = []
    for h in range(H_D):
        q_h = q[:, MLA_QW * h:MLA_QW * (h + 1)]
        if sample:
            qb = cq_ref.shape[0]
            rows = pl.ds(pl.multiple_of(qi * qb, qb), qb)
            q_h = _rotate_pairs(q_h, qc_ref[rows, :], ql_ref[rows, :], qh_ref[rows, :])
        qs.append(q_h.astype(BF16))
    scores = [_nt(qs[h], kv_s[:, MLA_HW * h:MLA_HW * h + MLA_QW]) for h in range(H_D)]
    weights = [jnp.exp(s - jnp.max(s, axis=-1, keepdims=True)).astype(BF16) for s in scores]
    sums = [_mm(weights[h], kv_s[:, MLA_HW * h + MLA_QW:MLA_HW * (h + 1)]) for h in range(H_D)]
    for h in range(H_D):
        o_ref[:, V_D * h:V_D * (h + 1)] = (sums[h][:, :V_D] / sums[h][:, V_D:]).astype(o_ref.dtype)


def _mla(proj, g_q, w_qb, g_kv, w_kvb, n_batch, seq, ctx=None, rope=None, kpe_f32=None, slot=0, prev=None):
    sample = ctx is not None
    m = n_batch * seq
    qb = min(QB_MLA, seq)
    nq = seq // qb
    in_specs = [
        pl.BlockSpec((qb, Q_RANK), lambda b, i: (b * nq + i, 6)),
        pl.BlockSpec((seq, KV_RANK), lambda b, i: (b, 14)),
        pl.BlockSpec((seq, LANES), lambda b, i: (b, 15)),
        pl.BlockSpec((None, 1, Q_RANK), lambda b, i: (slot, 0, 0)),
        pl.BlockSpec((None, Q_RANK, H_D * MLA_QW), lambda b, i: (slot, 0, 0)),
        pl.BlockSpec((None, 1, KV_RANK), lambda b, i: (slot, 0, 0)),
        pl.BlockSpec((None, KV_RANK, 1024), lambda b, i: (slot, 0, 0)),
    ]
    args = [proj, proj, proj, g_q, w_qb, g_kv, w_kvb]
    o_spec = pl.BlockSpec((qb, 512), lambda b, i: (b * nq + i, 0))
    o_shape = jax.ShapeDtypeStruct((m, 512), BF16)
    aliases = {}
    if sample:
        in_specs += [
            pl.BlockSpec((None, None, PAST_LEN, KV_RANK), lambda b, i: (b, slot, 0, 0)),
            pl.BlockSpec((None, None, PAST_LEN, ROPE_D), lambda b, i: (b, slot, 0, 0)),
        ] + [pl.BlockSpec((seq, MLA_QW), lambda b, i: (0, 0))] * 3 + [pl.BlockSpec((seq, LANES), lambda b, i: (0, 0))] * 3
        args += [ctx[0], ctx[1], *rope]
        out_specs, out_shape = o_spec, o_shape
    else:
        in_specs.append(pl.BlockSpec((seq, LANES), lambda b, i: (b, 0)))
        args.append(kpe_f32)
        if prev is not None:
            in_specs += [pl.BlockSpec(memory_space=pl.ANY)] * 2
            aliases = {len(args): 1, len(args) + 1: 2}
            args += list(prev)
        ckv_spec, ckv_shape = _stack_out(n_batch, DEPTH // 2, (seq, KV_RANK), slot, prev is None)
        kpe_spec, kpe_shape = _stack_out(n_batch, DEPTH // 2, (seq, ROPE_D), slot, prev is None)
        out_specs, out_shape = [o_spec, ckv_spec, kpe_spec], [o_shape, ckv_shape, kpe_shape]
    n_keys = seq + (PAST_LEN if sample else 0)
    return pl.pallas_call(
        functools.partial(_mla_kernel, sample=sample, has_prev=prev is not None, slot=slot),
        grid=(n_batch, nq),
        in_specs=in_specs,
        out_specs=out_specs,
        out_shape=out_shape,
        input_output_aliases=aliases,
        scratch_shapes=[pltpu.VMEM((n_keys, H_D * MLA_HW), BF16)],
        compiler_params=_params(2),
        name="mla_sample" if sample else "mla_prompt",
    )(*args)


def _dft(table, x):
    return _mm(table.astype(BF16), x.astype(BF16))


def _filter_kernel(z_ref, wf1_ref, bf1_ref, fr_ref, wf2_ref, bf2_ref, wf3_ref, t_ref, dl_ref,
                   c_ref, s_ref, gre_ref, gim_ref):
    n_tok = z_ref.shape[0]
    fr = fr_ref[...]
    hid = jnp.sin(fr * (_mm(z_ref[...].astype(BF16), wf1_ref[...].astype(BF16)) + bf1_ref[...]))
    hid = jnp.sin(fr * (_mm(hid.astype(BF16), wf2_ref[...].astype(BF16)) + bf2_ref[...]))
    filt = _mm(hid.astype(BF16), wf3_ref[...].astype(BF16))
    decay = jnp.exp(-t_ref[...] * dl_ref[...])
    row = lax.broadcasted_iota(jnp.int32, (n_tok, 1), 0)
    h_f = filt[:, :HY_W] * decay
    h_b = jnp.where(row == 0, 0.0, filt[:, HY_W:] * decay)
    p, m = h_f + h_b, h_f - h_b
    g_re = _dft(c_ref[...], p)
    g_im = _dft(s_ref[...], m)
    sign = jnp.where(row % 2 == 0, 1.0, -1.0)
    nyquist = jnp.sum(p * sign, axis=0, keepdims=True)
    g_im = jnp.where(row == 0, nyquist, g_im)
    wk = jnp.where(row == 0, 0.5 / n_tok, 1.0 / n_tok)
    gre_ref[...] = g_re * wk
    gim_ref[...] = g_im * wk


def _filter_spectrum(z, wf1, bf1, freq, wf2, bf2, wf3, t_col, deltas, tabs):
    n_layers, n_tok = wf1.shape[0], z.shape[0]
    shared = lambda a: pl.BlockSpec(a.shape, lambda l: (0,) * a.ndim)
    per_layer = lambda a: pl.BlockSpec((None,) + a.shape[1:], lambda l: (l,) + (0,) * (a.ndim - 1))
    row = lambda a: a.reshape(n_layers, 1, FILT_HID)
    args = [z, wf1, row(bf1), row(freq), wf2, row(bf2), wf3, t_col, deltas, tabs[0], tabs[1]]
    layered = [False, True, True, True, True, True, True, False, False, False, False]
    out = jax.ShapeDtypeStruct((n_layers, n_tok, HY_W), F32)
    out_spec = pl.BlockSpec((None, n_tok, HY_W), lambda l: (l, 0, 0))
    return pl.pallas_call(
        _filter_kernel,
        grid=(n_layers,),
        in_specs=[per_layer(a) if lay else shared(a) for a, lay in zip(args, layered)],
        out_specs=[out_spec, out_spec],
        out_shape=[out, out],
        compiler_params=_params(1),
        name="hyena_filter",
    )(*args)


HY_ROWS = 1024


HY_CT = HY_W


def _hyena_kernel(u0_ref, u1_ref, u2_ref, w0_ref, w1_ref, w2_ref, b0_ref, b1_ref, b2_ref, skip_ref,
                  gre_ref, gim_ref, cf_ref, sf_ref, stf_ref, o_ref, c_ref, s_ref, st_ref):
    seq = c_ref.shape[0]
    n_rows = u0_ref.shape[0]
    n_seq = n_rows // seq
    pos = lax.broadcasted_iota(jnp.int32, (n_rows, 1), 0) % seq

    @pl.when((pl.program_id(0) == 0) & (pl.program_id(1) == 0))
    def _():
        c_ref[...] = cf_ref[...].astype(BF16)
        s_ref[...] = sf_ref[...].astype(BF16)
        st_ref[...] = stf_ref[...].astype(BF16)

    def short_conv(u_ref, w_ref, b_ref):
        x, w = u_ref[...].astype(F32), w_ref[...]
        prev = jnp.where(pos == 0, 0.0, pltpu.roll(x, 1, 0))
        nxt = jnp.where(pos == seq - 1, 0.0, pltpu.roll(x, n_rows - 1, 0))
        return prev * w[0:1] + x * w[1:2] + nxt * w[2:3] + b_ref[...]

    def side_by_side(a):
        return a if n_seq == 1 else jnp.concatenate([a[s * seq:(s + 1) * seq] for s in range(n_seq)], axis=1)

    def stacked(a):
        ct = a.shape[1] // n_seq
        return a if n_seq == 1 else jnp.concatenate([a[:, s * ct:(s + 1) * ct] for s in range(n_seq)], axis=0)

    x0 = short_conv(u0_ref, w0_ref, b0_ref)
    gv = short_conv(u1_ref, w1_ref, b1_ref) * short_conv(u2_ref, w2_ref, b2_ref)
    sig = side_by_side(gv).astype(BF16)
    u_re = _mm(c_ref[...], sig)
    u_im = _mm(s_ref[...], sig)
    g_re = jnp.concatenate([gre_ref[...]] * n_seq, axis=1)
    g_im = jnp.concatenate([gim_ref[...]] * n_seq, axis=1)
    bin0 = lax.broadcasted_iota(jnp.int32, (seq, 1), 0) == 0
    p_im = u_im * g_im
    y_re = u_re * g_re - jnp.where(bin0, 0.0, p_im)
    y_im = jnp.where(bin0, p_im, u_re * g_im + u_im * g_re)
    y = stacked(_mm(c_ref[...], y_re.astype(BF16)) + _mm(st_ref[...], y_im.astype(BF16)))
    o_ref[...] = (x0 * (y + gv * skip_ref[...])).astype(o_ref.dtype)


def _hyena(proj, layer, w_conv, b_conv, skip, g_re, g_im, tabs, n_batch, seq):
    ct = HY_CT
    nct = HY_W // ct
    u_specs = [pl.BlockSpec((HY_ROWS, ct), functools.partial(lambda b, c, g: (b, g * nct + c), g=g)) for g in range(3)]
    w_specs = [pl.BlockSpec((None, 3, ct), functools.partial(lambda b, c, g: (layer, 0, g * nct + c), g=g)) for g in range(3)]
    b_specs = [pl.BlockSpec((None, 1, ct), functools.partial(lambda b, c, g: (layer, 0, g * nct + c), g=g)) for g in range(3)]
    tab_spec = pl.BlockSpec((seq, seq), lambda b, c: (0, 0))
    return pl.pallas_call(
        _hyena_kernel,
        grid=(n_batch * seq // HY_ROWS, nct),
        in_specs=u_specs + w_specs + b_specs + [
            pl.BlockSpec((None, 1, ct), lambda b, c: (layer, 0, c)),
            pl.BlockSpec((None, seq, ct), lambda b, c: (layer, 0, c)),
            pl.BlockSpec((None, seq, ct), lambda b, c: (layer, 0, c)),
        ] + [tab_spec] * 3,
        out_specs=pl.BlockSpec((HY_ROWS, ct), lambda b, c: (b, c)),
        out_shape=jax.ShapeDtypeStruct((n_batch * seq, HY_W), BF16),
        scratch_shapes=[pltpu.VMEM((seq, seq), BF16)] * 3,
        compiler_params=_params(2),
        name="hyena_conv",
    )(proj, proj, proj, w_conv, w_conv, w_conv, b_conv, b_conv, b_conv, skip, g_re, g_im, *tabs)


def _dft_tables(n_tok):
    k = np.arange(n_tok)[:, None]
    s = np.arange(n_tok)[None, :]
    ang = ((k * s) % (2 * n_tok)) * (np.pi / n_tok)
    cos_t = np.cos(ang)
    sin_f = np.where(k == 0, np.where(s % 2 == 0, 1.0, -1.0), -np.sin(ang))
    return [jnp.asarray(t, F32) for t in (cos_t, sin_f, sin_f.T)]


GLA_LEVELS = (32, 16, 8, 4, 2, 1)
GLA_SAFE_DECAY = 60.0
GLA_GROUP = 4


def _gla_constants():
    c = GLA_CHUNK
    idx = np.arange(c)
    i, t = idx[:, None], idx[None, :]
    masks = []
    for s in GLA_LEVELS:
        upper = (idx % (2 * s)) >= s
        masks.append(((i // (2 * s)) == (t // (2 * s))) & upper[:, None] & (~upper)[None, :])
    masks.append(i == t)
    tri = t <= i
    fwd_m = np.stack([np.tile(m, (H_A, 1)) for m in masks]).astype(np.float32)
    bwd_m = np.stack([np.tile(m[::-1, ::-1], (H_A, 1)) for m in masks]).astype(np.float32)
    head_of_row = np.repeat(np.arange(H_A), c)[:, None]
    head_of_lane = np.repeat(np.arange(H_A), DK_A)[None, :]
    head_mask = head_of_row == head_of_lane
    return (jnp.asarray(tri, BF16), jnp.asarray(tri[::-1, ::-1], BF16), jnp.asarray(fwd_m), jnp.asarray(bwd_m),
            jnp.asarray(head_mask, BF16))


def _pair_reference(b, s, backward, row):
    c = GLA_CHUNK
    ref = s if backward else s - 1
    if 2 * s >= 8:
        pieces = [jnp.broadcast_to(b[p * 2 * s + ref:p * 2 * s + ref + 1, :], (2 * s, b.shape[1]))
                  for p in range(c // (2 * s))]
        return pieces[0] if len(pieces) == 1 else jnp.concatenate(pieces, axis=0)
    pos = row % (2 * s)
    out = None
    for o in range(2 * s):
        d = ref - o
        shifted = b if d == 0 else pltpu.roll(b, (-d) % c, 0)
        out = shifted if out is None else jnp.where(pos == o, shifted, out)
    return out


def _chunk_log_decay(la, t_ref):
    l1 = la.astype(BF16)
    r1 = la - l1.astype(F32)
    l2 = r1.astype(BF16)
    l3 = (r1 - l2.astype(F32)).astype(BF16)
    tmat = t_ref[...]
    return _mm(tmat, l1) + _mm(tmat, l2) + _mm(tmat, l3)


def _stack_heads(a, hm):
    ab = a.astype(BF16)
    return jnp.concatenate([ab] * H_A, axis=0) * hm


def _state_terms(k, v, b, b_last):
    c = GLA_CHUNK
    k_rest = (k * jnp.exp(b_last - b)).T
    carry = jnp.broadcast_to(jnp.exp(b_last), (2 * c, b.shape[1])).T
    return k_rest.astype(BF16), carry


def _gla_chunk(q, k, v, la, t_ref, m_ref, hm, s_ref, backward):
    c = GLA_CHUNK
    b = _chunk_log_decay(la, t_ref)
    row = lax.broadcasted_iota(jnp.int32, (c, 1), 0)
    last = 0 if backward else c - 1
    b_last = b[last:last + 1, :]
    scores = _nt(_stack_heads(q, hm), k.astype(BF16)) * m_ref[len(GLA_LEVELS)]
    for lvl, s in enumerate(GLA_LEVELS):
        is_query = (row % (2 * s) < s) if backward else (row % (2 * s) >= s)
        delta = b - _pair_reference(b, s, backward, row)
        x = jnp.exp(jnp.where(is_query, delta, -delta))
        scores = scores + _nt(_stack_heads(q * x, hm), (k * x).astype(BF16)) * m_ref[lvl]
    scores = scores.astype(BF16)
    state = s_ref[...]
    inter = _mm(_stack_heads(q * jnp.exp(b), hm), state.astype(BF16))
    k_rest, carry = _state_terms(k, v, b, b_last)
    outs = []
    for h in range(H_A):
        rows = slice(c * h, c * (h + 1))
        v_h = v[:, DV_A * h:DV_A * (h + 1)]
        outs.append(_mm(scores[rows], v_h) + inter[rows])
        s_ref[rows, :] = state[rows] * carry[rows] + _mm(k_rest[rows], v_h)
    return jnp.concatenate(outs, axis=1)


def _gla_local(items, hm):
    c = GLA_CHUNK
    bs = [_chunk_log_decay(la, t_ref) for _, _, _, la, t_ref, _, _ in items]
    b_lasts = [b[(0 if it[6] else c - 1):(0 if it[6] else c - 1) + 1, :] for b, it in zip(bs, items)]
    q_decayed = [_stack_heads(it[0] * jnp.exp(b), hm) for it, b in zip(items, bs)]
    k_grown = [(it[1] * jnp.exp(-b)).astype(BF16) for it, b in zip(items, bs)]
    raw = [_nt(qd, kg) for qd, kg in zip(q_decayed, k_grown)]
    masked = [r * it[5] for r, it in zip(raw, items)]
    terms = [_state_terms(it[1], it[2], b, bl) for it, b, bl in zip(items, bs, b_lasts)]
    out = []
    for n in range(0, len(items), 2):
        v = items[n][2]
        heads = [(slice(c * h, c * (h + 1)), v[:, DV_A * h:DV_A * (h + 1)]) for h in range(H_A)]
        both = (masked[n] + masked[n + 1]).astype(BF16)
        intra = jnp.concatenate([_mm(both[rows], v_h) for rows, v_h in heads], axis=1)
        for m in (n, n + 1):
            k_rest, carry = terms[m]
            incr = jnp.concatenate([_mm(k_rest[rows], v_h) for rows, v_h in heads], axis=0)
            out.append((intra if m == n else None, q_decayed[m], incr, carry))
    return out


def _gla_kernel(*refs, sample, has_prev=False, slot=0):
    if sample:
        (x_ref, z_ref, wf_ref, bf_ref, wb_ref, bb_ref, tf_ref, tb_ref, mf_ref, mb_ref, hm_ref, gn_ref, sf0_ref, sb0_ref,
         o_ref, la_f, la_b, o_f, o_b, s_f, s_b, qd_f, qd_b, ds_f, ds_b, cr_f, cr_b) = refs
    else:
        x_ref, z_ref, wf_ref, bf_ref, wb_ref, bb_ref, tf_ref, tb_ref, mf_ref, mb_ref, hm_ref, gn_ref = refs[:12]
        (o_ref, sf_out, sb_out, la_f, la_b, o_f, o_b, s_f, s_b,
         qd_f, qd_b, ds_f, ds_b, cr_f, cr_b) = refs[12 + (2 if has_prev else 0):]
    n_tok = x_ref.shape[0]
    n_chunks = n_tok // GLA_CHUNK
    hk, hv = H_A * DK_A, H_A * DV_A
    zb = z_ref[...].astype(BF16)

    def log_sigmoid(t):
        return jnp.minimum(t, 0.0) - jnp.log(1.0 + jnp.exp(-jnp.abs(t)))

    la_f[...] = log_sigmoid(_mm(zb, wf_ref[...].astype(BF16)) + bf_ref[...]) / GLA_TAU
    la_b[...] = log_sigmoid(_mm(zb, wb_ref[...].astype(BF16)) + bb_ref[...]) / GLA_TAU
    if sample:
        s_f[...] = sf0_ref[...]
        s_b[...] = sb0_ref[...]
    else:
        s_f[...] = jnp.zeros_like(s_f)
        s_b[...] = jnp.zeros_like(s_b)
    hm = hm_ref[...]

    fwd = (la_f, tf_ref, mf_ref, s_f, o_f, qd_f, ds_f, cr_f, False)
    bwd = (la_b, tb_ref, mb_ref, s_b, o_b, qd_b, ds_b, cr_b, True)
    tri_f = jnp.sum(mf_ref[...], axis=0)
    tri_b = jnp.sum(mb_ref[...], axis=0)

    def chunk_rows(ci, backward):
        cidx = n_chunks - 1 - ci if backward else ci
        return cidx, pl.ds(pl.multiple_of(cidx * GLA_CHUNK, GLA_CHUNK), GLA_CHUNK)

    def load_qkv(rows):
        q = x_ref[rows, 0:hk].astype(F32) * (DK_A ** -0.5)
        return q, x_ref[rows, hk:2 * hk].astype(F32), x_ref[rows, 2 * hk:2 * hk + hv]

    def safe_step(ci, carry):
        for la_ref, t_ref, m_ref, s_ref, out_ref, _, _, _, backward in (fwd, bwd):
            _, rows = chunk_rows(ci, backward)
            out_ref[rows, :] = _gla_chunk(*load_qkv(rows), la_ref[rows, :], t_ref, m_ref, hm, s_ref, backward)
        return carry

    def local_step(gi, carry):
        items, dests = [], []
        for u in range(GLA_GROUP):
            cidx = gi * GLA_GROUP + u
            rows = pl.ds(pl.multiple_of(cidx * GLA_CHUNK, GLA_CHUNK), GLA_CHUNK)
            qkv = load_qkv(rows)
            for (la_ref, t_ref, _, _, out_ref, qd_ref, ds_ref, cr_ref, backward), tri in ((fwd, tri_f), (bwd, tri_b)):
                items.append((*qkv, la_ref[rows, :], t_ref, tri, backward))
                dests.append((out_ref, rows, qd_ref, ds_ref, cr_ref, cidx))
        for (out_ref, rows, qd_ref, ds_ref, cr_ref, cidx), (intra, qd, incr, factor) in zip(dests, _gla_local(items, hm)):
            out_ref[rows, :] = jnp.zeros((GLA_CHUNK, hv), F32) if intra is None else intra
            qd_ref[cidx] = qd
            ds_ref[cidx] = incr
            cr_ref[cidx] = factor
        return carry

    def scan_step(ci, carry):
        for _, _, _, s_ref, out_ref, qd_ref, ds_ref, cr_ref, backward in (fwd, bwd):
            cidx, rows = chunk_rows(ci, backward)
            state = s_ref[...]
            inter = _mm(qd_ref[cidx], state.astype(BF16))
            out_ref[rows, :] += jnp.concatenate(
                [inter[GLA_CHUNK * h:GLA_CHUNK * (h + 1)] for h in range(H_A)], axis=1)
            s_ref[...] = state * cr_ref[cidx] + ds_ref[cidx]
        return carry

    chunk_sums = [jnp.sum(ref[...].reshape(n_chunks, GLA_CHUNK, hk), axis=1) for ref in (la_f, la_b)]
    mild = jnp.minimum(jnp.min(chunk_sums[0]), jnp.min(chunk_sums[1])) > -GLA_SAFE_DECAY

    @pl.when(mild)
    def _():
        lax.fori_loop(0, n_chunks // GLA_GROUP, local_step, 0, unroll=2)
        lax.fori_loop(0, n_chunks, scan_step, 0, unroll=2)

    @pl.when(jnp.logical_not(mild))
    def _():
        lax.fori_loop(0, n_chunks, safe_step, 0)
    if not sample:
        _store_slot(sf_out, slot, not has_prev, s_f[...])
        _store_slot(sb_out, slot, not has_prev, s_b[...])
    gain = gn_ref[...]
    for h in range(H_A):
        cols = slice(DV_A * h, DV_A * (h + 1))
        r = x_ref[:, 2 * hk + hv + DV_A * h:2 * hk + hv + DV_A * (h + 1)].astype(F32)
        o_ref[:, cols] = (_rms(o_f[:, cols] + o_b[:, cols]) * gain * (r * jax.nn.sigmoid(r))).astype(o_ref.dtype)


def _gla(proj, w_gf, b_gf, w_gb, b_gb, g_norm, consts, n_batch, seq, ctx=None, slot=0, prev=None):
    sample = ctx is not None
    hk, hv = H_A * DK_A, H_A * DV_A
    n_ch = seq // GLA_CHUNK
    full = lambda shape: pl.BlockSpec(shape, lambda b: (0,) * len(shape))
    layered = lambda shape: pl.BlockSpec((None,) + shape, lambda b: (slot,) + (0,) * len(shape))
    in_specs = [
        pl.BlockSpec((seq, 2 * hk + 2 * hv), lambda b: (b, 0)),
        pl.BlockSpec((seq, LANES), lambda b: (b, EVEN_W // LANES - 1)),
        layered((LANES, hk)), layered((1, hk)), layered((LANES, hk)), layered((1, hk)),
        full(consts[0].shape), full(consts[1].shape), full(consts[2].shape), full(consts[3].shape), full(consts[4].shape),
        layered((1, DV_A)),
    ]
    args = [proj, proj, w_gf, b_gf, w_gb, b_gb, *consts, g_norm]
    o_spec = pl.BlockSpec((seq, hv), lambda b: (b, 0))
    o_shape = jax.ShapeDtypeStruct((n_batch * seq, hv), BF16)
    aliases = {}
    if sample:
        st_spec = pl.BlockSpec((None, None, hk, DV_A), lambda b: (b, slot, 0, 0))
        in_specs += [st_spec, st_spec]
        args += [ctx[0], ctx[1]]
        out_specs, out_shape = o_spec, o_shape
    else:
        if prev is not None:
            in_specs += [pl.BlockSpec(memory_space=pl.ANY)] * 2
            aliases = {len(args): 1, len(args) + 1: 2}
            args += list(prev)
        st_spec, st_shape = _stack_out(n_batch, (DEPTH + 1) // 2, (hk, DV_A), slot, prev is None)
        out_specs, out_shape = [o_spec, st_spec, st_spec], [o_shape, st_shape, st_shape]
    return pl.pallas_call(
        functools.partial(_gla_kernel, sample=sample, has_prev=prev is not None, slot=slot),
        grid=(n_batch,),
        in_specs=in_specs,
        out_specs=out_specs,
        out_shape=out_shape,
        input_output_aliases=aliases,
        scratch_shapes=[pltpu.VMEM((seq, hk), F32), pltpu.VMEM((seq, hk), F32),
                        pltpu.VMEM((seq, hv), F32), pltpu.VMEM((seq, hv), F32),
                        pltpu.VMEM((hk, DV_A), F32), pltpu.VMEM((hk, DV_A), F32),
                        pltpu.VMEM((n_ch, H_A * GLA_CHUNK, hk), BF16), pltpu.VMEM((n_ch, H_A * GLA_CHUNK, hk), BF16),
                        pltpu.VMEM((n_ch, hk, DV_A), F32), pltpu.VMEM((n_ch, hk, DV_A), F32),
                        pltpu.VMEM((n_ch, hk, DV_A), F32), pltpu.VMEM((n_ch, hk, DV_A), F32)],
        compiler_params=_params(1),
        name="gla_sample" if sample else "gla_prompt",
    )(*args)


def _axial_rope(n_tokens, dim):
    rows = n_tokens // GRID_W
    row = np.repeat(np.arange(rows), GRID_W).astype(np.float64)
    col = np.tile(np.arange(GRID_W), rows).astype(np.float64)
    n_freq = dim // 4
    inv = ROPE_THETA ** (-np.arange(n_freq) / n_freq)
    ang = np.concatenate([row[:, None] * inv, col[:, None] * inv], axis=-1)
    return np.cos(ang).astype(np.float32), np.sin(ang).astype(np.float32)


def _filter_features(n_tokens):
    t = np.linspace(0.0, 1.0, n_tokens)[:, None]
    w = 2.0 * np.pi * np.arange(n_tokens)[:, None] / n_tokens
    f = np.linspace(1e-4, FILT_BANDS - 1, FILT_BANDS)[None, :]
    z = np.concatenate([t, np.cos(f * w), -np.sin(f * w)], axis=-1)
    z = np.pad(z, ((0, 0), (0, LANES - FILT_EMB)))
    return jnp.asarray(z, F32), jnp.asarray(t, F32)


_QB_ZERO = H_D * (NOPE_D + ROPE_D)
_QB_PERM = np.array([(NOPE_D + ROPE_D) * (p // MLA_QW) + p % MLA_QW if p % MLA_QW < NOPE_D + ROPE_D else _QB_ZERO
                     for p in range(H_D * MLA_QW)])


def _mla_rope_tables(cos_d, sin_d):
    n, half = cos_d.shape
    zeros = np.zeros((n, half), np.float32)

    def lanes(pre, width):
        pad = np.zeros((n, width - pre.shape[1] - 2 * half), np.float32)
        build = lambda first, second, lead: np.concatenate([lead, first, second, pad], axis=1)
        return (build(cos_d, cos_d, pre), build(-sin_d, zeros, 0 * pre), build(zeros, sin_d, 0 * pre))

    q_tabs = lanes(np.ones((n, NOPE_D), np.float32), MLA_QW)
    k_tabs = lanes(np.zeros((n, 0), np.float32), LANES)
    return tuple(jnp.asarray(t) for t in q_tabs + k_tabs)

EVEN_ROW_GROUPS = ((0, 0, 1536), (1568, 1536, 1024), (1536, EVEN_W - 2 * GATE_RANK, 2 * GATE_RANK))
ODD_ROW_GROUPS = ((0, 0, 1984),)
EVEN_KEEP = (2304, 256)
ODD_KEEP = (1920, LANES)


def kernel(x_prompt, x_sample, state_gla_fwd, state_gla_bwd, cache_gqa_k, cache_gqa_v, cache_mla_ckv, cache_mla_kpe, c, c_ctx, w_mod, b_mod, w_in_even, w_gla_gate_f, b_gla_gate_f, w_gla_gate_b, b_gla_gate_b, g_gla_norm, g_gqa_q, g_gqa_k, w_out_even, w_in_odd, w_hy_conv, b_hy_conv, hy_skip, w_filt1, b_filt1, filt_freq, w_filt2, b_filt2, w_filt3, g_mla_q, w_mla_qb, g_mla_kv, w_mla_kvb, w_out_odd, w_ffn_in, w_ffn_out, g_final):
    cvec = jnp.concatenate([c_ctx[None, :], c, jnp.zeros((8 - 1 - DEC_BATCH, D_MODEL), F32)], axis=0)
    mod = _modulation(cvec, w_mod, b_mod)
    xc, xs, s_row0 = x_prompt.reshape(N_PROMPT, D_MODEL), x_sample.reshape(N_SAMPLE, D_MODEL), 0

    gla_consts = _gla_constants()
    cos_b, sin_b = _axial_rope(DEC_SEQ, HD_B)
    rope_b = (jnp.asarray(np.concatenate([cos_b, cos_b], axis=1)), jnp.asarray(np.concatenate([-sin_b, sin_b], axis=1)))
    cos_d, sin_d = _axial_rope(DEC_SEQ, ROPE_D)
    rope_d = _mla_rope_tables(cos_d, sin_d)
    w_qb_all = jnp.pad(w_mla_qb, ((0, 0), (0, 0), (0, 1)))[:, :, _QB_PERM]
    tabs_c, tabs_s = _dft_tables(SEQ), _dft_tables(DEC_SEQ)
    z_c, t_c = _filter_features(SEQ)
    z_s, t_s = _filter_features(DEC_SEQ)
    deltas = jnp.asarray(np.abs(np.linspace(HY_MIN_DECAY, HY_MAX_DECAY, HY_W))[None, :], F32)

    wt_even = jnp.swapaxes(w_in_even, 1, 2)
    wt_odd = jnp.swapaxes(w_in_odd, 1, 2)

    filt_args = (jnp.pad(w_filt1, ((0, 0), (0, LANES - FILT_EMB), (0, 0))), b_filt1, filt_freq, w_filt2, b_filt2, w_filt3)
    g_c = _filter_spectrum(z_c, *filt_args, t_c, deltas, tabs_c)
    g_s = _filter_spectrum(z_s, *filt_args, t_s, deltas, tabs_s)
    b_conv = b_hy_conv.reshape(DEPTH // 2, 1, 3 * HY_W)
    skip = hy_skip.reshape(DEPTH // 2, 1, HY_W)

    n_even, n_odd = (DEPTH + 1) // 2, DEPTH // 2
    hk = H_A * DK_A
    z0 = LANES - 2 * GATE_RANK
    pad_f = jnp.zeros((n_even, LANES, hk), F32).at[:, z0:z0 + GATE_RANK].set(w_gla_gate_f)
    pad_b = jnp.zeros((n_even, LANES, hk), F32).at[:, z0 + GATE_RANK:].set(w_gla_gate_b)
    gate_args = (pad_f, b_gla_gate_f.reshape(n_even, 1, hk), pad_b, b_gla_gate_b.reshape(n_even, 1, hk),
                 g_gla_norm.reshape(n_even, 1, DV_A), gla_consts)
    ctx_a = (state_gla_fwd.reshape(DEC_BATCH, n_even, hk, DV_A), state_gla_bwd.reshape(DEC_BATCH, n_even, hk, DV_A))
    gq_b, gk_b = g_gqa_q.reshape(n_even, 1, HD_B), g_gqa_k.reshape(n_even, 1, HD_B)
    mla_args = (g_mla_q.reshape(n_odd, 1, Q_RANK), w_qb_all, g_mla_kv.reshape(n_odd, 1, KV_RANK), w_mla_kvb)

    new_states = new_kv = new_latent = None
    for i in range(DEPTH):
        j = i // 2
        if i % 2 == 0:
            pc, v_new, ps = _in_proj(xc, xs, s_row0, mod, i, wt_even, j, EVEN_ROW_GROUPS, EVEN_W, EVEN_KEEP)
            a_c, *new_states = _gla(pc, *gate_args, BATCH, SEQ, slot=j, prev=new_states)
            a_s = _gla(ps, *gate_args, DEC_BATCH, DEC_SEQ, ctx=ctx_a, slot=j)
            b_c, *new_kv = _gqa(pc, gq_b, gk_b, BATCH, SEQ, v_f32=v_new, slot=j, prev=new_kv)
            b_s = _gqa(ps, gq_b, gk_b, DEC_BATCH, DEC_SEQ, ctx=(cache_gqa_k, cache_gqa_v), rope=rope_b, slot=j)
            w_out = w_out_even
        else:
            pc, kpe_new, ps = _in_proj(xc, xs, s_row0, mod, i, wt_odd, j, ODD_ROW_GROUPS, ODD_W, ODD_KEEP)
            a_c = _hyena(pc, j, w_hy_conv, b_conv, skip, g_c[0], g_c[1], tabs_c, BATCH, SEQ)
            a_s = _hyena(ps, j, w_hy_conv, b_conv, skip, g_s[0], g_s[1], tabs_s, DEC_BATCH, DEC_SEQ)
            b_c, *new_latent = _mla(pc, *mla_args, BATCH, SEQ, kpe_f32=kpe_new, slot=j, prev=new_latent)
            b_s = _mla(ps, *mla_args, DEC_BATCH, DEC_SEQ, ctx=(cache_mla_ckv, cache_mla_kpe), rope=rope_d, slot=j)
            w_out = w_out_odd
        x_mid = _out_proj([a_c, b_c], [a_s, b_s], w_out, j, xc, xs, s_row0, mod, i, 2)
        if i < DEPTH - 1:
            x_all = _ffn(x_mid, 0, N_PROMPT + N_SAMPLE, mod, i, w_ffn_in, w_ffn_out)
            xc, xs, s_row0 = x_all, x_all, N_PROMPT
        else:
            xc = _ffn(x_mid, 0, N_PROMPT, mod, i, w_ffn_in, w_ffn_out, final_gain=g_final)
            xs = _ffn(x_mid, N_PROMPT, N_SAMPLE, mod, i, w_ffn_in, w_ffn_out, final_gain=g_final)
    y_prompt = xc.reshape(BATCH, SEQ, D_MODEL)
    y_sample = xs.reshape(DEC_BATCH, DEC_SEQ, D_MODEL)
    state_shape = (BATCH, (DEPTH + 1) // 2, H_A, DK_A, DV_A)
    return (y_prompt, y_sample, new_states[0].reshape(state_shape), new_states[1].reshape(state_shape),
            new_kv[0], new_kv[1], new_latent[0], new_latent[1])
```
